```python
import jax, jax.numpy as jnp
from jax import lax
import numpy as np

D_MODEL = 1024
BATCH = 8
SEQ = 2048
DEPTH = 4

ATT_HEAD_DIM = 64
ATT_HEADS_PER_GROUP = D_MODEL // 256
DILATED_GROUPS = ((128, 1), (512, 4), (2048, 16))
N_GROUPS = len(DILATED_GROUPS)
ATT_WIDTH = N_GROUPS * ATT_HEADS_PER_GROUP * ATT_HEAD_DIM
ATT_OUT_WIDTH = ATT_HEADS_PER_GROUP * ATT_HEAD_DIM
ATT_BLOCK = 128

DN_HEAD_DIM = 128
DN_HEADS = D_MODEL // DN_HEAD_DIM
DN_WIDTH = DN_HEADS * DN_HEAD_DIM
CONV_WIDTH = 4
DN_CHUNK = 64

D_FF = 2816
EPS = 1e-6
N_ADA = 9

OFF_DN_QKV = 3 * ATT_WIDTH
OFF_DN_GATE = OFF_DN_QKV + 3 * DN_WIDTH
OFF_DN_A = OFF_DN_GATE + DN_WIDTH
OFF_DN_B = OFF_DN_A + DN_HEADS
OFF_MERGE = OFF_DN_B + DN_HEADS
N_IN = OFF_MERGE + 2 * D_MODEL

kernel_name = "hybrid_dilated_attn_gated_deltanet_macaron_adaln"


def rms_norm(x, g):
    xf = x.astype(jnp.float32)
    y = xf * lax.rsqrt(jnp.mean(xf * xf, axis=-1, keepdims=True) + EPS)
    return (y * g.astype(jnp.float32)).astype(x.dtype)


def l2_norm(x):
    xf = x.astype(jnp.float32)
    return xf * lax.rsqrt(jnp.sum(xf * xf, axis=-1, keepdims=True) + EPS)


def modulate(h, shift, scale):
    return h * (1.0 + scale[:, None, :]) + shift[:, None, :]


def swiglu(h, w_up, w_down):
    gate, up = jnp.split(h @ w_up, 2, axis=-1)
    return (jax.nn.silu(gate) * up) @ w_down


def dilated_group_attention(q, k, v, window, dilation):
    B, T, H, hd = q.shape
    L = T // dilation
    w_sub = window // dilation
    Lp = -(-L // ATT_BLOCK) * ATT_BLOCK
    nb = Lp // ATT_BLOCK

    def to_blocks(t):
        t = t.reshape(B, L, dilation, H, hd).transpose(0, 2, 1, 3, 4)
        t = jnp.pad(t, ((0, 0), (0, 0), (0, Lp - L), (0, 0), (0, 0)))
        return t.reshape(B, dilation, nb, ATT_BLOCK, H, hd)

    qb, kb, vb = to_blocks(q), to_blocks(k), to_blocks(v)

    def with_prev(t):
        prev = jnp.pad(t, ((0, 0), (0, 0), (1, 0), (0, 0), (0, 0), (0, 0)))[:, :, :-1]
        return jnp.concatenate([prev, t], axis=3)

    kk, vv = with_prev(kb), with_prev(vb)
    s = jnp.einsum('bgnqhd,bgnkhd->bgnhqk', qb, kk,
                   preferred_element_type=jnp.float32) * (ATT_HEAD_DIM ** -0.5)
    qi = jnp.arange(ATT_BLOCK)[:, None]
    kj = jnp.arange(2 * ATT_BLOCK)[None, :]
    dist = ATT_BLOCK + qi - kj
    blk = jnp.arange(nb)[:, None, None]
    valid = (dist >= 0) & (dist <= w_sub) & ((blk > 0) | (kj >= ATT_BLOCK))
    s = jnp.where(valid[None, None, :, None], s, -jnp.inf)
    m = jnp.max(s, axis=-1, keepdims=True)
    p = jnp.exp(s - m)
    denom = jnp.sum(p, axis=-1, keepdims=True)
    o = jnp.einsum('bgnhqk,bgnkhd->bgnqhd', p / denom, vv.astype(jnp.float32))
    lse = (m + jnp.log(denom))[..., 0].transpose(0, 1, 2, 4, 3)
    o = o.reshape(B, dilation, Lp, H, hd)[:, :, :L].transpose(0, 2, 1, 3, 4).reshape(B, T, H, hd)
    lse = lse.reshape(B, dilation, Lp, H)[:, :, :L].transpose(0, 2, 1, 3).reshape(B, T, H)
    return o, lse


def causal_depthwise_conv(x, w):
    T = x.shape[1]
    xp = jnp.pad(x, ((0, 0), (CONV_WIDTH - 1, 0), (0, 0)))
    return sum(xp[:, i:i + T] * w[i] for i in range(CONV_WIDTH))


def chunk_gated_delta_rule(q, k, v, g, beta):
    B, T, H, dk = q.shape
    dv = v.shape[-1]
    C = DN_CHUNK
    N = T // C

    def chunks(t):
        t = t.reshape(B, N, C, H, *t.shape[3:])
        return jnp.moveaxis(t, 3, 2)

    q, k, v, g, beta = chunks(q), chunks(k), chunks(v), chunks(g), chunks(beta)
    gc = jnp.cumsum(g, axis=-1)
    tril = jnp.tril(jnp.ones((C, C), dtype=bool))
    strict = tril & ~jnp.eye(C, dtype=bool)
    diff = gc[..., :, None] - gc[..., None, :]
    ldec = jnp.where(tril, jnp.exp(jnp.where(tril, diff, 0.0)), 0.0)
    kb = k * beta[..., None]
    vb = v * beta[..., None]
    a_mat = jnp.where(strict, jnp.einsum('bnhid,bnhjd->bnhij', kb, k) * ldec, 0.0)
    eye = jnp.broadcast_to(jnp.eye(C, dtype=jnp.float32), a_mat.shape)
    t_inv = lax.linalg.triangular_solve(eye + a_mat, eye, left_side=True, lower=True)
    u = jnp.einsum('bnhij,bnhjd->bnhid', t_inv, vb)
    w = jnp.einsum('bnhij,bnhjd->bnhid', t_inv, kb * jnp.exp(gc)[..., None])
    attn_intra = jnp.where(tril, jnp.einsum('bnhid,bnhjd->bnhij', q, k) * ldec, 0.0)
    q_dec = q * jnp.exp(gc)[..., None]
    k_dec = k * jnp.exp(gc[..., -1:] - gc)[..., None]
    g_last = jnp.exp(gc[..., -1])

    def step(S, inp):
        w_c, u_c, qd_c, kd_c, a_c, gl_c = inp
        v_new = u_c - jnp.einsum('bhcd,bhde->bhce', w_c, S)
        o = jnp.einsum('bhcd,bhde->bhce', qd_c, S) + jnp.einsum('bhij,bhje->bhie', a_c, v_new)
        S = S * gl_c[..., None, None] + jnp.einsum('bhcd,bhce->bhde', kd_c, v_new)
        return S, o

    xs = tuple(jnp.moveaxis(t, 1, 0) for t in (w, u, q_dec, k_dec, attn_intra, g_last))
    S0 = jnp.zeros((B, H, dk, dv), jnp.float32)
    _, o = lax.scan(step, S0, xs)
    return o.transpose(1, 0, 3, 2, 4).reshape(B, T, H, dv)


def hybrid_mixer(h, w_in, q_norm, k_norm, conv_w, a_log, dt_bias, dn_norm,
                 w_proj_att, w_proj_dn, w_out):
    B, T, _ = h.shape
    z = h @ w_in
    qkv = z[..., :OFF_DN_QKV].reshape(B, T, 3, N_GROUPS, ATT_HEADS_PER_GROUP, ATT_HEAD_DIM)
    q = rms_norm(qkv[:, :, 0], q_norm)
    k = rms_norm(qkv[:, :, 1], k_norm)
    v = qkv[:, :, 2]
    outs, lses = [], []
    for gi, (window, dilation) in enumerate(DILATED_GROUPS):
        o, lse = dilated_group_attention(q[:, :, gi], k[:, :, gi], v[:, :, gi], window, dilation)
        outs.append(o)
        lses.append(lse)
    wts = jax.nn.softmax(jnp.stack(lses), axis=0)
    y_att = jnp.sum(wts[..., None] * jnp.stack(outs), axis=0)
    y_att = y_att.reshape(B, T, ATT_OUT_WIDTH).astype(h.dtype) @ w_proj_att
    dn_qkv = jax.nn.silu(causal_depthwise_conv(z[..., OFF_DN_QKV:OFF_DN_GATE], conv_w))
    dq, dk, dv = jnp.split(dn_qkv, 3, axis=-1)
    dq = l2_norm(dq.reshape(B, T, DN_HEADS, DN_HEAD_DIM)) * (DN_HEAD_DIM ** -0.5)
    dk = l2_norm(dk.reshape(B, T, DN_HEADS, DN_HEAD_DIM))
    dv = dv.reshape(B, T, DN_HEADS, DN_HEAD_DIM).astype(jnp.float32)
    a_in = z[..., OFF_DN_A:OFF_DN_B].astype(jnp.float32)
    b_in = z[..., OFF_DN_B:OFF_MERGE].astype(jnp.float32)
    g_log = -jnp.exp(a_log.astype(jnp.float32)) * jax.nn.softplus(a_in + dt_bias.astype(jnp.float32))
    beta = jax.nn.sigmoid(b_in)
    o_dn = chunk_gated_delta_rule(dq, dk, dv, g_log, beta)
    out_gate = z[..., OFF_DN_GATE:OFF_DN_A].reshape(B, T, DN_HEADS, DN_HEAD_DIM).astype(jnp.float32)
    o_dn = rms_norm(o_dn, dn_norm) * jax.nn.silu(out_gate)
    y_dn = o_dn.reshape(B, T, DN_WIDTH).astype(h.dtype) @ w_proj_dn
    g_att, g_dn = jnp.split(jax.nn.sigmoid(z[..., OFF_MERGE:]), 2, axis=-1)
    return (g_att * y_att + g_dn * y_dn) @ w_out


def _fwd_setup_inputs(seed: int = 0) -> dict:
    key = jax.random.key(seed)
    ks = iter(jax.random.split(key, 32))

    def nrm(shape, scale):
        return jax.random.normal(next(ks), shape, jnp.float32) * scale

    def gain(shape):
        return 1.0 + nrm(shape, 0.05)

    L, D = DEPTH, D_MODEL
    x = nrm((BATCH, SEQ, D), 1.0)
    c = nrm((BATCH, D), 1.0)
    ada_w = nrm((L, D, N_ADA * D), 0.02)
    ada_b = nrm((L, N_ADA * D), 0.1)
    norm_ff1 = gain((L, D))
    ffn1_w_up = nrm((L, D, 2 * D_FF), D ** -0.5)
    ffn1_w_down = nrm((L, D_FF, D), D_FF ** -0.5)
    norm_mix = gain((L, D))
    w_in = nrm((L, D, N_IN), D ** -0.5)
    q_norm = gain((L, ATT_HEAD_DIM))
    k_norm = gain((L, ATT_HEAD_DIM))
    conv_w = nrm((L, CONV_WIDTH, 3 * DN_WIDTH), CONV_WIDTH ** -0.5)
    a_log = jnp.log(jax.random.uniform(next(ks), (L, DN_HEADS), jnp.float32, 1.0, 16.0))
    dt = jnp.exp(jax.random.uniform(next(ks), (L, DN_HEADS), jnp.float32,
                                    float(np.log(1e-3)), float(np.log(1e-1))))
    dt_bias = jnp.log(jnp.expm1(dt))
    dn_norm = gain((L, DN_HEAD_DIM))
    w_proj_att = nrm((L, ATT_OUT_WIDTH, D), ATT_OUT_WIDTH ** -0.5)
    w_proj_dn = nrm((L, DN_WIDTH, D), DN_WIDTH ** -0.5)
    w_out = nrm((L, D, D), D ** -0.5)
    norm_ff2 = gain((L, D))
    ffn2_w_up = nrm((L, D, 2 * D_FF), D ** -0.5)
    ffn2_w_down = nrm((L, D_FF, D), D_FF ** -0.5)
    return {"x": x, "c": c, "ada_w": ada_w, "ada_b": ada_b,
            "norm_ff1": norm_ff1, "ffn1_w_up": ffn1_w_up, "ffn1_w_down": ffn1_w_down,
            "norm_mix": norm_mix, "w_in": w_in, "q_norm": q_norm, "k_norm": k_norm,
            "conv_w": conv_w, "a_log": a_log, "dt_bias": dt_bias, "dn_norm": dn_norm,
            "w_proj_att": w_proj_att, "w_proj_dn": w_proj_dn, "w_out": w_out,
            "norm_ff2": norm_ff2, "ffn2_w_up": ffn2_w_up, "ffn2_w_down": ffn2_w_down}


def _fwd_reference(x, c, ada_w, ada_b, norm_ff1, ffn1_w_up, ffn1_w_down, norm_mix, w_in,
              q_norm, k_norm, conv_w, a_log, dt_bias, dn_norm, w_proj_att, w_proj_dn,
              w_out, norm_ff2, ffn2_w_up, ffn2_w_down):
    c_act = jax.nn.silu(c)
    for l in range(DEPTH):
        mod = c_act @ ada_w[l] + ada_b[l]
        (sh1, sc1, gt1, sh2, sc2, gt2, sh3, sc3, gt3) = jnp.split(mod, N_ADA, axis=-1)
        h = modulate(rms_norm(x, norm_ff1[l]), sh1, sc1)
        x = x + 0.5 * gt1[:, None, :] * swiglu(h, ffn1_w_up[l], ffn1_w_down[l])
        h = modulate(rms_norm(x, norm_mix[l]), sh2, sc2)
        x = x + gt2[:, None, :] * hybrid_mixer(h, w_in[l], q_norm[l], k_norm[l], conv_w[l],
                                                a_log[l], dt_bias[l], dn_norm[l],
                                                w_proj_att[l], w_proj_dn[l], w_out[l])
        h = modulate(rms_norm(x, norm_ff2[l]), sh3, sc3)
        x = x + 0.5 * gt3[:, None, :] * swiglu(h, ffn2_w_up[l], ffn2_w_down[l])
    return x


import jax as _jax
import jax.numpy as _jnp

TWIN_FORMAT = 'train_step'
FWD_PARAMS = ['x', 'c', 'ada_w', 'ada_b', 'norm_ff1', 'ffn1_w_up', 'ffn1_w_down', 'norm_mix', 'w_in', 'q_norm', 'k_norm', 'conv_w', 'a_log', 'dt_bias', 'dn_norm', 'w_proj_att', 'w_proj_dn', 'w_out', 'norm_ff2', 'ffn2_w_up', 'ffn2_w_down']
TWIN_WEIGHTS = ['ada_w', 'ada_b', 'norm_ff1', 'ffn1_w_up', 'ffn1_w_down', 'norm_mix', 'w_in', 'q_norm', 'k_norm', 'conv_w', 'a_log', 'dt_bias', 'dn_norm', 'w_proj_att', 'w_proj_dn', 'w_out', 'norm_ff2', 'ffn2_w_up', 'ffn2_w_down']
TWIN_DIFF_INPUT = 'x'
TWIN_INPUTS = ['x', 'c', 'ada_w', 'ada_b', 'norm_ff1', 'ffn1_w_up', 'ffn1_w_down', 'norm_mix', 'w_in', 'q_norm', 'k_norm', 'conv_w', 'a_log', 'dt_bias', 'dn_norm', 'w_proj_att', 'w_proj_dn', 'w_out', 'norm_ff2', 'ffn2_w_up', 'ffn2_w_down', 'loss_target', 'm_ada_w', 'm_ada_b', 'm_norm_ff1', 'm_ffn1_w_up', 'm_ffn1_w_down', 'm_norm_mix', 'm_w_in', 'm_q_norm', 'm_k_norm', 'm_conv_w', 'm_a_log', 'm_dt_bias', 'm_dn_norm', 'm_w_proj_att', 'm_w_proj_dn', 'm_w_out', 'm_norm_ff2', 'm_ffn2_w_up', 'm_ffn2_w_down', 'v_ada_w', 'v_ada_b', 'v_norm_ff1', 'v_ffn1_w_up', 'v_ffn1_w_down', 'v_norm_mix', 'v_w_in', 'v_q_norm', 'v_k_norm', 'v_conv_w', 'v_a_log', 'v_dt_bias', 'v_dn_norm', 'v_w_proj_att', 'v_w_proj_dn', 'v_w_out', 'v_norm_ff2', 'v_ffn2_w_up', 'v_ffn2_w_down']
TWIN_OUTPUTS = ['loss', 'grad_x', 'grad_ada_w', 'grad_ada_b', 'grad_norm_ff1', 'grad_ffn1_w_up', 'grad_ffn1_w_down', 'grad_norm_mix', 'grad_w_in', 'grad_q_norm', 'grad_k_norm', 'grad_conv_w', 'grad_a_log', 'grad_dt_bias', 'grad_dn_norm', 'grad_w_proj_att', 'grad_w_proj_dn', 'grad_w_out', 'grad_norm_ff2', 'grad_ffn2_w_up', 'grad_ffn2_w_down', 'delta_ada_w', 'delta_ada_b', 'delta_norm_ff1', 'delta_ffn1_w_up', 'delta_ffn1_w_down', 'delta_norm_mix', 'delta_w_in', 'delta_q_norm', 'delta_k_norm', 'delta_conv_w', 'delta_a_log', 'delta_dt_bias', 'delta_dn_norm', 'delta_w_proj_att', 'delta_w_proj_dn', 'delta_w_out', 'delta_norm_ff2', 'delta_ffn2_w_up', 'delta_ffn2_w_down', 'new_m_ada_w', 'new_m_ada_b', 'new_m_norm_ff1', 'new_m_ffn1_w_up', 'new_m_ffn1_w_down', 'new_m_norm_mix', 'new_m_w_in', 'new_m_q_norm', 'new_m_k_norm', 'new_m_conv_w', 'new_m_a_log', 'new_m_dt_bias', 'new_m_dn_norm', 'new_m_w_proj_att', 'new_m_w_proj_dn', 'new_m_w_out', 'new_m_norm_ff2', 'new_m_ffn2_w_up', 'new_m_ffn2_w_down', 'new_v_ada_w', 'new_v_ada_b', 'new_v_norm_ff1', 'new_v_ffn1_w_up', 'new_v_ffn1_w_down', 'new_v_norm_mix', 'new_v_w_in', 'new_v_q_norm', 'new_v_k_norm', 'new_v_conv_w', 'new_v_a_log', 'new_v_dt_bias', 'new_v_dn_norm', 'new_v_w_proj_att', 'new_v_w_proj_dn', 'new_v_w_out', 'new_v_norm_ff2', 'new_v_ffn2_w_up', 'new_v_ffn2_w_down']
TWIN_LEAF_KINDS = {'loss': 'loss', 'grad_x': 'grad_x', 'grad_ada_w': 'grad_w', 'grad_ada_b': 'grad_w', 'grad_norm_ff1': 'grad_w', 'grad_ffn1_w_up': 'grad_w', 'grad_ffn1_w_down': 'grad_w', 'grad_norm_mix': 'grad_w', 'grad_w_in': 'grad_w', 'grad_q_norm': 'grad_w', 'grad_k_norm': 'grad_w', 'grad_conv_w': 'grad_w', 'grad_a_log': 'grad_w', 'grad_dt_bias': 'grad_w', 'grad_dn_norm': 'grad_w', 'grad_w_proj_att': 'grad_w', 'grad_w_proj_dn': 'grad_w', 'grad_w_out': 'grad_w', 'grad_norm_ff2': 'grad_w', 'grad_ffn2_w_up': 'grad_w', 'grad_ffn2_w_down': 'grad_w', 'delta_ada_w': 'delta_w', 'delta_ada_b': 'delta_w', 'delta_norm_ff1': 'delta_w', 'delta_ffn1_w_up': 'delta_w', 'delta_ffn1_w_down': 'delta_w', 'delta_norm_mix': 'delta_w', 'delta_w_in': 'delta_w', 'delta_q_norm': 'delta_w', 'delta_k_norm': 'delta_w', 'delta_conv_w': 'delta_w', 'delta_a_log': 'delta_w', 'delta_dt_bias': 'delta_w', 'delta_dn_norm': 'delta_w', 'delta_w_proj_att': 'delta_w', 'delta_w_proj_dn': 'delta_w', 'delta_w_out': 'delta_w', 'delta_norm_ff2': 'delta_w', 'delta_ffn2_w_up': 'delta_w', 'delta_ffn2_w_down': 'delta_w', 'new_m_ada_w': 'new_m', 'new_m_ada_b': 'new_m', 'new_m_norm_ff1': 'new_m', 'new_m_ffn1_w_up': 'new_m', 'new_m_ffn1_w_down': 'new_m', 'new_m_norm_mix': 'new_m', 'new_m_w_in': 'new_m', 'new_m_q_norm': 'new_m', 'new_m_k_norm': 'new_m', 'new_m_conv_w': 'new_m', 'new_m_a_log': 'new_m', 'new_m_dt_bias': 'new_m', 'new_m_dn_norm': 'new_m', 'new_m_w_proj_att': 'new_m', 'new_m_w_proj_dn': 'new_m', 'new_m_w_out': 'new_m', 'new_m_norm_ff2': 'new_m', 'new_m_ffn2_w_up': 'new_m', 'new_m_ffn2_w_down': 'new_m', 'new_v_ada_w': 'new_v', 'new_v_ada_b': 'new_v', 'new_v_norm_ff1': 'new_v', 'new_v_ffn1_w_up': 'new_v', 'new_v_ffn1_w_down': 'new_v', 'new_v_norm_mix': 'new_v', 'new_v_w_in': 'new_v', 'new_v_q_norm': 'new_v', 'new_v_k_norm': 'new_v', 'new_v_conv_w': 'new_v', 'new_v_a_log': 'new_v', 'new_v_dt_bias': 'new_v', 'new_v_dn_norm': 'new_v', 'new_v_w_proj_att': 'new_v', 'new_v_w_proj_dn': 'new_v', 'new_v_w_out': 'new_v', 'new_v_norm_ff2': 'new_v', 'new_v_ffn2_w_up': 'new_v', 'new_v_ffn2_w_down': 'new_v'}


def _forward(args):
    return _fwd_reference(*[args[k] for k in FWD_PARAMS])


def _output_shape():
    out = _jax.eval_shape(lambda: _forward(_fwd_setup_inputs(0)))
    return out.shape, out.dtype

N_MICROBATCH = 1
ADAM_LR = 0.001
ADAM_B1 = 0.9
ADAM_B2 = 0.999
ADAM_EPS = 1e-08
ADAM_WD = 0.01
ADAM_STEP = 10
PER_EXAMPLE_BATCH_AXIS = {'x': 0, 'c': 0, 'loss_target': 0}
SHARED_INPUTS = []
_WEIGHT_DTYPES = {'ada_w': _jnp.float32, 'ada_b': _jnp.float32, 'norm_ff1': _jnp.float32, 'ffn1_w_up': _jnp.float32, 'ffn1_w_down': _jnp.float32, 'norm_mix': _jnp.float32, 'w_in': _jnp.float32, 'q_norm': _jnp.float32, 'k_norm': _jnp.float32, 'conv_w': _jnp.float32, 'a_log': _jnp.float32, 'dt_bias': _jnp.float32, 'dn_norm': _jnp.float32, 'w_proj_att': _jnp.float32, 'w_proj_dn': _jnp.float32, 'w_out': _jnp.float32, 'norm_ff2': _jnp.float32, 'ffn2_w_up': _jnp.float32, 'ffn2_w_down': _jnp.float32}
MOMENT_SCALE = {'ada_w': 2.678331e-01, 'ada_b': 5.858513e-01, 'norm_ff1': 8.994865e-01, 'ffn1_w_up': 3.602380e-02, 'ffn1_w_down': 5.525821e-02, 'norm_mix': 4.772918e-01, 'w_in': 6.819926e-02, 'q_norm': 1.329043e-01, 'k_norm': 1.322219e-01, 'conv_w': 8.172096e-02, 'a_log': 1.119135e+00, 'dt_bias': 1.058646e+00, 'dn_norm': 3.404486e+00, 'w_proj_att': 8.153985e-02, 'w_proj_dn': 8.566808e-02, 'w_out': 1.107028e-01, 'norm_ff2': 9.177832e-01, 'ffn2_w_up': 3.658175e-02, 'ffn2_w_down': 5.540277e-02}


def _to_microbatches(a, axis):
    t = _jnp.moveaxis(a, axis, 0)
    t = t.reshape((N_MICROBATCH, t.shape[0] // N_MICROBATCH) + t.shape[1:])
    return _jnp.moveaxis(t, 1, axis + 1)


def setup_inputs(seed: int = 0) -> dict:
    inp = _fwd_setup_inputs(seed)
    key = _jax.random.fold_in(_jax.random.key(seed), 7919)
    shape, _ = _output_shape()
    out = dict(inp)
    out["loss_target"] = _jax.random.normal(_jax.random.fold_in(key, 0), shape, _jnp.float32)
    for i, name in enumerate(TWIN_WEIGHTS):
        w = inp[name].astype(_jnp.float32)
        if MOMENT_SCALE is None:
            s = _jnp.sqrt(_jnp.mean(_jnp.square(w)) + 1e-30)
        else:
            s = MOMENT_SCALE[name]
        km, kv = _jax.random.split(_jax.random.fold_in(key, i + 1))
        out[name] = w
        out["m_" + name] = s * _jax.random.normal(km, w.shape, _jnp.float32)
        out["v_" + name] = (s * s) * _jax.random.uniform(kv, w.shape, _jnp.float32, 0.5, 1.5)
    if N_MICROBATCH > 1:
        for name, axis in PER_EXAMPLE_BATCH_AXIS.items():
            out[name] = _to_microbatches(out[name], axis)
    return {'x': out['x'], 'c': out['c'], 'ada_w': out['ada_w'], 'ada_b': out['ada_b'], 'norm_ff1': out['norm_ff1'], 'ffn1_w_up': out['ffn1_w_up'], 'ffn1_w_down': out['ffn1_w_down'], 'norm_mix': out['norm_mix'], 'w_in': out['w_in'], 'q_norm': out['q_norm'], 'k_norm': out['k_norm'], 'conv_w': out['conv_w'], 'a_log': out['a_log'], 'dt_bias': out['dt_bias'], 'dn_norm': out['dn_norm'], 'w_proj_att': out['w_proj_att'], 'w_proj_dn': out['w_proj_dn'], 'w_out': out['w_out'], 'norm_ff2': out['norm_ff2'], 'ffn2_w_up': out['ffn2_w_up'], 'ffn2_w_down': out['ffn2_w_down'], 'loss_target': out['loss_target'], 'm_ada_w': out['m_ada_w'], 'm_ada_b': out['m_ada_b'], 'm_norm_ff1': out['m_norm_ff1'], 'm_ffn1_w_up': out['m_ffn1_w_up'], 'm_ffn1_w_down': out['m_ffn1_w_down'], 'm_norm_mix': out['m_norm_mix'], 'm_w_in': out['m_w_in'], 'm_q_norm': out['m_q_norm'], 'm_k_norm': out['m_k_norm'], 'm_conv_w': out['m_conv_w'], 'm_a_log': out['m_a_log'], 'm_dt_bias': out['m_dt_bias'], 'm_dn_norm': out['m_dn_norm'], 'm_w_proj_att': out['m_w_proj_att'], 'm_w_proj_dn': out['m_w_proj_dn'], 'm_w_out': out['m_w_out'], 'm_norm_ff2': out['m_norm_ff2'], 'm_ffn2_w_up': out['m_ffn2_w_up'], 'm_ffn2_w_down': out['m_ffn2_w_down'], 'v_ada_w': out['v_ada_w'], 'v_ada_b': out['v_ada_b'], 'v_norm_ff1': out['v_norm_ff1'], 'v_ffn1_w_up': out['v_ffn1_w_up'], 'v_ffn1_w_down': out['v_ffn1_w_down'], 'v_norm_mix': out['v_norm_mix'], 'v_w_in': out['v_w_in'], 'v_q_norm': out['v_q_norm'], 'v_k_norm': out['v_k_norm'], 'v_conv_w': out['v_conv_w'], 'v_a_log': out['v_a_log'], 'v_dt_bias': out['v_dt_bias'], 'v_dn_norm': out['v_dn_norm'], 'v_w_proj_att': out['v_w_proj_att'], 'v_w_proj_dn': out['v_w_proj_dn'], 'v_w_out': out['v_w_out'], 'v_norm_ff2': out['v_norm_ff2'], 'v_ffn2_w_up': out['v_ffn2_w_up'], 'v_ffn2_w_down': out['v_ffn2_w_down']}


def _loss(weights, diff, rest, loss_target):
    with _jax.named_scope("forward"):
        args = {**rest, TWIN_DIFF_INPUT: diff, **{k: w.astype(_WEIGHT_DTYPES[k]) for k, w in weights.items()}}
        y = _forward(args)
    with _jax.named_scope("loss_head"):
        err = _jnp.square(y.astype(_jnp.float32) - loss_target)
        return 0.5 * _jnp.sum(_jnp.mean(err, axis=-1)) if err.ndim else 0.5 * err


def _adamw(w, g, m, v):
    m = ADAM_B1 * m + (1.0 - ADAM_B1) * g
    v = ADAM_B2 * v + (1.0 - ADAM_B2) * _jnp.square(g)
    m_hat = m / (1.0 - ADAM_B1 ** ADAM_STEP)
    v_hat = v / (1.0 - ADAM_B2 ** ADAM_STEP)
    delta = -ADAM_LR * (m_hat / (_jnp.sqrt(v_hat) + ADAM_EPS) + ADAM_WD * w)
    return delta, m, v


def reference(x, c, ada_w, ada_b, norm_ff1, ffn1_w_up, ffn1_w_down, norm_mix, w_in, q_norm, k_norm, conv_w, a_log, dt_bias, dn_norm, w_proj_att, w_proj_dn, w_out, norm_ff2, ffn2_w_up, ffn2_w_down, loss_target, m_ada_w, m_ada_b, m_norm_ff1, m_ffn1_w_up, m_ffn1_w_down, m_norm_mix, m_w_in, m_q_norm, m_k_norm, m_conv_w, m_a_log, m_dt_bias, m_dn_norm, m_w_proj_att, m_w_proj_dn, m_w_out, m_norm_ff2, m_ffn2_w_up, m_ffn2_w_down, v_ada_w, v_ada_b, v_norm_ff1, v_ffn1_w_up, v_ffn1_w_down, v_norm_mix, v_w_in, v_q_norm, v_k_norm, v_conv_w, v_a_log, v_dt_bias, v_dn_norm, v_w_proj_att, v_w_proj_dn, v_w_out, v_norm_ff2, v_ffn2_w_up, v_ffn2_w_down):
    given = dict(x=x, c=c, ada_w=ada_w, ada_b=ada_b, norm_ff1=norm_ff1, ffn1_w_up=ffn1_w_up, ffn1_w_down=ffn1_w_down, norm_mix=norm_mix, w_in=w_in, q_norm=q_norm, k_norm=k_norm, conv_w=conv_w, a_log=a_log, dt_bias=dt_bias, dn_norm=dn_norm, w_proj_att=w_proj_att, w_proj_dn=w_proj_dn, w_out=w_out, norm_ff2=norm_ff2, ffn2_w_up=ffn2_w_up, ffn2_w_down=ffn2_w_down, loss_target=loss_target, m_ada_w=m_ada_w, m_ada_b=m_ada_b, m_norm_ff1=m_norm_ff1, m_ffn1_w_up=m_ffn1_w_up, m_ffn1_w_down=m_ffn1_w_down, m_norm_mix=m_norm_mix, m_w_in=m_w_in, m_q_norm=m_q_norm, m_k_norm=m_k_norm, m_conv_w=m_conv_w, m_a_log=m_a_log, m_dt_bias=m_dt_bias, m_dn_norm=m_dn_norm, m_w_proj_att=m_w_proj_att, m_w_proj_dn=m_w_proj_dn, m_w_out=m_w_out, m_norm_ff2=m_norm_ff2, m_ffn2_w_up=m_ffn2_w_up, m_ffn2_w_down=m_ffn2_w_down, v_ada_w=v_ada_w, v_ada_b=v_ada_b, v_norm_ff1=v_norm_ff1, v_ffn1_w_up=v_ffn1_w_up, v_ffn1_w_down=v_ffn1_w_down, v_norm_mix=v_norm_mix, v_w_in=v_w_in, v_q_norm=v_q_norm, v_k_norm=v_k_norm, v_conv_w=v_conv_w, v_a_log=v_a_log, v_dt_bias=v_dt_bias, v_dn_norm=v_dn_norm, v_w_proj_att=v_w_proj_att, v_w_proj_dn=v_w_proj_dn, v_w_out=v_w_out, v_norm_ff2=v_norm_ff2, v_ffn2_w_up=v_ffn2_w_up, v_ffn2_w_down=v_ffn2_w_down)
    weights = {n: given[n] for n in TWIN_WEIGHTS}
    shared = {n: given[n] for n in SHARED_INPUTS}
    per_example = {n: given[n] for n in ['x', 'c']}
    grad_fn = _jax.value_and_grad(_loss, argnums=(0, 1))

    def one_microbatch(ex, loss_target):
        ex = dict(ex)
        diff = ex.pop(TWIN_DIFF_INPUT)
        return grad_fn(weights, diff, {**shared, **ex}, loss_target)

    if N_MICROBATCH == 1:
        loss, (grad_w, grad_x) = one_microbatch(per_example, given["loss_target"])
    else:
        def body(carry, xs):
            loss_sum, grad_sum = carry
            l_k, (gw_k, gx_k) = one_microbatch(xs[0], xs[1])
            with _jax.named_scope("update"):
                return (loss_sum + l_k, _jax.tree.map(_jnp.add, grad_sum, gw_k)), gx_k

        init = (_jnp.zeros((), _jnp.float32), _jax.tree.map(_jnp.zeros_like, weights))
        (loss, grad_w), grad_x = _jax.lax.scan(body, init, (per_example, given["loss_target"]))
    with _jax.named_scope("update"):
        delta_w, new_m, new_v = {}, {}, {}
        for n in TWIN_WEIGHTS:
            delta_w[n], new_m[n], new_v[n] = _adamw(weights[n], grad_w[n], given["m_" + n], given["v_" + n])
    return (loss, grad_x, *[grad_w[n] for n in TWIN_WEIGHTS], *[delta_w[n] for n in TWIN_WEIGHTS],
            *[new_m[n] for n in TWIN_WEIGHTS], *[new_v[n] for n in TWIN_WEIGHTS])
```

```python
import functools

import jax
import jax.numpy as jnp
from jax import lax
from jax.experimental import pallas as pl
from jax.experimental.pallas import tpu as pltpu

F32 = jnp.float32
CD = jnp.bfloat16
EPS = 1e-6
N_DEV = 8
LANE = 128
ATT_HEAD_DIM = 64
ATT_BLOCK = 128
DILATIONS = (1, 4, 16)
DN_HEAD_DIM = 128
DN_CHUNK = 64
CONV_WIDTH = 4
N_ADA = 9
ADAM_LR, ADAM_B1, ADAM_B2, ADAM_EPS, ADAM_WD, ADAM_STEP = 0.001, 0.9, 0.999, 1e-08, 0.01, 10
VMEM_LIMIT = 56 * 1024 * 1024
NEG = -1e30
MESH = pl.DeviceIdType.MESH
HI = lax.Precision.HIGHEST


def _params(sem=None):
    return pltpu.CompilerParams(dimension_semantics=sem, vmem_limit_bytes=VMEM_LIMIT)


def _tile(n, pref, unit):
    best = None
    t = unit
    while t <= min(n, pref):
        if n % t == 0:
            best = t
        t += unit
    return best if best is not None else n


def _silu(x):
    return x * jax.nn.sigmoid(x)


def _dot(a, b, dims, precision=None):
    if precision is None:
        a, b = a.astype(CD), b.astype(CD)
    return lax.dot_general(a, b, (dims, ((), ())), precision=precision, preferred_element_type=F32)


def _nn(a, b, precision=None):
    return _dot(a, b, ((1,), (0,)), precision)


def _nt(a, b, precision=None):
    return _dot(a, b, ((1,), (1,)), precision)


def _tn(a, b, precision=None):
    return _dot(a, b, ((0,), (0,)), precision)


def all_gather(arrs, name):
    n = len(arrs)

    def body(*refs):
        x_refs, out_refs = refs[:n], refs[n:2 * n]
        send_sems, recv_sems, local_sems = refs[2 * n:]
        x, y, c = lax.axis_index("x"), lax.axis_index("y"), lax.axis_index("c")
        me, sibling = (x, y, c), (x, y, 1 - c)
        chips = [(1 - x, y), (x, 1 - y), (1 - x, 1 - y)]

        def copy(a, k, block, to, src=None):
            slot = out_refs[a].at[4 * block[0] + 2 * block[1] + block[2]]
            return pltpu.make_async_remote_copy(
                src_ref=slot if src is None else src, dst_ref=slot,
                send_sem=send_sems.at[7 * a + k], recv_sem=recv_sems.at[7 * a + k],
                device_id=to, device_id_type=MESH)

        mine, first, passed = [], [], []
        for a in range(n):
            cp = pltpu.make_async_copy(x_refs[a], out_refs[a].at[4 * x + 2 * y + c], local_sems.at[a])
            cp.start()
            mine.append(cp)
            first.append(copy(a, 0, me, sibling, src=x_refs[a]))
            first += [copy(a, 1 + j, me, (*chip, c), src=x_refs[a]) for j, chip in enumerate(chips)]
        for cp in first:
            cp.start()
        for j, chip in enumerate(chips):
            for a in range(n):
                copy(a, 1 + j, (*chip, c), me).wait_recv()
                cp = copy(a, 4 + j, (*chip, c), sibling)
                cp.start()
                passed.append(cp)
        for a in range(n):
            copy(a, 0, sibling, me).wait_recv()
            for j, chip in enumerate(chips):
                copy(a, 4 + j, (*chip, 1 - c), me).wait_recv()
        for cp in first + passed:
            cp.wait_send()
        for cp in mine:
            cp.wait()

    hbm = pl.BlockSpec(memory_space=pl.ANY)
    outs = pl.pallas_call(
        body, name=name,
        out_shape=[jax.ShapeDtypeStruct((N_DEV,) + a.shape, a.dtype) for a in arrs],
        in_specs=[hbm] * n, out_specs=[hbm] * n,
        scratch_shapes=[pltpu.SemaphoreType.DMA((7 * n,)), pltpu.SemaphoreType.DMA((7 * n,)),
                        pltpu.SemaphoreType.DMA((n,))],
    )(*arrs)
    return list(outs)


def all_to_all(arrs, name):
    n = len(arrs)

    def body(*refs):
        x_refs, out_refs = refs[:n], refs[n:2 * n]
        send_sems, recv_sems, local_sems = refs[2 * n:]
        x, y, c = lax.axis_index("x"), lax.axis_index("y"), lax.axis_index("c")
        me = 4 * x + 2 * y + c
        mine, sends = [], []
        for a in range(n):
            cp = pltpu.make_async_copy(x_refs[a].at[me], out_refs[a].at[me], local_sems.at[a])
            cp.start()
            mine.append(cp)
        peers = []
        for k in range(1, N_DEV):
            px, py, pc = x ^ (k >> 2), y ^ ((k >> 1) & 1), c ^ (k & 1)
            peers.append((px, py, pc, 4 * px + 2 * py + pc))
        for k in range(1, N_DEV):
            px, py, pc, pidx = peers[k - 1]
            for a in range(n):
                cp = pltpu.make_async_remote_copy(
                    src_ref=x_refs[a].at[pidx], dst_ref=out_refs[a].at[me],
                    send_sem=send_sems.at[7 * a + k - 1], recv_sem=recv_sems.at[7 * a + k - 1],
                    device_id=(px, py, pc), device_id_type=MESH)
                cp.start()
                sends.append(cp)
        for k in range(1, N_DEV):
            pidx = peers[k - 1][3]
            for a in range(n):
                pltpu.make_async_remote_copy(
                    src_ref=x_refs[a].at[pidx], dst_ref=out_refs[a].at[pidx],
                    send_sem=send_sems.at[7 * a + k - 1], recv_sem=recv_sems.at[7 * a + k - 1],
                    device_id=(x, y, c), device_id_type=MESH).wait_recv()
        for cp in sends:
            cp.wait_send()
        for cp in mine:
            cp.wait()

    hbm = pl.BlockSpec(memory_space=pl.ANY)
    outs = pl.pallas_call(
        body, name=name,
        out_shape=[jax.ShapeDtypeStruct(a.shape, a.dtype) for a in arrs],
        in_specs=[hbm] * n, out_specs=[hbm] * n,
        scratch_shapes=[pltpu.SemaphoreType.DMA((7 * n,)), pltpu.SemaphoreType.DMA((7 * n,)),
                        pltpu.SemaphoreType.DMA((n,))],
    )(*arrs)
    return list(outs)


def matmul(pairs, mode, out_dtype, name, *, a_scale=None, out_scale=None, resid=None,
           save_acc=False, tm=512, tn=512, tk=2048):
    a0, b0 = pairs[0]
    if mode == "nn":
        (m, kdim), n = a0.shape, b0.shape[1]
    elif mode == "nt":
        (m, kdim), n = a0.shape, b0.shape[0]
    else:
        (kdim, m), n = a0.shape, b0.shape[1]
    tm, tn = _tile(m, tm, 8 if m % LANE else LANE), _tile(n, tn, LANE)
    tk = _tile(kdim, tk, LANE)
    nk = kdim // tk
    npairs = len(pairs)
    dims = {"nn": ((1,), (0,)), "nt": ((1,), (1,)), "tn": ((0,), (0,))}[mode]

    if mode == "nn":
        a_spec = pl.BlockSpec((tm, tk), lambda i, j, k: (i, k))
        b_spec = pl.BlockSpec((tk, tn), lambda i, j, k: (k, j))
    elif mode == "nt":
        a_spec = pl.BlockSpec((tm, tk), lambda i, j, k: (i, k))
        b_spec = pl.BlockSpec((tn, tk), lambda i, j, k: (j, k))
    else:
        a_spec = pl.BlockSpec((tk, tm), lambda i, j, k: (k, i))
        b_spec = pl.BlockSpec((tk, tn), lambda i, j, k: (k, j))

    def body(*refs):
        it = iter(refs)
        pair_refs = [(next(it), next(it)) for _ in range(npairs)]
        as_ref = next(it) if a_scale is not None else None
        os_ref = next(it) if out_scale is not None else None
        rs_ref = next(it) if resid is not None else None
        o_ref = next(it)
        acc_out = next(it) if save_acc else None
        acc_ref = next(it) if nk > 1 else None
        part = None
        for a_ref, b_ref in pair_refs:
            a = a_ref[...]
            if as_ref is not None:
                a = a.astype(F32) * (as_ref[...] if mode != "tn" else as_ref[...].reshape(tk, 1))
            d = _dot(a, b_ref[...], dims)
            part = d if part is None else part + d

        def finish(acc):
            if acc_out is not None:
                acc_out[...] = acc
            if os_ref is not None:
                acc = acc * os_ref[...]
            if rs_ref is not None:
                acc = rs_ref[...] + acc
            o_ref[...] = acc.astype(o_ref.dtype)

        if nk == 1:
            finish(part)
        else:
            k = pl.program_id(2)

            @pl.when(k == 0)
            def _():
                acc_ref[...] = part

            @pl.when(k > 0)
            def _():
                acc_ref[...] += part

            @pl.when(k == nk - 1)
            def _():
                finish(acc_ref[...])

    in_specs, args = [], []
    for a, b in pairs:
        in_specs += [a_spec, b_spec]
        args += [a, b]
    if a_scale is not None:
        assert mode != "tn"
        in_specs.append(pl.BlockSpec((1, tk), lambda i, j, k: (0, k)))
        args.append(a_scale)
    if out_scale is not None:
        in_specs.append(pl.BlockSpec((1, tn), lambda i, j, k: (0, j)))
        args.append(out_scale)
    if resid is not None:
        in_specs.append(pl.BlockSpec((tm, tn), lambda i, j, k: (i, j)))
        args.append(resid)
    o_spec = pl.BlockSpec((tm, tn), lambda i, j, k: (i, j))
    out_shape = [jax.ShapeDtypeStruct((m, n), out_dtype)]
    out_specs = [o_spec]
    if save_acc:
        out_shape.append(jax.ShapeDtypeStruct((m, n), F32))
        out_specs.append(o_spec)
    scratch = [pltpu.VMEM((tm, tn), F32)] if nk > 1 else []
    res = pl.pallas_call(
        body, name=name, grid=(m // tm, n // tn, nk),
        in_specs=in_specs, out_specs=out_specs, out_shape=out_shape, scratch_shapes=scratch,
        compiler_params=_params(("parallel", "parallel", "arbitrary")),
    )(*args)
    return res if save_acc else res[0]


def _nm_fn(x, g, sc, sh):
    r = lax.rsqrt(jnp.mean(x * x, axis=-1, keepdims=True) + EPS)
    return (x * r * g) * (1.0 + sc) + sh


def norm_mod(x, g, sc, sh, name):
    t, d = x.shape
    tr = _tile(t, 256, 8)

    def body(x_ref, g_ref, sc_ref, sh_ref, h_ref):
        h_ref[...] = _nm_fn(x_ref[...], g_ref[...], sc_ref[...], sh_ref[...]).astype(h_ref.dtype)

    row = pl.BlockSpec((tr, d), lambda i: (i, 0))
    vec = pl.BlockSpec((1, d), lambda i: (0, 0))
    return pl.pallas_call(
        body, name=name, grid=(t // tr,), in_specs=[row, vec, vec, vec], out_specs=row,
        out_shape=jax.ShapeDtypeStruct((t, d), CD), compiler_params=_params(("parallel",)),
    )(x, g, sc, sh)


def norm_mod_bwd(x, g, sc, sh, dh, dxo, f, gate_scale, name):
    t, d = x.shape
    tr = _tile(t, 256, 8)

    def body(x_ref, g_ref, sc_ref, sh_ref, dh_ref, dxo_ref, f_ref, dx_ref, dg_ref, dsc_ref, dsh_ref, dgt_ref):
        _, vjp = jax.vjp(_nm_fn, x_ref[...], g_ref[...], sc_ref[...], sh_ref[...])
        dx, dg, dsc, dsh = vjp(dh_ref[...])
        dxo_v = dxo_ref[...]
        dx_ref[...] = dxo_v + dx
        dgt = gate_scale * jnp.sum(f_ref[...] * dxo_v, axis=0, keepdims=True)

        @pl.when(pl.program_id(0) == 0)
        def _():
            dg_ref[...] = dg
            dsc_ref[...] = dsc
            dsh_ref[...] = dsh
            dgt_ref[...] = dgt

        @pl.when(pl.program_id(0) > 0)
        def _():
            dg_ref[...] += dg
            dsc_ref[...] += dsc
            dsh_ref[...] += dsh
            dgt_ref[...] += dgt

    row = pl.BlockSpec((tr, d), lambda i: (i, 0))
    vec = pl.BlockSpec((1, d), lambda i: (0, 0))
    vshape = jax.ShapeDtypeStruct((1, d), F32)
    return pl.pallas_call(
        body, name=name, grid=(t // tr,), in_specs=[row, vec, vec, vec, row, row, row],
        out_specs=[row, vec, vec, vec, vec],
        out_shape=[jax.ShapeDtypeStruct((t, d), F32), vshape, vshape, vshape, vshape],
        compiler_params=_params(("arbitrary",)),
    )(x, g, sc, sh, dh, dxo, f)


HALF = N_DEV // 2


def ffn_up(h, u_all, l, name):
    t, d = h.shape
    cp = u_all.shape[3]
    f = HALF * cp
    tm, tn = _tile(t, 512, LANE), _tile(cp, 256, LANE)
    per = cp // tn

    def body(h_ref, wg_ref, wu_ref, g_ref, u_ref, a_ref):
        hv = h_ref[...]
        gate = _nn(hv, wg_ref[0, 0])
        up = _nn(hv, wu_ref[0, 0])
        g_ref[...] = gate
        u_ref[...] = up
        a_ref[...] = (_silu(gate) * up).astype(a_ref.dtype)

    o = pl.BlockSpec((tm, tn), lambda i, j: (i, j))
    wg = pl.BlockSpec((1, 1, d, tn), lambda i, j: (j // per, l, 0, j % per))
    wu = pl.BlockSpec((1, 1, d, tn), lambda i, j: (HALF + j // per, l, 0, j % per))
    return pl.pallas_call(
        body, name=name, grid=(t // tm, f // tn),
        in_specs=[pl.BlockSpec((tm, d), lambda i, j: (i, 0)), wg, wu], out_specs=[o, o, o],
        out_shape=[jax.ShapeDtypeStruct((t, f), F32), jax.ShapeDtypeStruct((t, f), F32),
                   jax.ShapeDtypeStruct((t, f), CD)],
        compiler_params=_params(("parallel", "parallel")),
    )(h, u_all, u_all)


def ffn_up_wg(h, dgate, dup, cp, name):
    t, d = h.shape
    tm = _tile(d, 512, LANE)

    def body(h_ref, dg_ref, du_ref, o_ref):
        s = pl.program_id(0)

        @pl.when(s < HALF)
        def _():
            o_ref[0] = _tn(h_ref[...], dg_ref[...]).astype(o_ref.dtype)

        @pl.when(s >= HALF)
        def _():
            o_ref[0] = _tn(h_ref[...], du_ref[...]).astype(o_ref.dtype)

    return pl.pallas_call(
        body, name=name, grid=(N_DEV, d // tm),
        in_specs=[pl.BlockSpec((t, tm), lambda s, i: (0, i)),
                  pl.BlockSpec((t, cp), lambda s, i: (0, jnp.minimum(s, HALF - 1))),
                  pl.BlockSpec((t, cp), lambda s, i: (0, jnp.maximum(s - HALF, 0)))],
        out_specs=pl.BlockSpec((1, tm, cp), lambda s, i: (s, i, 0)),
        out_shape=jax.ShapeDtypeStruct((N_DEV, d, cp), CD),
        compiler_params=_params(("parallel", "parallel")),
    )(h, dgate, dup)


def ffn_up_dg(dgate, dup, u_all, l, name):
    t = dgate.shape[0]
    d, cp = u_all.shape[2], u_all.shape[3]
    tm, tn = _tile(t, 512, LANE), _tile(d, 512, LANE)

    def body(dg_ref, du_ref, u_ref, o_ref, acc_ref):
        s = pl.program_id(2)

        @pl.when(s == 0)
        def _():
            acc_ref[...] = jnp.zeros_like(acc_ref)

        @pl.when(s < HALF)
        def _():
            acc_ref[...] += _nt(dg_ref[...], u_ref[0, 0])

        @pl.when(s >= HALF)
        def _():
            acc_ref[...] += _nt(du_ref[...], u_ref[0, 0])

        @pl.when(s == N_DEV - 1)
        def _():
            o_ref[...] = acc_ref[...]

    return pl.pallas_call(
        body, name=name, grid=(t // tm, d // tn, N_DEV),
        in_specs=[pl.BlockSpec((tm, cp), lambda i, j, s: (i, jnp.minimum(s, HALF - 1))),
                  pl.BlockSpec((tm, cp), lambda i, j, s: (i, jnp.maximum(s - HALF, 0))),
                  pl.BlockSpec((1, 1, tn, cp), lambda i, j, s: (s, l, j, 0))],
        out_specs=pl.BlockSpec((tm, tn), lambda i, j, s: (i, j)),
        out_shape=jax.ShapeDtypeStruct((t, d), F32),
        scratch_shapes=[pltpu.VMEM((tm, tn), F32)],
        compiler_params=_params(("parallel", "parallel", "arbitrary")),
    )(dgate, dup, u_all)


def ffn_dact(dxo, s, wd, gate, up, name):
    t, d = dxo.shape
    f = wd.shape[0]
    tm, tn = _tile(t, 512, LANE), _tile(f, 256, LANE)

    def body(dxo_ref, s_ref, wd_ref, g_ref, u_ref, dg_ref, du_ref):
        da = _nt(dxo_ref[...] * s_ref[...], wd_ref[...])
        gate, up = g_ref[...], u_ref[...]
        sg = jax.nn.sigmoid(gate)
        dg_ref[...] = (da * up * sg * (1.0 + gate * (1.0 - sg))).astype(dg_ref.dtype)
        du_ref[...] = (da * gate * sg).astype(du_ref.dtype)

    o = pl.BlockSpec((tm, tn), lambda i, j: (i, j))
    return pl.pallas_call(
        body, name=name, grid=(t // tm, f // tn),
        in_specs=[pl.BlockSpec((tm, d), lambda i, j: (i, 0)), pl.BlockSpec((1, d), lambda i, j: (0, 0)),
                  pl.BlockSpec((tn, d), lambda i, j: (j, 0)), o, o],
        out_specs=[o, o],
        out_shape=[jax.ShapeDtypeStruct((t, f), CD), jax.ShapeDtypeStruct((t, f), CD)],
        compiler_params=_params(("parallel", "parallel")),
    )(dxo, s, wd, gate, up)


def _bdot(a, b, ca, cb, precision=None):
    if precision is None:
        a, b = a.astype(CD), b.astype(CD)
    return lax.dot_general(a, b, (((ca,), (cb,)), ((0,), (0,))), precision=precision, preferred_element_type=F32)


def _bnn(a, b, precision=None):
    return _bdot(a, b, 2, 1, precision)


def _bnt(a, b, precision=None):
    return _bdot(a, b, 2, 2, precision)


def _btn(a, b, precision=None):
    return _bdot(a, b, 1, 1, precision)


def _att_fn(q, kp, kc, vp, vc, qg, kg, mask_p, mask_c):
    def rn(x, g):
        return x * lax.rsqrt(jnp.mean(x * x, axis=-1, keepdims=True) + EPS) * g

    qn, kpn, kcn = rn(q, qg), rn(kp, kg), rn(kc, kg)
    scale = ATT_HEAD_DIM ** -0.5
    sp = jnp.where(mask_p, _bnt(qn, kpn) * scale, NEG)
    sc = jnp.where(mask_c, _bnt(qn, kcn) * scale, NEG)
    m = jnp.maximum(jnp.max(sp, axis=-1, keepdims=True), jnp.max(sc, axis=-1, keepdims=True))
    pp, pc = jnp.exp(sp - m), jnp.exp(sc - m)
    den = jnp.sum(pp, axis=-1, keepdims=True) + jnp.sum(pc, axis=-1, keepdims=True)
    o = _bnn(pp / den, vp) + _bnn(pc / den, vc)
    return o, m + jnp.log(den)


def _att_masks(g, j, nb_total):
    nb = jnp.int32(nb_total // DILATIONS[0])
    for gi in range(1, len(DILATIONS)):
        nb = jnp.where(g == gi, jnp.int32(nb_total // DILATIONS[gi]), nb)
    has_prev = (j % nb) != 0
    row = lax.broadcasted_iota(jnp.int32, (ATT_BLOCK, ATT_BLOCK), 0)
    col = lax.broadcasted_iota(jnp.int32, (ATT_BLOCK, ATT_BLOCK), 1)
    mask_p = jnp.logical_and(col >= row, has_prev)
    mask_c = col <= row
    return mask_p, mask_c


def _att_specs(nh):
    blk = (1, nh, ATT_BLOCK, ATT_HEAD_DIM)
    cur = pl.BlockSpec(blk, lambda g, j: (g, 0, j, 0))
    prev = pl.BlockSpec(blk, lambda g, j: (g, 0, jnp.maximum(j - 1, 0), 0))
    gain = pl.BlockSpec((1, ATT_HEAD_DIM), lambda g, j: (0, 0))
    lse = pl.BlockSpec((1, nh, ATT_BLOCK, 1), lambda g, j: (g, 0, j, 0))
    return cur, prev, gain, lse


def att_fwd(q, k, v, qg, kg, name):
    ng, nh, t, _ = q.shape
    nbt = t // ATT_BLOCK
    cur, prev, gain, lse_spec = _att_specs(nh)

    def body(q_ref, kp_ref, kc_ref, vp_ref, vc_ref, qg_ref, kg_ref, o_ref, lse_ref):
        mask_p, mask_c = _att_masks(pl.program_id(0), pl.program_id(1), nbt)
        o, lse = _att_fn(q_ref[0], kp_ref[0], kc_ref[0], vp_ref[0], vc_ref[0],
                         qg_ref[...], kg_ref[...], mask_p, mask_c)
        o_ref[0] = o
        lse_ref[0] = lse

    return pl.pallas_call(
        body, name=name, grid=(ng, nbt),
        in_specs=[cur, prev, cur, prev, cur, gain, gain], out_specs=[cur, lse_spec],
        out_shape=[jax.ShapeDtypeStruct(q.shape, F32), jax.ShapeDtypeStruct((ng, nh, t, 1), F32)],
        compiler_params=_params(("parallel", "parallel")),
    )(q, k, k, v, v, qg, kg)


def att_bwd(q, k, v, qg, kg, do, dlse, name):
    ng, nh, t, _ = q.shape
    nbt = t // ATT_BLOCK
    cur, prev, gain, lse_spec = _att_specs(nh)
    whole = pl.BlockSpec((1, nh, t, ATT_HEAD_DIM), lambda g, j: (g, 0, 0, 0))

    def body(q_ref, kp_ref, kc_ref, vp_ref, vc_ref, qg_ref, kg_ref, do_ref, dlse_ref,
             dq_ref, dk_ref, dv_ref, dqg_ref, dkg_ref):
        g, j = pl.program_id(0), pl.program_id(1)
        mask_p, mask_c = _att_masks(g, j, nbt)

        @pl.when(j == 0)
        def _():
            dk_ref[...] = jnp.zeros_like(dk_ref)
            dv_ref[...] = jnp.zeros_like(dv_ref)

        @pl.when(jnp.logical_and(g == 0, j == 0))
        def _():
            dqg_ref[...] = jnp.zeros_like(dqg_ref)
            dkg_ref[...] = jnp.zeros_like(dkg_ref)

        rows_c = pl.ds(pl.multiple_of(j * ATT_BLOCK, ATT_BLOCK), ATT_BLOCK)
        rows_p = pl.ds(pl.multiple_of(jnp.maximum(j - 1, 0) * ATT_BLOCK, ATT_BLOCK), ATT_BLOCK)
        fn = functools.partial(_att_fn, mask_p=mask_p, mask_c=mask_c)
        _, vjp = jax.vjp(fn, q_ref[0], kp_ref[0], kc_ref[0], vp_ref[0], vc_ref[0], qg_ref[...], kg_ref[...])
        dq, dkp, dkc, dvp, dvc, dqg, dkg = vjp((do_ref[0], dlse_ref[0]))
        dq_ref[0] = dq
        dk_ref[0, :, rows_p, :] += dkp
        dv_ref[0, :, rows_p, :] += dvp
        dk_ref[0, :, rows_c, :] += dkc
        dv_ref[0, :, rows_c, :] += dvc
        dqg_ref[...] += dqg
        dkg_ref[...] += dkg

    gshape = jax.ShapeDtypeStruct((1, ATT_HEAD_DIM), F32)
    return pl.pallas_call(
        body, name=name, grid=(ng, nbt),
        in_specs=[cur, prev, cur, prev, cur, gain, gain, cur, lse_spec],
        out_specs=[cur, whole, whole, gain, gain],
        out_shape=[jax.ShapeDtypeStruct(q.shape, F32)] * 3 + [gshape, gshape],
        compiler_params=_params(("arbitrary", "arbitrary")),
    )(q, k, k, v, v, qg, kg, do, dlse)


def _combine_fn(o, lse):
    m = jnp.max(lse, axis=0, keepdims=True)
    e = jnp.exp(lse - m)
    w = e / jnp.sum(e, axis=0, keepdims=True)
    return jnp.sum(w * o, axis=0)


def att_combine(o, lse, name):
    ng, nh, t, hd = o.shape
    tr = _tile(t, 256, 8)
    o_spec = pl.BlockSpec((ng, nh, tr, hd), lambda i: (0, 0, i, 0))
    l_spec = pl.BlockSpec((ng, nh, tr, 1), lambda i: (0, 0, i, 0))
    y_spec = pl.BlockSpec((nh, tr, hd), lambda i: (0, i, 0))

    def body(o_ref, l_ref, y_ref):
        y_ref[...] = _combine_fn(o_ref[...], l_ref[...])

    return pl.pallas_call(
        body, name=name, grid=(t // tr,), in_specs=[o_spec, l_spec], out_specs=y_spec,
        out_shape=jax.ShapeDtypeStruct((nh, t, hd), F32), compiler_params=_params(("parallel",)),
    )(o, lse)


def att_combine_bwd(o, lse, dy, name):
    ng, nh, t, hd = o.shape
    tr = _tile(t, 256, 8)
    o_spec = pl.BlockSpec((ng, nh, tr, hd), lambda i: (0, 0, i, 0))
    l_spec = pl.BlockSpec((ng, nh, tr, 1), lambda i: (0, 0, i, 0))
    y_spec = pl.BlockSpec((nh, tr, hd), lambda i: (0, i, 0))

    def body(o_ref, l_ref, dy_ref, do_ref, dl_ref):
        _, vjp = jax.vjp(_combine_fn, o_ref[...], l_ref[...])
        do, dl = vjp(dy_ref[...])
        do_ref[...] = do
        dl_ref[...] = dl

    return pl.pallas_call(
        body, name=name, grid=(t // tr,), in_specs=[o_spec, l_spec, y_spec], out_specs=[o_spec, l_spec],
        out_shape=[jax.ShapeDtypeStruct(o.shape, F32), jax.ShapeDtypeStruct(lse.shape, F32)],
        compiler_params=_params(("parallel",)),
    )(o, lse, dy)


def _shift_down(x, s):
    if s == 0:
        return x
    row = lax.broadcasted_iota(jnp.int32, x.shape, 0)
    return jnp.where(row >= s, pltpu.roll(x, s, 0), 0.0)


def _shift_up(x, s):
    if s == 0:
        return x
    t = x.shape[0]
    row = lax.broadcasted_iota(jnp.int32, x.shape, 0)
    return jnp.where(row < t - s, pltpu.roll(x, t - s, 0), 0.0)


def conv_fwd(z, col0, w, name):
    t = z.shape[0]
    c = w.shape[1]
    nblk0 = col0 // LANE

    def body(z_ref, w_ref, c_ref):
        zv = z_ref[...]
        acc = None
        for i in range(CONV_WIDTH):
            term = _shift_down(zv, CONV_WIDTH - 1 - i) * w_ref[i:i + 1, :]
            acc = term if acc is None else acc + term
        c_ref[...] = acc

    return pl.pallas_call(
        body, name=name, grid=(c // LANE,),
        in_specs=[pl.BlockSpec((t, LANE), lambda j: (0, nblk0 + j)), pl.BlockSpec((CONV_WIDTH, LANE), lambda j: (0, j))],
        out_specs=pl.BlockSpec((t, LANE), lambda j: (0, j)),
        out_shape=jax.ShapeDtypeStruct((t, c), F32), compiler_params=_params(("parallel",)),
    )(z, w)


def conv_bwd(dc, z, col0, w, name):
    t = z.shape[0]
    c = w.shape[1]
    nblk0 = col0 // LANE

    def body(dc_ref, z_ref, w_ref, dz_ref, dw_ref):
        dcv, zv = dc_ref[...], z_ref[...]
        acc = None
        for i in range(CONV_WIDTH):
            s = CONV_WIDTH - 1 - i
            term = _shift_up(dcv, s) * w_ref[i:i + 1, :]
            acc = term if acc is None else acc + term
            dw_ref[i:i + 1, :] = jnp.sum(dcv * _shift_down(zv, s), axis=0, keepdims=True)
        dz_ref[...] = acc.astype(dz_ref.dtype)

    blk = pl.BlockSpec((t, LANE), lambda j: (0, j))
    wblk = pl.BlockSpec((CONV_WIDTH, LANE), lambda j: (0, j))
    return pl.pallas_call(
        body, name=name, grid=(c // LANE,),
        in_specs=[blk, pl.BlockSpec((t, LANE), lambda j: (0, nblk0 + j)), wblk], out_specs=[blk, wblk],
        out_shape=[jax.ShapeDtypeStruct((t, c), CD), jax.ShapeDtypeStruct((CONV_WIDTH, c), F32)],
        compiler_params=_params(("parallel",)),
    )(dc, z, w)


def _dn_consts():
    c = DN_CHUNK
    row = lax.broadcasted_iota(jnp.int32, (c, c), 0)
    col = lax.broadcasted_iota(jnp.int32, (c, c), 1)
    return dict(tril=row >= col, strict=row > col, eye=(row == col).astype(F32),
                tril_f=(row >= col).astype(F32), triu_f=(row <= col).astype(F32))


def _softplus(x):
    return jnp.maximum(x, 0.0) + jnp.log(1.0 + jnp.exp(-jnp.abs(x)))


def _dn_chunk(cq, ck, cv, og, a_col, b_col, al, dt, gn, s_prev, *, k):
    q = _silu(cq)
    q = q * lax.rsqrt(jnp.sum(q * q, axis=-1, keepdims=True) + EPS) * (DN_HEAD_DIM ** -0.5)
    kk = _silu(ck)
    kk = kk * lax.rsqrt(jnp.sum(kk * kk, axis=-1, keepdims=True) + EPS)
    v = _silu(cv)
    g = -jnp.exp(al) * _softplus(a_col + dt)
    beta = jax.nn.sigmoid(b_col)
    g_row = jnp.sum(k["eye"] * g, axis=1, keepdims=True)
    gc_col = jnp.sum(k["tril_f"] * g_row, axis=2, keepdims=True)
    gc_row = jnp.sum(k["triu_f"] * g, axis=1, keepdims=True)
    ldec = jnp.where(k["tril"], jnp.exp(jnp.where(k["tril"], gc_col - gc_row, 0.0)), 0.0)
    kb, vb = kk * beta, v * beta
    a_mat = jnp.where(k["strict"], _bnt(kb, kk) * ldec, 0.0)
    nk_ = -a_mat
    t_inv = k["eye"] + nk_
    for _ in range(DN_CHUNK.bit_length() - 2):
        nk_ = _bnn(nk_, nk_, HI)
        t_inv = t_inv + _bnn(t_inv, nk_, HI)
    egc = jnp.exp(gc_col)
    u = _bnn(t_inv, vb)
    w = _bnn(t_inv, kb * egc)
    attn = jnp.where(k["tril"], _bnt(q, kk) * ldec, 0.0)
    gc_last = jnp.sum(g, axis=1, keepdims=True)
    k_dec = kk * jnp.exp(gc_last - gc_col)
    v_new = u - _bnn(w, s_prev)
    o = _bnn(q * egc, s_prev) + _bnn(attn, v_new)
    s_new = s_prev * jnp.exp(gc_last) + _btn(k_dec, v_new)
    y = o * lax.rsqrt(jnp.mean(o * o, axis=-1, keepdims=True) + EPS) * gn * _silu(og)
    return y, s_new


def _dn_heads(ref, nh):
    hd = DN_HEAD_DIM
    return jnp.stack([ref[:, h * hd:(h + 1) * hd] for h in range(nh)])


def _dn_specs(nh, col_gate, col_ab, order):
    hd, c = DN_HEAD_DIM, DN_CHUNK
    w = nh * hd
    qs = pl.BlockSpec((c, w), lambda n: (order(n), 0))
    ks = pl.BlockSpec((c, w), lambda n: (order(n), 1))
    vs = pl.BlockSpec((c, w), lambda n: (order(n), 2))
    gs = pl.BlockSpec((c, w), lambda n: (order(n), col_gate // w))
    ab = pl.BlockSpec((c, LANE), lambda n: (order(n), col_ab // LANE))
    scal = pl.BlockSpec((nh, 1, 1), lambda n: (0, 0, 0))
    gn = pl.BlockSpec((1, hd), lambda n: (0, 0))
    st = pl.BlockSpec((1, nh, hd, hd), lambda n: (order(n), 0, 0, 0))
    return qs, ks, vs, gs, ab, scal, gn, st


def _lane_pick(x, idx):
    lane = lax.broadcasted_iota(jnp.int32, x.shape, 1)
    return jnp.sum(jnp.where(lane == idx, x, 0.0), axis=1, keepdims=True)


def dn_fwd(cv, z, col_gate, col_ab, a_log, dt_bias, gn, name):
    t = cv.shape[0]
    nh = a_log.shape[0]
    hd, c = DN_HEAD_DIM, DN_CHUNK
    n_chunks = t // c
    qs, ks, vs, gs, ab, scal, gnspec, st = _dn_specs(nh, col_gate, col_ab, lambda n: n)

    def body(q_ref, k_ref, v_ref, g_ref, ab_ref, al_ref, dt_ref, gn_ref, y_ref, st_ref, s_scr):
        @pl.when(pl.program_id(0) == 0)
        def _():
            s_scr[...] = jnp.zeros_like(s_scr)

        abv = ab_ref[...]
        a_col = jnp.stack([_lane_pick(abv, h) for h in range(nh)])
        b_col = jnp.stack([_lane_pick(abv, nh + h) for h in range(nh)])
        s_prev = s_scr[...]
        st_ref[0] = s_prev
        y, s_new = _dn_chunk(_dn_heads(q_ref, nh), _dn_heads(k_ref, nh), _dn_heads(v_ref, nh), _dn_heads(g_ref, nh),
                             a_col, b_col, al_ref[...], dt_ref[...], gn_ref[...], s_prev, k=_dn_consts())
        for h in range(nh):
            y_ref[:, h * hd:(h + 1) * hd] = y[h].astype(y_ref.dtype)
        s_scr[...] = s_new

    return pl.pallas_call(
        body, name=name, grid=(n_chunks,),
        in_specs=[qs, ks, vs, gs, ab, scal, scal, gnspec],
        out_specs=[pl.BlockSpec((c, nh * hd), lambda n: (n, 0)), st],
        out_shape=[jax.ShapeDtypeStruct((t, nh * hd), CD), jax.ShapeDtypeStruct((n_chunks, nh, hd, hd), F32)],
        scratch_shapes=[pltpu.VMEM((nh, hd, hd), F32)],
        compiler_params=_params(("arbitrary",)),
    )(cv, cv, cv, z, z, a_log, dt_bias, gn)


def dn_bwd(cv, z, col_gate, col_ab, a_log, dt_bias, gn, states, dy, name):
    t = cv.shape[0]
    nh = a_log.shape[0]
    hd, c = DN_HEAD_DIM, DN_CHUNK
    w = nh * hd
    n_chunks = t // c
    rev = lambda n: n_chunks - 1 - n
    qs, ks, vs, gs, ab, scal, gnspec, st = _dn_specs(nh, col_gate, col_ab, rev)
    yspec = pl.BlockSpec((c, w), lambda n: (rev(n), 0))

    def body(q_ref, k_ref, v_ref, g_ref, ab_ref, al_ref, dt_ref, gn_ref, st_ref, dy_ref,
             dq_ref, dk_ref, dv_ref, dg_ref, dab_ref, dal_ref, ddt_ref, dgn_ref, ds_scr):
        @pl.when(pl.program_id(0) == 0)
        def _():
            ds_scr[...] = jnp.zeros_like(ds_scr)
            dal_ref[...] = jnp.zeros_like(dal_ref)
            ddt_ref[...] = jnp.zeros_like(ddt_ref)
            dgn_ref[...] = jnp.zeros_like(dgn_ref)

        abv = ab_ref[...]
        a_col = jnp.stack([_lane_pick(abv, h) for h in range(nh)])
        b_col = jnp.stack([_lane_pick(abv, nh + h) for h in range(nh)])
        fn = functools.partial(_dn_chunk, k=_dn_consts())
        _, vjp = jax.vjp(fn, _dn_heads(q_ref, nh), _dn_heads(k_ref, nh), _dn_heads(v_ref, nh), _dn_heads(g_ref, nh),
                         a_col, b_col, al_ref[...], dt_ref[...], gn_ref[...], st_ref[0])
        dq, dk, dv, dg, da, db, dal, ddt, dgn, ds = vjp((_dn_heads(dy_ref, nh), ds_scr[...]))
        lane = lax.broadcasted_iota(jnp.int32, (c, LANE), 1)
        dab = jnp.zeros((c, LANE), F32)
        for h in range(nh):
            cols = slice(h * hd, (h + 1) * hd)
            dq_ref[:, cols] = dq[h]
            dk_ref[:, cols] = dk[h]
            dv_ref[:, cols] = dv[h]
            dg_ref[:, cols] = dg[h].astype(dg_ref.dtype)
            dab = dab + jnp.where(lane == h, da[h], 0.0) + jnp.where(lane == nh + h, db[h], 0.0)
        dab_ref[...] = dab.astype(dab_ref.dtype)
        dal_ref[...] += dal
        ddt_ref[...] += ddt
        dgn_ref[...] += dgn
        ds_scr[...] = ds

    sshape = jax.ShapeDtypeStruct((nh, 1, 1), F32)
    res = pl.pallas_call(
        body, name=name, grid=(n_chunks,),
        in_specs=[qs, ks, vs, gs, ab, scal, scal, gnspec, st, yspec],
        out_specs=[yspec, yspec, yspec, yspec, pl.BlockSpec((c, LANE), lambda n: (rev(n), 0)), scal, scal, gnspec],
        out_shape=[jax.ShapeDtypeStruct((t, w), F32)] * 3
        + [jax.ShapeDtypeStruct((t, w), CD), jax.ShapeDtypeStruct((t, LANE), CD), sshape, sshape,
           jax.ShapeDtypeStruct((1, hd), F32)],
        scratch_shapes=[pltpu.VMEM((nh, hd, hd), F32)],
        compiler_params=_params(("arbitrary",)),
    )(cv, cv, cv, z, z, a_log, dt_bias, gn, states, dy)
    return res


def merge_fwd(ya, wpa, yd, wpd, z, col_m, name):
    t, d = yd.shape[0], wpd.shape[1]
    tm, tn = _tile(t, 512, LANE), _tile(d, 256, LANE)
    nb1, nb2 = col_m // tn, (col_m + d) // tn

    def body(ya_ref, wpa_ref, yd_ref, wpd_ref, z1_ref, z2_ref, m_ref, pa_ref, pd_ref):
        pa = _nn(ya_ref[...], wpa_ref[...])
        pd = _nn(yd_ref[...], wpd_ref[...])
        pa_ref[...] = pa
        pd_ref[...] = pd
        m_ref[...] = (jax.nn.sigmoid(z1_ref[...]) * pa + jax.nn.sigmoid(z2_ref[...]) * pd).astype(m_ref.dtype)

    o = pl.BlockSpec((tm, tn), lambda i, j: (i, j))
    return pl.pallas_call(
        body, name=name, grid=(t // tm, d // tn),
        in_specs=[pl.BlockSpec((tm, ya.shape[1]), lambda i, j: (i, 0)),
                  pl.BlockSpec((wpa.shape[0], tn), lambda i, j: (0, j)),
                  pl.BlockSpec((tm, yd.shape[1]), lambda i, j: (i, 0)),
                  pl.BlockSpec((wpd.shape[0], tn), lambda i, j: (0, j)),
                  pl.BlockSpec((tm, tn), lambda i, j: (i, nb1 + j)),
                  pl.BlockSpec((tm, tn), lambda i, j: (i, nb2 + j))],
        out_specs=[o, o, o],
        out_shape=[jax.ShapeDtypeStruct((t, d), CD), jax.ShapeDtypeStruct((t, d), F32),
                   jax.ShapeDtypeStruct((t, d), F32)],
        compiler_params=_params(("parallel", "parallel")),
    )(ya, wpa, yd, wpd, z, z)


def merge_bwd(dm, pa, pd, z, col_m, name):
    t, d = dm.shape
    tm, tn = _tile(t, 512, 8), _tile(d, 256, LANE)
    nb1, nb2 = col_m // tn, (col_m + d) // tn

    def body(dm_ref, pa_ref, pd_ref, z1_ref, z2_ref, dpa_ref, dpd_ref, dz1_ref, dz2_ref):
        dmv = dm_ref[...]
        s1, s2 = jax.nn.sigmoid(z1_ref[...]), jax.nn.sigmoid(z2_ref[...])
        dpa_ref[...] = (dmv * s1).astype(dpa_ref.dtype)
        dpd_ref[...] = (dmv * s2).astype(dpd_ref.dtype)
        dz1_ref[...] = (dmv * pa_ref[...] * s1 * (1.0 - s1)).astype(dz1_ref.dtype)
        dz2_ref[...] = (dmv * pd_ref[...] * s2 * (1.0 - s2)).astype(dz2_ref.dtype)

    o = pl.BlockSpec((tm, tn), lambda i, j: (i, j))
    sh = jax.ShapeDtypeStruct((t, d), CD)
    return pl.pallas_call(
        body, name=name, grid=(t // tm, d // tn),
        in_specs=[o, o, o, pl.BlockSpec((tm, tn), lambda i, j: (i, nb1 + j)),
                  pl.BlockSpec((tm, tn), lambda i, j: (i, nb2 + j))],
        out_specs=[o, o, o, o], out_shape=[sh, sh, sh, sh],
        compiler_params=_params(("parallel", "parallel")),
    )(dm, pa, pd, z, z)


def ada_fwd(c_all, w, b, name):
    nl, d, n = w.shape
    tn = _tile(n, 384, LANE)

    def body(c_ref, w_ref, b_ref, o_ref):
        o_ref[0] = _nn(_silu(c_ref[...]), w_ref[0]) + b_ref[0]

    return pl.pallas_call(
        body, name=name, grid=(nl, n // tn),
        in_specs=[pl.BlockSpec(c_all.shape, lambda l, j: (0, 0)), pl.BlockSpec((1, d, tn), lambda l, j: (l, 0, j)),
                  pl.BlockSpec((1, 1, tn), lambda l, j: (l, 0, j))],
        out_specs=pl.BlockSpec((1, c_all.shape[0], tn), lambda l, j: (l, 0, j)),
        out_shape=jax.ShapeDtypeStruct((nl, c_all.shape[0], n), F32),
        compiler_params=_params(("parallel", "parallel")),
    )(c_all, w, b)


def ada_bwd(c_pad, dmod_pad, name):
    nl, kp, n = dmod_pad.shape
    d = c_pad.shape[1]
    tn = _tile(n, 384, LANE)

    def body(c_ref, g_ref, o_ref):
        o_ref[0] = _tn(_silu(c_ref[...]), g_ref[0])

    return pl.pallas_call(
        body, name=name, grid=(nl, n // tn),
        in_specs=[pl.BlockSpec((kp, d), lambda l, j: (0, 0)), pl.BlockSpec((1, kp, tn), lambda l, j: (l, 0, j))],
        out_specs=pl.BlockSpec((1, d, tn), lambda l, j: (l, 0, j)),
        out_shape=jax.ShapeDtypeStruct((nl, d, n), F32),
        compiler_params=_params(("parallel", "parallel")),
    )(c_pad, dmod_pad)


def loss_head(y, target, name):
    t, d = y.shape
    tr = _tile(t, 256, 8)

    def body(y_ref, t_ref, dy_ref, l_ref):
        err = y_ref[...] - t_ref[...]
        dy_ref[...] = err * (1.0 / d)
        part = jnp.sum(jnp.sum(err * err, axis=1, keepdims=True), axis=0, keepdims=True) * (0.5 / d)

        @pl.when(pl.program_id(0) == 0)
        def _():
            l_ref[...] = jnp.zeros_like(l_ref)

        l_ref[...] += part

    row = pl.BlockSpec((tr, d), lambda i: (i, 0))
    return pl.pallas_call(
        body, name=name, grid=(t // tr,), in_specs=[row, row],
        out_specs=[row, pl.BlockSpec((8, LANE), lambda i: (0, 0))],
        out_shape=[jax.ShapeDtypeStruct((t, d), F32), jax.ShapeDtypeStruct((8, LANE), F32)],
        compiler_params=_params(("arbitrary",)),
    )(y, target)


def adamw(w, m, v, g_slots, name):
    shape = w.shape
    nslot = g_slots.shape[0]
    if w.ndim == 2:
        w3, m3, v3, g4 = w[None], m[None], v[None], g_slots[:, None]
    else:
        w3, m3, v3, g4 = w, m, v, g_slots
    nl, r, c = w3.shape
    cg = g4.shape[3]
    tr = _tile(r, 256, 16)

    def body(w_ref, m_ref, v_ref, g_ref, go_ref, d_ref, mo_ref, vo_ref):
        g = g_ref[0, :, :, :c].astype(F32)
        for s in range(1, nslot):
            g = g + g_ref[s, :, :, :c].astype(F32)
        m_new = ADAM_B1 * m_ref[...] + (1.0 - ADAM_B1) * g
        v_new = ADAM_B2 * v_ref[...] + (1.0 - ADAM_B2) * jnp.square(g)
        m_hat = m_new / (1.0 - ADAM_B1 ** ADAM_STEP)
        v_hat = v_new / (1.0 - ADAM_B2 ** ADAM_STEP)
        go_ref[...] = g
        d_ref[...] = -ADAM_LR * (m_hat / (jnp.sqrt(v_hat) + ADAM_EPS) + ADAM_WD * w_ref[...])
        mo_ref[...] = m_new
        vo_ref[...] = v_new

    blk = pl.BlockSpec((1, tr, c), lambda l, i: (l, i, 0))
    gblk = pl.BlockSpec((nslot, 1, tr, cg), lambda l, i: (0, l, i, 0))
    sh = jax.ShapeDtypeStruct(w3.shape, F32)
    outs = pl.pallas_call(
        body, name=name, grid=(nl, r // tr), in_specs=[blk, blk, blk, gblk], out_specs=[blk] * 4,
        out_shape=[sh] * 4, compiler_params=_params(("parallel", "parallel")),
    )(w3, m3, v3, g4)
    return tuple(o.reshape(shape) for o in outs)


def _rows(flat, unit=16):
    n = flat.shape[0]
    per = 1024 * unit
    pad = (-n) % per
    if pad:
        flat = jnp.concatenate([flat, jnp.zeros((pad,), flat.dtype)])
    return flat.reshape(-1, 1024)


def _to_classes(a, t):
    outs = []
    for gi, dil in enumerate(DILATIONS):
        x = a[:, gi]
        x = x.reshape(t // dil, dil, *x.shape[1:]).transpose(2, 1, 0, 3)
        outs.append(x.reshape(x.shape[0], t, x.shape[3]))
    return jnp.stack(outs)


def _from_classes(a, t):
    outs = []
    for gi, dil in enumerate(DILATIONS):
        x = a[gi]
        x = x.reshape(x.shape[0], dil, t // dil, x.shape[2]).transpose(2, 1, 0, 3)
        outs.append(x.reshape(t, x.shape[2], x.shape[3]))
    return jnp.stack(outs, axis=1)


def _unshard_cols(g):
    return g.transpose(1, 0, 2).reshape(g.shape[1], -1)


def _shard_cols(full):
    r = full.shape[0]
    return full.reshape(r, N_DEV, -1).transpose(1, 0, 2)


def kernel(x, c, ada_w, ada_b, norm_ff1, ffn1_w_up, ffn1_w_down, norm_mix, w_in, q_norm, k_norm, conv_w, a_log, dt_bias, dn_norm, w_proj_att, w_proj_dn, w_out, norm_ff2, ffn2_w_up, ffn2_w_down, loss_target, m_ada_w, m_ada_b, m_norm_ff1, m_ffn1_w_up, m_ffn1_w_down, m_norm_mix, m_w_in, m_q_norm, m_k_norm, m_conv_w, m_a_log, m_dt_bias, m_dn_norm, m_w_proj_att, m_w_proj_dn, m_w_out, m_norm_ff2, m_ffn2_w_up, m_ffn2_w_down, v_ada_w, v_ada_b, v_norm_ff1, v_ffn1_w_up, v_ffn1_w_down, v_norm_mix, v_w_in, v_q_norm, v_k_norm, v_conv_w, v_a_log, v_dt_bias, v_dn_norm, v_w_proj_att, v_w_proj_dn, v_w_out, v_norm_ff2, v_ffn2_w_up, v_ffn2_w_down):
    nl = ada_w.shape[0]
    t, d = x.shape[1], x.shape[2]
    dff = ffn1_w_down.shape[1] * N_DEV
    ha = d // 256
    ng = len(DILATIONS)
    att_w = ng * ha * ATT_HEAD_DIM
    nh = d // DN_HEAD_DIM
    dn_w = nh * DN_HEAD_DIM
    n_in = w_in.shape[2] * N_DEV
    off_dn, off_gate = 3 * att_w, 3 * att_w + 3 * dn_w
    off_a = off_gate + dn_w
    off_merge = off_a + 2 * nh
    assert off_merge + 2 * d == n_in
    col_dn, col_gate, col_m = 0, 3 * dn_w, 4 * dn_w
    col_att = col_m + 2 * d
    col_ab = col_att + 3 * att_w
    zw = col_ab + 2 * LANE
    me = 4 * lax.axis_index("x") + 2 * lax.axis_index("y") + lax.axis_index("c")
    xs = x[0]
    target = loss_target[0]

    conv_rows = _rows(conv_w.reshape(-1), 8)
    pack0 = jnp.concatenate([jnp.concatenate([c, jnp.zeros((7, d), F32)]).reshape(-1), conv_rows.reshape(-1)])
    pack0 = _rows(pack0, 8)
    g0 = all_gather([pack0], "ag_c_conv")[0].reshape(N_DEV, -1)
    c_all = g0[:, :d]
    cw = g0[:, 8 * d:8 * d + conv_w.size].reshape(N_DEV, nl, CONV_WIDTH, -1)
    conv_full = cw.transpose(1, 2, 0, 3).reshape(nl, CONV_WIDTH, 3 * dn_w)

    n_ada = ada_w.shape[2]
    b_mine = lax.dynamic_slice_in_dim(ada_b, me * n_ada, n_ada, axis=1)[:, None, :]
    mod_s = ada_fwd(c_all, ada_w, b_mine, "ada_fwd")
    gm = all_gather([mod_s], "ag_mod")[0]
    mod = lax.dynamic_index_in_dim(gm, me, axis=2, keepdims=False)
    mod = mod.transpose(1, 0, 2).reshape(nl, N_ADA, 1, d)

    kinds = [ffn1_w_up, ffn1_w_down, w_in, w_proj_att, w_proj_dn, w_out, ffn2_w_up, ffn2_w_down]
    c_up = ffn1_w_up.shape[2]
    cp = -(-c_up // LANE) * LANE
    r_dn = ffn1_w_down.shape[1]
    assert 2 * r_dn == c_up

    def pad_up(w):
        return jnp.pad(w.astype(CD), ((0, 0), (0, 0), (0, cp - c_up)))

    gathered = all_gather([pad_up(ffn1_w_up), ffn1_w_down.astype(CD), w_in.astype(CD), w_proj_att.astype(CD),
                           w_proj_dn.astype(CD), w_out.astype(CD), pad_up(ffn2_w_up), ffn2_w_down.astype(CD)],
                          "ag_weights")

    def full_weight(ki, l):
        blk = gathered[ki][:, l]
        if ki in (2, 3):
            return _unshard_cols(blk)
        if ki in (1, 7):
            pairs = blk.reshape(HALF, c_up, d)
            return jnp.pad(pairs, ((0, 0), (0, cp - c_up), (0, 0))).reshape(HALF * cp, d)
        return blk.reshape(-1, blk.shape[2])

    def down_grad_slots(g):
        return g.reshape(HALF, cp, d)[:, :c_up].reshape(N_DEV, r_dn, d)

    def w_in_cols(w):
        pad = jnp.zeros((w.shape[0], zw - n_in), w.dtype)
        return jnp.concatenate([w[:, off_dn:off_a], w[:, off_merge:], w[:, :off_dn], w[:, off_a:off_merge], pad], axis=1)

    def w_in_cols_inv(g):
        return jnp.concatenate([g[:, col_att:col_ab], g[:, :col_m], g[:, col_ab:col_ab + 2 * nh], g[:, col_m:col_att]], axis=1)

    saved = []
    xc = xs
    for l in range(nl):
        sv = {}
        sh1, sc1, gt1, sh2, sc2, gt2, sh3, sc3, gt3 = [mod[l, i] for i in range(N_ADA)]
        w_dn1, w_dn2 = full_weight(1, l), full_weight(7, l)
        win = w_in_cols(full_weight(2, l))
        wpa, wpd, wo = full_weight(3, l), full_weight(4, l), full_weight(5, l)
        sv["w"] = (w_dn1, w_dn2, win, wpa, wpd, wo)

        def ffn(xin, g, sh, sc, gt, u_all, w_dn):
            h = norm_mod(xin, g, sc, sh, "norm_mod")
            gate, up, a = ffn_up(h, u_all, l, "ffn_up")
            xo, f = matmul([(a, w_dn)], "nn", F32, "ffn_down", out_scale=0.5 * gt, resid=xin, save_acc=True)
            return xo, (xin, h, gate, up, a, f)

        xc, sv["ffn1"] = ffn(xc, norm_ff1[l:l + 1], sh1, sc1, gt1, gathered[0], w_dn1)

        x_mix = xc
        h2 = norm_mod(x_mix, norm_mix[l:l + 1], sc2, sh2, "norm_mod")
        z = matmul([(h2, win)], "nn", F32, "w_in", tn=512)
        qkv = z[:, col_att:col_ab].reshape(t, 3, ng, ha, ATT_HEAD_DIM)
        qp, kp, vp = (_to_classes(qkv[:, i], t) for i in range(3))
        qg, kg = q_norm[l:l + 1], k_norm[l:l + 1]
        o_cls, lse_cls = att_fwd(qp, kp, vp, qg, kg, "att_fwd")
        o_tok = _from_classes(o_cls, t).transpose(1, 2, 0, 3)
        lse_tok = _from_classes(lse_cls, t).transpose(1, 2, 0, 3)
        ya_h = att_combine(o_tok, lse_tok, "att_combine")
        ya = ya_h.transpose(1, 0, 2).reshape(t, ha * ATT_HEAD_DIM).astype(CD)
        cvo = conv_fwd(z, col_dn, conv_full[l], "conv_fwd")
        al3, dt3 = a_log[l].reshape(nh, 1, 1), dt_bias[l].reshape(nh, 1, 1)
        gn = dn_norm[l:l + 1]
        yd, states = dn_fwd(cvo, z, col_gate, col_ab, al3, dt3, gn, "dn_fwd")
        mrg, pa, pd = merge_fwd(ya, wpa, yd, wpd, z, col_m, "merge_fwd")
        xc, f2 = matmul([(mrg, wo)], "nn", F32, "w_out", out_scale=gt2, resid=x_mix, save_acc=True)
        sv["mix"] = (x_mix, h2, z, qp, kp, vp, o_tok, lse_tok, ya, cvo, yd, states, mrg, pa, pd, f2)

        xc, sv["ffn2"] = ffn(xc, norm_ff2[l:l + 1], sh3, sc3, gt3, gathered[6], w_dn2)
        saved.append(sv)

    dxc, loss_blk = loss_head(xc, target, "loss_head")

    gbig = [[None] * nl for _ in kinds]
    dmods, small = [], []
    for l in reversed(range(nl)):
        sv = saved[l]
        sh1, sc1, gt1, sh2, sc2, gt2, sh3, sc3, gt3 = [mod[l, i] for i in range(N_ADA)]
        w_dn1, w_dn2, win, wpa, wpd, wo = sv["w"]

        def ffn_bwd(dxo, g, sh, sc, gt, u_all, w_dn, sv_f):
            xin, h, gate, up, a, f = sv_f
            s = 0.5 * gt
            g_dn = down_grad_slots(matmul([(a, dxo)], "tn", CD, "ffn_down_wg", out_scale=s))
            dgate, dup = ffn_dact(dxo, s, w_dn, gate, up, "ffn_dact")
            g_up = ffn_up_wg(h, dgate, dup, cp, "ffn_up_wg")
            dh = ffn_up_dg(dgate, dup, u_all, l, "ffn_up_dg")
            dx, dg, dsc, dsh, dgt = norm_mod_bwd(xin, g, sc, sh, dh, dxo, f, 0.5, "norm_mod_bwd")
            return dx, g_up, g_dn, (dg, dsc, dsh, dgt)

        dxc, gbig[6][l], gbig[7][l], (dg3, dsc3, dsh3, dgt3) = ffn_bwd(
            dxc, norm_ff2[l:l + 1], sh3, sc3, gt3, gathered[6], w_dn2, sv["ffn2"])

        x_mix, h2, z, qp, kp, vp, o_tok, lse_tok, ya, cvo, yd, states, mrg, pa, pd, f2 = sv["mix"]
        gbig[5][l] = matmul([(mrg, dxc)], "tn", CD, "w_out_wg", out_scale=gt2)
        dm = matmul([(dxc, wo)], "nt", F32, "w_out_dg", a_scale=gt2)
        dpa, dpd, dz1, dz2 = merge_bwd(dm, pa, pd, z, col_m, "merge_bwd")
        gbig[3][l] = matmul([(ya, dpa)], "tn", CD, "patt_wg")
        gbig[4][l] = matmul([(yd, dpd)], "tn", CD, "pdn_wg")
        dya = matmul([(dpa, wpa)], "nt", F32, "patt_dg")
        dyd = matmul([(dpd, wpd)], "nt", F32, "pdn_dg")
        dya_h = dya.reshape(t, ha, ATT_HEAD_DIM).transpose(1, 0, 2)
        do_tok, dlse_tok = att_combine_bwd(o_tok, lse_tok, dya_h, "att_combine_bwd")
        do_cls = _to_classes(do_tok.transpose(2, 0, 1, 3), t)
        dlse_cls = _to_classes(dlse_tok.transpose(2, 0, 1, 3), t)
        qg, kg = q_norm[l:l + 1], k_norm[l:l + 1]
        dq, dk, dv, dqg, dkg = att_bwd(qp, kp, vp, qg, kg, do_cls, dlse_cls, "att_bwd")
        dz_att = jnp.stack([_from_classes(a_, t) for a_ in (dq, dk, dv)], axis=1).reshape(t, 3 * att_w).astype(CD)
        al3, dt3 = a_log[l].reshape(nh, 1, 1), dt_bias[l].reshape(nh, 1, 1)
        gn = dn_norm[l:l + 1]
        dcq, dck, dcv, dz_gate, dz_ab, dal, ddt, dgn = dn_bwd(
            cvo, z, col_gate, col_ab, al3, dt3, gn, states, dyd, "dn_bwd")
        dcvo = jnp.concatenate([dcq, dck, dcv], axis=1)
        dz_dn, dconv = conv_bwd(dcvo, z, col_dn, conv_full[l], "conv_bwd")
        dz = jnp.concatenate([dz_dn, dz_gate, dz1, dz2, dz_att, dz_ab, jnp.zeros((t, LANE), CD)], axis=1)
        gbig[2][l] = w_in_cols_inv(matmul([(h2, dz)], "tn", CD, "w_in_wg", tn=512))
        dh2 = matmul([(dz, win)], "nt", F32, "w_in_dg", tk=zw // 4 if (zw // 4) % LANE == 0 else zw)
        dxc, dg2, dsc2, dsh2, dgt2 = norm_mod_bwd(x_mix, norm_mix[l:l + 1], sc2, sh2, dh2, dxc, f2, 1.0, "norm_mod_bwd_mix")

        dxc, gbig[0][l], gbig[1][l], (dg1, dsc1, dsh1, dgt1) = ffn_bwd(
            dxc, norm_ff1[l:l + 1], sh1, sc1, gt1, gathered[0], w_dn1, sv["ffn1"])

        dmods.append(jnp.concatenate([dsh1, dsc1, dgt1, dsh2, dsc2, dgt2, dsh3, dsc3, dgt3], axis=1))
        small.append((dg1, dg2, dg3, dconv, dqg, dkg, dal.reshape(1, nh), ddt.reshape(1, nh), dgn))
    dmods.reverse()
    small.reverse()

    def slots(ki, g):
        if ki in (0, 1, 6, 7):
            return g
        return _shard_cols(g) if ki in (2, 3) else g.reshape(N_DEV, -1, g.shape[1])

    send = [jnp.stack([slots(ki, gbig[ki][l]) for l in range(nl)], axis=1) for ki in range(len(kinds))]
    recv = all_to_all(send, "a2a_grads")

    fields = [jnp.stack(dmods).reshape(-1)]
    fields += [jnp.stack([s[i] for s in small]).reshape(-1) for i in range(9)]
    fields.append(loss_blk[0, :1])
    fsizes = [f.size for f in fields]
    foffs = [sum(fsizes[:i]) for i in range(len(fields))]
    g1 = all_gather([_rows(jnp.concatenate(fields), 8)], "ag_small")[0].reshape(N_DEV, -1)

    def field(i, shape):
        return g1[:, foffs[i]:foffs[i] + fsizes[i]].reshape(N_DEV, *shape)

    loss = field(10, (1,))[0, 0]
    for j in range(1, N_DEV):
        loss = loss + field(10, (1,))[j, 0]

    results = {}
    dmod_all = field(0, (nl, N_ADA * d))
    c_pad = jnp.concatenate([c_all, jnp.zeros((8, d), F32)])
    dmod_mine = lax.dynamic_slice_in_dim(dmod_all, me * n_ada, n_ada, axis=2).transpose(1, 0, 2)
    dmod_pad = jnp.concatenate([dmod_mine, jnp.zeros((nl, 8, n_ada), F32)], axis=1)
    g_ada_w = ada_bwd(c_pad, dmod_pad, "ada_bwd")
    results["ada_w"] = adamw(ada_w, m_ada_w, v_ada_w, g_ada_w[None], "adamw_ada_w")
    results["ada_b"] = adamw(ada_b, m_ada_b, v_ada_b, dmod_all, "adamw_ada_b")
    results["norm_ff1"] = adamw(norm_ff1, m_norm_ff1, v_norm_ff1, field(1, (nl, d)), "adamw_norm_ff1")
    results["norm_mix"] = adamw(norm_mix, m_norm_mix, v_norm_mix, field(2, (nl, d)), "adamw_norm_mix")
    results["norm_ff2"] = adamw(norm_ff2, m_norm_ff2, v_norm_ff2, field(3, (nl, d)), "adamw_norm_ff2")
    conv_slots = lax.dynamic_slice_in_dim(field(4, (nl, CONV_WIDTH, 3 * dn_w)), me * conv_w.shape[2],
                                          conv_w.shape[2], axis=3)
    results["conv_w"] = adamw(conv_w, m_conv_w, v_conv_w, conv_slots, "adamw_conv_w")
    results["q_norm"] = adamw(q_norm, m_q_norm, v_q_norm, field(5, (nl, ATT_HEAD_DIM)), "adamw_q_norm")
    results["k_norm"] = adamw(k_norm, m_k_norm, v_k_norm, field(6, (nl, ATT_HEAD_DIM)), "adamw_k_norm")
    results["a_log"] = adamw(a_log, m_a_log, v_a_log, field(7, (nl, nh)), "adamw_a_log")
    results["dt_bias"] = adamw(dt_bias, m_dt_bias, v_dt_bias, field(8, (nl, nh)), "adamw_dt_bias")
    results["dn_norm"] = adamw(dn_norm, m_dn_norm, v_dn_norm, field(9, (nl, DN_HEAD_DIM)), "adamw_dn_norm")
    big_names = ["ffn1_w_up", "ffn1_w_down", "w_in", "w_proj_att", "w_proj_dn", "w_out", "ffn2_w_up", "ffn2_w_down"]
    big_m = [m_ffn1_w_up, m_ffn1_w_down, m_w_in, m_w_proj_att, m_w_proj_dn, m_w_out, m_ffn2_w_up, m_ffn2_w_down]
    big_v = [v_ffn1_w_up, v_ffn1_w_down, v_w_in, v_w_proj_att, v_w_proj_dn, v_w_out, v_ffn2_w_up, v_ffn2_w_down]
    for ki, nm in enumerate(big_names):
        results[nm] = adamw(kinds[ki], big_m[ki], big_v[ki], recv[ki], f"adamw_{nm}")

    order = ["ada_w", "ada_b", "norm_ff1", "ffn1_w_up", "ffn1_w_down", "norm_mix", "w_in", "q_norm", "k_norm",
             "conv_w", "a_log", "dt_bias", "dn_norm", "w_proj_att", "w_proj_dn", "w_out", "norm_ff2",
             "ffn2_w_up", "ffn2_w_down"]
    outs = [loss, dxc[None]]
    for part in range(4):
        outs += [results[n][part] for n in order]
    return tuple(outs)
```

```python
import functools

import jax
import jax.numpy as jnp
from jax import lax
from jax.experimental import pallas as pl
from jax.experimental.pallas import tpu as pltpu

F32 = jnp.float32
CD = jnp.bfloat16
EPS = 1e-6
N_DEV = 8
LANE = 128
ATT_HEAD_DIM = 64
ATT_BLOCK = 128
DILATIONS = (1, 4, 16)
DN_HEAD_DIM = 128
DN_CHUNK = 64
CONV_WIDTH = 4
N_ADA = 9
ADAM_LR, ADAM_B1, ADAM_B2, ADAM_EPS, ADAM_WD, ADAM_STEP = 0.001, 0.9, 0.999, 1e-08, 0.01, 10
VMEM_LIMIT = 56 * 1024 * 1024
NEG = -1e30
MESH = pl.DeviceIdType.MESH
HI = lax.Precision.HIGHEST


def _params(sem=None):
    return pltpu.CompilerParams(dimension_semantics=sem, vmem_limit_bytes=VMEM_LIMIT)


def _tile(n, pref, unit):
    best = None
    t = unit
    while t <= min(n, pref):
        if n % t == 0:
            best = t
        t += unit
    return best if best is not None else n


def _silu(x):
    return x * jax.nn.sigmoid(x)


def _dot(a, b, dims, precision=None):
    if precision is None:
        a, b = a.astype(CD), b.astype(CD)
    return lax.dot_general(a, b, (dims, ((), ())), precision=precision, preferred_element_type=F32)


def _nn(a, b, precision=None):
    return _dot(a, b, ((1,), (0,)), precision)


def _nt(a, b, precision=None):
    return _dot(a, b, ((1,), (1,)), precision)


def _tn(a, b, precision=None):
    return _dot(a, b, ((0,), (0,)), precision)


def all_gather(arrs, name):
    n = len(arrs)

    def body(*refs):
        x_refs, out_refs = refs[:n], refs[n:2 * n]
        send_sems, recv_sems, local_sems = refs[2 * n:]
        x, y, c = lax.axis_index("x"), lax.axis_index("y"), lax.axis_index("c")
        me, sibling = (x, y, c), (x, y, 1 - c)
        chips = [(1 - x, y), (x, 1 - y), (1 - x, 1 - y)]

        def copy(a, k, block, to, src=None):
            slot = out_refs[a].at[4 * block[0] + 2 * block[1] + block[2]]
            return pltpu.make_async_remote_copy(
                src_ref=slot if src is None else src, dst_ref=slot,
                send_sem=send_sems.at[7 * a + k], recv_sem=recv_sems.at[7 * a + k],
                device_id=to, device_id_type=MESH)

        mine, first, passed = [], [], []
        for a in range(n):
            cp = pltpu.make_async_copy(x_refs[a], out_refs[a].at[4 * x + 2 * y + c], local_sems.at[a])
            cp.start()
            mine.append(cp)
            first.append(copy(a, 0, me, sibling, src=x_refs[a]))
            first += [copy(a, 1 + j, me, (*chip, c), src=x_refs[a]) for j, chip in enumerate(chips)]
        for cp in first:
            cp.start()
        for j, chip in enumerate(chips):
            for a in range(n):
                copy(a, 1 + j, (*chip, c), me).wait_recv()
                cp = copy(a, 4 + j, (*chip, c), sibling)
                cp.start()
                passed.append(cp)
        for a in range(n):
            copy(a, 0, sibling, me).wait_recv()
            for j, chip in enumerate(chips):
                copy(a, 4 + j, (*chip, 1 - c), me).wait_recv()
        for cp in first + passed:
            cp.wait_send()
        for cp in mine:
            cp.wait()

    hbm = pl.BlockSpec(memory_space=pl.ANY)
    outs = pl.pallas_call(
        body, name=name,
        out_shape=[jax.ShapeDtypeStruct((N_DEV,) + a.shape, a.dtype) for a in arrs],
        in_specs=[hbm] * n, out_specs=[hbm] * n,
        scratch_shapes=[pltpu.SemaphoreType.DMA((7 * n,)), pltpu.SemaphoreType.DMA((7 * n,)),
                        pltpu.SemaphoreType.DMA((n,))],
    )(*arrs)
    return list(outs)


def all_to_all(arrs, name):
    n = len(arrs)

    def body(*refs):
        x_refs, out_refs = refs[:n], refs[n:2 * n]
        send_sems, recv_sems, local_sems = refs[2 * n:]
        x, y, c = lax.axis_index("x"), lax.axis_index("y"), lax.axis_index("c")
        me = 4 * x + 2 * y + c
        mine, sends = [], []
        for a in range(n):
            cp = pltpu.make_async_copy(x_refs[a].at[me], out_refs[a].at[me], local_sems.at[a])
            cp.start()
            mine.append(cp)
        peers = []
        for k in range(1, N_DEV):
            px, py, pc = x ^ (k >> 2), y ^ ((k >> 1) & 1), c ^ (k & 1)
            peers.append((px, py, pc, 4 * px + 2 * py + pc))
        for k in range(1, N_DEV):
            px, py, pc, pidx = peers[k - 1]
            for a in range(n):
                cp = pltpu.make_async_remote_copy(
                    src_ref=x_refs[a].at[pidx], dst_ref=out_refs[a].at[me],
                    send_sem=send_sems.at[7 * a + k - 1], recv_sem=recv_sems.at[7 * a + k - 1],
                    device_id=(px, py, pc), device_id_type=MESH)
                cp.start()
                sends.append(cp)
        for k in range(1, N_DEV):
            pidx = peers[k - 1][3]
            for a in range(n):
                pltpu.make_async_remote_copy(
                    src_ref=x_refs[a].at[pidx], dst_ref=out_refs[a].at[pidx],
                    send_sem=send_sems.at[7 * a + k - 1], recv_sem=recv_sems.at[7 * a + k - 1],
                    device_id=(x, y, c), device_id_type=MESH).wait_recv()
        for cp in sends:
            cp.wait_send()
        for cp in mine:
            cp.wait()

    hbm = pl.BlockSpec(memory_space=pl.ANY)
    outs = pl.pallas_call(
        body, name=name,
        out_shape=[jax.ShapeDtypeStruct(a.shape, a.dtype) for a in arrs],
        in_specs=[hbm] * n, out_specs=[hbm] * n,
        scratch_shapes=[pltpu.SemaphoreType.DMA((7 * n,)), pltpu.SemaphoreType.DMA((7 * n,)),
                        pltpu.SemaphoreType.DMA((n,))],
    )(*arrs)
    return list(outs)


def matmul(pairs, mode, out_dtype, name, *, a_scale=None, out_scale=None, resid=None,
           save_acc=False, tm=512, tn=512, tk=2048):
    a0, b0 = pairs[0]
    if mode == "nn":
        (m, kdim), n = a0.shape, b0.shape[1]
    elif mode == "nt":
        (m, kdim), n = a0.shape, b0.shape[0]
    else:
        (kdim, m), n = a0.shape, b0.shape[1]
    tm, tn = _tile(m, tm, 8 if m % LANE else LANE), _tile(n, tn, LANE)
    tk = _tile(kdim, tk, LANE)
    nk = kdim // tk
    npairs = len(pairs)
    dims = {"nn": ((1,), (0,)), "nt": ((1,), (1,)), "tn": ((0,), (0,))}[mode]

    if mode == "nn":
        a_spec = pl.BlockSpec((tm, tk), lambda i, j, k: (i, k))
        b_spec = pl.BlockSpec((tk, tn), lambda i, j, k: (k, j))
    elif mode == "nt":
        a_spec = pl.BlockSpec((tm, tk), lambda i, j, k: (i, k))
        b_spec = pl.BlockSpec((tn, tk), lambda i, j, k: (j, k))
    else:
        a_spec = pl.BlockSpec((tk, tm), lambda i, j, k: (k, i))
        b_spec = pl.BlockSpec((tk, tn), lambda i, j, k: (k, j))

    def body(*refs):
        it = iter(refs)
        pair_refs = [(next(it), next(it)) for _ in range(npairs)]
        as_ref = next(it) if a_scale is not None else None
        os_ref = next(it) if out_scale is not None else None
        rs_ref = next(it) if resid is not None else None
        o_ref = next(it)
        acc_out = next(it) if save_acc else None
        acc_ref = next(it) if nk > 1 else None
        part = None
        for a_ref, b_ref in pair_refs:
            a = a_ref[...]
            if as_ref is not None:
                a = a.astype(F32) * (as_ref[...] if mode != "tn" else as_ref[...].reshape(tk, 1))
            d = _dot(a, b_ref[...], dims)
            part = d if part is None else part + d

        def finish(acc):
            if acc_out is not None:
                acc_out[...] = acc
            if os_ref is not None:
                acc = acc * os_ref[...]
            if rs_ref is not None:
                acc = rs_ref[...] + acc
            o_ref[...] = acc.astype(o_ref.dtype)

        if nk == 1:
            finish(part)
        else:
            k = pl.program_id(2)

            @pl.when(k == 0)
            def _():
                acc_ref[...] = part

            @pl.when(k > 0)
            def _():
                acc_ref[...] += part

            @pl.when(k == nk - 1)
            def _():
                finish(acc_ref[...])

    in_specs, args = [], []
    for a, b in pairs:
        in_specs += [a_spec, b_spec]
        args += [a, b]
    if a_scale is not None:
        assert mode != "tn"
        in_specs.append(pl.BlockSpec((1, tk), lambda i, j, k: (0, k)))
        args.append(a_scale)
    if out_scale is not None:
        in_specs.append(pl.BlockSpec((1, tn), lambda i, j, k: (0, j)))
        args.append(out_scale)
    if resid is not None:
        in_specs.append(pl.BlockSpec((tm, tn), lambda i, j, k: (i, j)))
        args.append(resid)
    o_spec = pl.BlockSpec((tm, tn), lambda i, j, k: (i, j))
    out_shape = [jax.ShapeDtypeStruct((m, n), out_dtype)]
    out_specs = [o_spec]
    if save_acc:
        out_shape.append(jax.ShapeDtypeStruct((m, n), F32))
        out_specs.append(o_spec)
    scratch = [pltpu.VMEM((tm, tn), F32)] if nk > 1 else []
    res = pl.pallas_call(
        body, name=name, grid=(m // tm, n // tn, nk),
        in_specs=in_specs, out_specs=out_specs, out_shape=out_shape, scratch_shapes=scratch,
        compiler_params=_params(("parallel", "parallel", "arbitrary")),
    )(*args)
    return res if save_acc else res[0]


def _nm_fn(x, g, sc, sh):
    r = lax.rsqrt(jnp.mean(x * x, axis=-1, keepdims=True) + EPS)
    return (x * r * g) * (1.0 + sc) + sh


def norm_mod(x, g, sc, sh, name):
    t, d = x.shape
    tr = _tile(t, 256, 8)

    def body(x_ref, g_ref, sc_ref, sh_ref, h_ref):
        h_ref[...] = _nm_fn(x_ref[...], g_ref[...], sc_ref[...], sh_ref[...]).astype(h_ref.dtype)

    row = pl.BlockSpec((tr, d), lambda i: (i, 0))
    vec = pl.BlockSpec((1, d), lambda i: (0, 0))
    return pl.pallas_call(
        body, name=name, grid=(t // tr,), in_specs=[row, vec, vec, vec], out_specs=row,
        out_shape=jax.ShapeDtypeStruct((t, d), CD), compiler_params=_params(("parallel",)),
    )(x, g, sc, sh)


def norm_mod_bwd(x, g, sc, sh, dh, dxo, f, gate_scale, name):
    t, d = x.shape
    tr = _tile(t, 256, 8)

    def body(x_ref, g_ref, sc_ref, sh_ref, dh_ref, dxo_ref, f_ref, dx_ref, dg_ref, dsc_ref, dsh_ref, dgt_ref):
        _, vjp = jax.vjp(_nm_fn, x_ref[...], g_ref[...], sc_ref[...], sh_ref[...])
        dx, dg, dsc, dsh = vjp(dh_ref[...])
        dxo_v = dxo_ref[...]
        dx_ref[...] = dxo_v + dx
        dgt = gate_scale * jnp.sum(f_ref[...] * dxo_v, axis=0, keepdims=True)

        @pl.when(pl.program_id(0) == 0)
        def _():
            dg_ref[...] = dg
            dsc_ref[...] = dsc
            dsh_ref[...] = dsh
            dgt_ref[...] = dgt

        @pl.when(pl.program_id(0) > 0)
        def _():
            dg_ref[...] += dg
            dsc_ref[...] += dsc
            dsh_ref[...] += dsh
            dgt_ref[...] += dgt

    row = pl.BlockSpec((tr, d), lambda i: (i, 0))
    vec = pl.BlockSpec((1, d), lambda i: (0, 0))
    vshape = jax.ShapeDtypeStruct((1, d), F32)
    return pl.pallas_call(
        body, name=name, grid=(t // tr,), in_specs=[row, vec, vec, vec, row, row, row],
        out_specs=[row, vec, vec, vec, vec],
        out_shape=[jax.ShapeDtypeStruct((t, d), F32), vshape, vshape, vshape, vshape],
        compiler_params=_params(("arbitrary",)),
    )(x, g, sc, sh, dh, dxo, f)


HALF = N_DEV // 2


def ffn_up(h, u_all, l, name):
    t, d = h.shape
    cp = u_all.shape[3]
    f = HALF * cp
    tm, tn = _tile(t, 1024, LANE), _tile(cp, 384, LANE)
    per = cp // tn

    def body(h_ref, wg_ref, wu_ref, g_ref, u_ref, a_ref):
        hv = h_ref[...]
        gate = _nn(hv, wg_ref[0, 0])
        up = _nn(hv, wu_ref[0, 0])
        g_ref[...] = gate.astype(g_ref.dtype)
        u_ref[...] = up.astype(u_ref.dtype)
        a_ref[...] = (_silu(gate) * up).astype(a_ref.dtype)

    o = pl.BlockSpec((tm, tn), lambda i, j: (i, j))
    wg = pl.BlockSpec((1, 1, d, tn), lambda i, j: (j // per, l, 0, j % per))
    wu = pl.BlockSpec((1, 1, d, tn), lambda i, j: (HALF + j // per, l, 0, j % per))
    return pl.pallas_call(
        body, name=name, grid=(t // tm, f // tn),
        in_specs=[pl.BlockSpec((tm, d), lambda i, j: (i, 0)), wg, wu], out_specs=[o, o, o],
        out_shape=[jax.ShapeDtypeStruct((t, f), CD)] * 3,
        compiler_params=_params(("parallel", "parallel")),
    )(h, u_all, u_all)


def ffn_up_wg(h, dgate, dup, cp, name):
    t, d = h.shape
    tm = _tile(d, 512, LANE)

    def body(h_ref, dg_ref, du_ref, o_ref):
        s = pl.program_id(0)

        @pl.when(s < HALF)
        def _():
            o_ref[0] = _tn(h_ref[...], dg_ref[...]).astype(o_ref.dtype)

        @pl.when(s >= HALF)
        def _():
            o_ref[0] = _tn(h_ref[...], du_ref[...]).astype(o_ref.dtype)

    return pl.pallas_call(
        body, name=name, grid=(N_DEV, d // tm),
        in_specs=[pl.BlockSpec((t, tm), lambda s, i: (0, i)),
                  pl.BlockSpec((t, cp), lambda s, i: (0, jnp.minimum(s, HALF - 1))),
                  pl.BlockSpec((t, cp), lambda s, i: (0, jnp.maximum(s - HALF, 0)))],
        out_specs=pl.BlockSpec((1, tm, cp), lambda s, i: (s, i, 0)),
        out_shape=jax.ShapeDtypeStruct((N_DEV, d, cp), CD),
        compiler_params=_params(("parallel", "parallel")),
    )(h, dgate, dup)


def ffn_up_dg(dgate, dup, u_all, l, name):
    t = dgate.shape[0]
    d, cp = u_all.shape[2], u_all.shape[3]
    tm, tn = _tile(t, 512, LANE), _tile(d, 512, LANE)

    def body(dg_ref, du_ref, u_ref, o_ref, acc_ref):
        s = pl.program_id(2)

        @pl.when(s == 0)
        def _():
            acc_ref[...] = jnp.zeros_like(acc_ref)

        @pl.when(s < HALF)
        def _():
            acc_ref[...] += _nt(dg_ref[...], u_ref[0, 0])

        @pl.when(s >= HALF)
        def _():
            acc_ref[...] += _nt(du_ref[...], u_ref[0, 0])

        @pl.when(s == N_DEV - 1)
        def _():
            o_ref[...] = acc_ref[...]

    return pl.pallas_call(
        body, name=name, grid=(t // tm, d // tn, N_DEV),
        in_specs=[pl.BlockSpec((tm, cp), lambda i, j, s: (i, jnp.minimum(s, HALF - 1))),
                  pl.BlockSpec((tm, cp), lambda i, j, s: (i, jnp.maximum(s - HALF, 0))),
                  pl.BlockSpec((1, 1, tn, cp), lambda i, j, s: (s, l, j, 0))],
        out_specs=pl.BlockSpec((tm, tn), lambda i, j, s: (i, j)),
        out_shape=jax.ShapeDtypeStruct((t, d), F32),
        scratch_shapes=[pltpu.VMEM((tm, tn), F32)],
        compiler_params=_params(("parallel", "parallel", "arbitrary")),
    )(dgate, dup, u_all)


def ffn_dact(dxo, s, wd, gate, up, name):
    t, d = dxo.shape
    f = wd.shape[0]
    tm, tn = _tile(t, 1024, LANE), _tile(f, 384, LANE)

    def body(dxo_ref, s_ref, wd_ref, g_ref, u_ref, dg_ref, du_ref):
        da = _nt(dxo_ref[...] * s_ref[...], wd_ref[...])
        gate, up = g_ref[...].astype(F32), u_ref[...].astype(F32)
        sg = jax.nn.sigmoid(gate)
        dg_ref[...] = (da * up * sg * (1.0 + gate * (1.0 - sg))).astype(dg_ref.dtype)
        du_ref[...] = (da * gate * sg).astype(du_ref.dtype)

    o = pl.BlockSpec((tm, tn), lambda i, j: (i, j))
    return pl.pallas_call(
        body, name=name, grid=(t // tm, f // tn),
        in_specs=[pl.BlockSpec((tm, d), lambda i, j: (i, 0)), pl.BlockSpec((1, d), lambda i, j: (0, 0)),
                  pl.BlockSpec((tn, d), lambda i, j: (j, 0)), o, o],
        out_specs=[o, o],
        out_shape=[jax.ShapeDtypeStruct((t, f), CD), jax.ShapeDtypeStruct((t, f), CD)],
        compiler_params=_params(("parallel", "parallel")),
    )(dxo, s, wd, gate, up)


def _bdot(a, b, ca, cb, precision=None):
    if precision is None:
        a, b = a.astype(CD), b.astype(CD)
    return lax.dot_general(a, b, (((ca,), (cb,)), ((0,), (0,))), precision=precision, preferred_element_type=F32)


def _bnn(a, b, precision=None):
    return _bdot(a, b, 2, 1, precision)


def _bnt(a, b, precision=None):
    return _bdot(a, b, 2, 2, precision)


def _btn(a, b, precision=None):
    return _bdot(a, b, 1, 1, precision)


def _att_fn(q, kp, kc, vp, vc, qg, kg, mask_p, mask_c):
    def rn(x, g):
        return x * lax.rsqrt(jnp.mean(x * x, axis=-1, keepdims=True) + EPS) * g

    qn, kpn, kcn = rn(q, qg), rn(kp, kg), rn(kc, kg)
    scale = ATT_HEAD_DIM ** -0.5
    sp = jnp.where(mask_p, _bnt(qn, kpn) * scale, NEG)
    sc = jnp.where(mask_c, _bnt(qn, kcn) * scale, NEG)
    m = jnp.maximum(jnp.max(sp, axis=-1, keepdims=True), jnp.max(sc, axis=-1, keepdims=True))
    pp, pc = jnp.exp(sp - m), jnp.exp(sc - m)
    den = jnp.sum(pp, axis=-1, keepdims=True) + jnp.sum(pc, axis=-1, keepdims=True)
    o = _bnn(pp / den, vp) + _bnn(pc / den, vc)
    return o, m + jnp.log(den)


def _att_masks(g, j, nb_total):
    nb = jnp.int32(nb_total // DILATIONS[0])
    for gi in range(1, len(DILATIONS)):
        nb = jnp.where(g == gi, jnp.int32(nb_total // DILATIONS[gi]), nb)
    has_prev = (j % nb) != 0
    row = lax.broadcasted_iota(jnp.int32, (ATT_BLOCK, ATT_BLOCK), 0)
    col = lax.broadcasted_iota(jnp.int32, (ATT_BLOCK, ATT_BLOCK), 1)
    mask_p = jnp.logical_and(col >= row, has_prev)
    mask_c = col <= row
    return mask_p, mask_c


def _att_specs(nh):
    blk = (1, nh, ATT_BLOCK, ATT_HEAD_DIM)
    cur = pl.BlockSpec(blk, lambda g, j: (g, 0, j, 0))
    prev = pl.BlockSpec(blk, lambda g, j: (g, 0, jnp.maximum(j - 1, 0), 0))
    gain = pl.BlockSpec((1, ATT_HEAD_DIM), lambda g, j: (0, 0))
    lse = pl.BlockSpec((1, nh, ATT_BLOCK, 1), lambda g, j: (g, 0, j, 0))
    return cur, prev, gain, lse


def att_fwd(q, k, v, qg, kg, name):
    ng, nh, t, _ = q.shape
    nbt = t // ATT_BLOCK
    cur, prev, gain, lse_spec = _att_specs(nh)

    def body(q_ref, kp_ref, kc_ref, vp_ref, vc_ref, qg_ref, kg_ref, o_ref, lse_ref):
        mask_p, mask_c = _att_masks(pl.program_id(0), pl.program_id(1), nbt)
        o, lse = _att_fn(q_ref[0], kp_ref[0], kc_ref[0], vp_ref[0], vc_ref[0],
                         qg_ref[...], kg_ref[...], mask_p, mask_c)
        o_ref[0] = o
        lse_ref[0] = lse

    return pl.pallas_call(
        body, name=name, grid=(ng, nbt),
        in_specs=[cur, prev, cur, prev, cur, gain, gain], out_specs=[cur, lse_spec],
        out_shape=[jax.ShapeDtypeStruct(q.shape, F32), jax.ShapeDtypeStruct((ng, nh, t, 1), F32)],
        compiler_params=_params(("parallel", "parallel")),
    )(q, k, k, v, v, qg, kg)


def att_bwd(q, k, v, qg, kg, do, dlse, name):
    ng, nh, t, _ = q.shape
    nbt = t // ATT_BLOCK
    cur, prev, gain, lse_spec = _att_specs(nh)
    whole = pl.BlockSpec((1, nh, t, ATT_HEAD_DIM), lambda g, j: (g, 0, 0, 0))

    def body(q_ref, kp_ref, kc_ref, vp_ref, vc_ref, qg_ref, kg_ref, do_ref, dlse_ref,
             dq_ref, dk_ref, dv_ref, dqg_ref, dkg_ref):
        g, j = pl.program_id(0), pl.program_id(1)
        mask_p, mask_c = _att_masks(g, j, nbt)

        @pl.when(j == 0)
        def _():
            dk_ref[...] = jnp.zeros_like(dk_ref)
            dv_ref[...] = jnp.zeros_like(dv_ref)

        @pl.when(jnp.logical_and(g == 0, j == 0))
        def _():
            dqg_ref[...] = jnp.zeros_like(dqg_ref)
            dkg_ref[...] = jnp.zeros_like(dkg_ref)

        rows_c = pl.ds(pl.multiple_of(j * ATT_BLOCK, ATT_BLOCK), ATT_BLOCK)
        rows_p = pl.ds(pl.multiple_of(jnp.maximum(j - 1, 0) * ATT_BLOCK, ATT_BLOCK), ATT_BLOCK)
        fn = functools.partial(_att_fn, mask_p=mask_p, mask_c=mask_c)
        _, vjp = jax.vjp(fn, q_ref[0], kp_ref[0], kc_ref[0], vp_ref[0], vc_ref[0], qg_ref[...], kg_ref[...])
        dq, dkp, dkc, dvp, dvc, dqg, dkg = vjp((do_ref[0], dlse_ref[0]))
        dq_ref[0] = dq
        dk_ref[0, :, rows_p, :] += dkp
        dv_ref[0, :, rows_p, :] += dvp
        dk_ref[0, :, rows_c, :] += dkc
        dv_ref[0, :, rows_c, :] += dvc
        dqg_ref[...] += dqg
        dkg_ref[...] += dkg

    gshape = jax.ShapeDtypeStruct((1, ATT_HEAD_DIM), F32)
    return pl.pallas_call(
        body, name=name, grid=(ng, nbt),
        in_specs=[cur, prev, cur, prev, cur, gain, gain, cur, lse_spec],
        out_specs=[cur, whole, whole, gain, gain],
        out_shape=[jax.ShapeDtypeStruct(q.shape, F32)] * 3 + [gshape, gshape],
        compiler_params=_params(("arbitrary", "arbitrary")),
    )(q, k, k, v, v, qg, kg, do, dlse)


def _combine_fn(o, lse):
    m = jnp.max(lse, axis=0, keepdims=True)
    e = jnp.exp(lse - m)
    w = e / jnp.sum(e, axis=0, keepdims=True)
    return jnp.sum(w * o, axis=0)


def att_combine(o, lse, name):
    ng, nh, t, hd = o.shape
    tr = _tile(t, 256, 8)
    o_spec = pl.BlockSpec((ng, nh, tr, hd), lambda i: (0, 0, i, 0))
    l_spec = pl.BlockSpec((ng, nh, tr, 1), lambda i: (0, 0, i, 0))
    y_spec = pl.BlockSpec((nh, tr, hd), lambda i: (0, i, 0))

    def body(o_ref, l_ref, y_ref):
        y_ref[...] = _combine_fn(o_ref[...], l_ref[...])

    return pl.pallas_call(
        body, name=name, grid=(t // tr,), in_specs=[o_spec, l_spec], out_specs=y_spec,
        out_shape=jax.ShapeDtypeStruct((nh, t, hd), F32), compiler_params=_params(("parallel",)),
    )(o, lse)


def att_combine_bwd(o, lse, dy, name):
    ng, nh, t, hd = o.shape
    tr = _tile(t, 256, 8)
    o_spec = pl.BlockSpec((ng, nh, tr, hd), lambda i: (0, 0, i, 0))
    l_spec = pl.BlockSpec((ng, nh, tr, 1), lambda i: (0, 0, i, 0))
    y_spec = pl.BlockSpec((nh, tr, hd), lambda i: (0, i, 0))

    def body(o_ref, l_ref, dy_ref, do_ref, dl_ref):
        _, vjp = jax.vjp(_combine_fn, o_ref[...], l_ref[...])
        do, dl = vjp(dy_ref[...])
        do_ref[...] = do
        dl_ref[...] = dl

    return pl.pallas_call(
        body, name=name, grid=(t // tr,), in_specs=[o_spec, l_spec, y_spec], out_specs=[o_spec, l_spec],
        out_shape=[jax.ShapeDtypeStruct(o.shape, F32), jax.ShapeDtypeStruct(lse.shape, F32)],
        compiler_params=_params(("parallel",)),
    )(o, lse, dy)


def _shift_down(x, s):
    if s == 0:
        return x
    row = lax.broadcasted_iota(jnp.int32, x.shape, 0)
    return jnp.where(row >= s, pltpu.roll(x, s, 0), 0.0)


def _shift_up(x, s):
    if s == 0:
        return x
    t = x.shape[0]
    row = lax.broadcasted_iota(jnp.int32, x.shape, 0)
    return jnp.where(row < t - s, pltpu.roll(x, t - s, 0), 0.0)


def conv_fwd(z, col0, w, name):
    t = z.shape[0]
    c = w.shape[1]
    nblk0 = col0 // LANE

    def body(z_ref, w_ref, c_ref):
        zv = z_ref[...]
        acc = None
        for i in range(CONV_WIDTH):
            term = _shift_down(zv, CONV_WIDTH - 1 - i) * w_ref[i:i + 1, :]
            acc = term if acc is None else acc + term
        c_ref[...] = acc

    return pl.pallas_call(
        body, name=name, grid=(c // LANE,),
        in_specs=[pl.BlockSpec((t, LANE), lambda j: (0, nblk0 + j)), pl.BlockSpec((CONV_WIDTH, LANE), lambda j: (0, j))],
        out_specs=pl.BlockSpec((t, LANE), lambda j: (0, j)),
        out_shape=jax.ShapeDtypeStruct((t, c), F32), compiler_params=_params(("parallel",)),
    )(z, w)


def conv_bwd(dc, z, col0, w, name):
    t = z.shape[0]
    c = w.shape[1]
    nblk0 = col0 // LANE

    def body(dc_ref, z_ref, w_ref, dz_ref, dw_ref):
        dcv, zv = dc_ref[...], z_ref[...]
        acc = None
        for i in range(CONV_WIDTH):
            s = CONV_WIDTH - 1 - i
            term = _shift_up(dcv, s) * w_ref[i:i + 1, :]
            acc = term if acc is None else acc + term
            dw_ref[i:i + 1, :] = jnp.sum(dcv * _shift_down(zv, s), axis=0, keepdims=True)
        dz_ref[...] = acc.astype(dz_ref.dtype)

    blk = pl.BlockSpec((t, LANE), lambda j: (0, j))
    wblk = pl.BlockSpec((CONV_WIDTH, LANE), lambda j: (0, j))
    return pl.pallas_call(
        body, name=name, grid=(c // LANE,),
        in_specs=[blk, pl.BlockSpec((t, LANE), lambda j: (0, nblk0 + j)), wblk], out_specs=[blk, wblk],
        out_shape=[jax.ShapeDtypeStruct((t, c), CD), jax.ShapeDtypeStruct((CONV_WIDTH, c), F32)],
        compiler_params=_params(("parallel",)),
    )(dc, z, w)


def _dn_consts():
    c = DN_CHUNK
    row = lax.broadcasted_iota(jnp.int32, (c, c), 0)
    col = lax.broadcasted_iota(jnp.int32, (c, c), 1)
    return dict(tril=row >= col, strict=row > col, eye=(row == col).astype(F32),
                tril_f=(row >= col).astype(F32), triu_f=(row <= col).astype(F32))


def _softplus(x):
    return jnp.maximum(x, 0.0) + jnp.log(1.0 + jnp.exp(-jnp.abs(x)))


def _split(x):
    hi = x.astype(CD)
    return hi, (x - hi.astype(F32)).astype(CD)


def _bdot3(a, b, ca, cb):
    ah, al = _split(a)
    bh, bl = _split(b)
    return _bdot(ah, bh, ca, cb) + (_bdot(ah, bl, ca, cb) + _bdot(al, bh, ca, cb))


def _tri_inv_impl(a_mat):
    c = a_mat.shape[-1]
    eye = (lax.broadcasted_iota(jnp.int32, (c, c), 0) == lax.broadcasted_iota(jnp.int32, (c, c), 1)).astype(F32)
    nk_ = -a_mat
    t_inv = eye + nk_
    for _ in range(c.bit_length() - 2):
        nk_ = _bdot3(nk_, nk_, 2, 1)
        t_inv = t_inv + _bdot3(t_inv, nk_, 2, 1)
    return t_inv


@jax.custom_vjp
def _tri_inv(a_mat):
    return _tri_inv_impl(a_mat)


def _tri_inv_fwd(a_mat):
    t_inv = _tri_inv_impl(a_mat)
    return t_inv, t_inv


def _tri_inv_bwd(t_inv, dt_inv):
    return (-_bdot3(_bdot3(t_inv, dt_inv, 1, 1), t_inv, 2, 2),)


_tri_inv.defvjp(_tri_inv_fwd, _tri_inv_bwd)


def _dn_chunk(cq, ck, cv, og, a_col, b_col, al, dt, gn, s_prev, *, k, inv):
    q = _silu(cq)
    q = q * lax.rsqrt(jnp.sum(q * q, axis=-1, keepdims=True) + EPS) * (DN_HEAD_DIM ** -0.5)
    kk = _silu(ck)
    kk = kk * lax.rsqrt(jnp.sum(kk * kk, axis=-1, keepdims=True) + EPS)
    v = _silu(cv)
    g = -jnp.exp(al) * _softplus(a_col + dt)
    beta = jax.nn.sigmoid(b_col)
    g_row = jnp.sum(k["eye"] * g, axis=1, keepdims=True)
    gc_col = jnp.sum(k["tril_f"] * g_row, axis=2, keepdims=True)
    gc_row = jnp.sum(k["triu_f"] * g, axis=1, keepdims=True)
    ldec = jnp.where(k["tril"], jnp.exp(jnp.where(k["tril"], gc_col - gc_row, 0.0)), 0.0)
    kb, vb = kk * beta, v * beta
    a_mat = jnp.where(k["strict"], _bnt(kb, kk) * ldec, 0.0)
    t_inv = inv(a_mat)
    egc = jnp.exp(gc_col)
    u = _bnn(t_inv, vb)
    w = _bnn(t_inv, kb * egc)
    attn = jnp.where(k["tril"], _bnt(q, kk) * ldec, 0.0)
    gc_last = jnp.sum(g, axis=1, keepdims=True)
    k_dec = kk * jnp.exp(gc_last - gc_col)
    v_new = u - _bnn(w, s_prev)
    o = _bnn(q * egc, s_prev) + _bnn(attn, v_new)
    s_new = s_prev * jnp.exp(gc_last) + _btn(k_dec, v_new)
    y = o * lax.rsqrt(jnp.mean(o * o, axis=-1, keepdims=True) + EPS) * gn * _silu(og)
    return y, s_new


def _dn_heads(ref, nh):
    hd = DN_HEAD_DIM
    return jnp.stack([ref[:, h * hd:(h + 1) * hd] for h in range(nh)])


def _dn_specs(nh, col_gate, col_ab, order):
    hd, c = DN_HEAD_DIM, DN_CHUNK
    w = nh * hd
    qs = pl.BlockSpec((c, w), lambda n: (order(n), 0))
    ks = pl.BlockSpec((c, w), lambda n: (order(n), 1))
    vs = pl.BlockSpec((c, w), lambda n: (order(n), 2))
    gs = pl.BlockSpec((c, w), lambda n: (order(n), col_gate // w))
    ab = pl.BlockSpec((c, LANE), lambda n: (order(n), col_ab // LANE))
    scal = pl.BlockSpec((nh, 1, 1), lambda n: (0, 0, 0))
    gn = pl.BlockSpec((1, hd), lambda n: (0, 0))
    st = pl.BlockSpec((1, nh, hd, hd), lambda n: (order(n), 0, 0, 0))
    return qs, ks, vs, gs, ab, scal, gn, st


def _lane_pick(x, idx):
    lane = lax.broadcasted_iota(jnp.int32, x.shape, 1)
    return jnp.sum(jnp.where(lane == idx, x, 0.0), axis=1, keepdims=True)


def dn_fwd(cv, z, col_gate, col_ab, a_log, dt_bias, gn, name):
    t = cv.shape[0]
    nh = a_log.shape[0]
    hd, c = DN_HEAD_DIM, DN_CHUNK
    n_chunks = t // c
    qs, ks, vs, gs, ab, scal, gnspec, st = _dn_specs(nh, col_gate, col_ab, lambda n: n)

    def body(q_ref, k_ref, v_ref, g_ref, ab_ref, al_ref, dt_ref, gn_ref, y_ref, st_ref, s_scr):
        @pl.when(pl.program_id(0) == 0)
        def _():
            s_scr[...] = jnp.zeros_like(s_scr)

        abv = ab_ref[...]
        a_col = jnp.stack([_lane_pick(abv, h) for h in range(nh)])
        b_col = jnp.stack([_lane_pick(abv, nh + h) for h in range(nh)])
        s_prev = s_scr[...]
        st_ref[0] = s_prev
        y, s_new = _dn_chunk(_dn_heads(q_ref, nh), _dn_heads(k_ref, nh), _dn_heads(v_ref, nh), _dn_heads(g_ref, nh),
                             a_col, b_col, al_ref[...], dt_ref[...], gn_ref[...], s_prev,
                             k=_dn_consts(), inv=_tri_inv_impl)
        for h in range(nh):
            y_ref[:, h * hd:(h + 1) * hd] = y[h].astype(y_ref.dtype)
        s_scr[...] = s_new

    return pl.pallas_call(
        body, name=name, grid=(n_chunks,),
        in_specs=[qs, ks, vs, gs, ab, scal, scal, gnspec],
        out_specs=[pl.BlockSpec((c, nh * hd), lambda n: (n, 0)), st],
        out_shape=[jax.ShapeDtypeStruct((t, nh * hd), CD), jax.ShapeDtypeStruct((n_chunks, nh, hd, hd), F32)],
        scratch_shapes=[pltpu.VMEM((nh, hd, hd), F32)],
        compiler_params=_params(("arbitrary",)),
    )(cv, cv, cv, z, z, a_log, dt_bias, gn)


def dn_bwd(cv, z, col_gate, col_ab, a_log, dt_bias, gn, states, dy, name):
    t = cv.shape[0]
    nh = a_log.shape[0]
    hd, c = DN_HEAD_DIM, DN_CHUNK
    w = nh * hd
    n_chunks = t // c
    rev = lambda n: n_chunks - 1 - n
    qs, ks, vs, gs, ab, scal, gnspec, st = _dn_specs(nh, col_gate, col_ab, rev)
    yspec = pl.BlockSpec((c, w), lambda n: (rev(n), 0))

    def body(q_ref, k_ref, v_ref, g_ref, ab_ref, al_ref, dt_ref, gn_ref, st_ref, dy_ref,
             dc_ref, dg_ref, dab_ref, dal_ref, ddt_ref, dgn_ref, ds_scr):
        @pl.when(pl.program_id(0) == 0)
        def _():
            ds_scr[...] = jnp.zeros_like(ds_scr)
            dal_ref[...] = jnp.zeros_like(dal_ref)
            ddt_ref[...] = jnp.zeros_like(ddt_ref)
            dgn_ref[...] = jnp.zeros_like(dgn_ref)

        abv = ab_ref[...]
        a_col = jnp.stack([_lane_pick(abv, h) for h in range(nh)])
        b_col = jnp.stack([_lane_pick(abv, nh + h) for h in range(nh)])
        fn = functools.partial(_dn_chunk, k=_dn_consts(), inv=_tri_inv)
        _, vjp = jax.vjp(fn, _dn_heads(q_ref, nh), _dn_heads(k_ref, nh), _dn_heads(v_ref, nh), _dn_heads(g_ref, nh),
                         a_col, b_col, al_ref[...], dt_ref[...], gn_ref[...], st_ref[0])
        dq, dk, dv, dg, da, db, dal, ddt, dgn, ds = vjp((_dn_heads(dy_ref, nh), ds_scr[...]))
        lane = lax.broadcasted_iota(jnp.int32, (c, LANE), 1)
        dab = jnp.zeros((c, LANE), F32)
        for h in range(nh):
            cols = slice(h * hd, (h + 1) * hd)
            dc_ref[:, cols] = dq[h]
            dc_ref[:, w + h * hd:w + (h + 1) * hd] = dk[h]
            dc_ref[:, 2 * w + h * hd:2 * w + (h + 1) * hd] = dv[h]
            dg_ref[:, cols] = dg[h].astype(dg_ref.dtype)
            dab = dab + jnp.where(lane == h, da[h], 0.0) + jnp.where(lane == nh + h, db[h], 0.0)
        dab_ref[...] = dab.astype(dab_ref.dtype)
        dal_ref[...] += dal
        ddt_ref[...] += ddt
        dgn_ref[...] += dgn
        ds_scr[...] = ds

    sshape = jax.ShapeDtypeStruct((nh, 1, 1), F32)
    res = pl.pallas_call(
        body, name=name, grid=(n_chunks,),
        in_specs=[qs, ks, vs, gs, ab, scal, scal, gnspec, st, yspec],
        out_specs=[pl.BlockSpec((c, 3 * w), lambda n: (rev(n), 0)), yspec,
                   pl.BlockSpec((c, LANE), lambda n: (rev(n), 0)), scal, scal, gnspec],
        out_shape=[jax.ShapeDtypeStruct((t, 3 * w), F32), jax.ShapeDtypeStruct((t, w), CD),
                   jax.ShapeDtypeStruct((t, LANE), CD), sshape, sshape, jax.ShapeDtypeStruct((1, hd), F32)],
        scratch_shapes=[pltpu.VMEM((nh, hd, hd), F32)],
        compiler_params=_params(("arbitrary",)),
    )(cv, cv, cv, z, z, a_log, dt_bias, gn, states, dy)
    return res


def merge_fwd(ya, wpa, yd, wpd, z, col_m, name):
    t, d = yd.shape[0], wpd.shape[1]
    tm, tn = _tile(t, 512, LANE), _tile(d, 256, LANE)
    nb1, nb2 = col_m // tn, (col_m + d) // tn

    def body(ya_ref, wpa_ref, yd_ref, wpd_ref, z1_ref, z2_ref, m_ref, pa_ref, pd_ref):
        pa = _nn(ya_ref[...], wpa_ref[...])
        pd = _nn(yd_ref[...], wpd_ref[...])
        pa_ref[...] = pa
        pd_ref[...] = pd
        m_ref[...] = (jax.nn.sigmoid(z1_ref[...]) * pa + jax.nn.sigmoid(z2_ref[...]) * pd).astype(m_ref.dtype)

    o = pl.BlockSpec((tm, tn), lambda i, j: (i, j))
    return pl.pallas_call(
        body, name=name, grid=(t // tm, d // tn),
        in_specs=[pl.BlockSpec((tm, ya.shape[1]), lambda i, j: (i, 0)),
                  pl.BlockSpec((wpa.shape[0], tn), lambda i, j: (0, j)),
                  pl.BlockSpec((tm, yd.shape[1]), lambda i, j: (i, 0)),
                  pl.BlockSpec((wpd.shape[0], tn), lambda i, j: (0, j)),
                  pl.BlockSpec((tm, tn), lambda i, j: (i, nb1 + j)),
                  pl.BlockSpec((tm, tn), lambda i, j: (i, nb2 + j))],
        out_specs=[o, o, o],
        out_shape=[jax.ShapeDtypeStruct((t, d), CD), jax.ShapeDtypeStruct((t, d), F32),
                   jax.ShapeDtypeStruct((t, d), F32)],
        compiler_params=_params(("parallel", "parallel")),
    )(ya, wpa, yd, wpd, z, z)


def merge_bwd(dm, pa, pd, z, col_m, name):
    t, d = dm.shape
    tm, tn = _tile(t, 512, 8), _tile(d, 256, LANE)
    nb1, nb2 = col_m // tn, (col_m + d) // tn

    def body(dm_ref, pa_ref, pd_ref, z1_ref, z2_ref, dpa_ref, dpd_ref, dz1_ref, dz2_ref):
        dmv = dm_ref[...]
        s1, s2 = jax.nn.sigmoid(z1_ref[...]), jax.nn.sigmoid(z2_ref[...])
        dpa_ref[...] = (dmv * s1).astype(dpa_ref.dtype)
        dpd_ref[...] = (dmv * s2).astype(dpd_ref.dtype)
        dz1_ref[...] = (dmv * pa_ref[...] * s1 * (1.0 - s1)).astype(dz1_ref.dtype)
        dz2_ref[...] = (dmv * pd_ref[...] * s2 * (1.0 - s2)).astype(dz2_ref.dtype)

    o = pl.BlockSpec((tm, tn), lambda i, j: (i, j))
    sh = jax.ShapeDtypeStruct((t, d), CD)
    return pl.pallas_call(
        body, name=name, grid=(t // tm, d // tn),
        in_specs=[o, o, o, pl.BlockSpec((tm, tn), lambda i, j: (i, nb1 + j)),
                  pl.BlockSpec((tm, tn), lambda i, j: (i, nb2 + j))],
        out_specs=[o, o, o, o], out_shape=[sh, sh, sh, sh],
        compiler_params=_params(("parallel", "parallel")),
    )(dm, pa, pd, z, z)


def ada_fwd(c_all, w, b, name):
    nl, d, n = w.shape
    tn = _tile(n, 384, LANE)

    def body(c_ref, w_ref, b_ref, o_ref):
        o_ref[0] = _nn(_silu(c_ref[...]), w_ref[0]) + b_ref[0]

    return pl.pallas_call(
        body, name=name, grid=(nl, n // tn),
        in_specs=[pl.BlockSpec(c_all.shape, lambda l, j: (0, 0)), pl.BlockSpec((1, d, tn), lambda l, j: (l, 0, j)),
                  pl.BlockSpec((1, 1, tn), lambda l, j: (l, 0, j))],
        out_specs=pl.BlockSpec((1, c_all.shape[0], tn), lambda l, j: (l, 0, j)),
        out_shape=jax.ShapeDtypeStruct((nl, c_all.shape[0], n), F32),
        compiler_params=_params(("parallel", "parallel")),
    )(c_all, w, b)


def ada_bwd(c_pad, dmod_pad, name):
    nl, kp, n = dmod_pad.shape
    d = c_pad.shape[1]
    tn = _tile(n, 384, LANE)

    def body(c_ref, g_ref, o_ref):
        o_ref[0] = _tn(_silu(c_ref[...]), g_ref[0])

    return pl.pallas_call(
        body, name=name, grid=(nl, n // tn),
        in_specs=[pl.BlockSpec((kp, d), lambda l, j: (0, 0)), pl.BlockSpec((1, kp, tn), lambda l, j: (l, 0, j))],
        out_specs=pl.BlockSpec((1, d, tn), lambda l, j: (l, 0, j)),
        out_shape=jax.ShapeDtypeStruct((nl, d, n), F32),
        compiler_params=_params(("parallel", "parallel")),
    )(c_pad, dmod_pad)


def loss_head(y, target, name):
    t, d = y.shape
    tr = _tile(t, 256, 8)

    def body(y_ref, t_ref, dy_ref, l_ref):
        err = y_ref[...] - t_ref[...]
        dy_ref[...] = err * (1.0 / d)
        part = jnp.sum(jnp.sum(err * err, axis=1, keepdims=True), axis=0, keepdims=True) * (0.5 / d)

        @pl.when(pl.program_id(0) == 0)
        def _():
            l_ref[...] = jnp.zeros_like(l_ref)

        l_ref[...] += part

    row = pl.BlockSpec((tr, d), lambda i: (i, 0))
    return pl.pallas_call(
        body, name=name, grid=(t // tr,), in_specs=[row, row],
        out_specs=[row, pl.BlockSpec((8, LANE), lambda i: (0, 0))],
        out_shape=[jax.ShapeDtypeStruct((t, d), F32), jax.ShapeDtypeStruct((8, LANE), F32)],
        compiler_params=_params(("arbitrary",)),
    )(y, target)


def adamw(w, m, v, g_slots, name):
    shape = w.shape
    nslot = g_slots.shape[0]
    if w.ndim == 2:
        w3, m3, v3, g4 = w[None], m[None], v[None], g_slots[:, None]
    else:
        w3, m3, v3, g4 = w, m, v, g_slots
    nl, r, c = w3.shape
    cg = g4.shape[3]
    tr = _tile(r, 256, 16)

    def body(w_ref, m_ref, v_ref, g_ref, go_ref, d_ref, mo_ref, vo_ref):
        g = g_ref[0, :, :, :c].astype(F32)
        for s in range(1, nslot):
            g = g + g_ref[s, :, :, :c].astype(F32)
        m_new = ADAM_B1 * m_ref[...] + (1.0 - ADAM_B1) * g
        v_new = ADAM_B2 * v_ref[...] + (1.0 - ADAM_B2) * jnp.square(g)
        m_hat = m_new / (1.0 - ADAM_B1 ** ADAM_STEP)
        v_hat = v_new / (1.0 - ADAM_B2 ** ADAM_STEP)
        go_ref[...] = g
        d_ref[...] = -ADAM_LR * (m_hat / (jnp.sqrt(v_hat) + ADAM_EPS) + ADAM_WD * w_ref[...])
        mo_ref[...] = m_new
        vo_ref[...] = v_new

    blk = pl.BlockSpec((1, tr, c), lambda l, i: (l, i, 0))
    gblk = pl.BlockSpec((nslot, 1, tr, cg), lambda l, i: (0, l, i, 0))
    sh = jax.ShapeDtypeStruct(w3.shape, F32)
    outs = pl.pallas_call(
        body, name=name, grid=(nl, r // tr), in_specs=[blk, blk, blk, gblk], out_specs=[blk] * 4,
        out_shape=[sh] * 4, compiler_params=_params(("parallel", "parallel")),
    )(w3, m3, v3, g4)
    return tuple(o.reshape(shape) for o in outs)


def _rows(flat, unit=16):
    n = flat.shape[0]
    per = 1024 * unit
    pad = (-n) % per
    if pad:
        flat = jnp.concatenate([flat, jnp.zeros((pad,), flat.dtype)])
    return flat.reshape(-1, 1024)


def _to_classes(a, t):
    outs = []
    for gi, dil in enumerate(DILATIONS):
        x = a[:, gi]
        x = x.reshape(t // dil, dil, *x.shape[1:]).transpose(2, 1, 0, 3)
        outs.append(x.reshape(x.shape[0], t, x.shape[3]))
    return jnp.stack(outs)


def _from_classes(a, t):
    outs = []
    for gi, dil in enumerate(DILATIONS):
        x = a[gi]
        x = x.reshape(x.shape[0], dil, t // dil, x.shape[2]).transpose(2, 1, 0, 3)
        outs.append(x.reshape(t, x.shape[2], x.shape[3]))
    return jnp.stack(outs, axis=1)


def _unshard_cols(g):
    return g.transpose(1, 0, 2).reshape(g.shape[1], -1)


def _shard_cols(full):
    r = full.shape[0]
    return full.reshape(r, N_DEV, -1).transpose(1, 0, 2)


def kernel(x, c, ada_w, ada_b, norm_ff1, ffn1_w_up, ffn1_w_down, norm_mix, w_in, q_norm, k_norm, conv_w, a_log, dt_bias, dn_norm, w_proj_att, w_proj_dn, w_out, norm_ff2, ffn2_w_up, ffn2_w_down, loss_target, m_ada_w, m_ada_b, m_norm_ff1, m_ffn1_w_up, m_ffn1_w_down, m_norm_mix, m_w_in, m_q_norm, m_k_norm, m_conv_w, m_a_log, m_dt_bias, m_dn_norm, m_w_proj_att, m_w_proj_dn, m_w_out, m_norm_ff2, m_ffn2_w_up, m_ffn2_w_down, v_ada_w, v_ada_b, v_norm_ff1, v_ffn1_w_up, v_ffn1_w_down, v_norm_mix, v_w_in, v_q_norm, v_k_norm, v_conv_w, v_a_log, v_dt_bias, v_dn_norm, v_w_proj_att, v_w_proj_dn, v_w_out, v_norm_ff2, v_ffn2_w_up, v_ffn2_w_down):
    nl = ada_w.shape[0]
    t, d = x.shape[1], x.shape[2]
    dff = ffn1_w_down.shape[1] * N_DEV
    ha = d // 256
    ng = len(DILATIONS)
    att_w = ng * ha * ATT_HEAD_DIM
    nh = d // DN_HEAD_DIM
    dn_w = nh * DN_HEAD_DIM
    n_in = w_in.shape[2] * N_DEV
    off_dn, off_gate = 3 * att_w, 3 * att_w + 3 * dn_w
    off_a = off_gate + dn_w
    off_merge = off_a + 2 * nh
    assert off_merge + 2 * d == n_in
    col_dn, col_gate, col_m = 0, 3 * dn_w, 4 * dn_w
    col_att = col_m + 2 * d
    col_ab = col_att + 3 * att_w
    zw = col_ab + 2 * LANE
    me = 4 * lax.axis_index("x") + 2 * lax.axis_index("y") + lax.axis_index("c")
    xs = x[0]
    target = loss_target[0]

    conv_rows = _rows(conv_w.reshape(-1), 8)
    pack0 = jnp.concatenate([jnp.concatenate([c, jnp.zeros((7, d), F32)]).reshape(-1), conv_rows.reshape(-1)])
    pack0 = _rows(pack0, 8)
    g0 = all_gather([pack0], "ag_c_conv")[0].reshape(N_DEV, -1)
    c_all = g0[:, :d]
    cw = g0[:, 8 * d:8 * d + conv_w.size].reshape(N_DEV, nl, CONV_WIDTH, -1)
    conv_full = cw.transpose(1, 2, 0, 3).reshape(nl, CONV_WIDTH, 3 * dn_w)

    n_ada = ada_w.shape[2]
    b_mine = lax.dynamic_slice_in_dim(ada_b, me * n_ada, n_ada, axis=1)[:, None, :]
    mod_s = ada_fwd(c_all, ada_w, b_mine, "ada_fwd")
    gm = all_gather([mod_s], "ag_mod")[0]
    mod = lax.dynamic_index_in_dim(gm, me, axis=2, keepdims=False)
    mod = mod.transpose(1, 0, 2).reshape(nl, N_ADA, 1, d)

    kinds = [ffn1_w_up, ffn1_w_down, w_in, w_proj_att, w_proj_dn, w_out, ffn2_w_up, ffn2_w_down]
    c_up = ffn1_w_up.shape[2]
    cp = -(-c_up // LANE) * LANE
    r_dn = ffn1_w_down.shape[1]
    assert 2 * r_dn == c_up

    def pad_up(w):
        return jnp.pad(w.astype(CD), ((0, 0), (0, 0), (0, cp - c_up)))

    gathered = all_gather([pad_up(ffn1_w_up), ffn1_w_down.astype(CD), w_in.astype(CD), w_proj_att.astype(CD),
                           w_proj_dn.astype(CD), w_out.astype(CD), pad_up(ffn2_w_up), ffn2_w_down.astype(CD)],
                          "ag_weights")

    def full_weight(ki, l):
        blk = gathered[ki][:, l]
        if ki in (2, 3):
            return _unshard_cols(blk)
        if ki in (1, 7):
            pairs = blk.reshape(HALF, c_up, d)
            return jnp.pad(pairs, ((0, 0), (0, cp - c_up), (0, 0))).reshape(HALF * cp, d)
        return blk.reshape(-1, blk.shape[2])

    def down_grad_slots(g):
        return g.reshape(HALF, cp, d)[:, :c_up].reshape(N_DEV, r_dn, d)

    def w_in_cols(w):
        pad = jnp.zeros((w.shape[0], zw - n_in), w.dtype)
        return jnp.concatenate([w[:, off_dn:off_a], w[:, off_merge:], w[:, :off_dn], w[:, off_a:off_merge], pad], axis=1)

    def w_in_cols_inv(g):
        return jnp.concatenate([g[:, col_att:col_ab], g[:, :col_m], g[:, col_ab:col_ab + 2 * nh], g[:, col_m:col_att]], axis=1)

    saved = []
    xc = xs
    for l in range(nl):
        sv = {}
        sh1, sc1, gt1, sh2, sc2, gt2, sh3, sc3, gt3 = [mod[l, i] for i in range(N_ADA)]
        w_dn1, w_dn2 = full_weight(1, l), full_weight(7, l)
        win = w_in_cols(full_weight(2, l))
        wpa, wpd, wo = full_weight(3, l), full_weight(4, l), full_weight(5, l)
        sv["w"] = (w_dn1, w_dn2, win, wpa, wpd, wo)

        def ffn(xin, g, sh, sc, gt, u_all, w_dn):
            h = norm_mod(xin, g, sc, sh, "norm_mod")
            gate, up, a = ffn_up(h, u_all, l, "ffn_up")
            xo, f = matmul([(a, w_dn)], "nn", F32, "ffn_down", out_scale=0.5 * gt, resid=xin, save_acc=True)
            return xo, (xin, h, gate, up, a, f)

        xc, sv["ffn1"] = ffn(xc, norm_ff1[l:l + 1], sh1, sc1, gt1, gathered[0], w_dn1)

        x_mix = xc
        h2 = norm_mod(x_mix, norm_mix[l:l + 1], sc2, sh2, "norm_mod")
        z = matmul([(h2, win)], "nn", F32, "w_in", tn=512)
        qkv = z[:, col_att:col_ab].reshape(t, 3, ng, ha, ATT_HEAD_DIM)
        qp, kp, vp = (_to_classes(qkv[:, i], t) for i in range(3))
        qg, kg = q_norm[l:l + 1], k_norm[l:l + 1]
        o_cls, lse_cls = att_fwd(qp, kp, vp, qg, kg, "att_fwd")
        o_tok = _from_classes(o_cls, t).transpose(1, 2, 0, 3)
        lse_tok = _from_classes(lse_cls, t).transpose(1, 2, 0, 3)
        ya_h = att_combine(o_tok, lse_tok, "att_combine")
        ya = ya_h.transpose(1, 0, 2).reshape(t, ha * ATT_HEAD_DIM).astype(CD)
        cvo = conv_fwd(z, col_dn, conv_full[l], "conv_fwd")
        al3, dt3 = a_log[l].reshape(nh, 1, 1), dt_bias[l].reshape(nh, 1, 1)
        gn = dn_norm[l:l + 1]
        yd, states = dn_fwd(cvo, z, col_gate, col_ab, al3, dt3, gn, "dn_fwd")
        mrg, pa, pd = merge_fwd(ya, wpa, yd, wpd, z, col_m, "merge_fwd")
        xc, f2 = matmul([(mrg, wo)], "nn", F32, "w_out", out_scale=gt2, resid=x_mix, save_acc=True)
        sv["mix"] = (x_mix, h2, z, qp, kp, vp, o_tok, lse_tok, ya, cvo, yd, states, mrg, pa, pd, f2)

        xc, sv["ffn2"] = ffn(xc, norm_ff2[l:l + 1], sh3, sc3, gt3, gathered[6], w_dn2)
        saved.append(sv)

    dxc, loss_blk = loss_head(xc, target, "loss_head")

    gbig = [[None] * nl for _ in kinds]
    dmods, small = [], []
    for l in reversed(range(nl)):
        sv = saved[l]
        sh1, sc1, gt1, sh2, sc2, gt2, sh3, sc3, gt3 = [mod[l, i] for i in range(N_ADA)]
        w_dn1, w_dn2, win, wpa, wpd, wo = sv["w"]

        def ffn_bwd(dxo, g, sh, sc, gt, u_all, w_dn, sv_f):
            xin, h, gate, up, a, f = sv_f
            s = 0.5 * gt
            g_dn = down_grad_slots(matmul([(a, dxo)], "tn", CD, "ffn_down_wg", out_scale=s))
            dgate, dup = ffn_dact(dxo, s, w_dn, gate, up, "ffn_dact")
            g_up = ffn_up_wg(h, dgate, dup, cp, "ffn_up_wg")
            dh = ffn_up_dg(dgate, dup, u_all, l, "ffn_up_dg")
            dx, dg, dsc, dsh, dgt = norm_mod_bwd(xin, g, sc, sh, dh, dxo, f, 0.5, "norm_mod_bwd")
            return dx, g_up, g_dn, (dg, dsc, dsh, dgt)

        dxc, gbig[6][l], gbig[7][l], (dg3, dsc3, dsh3, dgt3) = ffn_bwd(
            dxc, norm_ff2[l:l + 1], sh3, sc3, gt3, gathered[6], w_dn2, sv["ffn2"])

        x_mix, h2, z, qp, kp, vp, o_tok, lse_tok, ya, cvo, yd, states, mrg, pa, pd, f2 = sv["mix"]
        gbig[5][l] = matmul([(mrg, dxc)], "tn", CD, "w_out_wg", out_scale=gt2)
        dm = matmul([(dxc, wo)], "nt", F32, "w_out_dg", a_scale=gt2)
        dpa, dpd, dz1, dz2 = merge_bwd(dm, pa, pd, z, col_m, "merge_bwd")
        gbig[3][l] = matmul([(ya, dpa)], "tn", CD, "patt_wg")
        gbig[4][l] = matmul([(yd, dpd)], "tn", CD, "pdn_wg")
        dya = matmul([(dpa, wpa)], "nt", F32, "patt_dg")
        dyd = matmul([(dpd, wpd)], "nt", F32, "pdn_dg")
        dya_h = dya.reshape(t, ha, ATT_HEAD_DIM).transpose(1, 0, 2)
        do_tok, dlse_tok = att_combine_bwd(o_tok, lse_tok, dya_h, "att_combine_bwd")
        do_cls = _to_classes(do_tok.transpose(2, 0, 1, 3), t)
        dlse_cls = _to_classes(dlse_tok.transpose(2, 0, 1, 3), t)
        qg, kg = q_norm[l:l + 1], k_norm[l:l + 1]
        dq, dk, dv, dqg, dkg = att_bwd(qp, kp, vp, qg, kg, do_cls, dlse_cls, "att_bwd")
        dz_att = jnp.stack([_from_classes(a_, t) for a_ in (dq, dk, dv)], axis=1).reshape(t, 3 * att_w).astype(CD)
        al3, dt3 = a_log[l].reshape(nh, 1, 1), dt_bias[l].reshape(nh, 1, 1)
        gn = dn_norm[l:l + 1]
        dcvo, dz_gate, dz_ab, dal, ddt, dgn = dn_bwd(
            cvo, z, col_gate, col_ab, al3, dt3, gn, states, dyd, "dn_bwd")
        dz_dn, dconv = conv_bwd(dcvo, z, col_dn, conv_full[l], "conv_bwd")
        dz = jnp.concatenate([dz_dn, dz_gate, dz1, dz2, dz_att, dz_ab, jnp.zeros((t, LANE), CD)], axis=1)
        gbig[2][l] = w_in_cols_inv(matmul([(h2, dz)], "tn", CD, "w_in_wg", tn=512))
        dh2 = matmul([(dz, win)], "nt", F32, "w_in_dg", tk=zw // 4 if (zw // 4) % LANE == 0 else zw)
        dxc, dg2, dsc2, dsh2, dgt2 = norm_mod_bwd(x_mix, norm_mix[l:l + 1], sc2, sh2, dh2, dxc, f2, 1.0, "norm_mod_bwd_mix")

        dxc, gbig[0][l], gbig[1][l], (dg1, dsc1, dsh1, dgt1) = ffn_bwd(
            dxc, norm_ff1[l:l + 1], sh1, sc1, gt1, gathered[0], w_dn1, sv["ffn1"])

        dmods.append(jnp.concatenate([dsh1, dsc1, dgt1, dsh2, dsc2, dgt2, dsh3, dsc3, dgt3], axis=1))
        small.append((dg1, dg2, dg3, dconv, dqg, dkg, dal.reshape(1, nh), ddt.reshape(1, nh), dgn))
    dmods.reverse()
    small.reverse()

    def slots(ki, g):
        if ki in (0, 1, 6, 7):
            return g
        return _shard_cols(g) if ki in (2, 3) else g.reshape(N_DEV, -1, g.shape[1])

    send = [jnp.stack([slots(ki, gbig[ki][l]) for l in range(nl)], axis=1) for ki in range(len(kinds))]
    recv = all_to_all(send, "a2a_grads")

    fields = [jnp.stack(dmods).reshape(-1)]
    fields += [jnp.stack([s[i] for s in small]).reshape(-1) for i in range(9)]
    fields.append(loss_blk[0, :1])
    fsizes = [f.size for f in fields]
    foffs = [sum(fsizes[:i]) for i in range(len(fields))]
    g1 = all_gather([_rows(jnp.concatenate(fields), 8)], "ag_small")[0].reshape(N_DEV, -1)

    def field(i, shape):
        return g1[:, foffs[i]:foffs[i] + fsizes[i]].reshape(N_DEV, *shape)

    loss = field(10, (1,))[0, 0]
    for j in range(1, N_DEV):
        loss = loss + field(10, (1,))[j, 0]

    results = {}
    dmod_all = field(0, (nl, N_ADA * d))
    c_pad = jnp.concatenate([c_all, jnp.zeros((8, d), F32)])
    dmod_mine = lax.dynamic_slice_in_dim(dmod_all, me * n_ada, n_ada, axis=2).transpose(1, 0, 2)
    dmod_pad = jnp.concatenate([dmod_mine, jnp.zeros((nl, 8, n_ada), F32)], axis=1)
    g_ada_w = ada_bwd(c_pad, dmod_pad, "ada_bwd")
    results["ada_w"] = adamw(ada_w, m_ada_w, v_ada_w, g_ada_w[None], "adamw_ada_w")
    results["ada_b"] = adamw(ada_b, m_ada_b, v_ada_b, dmod_all, "adamw_ada_b")
    results["norm_ff1"] = adamw(norm_ff1, m_norm_ff1, v_norm_ff1, field(1, (nl, d)), "adamw_norm_ff1")
    results["norm_mix"] = adamw(norm_mix, m_norm_mix, v_norm_mix, field(2, (nl, d)), "adamw_norm_mix")
    results["norm_ff2"] = adamw(norm_ff2, m_norm_ff2, v_norm_ff2, field(3, (nl, d)), "adamw_norm_ff2")
    conv_slots = lax.dynamic_slice_in_dim(field(4, (nl, CONV_WIDTH, 3 * dn_w)), me * conv_w.shape[2],
                                          conv_w.shape[2], axis=3)
    results["conv_w"] = adamw(conv_w, m_conv_w, v_conv_w, conv_slots, "adamw_conv_w")
    results["q_norm"] = adamw(q_norm, m_q_norm, v_q_norm, field(5, (nl, ATT_HEAD_DIM)), "adamw_q_norm")
    results["k_norm"] = adamw(k_norm, m_k_norm, v_k_norm, field(6, (nl, ATT_HEAD_DIM)), "adamw_k_norm")
    results["a_log"] = adamw(a_log, m_a_log, v_a_log, field(7, (nl, nh)), "adamw_a_log")
    results["dt_bias"] = adamw(dt_bias, m_dt_bias, v_dt_bias, field(8, (nl, nh)), "adamw_dt_bias")
    results["dn_norm"] = adamw(dn_norm, m_dn_norm, v_dn_norm, field(9, (nl, DN_HEAD_DIM)), "adamw_dn_norm")
    big_names = ["ffn1_w_up", "ffn1_w_down", "w_in", "w_proj_att", "w_proj_dn", "w_out", "ffn2_w_up", "ffn2_w_down"]
    big_m = [m_ffn1_w_up, m_ffn1_w_down, m_w_in, m_w_proj_att, m_w_proj_dn, m_w_out, m_ffn2_w_up, m_ffn2_w_down]
    big_v = [v_ffn1_w_up, v_ffn1_w_down, v_w_in, v_w_proj_att, v_w_proj_dn, v_w_out, v_ffn2_w_up, v_ffn2_w_down]
    for ki, nm in enumerate(big_names):
        results[nm] = adamw(kinds[ki], big_m[ki], big_v[ki], recv[ki], f"adamw_{nm}")

    order = ["ada_w", "ada_b", "norm_ff1", "ffn1_w_up", "ffn1_w_down", "norm_mix", "w_in", "q_norm", "k_norm",
             "conv_w", "a_log", "dt_bias", "dn_norm", "w_proj_att", "w_proj_dn", "w_out", "norm_ff2",
             "ffn2_w_up", "ffn2_w_down"]
    outs = [loss, dxc[None]]
    for part in range(4):
        outs += [results[n][part] for n in order]
    return tuple(outs)
```

```python
import functools

import jax
import jax.numpy as jnp
from jax import lax
from jax.experimental import pallas as pl
from jax.experimental.pallas import tpu as pltpu

F32 = jnp.float32
CD = jnp.bfloat16
EPS = 1e-6
N_DEV = 8
LANE = 128
ATT_HEAD_DIM = 64
ATT_BLOCK = 128
DILATIONS = (1, 4, 16)
DN_HEAD_DIM = 128
DN_CHUNK = 64
CONV_WIDTH = 4
N_ADA = 9
ADAM_LR, ADAM_B1, ADAM_B2, ADAM_EPS, ADAM_WD, ADAM_STEP = 0.001, 0.9, 0.999, 1e-08, 0.01, 10
VMEM_LIMIT = 56 * 1024 * 1024
NEG = -1e30
MESH = pl.DeviceIdType.MESH
HI = lax.Precision.HIGHEST


def _params(sem=None):
    return pltpu.CompilerParams(dimension_semantics=sem, vmem_limit_bytes=VMEM_LIMIT)


def _tile(n, pref, unit):
    best = None
    t = unit
    while t <= min(n, pref):
        if n % t == 0:
            best = t
        t += unit
    return best if best is not None else n


def _silu(x):
    return x * jax.nn.sigmoid(x)


def _dot(a, b, dims, precision=None):
    if precision is None:
        a, b = a.astype(CD), b.astype(CD)
    return lax.dot_general(a, b, (dims, ((), ())), precision=precision, preferred_element_type=F32)


def _nn(a, b, precision=None):
    return _dot(a, b, ((1,), (0,)), precision)


def _nt(a, b, precision=None):
    return _dot(a, b, ((1,), (1,)), precision)


def _tn(a, b, precision=None):
    return _dot(a, b, ((0,), (0,)), precision)


def all_gather(arrs, name):
    n = len(arrs)

    def body(*refs):
        x_refs, out_refs = refs[:n], refs[n:2 * n]
        send_sems, recv_sems, local_sems = refs[2 * n:]
        x, y, c = lax.axis_index("x"), lax.axis_index("y"), lax.axis_index("c")
        me, sibling = (x, y, c), (x, y, 1 - c)
        chips = [(1 - x, y), (x, 1 - y), (1 - x, 1 - y)]

        def copy(a, k, block, to, src=None):
            slot = out_refs[a].at[4 * block[0] + 2 * block[1] + block[2]]
            return pltpu.make_async_remote_copy(
                src_ref=slot if src is None else src, dst_ref=slot,
                send_sem=send_sems.at[7 * a + k], recv_sem=recv_sems.at[7 * a + k],
                device_id=to, device_id_type=MESH)

        mine, first, passed = [], [], []
        for a in range(n):
            cp = pltpu.make_async_copy(x_refs[a], out_refs[a].at[4 * x + 2 * y + c], local_sems.at[a])
            cp.start()
            mine.append(cp)
            first.append(copy(a, 0, me, sibling, src=x_refs[a]))
            first += [copy(a, 1 + j, me, (*chip, c), src=x_refs[a]) for j, chip in enumerate(chips)]
        for cp in first:
            cp.start()
        for j, chip in enumerate(chips):
            for a in range(n):
                copy(a, 1 + j, (*chip, c), me).wait_recv()
                cp = copy(a, 4 + j, (*chip, c), sibling)
                cp.start()
                passed.append(cp)
        for a in range(n):
            copy(a, 0, sibling, me).wait_recv()
            for j, chip in enumerate(chips):
                copy(a, 4 + j, (*chip, 1 - c), me).wait_recv()
        for cp in first + passed:
            cp.wait_send()
        for cp in mine:
            cp.wait()

    hbm = pl.BlockSpec(memory_space=pl.ANY)
    outs = pl.pallas_call(
        body, name=name,
        out_shape=[jax.ShapeDtypeStruct((N_DEV,) + a.shape, a.dtype) for a in arrs],
        in_specs=[hbm] * n, out_specs=[hbm] * n,
        scratch_shapes=[pltpu.SemaphoreType.DMA((7 * n,)), pltpu.SemaphoreType.DMA((7 * n,)),
                        pltpu.SemaphoreType.DMA((n,))],
    )(*arrs)
    return list(outs)


def all_to_all(arrs, name):
    n = len(arrs)

    def body(*refs):
        x_refs, out_refs = refs[:n], refs[n:2 * n]
        send_sems, recv_sems, local_sems = refs[2 * n:]
        x, y, c = lax.axis_index("x"), lax.axis_index("y"), lax.axis_index("c")
        me = 4 * x + 2 * y + c
        mine, sends = [], []
        for a in range(n):
            cp = pltpu.make_async_copy(x_refs[a].at[me], out_refs[a].at[me], local_sems.at[a])
            cp.start()
            mine.append(cp)
        peers = []
        for k in range(1, N_DEV):
            px, py, pc = x ^ (k >> 2), y ^ ((k >> 1) & 1), c ^ (k & 1)
            peers.append((px, py, pc, 4 * px + 2 * py + pc))
        for k in range(1, N_DEV):
            px, py, pc, pidx = peers[k - 1]
            for a in range(n):
                cp = pltpu.make_async_remote_copy(
                    src_ref=x_refs[a].at[pidx], dst_ref=out_refs[a].at[me],
                    send_sem=send_sems.at[7 * a + k - 1], recv_sem=recv_sems.at[7 * a + k - 1],
                    device_id=(px, py, pc), device_id_type=MESH)
                cp.start()
                sends.append(cp)
        for k in range(1, N_DEV):
            pidx = peers[k - 1][3]
            for a in range(n):
                pltpu.make_async_remote_copy(
                    src_ref=x_refs[a].at[pidx], dst_ref=out_refs[a].at[pidx],
                    send_sem=send_sems.at[7 * a + k - 1], recv_sem=recv_sems.at[7 * a + k - 1],
                    device_id=(x, y, c), device_id_type=MESH).wait_recv()
        for cp in sends:
            cp.wait_send()
        for cp in mine:
            cp.wait()

    hbm = pl.BlockSpec(memory_space=pl.ANY)
    outs = pl.pallas_call(
        body, name=name,
        out_shape=[jax.ShapeDtypeStruct(a.shape, a.dtype) for a in arrs],
        in_specs=[hbm] * n, out_specs=[hbm] * n,
        scratch_shapes=[pltpu.SemaphoreType.DMA((7 * n,)), pltpu.SemaphoreType.DMA((7 * n,)),
                        pltpu.SemaphoreType.DMA((n,))],
    )(*arrs)
    return list(outs)


_HBM = pl.BlockSpec(memory_space=pltpu.HBM)
_SEM = pl.BlockSpec(memory_space=pltpu.SEMAPHORE)
_EFFECT = pltpu.SideEffectType.DATAFLOW_SIDE_EFFECTING


def _exchange_copies(src_refs, land_refs, send_sem, recv_sem, gather):
    x, y, c = lax.axis_index("x"), lax.axis_index("y"), lax.axis_index("c")
    me = 4 * x + 2 * y + c
    pairs = []
    for k in range(1, N_DEV):
        px, py, pc = x ^ (k >> 2), y ^ ((k >> 1) & 1), c ^ (k & 1)
        pidx = 4 * px + 2 * py + pc
        for src, land in zip(src_refs, land_refs):
            mine = src if gather else src.at[pidx]
            out = pltpu.make_async_remote_copy(src_ref=mine, dst_ref=land.at[me], send_sem=send_sem, recv_sem=recv_sem,
                                               device_id=(px, py, pc), device_id_type=MESH)
            inc = pltpu.make_async_remote_copy(src_ref=mine, dst_ref=land.at[pidx], send_sem=send_sem, recv_sem=recv_sem,
                                               device_id=(px, py, pc), device_id_type=MESH)
            pairs.append((out, inc))
    return pairs


def exchange_start(srcs, lands, after, gather, name):
    n = len(srcs)

    def body(*refs):
        src_refs, land_refs = refs[:n], refs[n:2 * n]
        send_sem, recv_sem = refs[2 * n + 1], refs[2 * n + 2]
        token = refs[-1]
        for out, _ in _exchange_copies(src_refs, land_refs, send_sem, recv_sem, gather):
            out.start()
        token[...] = jnp.zeros_like(token)

    res = pl.pallas_call(
        body, name=name,
        out_shape=(pltpu.SemaphoreType.DMA(()), pltpu.SemaphoreType.DMA(()),
                   *[pltpu.HBM(a.shape, a.dtype) for a in srcs], *[pltpu.HBM(a.shape, a.dtype) for a in lands],
                   jax.ShapeDtypeStruct((8, LANE), F32)),
        in_specs=[_HBM] * (2 * n) + [pl.BlockSpec(memory_space=pl.ANY)],
        out_specs=(_SEM, _SEM, *[_HBM] * (2 * n), pl.BlockSpec(memory_space=pltpu.VMEM)),
        input_output_aliases={i: 2 + i for i in range(2 * n)},
        compiler_params=pltpu.CompilerParams(has_side_effects=_EFFECT),
    )(*[pltpu.with_memory_space_constraint(a, pltpu.HBM) for a in list(srcs) + list(lands)], after)
    return res[0], res[1], list(res[2:2 + n]), list(res[2 + n:2 + 2 * n]), res[-1]


def exchange_wait(send_sem, recv_sem, srcs, lands, after, gather, name):
    n = len(srcs)

    def body(*refs):
        src_refs, land_refs = refs[:n], refs[n:2 * n]
        s_sem, r_sem = refs[2 * n], refs[2 * n + 1]
        for out, inc in _exchange_copies(src_refs, land_refs, s_sem, r_sem, gather):
            out.wait_send()
            inc.wait_recv()

    res = pl.pallas_call(
        body, name=name,
        out_shape=[pltpu.HBM(a.shape, a.dtype) for a in list(srcs) + list(lands)],
        in_specs=[_HBM] * (2 * n) + [_SEM, _SEM, pl.BlockSpec(memory_space=pl.ANY)],
        out_specs=[_HBM] * (2 * n),
        input_output_aliases={i: i for i in range(2 * n)},
        compiler_params=pltpu.CompilerParams(has_side_effects=_EFFECT),
    )(*srcs, *lands, send_sem, recv_sem, after)
    return list(res[n:])


def matmul(pairs, mode, out_dtype, name, *, a_scale=None, out_scale=None, resid=None,
           save_acc=False, tm=512, tn=512, tk=2048):
    a0, b0 = pairs[0]
    if mode == "nn":
        (m, kdim), n = a0.shape, b0.shape[1]
    elif mode == "nt":
        (m, kdim), n = a0.shape, b0.shape[0]
    else:
        (kdim, m), n = a0.shape, b0.shape[1]
    tm, tn = _tile(m, tm, 8 if m % LANE else LANE), _tile(n, tn, LANE)
    tk = _tile(kdim, tk, LANE)
    nk = kdim // tk
    npairs = len(pairs)
    dims = {"nn": ((1,), (0,)), "nt": ((1,), (1,)), "tn": ((0,), (0,))}[mode]

    if mode == "nn":
        a_spec = pl.BlockSpec((tm, tk), lambda i, j, k: (i, k))
        b_spec = pl.BlockSpec((tk, tn), lambda i, j, k: (k, j))
    elif mode == "nt":
        a_spec = pl.BlockSpec((tm, tk), lambda i, j, k: (i, k))
        b_spec = pl.BlockSpec((tn, tk), lambda i, j, k: (j, k))
    else:
        a_spec = pl.BlockSpec((tk, tm), lambda i, j, k: (k, i))
        b_spec = pl.BlockSpec((tk, tn), lambda i, j, k: (k, j))

    def body(*refs):
        it = iter(refs)
        pair_refs = [(next(it), next(it)) for _ in range(npairs)]
        as_ref = next(it) if a_scale is not None else None
        os_ref = next(it) if out_scale is not None else None
        rs_ref = next(it) if resid is not None else None
        o_ref = next(it)
        acc_out = next(it) if save_acc else None
        acc_ref = next(it) if nk > 1 else None
        part = None
        for a_ref, b_ref in pair_refs:
            a = a_ref[...]
            if as_ref is not None:
                a = a.astype(F32) * (as_ref[...] if mode != "tn" else as_ref[...].reshape(tk, 1))
            d = _dot(a, b_ref[...], dims)
            part = d if part is None else part + d

        def finish(acc):
            if acc_out is not None:
                acc_out[...] = acc
            if os_ref is not None:
                acc = acc * os_ref[...]
            if rs_ref is not None:
                acc = rs_ref[...] + acc
            o_ref[...] = acc.astype(o_ref.dtype)

        if nk == 1:
            finish(part)
        else:
            k = pl.program_id(2)

            @pl.when(k == 0)
            def _():
                acc_ref[...] = part

            @pl.when(k > 0)
            def _():
                acc_ref[...] += part

            @pl.when(k == nk - 1)
            def _():
                finish(acc_ref[...])

    in_specs, args = [], []
    for a, b in pairs:
        in_specs += [a_spec, b_spec]
        args += [a, b]
    if a_scale is not None:
        assert mode != "tn"
        in_specs.append(pl.BlockSpec((1, tk), lambda i, j, k: (0, k)))
        args.append(a_scale)
    if out_scale is not None:
        in_specs.append(pl.BlockSpec((1, tn), lambda i, j, k: (0, j)))
        args.append(out_scale)
    if resid is not None:
        in_specs.append(pl.BlockSpec((tm, tn), lambda i, j, k: (i, j)))
        args.append(resid)
    o_spec = pl.BlockSpec((tm, tn), lambda i, j, k: (i, j))
    out_shape = [jax.ShapeDtypeStruct((m, n), out_dtype)]
    out_specs = [o_spec]
    if save_acc:
        out_shape.append(jax.ShapeDtypeStruct((m, n), F32))
        out_specs.append(o_spec)
    scratch = [pltpu.VMEM((tm, tn), F32)] if nk > 1 else []
    res = pl.pallas_call(
        body, name=name, grid=(m // tm, n // tn, nk),
        in_specs=in_specs, out_specs=out_specs, out_shape=out_shape, scratch_shapes=scratch,
        compiler_params=_params(("parallel", "parallel", "arbitrary")),
    )(*args)
    return res if save_acc else res[0]


def _nm_fn(x, g, sc, sh):
    r = lax.rsqrt(jnp.mean(x * x, axis=-1, keepdims=True) + EPS)
    return (x * r * g) * (1.0 + sc) + sh


def norm_mod(x, g, sc, sh, name):
    t, d = x.shape
    tr = _tile(t, 256, 8)

    def body(x_ref, g_ref, sc_ref, sh_ref, h_ref):
        h_ref[...] = _nm_fn(x_ref[...], g_ref[...], sc_ref[...], sh_ref[...]).astype(h_ref.dtype)

    row = pl.BlockSpec((tr, d), lambda i: (i, 0))
    vec = pl.BlockSpec((1, d), lambda i: (0, 0))
    return pl.pallas_call(
        body, name=name, grid=(t // tr,), in_specs=[row, vec, vec, vec], out_specs=row,
        out_shape=jax.ShapeDtypeStruct((t, d), CD), compiler_params=_params(("parallel",)),
    )(x, g, sc, sh)


def norm_mod_bwd(x, g, sc, sh, dh, dxo, f, gate_scale, name):
    t, d = x.shape
    tr = _tile(t, 256, 8)

    def body(x_ref, g_ref, sc_ref, sh_ref, dh_ref, dxo_ref, f_ref, dx_ref, dg_ref, dsc_ref, dsh_ref, dgt_ref):
        _, vjp = jax.vjp(_nm_fn, x_ref[...], g_ref[...], sc_ref[...], sh_ref[...])
        dx, dg, dsc, dsh = vjp(dh_ref[...])
        dxo_v = dxo_ref[...]
        dx_ref[...] = dxo_v + dx
        dgt = gate_scale * jnp.sum(f_ref[...] * dxo_v, axis=0, keepdims=True)

        @pl.when(pl.program_id(0) == 0)
        def _():
            dg_ref[...] = dg
            dsc_ref[...] = dsc
            dsh_ref[...] = dsh
            dgt_ref[...] = dgt

        @pl.when(pl.program_id(0) > 0)
        def _():
            dg_ref[...] += dg
            dsc_ref[...] += dsc
            dsh_ref[...] += dsh
            dgt_ref[...] += dgt

    row = pl.BlockSpec((tr, d), lambda i: (i, 0))
    vec = pl.BlockSpec((1, d), lambda i: (0, 0))
    vshape = jax.ShapeDtypeStruct((1, d), F32)
    return pl.pallas_call(
        body, name=name, grid=(t // tr,), in_specs=[row, vec, vec, vec, row, row, row],
        out_specs=[row, vec, vec, vec, vec],
        out_shape=[jax.ShapeDtypeStruct((t, d), F32), vshape, vshape, vshape, vshape],
        compiler_params=_params(("arbitrary",)),
    )(x, g, sc, sh, dh, dxo, f)


HALF = N_DEV // 2


def ffn_up(h, u_all, name):
    t, d = h.shape
    cp = u_all.shape[2]
    f = HALF * cp
    tm, tn = _tile(t, 1024, LANE), _tile(cp, 384, LANE)
    per = cp // tn

    def body(h_ref, wg_ref, wu_ref, g_ref, u_ref, a_ref):
        hv = h_ref[...]
        gate = _nn(hv, wg_ref[0])
        up = _nn(hv, wu_ref[0])
        g_ref[...] = gate.astype(g_ref.dtype)
        u_ref[...] = up.astype(u_ref.dtype)
        a_ref[...] = (_silu(gate) * up).astype(a_ref.dtype)

    o = pl.BlockSpec((tm, tn), lambda i, j: (i, j))
    wg = pl.BlockSpec((1, d, tn), lambda i, j: (j // per, 0, j % per))
    wu = pl.BlockSpec((1, d, tn), lambda i, j: (HALF + j // per, 0, j % per))
    return pl.pallas_call(
        body, name=name, grid=(t // tm, f // tn),
        in_specs=[pl.BlockSpec((tm, d), lambda i, j: (i, 0)), wg, wu], out_specs=[o, o, o],
        out_shape=[jax.ShapeDtypeStruct((t, f), CD)] * 3,
        compiler_params=_params(("parallel", "parallel")),
    )(h, u_all, u_all)


def ffn_up_wg(h, dgate, dup, cp, name):
    t, d = h.shape
    tm = _tile(d, 512, LANE)

    def body(h_ref, dg_ref, du_ref, o_ref):
        s = pl.program_id(0)

        @pl.when(s < HALF)
        def _():
            o_ref[0] = _tn(h_ref[...], dg_ref[...]).astype(o_ref.dtype)

        @pl.when(s >= HALF)
        def _():
            o_ref[0] = _tn(h_ref[...], du_ref[...]).astype(o_ref.dtype)

    return pl.pallas_call(
        body, name=name, grid=(N_DEV, d // tm),
        in_specs=[pl.BlockSpec((t, tm), lambda s, i: (0, i)),
                  pl.BlockSpec((t, cp), lambda s, i: (0, jnp.minimum(s, HALF - 1))),
                  pl.BlockSpec((t, cp), lambda s, i: (0, jnp.maximum(s - HALF, 0)))],
        out_specs=pl.BlockSpec((1, tm, cp), lambda s, i: (s, i, 0)),
        out_shape=jax.ShapeDtypeStruct((N_DEV, d, cp), CD),
        compiler_params=_params(("parallel", "parallel")),
    )(h, dgate, dup)


def ffn_up_dg(dgate, dup, u_all, name):
    t = dgate.shape[0]
    d, cp = u_all.shape[1], u_all.shape[2]
    tm, tn = _tile(t, 512, LANE), _tile(d, 512, LANE)

    def body(dg_ref, du_ref, u_ref, o_ref, acc_ref):
        s = pl.program_id(2)

        @pl.when(s == 0)
        def _():
            acc_ref[...] = jnp.zeros_like(acc_ref)

        @pl.when(s < HALF)
        def _():
            acc_ref[...] += _nt(dg_ref[...], u_ref[0])

        @pl.when(s >= HALF)
        def _():
            acc_ref[...] += _nt(du_ref[...], u_ref[0])

        @pl.when(s == N_DEV - 1)
        def _():
            o_ref[...] = acc_ref[...]

    return pl.pallas_call(
        body, name=name, grid=(t // tm, d // tn, N_DEV),
        in_specs=[pl.BlockSpec((tm, cp), lambda i, j, s: (i, jnp.minimum(s, HALF - 1))),
                  pl.BlockSpec((tm, cp), lambda i, j, s: (i, jnp.maximum(s - HALF, 0))),
                  pl.BlockSpec((1, tn, cp), lambda i, j, s: (s, j, 0))],
        out_specs=pl.BlockSpec((tm, tn), lambda i, j, s: (i, j)),
        out_shape=jax.ShapeDtypeStruct((t, d), F32),
        scratch_shapes=[pltpu.VMEM((tm, tn), F32)],
        compiler_params=_params(("parallel", "parallel", "arbitrary")),
    )(dgate, dup, u_all)


def ffn_dact(dxo, s, wd, gate, up, name):
    t, d = dxo.shape
    f = wd.shape[0]
    tm, tn = _tile(t, 1024, LANE), _tile(f, 384, LANE)

    def body(dxo_ref, s_ref, wd_ref, g_ref, u_ref, dg_ref, du_ref):
        da = _nt(dxo_ref[...] * s_ref[...], wd_ref[...])
        gate, up = g_ref[...].astype(F32), u_ref[...].astype(F32)
        sg = jax.nn.sigmoid(gate)
        dg_ref[...] = (da * up * sg * (1.0 + gate * (1.0 - sg))).astype(dg_ref.dtype)
        du_ref[...] = (da * gate * sg).astype(du_ref.dtype)

    o = pl.BlockSpec((tm, tn), lambda i, j: (i, j))
    return pl.pallas_call(
        body, name=name, grid=(t // tm, f // tn),
        in_specs=[pl.BlockSpec((tm, d), lambda i, j: (i, 0)), pl.BlockSpec((1, d), lambda i, j: (0, 0)),
                  pl.BlockSpec((tn, d), lambda i, j: (j, 0)), o, o],
        out_specs=[o, o],
        out_shape=[jax.ShapeDtypeStruct((t, f), CD), jax.ShapeDtypeStruct((t, f), CD)],
        compiler_params=_params(("parallel", "parallel")),
    )(dxo, s, wd, gate, up)


def _bdot(a, b, ca, cb, precision=None):
    if precision is None:
        a, b = a.astype(CD), b.astype(CD)
    return lax.dot_general(a, b, (((ca,), (cb,)), ((0,), (0,))), precision=precision, preferred_element_type=F32)


def _bnn(a, b, precision=None):
    return _bdot(a, b, 2, 1, precision)


def _bnt(a, b, precision=None):
    return _bdot(a, b, 2, 2, precision)


def _btn(a, b, precision=None):
    return _bdot(a, b, 1, 1, precision)


def _att_fn(q, kp, kc, vp, vc, qg, kg, mask_p, mask_c):
    def rn(x, g):
        return x * lax.rsqrt(jnp.mean(x * x, axis=-1, keepdims=True) + EPS) * g

    qn, kpn, kcn = rn(q, qg), rn(kp, kg), rn(kc, kg)
    scale = ATT_HEAD_DIM ** -0.5
    sp = jnp.where(mask_p, _bnt(qn, kpn) * scale, NEG)
    sc = jnp.where(mask_c, _bnt(qn, kcn) * scale, NEG)
    m = jnp.maximum(jnp.max(sp, axis=-1, keepdims=True), jnp.max(sc, axis=-1, keepdims=True))
    pp, pc = jnp.exp(sp - m), jnp.exp(sc - m)
    den = jnp.sum(pp, axis=-1, keepdims=True) + jnp.sum(pc, axis=-1, keepdims=True)
    o = _bnn(pp / den, vp) + _bnn(pc / den, vc)
    return o, m + jnp.log(den)


def _att_masks(g, j, nb_total):
    nb = jnp.int32(nb_total // DILATIONS[0])
    for gi in range(1, len(DILATIONS)):
        nb = jnp.where(g == gi, jnp.int32(nb_total // DILATIONS[gi]), nb)
    has_prev = (j % nb) != 0
    row = lax.broadcasted_iota(jnp.int32, (ATT_BLOCK, ATT_BLOCK), 0)
    col = lax.broadcasted_iota(jnp.int32, (ATT_BLOCK, ATT_BLOCK), 1)
    mask_p = jnp.logical_and(col >= row, has_prev)
    mask_c = col <= row
    return mask_p, mask_c


def _att_specs(nh):
    blk = (1, nh, ATT_BLOCK, ATT_HEAD_DIM)
    cur = pl.BlockSpec(blk, lambda g, j: (g, 0, j, 0))
    prev = pl.BlockSpec(blk, lambda g, j: (g, 0, jnp.maximum(j - 1, 0), 0))
    gain = pl.BlockSpec((1, ATT_HEAD_DIM), lambda g, j: (0, 0))
    lse = pl.BlockSpec((1, nh, ATT_BLOCK, 1), lambda g, j: (g, 0, j, 0))
    return cur, prev, gain, lse


def att_fwd(q, k, v, qg, kg, name):
    ng, nh, t, _ = q.shape
    nbt = t // ATT_BLOCK
    cur, prev, gain, lse_spec = _att_specs(nh)

    def body(q_ref, kp_ref, kc_ref, vp_ref, vc_ref, qg_ref, kg_ref, o_ref, lse_ref):
        mask_p, mask_c = _att_masks(pl.program_id(0), pl.program_id(1), nbt)
        o, lse = _att_fn(q_ref[0], kp_ref[0], kc_ref[0], vp_ref[0], vc_ref[0],
                         qg_ref[...], kg_ref[...], mask_p, mask_c)
        o_ref[0] = o
        lse_ref[0] = lse

    return pl.pallas_call(
        body, name=name, grid=(ng, nbt),
        in_specs=[cur, prev, cur, prev, cur, gain, gain], out_specs=[cur, lse_spec],
        out_shape=[jax.ShapeDtypeStruct(q.shape, F32), jax.ShapeDtypeStruct((ng, nh, t, 1), F32)],
        compiler_params=_params(("parallel", "parallel")),
    )(q, k, k, v, v, qg, kg)


def att_bwd(q, k, v, qg, kg, do, dlse, name):
    ng, nh, t, _ = q.shape
    nbt = t // ATT_BLOCK
    cur, prev, gain, lse_spec = _att_specs(nh)
    whole = pl.BlockSpec((1, nh, t, ATT_HEAD_DIM), lambda g, j: (g, 0, 0, 0))

    def body(q_ref, kp_ref, kc_ref, vp_ref, vc_ref, qg_ref, kg_ref, do_ref, dlse_ref,
             dq_ref, dk_ref, dv_ref, dqg_ref, dkg_ref):
        g, j = pl.program_id(0), pl.program_id(1)
        mask_p, mask_c = _att_masks(g, j, nbt)

        @pl.when(j == 0)
        def _():
            dk_ref[...] = jnp.zeros_like(dk_ref)
            dv_ref[...] = jnp.zeros_like(dv_ref)

        @pl.when(jnp.logical_and(g == 0, j == 0))
        def _():
            dqg_ref[...] = jnp.zeros_like(dqg_ref)
            dkg_ref[...] = jnp.zeros_like(dkg_ref)

        rows_c = pl.ds(pl.multiple_of(j * ATT_BLOCK, ATT_BLOCK), ATT_BLOCK)
        rows_p = pl.ds(pl.multiple_of(jnp.maximum(j - 1, 0) * ATT_BLOCK, ATT_BLOCK), ATT_BLOCK)
        fn = functools.partial(_att_fn, mask_p=mask_p, mask_c=mask_c)
        _, vjp = jax.vjp(fn, q_ref[0], kp_ref[0], kc_ref[0], vp_ref[0], vc_ref[0], qg_ref[...], kg_ref[...])
        dq, dkp, dkc, dvp, dvc, dqg, dkg = vjp((do_ref[0], dlse_ref[0]))
        dq_ref[0] = dq
        dk_ref[0, :, rows_p, :] += dkp
        dv_ref[0, :, rows_p, :] += dvp
        dk_ref[0, :, rows_c, :] += dkc
        dv_ref[0, :, rows_c, :] += dvc
        dqg_ref[...] += dqg
        dkg_ref[...] += dkg

    gshape = jax.ShapeDtypeStruct((1, ATT_HEAD_DIM), F32)
    return pl.pallas_call(
        body, name=name, grid=(ng, nbt),
        in_specs=[cur, prev, cur, prev, cur, gain, gain, cur, lse_spec],
        out_specs=[cur, whole, whole, gain, gain],
        out_shape=[jax.ShapeDtypeStruct(q.shape, F32)] * 3 + [gshape, gshape],
        compiler_params=_params(("arbitrary", "arbitrary")),
    )(q, k, k, v, v, qg, kg, do, dlse)


def _combine_fn(o, lse):
    m = jnp.max(lse, axis=0, keepdims=True)
    e = jnp.exp(lse - m)
    w = e / jnp.sum(e, axis=0, keepdims=True)
    return jnp.sum(w * o, axis=0)


def att_combine(o, lse, name):
    ng, nh, t, hd = o.shape
    tr = _tile(t, 256, 8)
    o_spec = pl.BlockSpec((ng, nh, tr, hd), lambda i: (0, 0, i, 0))
    l_spec = pl.BlockSpec((ng, nh, tr, 1), lambda i: (0, 0, i, 0))
    y_spec = pl.BlockSpec((nh, tr, hd), lambda i: (0, i, 0))

    def body(o_ref, l_ref, y_ref):
        y_ref[...] = _combine_fn(o_ref[...], l_ref[...])

    return pl.pallas_call(
        body, name=name, grid=(t // tr,), in_specs=[o_spec, l_spec], out_specs=y_spec,
        out_shape=jax.ShapeDtypeStruct((nh, t, hd), F32), compiler_params=_params(("parallel",)),
    )(o, lse)


def att_combine_bwd(o, lse, dy, name):
    ng, nh, t, hd = o.shape
    tr = _tile(t, 256, 8)
    o_spec = pl.BlockSpec((ng, nh, tr, hd), lambda i: (0, 0, i, 0))
    l_spec = pl.BlockSpec((ng, nh, tr, 1), lambda i: (0, 0, i, 0))
    y_spec = pl.BlockSpec((nh, tr, hd), lambda i: (0, i, 0))

    def body(o_ref, l_ref, dy_ref, do_ref, dl_ref):
        _, vjp = jax.vjp(_combine_fn, o_ref[...], l_ref[...])
        do, dl = vjp(dy_ref[...])
        do_ref[...] = do
        dl_ref[...] = dl

    return pl.pallas_call(
        body, name=name, grid=(t // tr,), in_specs=[o_spec, l_spec, y_spec], out_specs=[o_spec, l_spec],
        out_shape=[jax.ShapeDtypeStruct(o.shape, F32), jax.ShapeDtypeStruct(lse.shape, F32)],
        compiler_params=_params(("parallel",)),
    )(o, lse, dy)


def _shift_down(x, s):
    if s == 0:
        return x
    row = lax.broadcasted_iota(jnp.int32, x.shape, 0)
    return jnp.where(row >= s, pltpu.roll(x, s, 0), 0.0)


def _shift_up(x, s):
    if s == 0:
        return x
    t = x.shape[0]
    row = lax.broadcasted_iota(jnp.int32, x.shape, 0)
    return jnp.where(row < t - s, pltpu.roll(x, t - s, 0), 0.0)


def conv_fwd(z, col0, w, name):
    t = z.shape[0]
    c = w.shape[1]
    nblk0 = col0 // LANE

    def body(z_ref, w_ref, c_ref):
        zv = z_ref[...]
        acc = None
        for i in range(CONV_WIDTH):
            term = _shift_down(zv, CONV_WIDTH - 1 - i) * w_ref[i:i + 1, :]
            acc = term if acc is None else acc + term
        c_ref[...] = acc

    return pl.pallas_call(
        body, name=name, grid=(c // LANE,),
        in_specs=[pl.BlockSpec((t, LANE), lambda j: (0, nblk0 + j)), pl.BlockSpec((CONV_WIDTH, LANE), lambda j: (0, j))],
        out_specs=pl.BlockSpec((t, LANE), lambda j: (0, j)),
        out_shape=jax.ShapeDtypeStruct((t, c), F32), compiler_params=_params(("parallel",)),
    )(z, w)


def conv_bwd(dc, z, col0, w, name):
    t = z.shape[0]
    c = w.shape[1]
    nblk0 = col0 // LANE

    def body(dc_ref, z_ref, w_ref, dz_ref, dw_ref):
        dcv, zv = dc_ref[...], z_ref[...]
        acc = None
        for i in range(CONV_WIDTH):
            s = CONV_WIDTH - 1 - i
            term = _shift_up(dcv, s) * w_ref[i:i + 1, :]
            acc = term if acc is None else acc + term
            dw_ref[i:i + 1, :] = jnp.sum(dcv * _shift_down(zv, s), axis=0, keepdims=True)
        dz_ref[...] = acc.astype(dz_ref.dtype)

    blk = pl.BlockSpec((t, LANE), lambda j: (0, j))
    wblk = pl.BlockSpec((CONV_WIDTH, LANE), lambda j: (0, j))
    return pl.pallas_call(
        body, name=name, grid=(c // LANE,),
        in_specs=[blk, pl.BlockSpec((t, LANE), lambda j: (0, nblk0 + j)), wblk], out_specs=[blk, wblk],
        out_shape=[jax.ShapeDtypeStruct((t, c), CD), jax.ShapeDtypeStruct((CONV_WIDTH, c), F32)],
        compiler_params=_params(("parallel",)),
    )(dc, z, w)


def _dn_consts():
    c = DN_CHUNK
    row = lax.broadcasted_iota(jnp.int32, (c, c), 0)
    col = lax.broadcasted_iota(jnp.int32, (c, c), 1)
    return dict(tril=row >= col, strict=row > col, eye=(row == col).astype(F32),
                tril_f=(row >= col).astype(F32), triu_f=(row <= col).astype(F32))


def _softplus(x):
    return jnp.maximum(x, 0.0) + jnp.log(1.0 + jnp.exp(-jnp.abs(x)))


def _split(x):
    hi = x.astype(CD)
    return hi, (x - hi.astype(F32)).astype(CD)


def _bdot3(a, b, ca, cb):
    ah, al = _split(a)
    bh, bl = _split(b)
    return _bdot(ah, bh, ca, cb) + (_bdot(ah, bl, ca, cb) + _bdot(al, bh, ca, cb))


def _tri_inv_impl(a_mat):
    c = a_mat.shape[-1]
    eye = (lax.broadcasted_iota(jnp.int32, (c, c), 0) == lax.broadcasted_iota(jnp.int32, (c, c), 1)).astype(F32)
    nk_ = -a_mat
    t_inv = eye + nk_
    for _ in range(c.bit_length() - 2):
        nk_ = _bdot3(nk_, nk_, 2, 1)
        t_inv = t_inv + _bdot3(t_inv, nk_, 2, 1)
    return t_inv


@jax.custom_vjp
def _tri_inv(a_mat):
    return _tri_inv_impl(a_mat)


def _tri_inv_fwd(a_mat):
    t_inv = _tri_inv_impl(a_mat)
    return t_inv, t_inv


def _tri_inv_bwd(t_inv, dt_inv):
    return (-_bdot3(_bdot3(t_inv, dt_inv, 1, 1), t_inv, 2, 2),)


_tri_inv.defvjp(_tri_inv_fwd, _tri_inv_bwd)


def _dn_chunk(cq, ck, cv, og, a_col, b_col, al, dt, gn, s_prev, *, k, inv):
    q = _silu(cq)
    q = q * lax.rsqrt(jnp.sum(q * q, axis=-1, keepdims=True) + EPS) * (DN_HEAD_DIM ** -0.5)
    kk = _silu(ck)
    kk = kk * lax.rsqrt(jnp.sum(kk * kk, axis=-1, keepdims=True) + EPS)
    v = _silu(cv)
    g = -jnp.exp(al) * _softplus(a_col + dt)
    beta = jax.nn.sigmoid(b_col)
    g_row = jnp.sum(k["eye"] * g, axis=1, keepdims=True)
    gc_col = jnp.sum(k["tril_f"] * g_row, axis=2, keepdims=True)
    gc_row = jnp.sum(k["triu_f"] * g, axis=1, keepdims=True)
    ldec = jnp.where(k["tril"], jnp.exp(jnp.where(k["tril"], gc_col - gc_row, 0.0)), 0.0)
    kb, vb = kk * beta, v * beta
    a_mat = jnp.where(k["strict"], _bnt(kb, kk) * ldec, 0.0)
    t_inv = inv(a_mat)
    egc = jnp.exp(gc_col)
    u = _bnn(t_inv, vb)
    w = _bnn(t_inv, kb * egc)
    attn = jnp.where(k["tril"], _bnt(q, kk) * ldec, 0.0)
    gc_last = jnp.sum(g, axis=1, keepdims=True)
    k_dec = kk * jnp.exp(gc_last - gc_col)
    v_new = u - _bnn(w, s_prev)
    o = _bnn(q * egc, s_prev) + _bnn(attn, v_new)
    s_new = s_prev * jnp.exp(gc_last) + _btn(k_dec, v_new)
    y = o * lax.rsqrt(jnp.mean(o * o, axis=-1, keepdims=True) + EPS) * gn * _silu(og)
    return y, s_new


def _dn_heads(ref, nh):
    hd = DN_HEAD_DIM
    return jnp.stack([ref[:, h * hd:(h + 1) * hd] for h in range(nh)])


def _dn_specs(nh, col_gate, col_ab, order):
    hd, c = DN_HEAD_DIM, DN_CHUNK
    w = nh * hd
    qs = pl.BlockSpec((c, w), lambda n: (order(n), 0))
    ks = pl.BlockSpec((c, w), lambda n: (order(n), 1))
    vs = pl.BlockSpec((c, w), lambda n: (order(n), 2))
    gs = pl.BlockSpec((c, w), lambda n: (order(n), col_gate // w))
    ab = pl.BlockSpec((c, LANE), lambda n: (order(n), col_ab // LANE))
    scal = pl.BlockSpec((nh, 1, 1), lambda n: (0, 0, 0))
    gn = pl.BlockSpec((1, hd), lambda n: (0, 0))
    st = pl.BlockSpec((1, nh, hd, hd), lambda n: (order(n), 0, 0, 0))
    return qs, ks, vs, gs, ab, scal, gn, st


def _lane_pick(x, idx):
    lane = lax.broadcasted_iota(jnp.int32, x.shape, 1)
    return jnp.sum(jnp.where(lane == idx, x, 0.0), axis=1, keepdims=True)


def dn_fwd(cv, z, col_gate, col_ab, a_log, dt_bias, gn, name):
    t = cv.shape[0]
    nh = a_log.shape[0]
    hd, c = DN_HEAD_DIM, DN_CHUNK
    n_chunks = t // c
    qs, ks, vs, gs, ab, scal, gnspec, st = _dn_specs(nh, col_gate, col_ab, lambda n: n)

    def body(q_ref, k_ref, v_ref, g_ref, ab_ref, al_ref, dt_ref, gn_ref, y_ref, st_ref, s_scr):
        @pl.when(pl.program_id(0) == 0)
        def _():
            s_scr[...] = jnp.zeros_like(s_scr)

        abv = ab_ref[...]
        a_col = jnp.stack([_lane_pick(abv, h) for h in range(nh)])
        b_col = jnp.stack([_lane_pick(abv, nh + h) for h in range(nh)])
        s_prev = s_scr[...]
        st_ref[0] = s_prev
        y, s_new = _dn_chunk(_dn_heads(q_ref, nh), _dn_heads(k_ref, nh), _dn_heads(v_ref, nh), _dn_heads(g_ref, nh),
                             a_col, b_col, al_ref[...], dt_ref[...], gn_ref[...], s_prev,
                             k=_dn_consts(), inv=_tri_inv_impl)
        for h in range(nh):
            y_ref[:, h * hd:(h + 1) * hd] = y[h].astype(y_ref.dtype)
        s_scr[...] = s_new

    return pl.pallas_call(
        body, name=name, grid=(n_chunks,),
        in_specs=[qs, ks, vs, gs, ab, scal, scal, gnspec],
        out_specs=[pl.BlockSpec((c, nh * hd), lambda n: (n, 0)), st],
        out_shape=[jax.ShapeDtypeStruct((t, nh * hd), CD), jax.ShapeDtypeStruct((n_chunks, nh, hd, hd), F32)],
        scratch_shapes=[pltpu.VMEM((nh, hd, hd), F32)],
        compiler_params=_params(("arbitrary",)),
    )(cv, cv, cv, z, z, a_log, dt_bias, gn)


def dn_bwd(cv, z, col_gate, col_ab, a_log, dt_bias, gn, states, dy, name):
    t = cv.shape[0]
    nh = a_log.shape[0]
    hd, c = DN_HEAD_DIM, DN_CHUNK
    w = nh * hd
    n_chunks = t // c
    rev = lambda n: n_chunks - 1 - n
    qs, ks, vs, gs, ab, scal, gnspec, st = _dn_specs(nh, col_gate, col_ab, rev)
    yspec = pl.BlockSpec((c, w), lambda n: (rev(n), 0))

    def body(q_ref, k_ref, v_ref, g_ref, ab_ref, al_ref, dt_ref, gn_ref, st_ref, dy_ref,
             dc_ref, dg_ref, dab_ref, dal_ref, ddt_ref, dgn_ref, ds_scr):
        @pl.when(pl.program_id(0) == 0)
        def _():
            ds_scr[...] = jnp.zeros_like(ds_scr)
            dal_ref[...] = jnp.zeros_like(dal_ref)
            ddt_ref[...] = jnp.zeros_like(ddt_ref)
            dgn_ref[...] = jnp.zeros_like(dgn_ref)

        abv = ab_ref[...]
        a_col = jnp.stack([_lane_pick(abv, h) for h in range(nh)])
        b_col = jnp.stack([_lane_pick(abv, nh + h) for h in range(nh)])
        fn = functools.partial(_dn_chunk, k=_dn_consts(), inv=_tri_inv)
        _, vjp = jax.vjp(fn, _dn_heads(q_ref, nh), _dn_heads(k_ref, nh), _dn_heads(v_ref, nh), _dn_heads(g_ref, nh),
                         a_col, b_col, al_ref[...], dt_ref[...], gn_ref[...], st_ref[0])
        dq, dk, dv, dg, da, db, dal, ddt, dgn, ds = vjp((_dn_heads(dy_ref, nh), ds_scr[...]))
        lane = lax.broadcasted_iota(jnp.int32, (c, LANE), 1)
        dab = jnp.zeros((c, LANE), F32)
        for h in range(nh):
            cols = slice(h * hd, (h + 1) * hd)
            dc_ref[:, cols] = dq[h]
            dc_ref[:, w + h * hd:w + (h + 1) * hd] = dk[h]
            dc_ref[:, 2 * w + h * hd:2 * w + (h + 1) * hd] = dv[h]
            dg_ref[:, cols] = dg[h].astype(dg_ref.dtype)
            dab = dab + jnp.where(lane == h, da[h], 0.0) + jnp.where(lane == nh + h, db[h], 0.0)
        dab_ref[...] = dab.astype(dab_ref.dtype)
        dal_ref[...] += dal
        ddt_ref[...] += ddt
        dgn_ref[...] += dgn
        ds_scr[...] = ds

    sshape = jax.ShapeDtypeStruct((nh, 1, 1), F32)
    res = pl.pallas_call(
        body, name=name, grid=(n_chunks,),
        in_specs=[qs, ks, vs, gs, ab, scal, scal, gnspec, st, yspec],
        out_specs=[pl.BlockSpec((c, 3 * w), lambda n: (rev(n), 0)), yspec,
                   pl.BlockSpec((c, LANE), lambda n: (rev(n), 0)), scal, scal, gnspec],
        out_shape=[jax.ShapeDtypeStruct((t, 3 * w), F32), jax.ShapeDtypeStruct((t, w), CD),
                   jax.ShapeDtypeStruct((t, LANE), CD), sshape, sshape, jax.ShapeDtypeStruct((1, hd), F32)],
        scratch_shapes=[pltpu.VMEM((nh, hd, hd), F32)],
        compiler_params=_params(("arbitrary",)),
    )(cv, cv, cv, z, z, a_log, dt_bias, gn, states, dy)
    return res


def merge_fwd(ya, wpa, yd, wpd, z, col_m, name):
    t, d = yd.shape[0], wpd.shape[1]
    tm, tn = _tile(t, 512, LANE), _tile(d, 256, LANE)
    nb1, nb2 = col_m // tn, (col_m + d) // tn

    def body(ya_ref, wpa_ref, yd_ref, wpd_ref, z1_ref, z2_ref, m_ref, pa_ref, pd_ref):
        pa = _nn(ya_ref[...], wpa_ref[...])
        pd = _nn(yd_ref[...], wpd_ref[...])
        pa_ref[...] = pa
        pd_ref[...] = pd
        m_ref[...] = (jax.nn.sigmoid(z1_ref[...]) * pa + jax.nn.sigmoid(z2_ref[...]) * pd).astype(m_ref.dtype)

    o = pl.BlockSpec((tm, tn), lambda i, j: (i, j))
    return pl.pallas_call(
        body, name=name, grid=(t // tm, d // tn),
        in_specs=[pl.BlockSpec((tm, ya.shape[1]), lambda i, j: (i, 0)),
                  pl.BlockSpec((wpa.shape[0], tn), lambda i, j: (0, j)),
                  pl.BlockSpec((tm, yd.shape[1]), lambda i, j: (i, 0)),
                  pl.BlockSpec((wpd.shape[0], tn), lambda i, j: (0, j)),
                  pl.BlockSpec((tm, tn), lambda i, j: (i, nb1 + j)),
                  pl.BlockSpec((tm, tn), lambda i, j: (i, nb2 + j))],
        out_specs=[o, o, o],
        out_shape=[jax.ShapeDtypeStruct((t, d), CD), jax.ShapeDtypeStruct((t, d), F32),
                   jax.ShapeDtypeStruct((t, d), F32)],
        compiler_params=_params(("parallel", "parallel")),
    )(ya, wpa, yd, wpd, z, z)


def merge_bwd(dm, pa, pd, z, col_m, name):
    t, d = dm.shape
    tm, tn = _tile(t, 512, 8), _tile(d, 256, LANE)
    nb1, nb2 = col_m // tn, (col_m + d) // tn

    def body(dm_ref, pa_ref, pd_ref, z1_ref, z2_ref, dpa_ref, dpd_ref, dz1_ref, dz2_ref):
        dmv = dm_ref[...]
        s1, s2 = jax.nn.sigmoid(z1_ref[...]), jax.nn.sigmoid(z2_ref[...])
        dpa_ref[...] = (dmv * s1).astype(dpa_ref.dtype)
        dpd_ref[...] = (dmv * s2).astype(dpd_ref.dtype)
        dz1_ref[...] = (dmv * pa_ref[...] * s1 * (1.0 - s1)).astype(dz1_ref.dtype)
        dz2_ref[...] = (dmv * pd_ref[...] * s2 * (1.0 - s2)).astype(dz2_ref.dtype)

    o = pl.BlockSpec((tm, tn), lambda i, j: (i, j))
    sh = jax.ShapeDtypeStruct((t, d), CD)
    return pl.pallas_call(
        body, name=name, grid=(t // tm, d // tn),
        in_specs=[o, o, o, pl.BlockSpec((tm, tn), lambda i, j: (i, nb1 + j)),
                  pl.BlockSpec((tm, tn), lambda i, j: (i, nb2 + j))],
        out_specs=[o, o, o, o], out_shape=[sh, sh, sh, sh],
        compiler_params=_params(("parallel", "parallel")),
    )(dm, pa, pd, z, z)


def ada_fwd(c_all, w, b, name):
    nl, d, n = w.shape
    tn = _tile(n, 384, LANE)

    def body(c_ref, w_ref, b_ref, o_ref):
        o_ref[0] = _nn(_silu(c_ref[...]), w_ref[0]) + b_ref[0]

    return pl.pallas_call(
        body, name=name, grid=(nl, n // tn),
        in_specs=[pl.BlockSpec(c_all.shape, lambda l, j: (0, 0)), pl.BlockSpec((1, d, tn), lambda l, j: (l, 0, j)),
                  pl.BlockSpec((1, 1, tn), lambda l, j: (l, 0, j))],
        out_specs=pl.BlockSpec((1, c_all.shape[0], tn), lambda l, j: (l, 0, j)),
        out_shape=jax.ShapeDtypeStruct((nl, c_all.shape[0], n), F32),
        compiler_params=_params(("parallel", "parallel")),
    )(c_all, w, b)


def ada_bwd(c_pad, dmod_pad, name):
    nl, kp, n = dmod_pad.shape
    d = c_pad.shape[1]
    tn = _tile(n, 384, LANE)

    def body(c_ref, g_ref, o_ref):
        o_ref[0] = _tn(_silu(c_ref[...]), g_ref[0])

    return pl.pallas_call(
        body, name=name, grid=(nl, n // tn),
        in_specs=[pl.BlockSpec((kp, d), lambda l, j: (0, 0)), pl.BlockSpec((1, kp, tn), lambda l, j: (l, 0, j))],
        out_specs=pl.BlockSpec((1, d, tn), lambda l, j: (l, 0, j)),
        out_shape=jax.ShapeDtypeStruct((nl, d, n), F32),
        compiler_params=_params(("parallel", "parallel")),
    )(c_pad, dmod_pad)


def loss_head(y, target, name):
    t, d = y.shape
    tr = _tile(t, 256, 8)

    def body(y_ref, t_ref, dy_ref, l_ref):
        err = y_ref[...] - t_ref[...]
        dy_ref[...] = err * (1.0 / d)
        part = jnp.sum(jnp.sum(err * err, axis=1, keepdims=True), axis=0, keepdims=True) * (0.5 / d)

        @pl.when(pl.program_id(0) == 0)
        def _():
            l_ref[...] = jnp.zeros_like(l_ref)

        l_ref[...] += part

    row = pl.BlockSpec((tr, d), lambda i: (i, 0))
    return pl.pallas_call(
        body, name=name, grid=(t // tr,), in_specs=[row, row],
        out_specs=[row, pl.BlockSpec((8, LANE), lambda i: (0, 0))],
        out_shape=[jax.ShapeDtypeStruct((t, d), F32), jax.ShapeDtypeStruct((8, LANE), F32)],
        compiler_params=_params(("arbitrary",)),
    )(y, target)


def _adamw_update(g, w_ref, m_ref, v_ref, go_ref, d_ref, mo_ref, vo_ref):
    m_new = ADAM_B1 * m_ref[...] + (1.0 - ADAM_B1) * g
    v_new = ADAM_B2 * v_ref[...] + (1.0 - ADAM_B2) * jnp.square(g)
    m_hat = m_new / (1.0 - ADAM_B1 ** ADAM_STEP)
    v_hat = v_new / (1.0 - ADAM_B2 ** ADAM_STEP)
    go_ref[...] = g
    d_ref[...] = -ADAM_LR * (m_hat / (jnp.sqrt(v_hat) + ADAM_EPS) + ADAM_WD * w_ref[...])
    mo_ref[...] = m_new
    vo_ref[...] = v_new


def adamw(w, m, v, g_slots, name):
    shape = w.shape
    nslot = g_slots.shape[0]
    if w.ndim == 2:
        w3, m3, v3, g4 = w[None], m[None], v[None], g_slots[:, None]
    else:
        w3, m3, v3, g4 = w, m, v, g_slots
    nl, r, c = w3.shape
    tr = _tile(r, 256, 16)

    def body(w_ref, m_ref, v_ref, g_ref, *outs):
        g = g_ref[0].astype(F32)
        for s in range(1, nslot):
            g = g + g_ref[s].astype(F32)
        _adamw_update(g, w_ref, m_ref, v_ref, *outs)

    blk = pl.BlockSpec((1, tr, c), lambda l, i: (l, i, 0))
    gblk = pl.BlockSpec((nslot, 1, tr, c), lambda l, i: (0, l, i, 0))
    sh = jax.ShapeDtypeStruct(w3.shape, F32)
    outs = pl.pallas_call(
        body, name=name, grid=(nl, r // tr), in_specs=[blk, blk, blk, gblk], out_specs=[blk] * 4,
        out_shape=[sh] * 4, compiler_params=_params(("parallel", "parallel")),
    )(w3, m3, v3, g4)
    return tuple(o.reshape(shape) for o in outs)


def adamw_layers(w, m, v, g_layers, name):
    nl, r, c = w.shape
    nslot, _, cg = g_layers[0].shape
    tr = _tile(r, 256, 16)

    def body(w_ref, m_ref, v_ref, *rest):
        g_refs, outs = rest[:nl], rest[nl:]
        for li in range(nl):
            @pl.when(pl.program_id(0) == li)
            def _(g_ref=g_refs[li]):
                g = g_ref[0, :, :c].astype(F32)
                for s in range(1, nslot):
                    g = g + g_ref[s, :, :c].astype(F32)
                _adamw_update(g[None], w_ref, m_ref, v_ref, *outs)

    blk = pl.BlockSpec((1, tr, c), lambda l, i: (l, i, 0))
    gblks = [pl.BlockSpec((nslot, tr, cg), lambda l, i, li=li: (0, jnp.where(l == li, i, 0), 0)) for li in range(nl)]
    sh = jax.ShapeDtypeStruct(w.shape, F32)
    return tuple(pl.pallas_call(
        body, name=name, grid=(nl, r // tr), in_specs=[blk, blk, blk] + gblks, out_specs=[blk] * 4,
        out_shape=[sh] * 4, compiler_params=_params(("arbitrary", "arbitrary")),
    )(w, m, v, *g_layers))


def _rows(flat, unit=16):
    n = flat.shape[0]
    per = 1024 * unit
    pad = (-n) % per
    if pad:
        flat = jnp.concatenate([flat, jnp.zeros((pad,), flat.dtype)])
    return flat.reshape(-1, 1024)


def _to_classes(a, t):
    outs = []
    for gi, dil in enumerate(DILATIONS):
        x = a[:, gi]
        x = x.reshape(t // dil, dil, *x.shape[1:]).transpose(2, 1, 0, 3)
        outs.append(x.reshape(x.shape[0], t, x.shape[3]))
    return jnp.stack(outs)


def _from_classes(a, t):
    outs = []
    for gi, dil in enumerate(DILATIONS):
        x = a[gi]
        x = x.reshape(x.shape[0], dil, t // dil, x.shape[2]).transpose(2, 1, 0, 3)
        outs.append(x.reshape(t, x.shape[2], x.shape[3]))
    return jnp.stack(outs, axis=1)


def _unshard_cols(g):
    return g.transpose(1, 0, 2).reshape(g.shape[1], -1)


def _shard_cols(full):
    r = full.shape[0]
    return full.reshape(r, N_DEV, -1).transpose(1, 0, 2)


def kernel(x, c, ada_w, ada_b, norm_ff1, ffn1_w_up, ffn1_w_down, norm_mix, w_in, q_norm, k_norm, conv_w, a_log, dt_bias, dn_norm, w_proj_att, w_proj_dn, w_out, norm_ff2, ffn2_w_up, ffn2_w_down, loss_target, m_ada_w, m_ada_b, m_norm_ff1, m_ffn1_w_up, m_ffn1_w_down, m_norm_mix, m_w_in, m_q_norm, m_k_norm, m_conv_w, m_a_log, m_dt_bias, m_dn_norm, m_w_proj_att, m_w_proj_dn, m_w_out, m_norm_ff2, m_ffn2_w_up, m_ffn2_w_down, v_ada_w, v_ada_b, v_norm_ff1, v_ffn1_w_up, v_ffn1_w_down, v_norm_mix, v_w_in, v_q_norm, v_k_norm, v_conv_w, v_a_log, v_dt_bias, v_dn_norm, v_w_proj_att, v_w_proj_dn, v_w_out, v_norm_ff2, v_ffn2_w_up, v_ffn2_w_down):
    nl = ada_w.shape[0]
    t, d = x.shape[1], x.shape[2]
    dff = ffn1_w_down.shape[1] * N_DEV
    ha = d // 256
    ng = len(DILATIONS)
    att_w = ng * ha * ATT_HEAD_DIM
    nh = d // DN_HEAD_DIM
    dn_w = nh * DN_HEAD_DIM
    n_in = w_in.shape[2] * N_DEV
    off_dn, off_gate = 3 * att_w, 3 * att_w + 3 * dn_w
    off_a = off_gate + dn_w
    off_merge = off_a + 2 * nh
    assert off_merge + 2 * d == n_in
    col_dn, col_gate, col_m = 0, 3 * dn_w, 4 * dn_w
    col_att = col_m + 2 * d
    col_ab = col_att + 3 * att_w
    zw = col_ab + 2 * LANE
    me = 4 * lax.axis_index("x") + 2 * lax.axis_index("y") + lax.axis_index("c")
    xs = x[0]
    target = loss_target[0]

    conv_rows = _rows(conv_w.reshape(-1), 8)
    pack0 = jnp.concatenate([jnp.concatenate([c, jnp.zeros((7, d), F32)]).reshape(-1), conv_rows.reshape(-1)])
    pack0 = _rows(pack0, 8)
    g0 = all_gather([pack0], "ag_c_conv")[0].reshape(N_DEV, -1)
    c_all = g0[:, :d]
    cw = g0[:, 8 * d:8 * d + conv_w.size].reshape(N_DEV, nl, CONV_WIDTH, -1)
    conv_full = cw.transpose(1, 2, 0, 3).reshape(nl, CONV_WIDTH, 3 * dn_w)

    n_ada = ada_w.shape[2]
    b_mine = lax.dynamic_slice_in_dim(ada_b, me * n_ada, n_ada, axis=1)[:, None, :]
    mod_s = ada_fwd(c_all, ada_w, b_mine, "ada_fwd")
    gm = all_gather([mod_s], "ag_mod")[0]
    mod = lax.dynamic_index_in_dim(gm, me, axis=2, keepdims=False)
    mod = mod.transpose(1, 0, 2).reshape(nl, N_ADA, 1, d)

    kinds = [ffn1_w_up, ffn1_w_down, w_in, w_proj_att, w_proj_dn, w_out, ffn2_w_up, ffn2_w_down]
    c_up = ffn1_w_up.shape[2]
    cp = -(-c_up // LANE) * LANE
    r_dn = ffn1_w_down.shape[1]
    assert 2 * r_dn == c_up

    def layer_shards(l):
        def pad_up(w):
            return jnp.pad(w.astype(CD), ((0, 0), (0, cp - c_up)))
        return [pad_up(ffn1_w_up[l]), ffn1_w_down[l].astype(CD), w_in[l].astype(CD), w_proj_att[l].astype(CD),
                w_proj_dn[l].astype(CD), w_out[l].astype(CD), pad_up(ffn2_w_up[l]), ffn2_w_down[l].astype(CD)]

    def gather_landing(shards):
        return [lax.dynamic_update_index_in_dim(lax.empty((N_DEV,) + s.shape, s.dtype), s, me, 0) for s in shards]

    def slot_landing(arrs):
        return [lax.dynamic_update_index_in_dim(lax.empty(a.shape, a.dtype),
                                                lax.dynamic_index_in_dim(a, me, 0, keepdims=False), me, 0) for a in arrs]

    gathered = all_gather(layer_shards(0), "ag_weights0")
    prefetch = None

    def full_weight(ki, l):
        blk = gathered[ki]
        if ki in (2, 3):
            return _unshard_cols(blk)
        if ki in (1, 7):
            pairs = blk.reshape(HALF, c_up, d)
            return jnp.pad(pairs, ((0, 0), (0, cp - c_up), (0, 0))).reshape(HALF * cp, d)
        return blk.reshape(-1, blk.shape[2])

    def down_grad_slots(g):
        return g.reshape(HALF, cp, d)[:, :c_up].reshape(N_DEV, r_dn, d)

    def w_in_cols(w):
        pad = jnp.zeros((w.shape[0], zw - n_in), w.dtype)
        return jnp.concatenate([w[:, off_dn:off_a], w[:, off_merge:], w[:, :off_dn], w[:, off_a:off_merge], pad], axis=1)

    def w_in_cols_inv(g):
        return jnp.concatenate([g[:, col_att:col_ab], g[:, :col_m], g[:, col_ab:col_ab + 2 * nh], g[:, col_m:col_att]], axis=1)

    saved = []
    xc = xs
    mods = []
    for l in range(nl):
        sv = {}
        if prefetch is not None:
            gathered = exchange_wait(*prefetch[:4], xc, True, f"ag_wait{l}")
        mod_l = mod[l]
        if l + 1 < nl:
            shards = layer_shards(l + 1)
            prefetch = exchange_start(shards, gather_landing(shards), gathered[0], True, f"ag_start{l + 1}")
            mod_l = mod_l + prefetch[4][0, 0]
        mods.append(mod_l)
        sh1, sc1, gt1, sh2, sc2, gt2, sh3, sc3, gt3 = [mod_l[i] for i in range(N_ADA)]
        w_dn1, w_dn2 = full_weight(1, l), full_weight(7, l)
        win = w_in_cols(full_weight(2, l))
        wpa, wpd, wo = full_weight(3, l), full_weight(4, l), full_weight(5, l)
        sv["w"] = (gathered[0], gathered[6], w_dn1, w_dn2, win, wpa, wpd, wo)

        def ffn(xin, g, sh, sc, gt, u_all, w_dn):
            h = norm_mod(xin, g, sc, sh, "norm_mod")
            gate, up, a = ffn_up(h, u_all, "ffn_up")
            xo, f = matmul([(a, w_dn)], "nn", F32, "ffn_down", out_scale=0.5 * gt, resid=xin, save_acc=True)
            return xo, (xin, h, gate, up, a, f)

        xc, sv["ffn1"] = ffn(xc, norm_ff1[l:l + 1], sh1, sc1, gt1, gathered[0], w_dn1)

        x_mix = xc
        h2 = norm_mod(x_mix, norm_mix[l:l + 1], sc2, sh2, "norm_mod")
        z = matmul([(h2, win)], "nn", F32, "w_in", tn=512)
        qkv = z[:, col_att:col_ab].reshape(t, 3, ng, ha, ATT_HEAD_DIM)
        qp, kp, vp = (_to_classes(qkv[:, i], t) for i in range(3))
        qg, kg = q_norm[l:l + 1], k_norm[l:l + 1]
        o_cls, lse_cls = att_fwd(qp, kp, vp, qg, kg, "att_fwd")
        o_tok = _from_classes(o_cls, t).transpose(1, 2, 0, 3)
        lse_tok = _from_classes(lse_cls, t).transpose(1, 2, 0, 3)
        ya_h = att_combine(o_tok, lse_tok, "att_combine")
        ya = ya_h.transpose(1, 0, 2).reshape(t, ha * ATT_HEAD_DIM).astype(CD)
        cvo = conv_fwd(z, col_dn, conv_full[l], "conv_fwd")
        al3, dt3 = a_log[l].reshape(nh, 1, 1), dt_bias[l].reshape(nh, 1, 1)
        gn = dn_norm[l:l + 1]
        yd, states = dn_fwd(cvo, z, col_gate, col_ab, al3, dt3, gn, "dn_fwd")
        mrg, pa, pd = merge_fwd(ya, wpa, yd, wpd, z, col_m, "merge_fwd")
        xc, f2 = matmul([(mrg, wo)], "nn", F32, "w_out", out_scale=gt2, resid=x_mix, save_acc=True)
        sv["mix"] = (x_mix, h2, z, qp, kp, vp, o_tok, lse_tok, ya, cvo, yd, states, mrg, pa, pd, f2)

        xc, sv["ffn2"] = ffn(xc, norm_ff2[l:l + 1], sh3, sc3, gt3, gathered[6], w_dn2)
        saved.append(sv)

    dxc, loss_blk = loss_head(xc, target, "loss_head")

    def slots(ki, g):
        if ki in (0, 1, 6, 7):
            return g
        return _shard_cols(g) if ki in (2, 3) else g.reshape(N_DEV, -1, g.shape[1])

    gbig = [None] * len(kinds)
    dmods, small, pending = [], [], []
    token = None
    for l in reversed(range(nl)):
        sv = saved[l]
        mod_l = mods[l] if token is None else mods[l] + token[0, 0]
        sh1, sc1, gt1, sh2, sc2, gt2, sh3, sc3, gt3 = [mod_l[i] for i in range(N_ADA)]
        u_all1, u_all2, w_dn1, w_dn2, win, wpa, wpd, wo = sv["w"]

        def ffn_bwd(dxo, g, sh, sc, gt, u_all, w_dn, sv_f):
            xin, h, gate, up, a, f = sv_f
            s = 0.5 * gt
            g_dn = down_grad_slots(matmul([(a, dxo)], "tn", CD, "ffn_down_wg", out_scale=s))
            dgate, dup = ffn_dact(dxo, s, w_dn, gate, up, "ffn_dact")
            g_up = ffn_up_wg(h, dgate, dup, cp, "ffn_up_wg")
            dh = ffn_up_dg(dgate, dup, u_all, "ffn_up_dg")
            dx, dg, dsc, dsh, dgt = norm_mod_bwd(xin, g, sc, sh, dh, dxo, f, 0.5, "norm_mod_bwd")
            return dx, g_up, g_dn, (dg, dsc, dsh, dgt)

        dxc, gbig[6], gbig[7], (dg3, dsc3, dsh3, dgt3) = ffn_bwd(
            dxc, norm_ff2[l:l + 1], sh3, sc3, gt3, u_all2, w_dn2, sv["ffn2"])

        x_mix, h2, z, qp, kp, vp, o_tok, lse_tok, ya, cvo, yd, states, mrg, pa, pd, f2 = sv["mix"]
        gbig[5] = matmul([(mrg, dxc)], "tn", CD, "w_out_wg", out_scale=gt2)
        dm = matmul([(dxc, wo)], "nt", F32, "w_out_dg", a_scale=gt2)
        dpa, dpd, dz1, dz2 = merge_bwd(dm, pa, pd, z, col_m, "merge_bwd")
        gbig[3] = matmul([(ya, dpa)], "tn", CD, "patt_wg")
        gbig[4] = matmul([(yd, dpd)], "tn", CD, "pdn_wg")
        dya = matmul([(dpa, wpa)], "nt", F32, "patt_dg")
        dyd = matmul([(dpd, wpd)], "nt", F32, "pdn_dg")
        dya_h = dya.reshape(t, ha, ATT_HEAD_DIM).transpose(1, 0, 2)
        do_tok, dlse_tok = att_combine_bwd(o_tok, lse_tok, dya_h, "att_combine_bwd")
        do_cls = _to_classes(do_tok.transpose(2, 0, 1, 3), t)
        dlse_cls = _to_classes(dlse_tok.transpose(2, 0, 1, 3), t)
        qg, kg = q_norm[l:l + 1], k_norm[l:l + 1]
        dq, dk, dv, dqg, dkg = att_bwd(qp, kp, vp, qg, kg, do_cls, dlse_cls, "att_bwd")
        dz_att = jnp.stack([_from_classes(a_, t) for a_ in (dq, dk, dv)], axis=1).reshape(t, 3 * att_w).astype(CD)
        al3, dt3 = a_log[l].reshape(nh, 1, 1), dt_bias[l].reshape(nh, 1, 1)
        gn = dn_norm[l:l + 1]
        dcvo, dz_gate, dz_ab, dal, ddt, dgn = dn_bwd(
            cvo, z, col_gate, col_ab, al3, dt3, gn, states, dyd, "dn_bwd")
        dz_dn, dconv = conv_bwd(dcvo, z, col_dn, conv_full[l], "conv_bwd")
        dz = jnp.concatenate([dz_dn, dz_gate, dz1, dz2, dz_att, dz_ab, jnp.zeros((t, LANE), CD)], axis=1)
        gbig[2] = w_in_cols_inv(matmul([(h2, dz)], "tn", CD, "w_in_wg", tn=512))
        dh2 = matmul([(dz, win)], "nt", F32, "w_in_dg", tk=zw // 4 if (zw // 4) % LANE == 0 else zw)
        dxc, dg2, dsc2, dsh2, dgt2 = norm_mod_bwd(x_mix, norm_mix[l:l + 1], sc2, sh2, dh2, dxc, f2, 1.0, "norm_mod_bwd_mix")

        dxc, gbig[0], gbig[1], (dg1, dsc1, dsh1, dgt1) = ffn_bwd(
            dxc, norm_ff1[l:l + 1], sh1, sc1, gt1, u_all1, w_dn1, sv["ffn1"])

        send = [slots(ki, gbig[ki]) for ki in range(len(kinds))]
        started = exchange_start(send, slot_landing(send), dxc, False, f"a2a_start{l}")
        token = started[4]
        pending.append(started)
        dmods.append(jnp.concatenate([dsh1, dsc1, dgt1, dsh2, dsc2, dgt2, dsh3, dsc3, dgt3], axis=1))
        small.append((dg1, dg2, dg3, dconv, dqg, dkg, dal.reshape(1, nh), ddt.reshape(1, nh), dgn))
    dmods.reverse()
    small.reverse()
    recv_layers = [exchange_wait(*st[:4], dxc, False, f"a2a_wait{nl - 1 - i}") for i, st in enumerate(pending)]
    recv_layers.reverse()

    fields = [jnp.stack(dmods).reshape(-1)]
    fields += [jnp.stack([s[i] for s in small]).reshape(-1) for i in range(9)]
    fields.append(loss_blk[0, :1])
    fsizes = [f.size for f in fields]
    foffs = [sum(fsizes[:i]) for i in range(len(fields))]
    g1 = all_gather([_rows(jnp.concatenate(fields), 8)], "ag_small")[0].reshape(N_DEV, -1)

    def field(i, shape):
        return g1[:, foffs[i]:foffs[i] + fsizes[i]].reshape(N_DEV, *shape)

    loss = field(10, (1,))[0, 0]
    for j in range(1, N_DEV):
        loss = loss + field(10, (1,))[j, 0]

    results = {}
    dmod_all = field(0, (nl, N_ADA * d))
    c_pad = jnp.concatenate([c_all, jnp.zeros((8, d), F32)])
    dmod_mine = lax.dynamic_slice_in_dim(dmod_all, me * n_ada, n_ada, axis=2).transpose(1, 0, 2)
    dmod_pad = jnp.concatenate([dmod_mine, jnp.zeros((nl, 8, n_ada), F32)], axis=1)
    g_ada_w = ada_bwd(c_pad, dmod_pad, "ada_bwd")
    results["ada_w"] = adamw(ada_w, m_ada_w, v_ada_w, g_ada_w[None], "adamw_ada_w")
    results["ada_b"] = adamw(ada_b, m_ada_b, v_ada_b, dmod_all, "adamw_ada_b")
    results["norm_ff1"] = adamw(norm_ff1, m_norm_ff1, v_norm_ff1, field(1, (nl, d)), "adamw_norm_ff1")
    results["norm_mix"] = adamw(norm_mix, m_norm_mix, v_norm_mix, field(2, (nl, d)), "adamw_norm_mix")
    results["norm_ff2"] = adamw(norm_ff2, m_norm_ff2, v_norm_ff2, field(3, (nl, d)), "adamw_norm_ff2")
    conv_slots = lax.dynamic_slice_in_dim(field(4, (nl, CONV_WIDTH, 3 * dn_w)), me * conv_w.shape[2],
                                          conv_w.shape[2], axis=3)
    results["conv_w"] = adamw(conv_w, m_conv_w, v_conv_w, conv_slots, "adamw_conv_w")
    results["q_norm"] = adamw(q_norm, m_q_norm, v_q_norm, field(5, (nl, ATT_HEAD_DIM)), "adamw_q_norm")
    results["k_norm"] = adamw(k_norm, m_k_norm, v_k_norm, field(6, (nl, ATT_HEAD_DIM)), "adamw_k_norm")
    results["a_log"] = adamw(a_log, m_a_log, v_a_log, field(7, (nl, nh)), "adamw_a_log")
    results["dt_bias"] = adamw(dt_bias, m_dt_bias, v_dt_bias, field(8, (nl, nh)), "adamw_dt_bias")
    results["dn_norm"] = adamw(dn_norm, m_dn_norm, v_dn_norm, field(9, (nl, DN_HEAD_DIM)), "adamw_dn_norm")
    big_names = ["ffn1_w_up", "ffn1_w_down", "w_in", "w_proj_att", "w_proj_dn", "w_out", "ffn2_w_up", "ffn2_w_down"]
    big_m = [m_ffn1_w_up, m_ffn1_w_down, m_w_in, m_w_proj_att, m_w_proj_dn, m_w_out, m_ffn2_w_up, m_ffn2_w_down]
    big_v = [v_ffn1_w_up, v_ffn1_w_down, v_w_in, v_w_proj_att, v_w_proj_dn, v_w_out, v_ffn2_w_up, v_ffn2_w_down]
    for ki, nm in enumerate(big_names):
        results[nm] = adamw_layers(kinds[ki], big_m[ki], big_v[ki], [recv_layers[l][ki] for l in range(nl)],
                                   f"adamw_{nm}")

    order = ["ada_w", "ada_b", "norm_ff1", "ffn1_w_up", "ffn1_w_down", "norm_mix", "w_in", "q_norm", "k_norm",
             "conv_w", "a_log", "dt_bias", "dn_norm", "w_proj_att", "w_proj_dn", "w_out", "norm_ff2",
             "ffn2_w_up", "ffn2_w_down"]
    outs = [loss, dxc[None]]
    for part in range(4):
        outs += [results[n][part] for n in order]
    return tuple(outs)
```

```python
import functools

import jax
import jax.numpy as jnp
from jax import lax
from jax.experimental import pallas as pl
from jax.experimental.pallas import tpu as pltpu

F32 = jnp.float32
CD = jnp.bfloat16
EPS = 1e-6
N_DEV = 8
LANE = 128
ATT_HEAD_DIM = 64
ATT_BLOCK = 128
DILATIONS = (1, 4, 16)
DN_HEAD_DIM = 128
DN_CHUNK = 64
CONV_WIDTH = 4
N_ADA = 9
ADAM_LR, ADAM_B1, ADAM_B2, ADAM_EPS, ADAM_WD, ADAM_STEP = 0.001, 0.9, 0.999, 1e-08, 0.01, 10
VMEM_LIMIT = 56 * 1024 * 1024
NEG = -1e30
MESH = pl.DeviceIdType.MESH
HI = lax.Precision.HIGHEST


def _params(sem=None):
    return pltpu.CompilerParams(dimension_semantics=sem, vmem_limit_bytes=VMEM_LIMIT)


def _tile(n, pref, unit):
    best = None
    t = unit
    while t <= min(n, pref):
        if n % t == 0:
            best = t
        t += unit
    return best if best is not None else n


def _silu(x):
    return x * jax.nn.sigmoid(x)


def _dot(a, b, dims, precision=None):
    if precision is None:
        a, b = a.astype(CD), b.astype(CD)
    return lax.dot_general(a, b, (dims, ((), ())), precision=precision, preferred_element_type=F32)


def _nn(a, b, precision=None):
    return _dot(a, b, ((1,), (0,)), precision)


def _nt(a, b, precision=None):
    return _dot(a, b, ((1,), (1,)), precision)


def _tn(a, b, precision=None):
    return _dot(a, b, ((0,), (0,)), precision)


def all_gather(arrs, name):
    n = len(arrs)

    def body(*refs):
        x_refs, out_refs = refs[:n], refs[n:2 * n]
        send_sems, recv_sems, local_sems = refs[2 * n:]
        x, y, c = lax.axis_index("x"), lax.axis_index("y"), lax.axis_index("c")
        me, sibling = (x, y, c), (x, y, 1 - c)
        chips = [(1 - x, y), (x, 1 - y), (1 - x, 1 - y)]

        def copy(a, k, block, to, src=None):
            slot = out_refs[a].at[4 * block[0] + 2 * block[1] + block[2]]
            return pltpu.make_async_remote_copy(
                src_ref=slot if src is None else src, dst_ref=slot,
                send_sem=send_sems.at[7 * a + k], recv_sem=recv_sems.at[7 * a + k],
                device_id=to, device_id_type=MESH)

        mine, first, passed = [], [], []
        for a in range(n):
            cp = pltpu.make_async_copy(x_refs[a], out_refs[a].at[4 * x + 2 * y + c], local_sems.at[a])
            cp.start()
            mine.append(cp)
            first.append(copy(a, 0, me, sibling, src=x_refs[a]))
            first += [copy(a, 1 + j, me, (*chip, c), src=x_refs[a]) for j, chip in enumerate(chips)]
        for cp in first:
            cp.start()
        for j, chip in enumerate(chips):
            for a in range(n):
                copy(a, 1 + j, (*chip, c), me).wait_recv()
                cp = copy(a, 4 + j, (*chip, c), sibling)
                cp.start()
                passed.append(cp)
        for a in range(n):
            copy(a, 0, sibling, me).wait_recv()
            for j, chip in enumerate(chips):
                copy(a, 4 + j, (*chip, 1 - c), me).wait_recv()
        for cp in first + passed:
            cp.wait_send()
        for cp in mine:
            cp.wait()

    hbm = pl.BlockSpec(memory_space=pl.ANY)
    outs = pl.pallas_call(
        body, name=name,
        out_shape=[jax.ShapeDtypeStruct((N_DEV,) + a.shape, a.dtype) for a in arrs],
        in_specs=[hbm] * n, out_specs=[hbm] * n,
        scratch_shapes=[pltpu.SemaphoreType.DMA((7 * n,)), pltpu.SemaphoreType.DMA((7 * n,)),
                        pltpu.SemaphoreType.DMA((n,))],
    )(*arrs)
    return list(outs)


def all_to_all(arrs, name):
    n = len(arrs)

    def body(*refs):
        x_refs, out_refs = refs[:n], refs[n:2 * n]
        send_sems, recv_sems, local_sems = refs[2 * n:]
        x, y, c = lax.axis_index("x"), lax.axis_index("y"), lax.axis_index("c")
        me = 4 * x + 2 * y + c
        mine, sends = [], []
        for a in range(n):
            cp = pltpu.make_async_copy(x_refs[a].at[me], out_refs[a].at[me], local_sems.at[a])
            cp.start()
            mine.append(cp)
        peers = []
        for k in range(1, N_DEV):
            px, py, pc = x ^ (k >> 2), y ^ ((k >> 1) & 1), c ^ (k & 1)
            peers.append((px, py, pc, 4 * px + 2 * py + pc))
        for k in range(1, N_DEV):
            px, py, pc, pidx = peers[k - 1]
            for a in range(n):
                cp = pltpu.make_async_remote_copy(
                    src_ref=x_refs[a].at[pidx], dst_ref=out_refs[a].at[me],
                    send_sem=send_sems.at[7 * a + k - 1], recv_sem=recv_sems.at[7 * a + k - 1],
                    device_id=(px, py, pc), device_id_type=MESH)
                cp.start()
                sends.append(cp)
        for k in range(1, N_DEV):
            pidx = peers[k - 1][3]
            for a in range(n):
                pltpu.make_async_remote_copy(
                    src_ref=x_refs[a].at[pidx], dst_ref=out_refs[a].at[pidx],
                    send_sem=send_sems.at[7 * a + k - 1], recv_sem=recv_sems.at[7 * a + k - 1],
                    device_id=(x, y, c), device_id_type=MESH).wait_recv()
        for cp in sends:
            cp.wait_send()
        for cp in mine:
            cp.wait()

    hbm = pl.BlockSpec(memory_space=pl.ANY)
    outs = pl.pallas_call(
        body, name=name,
        out_shape=[jax.ShapeDtypeStruct(a.shape, a.dtype) for a in arrs],
        in_specs=[hbm] * n, out_specs=[hbm] * n,
        scratch_shapes=[pltpu.SemaphoreType.DMA((7 * n,)), pltpu.SemaphoreType.DMA((7 * n,)),
                        pltpu.SemaphoreType.DMA((n,))],
    )(*arrs)
    return list(outs)


_HBM = pl.BlockSpec(memory_space=pltpu.HBM)
_SEM = pl.BlockSpec(memory_space=pltpu.SEMAPHORE)
_EFFECT = pltpu.SideEffectType.DATAFLOW_SIDE_EFFECTING


def _exchange_copies(src_refs, land_refs, send_sem, recv_sem, gather):
    x, y, c = lax.axis_index("x"), lax.axis_index("y"), lax.axis_index("c")
    me = 4 * x + 2 * y + c
    pairs = []
    for k in range(1, N_DEV):
        px, py, pc = x ^ (k >> 2), y ^ ((k >> 1) & 1), c ^ (k & 1)
        pidx = 4 * px + 2 * py + pc
        for src, land in zip(src_refs, land_refs):
            mine = src if gather else src.at[pidx]
            out = pltpu.make_async_remote_copy(src_ref=mine, dst_ref=land.at[me], send_sem=send_sem, recv_sem=recv_sem,
                                               device_id=(px, py, pc), device_id_type=MESH)
            inc = pltpu.make_async_remote_copy(src_ref=mine, dst_ref=land.at[pidx], send_sem=send_sem, recv_sem=recv_sem,
                                               device_id=(px, py, pc), device_id_type=MESH)
            pairs.append((out, inc))
    return pairs


def exchange_start(srcs, lands, after, gather, name):
    n = len(srcs)

    def body(*refs):
        src_refs, land_refs = refs[:n], refs[n:2 * n]
        send_sem, recv_sem = refs[2 * n + 1], refs[2 * n + 2]
        token = refs[-1]
        for out, _ in _exchange_copies(src_refs, land_refs, send_sem, recv_sem, gather):
            out.start()
        token[...] = jnp.zeros_like(token)

    res = pl.pallas_call(
        body, name=name,
        out_shape=(pltpu.SemaphoreType.DMA(()), pltpu.SemaphoreType.DMA(()),
                   *[pltpu.HBM(a.shape, a.dtype) for a in srcs], *[pltpu.HBM(a.shape, a.dtype) for a in lands],
                   jax.ShapeDtypeStruct((8, LANE), F32)),
        in_specs=[_HBM] * (2 * n) + [pl.BlockSpec(memory_space=pl.ANY)],
        out_specs=(_SEM, _SEM, *[_HBM] * (2 * n), pl.BlockSpec(memory_space=pltpu.VMEM)),
        input_output_aliases={i: 2 + i for i in range(2 * n)},
        compiler_params=pltpu.CompilerParams(has_side_effects=_EFFECT),
    )(*[pltpu.with_memory_space_constraint(a, pltpu.HBM) for a in list(srcs) + list(lands)], after)
    return res[0], res[1], list(res[2:2 + n]), list(res[2 + n:2 + 2 * n]), res[-1]


def exchange_wait(send_sem, recv_sem, srcs, lands, after, gather, name):
    n = len(srcs)

    def body(*refs):
        src_refs, land_refs = refs[:n], refs[n:2 * n]
        s_sem, r_sem = refs[2 * n], refs[2 * n + 1]
        for out, inc in _exchange_copies(src_refs, land_refs, s_sem, r_sem, gather):
            out.wait_send()
            inc.wait_recv()

    res = pl.pallas_call(
        body, name=name,
        out_shape=[pltpu.HBM(a.shape, a.dtype) for a in list(srcs) + list(lands)],
        in_specs=[_HBM] * (2 * n) + [_SEM, _SEM, pl.BlockSpec(memory_space=pl.ANY)],
        out_specs=[_HBM] * (2 * n),
        input_output_aliases={i: i for i in range(2 * n)},
        compiler_params=pltpu.CompilerParams(has_side_effects=_EFFECT),
    )(*srcs, *lands, send_sem, recv_sem, after)
    return list(res[n:])


def matmul(pairs, mode, out_dtype, name, *, a_scale=None, out_scale=None, resid=None,
           save_acc=False, tm=512, tn=512, tk=2048):
    a0, b0 = pairs[0]
    if mode == "nn":
        (m, kdim), n = a0.shape, b0.shape[1]
    elif mode == "nt":
        (m, kdim), n = a0.shape, b0.shape[0]
    else:
        (kdim, m), n = a0.shape, b0.shape[1]
    tm, tn = _tile(m, tm, 8 if m % LANE else LANE), _tile(n, tn, LANE)
    tk = _tile(kdim, tk, LANE)
    nk = kdim // tk
    npairs = len(pairs)
    dims = {"nn": ((1,), (0,)), "nt": ((1,), (1,)), "tn": ((0,), (0,))}[mode]

    if mode == "nn":
        a_spec = pl.BlockSpec((tm, tk), lambda i, j, k: (i, k))
        b_spec = pl.BlockSpec((tk, tn), lambda i, j, k: (k, j))
    elif mode == "nt":
        a_spec = pl.BlockSpec((tm, tk), lambda i, j, k: (i, k))
        b_spec = pl.BlockSpec((tn, tk), lambda i, j, k: (j, k))
    else:
        a_spec = pl.BlockSpec((tk, tm), lambda i, j, k: (k, i))
        b_spec = pl.BlockSpec((tk, tn), lambda i, j, k: (k, j))

    def body(*refs):
        it = iter(refs)
        pair_refs = [(next(it), next(it)) for _ in range(npairs)]
        as_ref = next(it) if a_scale is not None else None
        os_ref = next(it) if out_scale is not None else None
        rs_ref = next(it) if resid is not None else None
        o_ref = next(it)
        acc_out = next(it) if save_acc else None
        acc_ref = next(it) if nk > 1 else None
        part = None
        for a_ref, b_ref in pair_refs:
            a = a_ref[...]
            if as_ref is not None:
                a = a.astype(F32) * (as_ref[...] if mode != "tn" else as_ref[...].reshape(tk, 1))
            d = _dot(a, b_ref[...], dims)
            part = d if part is None else part + d

        def finish(acc):
            if acc_out is not None:
                acc_out[...] = acc
            if os_ref is not None:
                acc = acc * os_ref[...]
            if rs_ref is not None:
                acc = rs_ref[...] + acc
            o_ref[...] = acc.astype(o_ref.dtype)

        if nk == 1:
            finish(part)
        else:
            k = pl.program_id(2)

            @pl.when(k == 0)
            def _():
                acc_ref[...] = part

            @pl.when(k > 0)
            def _():
                acc_ref[...] += part

            @pl.when(k == nk - 1)
            def _():
                finish(acc_ref[...])

    in_specs, args = [], []
    for a, b in pairs:
        in_specs += [a_spec, b_spec]
        args += [a, b]
    if a_scale is not None:
        assert mode != "tn"
        in_specs.append(pl.BlockSpec((1, tk), lambda i, j, k: (0, k)))
        args.append(a_scale)
    if out_scale is not None:
        in_specs.append(pl.BlockSpec((1, tn), lambda i, j, k: (0, j)))
        args.append(out_scale)
    if resid is not None:
        in_specs.append(pl.BlockSpec((tm, tn), lambda i, j, k: (i, j)))
        args.append(resid)
    o_spec = pl.BlockSpec((tm, tn), lambda i, j, k: (i, j))
    out_shape = [jax.ShapeDtypeStruct((m, n), out_dtype)]
    out_specs = [o_spec]
    if save_acc:
        out_shape.append(jax.ShapeDtypeStruct((m, n), F32))
        out_specs.append(o_spec)
    scratch = [pltpu.VMEM((tm, tn), F32)] if nk > 1 else []
    res = pl.pallas_call(
        body, name=name, grid=(m // tm, n // tn, nk),
        in_specs=in_specs, out_specs=out_specs, out_shape=out_shape, scratch_shapes=scratch,
        compiler_params=_params(("parallel", "parallel", "arbitrary")),
    )(*args)
    return res if save_acc else res[0]


def _nm_fn(x, g, sc, sh):
    r = lax.rsqrt(jnp.mean(x * x, axis=-1, keepdims=True) + EPS)
    return (x * r * g) * (1.0 + sc) + sh


def norm_mod(x, g, sc, sh, name):
    t, d = x.shape
    tr = _tile(t, 256, 8)

    def body(x_ref, g_ref, sc_ref, sh_ref, h_ref):
        h_ref[...] = _nm_fn(x_ref[...], g_ref[...], sc_ref[...], sh_ref[...]).astype(h_ref.dtype)

    row = pl.BlockSpec((tr, d), lambda i: (i, 0))
    vec = pl.BlockSpec((1, d), lambda i: (0, 0))
    return pl.pallas_call(
        body, name=name, grid=(t // tr,), in_specs=[row, vec, vec, vec], out_specs=row,
        out_shape=jax.ShapeDtypeStruct((t, d), CD), compiler_params=_params(("parallel",)),
    )(x, g, sc, sh)


def norm_mod_bwd(x, g, sc, sh, dh, dxo, f, gate_scale, name):
    t, d = x.shape
    tr = _tile(t, 256, 8)

    def body(x_ref, g_ref, sc_ref, sh_ref, dh_ref, dxo_ref, f_ref, dx_ref, dg_ref, dsc_ref, dsh_ref, dgt_ref):
        _, vjp = jax.vjp(_nm_fn, x_ref[...], g_ref[...], sc_ref[...], sh_ref[...])
        dx, dg, dsc, dsh = vjp(dh_ref[...])
        dxo_v = dxo_ref[...]
        dx_ref[...] = dxo_v + dx
        dgt = gate_scale * jnp.sum(f_ref[...] * dxo_v, axis=0, keepdims=True)

        @pl.when(pl.program_id(0) == 0)
        def _():
            dg_ref[...] = dg
            dsc_ref[...] = dsc
            dsh_ref[...] = dsh
            dgt_ref[...] = dgt

        @pl.when(pl.program_id(0) > 0)
        def _():
            dg_ref[...] += dg
            dsc_ref[...] += dsc
            dsh_ref[...] += dsh
            dgt_ref[...] += dgt

    row = pl.BlockSpec((tr, d), lambda i: (i, 0))
    vec = pl.BlockSpec((1, d), lambda i: (0, 0))
    vshape = jax.ShapeDtypeStruct((1, d), F32)
    return pl.pallas_call(
        body, name=name, grid=(t // tr,), in_specs=[row, vec, vec, vec, row, row, row],
        out_specs=[row, vec, vec, vec, vec],
        out_shape=[jax.ShapeDtypeStruct((t, d), F32), vshape, vshape, vshape, vshape],
        compiler_params=_params(("arbitrary",)),
    )(x, g, sc, sh, dh, dxo, f)


HALF = N_DEV // 2


def ffn_up(h, u_all, name):
    t, d = h.shape
    cp = u_all.shape[2]
    f = HALF * cp
    tm, tn = _tile(t, 1024, LANE), _tile(cp, 384, LANE)
    per = cp // tn

    def body(h_ref, wg_ref, wu_ref, g_ref, u_ref, a_ref):
        hv = h_ref[...]
        gate = _nn(hv, wg_ref[0])
        up = _nn(hv, wu_ref[0])
        g_ref[...] = gate.astype(g_ref.dtype)
        u_ref[...] = up.astype(u_ref.dtype)
        a_ref[...] = (_silu(gate) * up).astype(a_ref.dtype)

    o = pl.BlockSpec((tm, tn), lambda i, j: (i, j))
    wg = pl.BlockSpec((1, d, tn), lambda i, j: (j // per, 0, j % per))
    wu = pl.BlockSpec((1, d, tn), lambda i, j: (HALF + j // per, 0, j % per))
    return pl.pallas_call(
        body, name=name, grid=(t // tm, f // tn),
        in_specs=[pl.BlockSpec((tm, d), lambda i, j: (i, 0)), wg, wu], out_specs=[o, o, o],
        out_shape=[jax.ShapeDtypeStruct((t, f), CD)] * 3,
        compiler_params=_params(("parallel", "parallel")),
    )(h, u_all, u_all)


def ffn_up_wg(h, dgate, dup, cp, name):
    t, d = h.shape
    tm = _tile(d, 512, LANE)

    def body(h_ref, dg_ref, du_ref, o_ref):
        s = pl.program_id(0)

        @pl.when(s < HALF)
        def _():
            o_ref[0] = _tn(h_ref[...], dg_ref[...]).astype(o_ref.dtype)

        @pl.when(s >= HALF)
        def _():
            o_ref[0] = _tn(h_ref[...], du_ref[...]).astype(o_ref.dtype)

    return pl.pallas_call(
        body, name=name, grid=(N_DEV, d // tm),
        in_specs=[pl.BlockSpec((t, tm), lambda s, i: (0, i)),
                  pl.BlockSpec((t, cp), lambda s, i: (0, jnp.minimum(s, HALF - 1))),
                  pl.BlockSpec((t, cp), lambda s, i: (0, jnp.maximum(s - HALF, 0)))],
        out_specs=pl.BlockSpec((1, tm, cp), lambda s, i: (s, i, 0)),
        out_shape=jax.ShapeDtypeStruct((N_DEV, d, cp), CD),
        compiler_params=_params(("parallel", "parallel")),
    )(h, dgate, dup)


def ffn_up_dg(dgate, dup, u_all, name):
    t = dgate.shape[0]
    d, cp = u_all.shape[1], u_all.shape[2]
    tm, tn = _tile(t, 1024, LANE), _tile(d, 1024, LANE)

    def body(dg_ref, du_ref, u_ref, o_ref, acc_ref):
        s = pl.program_id(2)

        @pl.when(s == 0)
        def _():
            acc_ref[...] = jnp.zeros_like(acc_ref)

        @pl.when(s < HALF)
        def _():
            acc_ref[...] += _nt(dg_ref[...], u_ref[0])

        @pl.when(s >= HALF)
        def _():
            acc_ref[...] += _nt(du_ref[...], u_ref[0])

        @pl.when(s == N_DEV - 1)
        def _():
            o_ref[...] = acc_ref[...]

    return pl.pallas_call(
        body, name=name, grid=(t // tm, d // tn, N_DEV),
        in_specs=[pl.BlockSpec((tm, cp), lambda i, j, s: (i, jnp.minimum(s, HALF - 1))),
                  pl.BlockSpec((tm, cp), lambda i, j, s: (i, jnp.maximum(s - HALF, 0))),
                  pl.BlockSpec((1, tn, cp), lambda i, j, s: (s, j, 0))],
        out_specs=pl.BlockSpec((tm, tn), lambda i, j, s: (i, j)),
        out_shape=jax.ShapeDtypeStruct((t, d), F32),
        scratch_shapes=[pltpu.VMEM((tm, tn), F32)],
        compiler_params=_params(("parallel", "parallel", "arbitrary")),
    )(dgate, dup, u_all)


def ffn_dact(dxo, s, wd, gate, up, name):
    t, d = dxo.shape
    f = wd.shape[0]
    tm, tn = _tile(t, 1024, LANE), _tile(f, 384, LANE)

    def body(dxo_ref, s_ref, wd_ref, g_ref, u_ref, dg_ref, du_ref):
        da = _nt(dxo_ref[...] * s_ref[...], wd_ref[...])
        gate, up = g_ref[...].astype(F32), u_ref[...].astype(F32)
        sg = jax.nn.sigmoid(gate)
        dg_ref[...] = (da * up * sg * (1.0 + gate * (1.0 - sg))).astype(dg_ref.dtype)
        du_ref[...] = (da * gate * sg).astype(du_ref.dtype)

    o = pl.BlockSpec((tm, tn), lambda i, j: (i, j))
    return pl.pallas_call(
        body, name=name, grid=(t // tm, f // tn),
        in_specs=[pl.BlockSpec((tm, d), lambda i, j: (i, 0)), pl.BlockSpec((1, d), lambda i, j: (0, 0)),
                  pl.BlockSpec((tn, d), lambda i, j: (j, 0)), o, o],
        out_specs=[o, o],
        out_shape=[jax.ShapeDtypeStruct((t, f), CD), jax.ShapeDtypeStruct((t, f), CD)],
        compiler_params=_params(("parallel", "parallel")),
    )(dxo, s, wd, gate, up)


def _bdot(a, b, ca, cb, precision=None):
    if precision is None:
        a, b = a.astype(CD), b.astype(CD)
    return lax.dot_general(a, b, (((ca,), (cb,)), ((0,), (0,))), precision=precision, preferred_element_type=F32)


def _bnn(a, b, precision=None):
    return _bdot(a, b, 2, 1, precision)


def _bnt(a, b, precision=None):
    return _bdot(a, b, 2, 2, precision)


def _btn(a, b, precision=None):
    return _bdot(a, b, 1, 1, precision)


def _att_fn(q, kp, kc, vp, vc, qg, kg, mask_p, mask_c):
    def rn(x, g):
        return x * lax.rsqrt(jnp.mean(x * x, axis=-1, keepdims=True) + EPS) * g

    qn, kpn, kcn = rn(q, qg), rn(kp, kg), rn(kc, kg)
    scale = ATT_HEAD_DIM ** -0.5
    sp = jnp.where(mask_p, _bnt(qn, kpn) * scale, NEG)
    sc = jnp.where(mask_c, _bnt(qn, kcn) * scale, NEG)
    m = jnp.maximum(jnp.max(sp, axis=-1, keepdims=True), jnp.max(sc, axis=-1, keepdims=True))
    pp, pc = jnp.exp(sp - m), jnp.exp(sc - m)
    den = jnp.sum(pp, axis=-1, keepdims=True) + jnp.sum(pc, axis=-1, keepdims=True)
    o = _bnn(pp / den, vp) + _bnn(pc / den, vc)
    return o, m + jnp.log(den)


def _att_masks(g, j, nb_total):
    nb = jnp.int32(nb_total // DILATIONS[0])
    for gi in range(1, len(DILATIONS)):
        nb = jnp.where(g == gi, jnp.int32(nb_total // DILATIONS[gi]), nb)
    has_prev = (j % nb) != 0
    row = lax.broadcasted_iota(jnp.int32, (ATT_BLOCK, ATT_BLOCK), 0)
    col = lax.broadcasted_iota(jnp.int32, (ATT_BLOCK, ATT_BLOCK), 1)
    mask_p = jnp.logical_and(col >= row, has_prev)
    mask_c = col <= row
    return mask_p, mask_c


def _att_specs(nh):
    blk = (1, nh, ATT_BLOCK, ATT_HEAD_DIM)
    cur = pl.BlockSpec(blk, lambda g, j: (g, 0, j, 0))
    prev = pl.BlockSpec(blk, lambda g, j: (g, 0, jnp.maximum(j - 1, 0), 0))
    gain = pl.BlockSpec((1, ATT_HEAD_DIM), lambda g, j: (0, 0))
    lse = pl.BlockSpec((1, nh, ATT_BLOCK, 1), lambda g, j: (g, 0, j, 0))
    return cur, prev, gain, lse


def att_fwd(q, k, v, qg, kg, name):
    ng, nh, t, _ = q.shape
    nbt = t // ATT_BLOCK
    cur, prev, gain, lse_spec = _att_specs(nh)

    def body(q_ref, kp_ref, kc_ref, vp_ref, vc_ref, qg_ref, kg_ref, o_ref, lse_ref):
        mask_p, mask_c = _att_masks(pl.program_id(0), pl.program_id(1), nbt)
        o, lse = _att_fn(q_ref[0], kp_ref[0], kc_ref[0], vp_ref[0], vc_ref[0],
                         qg_ref[...], kg_ref[...], mask_p, mask_c)
        o_ref[0] = o
        lse_ref[0] = lse

    return pl.pallas_call(
        body, name=name, grid=(ng, nbt),
        in_specs=[cur, prev, cur, prev, cur, gain, gain], out_specs=[cur, lse_spec],
        out_shape=[jax.ShapeDtypeStruct(q.shape, F32), jax.ShapeDtypeStruct((ng, nh, t, 1), F32)],
        compiler_params=_params(("parallel", "parallel")),
    )(q, k, k, v, v, qg, kg)


def att_bwd(q, k, v, qg, kg, do, dlse, name):
    ng, nh, t, _ = q.shape
    nbt = t // ATT_BLOCK
    cur, prev, gain, lse_spec = _att_specs(nh)
    whole = pl.BlockSpec((1, nh, t, ATT_HEAD_DIM), lambda g, j: (g, 0, 0, 0))

    def body(q_ref, kp_ref, kc_ref, vp_ref, vc_ref, qg_ref, kg_ref, do_ref, dlse_ref,
             dq_ref, dk_ref, dv_ref, dqg_ref, dkg_ref):
        g, j = pl.program_id(0), pl.program_id(1)
        mask_p, mask_c = _att_masks(g, j, nbt)

        @pl.when(j == 0)
        def _():
            dk_ref[...] = jnp.zeros_like(dk_ref)
            dv_ref[...] = jnp.zeros_like(dv_ref)

        @pl.when(jnp.logical_and(g == 0, j == 0))
        def _():
            dqg_ref[...] = jnp.zeros_like(dqg_ref)
            dkg_ref[...] = jnp.zeros_like(dkg_ref)

        rows_c = pl.ds(pl.multiple_of(j * ATT_BLOCK, ATT_BLOCK), ATT_BLOCK)
        rows_p = pl.ds(pl.multiple_of(jnp.maximum(j - 1, 0) * ATT_BLOCK, ATT_BLOCK), ATT_BLOCK)
        fn = functools.partial(_att_fn, mask_p=mask_p, mask_c=mask_c)
        _, vjp = jax.vjp(fn, q_ref[0], kp_ref[0], kc_ref[0], vp_ref[0], vc_ref[0], qg_ref[...], kg_ref[...])
        dq, dkp, dkc, dvp, dvc, dqg, dkg = vjp((do_ref[0], dlse_ref[0]))
        dq_ref[0] = dq
        dk_ref[0, :, rows_p, :] += dkp
        dv_ref[0, :, rows_p, :] += dvp
        dk_ref[0, :, rows_c, :] += dkc
        dv_ref[0, :, rows_c, :] += dvc
        dqg_ref[...] += dqg
        dkg_ref[...] += dkg

    gshape = jax.ShapeDtypeStruct((1, ATT_HEAD_DIM), F32)
    return pl.pallas_call(
        body, name=name, grid=(ng, nbt),
        in_specs=[cur, prev, cur, prev, cur, gain, gain, cur, lse_spec],
        out_specs=[cur, whole, whole, gain, gain],
        out_shape=[jax.ShapeDtypeStruct(q.shape, F32)] * 3 + [gshape, gshape],
        compiler_params=_params(("arbitrary", "arbitrary")),
    )(q, k, k, v, v, qg, kg, do, dlse)


def _combine_fn(o, lse):
    m = jnp.max(lse, axis=0, keepdims=True)
    e = jnp.exp(lse - m)
    w = e / jnp.sum(e, axis=0, keepdims=True)
    return jnp.sum(w * o, axis=0)


def att_combine(o, lse, name):
    ng, nh, t, hd = o.shape
    tr = _tile(t, 256, 8)
    o_spec = pl.BlockSpec((ng, nh, tr, hd), lambda i: (0, 0, i, 0))
    l_spec = pl.BlockSpec((ng, nh, tr, 1), lambda i: (0, 0, i, 0))
    y_spec = pl.BlockSpec((nh, tr, hd), lambda i: (0, i, 0))

    def body(o_ref, l_ref, y_ref):
        y_ref[...] = _combine_fn(o_ref[...], l_ref[...])

    return pl.pallas_call(
        body, name=name, grid=(t // tr,), in_specs=[o_spec, l_spec], out_specs=y_spec,
        out_shape=jax.ShapeDtypeStruct((nh, t, hd), F32), compiler_params=_params(("parallel",)),
    )(o, lse)


def att_combine_bwd(o, lse, dy, name):
    ng, nh, t, hd = o.shape
    tr = _tile(t, 256, 8)
    o_spec = pl.BlockSpec((ng, nh, tr, hd), lambda i: (0, 0, i, 0))
    l_spec = pl.BlockSpec((ng, nh, tr, 1), lambda i: (0, 0, i, 0))
    y_spec = pl.BlockSpec((nh, tr, hd), lambda i: (0, i, 0))

    def body(o_ref, l_ref, dy_ref, do_ref, dl_ref):
        _, vjp = jax.vjp(_combine_fn, o_ref[...], l_ref[...])
        do, dl = vjp(dy_ref[...])
        do_ref[...] = do
        dl_ref[...] = dl

    return pl.pallas_call(
        body, name=name, grid=(t // tr,), in_specs=[o_spec, l_spec, y_spec], out_specs=[o_spec, l_spec],
        out_shape=[jax.ShapeDtypeStruct(o.shape, F32), jax.ShapeDtypeStruct(lse.shape, F32)],
        compiler_params=_params(("parallel",)),
    )(o, lse, dy)


def _shift_down(x, s):
    if s == 0:
        return x
    row = lax.broadcasted_iota(jnp.int32, x.shape, 0)
    return jnp.where(row >= s, pltpu.roll(x, s, 0), 0.0)


def _shift_up(x, s):
    if s == 0:
        return x
    t = x.shape[0]
    row = lax.broadcasted_iota(jnp.int32, x.shape, 0)
    return jnp.where(row < t - s, pltpu.roll(x, t - s, 0), 0.0)


def conv_fwd(z, col0, w, name):
    t = z.shape[0]
    c = w.shape[1]
    nblk0 = col0 // LANE

    def body(z_ref, w_ref, c_ref):
        zv = z_ref[...]
        acc = None
        for i in range(CONV_WIDTH):
            term = _shift_down(zv, CONV_WIDTH - 1 - i) * w_ref[i:i + 1, :]
            acc = term if acc is None else acc + term
        c_ref[...] = acc

    return pl.pallas_call(
        body, name=name, grid=(c // LANE,),
        in_specs=[pl.BlockSpec((t, LANE), lambda j: (0, nblk0 + j)), pl.BlockSpec((CONV_WIDTH, LANE), lambda j: (0, j))],
        out_specs=pl.BlockSpec((t, LANE), lambda j: (0, j)),
        out_shape=jax.ShapeDtypeStruct((t, c), F32), compiler_params=_params(("parallel",)),
    )(z, w)


def conv_bwd(dc, z, col0, w, name):
    t = z.shape[0]
    c = w.shape[1]
    nblk0 = col0 // LANE

    def body(dc_ref, z_ref, w_ref, dz_ref, dw_ref):
        dcv, zv = dc_ref[...], z_ref[...]
        acc = None
        for i in range(CONV_WIDTH):
            s = CONV_WIDTH - 1 - i
            term = _shift_up(dcv, s) * w_ref[i:i + 1, :]
            acc = term if acc is None else acc + term
            dw_ref[i:i + 1, :] = jnp.sum(dcv * _shift_down(zv, s), axis=0, keepdims=True)
        dz_ref[...] = acc.astype(dz_ref.dtype)

    blk = pl.BlockSpec((t, LANE), lambda j: (0, j))
    wblk = pl.BlockSpec((CONV_WIDTH, LANE), lambda j: (0, j))
    return pl.pallas_call(
        body, name=name, grid=(c // LANE,),
        in_specs=[blk, pl.BlockSpec((t, LANE), lambda j: (0, nblk0 + j)), wblk], out_specs=[blk, wblk],
        out_shape=[jax.ShapeDtypeStruct((t, c), CD), jax.ShapeDtypeStruct((CONV_WIDTH, c), F32)],
        compiler_params=_params(("parallel",)),
    )(dc, z, w)


def _dn_consts():
    c = DN_CHUNK
    row = lax.broadcasted_iota(jnp.int32, (c, c), 0)
    col = lax.broadcasted_iota(jnp.int32, (c, c), 1)
    return dict(tril=row >= col, strict=row > col, eye=(row == col).astype(F32),
                tril_f=(row >= col).astype(F32), triu_f=(row <= col).astype(F32))


def _softplus(x):
    return jnp.maximum(x, 0.0) + jnp.log(1.0 + jnp.exp(-jnp.abs(x)))


def _split(x):
    hi = x.astype(CD)
    return hi, (x - hi.astype(F32)).astype(CD)


def _bdot3(a, b, ca, cb):
    ah, al = _split(a)
    bh, bl = _split(b)
    return _bdot(ah, bh, ca, cb) + (_bdot(ah, bl, ca, cb) + _bdot(al, bh, ca, cb))


def _tri_inv_impl(a_mat):
    c = a_mat.shape[-1]
    eye = (lax.broadcasted_iota(jnp.int32, (c, c), 0) == lax.broadcasted_iota(jnp.int32, (c, c), 1)).astype(F32)
    nk_ = -a_mat
    t_inv = eye + nk_
    for _ in range(c.bit_length() - 2):
        nk_ = _bdot3(nk_, nk_, 2, 1)
        t_inv = t_inv + _bdot3(t_inv, nk_, 2, 1)
    return t_inv


@jax.custom_vjp
def _tri_inv(a_mat):
    return _tri_inv_impl(a_mat)


def _tri_inv_fwd(a_mat):
    t_inv = _tri_inv_impl(a_mat)
    return t_inv, t_inv


def _tri_inv_bwd(t_inv, dt_inv):
    return (-_bdot3(_bdot3(t_inv, dt_inv, 1, 1), t_inv, 2, 2),)


_tri_inv.defvjp(_tri_inv_fwd, _tri_inv_bwd)


def _dn_chunk(cq, ck, cv, og, a_col, b_col, al, dt, gn, s_prev, *, k, inv):
    q = _silu(cq)
    q = q * lax.rsqrt(jnp.sum(q * q, axis=-1, keepdims=True) + EPS) * (DN_HEAD_DIM ** -0.5)
    kk = _silu(ck)
    kk = kk * lax.rsqrt(jnp.sum(kk * kk, axis=-1, keepdims=True) + EPS)
    v = _silu(cv)
    g = -jnp.exp(al) * _softplus(a_col + dt)
    beta = jax.nn.sigmoid(b_col)
    g_row = jnp.sum(k["eye"] * g, axis=1, keepdims=True)
    gc_col = jnp.sum(k["tril_f"] * g_row, axis=2, keepdims=True)
    gc_row = jnp.sum(k["triu_f"] * g, axis=1, keepdims=True)
    ldec = jnp.where(k["tril"], jnp.exp(jnp.where(k["tril"], gc_col - gc_row, 0.0)), 0.0)
    kb, vb = kk * beta, v * beta
    a_mat = jnp.where(k["strict"], _bnt(kb, kk) * ldec, 0.0)
    t_inv = inv(a_mat)
    egc = jnp.exp(gc_col)
    u = _bnn(t_inv, vb)
    w = _bnn(t_inv, kb * egc)
    attn = jnp.where(k["tril"], _bnt(q, kk) * ldec, 0.0)
    gc_last = jnp.sum(g, axis=1, keepdims=True)
    k_dec = kk * jnp.exp(gc_last - gc_col)
    v_new = u - _bnn(w, s_prev)
    o = _bnn(q * egc, s_prev) + _bnn(attn, v_new)
    s_new = s_prev * jnp.exp(gc_last) + _btn(k_dec, v_new)
    y = o * lax.rsqrt(jnp.mean(o * o, axis=-1, keepdims=True) + EPS) * gn * _silu(og)
    return y, s_new


def _dn_heads(ref, nh):
    hd = DN_HEAD_DIM
    return jnp.stack([ref[:, h * hd:(h + 1) * hd] for h in range(nh)])


def _dn_specs(nh, col_gate, col_ab, order):
    hd, c = DN_HEAD_DIM, DN_CHUNK
    w = nh * hd
    qs = pl.BlockSpec((c, w), lambda n: (order(n), 0))
    ks = pl.BlockSpec((c, w), lambda n: (order(n), 1))
    vs = pl.BlockSpec((c, w), lambda n: (order(n), 2))
    gs = pl.BlockSpec((c, w), lambda n: (order(n), col_gate // w))
    ab = pl.BlockSpec((c, LANE), lambda n: (order(n), col_ab // LANE))
    scal = pl.BlockSpec((nh, 1, 1), lambda n: (0, 0, 0))
    gn = pl.BlockSpec((1, hd), lambda n: (0, 0))
    st = pl.BlockSpec((1, nh, hd, hd), lambda n: (order(n), 0, 0, 0))
    return qs, ks, vs, gs, ab, scal, gn, st


def _lane_pick(x, idx):
    lane = lax.broadcasted_iota(jnp.int32, x.shape, 1)
    return jnp.sum(jnp.where(lane == idx, x, 0.0), axis=1, keepdims=True)


def dn_fwd(cv, z, col_gate, col_ab, a_log, dt_bias, gn, name):
    t = cv.shape[0]
    nh = a_log.shape[0]
    hd, c = DN_HEAD_DIM, DN_CHUNK
    n_chunks = t // c
    qs, ks, vs, gs, ab, scal, gnspec, st = _dn_specs(nh, col_gate, col_ab, lambda n: n)

    def body(q_ref, k_ref, v_ref, g_ref, ab_ref, al_ref, dt_ref, gn_ref, y_ref, st_ref, s_scr):
        @pl.when(pl.program_id(0) == 0)
        def _():
            s_scr[...] = jnp.zeros_like(s_scr)

        abv = ab_ref[...]
        a_col = jnp.stack([_lane_pick(abv, h) for h in range(nh)])
        b_col = jnp.stack([_lane_pick(abv, nh + h) for h in range(nh)])
        s_prev = s_scr[...]
        st_ref[0] = s_prev
        y, s_new = _dn_chunk(_dn_heads(q_ref, nh), _dn_heads(k_ref, nh), _dn_heads(v_ref, nh), _dn_heads(g_ref, nh),
                             a_col, b_col, al_ref[...], dt_ref[...], gn_ref[...], s_prev,
                             k=_dn_consts(), inv=_tri_inv_impl)
        for h in range(nh):
            y_ref[:, h * hd:(h + 1) * hd] = y[h].astype(y_ref.dtype)
        s_scr[...] = s_new

    return pl.pallas_call(
        body, name=name, grid=(n_chunks,),
        in_specs=[qs, ks, vs, gs, ab, scal, scal, gnspec],
        out_specs=[pl.BlockSpec((c, nh * hd), lambda n: (n, 0)), st],
        out_shape=[jax.ShapeDtypeStruct((t, nh * hd), CD), jax.ShapeDtypeStruct((n_chunks, nh, hd, hd), F32)],
        scratch_shapes=[pltpu.VMEM((nh, hd, hd), F32)],
        compiler_params=_params(("arbitrary",)),
    )(cv, cv, cv, z, z, a_log, dt_bias, gn)


def dn_bwd(cv, z, col_gate, col_ab, a_log, dt_bias, gn, states, dy, name):
    t = cv.shape[0]
    nh = a_log.shape[0]
    hd, c = DN_HEAD_DIM, DN_CHUNK
    w = nh * hd
    n_chunks = t // c
    rev = lambda n: n_chunks - 1 - n
    qs, ks, vs, gs, ab, scal, gnspec, st = _dn_specs(nh, col_gate, col_ab, rev)
    yspec = pl.BlockSpec((c, w), lambda n: (rev(n), 0))

    def body(q_ref, k_ref, v_ref, g_ref, ab_ref, al_ref, dt_ref, gn_ref, st_ref, dy_ref,
             dc_ref, dg_ref, dab_ref, dal_ref, ddt_ref, dgn_ref, ds_scr):
        @pl.when(pl.program_id(0) == 0)
        def _():
            ds_scr[...] = jnp.zeros_like(ds_scr)
            dal_ref[...] = jnp.zeros_like(dal_ref)
            ddt_ref[...] = jnp.zeros_like(ddt_ref)
            dgn_ref[...] = jnp.zeros_like(dgn_ref)

        abv = ab_ref[...]
        a_col = jnp.stack([_lane_pick(abv, h) for h in range(nh)])
        b_col = jnp.stack([_lane_pick(abv, nh + h) for h in range(nh)])
        fn = functools.partial(_dn_chunk, k=_dn_consts(), inv=_tri_inv)
        _, vjp = jax.vjp(fn, _dn_heads(q_ref, nh), _dn_heads(k_ref, nh), _dn_heads(v_ref, nh), _dn_heads(g_ref, nh),
                         a_col, b_col, al_ref[...], dt_ref[...], gn_ref[...], st_ref[0])
        dq, dk, dv, dg, da, db, dal, ddt, dgn, ds = vjp((_dn_heads(dy_ref, nh), ds_scr[...]))
        lane = lax.broadcasted_iota(jnp.int32, (c, LANE), 1)
        dab = jnp.zeros((c, LANE), F32)
        for h in range(nh):
            cols = slice(h * hd, (h + 1) * hd)
            dc_ref[:, cols] = dq[h]
            dc_ref[:, w + h * hd:w + (h + 1) * hd] = dk[h]
            dc_ref[:, 2 * w + h * hd:2 * w + (h + 1) * hd] = dv[h]
            dg_ref[:, cols] = dg[h].astype(dg_ref.dtype)
            dab = dab + jnp.where(lane == h, da[h], 0.0) + jnp.where(lane == nh + h, db[h], 0.0)
        dab_ref[...] = dab.astype(dab_ref.dtype)
        dal_ref[...] += dal
        ddt_ref[...] += ddt
        dgn_ref[...] += dgn
        ds_scr[...] = ds

    sshape = jax.ShapeDtypeStruct((nh, 1, 1), F32)
    res = pl.pallas_call(
        body, name=name, grid=(n_chunks,),
        in_specs=[qs, ks, vs, gs, ab, scal, scal, gnspec, st, yspec],
        out_specs=[pl.BlockSpec((c, 3 * w), lambda n: (rev(n), 0)), yspec,
                   pl.BlockSpec((c, LANE), lambda n: (rev(n), 0)), scal, scal, gnspec],
        out_shape=[jax.ShapeDtypeStruct((t, 3 * w), F32), jax.ShapeDtypeStruct((t, w), CD),
                   jax.ShapeDtypeStruct((t, LANE), CD), sshape, sshape, jax.ShapeDtypeStruct((1, hd), F32)],
        scratch_shapes=[pltpu.VMEM((nh, hd, hd), F32)],
        compiler_params=_params(("arbitrary",)),
    )(cv, cv, cv, z, z, a_log, dt_bias, gn, states, dy)
    return res


def merge_fwd(ya, wpa, yd, wpd, z, col_m, name):
    t, d = yd.shape[0], wpd.shape[1]
    tm, tn = _tile(t, 512, LANE), _tile(d, 256, LANE)
    nb1, nb2 = col_m // tn, (col_m + d) // tn

    def body(ya_ref, wpa_ref, yd_ref, wpd_ref, z1_ref, z2_ref, m_ref, pa_ref, pd_ref):
        pa = _nn(ya_ref[...], wpa_ref[...])
        pd = _nn(yd_ref[...], wpd_ref[...])
        pa_ref[...] = pa
        pd_ref[...] = pd
        m_ref[...] = (jax.nn.sigmoid(z1_ref[...]) * pa + jax.nn.sigmoid(z2_ref[...]) * pd).astype(m_ref.dtype)

    o = pl.BlockSpec((tm, tn), lambda i, j: (i, j))
    return pl.pallas_call(
        body, name=name, grid=(t // tm, d // tn),
        in_specs=[pl.BlockSpec((tm, ya.shape[1]), lambda i, j: (i, 0)),
                  pl.BlockSpec((wpa.shape[0], tn), lambda i, j: (0, j)),
                  pl.BlockSpec((tm, yd.shape[1]), lambda i, j: (i, 0)),
                  pl.BlockSpec((wpd.shape[0], tn), lambda i, j: (0, j)),
                  pl.BlockSpec((tm, tn), lambda i, j: (i, nb1 + j)),
                  pl.BlockSpec((tm, tn), lambda i, j: (i, nb2 + j))],
        out_specs=[o, o, o],
        out_shape=[jax.ShapeDtypeStruct((t, d), CD), jax.ShapeDtypeStruct((t, d), F32),
                   jax.ShapeDtypeStruct((t, d), F32)],
        compiler_params=_params(("parallel", "parallel")),
    )(ya, wpa, yd, wpd, z, z)


def merge_bwd(dm, pa, pd, z, col_m, name):
    t, d = dm.shape
    tm, tn = _tile(t, 512, 8), _tile(d, 256, LANE)
    nb1, nb2 = col_m // tn, (col_m + d) // tn

    def body(dm_ref, pa_ref, pd_ref, z1_ref, z2_ref, dpa_ref, dpd_ref, dz1_ref, dz2_ref):
        dmv = dm_ref[...]
        s1, s2 = jax.nn.sigmoid(z1_ref[...]), jax.nn.sigmoid(z2_ref[...])
        dpa_ref[...] = (dmv * s1).astype(dpa_ref.dtype)
        dpd_ref[...] = (dmv * s2).astype(dpd_ref.dtype)
        dz1_ref[...] = (dmv * pa_ref[...] * s1 * (1.0 - s1)).astype(dz1_ref.dtype)
        dz2_ref[...] = (dmv * pd_ref[...] * s2 * (1.0 - s2)).astype(dz2_ref.dtype)

    o = pl.BlockSpec((tm, tn), lambda i, j: (i, j))
    sh = jax.ShapeDtypeStruct((t, d), CD)
    return pl.pallas_call(
        body, name=name, grid=(t // tm, d // tn),
        in_specs=[o, o, o, pl.BlockSpec((tm, tn), lambda i, j: (i, nb1 + j)),
                  pl.BlockSpec((tm, tn), lambda i, j: (i, nb2 + j))],
        out_specs=[o, o, o, o], out_shape=[sh, sh, sh, sh],
        compiler_params=_params(("parallel", "parallel")),
    )(dm, pa, pd, z, z)


def ada_fwd(c_all, w, b, name):
    nl, d, n = w.shape
    tn = _tile(n, 384, LANE)

    def body(c_ref, w_ref, b_ref, o_ref):
        o_ref[0] = _nn(_silu(c_ref[...]), w_ref[0]) + b_ref[0]

    return pl.pallas_call(
        body, name=name, grid=(nl, n // tn),
        in_specs=[pl.BlockSpec(c_all.shape, lambda l, j: (0, 0)), pl.BlockSpec((1, d, tn), lambda l, j: (l, 0, j)),
                  pl.BlockSpec((1, 1, tn), lambda l, j: (l, 0, j))],
        out_specs=pl.BlockSpec((1, c_all.shape[0], tn), lambda l, j: (l, 0, j)),
        out_shape=jax.ShapeDtypeStruct((nl, c_all.shape[0], n), F32),
        compiler_params=_params(("parallel", "parallel")),
    )(c_all, w, b)


def ada_bwd(c_pad, dmod_pad, name):
    nl, kp, n = dmod_pad.shape
    d = c_pad.shape[1]
    tn = _tile(n, 384, LANE)

    def body(c_ref, g_ref, o_ref):
        o_ref[0] = _tn(_silu(c_ref[...]), g_ref[0])

    return pl.pallas_call(
        body, name=name, grid=(nl, n // tn),
        in_specs=[pl.BlockSpec((kp, d), lambda l, j: (0, 0)), pl.BlockSpec((1, kp, tn), lambda l, j: (l, 0, j))],
        out_specs=pl.BlockSpec((1, d, tn), lambda l, j: (l, 0, j)),
        out_shape=jax.ShapeDtypeStruct((nl, d, n), F32),
        compiler_params=_params(("parallel", "parallel")),
    )(c_pad, dmod_pad)


def loss_head(y, target, name):
    t, d = y.shape
    tr = _tile(t, 256, 8)

    def body(y_ref, t_ref, dy_ref, l_ref):
        err = y_ref[...] - t_ref[...]
        dy_ref[...] = err * (1.0 / d)
        part = jnp.sum(jnp.sum(err * err, axis=1, keepdims=True), axis=0, keepdims=True) * (0.5 / d)

        @pl.when(pl.program_id(0) == 0)
        def _():
            l_ref[...] = jnp.zeros_like(l_ref)

        l_ref[...] += part

    row = pl.BlockSpec((tr, d), lambda i: (i, 0))
    return pl.pallas_call(
        body, name=name, grid=(t // tr,), in_specs=[row, row],
        out_specs=[row, pl.BlockSpec((8, LANE), lambda i: (0, 0))],
        out_shape=[jax.ShapeDtypeStruct((t, d), F32), jax.ShapeDtypeStruct((8, LANE), F32)],
        compiler_params=_params(("arbitrary",)),
    )(y, target)


def _adamw_update(g, w_ref, m_ref, v_ref, go_ref, d_ref, mo_ref, vo_ref):
    m_new = ADAM_B1 * m_ref[...] + (1.0 - ADAM_B1) * g
    v_new = ADAM_B2 * v_ref[...] + (1.0 - ADAM_B2) * jnp.square(g)
    m_hat = m_new / (1.0 - ADAM_B1 ** ADAM_STEP)
    v_hat = v_new / (1.0 - ADAM_B2 ** ADAM_STEP)
    go_ref[...] = g
    d_ref[...] = -ADAM_LR * (m_hat / (jnp.sqrt(v_hat) + ADAM_EPS) + ADAM_WD * w_ref[...])
    mo_ref[...] = m_new
    vo_ref[...] = v_new


def adamw(w, m, v, g_slots, name):
    shape = w.shape
    nslot = g_slots.shape[0]
    if w.ndim == 2:
        w3, m3, v3, g4 = w[None], m[None], v[None], g_slots[:, None]
    else:
        w3, m3, v3, g4 = w, m, v, g_slots
    nl, r, c = w3.shape
    tr = _tile(r, 256, 16)

    def body(w_ref, m_ref, v_ref, g_ref, *outs):
        g = g_ref[0].astype(F32)
        for s in range(1, nslot):
            g = g + g_ref[s].astype(F32)
        _adamw_update(g, w_ref, m_ref, v_ref, *outs)

    blk = pl.BlockSpec((1, tr, c), lambda l, i: (l, i, 0))
    gblk = pl.BlockSpec((nslot, 1, tr, c), lambda l, i: (0, l, i, 0))
    sh = jax.ShapeDtypeStruct(w3.shape, F32)
    outs = pl.pallas_call(
        body, name=name, grid=(nl, r // tr), in_specs=[blk, blk, blk, gblk], out_specs=[blk] * 4,
        out_shape=[sh] * 4, compiler_params=_params(("parallel", "parallel")),
    )(w3, m3, v3, g4)
    return tuple(o.reshape(shape) for o in outs)


def adamw_layers(w, m, v, g_layers, name):
    nl, r, c = w.shape
    nslot, _, cg = g_layers[0].shape
    tr = _tile(r, 256, 16)

    def body(w_ref, m_ref, v_ref, *rest):
        g_refs, outs = rest[:nl], rest[nl:]
        for li in range(nl):
            @pl.when(pl.program_id(0) == li)
            def _(g_ref=g_refs[li]):
                g = g_ref[0, :, :c].astype(F32)
                for s in range(1, nslot):
                    g = g + g_ref[s, :, :c].astype(F32)
                _adamw_update(g[None], w_ref, m_ref, v_ref, *outs)

    blk = pl.BlockSpec((1, tr, c), lambda l, i: (l, i, 0))
    gblks = [pl.BlockSpec((nslot, tr, cg), lambda l, i, li=li: (0, jnp.where(l == li, i, 0), 0)) for li in range(nl)]
    sh = jax.ShapeDtypeStruct(w.shape, F32)
    return tuple(pl.pallas_call(
        body, name=name, grid=(nl, r // tr), in_specs=[blk, blk, blk] + gblks, out_specs=[blk] * 4,
        out_shape=[sh] * 4, compiler_params=_params(("arbitrary", "arbitrary")),
    )(w, m, v, *g_layers))


def _rows(flat, unit=16):
    n = flat.shape[0]
    per = 1024 * unit
    pad = (-n) % per
    if pad:
        flat = jnp.concatenate([flat, jnp.zeros((pad,), flat.dtype)])
    return flat.reshape(-1, 1024)


def _to_classes(a, t):
    outs = []
    for gi, dil in enumerate(DILATIONS):
        x = a[:, gi]
        x = x.reshape(t // dil, dil, *x.shape[1:]).transpose(2, 1, 0, 3)
        outs.append(x.reshape(x.shape[0], t, x.shape[3]))
    return jnp.stack(outs)


def _from_classes(a, t):
    outs = []
    for gi, dil in enumerate(DILATIONS):
        x = a[gi]
        x = x.reshape(x.shape[0], dil, t // dil, x.shape[2]).transpose(2, 1, 0, 3)
        outs.append(x.reshape(t, x.shape[2], x.shape[3]))
    return jnp.stack(outs, axis=1)


def _unshard_cols(g):
    return g.transpose(1, 0, 2).reshape(g.shape[1], -1)


def _shard_cols(full):
    r = full.shape[0]
    return full.reshape(r, N_DEV, -1).transpose(1, 0, 2)


def kernel(x, c, ada_w, ada_b, norm_ff1, ffn1_w_up, ffn1_w_down, norm_mix, w_in, q_norm, k_norm, conv_w, a_log, dt_bias, dn_norm, w_proj_att, w_proj_dn, w_out, norm_ff2, ffn2_w_up, ffn2_w_down, loss_target, m_ada_w, m_ada_b, m_norm_ff1, m_ffn1_w_up, m_ffn1_w_down, m_norm_mix, m_w_in, m_q_norm, m_k_norm, m_conv_w, m_a_log, m_dt_bias, m_dn_norm, m_w_proj_att, m_w_proj_dn, m_w_out, m_norm_ff2, m_ffn2_w_up, m_ffn2_w_down, v_ada_w, v_ada_b, v_norm_ff1, v_ffn1_w_up, v_ffn1_w_down, v_norm_mix, v_w_in, v_q_norm, v_k_norm, v_conv_w, v_a_log, v_dt_bias, v_dn_norm, v_w_proj_att, v_w_proj_dn, v_w_out, v_norm_ff2, v_ffn2_w_up, v_ffn2_w_down):
    nl = ada_w.shape[0]
    t, d = x.shape[1], x.shape[2]
    dff = ffn1_w_down.shape[1] * N_DEV
    ha = d // 256
    ng = len(DILATIONS)
    att_w = ng * ha * ATT_HEAD_DIM
    nh = d // DN_HEAD_DIM
    dn_w = nh * DN_HEAD_DIM
    n_in = w_in.shape[2] * N_DEV
    off_dn, off_gate = 3 * att_w, 3 * att_w + 3 * dn_w
    off_a = off_gate + dn_w
    off_merge = off_a + 2 * nh
    assert off_merge + 2 * d == n_in
    col_dn, col_gate, col_m = 0, 3 * dn_w, 4 * dn_w
    col_att = col_m + 2 * d
    col_ab = col_att + 3 * att_w
    zw = col_ab + 2 * LANE
    me = 4 * lax.axis_index("x") + 2 * lax.axis_index("y") + lax.axis_index("c")
    xs = x[0]
    target = loss_target[0]

    conv_rows = _rows(conv_w.reshape(-1), 8)
    pack0 = jnp.concatenate([jnp.concatenate([c, jnp.zeros((7, d), F32)]).reshape(-1), conv_rows.reshape(-1)])
    pack0 = _rows(pack0, 8)
    g0 = all_gather([pack0], "ag_c_conv")[0].reshape(N_DEV, -1)
    c_all = g0[:, :d]
    cw = g0[:, 8 * d:8 * d + conv_w.size].reshape(N_DEV, nl, CONV_WIDTH, -1)
    conv_full = cw.transpose(1, 2, 0, 3).reshape(nl, CONV_WIDTH, 3 * dn_w)

    n_ada = ada_w.shape[2]
    b_mine = lax.dynamic_slice_in_dim(ada_b, me * n_ada, n_ada, axis=1)[:, None, :]
    mod_s = ada_fwd(c_all, ada_w, b_mine, "ada_fwd")
    gm = all_gather([mod_s], "ag_mod")[0]
    mod = lax.dynamic_index_in_dim(gm, me, axis=2, keepdims=False)
    mod = mod.transpose(1, 0, 2).reshape(nl, N_ADA, 1, d)

    kinds = [ffn1_w_up, ffn1_w_down, w_in, w_proj_att, w_proj_dn, w_out, ffn2_w_up, ffn2_w_down]
    c_up = ffn1_w_up.shape[2]
    cp = -(-c_up // LANE) * LANE
    r_dn = ffn1_w_down.shape[1]
    assert 2 * r_dn == c_up

    def layer_shards(l):
        def pad_up(w):
            return jnp.pad(w.astype(CD), ((0, 0), (0, cp - c_up)))
        return [pad_up(ffn1_w_up[l]), ffn1_w_down[l].astype(CD), w_in[l].astype(CD), w_proj_att[l].astype(CD),
                w_proj_dn[l].astype(CD), w_out[l].astype(CD), pad_up(ffn2_w_up[l]), ffn2_w_down[l].astype(CD)]

    def gather_landing(shards):
        return [lax.dynamic_update_index_in_dim(lax.empty((N_DEV,) + s.shape, s.dtype), s, me, 0) for s in shards]

    def slot_landing(arrs):
        return [lax.dynamic_update_index_in_dim(lax.empty(a.shape, a.dtype),
                                                lax.dynamic_index_in_dim(a, me, 0, keepdims=False), me, 0) for a in arrs]

    gathered = all_gather(layer_shards(0), "ag_weights0")
    prefetch = None

    def full_weight(ki, l):
        blk = gathered[ki]
        if ki in (2, 3):
            return _unshard_cols(blk)
        if ki in (1, 7):
            pairs = blk.reshape(HALF, c_up, d)
            return jnp.pad(pairs, ((0, 0), (0, cp - c_up), (0, 0))).reshape(HALF * cp, d)
        return blk.reshape(-1, blk.shape[2])

    def down_grad_slots(g):
        return g.reshape(HALF, cp, d)[:, :c_up].reshape(N_DEV, r_dn, d)

    def w_in_cols(w):
        pad = jnp.zeros((w.shape[0], zw - n_in), w.dtype)
        return jnp.concatenate([w[:, off_dn:off_a], w[:, off_merge:], w[:, :off_dn], w[:, off_a:off_merge], pad], axis=1)

    def w_in_cols_inv(g):
        return jnp.concatenate([g[:, col_att:col_ab], g[:, :col_m], g[:, col_ab:col_ab + 2 * nh], g[:, col_m:col_att]], axis=1)

    saved = []
    xc = xs
    mods = []
    for l in range(nl):
        sv = {}
        if prefetch is not None:
            gathered = exchange_wait(*prefetch[:4], xc, True, f"ag_wait{l}")
        mod_l = mod[l]
        if l + 1 < nl:
            shards = layer_shards(l + 1)
            behind = gathered[0][0, :1, :1].astype(F32) + mod[0, 0, :, :1]
            prefetch = exchange_start(shards, gather_landing(shards), behind, True, f"ag_start{l + 1}")
            mod_l = mod_l + prefetch[4][0, 0]
        mods.append(mod_l)
        sh1, sc1, gt1, sh2, sc2, gt2, sh3, sc3, gt3 = [mod_l[i] for i in range(N_ADA)]
        w_dn1, w_dn2 = full_weight(1, l), full_weight(7, l)
        win = w_in_cols(full_weight(2, l))
        wpa, wpd, wo = full_weight(3, l), full_weight(4, l), full_weight(5, l)
        sv["w"] = (gathered[0], gathered[6], w_dn1, w_dn2, win, wpa, wpd, wo)

        def ffn(xin, g, sh, sc, gt, u_all, w_dn):
            h = norm_mod(xin, g, sc, sh, "norm_mod")
            gate, up, a = ffn_up(h, u_all, "ffn_up")
            xo, f = matmul([(a, w_dn)], "nn", F32, "ffn_down", out_scale=0.5 * gt, resid=xin, save_acc=True,
                           tm=1024, tk=4096)
            return xo, (xin, h, gate, up, a, f)

        xc, sv["ffn1"] = ffn(xc, norm_ff1[l:l + 1], sh1, sc1, gt1, gathered[0], w_dn1)

        x_mix = xc
        h2 = norm_mod(x_mix, norm_mix[l:l + 1], sc2, sh2, "norm_mod")
        z = matmul([(h2, win)], "nn", F32, "w_in", tm=1024, tn=512)
        qkv = z[:, col_att:col_ab].reshape(t, 3, ng, ha, ATT_HEAD_DIM)
        qp, kp, vp = (_to_classes(qkv[:, i], t) for i in range(3))
        qg, kg = q_norm[l:l + 1], k_norm[l:l + 1]
        o_cls, lse_cls = att_fwd(qp, kp, vp, qg, kg, "att_fwd")
        o_tok = _from_classes(o_cls, t).transpose(1, 2, 0, 3)
        lse_tok = _from_classes(lse_cls, t).transpose(1, 2, 0, 3)
        ya_h = att_combine(o_tok, lse_tok, "att_combine")
        ya = ya_h.transpose(1, 0, 2).reshape(t, ha * ATT_HEAD_DIM).astype(CD)
        cvo = conv_fwd(z, col_dn, conv_full[l], "conv_fwd")
        al3, dt3 = a_log[l].reshape(nh, 1, 1), dt_bias[l].reshape(nh, 1, 1)
        gn = dn_norm[l:l + 1]
        yd, states = dn_fwd(cvo, z, col_gate, col_ab, al3, dt3, gn, "dn_fwd")
        mrg, pa, pd = merge_fwd(ya, wpa, yd, wpd, z, col_m, "merge_fwd")
        xc, f2 = matmul([(mrg, wo)], "nn", F32, "w_out", out_scale=gt2, resid=x_mix, save_acc=True)
        sv["mix"] = (x_mix, h2, z, qp, kp, vp, o_tok, lse_tok, ya, cvo, yd, states, mrg, pa, pd, f2)

        xc, sv["ffn2"] = ffn(xc, norm_ff2[l:l + 1], sh3, sc3, gt3, gathered[6], w_dn2)
        saved.append(sv)

    dxc, loss_blk = loss_head(xc, target, "loss_head")

    def slots(ki, g):
        if ki in (0, 1, 6, 7):
            return g
        return _shard_cols(g) if ki in (2, 3) else g.reshape(N_DEV, -1, g.shape[1])

    gbig = [None] * len(kinds)
    dmods, small, pending = [], [], []
    token = None
    for l in reversed(range(nl)):
        sv = saved[l]
        mod_l = mods[l] if token is None else mods[l] + token[0, 0]
        sh1, sc1, gt1, sh2, sc2, gt2, sh3, sc3, gt3 = [mod_l[i] for i in range(N_ADA)]
        u_all1, u_all2, w_dn1, w_dn2, win, wpa, wpd, wo = sv["w"]

        def ffn_bwd(dxo, g, sh, sc, gt, u_all, w_dn, sv_f):
            xin, h, gate, up, a, f = sv_f
            s = 0.5 * gt
            g_dn = down_grad_slots(matmul([(a, dxo)], "tn", CD, "ffn_down_wg", out_scale=s, tm=1024, tn=1024))
            dgate, dup = ffn_dact(dxo, s, w_dn, gate, up, "ffn_dact")
            g_up = ffn_up_wg(h, dgate, dup, cp, "ffn_up_wg")
            dh = ffn_up_dg(dgate, dup, u_all, "ffn_up_dg")
            dx, dg, dsc, dsh, dgt = norm_mod_bwd(xin, g, sc, sh, dh, dxo, f, 0.5, "norm_mod_bwd")
            return dx, g_up, g_dn, (dg, dsc, dsh, dgt)

        dxc, gbig[6], gbig[7], (dg3, dsc3, dsh3, dgt3) = ffn_bwd(
            dxc, norm_ff2[l:l + 1], sh3, sc3, gt3, u_all2, w_dn2, sv["ffn2"])

        x_mix, h2, z, qp, kp, vp, o_tok, lse_tok, ya, cvo, yd, states, mrg, pa, pd, f2 = sv["mix"]
        gbig[5] = matmul([(mrg, dxc)], "tn", CD, "w_out_wg", out_scale=gt2)
        dm = matmul([(dxc, wo)], "nt", F32, "w_out_dg", a_scale=gt2)
        dpa, dpd, dz1, dz2 = merge_bwd(dm, pa, pd, z, col_m, "merge_bwd")
        gbig[3] = matmul([(ya, dpa)], "tn", CD, "patt_wg")
        gbig[4] = matmul([(yd, dpd)], "tn", CD, "pdn_wg")
        dya = matmul([(dpa, wpa)], "nt", F32, "patt_dg")
        dyd = matmul([(dpd, wpd)], "nt", F32, "pdn_dg")
        dya_h = dya.reshape(t, ha, ATT_HEAD_DIM).transpose(1, 0, 2)
        do_tok, dlse_tok = att_combine_bwd(o_tok, lse_tok, dya_h, "att_combine_bwd")
        do_cls = _to_classes(do_tok.transpose(2, 0, 1, 3), t)
        dlse_cls = _to_classes(dlse_tok.transpose(2, 0, 1, 3), t)
        qg, kg = q_norm[l:l + 1], k_norm[l:l + 1]
        dq, dk, dv, dqg, dkg = att_bwd(qp, kp, vp, qg, kg, do_cls, dlse_cls, "att_bwd")
        dz_att = jnp.stack([_from_classes(a_, t) for a_ in (dq, dk, dv)], axis=1).reshape(t, 3 * att_w).astype(CD)
        al3, dt3 = a_log[l].reshape(nh, 1, 1), dt_bias[l].reshape(nh, 1, 1)
        gn = dn_norm[l:l + 1]
        dcvo, dz_gate, dz_ab, dal, ddt, dgn = dn_bwd(
            cvo, z, col_gate, col_ab, al3, dt3, gn, states, dyd, "dn_bwd")
        dz_dn, dconv = conv_bwd(dcvo, z, col_dn, conv_full[l], "conv_bwd")
        dz = jnp.concatenate([dz_dn, dz_gate, dz1, dz2, dz_att, dz_ab, jnp.zeros((t, LANE), CD)], axis=1)
        gbig[2] = w_in_cols_inv(matmul([(h2, dz)], "tn", CD, "w_in_wg", tm=1024, tn=1024))
        dh2 = matmul([(dz, win)], "nt", F32, "w_in_dg", tm=1024, tn=1024,
                     tk=zw // 4 if (zw // 4) % LANE == 0 else zw)
        dxc, dg2, dsc2, dsh2, dgt2 = norm_mod_bwd(x_mix, norm_mix[l:l + 1], sc2, sh2, dh2, dxc, f2, 1.0, "norm_mod_bwd_mix")

        def start_exchange(ids, tag, behind):
            send = [slots(ki, gbig[ki]) for ki in ids]
            started = exchange_start(send, slot_landing(send), behind, False, f"a2a_start{l}{tag}")
            pending.append((l, ids, tag, started))
            return started[4]

        if l == 0:
            token = start_exchange(list(range(2, len(kinds))), "a", dxc)
            sh1, sc1, gt1 = (vec + token[0, 0] for vec in (sh1, sc1, gt1))
        dxc, gbig[0], gbig[1], (dg1, dsc1, dsh1, dgt1) = ffn_bwd(
            dxc, norm_ff1[l:l + 1], sh1, sc1, gt1, u_all1, w_dn1, sv["ffn1"])
        if l > 0:
            token = start_exchange(list(range(len(kinds))), "", dxc)
        dmods.append(jnp.concatenate([dsh1, dsc1, dgt1, dsh2, dsc2, dgt2, dsh3, dsc3, dgt3], axis=1))
        small.append((dg1, dg2, dg3, dconv, dqg, dkg, dal.reshape(1, nh), ddt.reshape(1, nh), dgn))
    dmods.reverse()
    small.reverse()

    fields = [jnp.stack(dmods).reshape(-1)]
    fields += [jnp.stack([s[i] for s in small]).reshape(-1) for i in range(9)]
    fields.append(loss_blk[0, :1])
    fsizes = [f.size for f in fields]
    foffs = [sum(fsizes[:i]) for i in range(len(fields))]
    g1 = all_gather([_rows(jnp.concatenate(fields), 8)], "ag_small")[0].reshape(N_DEV, -1)
    l = 0
    start_exchange([0, 1], "b", g1)

    def field(i, shape):
        return g1[:, foffs[i]:foffs[i] + fsizes[i]].reshape(N_DEV, *shape)

    loss = field(10, (1,))[0, 0]
    for j in range(1, N_DEV):
        loss = loss + field(10, (1,))[j, 0]

    results = {}
    dmod_all = field(0, (nl, N_ADA * d))
    c_pad = jnp.concatenate([c_all, jnp.zeros((8, d), F32)])
    dmod_mine = lax.dynamic_slice_in_dim(dmod_all, me * n_ada, n_ada, axis=2).transpose(1, 0, 2)
    dmod_pad = jnp.concatenate([dmod_mine, jnp.zeros((nl, 8, n_ada), F32)], axis=1)
    g_ada_w = ada_bwd(c_pad, dmod_pad, "ada_bwd")
    results["ada_w"] = adamw(ada_w, m_ada_w, v_ada_w, g_ada_w[None], "adamw_ada_w")
    results["ada_b"] = adamw(ada_b, m_ada_b, v_ada_b, dmod_all, "adamw_ada_b")
    results["norm_ff1"] = adamw(norm_ff1, m_norm_ff1, v_norm_ff1, field(1, (nl, d)), "adamw_norm_ff1")
    results["norm_mix"] = adamw(norm_mix, m_norm_mix, v_norm_mix, field(2, (nl, d)), "adamw_norm_mix")
    results["norm_ff2"] = adamw(norm_ff2, m_norm_ff2, v_norm_ff2, field(3, (nl, d)), "adamw_norm_ff2")
    conv_slots = lax.dynamic_slice_in_dim(field(4, (nl, CONV_WIDTH, 3 * dn_w)), me * conv_w.shape[2],
                                          conv_w.shape[2], axis=3)
    results["conv_w"] = adamw(conv_w, m_conv_w, v_conv_w, conv_slots, "adamw_conv_w")
    results["q_norm"] = adamw(q_norm, m_q_norm, v_q_norm, field(5, (nl, ATT_HEAD_DIM)), "adamw_q_norm")
    results["k_norm"] = adamw(k_norm, m_k_norm, v_k_norm, field(6, (nl, ATT_HEAD_DIM)), "adamw_k_norm")
    results["a_log"] = adamw(a_log, m_a_log, v_a_log, field(7, (nl, nh)), "adamw_a_log")
    results["dt_bias"] = adamw(dt_bias, m_dt_bias, v_dt_bias, field(8, (nl, nh)), "adamw_dt_bias")
    results["dn_norm"] = adamw(dn_norm, m_dn_norm, v_dn_norm, field(9, (nl, DN_HEAD_DIM)), "adamw_dn_norm")
    big_names = ["ffn1_w_up", "ffn1_w_down", "w_in", "w_proj_att", "w_proj_dn", "w_out", "ffn2_w_up", "ffn2_w_down"]
    big_m = [m_ffn1_w_up, m_ffn1_w_down, m_w_in, m_w_proj_att, m_w_proj_dn, m_w_out, m_ffn2_w_up, m_ffn2_w_down]
    big_v = [v_ffn1_w_up, v_ffn1_w_down, v_w_in, v_w_proj_att, v_w_proj_dn, v_w_out, v_ffn2_w_up, v_ffn2_w_down]
    recv_layers = [[None] * len(kinds) for _ in range(nl)]
    for l, ids, tag, st in pending:
        got = exchange_wait(*st[:4], results["ada_w"][1], False, f"a2a_wait{l}{tag}")
        for ki, arr in zip(ids, got):
            recv_layers[l][ki] = arr
    for ki, nm in enumerate(big_names):
        results[nm] = adamw_layers(kinds[ki], big_m[ki], big_v[ki], [recv_layers[l][ki] for l in range(nl)],
                                   f"adamw_{nm}")

    order = ["ada_w", "ada_b", "norm_ff1", "ffn1_w_up", "ffn1_w_down", "norm_mix", "w_in", "q_norm", "k_norm",
             "conv_w", "a_log", "dt_bias", "dn_norm", "w_proj_att", "w_proj_dn", "w_out", "norm_ff2",
             "ffn2_w_up", "ffn2_w_down"]
    outs = [loss, dxc[None]]
    for part in range(4):
        outs += [results[n][part] for n in order]
    return tuple(outs)
```

```python
import functools

import jax
import jax.numpy as jnp
from jax import lax
from jax.experimental import pallas as pl
from jax.experimental.pallas import tpu as pltpu

F32 = jnp.float32
CD = jnp.bfloat16
EPS = 1e-6
N_DEV = 8
LANE = 128
ATT_HEAD_DIM = 64
ATT_BLOCK = 128
DILATIONS = (1, 4, 16)
DN_HEAD_DIM = 128
DN_CHUNK = 64
CONV_WIDTH = 4
N_ADA = 9
ADAM_LR, ADAM_B1, ADAM_B2, ADAM_EPS, ADAM_WD, ADAM_STEP = 0.001, 0.9, 0.999, 1e-08, 0.01, 10
VMEM_LIMIT = 56 * 1024 * 1024
NEG = -1e30
MESH = pl.DeviceIdType.MESH
HI = lax.Precision.HIGHEST


def _params(sem=None):
    return pltpu.CompilerParams(dimension_semantics=sem, vmem_limit_bytes=VMEM_LIMIT)


def _tile(n, pref, unit):
    best = None
    t = unit
    while t <= min(n, pref):
        if n % t == 0:
            best = t
        t += unit
    return best if best is not None else n


def _silu(x):
    return x * jax.nn.sigmoid(x)


def _dot(a, b, dims, precision=None):
    if precision is None:
        a, b = a.astype(CD), b.astype(CD)
    return lax.dot_general(a, b, (dims, ((), ())), precision=precision, preferred_element_type=F32)


def _nn(a, b, precision=None):
    return _dot(a, b, ((1,), (0,)), precision)


def _nt(a, b, precision=None):
    return _dot(a, b, ((1,), (1,)), precision)


def _tn(a, b, precision=None):
    return _dot(a, b, ((0,), (0,)), precision)


def all_gather(arrs, name):
    n = len(arrs)

    def body(*refs):
        x_refs, out_refs = refs[:n], refs[n:2 * n]
        send_sems, recv_sems, local_sems = refs[2 * n:]
        x, y, c = lax.axis_index("x"), lax.axis_index("y"), lax.axis_index("c")
        me, sibling = (x, y, c), (x, y, 1 - c)
        chips = [(1 - x, y), (x, 1 - y), (1 - x, 1 - y)]

        def copy(a, k, block, to, src=None):
            slot = out_refs[a].at[4 * block[0] + 2 * block[1] + block[2]]
            return pltpu.make_async_remote_copy(
                src_ref=slot if src is None else src, dst_ref=slot,
                send_sem=send_sems.at[7 * a + k], recv_sem=recv_sems.at[7 * a + k],
                device_id=to, device_id_type=MESH)

        mine, first, passed = [], [], []
        for a in range(n):
            cp = pltpu.make_async_copy(x_refs[a], out_refs[a].at[4 * x + 2 * y + c], local_sems.at[a])
            cp.start()
            mine.append(cp)
            first.append(copy(a, 0, me, sibling, src=x_refs[a]))
            first += [copy(a, 1 + j, me, (*chip, c), src=x_refs[a]) for j, chip in enumerate(chips)]
        for cp in first:
            cp.start()
        for j, chip in enumerate(chips):
            for a in range(n):
                copy(a, 1 + j, (*chip, c), me).wait_recv()
                cp = copy(a, 4 + j, (*chip, c), sibling)
                cp.start()
                passed.append(cp)
        for a in range(n):
            copy(a, 0, sibling, me).wait_recv()
            for j, chip in enumerate(chips):
                copy(a, 4 + j, (*chip, 1 - c), me).wait_recv()
        for cp in first + passed:
            cp.wait_send()
        for cp in mine:
            cp.wait()

    hbm = pl.BlockSpec(memory_space=pl.ANY)
    outs = pl.pallas_call(
        body, name=name,
        out_shape=[jax.ShapeDtypeStruct((N_DEV,) + a.shape, a.dtype) for a in arrs],
        in_specs=[hbm] * n, out_specs=[hbm] * n,
        scratch_shapes=[pltpu.SemaphoreType.DMA((7 * n,)), pltpu.SemaphoreType.DMA((7 * n,)),
                        pltpu.SemaphoreType.DMA((n,))],
    )(*arrs)
    return list(outs)


def all_to_all(arrs, name):
    n = len(arrs)

    def body(*refs):
        x_refs, out_refs = refs[:n], refs[n:2 * n]
        send_sems, recv_sems, local_sems = refs[2 * n:]
        x, y, c = lax.axis_index("x"), lax.axis_index("y"), lax.axis_index("c")
        me = 4 * x + 2 * y + c
        mine, sends = [], []
        for a in range(n):
            cp = pltpu.make_async_copy(x_refs[a].at[me], out_refs[a].at[me], local_sems.at[a])
            cp.start()
            mine.append(cp)
        peers = []
        for k in range(1, N_DEV):
            px, py, pc = x ^ (k >> 2), y ^ ((k >> 1) & 1), c ^ (k & 1)
            peers.append((px, py, pc, 4 * px + 2 * py + pc))
        for k in range(1, N_DEV):
            px, py, pc, pidx = peers[k - 1]
            for a in range(n):
                cp = pltpu.make_async_remote_copy(
                    src_ref=x_refs[a].at[pidx], dst_ref=out_refs[a].at[me],
                    send_sem=send_sems.at[7 * a + k - 1], recv_sem=recv_sems.at[7 * a + k - 1],
                    device_id=(px, py, pc), device_id_type=MESH)
                cp.start()
                sends.append(cp)
        for k in range(1, N_DEV):
            pidx = peers[k - 1][3]
            for a in range(n):
                pltpu.make_async_remote_copy(
                    src_ref=x_refs[a].at[pidx], dst_ref=out_refs[a].at[pidx],
                    send_sem=send_sems.at[7 * a + k - 1], recv_sem=recv_sems.at[7 * a + k - 1],
                    device_id=(x, y, c), device_id_type=MESH).wait_recv()
        for cp in sends:
            cp.wait_send()
        for cp in mine:
            cp.wait()

    hbm = pl.BlockSpec(memory_space=pl.ANY)
    outs = pl.pallas_call(
        body, name=name,
        out_shape=[jax.ShapeDtypeStruct(a.shape, a.dtype) for a in arrs],
        in_specs=[hbm] * n, out_specs=[hbm] * n,
        scratch_shapes=[pltpu.SemaphoreType.DMA((7 * n,)), pltpu.SemaphoreType.DMA((7 * n,)),
                        pltpu.SemaphoreType.DMA((n,))],
    )(*arrs)
    return list(outs)


_HBM = pl.BlockSpec(memory_space=pltpu.HBM)
_SEM = pl.BlockSpec(memory_space=pltpu.SEMAPHORE)
_EFFECT = pltpu.SideEffectType.DATAFLOW_SIDE_EFFECTING


def _exchange_copies(src_refs, land_refs, send_sem, recv_sem, gather):
    x, y, c = lax.axis_index("x"), lax.axis_index("y"), lax.axis_index("c")
    me = 4 * x + 2 * y + c
    pairs = []
    for k in range(1, N_DEV):
        px, py, pc = x ^ (k >> 2), y ^ ((k >> 1) & 1), c ^ (k & 1)
        pidx = 4 * px + 2 * py + pc
        for src, land in zip(src_refs, land_refs):
            mine = src if gather else src.at[pidx]
            out = pltpu.make_async_remote_copy(src_ref=mine, dst_ref=land.at[me], send_sem=send_sem, recv_sem=recv_sem,
                                               device_id=(px, py, pc), device_id_type=MESH)
            inc = pltpu.make_async_remote_copy(src_ref=mine, dst_ref=land.at[pidx], send_sem=send_sem, recv_sem=recv_sem,
                                               device_id=(px, py, pc), device_id_type=MESH)
            pairs.append((out, inc))
    return pairs


def exchange_start(srcs, lands, after, gather, name):
    n = len(srcs)

    def body(*refs):
        src_refs, land_refs = refs[:n], refs[n:2 * n]
        send_sem, recv_sem = refs[2 * n + 1], refs[2 * n + 2]
        token = refs[-1]
        for out, _ in _exchange_copies(src_refs, land_refs, send_sem, recv_sem, gather):
            out.start()
        token[...] = jnp.zeros_like(token)

    res = pl.pallas_call(
        body, name=name,
        out_shape=(pltpu.SemaphoreType.DMA(()), pltpu.SemaphoreType.DMA(()),
                   *[pltpu.HBM(a.shape, a.dtype) for a in srcs], *[pltpu.HBM(a.shape, a.dtype) for a in lands],
                   jax.ShapeDtypeStruct((8, LANE), F32)),
        in_specs=[_HBM] * (2 * n) + [pl.BlockSpec(memory_space=pl.ANY)],
        out_specs=(_SEM, _SEM, *[_HBM] * (2 * n), pl.BlockSpec(memory_space=pltpu.VMEM)),
        input_output_aliases={i: 2 + i for i in range(2 * n)},
        compiler_params=pltpu.CompilerParams(has_side_effects=_EFFECT),
    )(*[pltpu.with_memory_space_constraint(a, pltpu.HBM) for a in list(srcs) + list(lands)], after)
    return res[0], res[1], list(res[2:2 + n]), list(res[2 + n:2 + 2 * n]), res[-1]


def exchange_wait(send_sem, recv_sem, srcs, lands, after, gather, name):
    n = len(srcs)

    def body(*refs):
        src_refs, land_refs = refs[:n], refs[n:2 * n]
        s_sem, r_sem = refs[2 * n], refs[2 * n + 1]
        for out, inc in _exchange_copies(src_refs, land_refs, s_sem, r_sem, gather):
            out.wait_send()
            inc.wait_recv()

    res = pl.pallas_call(
        body, name=name,
        out_shape=[pltpu.HBM(a.shape, a.dtype) for a in list(srcs) + list(lands)],
        in_specs=[_HBM] * (2 * n) + [_SEM, _SEM, pl.BlockSpec(memory_space=pl.ANY)],
        out_specs=[_HBM] * (2 * n),
        input_output_aliases={i: i for i in range(2 * n)},
        compiler_params=pltpu.CompilerParams(has_side_effects=_EFFECT),
    )(*srcs, *lands, send_sem, recv_sem, after)
    return list(res[n:])


def matmul(pairs, mode, out_dtype, name, *, a_scale=None, out_scale=None, resid=None,
           save_acc=False, tm=512, tn=512, tk=2048):
    a0, b0 = pairs[0]
    if mode == "nn":
        (m, kdim), n = a0.shape, b0.shape[1]
    elif mode == "nt":
        (m, kdim), n = a0.shape, b0.shape[0]
    else:
        (kdim, m), n = a0.shape, b0.shape[1]
    tm, tn = _tile(m, tm, 8 if m % LANE else LANE), _tile(n, tn, LANE)
    tk = _tile(kdim, tk, LANE)
    nk = kdim // tk
    npairs = len(pairs)
    dims = {"nn": ((1,), (0,)), "nt": ((1,), (1,)), "tn": ((0,), (0,))}[mode]

    if mode == "nn":
        a_spec = pl.BlockSpec((tm, tk), lambda i, j, k: (i, k))
        b_spec = pl.BlockSpec((tk, tn), lambda i, j, k: (k, j))
    elif mode == "nt":
        a_spec = pl.BlockSpec((tm, tk), lambda i, j, k: (i, k))
        b_spec = pl.BlockSpec((tn, tk), lambda i, j, k: (j, k))
    else:
        a_spec = pl.BlockSpec((tk, tm), lambda i, j, k: (k, i))
        b_spec = pl.BlockSpec((tk, tn), lambda i, j, k: (k, j))

    def body(*refs):
        it = iter(refs)
        pair_refs = [(next(it), next(it)) for _ in range(npairs)]
        as_ref = next(it) if a_scale is not None else None
        os_ref = next(it) if out_scale is not None else None
        rs_ref = next(it) if resid is not None else None
        o_ref = next(it)
        acc_out = next(it) if save_acc else None
        acc_ref = next(it) if nk > 1 else None
        part = None
        for a_ref, b_ref in pair_refs:
            a = a_ref[...]
            if as_ref is not None:
                a = a.astype(F32) * (as_ref[...] if mode != "tn" else as_ref[...].reshape(tk, 1))
            d = _dot(a, b_ref[...], dims)
            part = d if part is None else part + d

        def finish(acc):
            if acc_out is not None:
                acc_out[...] = acc
            if os_ref is not None:
                acc = acc * os_ref[...]
            if rs_ref is not None:
                acc = rs_ref[...] + acc
            o_ref[...] = acc.astype(o_ref.dtype)

        if nk == 1:
            finish(part)
        else:
            k = pl.program_id(2)

            @pl.when(k == 0)
            def _():
                acc_ref[...] = part

            @pl.when(k > 0)
            def _():
                acc_ref[...] += part

            @pl.when(k == nk - 1)
            def _():
                finish(acc_ref[...])

    in_specs, args = [], []
    for a, b in pairs:
        in_specs += [a_spec, b_spec]
        args += [a, b]
    if a_scale is not None:
        assert mode != "tn"
        in_specs.append(pl.BlockSpec((1, tk), lambda i, j, k: (0, k)))
        args.append(a_scale)
    if out_scale is not None:
        in_specs.append(pl.BlockSpec((1, tn), lambda i, j, k: (0, j)))
        args.append(out_scale)
    if resid is not None:
        in_specs.append(pl.BlockSpec((tm, tn), lambda i, j, k: (i, j)))
        args.append(resid)
    o_spec = pl.BlockSpec((tm, tn), lambda i, j, k: (i, j))
    out_shape = [jax.ShapeDtypeStruct((m, n), out_dtype)]
    out_specs = [o_spec]
    if save_acc:
        out_shape.append(jax.ShapeDtypeStruct((m, n), F32))
        out_specs.append(o_spec)
    scratch = [pltpu.VMEM((tm, tn), F32)] if nk > 1 else []
    res = pl.pallas_call(
        body, name=name, grid=(m // tm, n // tn, nk),
        in_specs=in_specs, out_specs=out_specs, out_shape=out_shape, scratch_shapes=scratch,
        compiler_params=_params(("parallel", "parallel", "arbitrary")),
    )(*args)
    return res if save_acc else res[0]


def _nm_fn(x, g, sc, sh):
    r = lax.rsqrt(jnp.mean(x * x, axis=-1, keepdims=True) + EPS)
    return (x * r * g) * (1.0 + sc) + sh


def norm_mod(x, g, sc, sh, name):
    t, d = x.shape
    tr = _tile(t, 256, 8)

    def body(x_ref, g_ref, sc_ref, sh_ref, h_ref):
        h_ref[...] = _nm_fn(x_ref[...], g_ref[...], sc_ref[...], sh_ref[...]).astype(h_ref.dtype)

    row = pl.BlockSpec((tr, d), lambda i: (i, 0))
    vec = pl.BlockSpec((1, d), lambda i: (0, 0))
    return pl.pallas_call(
        body, name=name, grid=(t // tr,), in_specs=[row, vec, vec, vec], out_specs=row,
        out_shape=jax.ShapeDtypeStruct((t, d), CD), compiler_params=_params(("parallel",)),
    )(x, g, sc, sh)


def norm_mod_bwd(x, g, sc, sh, dh, dxo, f, gate_scale, name):
    t, d = x.shape
    tr = _tile(t, 256, 8)

    def body(x_ref, g_ref, sc_ref, sh_ref, dh_ref, dxo_ref, f_ref, dx_ref, dg_ref, dsc_ref, dsh_ref, dgt_ref):
        _, vjp = jax.vjp(_nm_fn, x_ref[...], g_ref[...], sc_ref[...], sh_ref[...])
        dx, dg, dsc, dsh = vjp(dh_ref[...])
        dxo_v = dxo_ref[...]
        dx_ref[...] = dxo_v + dx
        dgt = gate_scale * jnp.sum(f_ref[...] * dxo_v, axis=0, keepdims=True)

        @pl.when(pl.program_id(0) == 0)
        def _():
            dg_ref[...] = dg
            dsc_ref[...] = dsc
            dsh_ref[...] = dsh
            dgt_ref[...] = dgt

        @pl.when(pl.program_id(0) > 0)
        def _():
            dg_ref[...] += dg
            dsc_ref[...] += dsc
            dsh_ref[...] += dsh
            dgt_ref[...] += dgt

    row = pl.BlockSpec((tr, d), lambda i: (i, 0))
    vec = pl.BlockSpec((1, d), lambda i: (0, 0))
    vshape = jax.ShapeDtypeStruct((1, d), F32)
    return pl.pallas_call(
        body, name=name, grid=(t // tr,), in_specs=[row, vec, vec, vec, row, row, row],
        out_specs=[row, vec, vec, vec, vec],
        out_shape=[jax.ShapeDtypeStruct((t, d), F32), vshape, vshape, vshape, vshape],
        compiler_params=_params(("arbitrary",)),
    )(x, g, sc, sh, dh, dxo, f)


HALF = N_DEV // 2


def ffn_up(h, u_all, name):
    t, d = h.shape
    cp = u_all.shape[1]
    f = HALF * cp
    tm, tn = _tile(t, 1024, LANE), _tile(cp, 384, LANE)
    per = cp // tn

    def body(h_ref, wg_ref, wu_ref, g_ref, u_ref, a_ref):
        hv = h_ref[...]
        gate = _nt(hv, wg_ref[0])
        up = _nt(hv, wu_ref[0])
        g_ref[...] = gate.astype(g_ref.dtype)
        u_ref[...] = up.astype(u_ref.dtype)
        a_ref[...] = (_silu(gate) * up).astype(a_ref.dtype)

    o = pl.BlockSpec((tm, tn), lambda i, j: (i, j))
    wg = pl.BlockSpec((1, tn, d), lambda i, j: (j // per, j % per, 0))
    wu = pl.BlockSpec((1, tn, d), lambda i, j: (HALF + j // per, j % per, 0))
    return pl.pallas_call(
        body, name=name, grid=(t // tm, f // tn),
        in_specs=[pl.BlockSpec((tm, d), lambda i, j: (i, 0)), wg, wu], out_specs=[o, o, o],
        out_shape=[jax.ShapeDtypeStruct((t, f), CD)] * 3,
        compiler_params=_params(("parallel", "parallel")),
    )(h, u_all, u_all)


def ffn_up_wg(h, dgate, dup, cp, name):
    t, d = h.shape
    tn = _tile(d, 512, LANE)

    def body(h_ref, dg_ref, du_ref, o_ref):
        s = pl.program_id(0)

        @pl.when(s < HALF)
        def _():
            o_ref[0] = _tn(dg_ref[...], h_ref[...]).astype(o_ref.dtype)

        @pl.when(s >= HALF)
        def _():
            o_ref[0] = _tn(du_ref[...], h_ref[...]).astype(o_ref.dtype)

    return pl.pallas_call(
        body, name=name, grid=(N_DEV, d // tn),
        in_specs=[pl.BlockSpec((t, tn), lambda s, i: (0, i)),
                  pl.BlockSpec((t, cp), lambda s, i: (0, jnp.minimum(s, HALF - 1))),
                  pl.BlockSpec((t, cp), lambda s, i: (0, jnp.maximum(s - HALF, 0)))],
        out_specs=pl.BlockSpec((1, cp, tn), lambda s, i: (s, 0, i)),
        out_shape=jax.ShapeDtypeStruct((N_DEV, cp, d), CD),
        compiler_params=_params(("parallel", "parallel")),
    )(h, dgate, dup)


def ffn_up_dg(dgate, dup, u_all, name):
    t = dgate.shape[0]
    cp, d = u_all.shape[1], u_all.shape[2]
    tm, tn = _tile(t, 1024, LANE), _tile(d, 1024, LANE)

    def body(dg_ref, du_ref, u_ref, o_ref, acc_ref):
        s = pl.program_id(2)

        @pl.when(s == 0)
        def _():
            acc_ref[...] = jnp.zeros_like(acc_ref)

        @pl.when(s < HALF)
        def _():
            acc_ref[...] += _nn(dg_ref[...], u_ref[0])

        @pl.when(s >= HALF)
        def _():
            acc_ref[...] += _nn(du_ref[...], u_ref[0])

        @pl.when(s == N_DEV - 1)
        def _():
            o_ref[...] = acc_ref[...]

    return pl.pallas_call(
        body, name=name, grid=(t // tm, d // tn, N_DEV),
        in_specs=[pl.BlockSpec((tm, cp), lambda i, j, s: (i, jnp.minimum(s, HALF - 1))),
                  pl.BlockSpec((tm, cp), lambda i, j, s: (i, jnp.maximum(s - HALF, 0))),
                  pl.BlockSpec((1, cp, tn), lambda i, j, s: (s, 0, j))],
        out_specs=pl.BlockSpec((tm, tn), lambda i, j, s: (i, j)),
        out_shape=jax.ShapeDtypeStruct((t, d), F32),
        scratch_shapes=[pltpu.VMEM((tm, tn), F32)],
        compiler_params=_params(("parallel", "parallel", "arbitrary")),
    )(dgate, dup, u_all)


def ffn_dact(dxo, s, wd, gate, up, name):
    t, d = dxo.shape
    f = wd.shape[0]
    tm, tn = _tile(t, 1024, LANE), _tile(f, 384, LANE)

    def body(dxo_ref, s_ref, wd_ref, g_ref, u_ref, dg_ref, du_ref):
        da = _nt(dxo_ref[...] * s_ref[...], wd_ref[...])
        gate, up = g_ref[...].astype(F32), u_ref[...].astype(F32)
        sg = jax.nn.sigmoid(gate)
        dg_ref[...] = (da * up * sg * (1.0 + gate * (1.0 - sg))).astype(dg_ref.dtype)
        du_ref[...] = (da * gate * sg).astype(du_ref.dtype)

    o = pl.BlockSpec((tm, tn), lambda i, j: (i, j))
    return pl.pallas_call(
        body, name=name, grid=(t // tm, f // tn),
        in_specs=[pl.BlockSpec((tm, d), lambda i, j: (i, 0)), pl.BlockSpec((1, d), lambda i, j: (0, 0)),
                  pl.BlockSpec((tn, d), lambda i, j: (j, 0)), o, o],
        out_specs=[o, o],
        out_shape=[jax.ShapeDtypeStruct((t, f), CD), jax.ShapeDtypeStruct((t, f), CD)],
        compiler_params=_params(("parallel", "parallel")),
    )(dxo, s, wd, gate, up)


def _bdot(a, b, ca, cb, precision=None):
    if precision is None:
        a, b = a.astype(CD), b.astype(CD)
    return lax.dot_general(a, b, (((ca,), (cb,)), ((0,), (0,))), precision=precision, preferred_element_type=F32)


def _bnn(a, b, precision=None):
    return _bdot(a, b, 2, 1, precision)


def _bnt(a, b, precision=None):
    return _bdot(a, b, 2, 2, precision)


def _btn(a, b, precision=None):
    return _bdot(a, b, 1, 1, precision)


def _att_fn(q, kp, kc, vp, vc, qg, kg, mask_p, mask_c):
    b, w = q.shape
    nh = w // ATT_HEAD_DIM
    head_of_lane = lax.broadcasted_iota(jnp.int32, (nh, 1, w), 2) // ATT_HEAD_DIM
    hm = (head_of_lane == lax.broadcasted_iota(jnp.int32, (nh, 1, w), 0)).astype(F32)

    def rn(x, g):
        ss = jnp.sum((x * x)[None] * hm, axis=-1, keepdims=True)
        r = jnp.sum(lax.rsqrt(ss * (1.0 / ATT_HEAD_DIM) + EPS) * hm, axis=0)
        return x * r * g

    qn, kpn, kcn = rn(q, qg), rn(kp, kg), rn(kc, kg)
    q4 = (qn[None] * hm).reshape(nh * b, w)
    scale = ATT_HEAD_DIM ** -0.5
    sp = jnp.where(mask_p, _nt(q4, kpn) * scale, NEG)
    sc = jnp.where(mask_c, _nt(q4, kcn) * scale, NEG)
    m = jnp.maximum(jnp.max(sp, axis=-1, keepdims=True), jnp.max(sc, axis=-1, keepdims=True))
    pp, pc = jnp.exp(sp - m), jnp.exp(sc - m)
    den = jnp.sum(pp, axis=-1, keepdims=True) + jnp.sum(pc, axis=-1, keepdims=True)
    o4 = _nn(pp / den, vp) + _nn(pc / den, vc)
    o = jnp.sum(o4.reshape(nh, b, w) * hm, axis=0)
    lse = jnp.sum((m + jnp.log(den)).reshape(nh, b, 1) * hm, axis=0)
    return o, lse


def _att_masks(g, j, nb_total, nh):
    nb = jnp.int32(nb_total // DILATIONS[0])
    for gi in range(1, len(DILATIONS)):
        nb = jnp.where(g == gi, jnp.int32(nb_total // DILATIONS[gi]), nb)
    has_prev = (j % nb) != 0
    row = lax.broadcasted_iota(jnp.int32, (nh * ATT_BLOCK, ATT_BLOCK), 0) % ATT_BLOCK
    col = lax.broadcasted_iota(jnp.int32, (nh * ATT_BLOCK, ATT_BLOCK), 1)
    mask_p = jnp.logical_and(col >= row, has_prev)
    mask_c = col <= row
    return mask_p, mask_c


def _att_specs(w):
    blk = (1, 1, ATT_BLOCK, w)
    q = pl.BlockSpec(blk, lambda g, j: (0, g, j, 0))
    kp = pl.BlockSpec(blk, lambda g, j: (1, g, jnp.maximum(j - 1, 0), 0))
    kc = pl.BlockSpec(blk, lambda g, j: (1, g, j, 0))
    vp = pl.BlockSpec(blk, lambda g, j: (2, g, jnp.maximum(j - 1, 0), 0))
    vc = pl.BlockSpec(blk, lambda g, j: (2, g, j, 0))
    gain = pl.BlockSpec((1, w), lambda g, j: (0, 0))
    out = pl.BlockSpec((1, ATT_BLOCK, w), lambda g, j: (g, j, 0))
    return [q, kp, kc, vp, vc, gain, gain], out


def att_fwd(qkv, qg, kg, name):
    _, ng, t, w = qkv.shape
    nbt = t // ATT_BLOCK
    in_specs, out = _att_specs(w)

    def body(q_ref, kp_ref, kc_ref, vp_ref, vc_ref, qg_ref, kg_ref, o_ref, lse_ref):
        mask_p, mask_c = _att_masks(pl.program_id(0), pl.program_id(1), nbt, w // ATT_HEAD_DIM)
        o, lse = _att_fn(q_ref[0, 0], kp_ref[0, 0], kc_ref[0, 0], vp_ref[0, 0], vc_ref[0, 0],
                         qg_ref[...], kg_ref[...], mask_p, mask_c)
        o_ref[0] = o
        lse_ref[0] = lse

    sh = jax.ShapeDtypeStruct((ng, t, w), F32)
    return pl.pallas_call(
        body, name=name, grid=(ng, nbt), in_specs=in_specs, out_specs=[out, out], out_shape=[sh, sh],
        compiler_params=_params(("parallel", "parallel")),
    )(qkv, qkv, qkv, qkv, qkv, qg, kg)


def att_bwd(qkv, qg, kg, do, dlse, name):
    _, ng, t, w = qkv.shape
    nbt = t // ATT_BLOCK
    in_specs, out = _att_specs(w)
    whole = pl.BlockSpec((1, t, w), lambda g, j: (g, 0, 0))
    gain = in_specs[-1]

    def body(q_ref, kp_ref, kc_ref, vp_ref, vc_ref, qg_ref, kg_ref, do_ref, dlse_ref,
             dq_ref, dk_ref, dv_ref, dqg_ref, dkg_ref):
        g, j = pl.program_id(0), pl.program_id(1)
        mask_p, mask_c = _att_masks(g, j, nbt, w // ATT_HEAD_DIM)

        @pl.when(j == 0)
        def _():
            dk_ref[...] = jnp.zeros_like(dk_ref)
            dv_ref[...] = jnp.zeros_like(dv_ref)

        @pl.when(jnp.logical_and(g == 0, j == 0))
        def _():
            dqg_ref[...] = jnp.zeros_like(dqg_ref)
            dkg_ref[...] = jnp.zeros_like(dkg_ref)

        rows_c = pl.ds(pl.multiple_of(j * ATT_BLOCK, ATT_BLOCK), ATT_BLOCK)
        rows_p = pl.ds(pl.multiple_of(jnp.maximum(j - 1, 0) * ATT_BLOCK, ATT_BLOCK), ATT_BLOCK)
        fn = functools.partial(_att_fn, mask_p=mask_p, mask_c=mask_c)
        _, vjp = jax.vjp(fn, q_ref[0, 0], kp_ref[0, 0], kc_ref[0, 0], vp_ref[0, 0], vc_ref[0, 0],
                         qg_ref[...], kg_ref[...])
        dq, dkp, dkc, dvp, dvc, dqg, dkg = vjp((do_ref[0], dlse_ref[0]))
        dq_ref[0] = dq
        dk_ref[0, rows_p, :] += dkp
        dv_ref[0, rows_p, :] += dvp
        dk_ref[0, rows_c, :] += dkc
        dv_ref[0, rows_c, :] += dvc
        dqg_ref[...] += dqg
        dkg_ref[...] += dkg

    sh = jax.ShapeDtypeStruct((ng, t, w), F32)
    gshape = jax.ShapeDtypeStruct((1, w), F32)
    return pl.pallas_call(
        body, name=name, grid=(ng, nbt), in_specs=in_specs + [out, out],
        out_specs=[out, whole, whole, gain, gain], out_shape=[sh, sh, sh, gshape, gshape],
        compiler_params=_params(("arbitrary", "arbitrary")),
    )(qkv, qkv, qkv, qkv, qkv, qg, kg, do, dlse)


def _combine_fn(o, lse):
    m = jnp.max(lse, axis=0, keepdims=True)
    e = jnp.exp(lse - m)
    w = e / jnp.sum(e, axis=0, keepdims=True)
    return jnp.sum(w * o, axis=0)


def att_combine(o, lse, name):
    ng, t, w = o.shape
    tr = _tile(t, 256, 16)
    spec = pl.BlockSpec((ng, tr, w), lambda i: (0, i, 0))
    y_spec = pl.BlockSpec((tr, w), lambda i: (i, 0))

    def body(o_ref, l_ref, y_ref):
        y_ref[...] = _combine_fn(o_ref[...], l_ref[...]).astype(y_ref.dtype)

    return pl.pallas_call(
        body, name=name, grid=(t // tr,), in_specs=[spec, spec], out_specs=y_spec,
        out_shape=jax.ShapeDtypeStruct((t, w), CD), compiler_params=_params(("parallel",)),
    )(o, lse)


def att_combine_bwd(o, lse, dy, name):
    ng, t, w = o.shape
    tr = _tile(t, 256, 8)
    spec = pl.BlockSpec((ng, tr, w), lambda i: (0, i, 0))
    y_spec = pl.BlockSpec((tr, w), lambda i: (i, 0))

    def body(o_ref, l_ref, dy_ref, do_ref, dl_ref):
        _, vjp = jax.vjp(_combine_fn, o_ref[...], l_ref[...])
        do, dl = vjp(dy_ref[...])
        do_ref[...] = do
        dl_ref[...] = dl

    sh = jax.ShapeDtypeStruct(o.shape, F32)
    return pl.pallas_call(
        body, name=name, grid=(t // tr,), in_specs=[spec, spec, y_spec], out_specs=[spec, spec],
        out_shape=[sh, sh], compiler_params=_params(("parallel",)),
    )(o, lse, dy)


def _shift_down(x, s):
    if s == 0:
        return x
    row = lax.broadcasted_iota(jnp.int32, x.shape, 0)
    return jnp.where(row >= s, pltpu.roll(x, s, 0), 0.0)


def _shift_up(x, s):
    if s == 0:
        return x
    t = x.shape[0]
    row = lax.broadcasted_iota(jnp.int32, x.shape, 0)
    return jnp.where(row < t - s, pltpu.roll(x, t - s, 0), 0.0)


def conv_fwd(z, col0, w, name):
    t = z.shape[0]
    c = w.shape[1]
    nblk0 = col0 // LANE

    def body(z_ref, w_ref, c_ref):
        zv = z_ref[...]
        acc = None
        for i in range(CONV_WIDTH):
            term = _shift_down(zv, CONV_WIDTH - 1 - i) * w_ref[i:i + 1, :]
            acc = term if acc is None else acc + term
        c_ref[...] = acc

    return pl.pallas_call(
        body, name=name, grid=(c // LANE,),
        in_specs=[pl.BlockSpec((t, LANE), lambda j: (0, nblk0 + j)), pl.BlockSpec((CONV_WIDTH, LANE), lambda j: (0, j))],
        out_specs=pl.BlockSpec((t, LANE), lambda j: (0, j)),
        out_shape=jax.ShapeDtypeStruct((t, c), F32), compiler_params=_params(("parallel",)),
    )(z, w)


def conv_bwd(dc, z, col0, w, name):
    t = z.shape[0]
    c = w.shape[1]
    nblk0 = col0 // LANE

    def body(dc_ref, z_ref, w_ref, dz_ref, dw_ref):
        dcv, zv = dc_ref[...], z_ref[...]
        acc = None
        for i in range(CONV_WIDTH):
            s = CONV_WIDTH - 1 - i
            term = _shift_up(dcv, s) * w_ref[i:i + 1, :]
            acc = term if acc is None else acc + term
            dw_ref[i:i + 1, :] = jnp.sum(dcv * _shift_down(zv, s), axis=0, keepdims=True)
        dz_ref[...] = acc.astype(dz_ref.dtype)

    blk = pl.BlockSpec((t, LANE), lambda j: (0, j))
    wblk = pl.BlockSpec((CONV_WIDTH, LANE), lambda j: (0, j))
    return pl.pallas_call(
        body, name=name, grid=(c // LANE,),
        in_specs=[blk, pl.BlockSpec((t, LANE), lambda j: (0, nblk0 + j)), wblk], out_specs=[blk, wblk],
        out_shape=[jax.ShapeDtypeStruct((t, c), CD), jax.ShapeDtypeStruct((CONV_WIDTH, c), F32)],
        compiler_params=_params(("parallel",)),
    )(dc, z, w)


def _dn_consts():
    c = DN_CHUNK
    row = lax.broadcasted_iota(jnp.int32, (c, c), 0)
    col = lax.broadcasted_iota(jnp.int32, (c, c), 1)
    return dict(tril=row >= col, strict=row > col, eye=(row == col).astype(F32),
                tril_f=(row >= col).astype(F32), triu_f=(row <= col).astype(F32))


def _softplus(x):
    return jnp.maximum(x, 0.0) + jnp.log(1.0 + jnp.exp(-jnp.abs(x)))


def _split(x):
    hi = x.astype(CD)
    return hi, (x - hi.astype(F32)).astype(CD)


def _bdot3(a, b, ca, cb):
    ah, al = _split(a)
    bh, bl = _split(b)
    return _bdot(ah, bh, ca, cb) + (_bdot(ah, bl, ca, cb) + _bdot(al, bh, ca, cb))


def _tri_inv_impl(a_mat):
    c = a_mat.shape[-1]
    eye = (lax.broadcasted_iota(jnp.int32, (c, c), 0) == lax.broadcasted_iota(jnp.int32, (c, c), 1)).astype(F32)
    nk_ = -a_mat
    t_inv = eye + nk_
    for _ in range(c.bit_length() - 2):
        nk_ = _bdot3(nk_, nk_, 2, 1)
        t_inv = t_inv + _bdot3(t_inv, nk_, 2, 1)
    return t_inv


@jax.custom_vjp
def _tri_inv(a_mat):
    return _tri_inv_impl(a_mat)


def _tri_inv_fwd(a_mat):
    t_inv = _tri_inv_impl(a_mat)
    return t_inv, t_inv


def _tri_inv_bwd(t_inv, dt_inv):
    return (-_bdot3(_bdot3(t_inv, dt_inv, 1, 1), t_inv, 2, 2),)


_tri_inv.defvjp(_tri_inv_fwd, _tri_inv_bwd)


def _dn_chunk(cq, ck, cv, og, a_col, b_col, al, dt, gn, s_prev, *, k, inv):
    q = _silu(cq)
    q = q * lax.rsqrt(jnp.sum(q * q, axis=-1, keepdims=True) + EPS) * (DN_HEAD_DIM ** -0.5)
    kk = _silu(ck)
    kk = kk * lax.rsqrt(jnp.sum(kk * kk, axis=-1, keepdims=True) + EPS)
    v = _silu(cv)
    g = -jnp.exp(al) * _softplus(a_col + dt)
    beta = jax.nn.sigmoid(b_col)
    g_row = jnp.sum(k["eye"] * g, axis=1, keepdims=True)
    gc_col = jnp.sum(k["tril_f"] * g_row, axis=2, keepdims=True)
    gc_row = jnp.sum(k["triu_f"] * g, axis=1, keepdims=True)
    ldec = jnp.where(k["tril"], jnp.exp(jnp.where(k["tril"], gc_col - gc_row, 0.0)), 0.0)
    kb, vb = kk * beta, v * beta
    a_mat = jnp.where(k["strict"], _bnt(kb, kk) * ldec, 0.0)
    t_inv = inv(a_mat)
    egc = jnp.exp(gc_col)
    u = _bnn(t_inv, vb)
    w = _bnn(t_inv, kb * egc)
    attn = jnp.where(k["tril"], _bnt(q, kk) * ldec, 0.0)
    gc_last = jnp.sum(g, axis=1, keepdims=True)
    k_dec = kk * jnp.exp(gc_last - gc_col)
    v_new = u - _bnn(w, s_prev)
    o = _bnn(q * egc, s_prev) + _bnn(attn, v_new)
    s_new = s_prev * jnp.exp(gc_last) + _btn(k_dec, v_new)
    y = o * lax.rsqrt(jnp.mean(o * o, axis=-1, keepdims=True) + EPS) * gn * _silu(og)
    return y, s_new


def _dn_heads(ref, nh):
    hd = DN_HEAD_DIM
    return jnp.stack([ref[:, h * hd:(h + 1) * hd] for h in range(nh)])


def _dn_specs(nh, col_gate, col_ab, order):
    hd, c = DN_HEAD_DIM, DN_CHUNK
    w = nh * hd
    qs = pl.BlockSpec((c, w), lambda n: (order(n), 0))
    ks = pl.BlockSpec((c, w), lambda n: (order(n), 1))
    vs = pl.BlockSpec((c, w), lambda n: (order(n), 2))
    gs = pl.BlockSpec((c, w), lambda n: (order(n), col_gate // w))
    ab = pl.BlockSpec((c, LANE), lambda n: (order(n), col_ab // LANE))
    scal = pl.BlockSpec((nh, 1, 1), lambda n: (0, 0, 0))
    gn = pl.BlockSpec((1, hd), lambda n: (0, 0))
    st = pl.BlockSpec((1, nh, hd, hd), lambda n: (order(n), 0, 0, 0))
    return qs, ks, vs, gs, ab, scal, gn, st


def _lane_pick(x, idx):
    lane = lax.broadcasted_iota(jnp.int32, x.shape, 1)
    return jnp.sum(jnp.where(lane == idx, x, 0.0), axis=1, keepdims=True)


def dn_fwd(cv, z, col_gate, col_ab, a_log, dt_bias, gn, name):
    t = cv.shape[0]
    nh = a_log.shape[0]
    hd, c = DN_HEAD_DIM, DN_CHUNK
    n_chunks = t // c
    qs, ks, vs, gs, ab, scal, gnspec, st = _dn_specs(nh, col_gate, col_ab, lambda n: n)

    def body(q_ref, k_ref, v_ref, g_ref, ab_ref, al_ref, dt_ref, gn_ref, y_ref, st_ref, s_scr):
        @pl.when(pl.program_id(0) == 0)
        def _():
            s_scr[...] = jnp.zeros_like(s_scr)

        abv = ab_ref[...]
        a_col = jnp.stack([_lane_pick(abv, h) for h in range(nh)])
        b_col = jnp.stack([_lane_pick(abv, nh + h) for h in range(nh)])
        s_prev = s_scr[...]
        st_ref[0] = s_prev
        y, s_new = _dn_chunk(_dn_heads(q_ref, nh), _dn_heads(k_ref, nh), _dn_heads(v_ref, nh), _dn_heads(g_ref, nh),
                             a_col, b_col, al_ref[...], dt_ref[...], gn_ref[...], s_prev,
                             k=_dn_consts(), inv=_tri_inv_impl)
        for h in range(nh):
            y_ref[:, h * hd:(h + 1) * hd] = y[h].astype(y_ref.dtype)
        s_scr[...] = s_new

    return pl.pallas_call(
        body, name=name, grid=(n_chunks,),
        in_specs=[qs, ks, vs, gs, ab, scal, scal, gnspec],
        out_specs=[pl.BlockSpec((c, nh * hd), lambda n: (n, 0)), st],
        out_shape=[jax.ShapeDtypeStruct((t, nh * hd), CD), jax.ShapeDtypeStruct((n_chunks, nh, hd, hd), F32)],
        scratch_shapes=[pltpu.VMEM((nh, hd, hd), F32)],
        compiler_params=_params(("arbitrary",)),
    )(cv, cv, cv, z, z, a_log, dt_bias, gn)


def dn_bwd(cv, z, col_gate, col_ab, a_log, dt_bias, gn, states, dy, name):
    t = cv.shape[0]
    nh = a_log.shape[0]
    hd, c = DN_HEAD_DIM, DN_CHUNK
    w = nh * hd
    n_chunks = t // c
    rev = lambda n: n_chunks - 1 - n
    qs, ks, vs, gs, ab, scal, gnspec, st = _dn_specs(nh, col_gate, col_ab, rev)
    yspec = pl.BlockSpec((c, w), lambda n: (rev(n), 0))

    def body(q_ref, k_ref, v_ref, g_ref, ab_ref, al_ref, dt_ref, gn_ref, st_ref, dy_ref,
             dc_ref, dg_ref, dab_ref, dal_ref, ddt_ref, dgn_ref, ds_scr):
        @pl.when(pl.program_id(0) == 0)
        def _():
            ds_scr[...] = jnp.zeros_like(ds_scr)
            dal_ref[...] = jnp.zeros_like(dal_ref)
            ddt_ref[...] = jnp.zeros_like(ddt_ref)
            dgn_ref[...] = jnp.zeros_like(dgn_ref)

        abv = ab_ref[...]
        a_col = jnp.stack([_lane_pick(abv, h) for h in range(nh)])
        b_col = jnp.stack([_lane_pick(abv, nh + h) for h in range(nh)])
        fn = functools.partial(_dn_chunk, k=_dn_consts(), inv=_tri_inv)
        _, vjp = jax.vjp(fn, _dn_heads(q_ref, nh), _dn_heads(k_ref, nh), _dn_heads(v_ref, nh), _dn_heads(g_ref, nh),
                         a_col, b_col, al_ref[...], dt_ref[...], gn_ref[...], st_ref[0])
        dq, dk, dv, dg, da, db, dal, ddt, dgn, ds = vjp((_dn_heads(dy_ref, nh), ds_scr[...]))
        lane = lax.broadcasted_iota(jnp.int32, (c, LANE), 1)
        dab = jnp.zeros((c, LANE), F32)
        for h in range(nh):
            cols = slice(h * hd, (h + 1) * hd)
            dc_ref[:, cols] = dq[h]
            dc_ref[:, w + h * hd:w + (h + 1) * hd] = dk[h]
            dc_ref[:, 2 * w + h * hd:2 * w + (h + 1) * hd] = dv[h]
            dg_ref[:, cols] = dg[h].astype(dg_ref.dtype)
            dab = dab + jnp.where(lane == h, da[h], 0.0) + jnp.where(lane == nh + h, db[h], 0.0)
        dab_ref[...] = dab.astype(dab_ref.dtype)
        dal_ref[...] += dal
        ddt_ref[...] += ddt
        dgn_ref[...] += dgn
        ds_scr[...] = ds

    sshape = jax.ShapeDtypeStruct((nh, 1, 1), F32)
    res = pl.pallas_call(
        body, name=name, grid=(n_chunks,),
        in_specs=[qs, ks, vs, gs, ab, scal, scal, gnspec, st, yspec],
        out_specs=[pl.BlockSpec((c, 3 * w), lambda n: (rev(n), 0)), yspec,
                   pl.BlockSpec((c, LANE), lambda n: (rev(n), 0)), scal, scal, gnspec],
        out_shape=[jax.ShapeDtypeStruct((t, 3 * w), F32), jax.ShapeDtypeStruct((t, w), CD),
                   jax.ShapeDtypeStruct((t, LANE), CD), sshape, sshape, jax.ShapeDtypeStruct((1, hd), F32)],
        scratch_shapes=[pltpu.VMEM((nh, hd, hd), F32)],
        compiler_params=_params(("arbitrary",)),
    )(cv, cv, cv, z, z, a_log, dt_bias, gn, states, dy)
    return res


def merge_fwd(ya, wpa, yd, wpd, z, col_m, name):
    t, d = yd.shape[0], wpd.shape[1]
    tm, tn = _tile(t, 512, LANE), _tile(d, 256, LANE)
    nb1, nb2 = col_m // tn, (col_m + d) // tn

    def body(ya_ref, wpa_ref, yd_ref, wpd_ref, z1_ref, z2_ref, m_ref, pa_ref, pd_ref):
        pa = _nn(ya_ref[...], wpa_ref[...])
        pd = _nn(yd_ref[...], wpd_ref[...])
        pa_ref[...] = pa
        pd_ref[...] = pd
        m_ref[...] = (jax.nn.sigmoid(z1_ref[...]) * pa + jax.nn.sigmoid(z2_ref[...]) * pd).astype(m_ref.dtype)

    o = pl.BlockSpec((tm, tn), lambda i, j: (i, j))
    return pl.pallas_call(
        body, name=name, grid=(t // tm, d // tn),
        in_specs=[pl.BlockSpec((tm, ya.shape[1]), lambda i, j: (i, 0)),
                  pl.BlockSpec((wpa.shape[0], tn), lambda i, j: (0, j)),
                  pl.BlockSpec((tm, yd.shape[1]), lambda i, j: (i, 0)),
                  pl.BlockSpec((wpd.shape[0], tn), lambda i, j: (0, j)),
                  pl.BlockSpec((tm, tn), lambda i, j: (i, nb1 + j)),
                  pl.BlockSpec((tm, tn), lambda i, j: (i, nb2 + j))],
        out_specs=[o, o, o],
        out_shape=[jax.ShapeDtypeStruct((t, d), CD), jax.ShapeDtypeStruct((t, d), F32),
                   jax.ShapeDtypeStruct((t, d), F32)],
        compiler_params=_params(("parallel", "parallel")),
    )(ya, wpa, yd, wpd, z, z)


def merge_bwd(dm, pa, pd, z, col_m, name):
    t, d = dm.shape
    tm, tn = _tile(t, 512, 8), _tile(d, 256, LANE)
    nb1, nb2 = col_m // tn, (col_m + d) // tn

    def body(dm_ref, pa_ref, pd_ref, z1_ref, z2_ref, dpa_ref, dpd_ref, dz1_ref, dz2_ref):
        dmv = dm_ref[...]
        s1, s2 = jax.nn.sigmoid(z1_ref[...]), jax.nn.sigmoid(z2_ref[...])
        dpa_ref[...] = (dmv * s1).astype(dpa_ref.dtype)
        dpd_ref[...] = (dmv * s2).astype(dpd_ref.dtype)
        dz1_ref[...] = (dmv * pa_ref[...] * s1 * (1.0 - s1)).astype(dz1_ref.dtype)
        dz2_ref[...] = (dmv * pd_ref[...] * s2 * (1.0 - s2)).astype(dz2_ref.dtype)

    o = pl.BlockSpec((tm, tn), lambda i, j: (i, j))
    sh = jax.ShapeDtypeStruct((t, d), CD)
    return pl.pallas_call(
        body, name=name, grid=(t // tm, d // tn),
        in_specs=[o, o, o, pl.BlockSpec((tm, tn), lambda i, j: (i, nb1 + j)),
                  pl.BlockSpec((tm, tn), lambda i, j: (i, nb2 + j))],
        out_specs=[o, o, o, o], out_shape=[sh, sh, sh, sh],
        compiler_params=_params(("parallel", "parallel")),
    )(dm, pa, pd, z, z)


def ada_fwd(c_all, w, b, name):
    nl, d, n = w.shape
    tn = _tile(n, 384, LANE)

    def body(c_ref, w_ref, b_ref, o_ref):
        o_ref[0] = _nn(_silu(c_ref[...]), w_ref[0]) + b_ref[0]

    return pl.pallas_call(
        body, name=name, grid=(nl, n // tn),
        in_specs=[pl.BlockSpec(c_all.shape, lambda l, j: (0, 0)), pl.BlockSpec((1, d, tn), lambda l, j: (l, 0, j)),
                  pl.BlockSpec((1, 1, tn), lambda l, j: (l, 0, j))],
        out_specs=pl.BlockSpec((1, c_all.shape[0], tn), lambda l, j: (l, 0, j)),
        out_shape=jax.ShapeDtypeStruct((nl, c_all.shape[0], n), F32),
        compiler_params=_params(("parallel", "parallel")),
    )(c_all, w, b)


def ada_bwd(c_pad, dmod_pad, name):
    nl, kp, n = dmod_pad.shape
    d = c_pad.shape[1]
    tn = _tile(n, 384, LANE)

    def body(c_ref, g_ref, o_ref):
        o_ref[0] = _tn(_silu(c_ref[...]), g_ref[0])

    return pl.pallas_call(
        body, name=name, grid=(nl, n // tn),
        in_specs=[pl.BlockSpec((kp, d), lambda l, j: (0, 0)), pl.BlockSpec((1, kp, tn), lambda l, j: (l, 0, j))],
        out_specs=pl.BlockSpec((1, d, tn), lambda l, j: (l, 0, j)),
        out_shape=jax.ShapeDtypeStruct((nl, d, n), F32),
        compiler_params=_params(("parallel", "parallel")),
    )(c_pad, dmod_pad)


def loss_head(y, target, name):
    t, d = y.shape
    tr = _tile(t, 256, 8)

    def body(y_ref, t_ref, dy_ref, l_ref):
        err = y_ref[...] - t_ref[...]
        dy_ref[...] = err * (1.0 / d)
        part = jnp.sum(jnp.sum(err * err, axis=1, keepdims=True), axis=0, keepdims=True) * (0.5 / d)

        @pl.when(pl.program_id(0) == 0)
        def _():
            l_ref[...] = jnp.zeros_like(l_ref)

        l_ref[...] += part

    row = pl.BlockSpec((tr, d), lambda i: (i, 0))
    return pl.pallas_call(
        body, name=name, grid=(t // tr,), in_specs=[row, row],
        out_specs=[row, pl.BlockSpec((8, LANE), lambda i: (0, 0))],
        out_shape=[jax.ShapeDtypeStruct((t, d), F32), jax.ShapeDtypeStruct((8, LANE), F32)],
        compiler_params=_params(("arbitrary",)),
    )(y, target)


def _adamw_update(g, w_ref, m_ref, v_ref, go_ref, d_ref, mo_ref, vo_ref):
    m_new = ADAM_B1 * m_ref[...] + (1.0 - ADAM_B1) * g
    v_new = ADAM_B2 * v_ref[...] + (1.0 - ADAM_B2) * jnp.square(g)
    m_hat = m_new / (1.0 - ADAM_B1 ** ADAM_STEP)
    v_hat = v_new / (1.0 - ADAM_B2 ** ADAM_STEP)
    go_ref[...] = g
    d_ref[...] = -ADAM_LR * (m_hat / (jnp.sqrt(v_hat) + ADAM_EPS) + ADAM_WD * w_ref[...])
    mo_ref[...] = m_new
    vo_ref[...] = v_new


def adamw(w, m, v, g_slots, name):
    shape = w.shape
    nslot = g_slots.shape[0]
    if w.ndim == 2:
        w3, m3, v3, g4 = w[None], m[None], v[None], g_slots[:, None]
    else:
        w3, m3, v3, g4 = w, m, v, g_slots
    nl, r, c = w3.shape
    tr = _tile(r, 256, 16)

    def body(w_ref, m_ref, v_ref, g_ref, *outs):
        g = g_ref[0].astype(F32)
        for s in range(1, nslot):
            g = g + g_ref[s].astype(F32)
        _adamw_update(g, w_ref, m_ref, v_ref, *outs)

    blk = pl.BlockSpec((1, tr, c), lambda l, i: (l, i, 0))
    gblk = pl.BlockSpec((nslot, 1, tr, c), lambda l, i: (0, l, i, 0))
    sh = jax.ShapeDtypeStruct(w3.shape, F32)
    outs = pl.pallas_call(
        body, name=name, grid=(nl, r // tr), in_specs=[blk, blk, blk, gblk], out_specs=[blk] * 4,
        out_shape=[sh] * 4, compiler_params=_params(("parallel", "parallel")),
    )(w3, m3, v3, g4)
    return tuple(o.reshape(shape) for o in outs)


def adamw_layers(w, m, v, g_layers, name):
    nl, r, c = w.shape
    nslot, _, cg = g_layers[0].shape
    tr = _tile(r, 256, 16)

    def body(w_ref, m_ref, v_ref, *rest):
        g_refs, outs = rest[:nl], rest[nl:]
        for li in range(nl):
            @pl.when(pl.program_id(0) == li)
            def _(g_ref=g_refs[li]):
                g = g_ref[0, :, :c].astype(F32)
                for s in range(1, nslot):
                    g = g + g_ref[s, :, :c].astype(F32)
                _adamw_update(g[None], w_ref, m_ref, v_ref, *outs)

    blk = pl.BlockSpec((1, tr, c), lambda l, i: (l, i, 0))
    gblks = [pl.BlockSpec((nslot, tr, cg), lambda l, i, li=li: (0, jnp.where(l == li, i, 0), 0)) for li in range(nl)]
    sh = jax.ShapeDtypeStruct(w.shape, F32)
    return tuple(pl.pallas_call(
        body, name=name, grid=(nl, r // tr), in_specs=[blk, blk, blk] + gblks, out_specs=[blk] * 4,
        out_shape=[sh] * 4, compiler_params=_params(("arbitrary", "arbitrary")),
    )(w, m, v, *g_layers))


def _rows(flat, unit=16):
    n = flat.shape[0]
    per = 1024 * unit
    pad = (-n) % per
    if pad:
        flat = jnp.concatenate([flat, jnp.zeros((pad,), flat.dtype)])
    return flat.reshape(-1, 1024)


def _to_classes(a):
    t, w = a.shape[1], a.shape[2]
    return jnp.stack([a[gi].reshape(t // dil, dil, w).transpose(1, 0, 2).reshape(t, w)
                      for gi, dil in enumerate(DILATIONS)])


def _from_classes(a):
    t, w = a.shape[1], a.shape[2]
    return jnp.stack([a[gi].reshape(dil, t // dil, w).transpose(1, 0, 2).reshape(t, w)
                      for gi, dil in enumerate(DILATIONS)])


def _unshard_cols(g):
    return g.transpose(1, 0, 2).reshape(g.shape[1], -1)


def _shard_cols(full):
    r = full.shape[0]
    return full.reshape(r, N_DEV, -1).transpose(1, 0, 2)


def kernel(x, c, ada_w, ada_b, norm_ff1, ffn1_w_up, ffn1_w_down, norm_mix, w_in, q_norm, k_norm, conv_w, a_log, dt_bias, dn_norm, w_proj_att, w_proj_dn, w_out, norm_ff2, ffn2_w_up, ffn2_w_down, loss_target, m_ada_w, m_ada_b, m_norm_ff1, m_ffn1_w_up, m_ffn1_w_down, m_norm_mix, m_w_in, m_q_norm, m_k_norm, m_conv_w, m_a_log, m_dt_bias, m_dn_norm, m_w_proj_att, m_w_proj_dn, m_w_out, m_norm_ff2, m_ffn2_w_up, m_ffn2_w_down, v_ada_w, v_ada_b, v_norm_ff1, v_ffn1_w_up, v_ffn1_w_down, v_norm_mix, v_w_in, v_q_norm, v_k_norm, v_conv_w, v_a_log, v_dt_bias, v_dn_norm, v_w_proj_att, v_w_proj_dn, v_w_out, v_norm_ff2, v_ffn2_w_up, v_ffn2_w_down):
    nl = ada_w.shape[0]
    t, d = x.shape[1], x.shape[2]
    dff = ffn1_w_down.shape[1] * N_DEV
    ha = d // 256
    ng = len(DILATIONS)
    wa = ha * ATT_HEAD_DIM
    att_w = ng * wa
    nh = d // DN_HEAD_DIM
    dn_w = nh * DN_HEAD_DIM
    n_in = w_in.shape[2] * N_DEV
    off_dn, off_gate = 3 * att_w, 3 * att_w + 3 * dn_w
    off_a = off_gate + dn_w
    off_merge = off_a + 2 * nh
    assert off_merge + 2 * d == n_in
    col_dn, col_gate, col_m = 0, 3 * dn_w, 4 * dn_w
    col_att = col_m + 2 * d
    col_ab = col_att + 3 * att_w
    zw = col_ab + 2 * LANE
    me = 4 * lax.axis_index("x") + 2 * lax.axis_index("y") + lax.axis_index("c")
    xs = x[0]
    target = loss_target[0]

    conv_rows = _rows(conv_w.reshape(-1), 8)
    pack0 = jnp.concatenate([jnp.concatenate([c, jnp.zeros((7, d), F32)]).reshape(-1), conv_rows.reshape(-1)])
    pack0 = _rows(pack0, 8)
    g0 = all_gather([pack0], "ag_c_conv")[0].reshape(N_DEV, -1)
    c_all = g0[:, :d]
    cw = g0[:, 8 * d:8 * d + conv_w.size].reshape(N_DEV, nl, CONV_WIDTH, -1)
    conv_full = cw.transpose(1, 2, 0, 3).reshape(nl, CONV_WIDTH, 3 * dn_w)

    n_ada = ada_w.shape[2]
    b_mine = lax.dynamic_slice_in_dim(ada_b, me * n_ada, n_ada, axis=1)[:, None, :]
    mod_s = ada_fwd(c_all, ada_w, b_mine, "ada_fwd")
    gm = all_gather([mod_s], "ag_mod")[0]
    mod = lax.dynamic_index_in_dim(gm, me, axis=2, keepdims=False)
    mod = mod.transpose(1, 0, 2).reshape(nl, N_ADA, 1, d)

    kinds = [ffn1_w_up, ffn1_w_down, w_in, w_proj_att, w_proj_dn, w_out, ffn2_w_up, ffn2_w_down]
    c_up = ffn1_w_up.shape[2]
    cp = -(-c_up // LANE) * LANE
    r_dn = ffn1_w_down.shape[1]
    assert 2 * r_dn == c_up

    up1_t, up2_t = jnp.swapaxes(ffn1_w_up, 1, 2), jnp.swapaxes(ffn2_w_up, 1, 2)
    w_in_t = jnp.transpose(w_in, (2, 0, 1))

    def layer_shards(l):
        def pad_up(wt):
            return jnp.pad(wt.astype(CD), ((0, cp - c_up), (0, 0)))
        return [pad_up(up1_t[l]), ffn1_w_down[l].astype(CD), w_in_t[:, l].astype(CD), w_proj_att[l].astype(CD),
                w_proj_dn[l].astype(CD), w_out[l].astype(CD), pad_up(up2_t[l]), ffn2_w_down[l].astype(CD)]

    def gather_landing(shards):
        return [lax.dynamic_update_index_in_dim(lax.empty((N_DEV,) + s.shape, s.dtype), s, me, 0) for s in shards]

    def slot_landing(arrs):
        return [lax.dynamic_update_index_in_dim(lax.empty(a.shape, a.dtype),
                                                lax.dynamic_index_in_dim(a, me, 0, keepdims=False), me, 0) for a in arrs]

    gathered = all_gather(layer_shards(0), "ag_weights0")
    prefetch = None

    def full_weight(ki, l):
        blk = gathered[ki]
        if ki == 3:
            return _unshard_cols(blk)
        if ki == 2:
            return w_in_rows(blk.reshape(n_in, d))
        if ki in (1, 7):
            pairs = blk.reshape(HALF, c_up, d)
            return jnp.pad(pairs, ((0, 0), (0, cp - c_up), (0, 0))).reshape(HALF * cp, d)
        return blk.reshape(-1, blk.shape[2])

    def down_grad_slots(g):
        return g.reshape(HALF, cp, d)[:, :c_up].reshape(N_DEV, r_dn, d)

    def w_in_rows(wt):
        pad = jnp.zeros((zw - n_in, wt.shape[1]), wt.dtype)
        return jnp.concatenate([wt[off_dn:off_a], wt[off_merge:], wt[:off_dn], wt[off_a:off_merge], pad], axis=0)

    def w_in_cols_inv(g):
        return jnp.concatenate([g[:, col_att:col_ab], g[:, :col_m], g[:, col_ab:col_ab + 2 * nh], g[:, col_m:col_att]], axis=1)

    saved = []
    xc = xs
    mods = []
    for l in range(nl):
        sv = {}
        if prefetch is not None:
            gathered = exchange_wait(*prefetch[:4], xc, True, f"ag_wait{l}")
        mod_l = mod[l]
        if l + 1 < nl:
            shards = layer_shards(l + 1)
            behind = gathered[0][0, :1, :1].astype(F32) + mod[0, 0, :, :1]
            prefetch = exchange_start(shards, gather_landing(shards), behind, True, f"ag_start{l + 1}")
            mod_l = mod_l + prefetch[4][0, 0]
        mods.append(mod_l)
        sh1, sc1, gt1, sh2, sc2, gt2, sh3, sc3, gt3 = [mod_l[i] for i in range(N_ADA)]
        w_dn1, w_dn2 = full_weight(1, l), full_weight(7, l)
        win = full_weight(2, l)
        wpa, wpd, wo = full_weight(3, l), full_weight(4, l), full_weight(5, l)
        sv["w"] = (gathered[0], gathered[6], w_dn1, w_dn2, win, wpa, wpd, wo)

        def ffn(xin, g, sh, sc, gt, u_all, w_dn):
            h = norm_mod(xin, g, sc, sh, "norm_mod")
            gate, up, a = ffn_up(h, u_all, "ffn_up")
            xo, f = matmul([(a, w_dn)], "nn", F32, "ffn_down", out_scale=0.5 * gt, resid=xin, save_acc=True,
                           tm=1024, tk=4096)
            return xo, (xin, h, gate, up, a, f)

        xc, sv["ffn1"] = ffn(xc, norm_ff1[l:l + 1], sh1, sc1, gt1, gathered[0], w_dn1)

        x_mix = xc
        h2 = norm_mod(x_mix, norm_mix[l:l + 1], sc2, sh2, "norm_mod")
        z = matmul([(h2, win)], "nt", F32, "w_in", tm=1024, tn=512)
        z_att = z[:, col_att:col_ab].reshape(t, 3, ng, wa).transpose(1, 2, 0, 3)
        qkv = jnp.stack([_to_classes(z_att[i]) for i in range(3)])
        qg, kg = jnp.tile(q_norm[l:l + 1], (1, ha)), jnp.tile(k_norm[l:l + 1], (1, ha))
        o_cls, lse_cls = att_fwd(qkv, qg, kg, "att_fwd")
        o_tok, lse_tok = _from_classes(o_cls), _from_classes(lse_cls)
        ya = att_combine(o_tok, lse_tok, "att_combine")
        cvo = conv_fwd(z, col_dn, conv_full[l], "conv_fwd")
        al3, dt3 = a_log[l].reshape(nh, 1, 1), dt_bias[l].reshape(nh, 1, 1)
        gn = dn_norm[l:l + 1]
        yd, states = dn_fwd(cvo, z, col_gate, col_ab, al3, dt3, gn, "dn_fwd")
        mrg, pa, pd = merge_fwd(ya, wpa, yd, wpd, z, col_m, "merge_fwd")
        xc, f2 = matmul([(mrg, wo)], "nn", F32, "w_out", out_scale=gt2, resid=x_mix, save_acc=True)
        sv["mix"] = (x_mix, h2, z, qkv, o_tok, lse_tok, ya, cvo, yd, states, mrg, pa, pd, f2)

        xc, sv["ffn2"] = ffn(xc, norm_ff2[l:l + 1], sh3, sc3, gt3, gathered[6], w_dn2)
        saved.append(sv)

    dxc, loss_blk = loss_head(xc, target, "loss_head")

    def slots(ki, g):
        if ki in (0, 1, 6, 7):
            return g
        return _shard_cols(g) if ki in (2, 3) else g.reshape(N_DEV, -1, g.shape[1])

    gbig = [None] * len(kinds)
    dmods, small, pending = [], [], []
    token = None
    for l in reversed(range(nl)):
        sv = saved[l]
        mod_l = mods[l] if token is None else mods[l] + token[0, 0]
        sh1, sc1, gt1, sh2, sc2, gt2, sh3, sc3, gt3 = [mod_l[i] for i in range(N_ADA)]
        u_all1, u_all2, w_dn1, w_dn2, win, wpa, wpd, wo = sv["w"]

        def ffn_bwd(dxo, g, sh, sc, gt, u_all, w_dn, sv_f):
            xin, h, gate, up, a, f = sv_f
            s = 0.5 * gt
            g_dn = down_grad_slots(matmul([(a, dxo)], "tn", CD, "ffn_down_wg", out_scale=s, tm=1024, tn=1024))
            dgate, dup = ffn_dact(dxo, s, w_dn, gate, up, "ffn_dact")
            g_up = ffn_up_wg(h, dgate, dup, cp, "ffn_up_wg")
            dh = ffn_up_dg(dgate, dup, u_all, "ffn_up_dg")
            dx, dg, dsc, dsh, dgt = norm_mod_bwd(xin, g, sc, sh, dh, dxo, f, 0.5, "norm_mod_bwd")
            return dx, g_up, g_dn, (dg, dsc, dsh, dgt)

        dxc, gbig[6], gbig[7], (dg3, dsc3, dsh3, dgt3) = ffn_bwd(
            dxc, norm_ff2[l:l + 1], sh3, sc3, gt3, u_all2, w_dn2, sv["ffn2"])

        x_mix, h2, z, qkv, o_tok, lse_tok, ya, cvo, yd, states, mrg, pa, pd, f2 = sv["mix"]
        gbig[5] = matmul([(mrg, dxc)], "tn", CD, "w_out_wg", out_scale=gt2)
        dm = matmul([(dxc, wo)], "nt", F32, "w_out_dg", a_scale=gt2)
        dpa, dpd, dz1, dz2 = merge_bwd(dm, pa, pd, z, col_m, "merge_bwd")
        gbig[3] = matmul([(ya, dpa)], "tn", CD, "patt_wg")
        gbig[4] = matmul([(yd, dpd)], "tn", CD, "pdn_wg")
        dya = matmul([(dpa, wpa)], "nt", F32, "patt_dg")
        dyd = matmul([(dpd, wpd)], "nt", F32, "pdn_dg")
        do_tok, dlse_tok = att_combine_bwd(o_tok, lse_tok, dya, "att_combine_bwd")
        qg, kg = jnp.tile(q_norm[l:l + 1], (1, ha)), jnp.tile(k_norm[l:l + 1], (1, ha))
        dq, dk, dv, dqg, dkg = att_bwd(qkv, qg, kg, _to_classes(do_tok), _to_classes(dlse_tok), "att_bwd")
        dqg, dkg = (jnp.sum(v_.reshape(ha, ATT_HEAD_DIM), axis=0, keepdims=True) for v_ in (dqg, dkg))
        dz_att = jnp.stack([_from_classes(a_) for a_ in (dq, dk, dv)])
        dz_att = dz_att.transpose(2, 0, 1, 3).reshape(t, 3 * att_w).astype(CD)
        al3, dt3 = a_log[l].reshape(nh, 1, 1), dt_bias[l].reshape(nh, 1, 1)
        gn = dn_norm[l:l + 1]
        dcvo, dz_gate, dz_ab, dal, ddt, dgn = dn_bwd(
            cvo, z, col_gate, col_ab, al3, dt3, gn, states, dyd, "dn_bwd")
        dz_dn, dconv = conv_bwd(dcvo, z, col_dn, conv_full[l], "conv_bwd")
        dz = jnp.concatenate([dz_dn, dz_gate, dz1, dz2, dz_att, dz_ab, jnp.zeros((t, LANE), CD)], axis=1)
        gbig[2] = w_in_cols_inv(matmul([(h2, dz)], "tn", CD, "w_in_wg", tm=1024, tn=1024))
        dh2 = matmul([(dz, win)], "nn", F32, "w_in_dg", tm=1024, tn=1024,
                     tk=zw // 4 if (zw // 4) % LANE == 0 else zw)
        dxc, dg2, dsc2, dsh2, dgt2 = norm_mod_bwd(x_mix, norm_mix[l:l + 1], sc2, sh2, dh2, dxc, f2, 1.0, "norm_mod_bwd_mix")

        def start_exchange(ids, tag, behind):
            send = [slots(ki, gbig[ki]) for ki in ids]
            started = exchange_start(send, slot_landing(send), behind, False, f"a2a_start{l}{tag}")
            pending.append((l, ids, tag, started))
            return started[4]

        if l == 0:
            token = start_exchange(list(range(2, len(kinds))), "a", dxc)
            sh1, sc1, gt1 = (vec + token[0, 0] for vec in (sh1, sc1, gt1))
        dxc, gbig[0], gbig[1], (dg1, dsc1, dsh1, dgt1) = ffn_bwd(
            dxc, norm_ff1[l:l + 1], sh1, sc1, gt1, u_all1, w_dn1, sv["ffn1"])
        if l > 0:
            token = start_exchange(list(range(len(kinds))), "", dxc)
        dmods.append(jnp.concatenate([dsh1, dsc1, dgt1, dsh2, dsc2, dgt2, dsh3, dsc3, dgt3], axis=1))
        small.append((dg1, dg2, dg3, dconv, dqg, dkg, dal.reshape(1, nh), ddt.reshape(1, nh), dgn))
    dmods.reverse()
    small.reverse()

    fields = [jnp.stack(dmods).reshape(-1)]
    fields += [jnp.stack([s[i] for s in small]).reshape(-1) for i in range(9)]
    fields.append(loss_blk[0, :1])
    fsizes = [f.size for f in fields]
    foffs = [sum(fsizes[:i]) for i in range(len(fields))]
    g1 = all_gather([_rows(jnp.concatenate(fields), 8)], "ag_small")[0].reshape(N_DEV, -1)
    l = 0
    start_exchange([0, 1], "b", g1)

    def field(i, shape):
        return g1[:, foffs[i]:foffs[i] + fsizes[i]].reshape(N_DEV, *shape)

    loss = field(10, (1,))[0, 0]
    for j in range(1, N_DEV):
        loss = loss + field(10, (1,))[j, 0]

    results = {}
    dmod_all = field(0, (nl, N_ADA * d))
    c_pad = jnp.concatenate([c_all, jnp.zeros((8, d), F32)])
    dmod_mine = lax.dynamic_slice_in_dim(dmod_all, me * n_ada, n_ada, axis=2).transpose(1, 0, 2)
    dmod_pad = jnp.concatenate([dmod_mine, jnp.zeros((nl, 8, n_ada), F32)], axis=1)
    g_ada_w = ada_bwd(c_pad, dmod_pad, "ada_bwd")
    results["ada_w"] = adamw(ada_w, m_ada_w, v_ada_w, g_ada_w[None], "adamw_ada_w")
    results["ada_b"] = adamw(ada_b, m_ada_b, v_ada_b, dmod_all, "adamw_ada_b")
    results["norm_ff1"] = adamw(norm_ff1, m_norm_ff1, v_norm_ff1, field(1, (nl, d)), "adamw_norm_ff1")
    results["norm_mix"] = adamw(norm_mix, m_norm_mix, v_norm_mix, field(2, (nl, d)), "adamw_norm_mix")
    results["norm_ff2"] = adamw(norm_ff2, m_norm_ff2, v_norm_ff2, field(3, (nl, d)), "adamw_norm_ff2")
    conv_slots = lax.dynamic_slice_in_dim(field(4, (nl, CONV_WIDTH, 3 * dn_w)), me * conv_w.shape[2],
                                          conv_w.shape[2], axis=3)
    results["conv_w"] = adamw(conv_w, m_conv_w, v_conv_w, conv_slots, "adamw_conv_w")
    results["q_norm"] = adamw(q_norm, m_q_norm, v_q_norm, field(5, (nl, ATT_HEAD_DIM)), "adamw_q_norm")
    results["k_norm"] = adamw(k_norm, m_k_norm, v_k_norm, field(6, (nl, ATT_HEAD_DIM)), "adamw_k_norm")
    results["a_log"] = adamw(a_log, m_a_log, v_a_log, field(7, (nl, nh)), "adamw_a_log")
    results["dt_bias"] = adamw(dt_bias, m_dt_bias, v_dt_bias, field(8, (nl, nh)), "adamw_dt_bias")
    results["dn_norm"] = adamw(dn_norm, m_dn_norm, v_dn_norm, field(9, (nl, DN_HEAD_DIM)), "adamw_dn_norm")
    big_names = ["ffn1_w_up", "ffn1_w_down", "w_in", "w_proj_att", "w_proj_dn", "w_out", "ffn2_w_up", "ffn2_w_down"]
    big_m = [m_ffn1_w_up, m_ffn1_w_down, m_w_in, m_w_proj_att, m_w_proj_dn, m_w_out, m_ffn2_w_up, m_ffn2_w_down]
    big_v = [v_ffn1_w_up, v_ffn1_w_down, v_w_in, v_w_proj_att, v_w_proj_dn, v_w_out, v_ffn2_w_up, v_ffn2_w_down]
    recv_layers = [[None] * len(kinds) for _ in range(nl)]
    for l, ids, tag, st in pending:
        got = exchange_wait(*st[:4], results["ada_w"][1], False, f"a2a_wait{l}{tag}")
        for ki, arr in zip(ids, got):
            recv_layers[l][ki] = arr
    for ki, nm in enumerate(big_names):
        g_layers = [recv_layers[l][ki] for l in range(nl)]
        if ki in (0, 6):
            res = adamw_layers(*(jnp.swapaxes(a, 1, 2) for a in (kinds[ki], big_m[ki], big_v[ki])), g_layers,
                               f"adamw_{nm}")
            results[nm] = tuple(jnp.swapaxes(r, 1, 2) for r in res)
        else:
            results[nm] = adamw_layers(kinds[ki], big_m[ki], big_v[ki], g_layers, f"adamw_{nm}")

    order = ["ada_w", "ada_b", "norm_ff1", "ffn1_w_up", "ffn1_w_down", "norm_mix", "w_in", "q_norm", "k_norm",
             "conv_w", "a_log", "dt_bias", "dn_norm", "w_proj_att", "w_proj_dn", "w_out", "norm_ff2",
             "ffn2_w_up", "ffn2_w_down"]
    outs = [loss, dxc[None]]
    for part in range(4):
        outs += [results[n][part] for n in order]
    return tuple(outs)
```

```python
import functools

import jax
import jax.numpy as jnp
from jax import lax
from jax.experimental import pallas as pl
from jax.experimental.pallas import tpu as pltpu

F32 = jnp.float32
CD = jnp.bfloat16
EPS = 1e-6
N_DEV = 8
LANE = 128
ATT_HEAD_DIM = 64
ATT_BLOCK = 128
DILATIONS = (1, 4, 16)
DN_HEAD_DIM = 128
DN_CHUNK = 64
CONV_WIDTH = 4
N_ADA = 9
ADAM_LR, ADAM_B1, ADAM_B2, ADAM_EPS, ADAM_WD, ADAM_STEP = 0.001, 0.9, 0.999, 1e-08, 0.01, 10
VMEM_LIMIT = 56 * 1024 * 1024
NEG = -1e30
MESH = pl.DeviceIdType.MESH
HI = lax.Precision.HIGHEST


def _params(sem=None):
    return pltpu.CompilerParams(dimension_semantics=sem, vmem_limit_bytes=VMEM_LIMIT)


def _tile(n, pref, unit):
    best = None
    t = unit
    while t <= min(n, pref):
        if n % t == 0:
            best = t
        t += unit
    return best if best is not None else n


def _silu(x):
    return x * jax.nn.sigmoid(x)


def _dot(a, b, dims, precision=None):
    if precision is None:
        a, b = a.astype(CD), b.astype(CD)
    return lax.dot_general(a, b, (dims, ((), ())), precision=precision, preferred_element_type=F32)


def _nn(a, b, precision=None):
    return _dot(a, b, ((1,), (0,)), precision)


def _nt(a, b, precision=None):
    return _dot(a, b, ((1,), (1,)), precision)


def _tn(a, b, precision=None):
    return _dot(a, b, ((0,), (0,)), precision)


def all_gather(arrs, name):
    n = len(arrs)

    def body(*refs):
        x_refs, out_refs = refs[:n], refs[n:2 * n]
        send_sems, recv_sems, local_sems = refs[2 * n:]
        x, y, c = lax.axis_index("x"), lax.axis_index("y"), lax.axis_index("c")
        me, sibling = (x, y, c), (x, y, 1 - c)
        chips = [(1 - x, y), (x, 1 - y), (1 - x, 1 - y)]

        def copy(a, k, block, to, src=None):
            slot = out_refs[a].at[4 * block[0] + 2 * block[1] + block[2]]
            return pltpu.make_async_remote_copy(
                src_ref=slot if src is None else src, dst_ref=slot,
                send_sem=send_sems.at[7 * a + k], recv_sem=recv_sems.at[7 * a + k],
                device_id=to, device_id_type=MESH)

        mine, first, passed = [], [], []
        for a in range(n):
            cp = pltpu.make_async_copy(x_refs[a], out_refs[a].at[4 * x + 2 * y + c], local_sems.at[a])
            cp.start()
            mine.append(cp)
            first.append(copy(a, 0, me, sibling, src=x_refs[a]))
            first += [copy(a, 1 + j, me, (*chip, c), src=x_refs[a]) for j, chip in enumerate(chips)]
        for cp in first:
            cp.start()
        for j, chip in enumerate(chips):
            for a in range(n):
                copy(a, 1 + j, (*chip, c), me).wait_recv()
                cp = copy(a, 4 + j, (*chip, c), sibling)
                cp.start()
                passed.append(cp)
        for a in range(n):
            copy(a, 0, sibling, me).wait_recv()
            for j, chip in enumerate(chips):
                copy(a, 4 + j, (*chip, 1 - c), me).wait_recv()
        for cp in first + passed:
            cp.wait_send()
        for cp in mine:
            cp.wait()

    hbm = pl.BlockSpec(memory_space=pl.ANY)
    outs = pl.pallas_call(
        body, name=name,
        out_shape=[jax.ShapeDtypeStruct((N_DEV,) + a.shape, a.dtype) for a in arrs],
        in_specs=[hbm] * n, out_specs=[hbm] * n,
        scratch_shapes=[pltpu.SemaphoreType.DMA((7 * n,)), pltpu.SemaphoreType.DMA((7 * n,)),
                        pltpu.SemaphoreType.DMA((n,))],
    )(*arrs)
    return list(outs)


def all_to_all(arrs, name):
    n = len(arrs)

    def body(*refs):
        x_refs, out_refs = refs[:n], refs[n:2 * n]
        send_sems, recv_sems, local_sems = refs[2 * n:]
        x, y, c = lax.axis_index("x"), lax.axis_index("y"), lax.axis_index("c")
        me = 4 * x + 2 * y + c
        mine, sends = [], []
        for a in range(n):
            cp = pltpu.make_async_copy(x_refs[a].at[me], out_refs[a].at[me], local_sems.at[a])
            cp.start()
            mine.append(cp)
        peers = []
        for k in range(1, N_DEV):
            px, py, pc = x ^ (k >> 2), y ^ ((k >> 1) & 1), c ^ (k & 1)
            peers.append((px, py, pc, 4 * px + 2 * py + pc))
        for k in range(1, N_DEV):
            px, py, pc, pidx = peers[k - 1]
            for a in range(n):
                cp = pltpu.make_async_remote_copy(
                    src_ref=x_refs[a].at[pidx], dst_ref=out_refs[a].at[me],
                    send_sem=send_sems.at[7 * a + k - 1], recv_sem=recv_sems.at[7 * a + k - 1],
                    device_id=(px, py, pc), device_id_type=MESH)
                cp.start()
                sends.append(cp)
        for k in range(1, N_DEV):
            pidx = peers[k - 1][3]
            for a in range(n):
                pltpu.make_async_remote_copy(
                    src_ref=x_refs[a].at[pidx], dst_ref=out_refs[a].at[pidx],
                    send_sem=send_sems.at[7 * a + k - 1], recv_sem=recv_sems.at[7 * a + k - 1],
                    device_id=(x, y, c), device_id_type=MESH).wait_recv()
        for cp in sends:
            cp.wait_send()
        for cp in mine:
            cp.wait()

    hbm = pl.BlockSpec(memory_space=pl.ANY)
    outs = pl.pallas_call(
        body, name=name,
        out_shape=[jax.ShapeDtypeStruct(a.shape, a.dtype) for a in arrs],
        in_specs=[hbm] * n, out_specs=[hbm] * n,
        scratch_shapes=[pltpu.SemaphoreType.DMA((7 * n,)), pltpu.SemaphoreType.DMA((7 * n,)),
                        pltpu.SemaphoreType.DMA((n,))],
    )(*arrs)
    return list(outs)


_HBM = pl.BlockSpec(memory_space=pltpu.HBM)
_SEM = pl.BlockSpec(memory_space=pltpu.SEMAPHORE)
_EFFECT = pltpu.SideEffectType.DATAFLOW_SIDE_EFFECTING


def _exchange_copies(src_refs, land_refs, send_sem, recv_sem, gather):
    x, y, c = lax.axis_index("x"), lax.axis_index("y"), lax.axis_index("c")
    me = 4 * x + 2 * y + c
    pairs = []
    for k in range(1, N_DEV):
        px, py, pc = x ^ (k >> 2), y ^ ((k >> 1) & 1), c ^ (k & 1)
        pidx = 4 * px + 2 * py + pc
        for src, land in zip(src_refs, land_refs):
            mine = src if gather else src.at[pidx]
            out = pltpu.make_async_remote_copy(src_ref=mine, dst_ref=land.at[me], send_sem=send_sem, recv_sem=recv_sem,
                                               device_id=(px, py, pc), device_id_type=MESH)
            inc = pltpu.make_async_remote_copy(src_ref=mine, dst_ref=land.at[pidx], send_sem=send_sem, recv_sem=recv_sem,
                                               device_id=(px, py, pc), device_id_type=MESH)
            pairs.append((out, inc))
    return pairs


def exchange_start(srcs, lands, after, gather, name):
    n = len(srcs)

    def body(*refs):
        src_refs, land_refs = refs[:n], refs[n:2 * n]
        send_sem, recv_sem = refs[2 * n + 1], refs[2 * n + 2]
        token = refs[-1]
        for out, _ in _exchange_copies(src_refs, land_refs, send_sem, recv_sem, gather):
            out.start()
        token[...] = jnp.zeros_like(token)

    res = pl.pallas_call(
        body, name=name,
        out_shape=(pltpu.SemaphoreType.DMA(()), pltpu.SemaphoreType.DMA(()),
                   *[pltpu.HBM(a.shape, a.dtype) for a in srcs], *[pltpu.HBM(a.shape, a.dtype) for a in lands],
                   jax.ShapeDtypeStruct((8, LANE), F32)),
        in_specs=[_HBM] * (2 * n) + [pl.BlockSpec(memory_space=pl.ANY)],
        out_specs=(_SEM, _SEM, *[_HBM] * (2 * n), pl.BlockSpec(memory_space=pltpu.VMEM)),
        input_output_aliases={i: 2 + i for i in range(2 * n)},
        compiler_params=pltpu.CompilerParams(has_side_effects=_EFFECT),
    )(*[pltpu.with_memory_space_constraint(a, pltpu.HBM) for a in list(srcs) + list(lands)], after)
    return res[0], res[1], list(res[2:2 + n]), list(res[2 + n:2 + 2 * n]), res[-1]


def exchange_wait(send_sem, recv_sem, srcs, lands, after, gather, name):
    n = len(srcs)

    def body(*refs):
        src_refs, land_refs = refs[:n], refs[n:2 * n]
        s_sem, r_sem = refs[2 * n], refs[2 * n + 1]
        for out, inc in _exchange_copies(src_refs, land_refs, s_sem, r_sem, gather):
            out.wait_send()
            inc.wait_recv()

    res = pl.pallas_call(
        body, name=name,
        out_shape=[pltpu.HBM(a.shape, a.dtype) for a in list(srcs) + list(lands)],
        in_specs=[_HBM] * (2 * n) + [_SEM, _SEM, pl.BlockSpec(memory_space=pl.ANY)],
        out_specs=[_HBM] * (2 * n),
        input_output_aliases={i: i for i in range(2 * n)},
        compiler_params=pltpu.CompilerParams(has_side_effects=_EFFECT),
    )(*srcs, *lands, send_sem, recv_sem, after)
    return list(res[n:])


def matmul(pairs, mode, out_dtype, name, *, a_scale=None, out_scale=None, resid=None,
           save_acc=False, tm=512, tn=512, tk=2048):
    a0, b0 = pairs[0]
    if mode == "nn":
        (m, kdim), n = a0.shape, b0.shape[1]
    elif mode == "nt":
        (m, kdim), n = a0.shape, b0.shape[0]
    else:
        (kdim, m), n = a0.shape, b0.shape[1]
    tm, tn = _tile(m, tm, 8 if m % LANE else LANE), _tile(n, tn, LANE)
    tk = _tile(kdim, tk, LANE)
    nk = kdim // tk
    npairs = len(pairs)
    dims = {"nn": ((1,), (0,)), "nt": ((1,), (1,)), "tn": ((0,), (0,))}[mode]

    if mode == "nn":
        a_spec = pl.BlockSpec((tm, tk), lambda i, j, k: (i, k))
        b_spec = pl.BlockSpec((tk, tn), lambda i, j, k: (k, j))
    elif mode == "nt":
        a_spec = pl.BlockSpec((tm, tk), lambda i, j, k: (i, k))
        b_spec = pl.BlockSpec((tn, tk), lambda i, j, k: (j, k))
    else:
        a_spec = pl.BlockSpec((tk, tm), lambda i, j, k: (k, i))
        b_spec = pl.BlockSpec((tk, tn), lambda i, j, k: (k, j))

    def body(*refs):
        it = iter(refs)
        pair_refs = [(next(it), next(it)) for _ in range(npairs)]
        as_ref = next(it) if a_scale is not None else None
        os_ref = next(it) if out_scale is not None else None
        rs_ref = next(it) if resid is not None else None
        o_ref = next(it)
        acc_out = next(it) if save_acc else None
        acc_ref = next(it) if nk > 1 else None
        part = None
        for a_ref, b_ref in pair_refs:
            a = a_ref[...]
            if as_ref is not None:
                a = a.astype(F32) * (as_ref[...] if mode != "tn" else as_ref[...].reshape(tk, 1))
            d = _dot(a, b_ref[...], dims)
            part = d if part is None else part + d

        def finish(acc):
            if acc_out is not None:
                acc_out[...] = acc
            if os_ref is not None:
                acc = acc * os_ref[...]
            if rs_ref is not None:
                acc = rs_ref[...] + acc
            o_ref[...] = acc.astype(o_ref.dtype)

        if nk == 1:
            finish(part)
        else:
            k = pl.program_id(2)

            @pl.when(k == 0)
            def _():
                acc_ref[...] = part

            @pl.when(k > 0)
            def _():
                acc_ref[...] += part

            @pl.when(k == nk - 1)
            def _():
                finish(acc_ref[...])

    in_specs, args = [], []
    for a, b in pairs:
        in_specs += [a_spec, b_spec]
        args += [a, b]
    if a_scale is not None:
        assert mode != "tn"
        in_specs.append(pl.BlockSpec((1, tk), lambda i, j, k: (0, k)))
        args.append(a_scale)
    if out_scale is not None:
        in_specs.append(pl.BlockSpec((1, tn), lambda i, j, k: (0, j)))
        args.append(out_scale)
    if resid is not None:
        in_specs.append(pl.BlockSpec((tm, tn), lambda i, j, k: (i, j)))
        args.append(resid)
    o_spec = pl.BlockSpec((tm, tn), lambda i, j, k: (i, j))
    out_shape = [jax.ShapeDtypeStruct((m, n), out_dtype)]
    out_specs = [o_spec]
    if save_acc:
        out_shape.append(jax.ShapeDtypeStruct((m, n), F32))
        out_specs.append(o_spec)
    scratch = [pltpu.VMEM((tm, tn), F32)] if nk > 1 else []
    res = pl.pallas_call(
        body, name=name, grid=(m // tm, n // tn, nk),
        in_specs=in_specs, out_specs=out_specs, out_shape=out_shape, scratch_shapes=scratch,
        compiler_params=_params(("parallel", "parallel", "arbitrary")),
    )(*args)
    return res if save_acc else res[0]


def _nm_fn(x, g, sc, sh):
    r = lax.rsqrt(jnp.mean(x * x, axis=-1, keepdims=True) + EPS)
    return (x * r * g) * (1.0 + sc) + sh


def norm_mod(x, g, sc, sh, name):
    t, d = x.shape
    tr = _tile(t, 256, 8)

    def body(x_ref, g_ref, sc_ref, sh_ref, h_ref):
        h_ref[...] = _nm_fn(x_ref[...], g_ref[...], sc_ref[...], sh_ref[...]).astype(h_ref.dtype)

    row = pl.BlockSpec((tr, d), lambda i: (i, 0))
    vec = pl.BlockSpec((1, d), lambda i: (0, 0))
    return pl.pallas_call(
        body, name=name, grid=(t // tr,), in_specs=[row, vec, vec, vec], out_specs=row,
        out_shape=jax.ShapeDtypeStruct((t, d), CD), compiler_params=_params(("parallel",)),
    )(x, g, sc, sh)


def norm_mod_bwd(x, g, sc, sh, dh, dxo, f, gate_scale, name):
    t, d = x.shape
    tr = _tile(t, 256, 8)

    def body(x_ref, g_ref, sc_ref, sh_ref, dh_ref, dxo_ref, f_ref, dx_ref, dg_ref, dsc_ref, dsh_ref, dgt_ref):
        _, vjp = jax.vjp(_nm_fn, x_ref[...], g_ref[...], sc_ref[...], sh_ref[...])
        dx, dg, dsc, dsh = vjp(dh_ref[...])
        dxo_v = dxo_ref[...]
        dx_ref[...] = dxo_v + dx
        dgt = gate_scale * jnp.sum(f_ref[...] * dxo_v, axis=0, keepdims=True)

        @pl.when(pl.program_id(0) == 0)
        def _():
            dg_ref[...] = dg
            dsc_ref[...] = dsc
            dsh_ref[...] = dsh
            dgt_ref[...] = dgt

        @pl.when(pl.program_id(0) > 0)
        def _():
            dg_ref[...] += dg
            dsc_ref[...] += dsc
            dsh_ref[...] += dsh
            dgt_ref[...] += dgt

    row = pl.BlockSpec((tr, d), lambda i: (i, 0))
    vec = pl.BlockSpec((1, d), lambda i: (0, 0))
    vshape = jax.ShapeDtypeStruct((1, d), F32)
    return pl.pallas_call(
        body, name=name, grid=(t // tr,), in_specs=[row, vec, vec, vec, row, row, row],
        out_specs=[row, vec, vec, vec, vec],
        out_shape=[jax.ShapeDtypeStruct((t, d), F32), vshape, vshape, vshape, vshape],
        compiler_params=_params(("arbitrary",)),
    )(x, g, sc, sh, dh, dxo, f)


HALF = N_DEV // 2


def ffn_up(h, u_all, name):
    t, d = h.shape
    cp = u_all.shape[1]
    f = HALF * cp
    tm, tn = _tile(t, 1024, LANE), _tile(cp, 384, LANE)
    per = cp // tn

    def body(h_ref, wg_ref, wu_ref, g_ref, u_ref, a_ref):
        hv = h_ref[...]
        gate = _nt(hv, wg_ref[0])
        up = _nt(hv, wu_ref[0])
        g_ref[...] = gate.astype(g_ref.dtype)
        u_ref[...] = up.astype(u_ref.dtype)
        a_ref[...] = (_silu(gate) * up).astype(a_ref.dtype)

    o = pl.BlockSpec((tm, tn), lambda i, j: (i, j))
    wg = pl.BlockSpec((1, tn, d), lambda i, j: (j // per, j % per, 0))
    wu = pl.BlockSpec((1, tn, d), lambda i, j: (HALF + j // per, j % per, 0))
    return pl.pallas_call(
        body, name=name, grid=(t // tm, f // tn),
        in_specs=[pl.BlockSpec((tm, d), lambda i, j: (i, 0)), wg, wu], out_specs=[o, o, o],
        out_shape=[jax.ShapeDtypeStruct((t, f), CD)] * 3,
        compiler_params=_params(("parallel", "parallel")),
    )(h, u_all, u_all)


def ffn_up_wg(h, dgate, dup, cp, name):
    t, d = h.shape
    tn = _tile(d, 512, LANE)

    def body(h_ref, dg_ref, du_ref, o_ref):
        s = pl.program_id(0)

        @pl.when(s < HALF)
        def _():
            o_ref[0] = _tn(dg_ref[...], h_ref[...]).astype(o_ref.dtype)

        @pl.when(s >= HALF)
        def _():
            o_ref[0] = _tn(du_ref[...], h_ref[...]).astype(o_ref.dtype)

    return pl.pallas_call(
        body, name=name, grid=(N_DEV, d // tn),
        in_specs=[pl.BlockSpec((t, tn), lambda s, i: (0, i)),
                  pl.BlockSpec((t, cp), lambda s, i: (0, jnp.minimum(s, HALF - 1))),
                  pl.BlockSpec((t, cp), lambda s, i: (0, jnp.maximum(s - HALF, 0)))],
        out_specs=pl.BlockSpec((1, cp, tn), lambda s, i: (s, 0, i)),
        out_shape=jax.ShapeDtypeStruct((N_DEV, cp, d), CD),
        compiler_params=_params(("parallel", "parallel")),
    )(h, dgate, dup)


def ffn_up_dg(dgate, dup, u_all, name):
    t = dgate.shape[0]
    cp, d = u_all.shape[1], u_all.shape[2]
    tm, tn = _tile(t, 1024, LANE), _tile(d, 1024, LANE)

    def body(dg_ref, du_ref, u_ref, o_ref, acc_ref):
        s = pl.program_id(2)

        @pl.when(s == 0)
        def _():
            acc_ref[...] = jnp.zeros_like(acc_ref)

        @pl.when(s < HALF)
        def _():
            acc_ref[...] += _nn(dg_ref[...], u_ref[0])

        @pl.when(s >= HALF)
        def _():
            acc_ref[...] += _nn(du_ref[...], u_ref[0])

        @pl.when(s == N_DEV - 1)
        def _():
            o_ref[...] = acc_ref[...]

    return pl.pallas_call(
        body, name=name, grid=(t // tm, d // tn, N_DEV),
        in_specs=[pl.BlockSpec((tm, cp), lambda i, j, s: (i, jnp.minimum(s, HALF - 1))),
                  pl.BlockSpec((tm, cp), lambda i, j, s: (i, jnp.maximum(s - HALF, 0))),
                  pl.BlockSpec((1, cp, tn), lambda i, j, s: (s, 0, j))],
        out_specs=pl.BlockSpec((tm, tn), lambda i, j, s: (i, j)),
        out_shape=jax.ShapeDtypeStruct((t, d), F32),
        scratch_shapes=[pltpu.VMEM((tm, tn), F32)],
        compiler_params=_params(("parallel", "parallel", "arbitrary")),
    )(dgate, dup, u_all)


def ffn_dact(dxo, s, wd, gate, up, name):
    t, d = dxo.shape
    f = wd.shape[0]
    tm, tn = _tile(t, 1024, LANE), _tile(f, 384, LANE)

    def body(dxo_ref, s_ref, wd_ref, g_ref, u_ref, dg_ref, du_ref):
        da = _nt(dxo_ref[...] * s_ref[...], wd_ref[...])
        gate, up = g_ref[...].astype(F32), u_ref[...].astype(F32)
        sg = jax.nn.sigmoid(gate)
        dg_ref[...] = (da * up * sg * (1.0 + gate * (1.0 - sg))).astype(dg_ref.dtype)
        du_ref[...] = (da * gate * sg).astype(du_ref.dtype)

    o = pl.BlockSpec((tm, tn), lambda i, j: (i, j))
    return pl.pallas_call(
        body, name=name, grid=(t // tm, f // tn),
        in_specs=[pl.BlockSpec((tm, d), lambda i, j: (i, 0)), pl.BlockSpec((1, d), lambda i, j: (0, 0)),
                  pl.BlockSpec((tn, d), lambda i, j: (j, 0)), o, o],
        out_specs=[o, o],
        out_shape=[jax.ShapeDtypeStruct((t, f), CD), jax.ShapeDtypeStruct((t, f), CD)],
        compiler_params=_params(("parallel", "parallel")),
    )(dxo, s, wd, gate, up)


def _bdot(a, b, ca, cb, precision=None):
    if precision is None:
        a, b = a.astype(CD), b.astype(CD)
    return lax.dot_general(a, b, (((ca,), (cb,)), ((0,), (0,))), precision=precision, preferred_element_type=F32)


def _bnn(a, b, precision=None):
    return _bdot(a, b, 2, 1, precision)


def _bnt(a, b, precision=None):
    return _bdot(a, b, 2, 2, precision)


def _btn(a, b, precision=None):
    return _bdot(a, b, 1, 1, precision)


def _att_fn(q, kp, kc, vp, vc, qg, kg, mask_p, mask_c):
    b, w = q.shape
    nh = w // ATT_HEAD_DIM
    head_of_lane = lax.broadcasted_iota(jnp.int32, (nh, 1, w), 2) // ATT_HEAD_DIM
    hm = (head_of_lane == lax.broadcasted_iota(jnp.int32, (nh, 1, w), 0)).astype(F32)

    def rn(x, g):
        ss = jnp.sum((x * x)[None] * hm, axis=-1, keepdims=True)
        r = jnp.sum(lax.rsqrt(ss * (1.0 / ATT_HEAD_DIM) + EPS) * hm, axis=0)
        return x * r * g

    qn, kpn, kcn = rn(q, qg), rn(kp, kg), rn(kc, kg)
    q4 = (qn[None] * hm).reshape(nh * b, w)
    scale = ATT_HEAD_DIM ** -0.5
    sp = jnp.where(mask_p, _nt(q4, kpn) * scale, NEG)
    sc = jnp.where(mask_c, _nt(q4, kcn) * scale, NEG)
    m = jnp.maximum(jnp.max(sp, axis=-1, keepdims=True), jnp.max(sc, axis=-1, keepdims=True))
    pp, pc = jnp.exp(sp - m), jnp.exp(sc - m)
    den = jnp.sum(pp, axis=-1, keepdims=True) + jnp.sum(pc, axis=-1, keepdims=True)
    o4 = _nn(pp / den, vp) + _nn(pc / den, vc)
    o = jnp.sum(o4.reshape(nh, b, w) * hm, axis=0)
    lse = jnp.sum((m + jnp.log(den)).reshape(nh, b, 1) * hm, axis=0)
    return o, lse


def _att_masks(g, j, nb_total, nh):
    nb = jnp.int32(nb_total // DILATIONS[0])
    for gi in range(1, len(DILATIONS)):
        nb = jnp.where(g == gi, jnp.int32(nb_total // DILATIONS[gi]), nb)
    has_prev = (j % nb) != 0
    row = lax.broadcasted_iota(jnp.int32, (nh * ATT_BLOCK, ATT_BLOCK), 0) % ATT_BLOCK
    col = lax.broadcasted_iota(jnp.int32, (nh * ATT_BLOCK, ATT_BLOCK), 1)
    mask_p = jnp.logical_and(col >= row, has_prev)
    mask_c = col <= row
    return mask_p, mask_c


def _att_specs(w):
    blk = (1, 1, ATT_BLOCK, w)
    q = pl.BlockSpec(blk, lambda g, j: (0, g, j, 0))
    kp = pl.BlockSpec(blk, lambda g, j: (1, g, jnp.maximum(j - 1, 0), 0))
    kc = pl.BlockSpec(blk, lambda g, j: (1, g, j, 0))
    vp = pl.BlockSpec(blk, lambda g, j: (2, g, jnp.maximum(j - 1, 0), 0))
    vc = pl.BlockSpec(blk, lambda g, j: (2, g, j, 0))
    gain = pl.BlockSpec((1, w), lambda g, j: (0, 0))
    out = pl.BlockSpec((1, ATT_BLOCK, w), lambda g, j: (g, j, 0))
    return [q, kp, kc, vp, vc, gain, gain], out


def att_fwd(qkv, qg, kg, name):
    _, ng, t, w = qkv.shape
    nbt = t // ATT_BLOCK
    in_specs, out = _att_specs(w)

    def body(q_ref, kp_ref, kc_ref, vp_ref, vc_ref, qg_ref, kg_ref, o_ref, lse_ref):
        mask_p, mask_c = _att_masks(pl.program_id(0), pl.program_id(1), nbt, w // ATT_HEAD_DIM)
        o, lse = _att_fn(q_ref[0, 0], kp_ref[0, 0], kc_ref[0, 0], vp_ref[0, 0], vc_ref[0, 0],
                         qg_ref[...], kg_ref[...], mask_p, mask_c)
        o_ref[0] = o
        lse_ref[0] = lse

    sh = jax.ShapeDtypeStruct((ng, t, w), F32)
    return pl.pallas_call(
        body, name=name, grid=(ng, nbt), in_specs=in_specs, out_specs=[out, out], out_shape=[sh, sh],
        compiler_params=_params(("parallel", "parallel")),
    )(qkv, qkv, qkv, qkv, qkv, qg, kg)


def att_bwd(qkv, qg, kg, do, dlse, name):
    _, ng, t, w = qkv.shape
    nbt = t // ATT_BLOCK
    in_specs, out = _att_specs(w)
    whole = pl.BlockSpec((1, t, w), lambda g, j: (g, 0, 0))
    gain = in_specs[-1]

    def body(q_ref, kp_ref, kc_ref, vp_ref, vc_ref, qg_ref, kg_ref, do_ref, dlse_ref,
             dq_ref, dk_ref, dv_ref, dqg_ref, dkg_ref):
        g, j = pl.program_id(0), pl.program_id(1)
        mask_p, mask_c = _att_masks(g, j, nbt, w // ATT_HEAD_DIM)

        @pl.when(j == 0)
        def _():
            dk_ref[...] = jnp.zeros_like(dk_ref)
            dv_ref[...] = jnp.zeros_like(dv_ref)

        @pl.when(jnp.logical_and(g == 0, j == 0))
        def _():
            dqg_ref[...] = jnp.zeros_like(dqg_ref)
            dkg_ref[...] = jnp.zeros_like(dkg_ref)

        rows_c = pl.ds(pl.multiple_of(j * ATT_BLOCK, ATT_BLOCK), ATT_BLOCK)
        rows_p = pl.ds(pl.multiple_of(jnp.maximum(j - 1, 0) * ATT_BLOCK, ATT_BLOCK), ATT_BLOCK)
        fn = functools.partial(_att_fn, mask_p=mask_p, mask_c=mask_c)
        _, vjp = jax.vjp(fn, q_ref[0, 0], kp_ref[0, 0], kc_ref[0, 0], vp_ref[0, 0], vc_ref[0, 0],
                         qg_ref[...], kg_ref[...])
        dq, dkp, dkc, dvp, dvc, dqg, dkg = vjp((do_ref[0], dlse_ref[0]))
        dq_ref[0] = dq
        dk_ref[0, rows_p, :] += dkp
        dv_ref[0, rows_p, :] += dvp
        dk_ref[0, rows_c, :] += dkc
        dv_ref[0, rows_c, :] += dvc
        dqg_ref[...] += dqg
        dkg_ref[...] += dkg

    sh = jax.ShapeDtypeStruct((ng, t, w), F32)
    gshape = jax.ShapeDtypeStruct((1, w), F32)
    return pl.pallas_call(
        body, name=name, grid=(ng, nbt), in_specs=in_specs + [out, out],
        out_specs=[out, whole, whole, gain, gain], out_shape=[sh, sh, sh, gshape, gshape],
        compiler_params=_params(("arbitrary", "arbitrary")),
    )(qkv, qkv, qkv, qkv, qkv, qg, kg, do, dlse)


def _combine_fn(o, lse):
    m = jnp.max(lse, axis=0, keepdims=True)
    e = jnp.exp(lse - m)
    w = e / jnp.sum(e, axis=0, keepdims=True)
    return jnp.sum(w * o, axis=0)


def att_combine(o, lse, name):
    ng, t, w = o.shape
    tr = _tile(t, 256, 16)
    spec = pl.BlockSpec((ng, tr, w), lambda i: (0, i, 0))
    y_spec = pl.BlockSpec((tr, w), lambda i: (i, 0))

    def body(o_ref, l_ref, y_ref):
        y_ref[...] = _combine_fn(o_ref[...], l_ref[...]).astype(y_ref.dtype)

    return pl.pallas_call(
        body, name=name, grid=(t // tr,), in_specs=[spec, spec], out_specs=y_spec,
        out_shape=jax.ShapeDtypeStruct((t, w), CD), compiler_params=_params(("parallel",)),
    )(o, lse)


def att_combine_bwd(o, lse, dy, name):
    ng, t, w = o.shape
    tr = _tile(t, 256, 8)
    spec = pl.BlockSpec((ng, tr, w), lambda i: (0, i, 0))
    y_spec = pl.BlockSpec((tr, w), lambda i: (i, 0))

    def body(o_ref, l_ref, dy_ref, do_ref, dl_ref):
        _, vjp = jax.vjp(_combine_fn, o_ref[...], l_ref[...])
        do, dl = vjp(dy_ref[...])
        do_ref[...] = do
        dl_ref[...] = dl

    sh = jax.ShapeDtypeStruct(o.shape, F32)
    return pl.pallas_call(
        body, name=name, grid=(t // tr,), in_specs=[spec, spec, y_spec], out_specs=[spec, spec],
        out_shape=[sh, sh], compiler_params=_params(("parallel",)),
    )(o, lse, dy)


def _shift_down(x, s):
    if s == 0:
        return x
    row = lax.broadcasted_iota(jnp.int32, x.shape, 0)
    return jnp.where(row >= s, pltpu.roll(x, s, 0), 0.0)


def _shift_up(x, s):
    if s == 0:
        return x
    t = x.shape[0]
    row = lax.broadcasted_iota(jnp.int32, x.shape, 0)
    return jnp.where(row < t - s, pltpu.roll(x, t - s, 0), 0.0)


def conv_fwd(z, col0, w, name):
    t = z.shape[0]
    c = w.shape[1]
    nblk0 = col0 // LANE

    def body(z_ref, w_ref, c_ref):
        zv = z_ref[...]
        acc = None
        for i in range(CONV_WIDTH):
            term = _shift_down(zv, CONV_WIDTH - 1 - i) * w_ref[i:i + 1, :]
            acc = term if acc is None else acc + term
        c_ref[...] = acc

    return pl.pallas_call(
        body, name=name, grid=(c // LANE,),
        in_specs=[pl.BlockSpec((t, LANE), lambda j: (0, nblk0 + j)), pl.BlockSpec((CONV_WIDTH, LANE), lambda j: (0, j))],
        out_specs=pl.BlockSpec((t, LANE), lambda j: (0, j)),
        out_shape=jax.ShapeDtypeStruct((t, c), F32), compiler_params=_params(("parallel",)),
    )(z, w)


def conv_bwd(dc, z, col0, w, name):
    t = z.shape[0]
    c = w.shape[1]
    nblk0 = col0 // LANE

    def body(dc_ref, z_ref, w_ref, dz_ref, dw_ref):
        dcv, zv = dc_ref[...], z_ref[...]
        acc = None
        for i in range(CONV_WIDTH):
            s = CONV_WIDTH - 1 - i
            term = _shift_up(dcv, s) * w_ref[i:i + 1, :]
            acc = term if acc is None else acc + term
            dw_ref[i:i + 1, :] = jnp.sum(dcv * _shift_down(zv, s), axis=0, keepdims=True)
        dz_ref[...] = acc.astype(dz_ref.dtype)

    blk = pl.BlockSpec((t, LANE), lambda j: (0, j))
    wblk = pl.BlockSpec((CONV_WIDTH, LANE), lambda j: (0, j))
    return pl.pallas_call(
        body, name=name, grid=(c // LANE,),
        in_specs=[blk, pl.BlockSpec((t, LANE), lambda j: (0, nblk0 + j)), wblk], out_specs=[blk, wblk],
        out_shape=[jax.ShapeDtypeStruct((t, c), CD), jax.ShapeDtypeStruct((CONV_WIDTH, c), F32)],
        compiler_params=_params(("parallel",)),
    )(dc, z, w)


def _dn_consts():
    c = DN_CHUNK
    row = lax.broadcasted_iota(jnp.int32, (c, c), 0)
    col = lax.broadcasted_iota(jnp.int32, (c, c), 1)
    return dict(tril=row >= col, strict=row > col, eye=(row == col).astype(F32),
                tril_f=(row >= col).astype(F32), triu_f=(row <= col).astype(F32))


def _softplus(x):
    return jnp.maximum(x, 0.0) + jnp.log(1.0 + jnp.exp(-jnp.abs(x)))


def _split(x):
    hi = x.astype(CD)
    return hi, (x - hi.astype(F32)).astype(CD)


def _bdot3(a, b, ca, cb):
    ah, al = _split(a)
    bh, bl = _split(b)
    return _bdot(ah, bh, ca, cb) + (_bdot(ah, bl, ca, cb) + _bdot(al, bh, ca, cb))


def _tri_inv_impl(a_mat):
    c = a_mat.shape[-1]
    eye = (lax.broadcasted_iota(jnp.int32, (c, c), 0) == lax.broadcasted_iota(jnp.int32, (c, c), 1)).astype(F32)
    nk_ = -a_mat
    t_inv = eye + nk_
    for _ in range(c.bit_length() - 2):
        nk_ = _bdot3(nk_, nk_, 2, 1)
        t_inv = t_inv + _bdot3(t_inv, nk_, 2, 1)
    return t_inv


@jax.custom_vjp
def _tri_inv(a_mat):
    return _tri_inv_impl(a_mat)


def _tri_inv_fwd(a_mat):
    t_inv = _tri_inv_impl(a_mat)
    return t_inv, t_inv


def _tri_inv_bwd(t_inv, dt_inv):
    return (-_bdot3(_bdot3(t_inv, dt_inv, 1, 1), t_inv, 2, 2),)


_tri_inv.defvjp(_tri_inv_fwd, _tri_inv_bwd)


def _dn_chunk(cq, ck, cv, og, a_col, b_col, al, dt, gn, s_prev, *, k, inv):
    q = _silu(cq)
    q = q * lax.rsqrt(jnp.sum(q * q, axis=-1, keepdims=True) + EPS) * (DN_HEAD_DIM ** -0.5)
    kk = _silu(ck)
    kk = kk * lax.rsqrt(jnp.sum(kk * kk, axis=-1, keepdims=True) + EPS)
    v = _silu(cv)
    g = -jnp.exp(al) * _softplus(a_col + dt)
    beta = jax.nn.sigmoid(b_col)
    g_row = jnp.sum(k["eye"] * g, axis=1, keepdims=True)
    gc_col = jnp.sum(k["tril_f"] * g_row, axis=2, keepdims=True)
    gc_row = jnp.sum(k["triu_f"] * g, axis=1, keepdims=True)
    ldec = jnp.where(k["tril"], jnp.exp(jnp.where(k["tril"], gc_col - gc_row, 0.0)), 0.0)
    kb, vb = kk * beta, v * beta
    a_mat = jnp.where(k["strict"], _bnt(kb, kk) * ldec, 0.0)
    t_inv = inv(a_mat)
    egc = jnp.exp(gc_col)
    u = _bnn(t_inv, vb)
    w = _bnn(t_inv, kb * egc)
    attn = jnp.where(k["tril"], _bnt(q, kk) * ldec, 0.0)
    gc_last = jnp.sum(g, axis=1, keepdims=True)
    k_dec = kk * jnp.exp(gc_last - gc_col)
    v_new = u - _bnn(w, s_prev)
    o = _bnn(q * egc, s_prev) + _bnn(attn, v_new)
    s_new = s_prev * jnp.exp(gc_last) + _btn(k_dec, v_new)
    y = o * lax.rsqrt(jnp.mean(o * o, axis=-1, keepdims=True) + EPS) * gn * _silu(og)
    return y, s_new


def _dn_heads(ref, nh):
    hd = DN_HEAD_DIM
    return jnp.stack([ref[:, h * hd:(h + 1) * hd] for h in range(nh)])


def _dn_specs(nh, col_gate, col_ab, order):
    hd, c = DN_HEAD_DIM, DN_CHUNK
    w = nh * hd
    qs = pl.BlockSpec((c, w), lambda n: (order(n), 0))
    ks = pl.BlockSpec((c, w), lambda n: (order(n), 1))
    vs = pl.BlockSpec((c, w), lambda n: (order(n), 2))
    gs = pl.BlockSpec((c, w), lambda n: (order(n), col_gate // w))
    ab = pl.BlockSpec((c, LANE), lambda n: (order(n), col_ab // LANE))
    scal = pl.BlockSpec((nh, 1, 1), lambda n: (0, 0, 0))
    gn = pl.BlockSpec((1, hd), lambda n: (0, 0))
    st = pl.BlockSpec((1, nh, hd, hd), lambda n: (order(n), 0, 0, 0))
    return qs, ks, vs, gs, ab, scal, gn, st


def _lane_pick(x, idx):
    lane = lax.broadcasted_iota(jnp.int32, x.shape, 1)
    return jnp.sum(jnp.where(lane == idx, x, 0.0), axis=1, keepdims=True)


def dn_fwd(cv, z, col_gate, col_ab, a_log, dt_bias, gn, name):
    t = cv.shape[0]
    nh = a_log.shape[0]
    hd, c = DN_HEAD_DIM, DN_CHUNK
    n_chunks = t // c
    qs, ks, vs, gs, ab, scal, gnspec, st = _dn_specs(nh, col_gate, col_ab, lambda n: n)

    def body(q_ref, k_ref, v_ref, g_ref, ab_ref, al_ref, dt_ref, gn_ref, y_ref, st_ref, s_scr):
        @pl.when(pl.program_id(0) == 0)
        def _():
            s_scr[...] = jnp.zeros_like(s_scr)

        abv = ab_ref[...]
        a_col = jnp.stack([_lane_pick(abv, h) for h in range(nh)])
        b_col = jnp.stack([_lane_pick(abv, nh + h) for h in range(nh)])
        s_prev = s_scr[...]
        st_ref[0] = s_prev
        y, s_new = _dn_chunk(_dn_heads(q_ref, nh), _dn_heads(k_ref, nh), _dn_heads(v_ref, nh), _dn_heads(g_ref, nh),
                             a_col, b_col, al_ref[...], dt_ref[...], gn_ref[...], s_prev,
                             k=_dn_consts(), inv=_tri_inv_impl)
        for h in range(nh):
            y_ref[:, h * hd:(h + 1) * hd] = y[h].astype(y_ref.dtype)
        s_scr[...] = s_new

    return pl.pallas_call(
        body, name=name, grid=(n_chunks,),
        in_specs=[qs, ks, vs, gs, ab, scal, scal, gnspec],
        out_specs=[pl.BlockSpec((c, nh * hd), lambda n: (n, 0)), st],
        out_shape=[jax.ShapeDtypeStruct((t, nh * hd), CD), jax.ShapeDtypeStruct((n_chunks, nh, hd, hd), F32)],
        scratch_shapes=[pltpu.VMEM((nh, hd, hd), F32)],
        compiler_params=_params(("arbitrary",)),
    )(cv, cv, cv, z, z, a_log, dt_bias, gn)


def dn_bwd(cv, z, col_gate, col_ab, a_log, dt_bias, gn, states, dy, name):
    t = cv.shape[0]
    nh = a_log.shape[0]
    hd, c = DN_HEAD_DIM, DN_CHUNK
    w = nh * hd
    n_chunks = t // c
    rev = lambda n: n_chunks - 1 - n
    qs, ks, vs, gs, ab, scal, gnspec, st = _dn_specs(nh, col_gate, col_ab, rev)
    yspec = pl.BlockSpec((c, w), lambda n: (rev(n), 0))

    def body(q_ref, k_ref, v_ref, g_ref, ab_ref, al_ref, dt_ref, gn_ref, st_ref, dy_ref,
             dc_ref, dg_ref, dab_ref, dal_ref, ddt_ref, dgn_ref, ds_scr):
        @pl.when(pl.program_id(0) == 0)
        def _():
            ds_scr[...] = jnp.zeros_like(ds_scr)
            dal_ref[...] = jnp.zeros_like(dal_ref)
            ddt_ref[...] = jnp.zeros_like(ddt_ref)
            dgn_ref[...] = jnp.zeros_like(dgn_ref)

        abv = ab_ref[...]
        a_col = jnp.stack([_lane_pick(abv, h) for h in range(nh)])
        b_col = jnp.stack([_lane_pick(abv, nh + h) for h in range(nh)])
        fn = functools.partial(_dn_chunk, k=_dn_consts(), inv=_tri_inv)
        _, vjp = jax.vjp(fn, _dn_heads(q_ref, nh), _dn_heads(k_ref, nh), _dn_heads(v_ref, nh), _dn_heads(g_ref, nh),
                         a_col, b_col, al_ref[...], dt_ref[...], gn_ref[...], st_ref[0])
        dq, dk, dv, dg, da, db, dal, ddt, dgn, ds = vjp((_dn_heads(dy_ref, nh), ds_scr[...]))
        lane = lax.broadcasted_iota(jnp.int32, (c, LANE), 1)
        dab = jnp.zeros((c, LANE), F32)
        for h in range(nh):
            cols = slice(h * hd, (h + 1) * hd)
            dc_ref[:, cols] = dq[h]
            dc_ref[:, w + h * hd:w + (h + 1) * hd] = dk[h]
            dc_ref[:, 2 * w + h * hd:2 * w + (h + 1) * hd] = dv[h]
            dg_ref[:, cols] = dg[h].astype(dg_ref.dtype)
            dab = dab + jnp.where(lane == h, da[h], 0.0) + jnp.where(lane == nh + h, db[h], 0.0)
        dab_ref[...] = dab.astype(dab_ref.dtype)
        dal_ref[...] += dal
        ddt_ref[...] += ddt
        dgn_ref[...] += dgn
        ds_scr[...] = ds

    sshape = jax.ShapeDtypeStruct((nh, 1, 1), F32)
    res = pl.pallas_call(
        body, name=name, grid=(n_chunks,),
        in_specs=[qs, ks, vs, gs, ab, scal, scal, gnspec, st, yspec],
        out_specs=[pl.BlockSpec((c, 3 * w), lambda n: (rev(n), 0)), yspec,
                   pl.BlockSpec((c, LANE), lambda n: (rev(n), 0)), scal, scal, gnspec],
        out_shape=[jax.ShapeDtypeStruct((t, 3 * w), F32), jax.ShapeDtypeStruct((t, w), CD),
                   jax.ShapeDtypeStruct((t, LANE), CD), sshape, sshape, jax.ShapeDtypeStruct((1, hd), F32)],
        scratch_shapes=[pltpu.VMEM((nh, hd, hd), F32)],
        compiler_params=_params(("arbitrary",)),
    )(cv, cv, cv, z, z, a_log, dt_bias, gn, states, dy)
    return res


def merge_fwd(ya, wpa, yd, wpd, z, col_m, name):
    t, d = yd.shape[0], wpd.shape[1]
    tm, tn = _tile(t, 512, LANE), _tile(d, 256, LANE)
    nb1, nb2 = col_m // tn, (col_m + d) // tn

    def body(ya_ref, wpa_ref, yd_ref, wpd_ref, z1_ref, z2_ref, m_ref, pa_ref, pd_ref):
        pa = _nn(ya_ref[...], wpa_ref[...])
        pd = _nn(yd_ref[...], wpd_ref[...])
        pa_ref[...] = pa
        pd_ref[...] = pd
        m_ref[...] = (jax.nn.sigmoid(z1_ref[...]) * pa + jax.nn.sigmoid(z2_ref[...]) * pd).astype(m_ref.dtype)

    o = pl.BlockSpec((tm, tn), lambda i, j: (i, j))
    return pl.pallas_call(
        body, name=name, grid=(t // tm, d // tn),
        in_specs=[pl.BlockSpec((tm, ya.shape[1]), lambda i, j: (i, 0)),
                  pl.BlockSpec((wpa.shape[0], tn), lambda i, j: (0, j)),
                  pl.BlockSpec((tm, yd.shape[1]), lambda i, j: (i, 0)),
                  pl.BlockSpec((wpd.shape[0], tn), lambda i, j: (0, j)),
                  pl.BlockSpec((tm, tn), lambda i, j: (i, nb1 + j)),
                  pl.BlockSpec((tm, tn), lambda i, j: (i, nb2 + j))],
        out_specs=[o, o, o],
        out_shape=[jax.ShapeDtypeStruct((t, d), CD), jax.ShapeDtypeStruct((t, d), F32),
                   jax.ShapeDtypeStruct((t, d), F32)],
        compiler_params=_params(("parallel", "parallel")),
    )(ya, wpa, yd, wpd, z, z)


def merge_bwd(dm, pa, pd, z, col_m, name):
    t, d = dm.shape
    tm, tn = _tile(t, 512, 8), _tile(d, 256, LANE)
    nb1, nb2 = col_m // tn, (col_m + d) // tn

    def body(dm_ref, pa_ref, pd_ref, z1_ref, z2_ref, dpa_ref, dpd_ref, dz1_ref, dz2_ref):
        dmv = dm_ref[...]
        s1, s2 = jax.nn.sigmoid(z1_ref[...]), jax.nn.sigmoid(z2_ref[...])
        dpa_ref[...] = (dmv * s1).astype(dpa_ref.dtype)
        dpd_ref[...] = (dmv * s2).astype(dpd_ref.dtype)
        dz1_ref[...] = (dmv * pa_ref[...] * s1 * (1.0 - s1)).astype(dz1_ref.dtype)
        dz2_ref[...] = (dmv * pd_ref[...] * s2 * (1.0 - s2)).astype(dz2_ref.dtype)

    o = pl.BlockSpec((tm, tn), lambda i, j: (i, j))
    sh = jax.ShapeDtypeStruct((t, d), CD)
    return pl.pallas_call(
        body, name=name, grid=(t // tm, d // tn),
        in_specs=[o, o, o, pl.BlockSpec((tm, tn), lambda i, j: (i, nb1 + j)),
                  pl.BlockSpec((tm, tn), lambda i, j: (i, nb2 + j))],
        out_specs=[o, o, o, o], out_shape=[sh, sh, sh, sh],
        compiler_params=_params(("parallel", "parallel")),
    )(dm, pa, pd, z, z)


def ada_fwd(c_all, w, b, name):
    nl, d, n = w.shape
    tn = _tile(n, 384, LANE)

    def body(c_ref, w_ref, b_ref, o_ref):
        o_ref[0] = _nn(_silu(c_ref[...]), w_ref[0]) + b_ref[0]

    return pl.pallas_call(
        body, name=name, grid=(nl, n // tn),
        in_specs=[pl.BlockSpec(c_all.shape, lambda l, j: (0, 0)), pl.BlockSpec((1, d, tn), lambda l, j: (l, 0, j)),
                  pl.BlockSpec((1, 1, tn), lambda l, j: (l, 0, j))],
        out_specs=pl.BlockSpec((1, c_all.shape[0], tn), lambda l, j: (l, 0, j)),
        out_shape=jax.ShapeDtypeStruct((nl, c_all.shape[0], n), F32),
        compiler_params=_params(("parallel", "parallel")),
    )(c_all, w, b)


def ada_bwd(c_pad, dmod_pad, name):
    nl, kp, n = dmod_pad.shape
    d = c_pad.shape[1]
    tn = _tile(n, 384, LANE)

    def body(c_ref, g_ref, o_ref):
        o_ref[0] = _tn(_silu(c_ref[...]), g_ref[0])

    return pl.pallas_call(
        body, name=name, grid=(nl, n // tn),
        in_specs=[pl.BlockSpec((kp, d), lambda l, j: (0, 0)), pl.BlockSpec((1, kp, tn), lambda l, j: (l, 0, j))],
        out_specs=pl.BlockSpec((1, d, tn), lambda l, j: (l, 0, j)),
        out_shape=jax.ShapeDtypeStruct((nl, d, n), F32),
        compiler_params=_params(("parallel", "parallel")),
    )(c_pad, dmod_pad)


def loss_head(y, target, name):
    t, d = y.shape
    tr = _tile(t, 256, 8)

    def body(y_ref, t_ref, dy_ref, l_ref):
        err = y_ref[...] - t_ref[...]
        dy_ref[...] = err * (1.0 / d)
        part = jnp.sum(jnp.sum(err * err, axis=1, keepdims=True), axis=0, keepdims=True) * (0.5 / d)

        @pl.when(pl.program_id(0) == 0)
        def _():
            l_ref[...] = jnp.zeros_like(l_ref)

        l_ref[...] += part

    row = pl.BlockSpec((tr, d), lambda i: (i, 0))
    return pl.pallas_call(
        body, name=name, grid=(t // tr,), in_specs=[row, row],
        out_specs=[row, pl.BlockSpec((8, LANE), lambda i: (0, 0))],
        out_shape=[jax.ShapeDtypeStruct((t, d), F32), jax.ShapeDtypeStruct((8, LANE), F32)],
        compiler_params=_params(("arbitrary",)),
    )(y, target)


def _adamw_update(g, w_ref, m_ref, v_ref, go_ref, d_ref, mo_ref, vo_ref):
    m_new = ADAM_B1 * m_ref[...] + (1.0 - ADAM_B1) * g
    v_new = ADAM_B2 * v_ref[...] + (1.0 - ADAM_B2) * jnp.square(g)
    m_hat = m_new / (1.0 - ADAM_B1 ** ADAM_STEP)
    v_hat = v_new / (1.0 - ADAM_B2 ** ADAM_STEP)
    go_ref[...] = g
    d_ref[...] = -ADAM_LR * (m_hat / (jnp.sqrt(v_hat) + ADAM_EPS) + ADAM_WD * w_ref[...])
    mo_ref[...] = m_new
    vo_ref[...] = v_new


def adamw(w, m, v, g_slots, name):
    shape = w.shape
    nslot = g_slots.shape[0]
    if w.ndim == 2:
        w3, m3, v3, g4 = w[None], m[None], v[None], g_slots[:, None]
    else:
        w3, m3, v3, g4 = w, m, v, g_slots
    nl, r, c = w3.shape
    tr = _tile(r, 256, 16)

    def body(w_ref, m_ref, v_ref, g_ref, *outs):
        g = g_ref[0].astype(F32)
        for s in range(1, nslot):
            g = g + g_ref[s].astype(F32)
        _adamw_update(g, w_ref, m_ref, v_ref, *outs)

    blk = pl.BlockSpec((1, tr, c), lambda l, i: (l, i, 0))
    gblk = pl.BlockSpec((nslot, 1, tr, c), lambda l, i: (0, l, i, 0))
    sh = jax.ShapeDtypeStruct(w3.shape, F32)
    outs = pl.pallas_call(
        body, name=name, grid=(nl, r // tr), in_specs=[blk, blk, blk, gblk], out_specs=[blk] * 4,
        out_shape=[sh] * 4, compiler_params=_params(("parallel", "parallel")),
    )(w3, m3, v3, g4)
    return tuple(o.reshape(shape) for o in outs)


def adamw_layers(w, m, v, g_layers, first, name, prev=None):
    _, r, c = w.shape
    n = len(g_layers)
    nslot, _, cg = g_layers[0].shape
    tr = _tile(r, 256, 16)

    def body(w_ref, m_ref, v_ref, *rest):
        g_refs, outs = rest[:n], rest[-4:]
        for li in range(n):
            @pl.when(pl.program_id(0) == li)
            def _(g_ref=g_refs[li]):
                g = g_ref[0, :, :c].astype(F32)
                for s in range(1, nslot):
                    g = g + g_ref[s, :, :c].astype(F32)
                _adamw_update(g[None], w_ref, m_ref, v_ref, *outs)

    blk = pl.BlockSpec((1, tr, c), lambda l, i: (first + l, i, 0))
    gblks = [pl.BlockSpec((nslot, tr, cg), lambda l, i, li=li: (0, jnp.where(l == li, i, 0), 0)) for li in range(n)]
    sh = jax.ShapeDtypeStruct(w.shape, F32)
    extra = [] if prev is None else list(prev)
    return tuple(pl.pallas_call(
        body, name=name, grid=(n, r // tr),
        in_specs=[blk, blk, blk] + gblks + [pl.BlockSpec(memory_space=pl.ANY)] * len(extra),
        out_specs=[blk] * 4, out_shape=[sh] * 4,
        input_output_aliases={3 + n + k: k for k in range(len(extra))},
        compiler_params=_params(("arbitrary", "arbitrary")),
    )(w, m, v, *g_layers, *extra))


def _rows(flat, unit=16):
    n = flat.shape[0]
    per = 1024 * unit
    pad = (-n) % per
    if pad:
        flat = jnp.concatenate([flat, jnp.zeros((pad,), flat.dtype)])
    return flat.reshape(-1, 1024)


def _to_classes(a):
    t, w = a.shape[1], a.shape[2]
    return jnp.stack([a[gi].reshape(t // dil, dil, w).transpose(1, 0, 2).reshape(t, w)
                      for gi, dil in enumerate(DILATIONS)])


def _from_classes(a):
    t, w = a.shape[1], a.shape[2]
    return jnp.stack([a[gi].reshape(dil, t // dil, w).transpose(1, 0, 2).reshape(t, w)
                      for gi, dil in enumerate(DILATIONS)])


def _unshard_cols(g):
    return g.transpose(1, 0, 2).reshape(g.shape[1], -1)


def _shard_cols(full):
    r = full.shape[0]
    return full.reshape(r, N_DEV, -1).transpose(1, 0, 2)


def kernel(x, c, ada_w, ada_b, norm_ff1, ffn1_w_up, ffn1_w_down, norm_mix, w_in, q_norm, k_norm, conv_w, a_log, dt_bias, dn_norm, w_proj_att, w_proj_dn, w_out, norm_ff2, ffn2_w_up, ffn2_w_down, loss_target, m_ada_w, m_ada_b, m_norm_ff1, m_ffn1_w_up, m_ffn1_w_down, m_norm_mix, m_w_in, m_q_norm, m_k_norm, m_conv_w, m_a_log, m_dt_bias, m_dn_norm, m_w_proj_att, m_w_proj_dn, m_w_out, m_norm_ff2, m_ffn2_w_up, m_ffn2_w_down, v_ada_w, v_ada_b, v_norm_ff1, v_ffn1_w_up, v_ffn1_w_down, v_norm_mix, v_w_in, v_q_norm, v_k_norm, v_conv_w, v_a_log, v_dt_bias, v_dn_norm, v_w_proj_att, v_w_proj_dn, v_w_out, v_norm_ff2, v_ffn2_w_up, v_ffn2_w_down):
    nl = ada_w.shape[0]
    t, d = x.shape[1], x.shape[2]
    dff = ffn1_w_down.shape[1] * N_DEV
    ha = d // 256
    ng = len(DILATIONS)
    wa = ha * ATT_HEAD_DIM
    att_w = ng * wa
    nh = d // DN_HEAD_DIM
    dn_w = nh * DN_HEAD_DIM
    n_in = w_in.shape[2] * N_DEV
    off_dn, off_gate = 3 * att_w, 3 * att_w + 3 * dn_w
    off_a = off_gate + dn_w
    off_merge = off_a + 2 * nh
    assert off_merge + 2 * d == n_in
    col_dn, col_gate, col_m = 0, 3 * dn_w, 4 * dn_w
    col_att = col_m + 2 * d
    col_ab = col_att + 3 * att_w
    zw = col_ab + 2 * LANE
    me = 4 * lax.axis_index("x") + 2 * lax.axis_index("y") + lax.axis_index("c")
    xs = x[0]
    target = loss_target[0]

    conv_rows = _rows(conv_w.reshape(-1), 8)
    pack0 = jnp.concatenate([jnp.concatenate([c, jnp.zeros((7, d), F32)]).reshape(-1), conv_rows.reshape(-1)])
    pack0 = _rows(pack0, 8)
    g0 = all_gather([pack0], "ag_c_conv")[0].reshape(N_DEV, -1)
    c_all = g0[:, :d]
    cw = g0[:, 8 * d:8 * d + conv_w.size].reshape(N_DEV, nl, CONV_WIDTH, -1)
    conv_full = cw.transpose(1, 2, 0, 3).reshape(nl, CONV_WIDTH, 3 * dn_w)

    n_ada = ada_w.shape[2]
    b_mine = lax.dynamic_slice_in_dim(ada_b, me * n_ada, n_ada, axis=1)[:, None, :]
    mod_s = ada_fwd(c_all, ada_w, b_mine, "ada_fwd")
    gm = all_gather([mod_s], "ag_mod")[0]
    mod = lax.dynamic_index_in_dim(gm, me, axis=2, keepdims=False)
    mod = mod.transpose(1, 0, 2).reshape(nl, N_ADA, 1, d)

    kinds = [ffn1_w_up, ffn1_w_down, w_in, w_proj_att, w_proj_dn, w_out, ffn2_w_up, ffn2_w_down]
    c_up = ffn1_w_up.shape[2]
    cp = -(-c_up // LANE) * LANE
    r_dn = ffn1_w_down.shape[1]
    assert 2 * r_dn == c_up

    up1_t, up2_t = jnp.swapaxes(ffn1_w_up, 1, 2), jnp.swapaxes(ffn2_w_up, 1, 2)
    w_in_t = jnp.transpose(w_in, (2, 0, 1))

    def layer_shards(l):
        def pad_up(wt):
            return jnp.pad(wt.astype(CD), ((0, cp - c_up), (0, 0)))
        return [pad_up(up1_t[l]), ffn1_w_down[l].astype(CD), w_in_t[:, l].astype(CD), w_proj_att[l].astype(CD),
                w_proj_dn[l].astype(CD), w_out[l].astype(CD), pad_up(up2_t[l]), ffn2_w_down[l].astype(CD)]

    def gather_landing(shards):
        return [lax.dynamic_update_index_in_dim(lax.empty((N_DEV,) + s.shape, s.dtype), s, me, 0) for s in shards]

    def slot_landing(arrs):
        return [lax.dynamic_update_index_in_dim(lax.empty(a.shape, a.dtype),
                                                lax.dynamic_index_in_dim(a, me, 0, keepdims=False), me, 0) for a in arrs]

    gathered = all_gather(layer_shards(0), "ag_weights0")
    prefetch = None

    def full_weight(ki, l):
        blk = gathered[ki]
        if ki == 3:
            return _unshard_cols(blk)
        if ki == 2:
            return w_in_rows(blk.reshape(n_in, d))
        if ki in (1, 7):
            pairs = blk.reshape(HALF, c_up, d)
            return jnp.pad(pairs, ((0, 0), (0, cp - c_up), (0, 0))).reshape(HALF * cp, d)
        return blk.reshape(-1, blk.shape[2])

    def down_grad_slots(g):
        return g.reshape(HALF, cp, d)[:, :c_up].reshape(N_DEV, r_dn, d)

    def w_in_rows(wt):
        pad = jnp.zeros((zw - n_in, wt.shape[1]), wt.dtype)
        return jnp.concatenate([wt[off_dn:off_a], wt[off_merge:], wt[:off_dn], wt[off_a:off_merge], pad], axis=0)

    def w_in_cols_inv(g):
        return jnp.concatenate([g[:, col_att:col_ab], g[:, :col_m], g[:, col_ab:col_ab + 2 * nh], g[:, col_m:col_att]], axis=1)

    saved = []
    xc = xs
    mods = []
    for l in range(nl):
        sv = {}
        if prefetch is not None:
            gathered = exchange_wait(*prefetch[:4], xc, True, f"ag_wait{l}")
        mod_l = mod[l]
        if l + 1 < nl:
            shards = layer_shards(l + 1)
            behind = gathered[0][0, :1, :1].astype(F32) + mod[0, 0, :, :1]
            prefetch = exchange_start(shards, gather_landing(shards), behind, True, f"ag_start{l + 1}")
            mod_l = mod_l + prefetch[4][0, 0]
        mods.append(mod_l)
        sh1, sc1, gt1, sh2, sc2, gt2, sh3, sc3, gt3 = [mod_l[i] for i in range(N_ADA)]
        w_dn1, w_dn2 = full_weight(1, l), full_weight(7, l)
        win = full_weight(2, l)
        wpa, wpd, wo = full_weight(3, l), full_weight(4, l), full_weight(5, l)
        sv["w"] = (gathered[0], gathered[6], w_dn1, w_dn2, win, wpa, wpd, wo)

        def ffn(xin, g, sh, sc, gt, u_all, w_dn):
            h = norm_mod(xin, g, sc, sh, "norm_mod")
            gate, up, a = ffn_up(h, u_all, "ffn_up")
            xo, f = matmul([(a, w_dn)], "nn", F32, "ffn_down", out_scale=0.5 * gt, resid=xin, save_acc=True,
                           tm=1024, tk=4096)
            return xo, (xin, h, gate, up, a, f)

        xc, sv["ffn1"] = ffn(xc, norm_ff1[l:l + 1], sh1, sc1, gt1, gathered[0], w_dn1)

        x_mix = xc
        h2 = norm_mod(x_mix, norm_mix[l:l + 1], sc2, sh2, "norm_mod")
        z = matmul([(h2, win)], "nt", F32, "w_in", tm=1024, tn=512)
        z_att = z[:, col_att:col_ab].reshape(t, 3, ng, wa).transpose(1, 2, 0, 3)
        qkv = jnp.stack([_to_classes(z_att[i]) for i in range(3)])
        qg, kg = jnp.tile(q_norm[l:l + 1], (1, ha)), jnp.tile(k_norm[l:l + 1], (1, ha))
        o_cls, lse_cls = att_fwd(qkv, qg, kg, "att_fwd")
        o_tok, lse_tok = _from_classes(o_cls), _from_classes(lse_cls)
        ya = att_combine(o_tok, lse_tok, "att_combine")
        cvo = conv_fwd(z, col_dn, conv_full[l], "conv_fwd")
        al3, dt3 = a_log[l].reshape(nh, 1, 1), dt_bias[l].reshape(nh, 1, 1)
        gn = dn_norm[l:l + 1]
        yd, states = dn_fwd(cvo, z, col_gate, col_ab, al3, dt3, gn, "dn_fwd")
        mrg, pa, pd = merge_fwd(ya, wpa, yd, wpd, z, col_m, "merge_fwd")
        xc, f2 = matmul([(mrg, wo)], "nn", F32, "w_out", out_scale=gt2, resid=x_mix, save_acc=True)
        sv["mix"] = (x_mix, h2, z, qkv, o_tok, lse_tok, ya, cvo, yd, states, mrg, pa, pd, f2)

        xc, sv["ffn2"] = ffn(xc, norm_ff2[l:l + 1], sh3, sc3, gt3, gathered[6], w_dn2)
        saved.append(sv)

    dxc, loss_blk = loss_head(xc, target, "loss_head")

    def slots(ki, g):
        if ki in (0, 1, 6, 7):
            return g
        return _shard_cols(g) if ki in (2, 3) else g.reshape(N_DEV, -1, g.shape[1])

    gbig = [None] * len(kinds)
    dmods, small, pending = [], [], []
    token = None
    for l in reversed(range(nl)):
        sv = saved[l]
        mod_l = mods[l] if token is None else mods[l] + token[0, 0]
        sh1, sc1, gt1, sh2, sc2, gt2, sh3, sc3, gt3 = [mod_l[i] for i in range(N_ADA)]
        u_all1, u_all2, w_dn1, w_dn2, win, wpa, wpd, wo = sv["w"]

        def ffn_bwd(dxo, g, sh, sc, gt, u_all, w_dn, sv_f):
            xin, h, gate, up, a, f = sv_f
            s = 0.5 * gt
            g_dn = down_grad_slots(matmul([(a, dxo)], "tn", CD, "ffn_down_wg", out_scale=s, tm=1024, tn=1024))
            dgate, dup = ffn_dact(dxo, s, w_dn, gate, up, "ffn_dact")
            g_up = ffn_up_wg(h, dgate, dup, cp, "ffn_up_wg")
            dh = ffn_up_dg(dgate, dup, u_all, "ffn_up_dg")
            dx, dg, dsc, dsh, dgt = norm_mod_bwd(xin, g, sc, sh, dh, dxo, f, 0.5, "norm_mod_bwd")
            return dx, g_up, g_dn, (dg, dsc, dsh, dgt)

        def start_exchange(ids, tag, behind):
            send = [slots(ki, gbig[ki]) for ki in ids]
            started = exchange_start(send, slot_landing(send), behind, False, f"a2a_start{l}{tag}")
            pending.append((l, ids, tag, started))
            return started[4]

        dxc, gbig[6], gbig[7], (dg3, dsc3, dsh3, dgt3) = ffn_bwd(
            dxc, norm_ff2[l:l + 1], sh3, sc3, gt3, u_all2, w_dn2, sv["ffn2"])
        token = start_exchange([6, 7], "f2", dxc)
        sh2, sc2, gt2 = (vec + token[0, 0] for vec in (sh2, sc2, gt2))

        x_mix, h2, z, qkv, o_tok, lse_tok, ya, cvo, yd, states, mrg, pa, pd, f2 = sv["mix"]
        gbig[5] = matmul([(mrg, dxc)], "tn", CD, "w_out_wg", out_scale=gt2)
        dm = matmul([(dxc, wo)], "nt", F32, "w_out_dg", a_scale=gt2)
        dpa, dpd, dz1, dz2 = merge_bwd(dm, pa, pd, z, col_m, "merge_bwd")
        gbig[3] = matmul([(ya, dpa)], "tn", CD, "patt_wg")
        gbig[4] = matmul([(yd, dpd)], "tn", CD, "pdn_wg")
        dya = matmul([(dpa, wpa)], "nt", F32, "patt_dg")
        dyd = matmul([(dpd, wpd)], "nt", F32, "pdn_dg")
        do_tok, dlse_tok = att_combine_bwd(o_tok, lse_tok, dya, "att_combine_bwd")
        qg, kg = jnp.tile(q_norm[l:l + 1], (1, ha)), jnp.tile(k_norm[l:l + 1], (1, ha))
        dq, dk, dv, dqg, dkg = att_bwd(qkv, qg, kg, _to_classes(do_tok), _to_classes(dlse_tok), "att_bwd")
        dqg, dkg = (jnp.sum(v_.reshape(ha, ATT_HEAD_DIM), axis=0, keepdims=True) for v_ in (dqg, dkg))
        dz_att = jnp.stack([_from_classes(a_) for a_ in (dq, dk, dv)])
        dz_att = dz_att.transpose(2, 0, 1, 3).reshape(t, 3 * att_w).astype(CD)
        al3, dt3 = a_log[l].reshape(nh, 1, 1), dt_bias[l].reshape(nh, 1, 1)
        gn = dn_norm[l:l + 1]
        dcvo, dz_gate, dz_ab, dal, ddt, dgn = dn_bwd(
            cvo, z, col_gate, col_ab, al3, dt3, gn, states, dyd, "dn_bwd")
        dz_dn, dconv = conv_bwd(dcvo, z, col_dn, conv_full[l], "conv_bwd")
        dz = jnp.concatenate([dz_dn, dz_gate, dz1, dz2, dz_att, dz_ab, jnp.zeros((t, LANE), CD)], axis=1)
        gbig[2] = w_in_cols_inv(matmul([(h2, dz)], "tn", CD, "w_in_wg", tm=1024, tn=1024))
        dh2 = matmul([(dz, win)], "nn", F32, "w_in_dg", tm=1024, tn=1024,
                     tk=zw // 4 if (zw // 4) % LANE == 0 else zw)
        dxc, dg2, dsc2, dsh2, dgt2 = norm_mod_bwd(x_mix, norm_mix[l:l + 1], sc2, sh2, dh2, dxc, f2, 1.0, "norm_mod_bwd_mix")

        token = start_exchange([2, 3, 4, 5], "mx", dxc)
        sh1, sc1, gt1 = (vec + token[0, 0] for vec in (sh1, sc1, gt1))
        dxc, gbig[0], gbig[1], (dg1, dsc1, dsh1, dgt1) = ffn_bwd(
            dxc, norm_ff1[l:l + 1], sh1, sc1, gt1, u_all1, w_dn1, sv["ffn1"])
        if l > 0:
            token = start_exchange([0, 1], "f1", dxc)
        dmods.append(jnp.concatenate([dsh1, dsc1, dgt1, dsh2, dsc2, dgt2, dsh3, dsc3, dgt3], axis=1))
        small.append((dg1, dg2, dg3, dconv, dqg, dkg, dal.reshape(1, nh), ddt.reshape(1, nh), dgn))
    dmods.reverse()
    small.reverse()

    big_names = ["ffn1_w_up", "ffn1_w_down", "w_in", "w_proj_att", "w_proj_dn", "w_out", "ffn2_w_up", "ffn2_w_down"]
    big_m = [m_ffn1_w_up, m_ffn1_w_down, m_w_in, m_w_proj_att, m_w_proj_dn, m_w_out, m_ffn2_w_up, m_ffn2_w_down]
    big_v = [v_ffn1_w_up, v_ffn1_w_down, v_w_in, v_w_proj_att, v_w_proj_dn, v_w_out, v_ffn2_w_up, v_ffn2_w_down]
    recv_layers = [[None] * len(kinds) for _ in range(nl)]

    def wait_exchanges(layers, behind):
        for pl_, ids, tag, st in pending:
            if pl_ in layers:
                got = exchange_wait(*st[:4], behind, False, f"a2a_wait{pl_}{tag}")
                for ki, arr in zip(ids, got):
                    recv_layers[pl_][ki] = arr

    def big_adamw(ki, layers, prev, tag):
        g_layers = [recv_layers[li][ki] for li in layers]
        wmv = (kinds[ki], big_m[ki], big_v[ki])
        if ki in (0, 6):
            wmv = tuple(jnp.swapaxes(a, 1, 2) for a in wmv)
        return adamw_layers(*wmv, g_layers, layers[0], f"adamw_{big_names[ki]}{tag}", prev)

    later = list(range(1, nl))
    partial = [None] * len(kinds)
    if later:
        wait_exchanges(later, dxc)
        partial = [big_adamw(ki, later, None, "_l1") for ki in range(len(kinds))]

    fields = [jnp.stack(dmods).reshape(-1)]
    fields += [jnp.stack([s[i] for s in small]).reshape(-1) for i in range(9)]
    fields.append(loss_blk[0, :1])
    if later:
        fields.append(partial[-1][1][1, :1, 0] * 0.0)
    fsizes = [f.size for f in fields]
    foffs = [sum(fsizes[:i]) for i in range(len(fields))]
    g1 = all_gather([_rows(jnp.concatenate(fields), 8)], "ag_small")[0].reshape(N_DEV, -1)
    l = 0
    start_exchange([0, 1], "f1", g1)

    def field(i, shape):
        return g1[:, foffs[i]:foffs[i] + fsizes[i]].reshape(N_DEV, *shape)

    loss = field(10, (1,))[0, 0]
    for j in range(1, N_DEV):
        loss = loss + field(10, (1,))[j, 0]

    results = {}
    dmod_all = field(0, (nl, N_ADA * d))
    c_pad = jnp.concatenate([c_all, jnp.zeros((8, d), F32)])
    dmod_mine = lax.dynamic_slice_in_dim(dmod_all, me * n_ada, n_ada, axis=2).transpose(1, 0, 2)
    dmod_pad = jnp.concatenate([dmod_mine, jnp.zeros((nl, 8, n_ada), F32)], axis=1)
    g_ada_w = ada_bwd(c_pad, dmod_pad, "ada_bwd")
    results["ada_w"] = adamw(ada_w, m_ada_w, v_ada_w, g_ada_w[None], "adamw_ada_w")
    results["ada_b"] = adamw(ada_b, m_ada_b, v_ada_b, dmod_all, "adamw_ada_b")
    results["norm_ff1"] = adamw(norm_ff1, m_norm_ff1, v_norm_ff1, field(1, (nl, d)), "adamw_norm_ff1")
    results["norm_mix"] = adamw(norm_mix, m_norm_mix, v_norm_mix, field(2, (nl, d)), "adamw_norm_mix")
    results["norm_ff2"] = adamw(norm_ff2, m_norm_ff2, v_norm_ff2, field(3, (nl, d)), "adamw_norm_ff2")
    conv_slots = lax.dynamic_slice_in_dim(field(4, (nl, CONV_WIDTH, 3 * dn_w)), me * conv_w.shape[2],
                                          conv_w.shape[2], axis=3)
    results["conv_w"] = adamw(conv_w, m_conv_w, v_conv_w, conv_slots, "adamw_conv_w")
    results["q_norm"] = adamw(q_norm, m_q_norm, v_q_norm, field(5, (nl, ATT_HEAD_DIM)), "adamw_q_norm")
    results["k_norm"] = adamw(k_norm, m_k_norm, v_k_norm, field(6, (nl, ATT_HEAD_DIM)), "adamw_k_norm")
    results["a_log"] = adamw(a_log, m_a_log, v_a_log, field(7, (nl, nh)), "adamw_a_log")
    results["dt_bias"] = adamw(dt_bias, m_dt_bias, v_dt_bias, field(8, (nl, nh)), "adamw_dt_bias")
    results["dn_norm"] = adamw(dn_norm, m_dn_norm, v_dn_norm, field(9, (nl, DN_HEAD_DIM)), "adamw_dn_norm")
    wait_exchanges([0], results["ada_w"][1])
    for ki, nm in enumerate(big_names):
        res = big_adamw(ki, [0], partial[ki], "_l0")
        results[nm] = tuple(jnp.swapaxes(r, 1, 2) for r in res) if ki in (0, 6) else res

    order = ["ada_w", "ada_b", "norm_ff1", "ffn1_w_up", "ffn1_w_down", "norm_mix", "w_in", "q_norm", "k_norm",
             "conv_w", "a_log", "dt_bias", "dn_norm", "w_proj_att", "w_proj_dn", "w_out", "norm_ff2",
             "ffn2_w_up", "ffn2_w_down"]
    outs = [loss, dxc[None]]
    for part in range(4):
        outs += [results[n][part] for n in order]
    return tuple(outs)
```

```python
import functools

import jax
import jax.numpy as jnp
from jax import lax
from jax.experimental import pallas as pl
from jax.experimental.pallas import tpu as pltpu

F32 = jnp.float32
CD = jnp.bfloat16
EPS = 1e-6
N_DEV = 8
LANE = 128
ATT_HEAD_DIM = 64
ATT_BLOCK = 128
DILATIONS = (1, 4, 16)
DN_HEAD_DIM = 128
DN_CHUNK = 64
CONV_WIDTH = 4
N_ADA = 9
ADAM_LR, ADAM_B1, ADAM_B2, ADAM_EPS, ADAM_WD, ADAM_STEP = 0.001, 0.9, 0.999, 1e-08, 0.01, 10
VMEM_LIMIT = 56 * 1024 * 1024
NEG = -1e30
MESH = pl.DeviceIdType.MESH
HI = lax.Precision.HIGHEST


def _params(sem=None):
    return pltpu.CompilerParams(dimension_semantics=sem, vmem_limit_bytes=VMEM_LIMIT)


def _tile(n, pref, unit):
    best = None
    t = unit
    while t <= min(n, pref):
        if n % t == 0:
            best = t
        t += unit
    return best if best is not None else n


def _silu(x):
    return x * jax.nn.sigmoid(x)


def _dot(a, b, dims, precision=None):
    if precision is None:
        a, b = a.astype(CD), b.astype(CD)
    return lax.dot_general(a, b, (dims, ((), ())), precision=precision, preferred_element_type=F32)


def _nn(a, b, precision=None):
    return _dot(a, b, ((1,), (0,)), precision)


def _nt(a, b, precision=None):
    return _dot(a, b, ((1,), (1,)), precision)


def _tn(a, b, precision=None):
    return _dot(a, b, ((0,), (0,)), precision)


def all_gather(arrs, name):
    n = len(arrs)

    def body(*refs):
        x_refs, out_refs = refs[:n], refs[n:2 * n]
        send_sems, recv_sems, local_sems = refs[2 * n:]
        x, y, c = lax.axis_index("x"), lax.axis_index("y"), lax.axis_index("c")
        me, sibling = (x, y, c), (x, y, 1 - c)
        chips = [(1 - x, y), (x, 1 - y), (1 - x, 1 - y)]

        def copy(a, k, block, to, src=None):
            slot = out_refs[a].at[4 * block[0] + 2 * block[1] + block[2]]
            return pltpu.make_async_remote_copy(
                src_ref=slot if src is None else src, dst_ref=slot,
                send_sem=send_sems.at[7 * a + k], recv_sem=recv_sems.at[7 * a + k],
                device_id=to, device_id_type=MESH)

        mine, first, passed = [], [], []
        for a in range(n):
            cp = pltpu.make_async_copy(x_refs[a], out_refs[a].at[4 * x + 2 * y + c], local_sems.at[a])
            cp.start()
            mine.append(cp)
            first.append(copy(a, 0, me, sibling, src=x_refs[a]))
            first += [copy(a, 1 + j, me, (*chip, c), src=x_refs[a]) for j, chip in enumerate(chips)]
        for cp in first:
            cp.start()
        for j, chip in enumerate(chips):
            for a in range(n):
                copy(a, 1 + j, (*chip, c), me).wait_recv()
                cp = copy(a, 4 + j, (*chip, c), sibling)
                cp.start()
                passed.append(cp)
        for a in range(n):
            copy(a, 0, sibling, me).wait_recv()
            for j, chip in enumerate(chips):
                copy(a, 4 + j, (*chip, 1 - c), me).wait_recv()
        for cp in first + passed:
            cp.wait_send()
        for cp in mine:
            cp.wait()

    hbm = pl.BlockSpec(memory_space=pl.ANY)
    outs = pl.pallas_call(
        body, name=name,
        out_shape=[jax.ShapeDtypeStruct((N_DEV,) + a.shape, a.dtype) for a in arrs],
        in_specs=[hbm] * n, out_specs=[hbm] * n,
        scratch_shapes=[pltpu.SemaphoreType.DMA((7 * n,)), pltpu.SemaphoreType.DMA((7 * n,)),
                        pltpu.SemaphoreType.DMA((n,))],
    )(*arrs)
    return list(outs)


_HBM = pl.BlockSpec(memory_space=pltpu.HBM)
_SEM = pl.BlockSpec(memory_space=pltpu.SEMAPHORE)
_EFFECT = pltpu.SideEffectType.DATAFLOW_SIDE_EFFECTING


def _exchange_copies(src_refs, land_refs, send_sem, recv_sem, gather):
    x, y, c = lax.axis_index("x"), lax.axis_index("y"), lax.axis_index("c")
    me = 4 * x + 2 * y + c
    pairs = []
    for k in range(1, N_DEV):
        px, py, pc = x ^ (k >> 2), y ^ ((k >> 1) & 1), c ^ (k & 1)
        pidx = 4 * px + 2 * py + pc
        for src, land in zip(src_refs, land_refs):
            mine = src if gather else src.at[pidx]
            out = pltpu.make_async_remote_copy(src_ref=mine, dst_ref=land.at[me], send_sem=send_sem, recv_sem=recv_sem,
                                               device_id=(px, py, pc), device_id_type=MESH)
            inc = pltpu.make_async_remote_copy(src_ref=mine, dst_ref=land.at[pidx], send_sem=send_sem, recv_sem=recv_sem,
                                               device_id=(px, py, pc), device_id_type=MESH)
            pairs.append((out, inc))
    return pairs


def exchange_start(srcs, lands, after, gather, name):
    n = len(srcs)

    def body(*refs):
        src_refs, land_refs = refs[:n], refs[n:2 * n]
        send_sem, recv_sem = refs[2 * n + 1], refs[2 * n + 2]
        token = refs[-1]
        for out, _ in _exchange_copies(src_refs, land_refs, send_sem, recv_sem, gather):
            out.start()
        token[...] = jnp.zeros_like(token)

    res = pl.pallas_call(
        body, name=name,
        out_shape=(pltpu.SemaphoreType.DMA(()), pltpu.SemaphoreType.DMA(()),
                   *[pltpu.HBM(a.shape, a.dtype) for a in srcs], *[pltpu.HBM(a.shape, a.dtype) for a in lands],
                   jax.ShapeDtypeStruct((8, LANE), F32)),
        in_specs=[_HBM] * (2 * n) + [pl.BlockSpec(memory_space=pl.ANY)],
        out_specs=(_SEM, _SEM, *[_HBM] * (2 * n), pl.BlockSpec(memory_space=pltpu.VMEM)),
        input_output_aliases={i: 2 + i for i in range(2 * n)},
        compiler_params=pltpu.CompilerParams(has_side_effects=_EFFECT),
    )(*[pltpu.with_memory_space_constraint(a, pltpu.HBM) for a in list(srcs) + list(lands)], after)
    return res[0], res[1], list(res[2:2 + n]), list(res[2 + n:2 + 2 * n]), res[-1]


def exchange_wait(send_sem, recv_sem, srcs, lands, after, gather, name):
    n = len(srcs)

    def body(*refs):
        src_refs, land_refs = refs[:n], refs[n:2 * n]
        s_sem, r_sem = refs[2 * n], refs[2 * n + 1]
        for out, inc in _exchange_copies(src_refs, land_refs, s_sem, r_sem, gather):
            out.wait_send()
            inc.wait_recv()

    res = pl.pallas_call(
        body, name=name,
        out_shape=[pltpu.HBM(a.shape, a.dtype) for a in list(srcs) + list(lands)],
        in_specs=[_HBM] * (2 * n) + [_SEM, _SEM, pl.BlockSpec(memory_space=pl.ANY)],
        out_specs=[_HBM] * (2 * n),
        input_output_aliases={i: i for i in range(2 * n)},
        compiler_params=pltpu.CompilerParams(has_side_effects=_EFFECT),
    )(*srcs, *lands, send_sem, recv_sem, after)
    return list(res[n:])


def matmul(pairs, mode, out_dtype, name, *, a_scale=None, out_scale=None, resid=None,
           save_acc=False, tm=512, tn=512, tk=2048):
    a0, b0 = pairs[0]
    if mode == "nn":
        (m, kdim), n = a0.shape, b0.shape[1]
    elif mode == "nt":
        (m, kdim), n = a0.shape, b0.shape[0]
    else:
        (kdim, m), n = a0.shape, b0.shape[1]
    tm, tn = _tile(m, tm, 8 if m % LANE else LANE), _tile(n, tn, LANE)
    tk = _tile(kdim, tk, LANE)
    nk = kdim // tk
    npairs = len(pairs)
    dims = {"nn": ((1,), (0,)), "nt": ((1,), (1,)), "tn": ((0,), (0,))}[mode]

    if mode == "nn":
        a_spec = pl.BlockSpec((tm, tk), lambda i, j, k: (i, k))
        b_spec = pl.BlockSpec((tk, tn), lambda i, j, k: (k, j))
    elif mode == "nt":
        a_spec = pl.BlockSpec((tm, tk), lambda i, j, k: (i, k))
        b_spec = pl.BlockSpec((tn, tk), lambda i, j, k: (j, k))
    else:
        a_spec = pl.BlockSpec((tk, tm), lambda i, j, k: (k, i))
        b_spec = pl.BlockSpec((tk, tn), lambda i, j, k: (k, j))

    def body(*refs):
        it = iter(refs)
        pair_refs = [(next(it), next(it)) for _ in range(npairs)]
        as_ref = next(it) if a_scale is not None else None
        os_ref = next(it) if out_scale is not None else None
        rs_ref = next(it) if resid is not None else None
        o_ref = next(it)
        acc_out = next(it) if save_acc else None
        acc_ref = next(it) if nk > 1 else None
        part = None
        for a_ref, b_ref in pair_refs:
            a = a_ref[...]
            if as_ref is not None:
                a = a.astype(F32) * (as_ref[...] if mode != "tn" else as_ref[...].reshape(tk, 1))
            d = _dot(a, b_ref[...], dims)
            part = d if part is None else part + d

        def finish(acc):
            if acc_out is not None:
                acc_out[...] = acc
            if os_ref is not None:
                acc = acc * os_ref[...]
            if rs_ref is not None:
                acc = rs_ref[...] + acc
            o_ref[...] = acc.astype(o_ref.dtype)

        if nk == 1:
            finish(part)
        else:
            k = pl.program_id(2)

            @pl.when(k == 0)
            def _():
                acc_ref[...] = part

            @pl.when(k > 0)
            def _():
                acc_ref[...] += part

            @pl.when(k == nk - 1)
            def _():
                finish(acc_ref[...])

    in_specs, args = [], []
    for a, b in pairs:
        in_specs += [a_spec, b_spec]
        args += [a, b]
    if a_scale is not None:
        assert mode != "tn"
        in_specs.append(pl.BlockSpec((1, tk), lambda i, j, k: (0, k)))
        args.append(a_scale)
    if out_scale is not None:
        in_specs.append(pl.BlockSpec((1, tn), lambda i, j, k: (0, j)))
        args.append(out_scale)
    if resid is not None:
        in_specs.append(pl.BlockSpec((tm, tn), lambda i, j, k: (i, j)))
        args.append(resid)
    o_spec = pl.BlockSpec((tm, tn), lambda i, j, k: (i, j))
    out_shape = [jax.ShapeDtypeStruct((m, n), out_dtype)]
    out_specs = [o_spec]
    if save_acc:
        out_shape.append(jax.ShapeDtypeStruct((m, n), F32))
        out_specs.append(o_spec)
    scratch = [pltpu.VMEM((tm, tn), F32)] if nk > 1 else []
    res = pl.pallas_call(
        body, name=name, grid=(m // tm, n // tn, nk),
        in_specs=in_specs, out_specs=out_specs, out_shape=out_shape, scratch_shapes=scratch,
        compiler_params=_params(("parallel", "parallel", "arbitrary")),
    )(*args)
    return res if save_acc else res[0]


def _nm_fn(x, g, sc, sh):
    r = lax.rsqrt(jnp.mean(x * x, axis=-1, keepdims=True) + EPS)
    return (x * r * g) * (1.0 + sc) + sh


def norm_mod(x, g, sc, sh, name):
    t, d = x.shape
    tr = _tile(t, 256, 8)

    def body(x_ref, g_ref, sc_ref, sh_ref, h_ref):
        h_ref[...] = _nm_fn(x_ref[...], g_ref[...], sc_ref[...], sh_ref[...]).astype(h_ref.dtype)

    row = pl.BlockSpec((tr, d), lambda i: (i, 0))
    vec = pl.BlockSpec((1, d), lambda i: (0, 0))
    return pl.pallas_call(
        body, name=name, grid=(t // tr,), in_specs=[row, vec, vec, vec], out_specs=row,
        out_shape=jax.ShapeDtypeStruct((t, d), CD), compiler_params=_params(("parallel",)),
    )(x, g, sc, sh)


def norm_mod_bwd(x, g, sc, sh, dh, dxo, f, gate_scale, name):
    t, d = x.shape
    tr = _tile(t, 256, 8)

    def body(x_ref, g_ref, sc_ref, sh_ref, dh_ref, dxo_ref, f_ref, dx_ref, dg_ref, dsc_ref, dsh_ref, dgt_ref):
        _, vjp = jax.vjp(_nm_fn, x_ref[...], g_ref[...], sc_ref[...], sh_ref[...])
        dx, dg, dsc, dsh = vjp(dh_ref[...])
        dxo_v = dxo_ref[...]
        dx_ref[...] = dxo_v + dx
        dgt = gate_scale * jnp.sum(f_ref[...] * dxo_v, axis=0, keepdims=True)

        @pl.when(pl.program_id(0) == 0)
        def _():
            dg_ref[...] = dg
            dsc_ref[...] = dsc
            dsh_ref[...] = dsh
            dgt_ref[...] = dgt

        @pl.when(pl.program_id(0) > 0)
        def _():
            dg_ref[...] += dg
            dsc_ref[...] += dsc
            dsh_ref[...] += dsh
            dgt_ref[...] += dgt

    row = pl.BlockSpec((tr, d), lambda i: (i, 0))
    vec = pl.BlockSpec((1, d), lambda i: (0, 0))
    vshape = jax.ShapeDtypeStruct((1, d), F32)
    return pl.pallas_call(
        body, name=name, grid=(t // tr,), in_specs=[row, vec, vec, vec, row, row, row],
        out_specs=[row, vec, vec, vec, vec],
        out_shape=[jax.ShapeDtypeStruct((t, d), F32), vshape, vshape, vshape, vshape],
        compiler_params=_params(("arbitrary",)),
    )(x, g, sc, sh, dh, dxo, f)


HALF = N_DEV // 2


def ffn_up(h, u_all, name):
    t, d = h.shape
    cp = u_all.shape[1]
    f = HALF * cp
    tm, tn = _tile(t, 1024, LANE), cp
    per = cp // tn
    sub = _tile(tn, 256, LANE)

    def body(h_ref, wg_ref, wu_ref, g_ref, u_ref, a_ref):
        hv = h_ref[...]
        for c0 in range(0, tn, sub):
            cols = slice(c0, c0 + sub)
            gate = _nt(hv, wg_ref[0, cols, :])
            up = _nt(hv, wu_ref[0, cols, :])
            g_ref[:, cols] = gate.astype(g_ref.dtype)
            u_ref[:, cols] = up.astype(u_ref.dtype)
            a_ref[:, cols] = (_silu(gate) * up).astype(a_ref.dtype)

    o = pl.BlockSpec((tm, tn), lambda i, j: (i, j))
    wg = pl.BlockSpec((1, tn, d), lambda i, j: (j // per, j % per, 0))
    wu = pl.BlockSpec((1, tn, d), lambda i, j: (HALF + j // per, j % per, 0))
    return pl.pallas_call(
        body, name=name, grid=(t // tm, f // tn),
        in_specs=[pl.BlockSpec((tm, d), lambda i, j: (i, 0)), wg, wu], out_specs=[o, o, o],
        out_shape=[jax.ShapeDtypeStruct((t, f), CD)] * 3,
        compiler_params=_params(("parallel", "parallel")),
    )(h, u_all, u_all)


def ffn_up_wg(h, dgate, dup, cp, name):
    t, d = h.shape
    tn = _tile(d, 512, LANE)

    def body(h_ref, dg_ref, du_ref, o_ref):
        s = pl.program_id(0)

        @pl.when(s < HALF)
        def _():
            o_ref[0] = _tn(dg_ref[...], h_ref[...]).astype(o_ref.dtype)

        @pl.when(s >= HALF)
        def _():
            o_ref[0] = _tn(du_ref[...], h_ref[...]).astype(o_ref.dtype)

    return pl.pallas_call(
        body, name=name, grid=(N_DEV, d // tn),
        in_specs=[pl.BlockSpec((t, tn), lambda s, i: (0, i)),
                  pl.BlockSpec((t, cp), lambda s, i: (0, jnp.minimum(s, HALF - 1))),
                  pl.BlockSpec((t, cp), lambda s, i: (0, jnp.maximum(s - HALF, 0)))],
        out_specs=pl.BlockSpec((1, cp, tn), lambda s, i: (s, 0, i)),
        out_shape=jax.ShapeDtypeStruct((N_DEV, cp, d), CD),
        compiler_params=_params(("parallel", "parallel")),
    )(h, dgate, dup)


def ffn_up_dg(dgate, dup, u_all, name):
    t = dgate.shape[0]
    cp, d = u_all.shape[1], u_all.shape[2]
    tm, tn = _tile(t, 1024, LANE), _tile(d, 1024, LANE)

    def body(dg_ref, du_ref, u_ref, o_ref, acc_ref):
        s = pl.program_id(2)

        @pl.when(s == 0)
        def _():
            acc_ref[...] = jnp.zeros_like(acc_ref)

        @pl.when(s < HALF)
        def _():
            acc_ref[...] += _nn(dg_ref[...], u_ref[0])

        @pl.when(s >= HALF)
        def _():
            acc_ref[...] += _nn(du_ref[...], u_ref[0])

        @pl.when(s == N_DEV - 1)
        def _():
            o_ref[...] = acc_ref[...]

    return pl.pallas_call(
        body, name=name, grid=(t // tm, d // tn, N_DEV),
        in_specs=[pl.BlockSpec((tm, cp), lambda i, j, s: (i, jnp.minimum(s, HALF - 1))),
                  pl.BlockSpec((tm, cp), lambda i, j, s: (i, jnp.maximum(s - HALF, 0))),
                  pl.BlockSpec((1, cp, tn), lambda i, j, s: (s, 0, j))],
        out_specs=pl.BlockSpec((tm, tn), lambda i, j, s: (i, j)),
        out_shape=jax.ShapeDtypeStruct((t, d), F32),
        scratch_shapes=[pltpu.VMEM((tm, tn), F32)],
        compiler_params=_params(("parallel", "parallel", "arbitrary")),
    )(dgate, dup, u_all)


def ffn_dact(dxo, s, wd, gate, up, name):
    t, d = dxo.shape
    f = wd.shape[0]
    tm, tn = _tile(t, 1024, LANE), _tile(f, 384, LANE)

    def body(dxo_ref, s_ref, wd_ref, g_ref, u_ref, dg_ref, du_ref):
        da = _nt(dxo_ref[...] * s_ref[...], wd_ref[...])
        gate, up = g_ref[...].astype(F32), u_ref[...].astype(F32)
        sg = jax.nn.sigmoid(gate)
        dg_ref[...] = (da * up * sg * (1.0 + gate * (1.0 - sg))).astype(dg_ref.dtype)
        du_ref[...] = (da * gate * sg).astype(du_ref.dtype)

    o = pl.BlockSpec((tm, tn), lambda i, j: (i, j))
    return pl.pallas_call(
        body, name=name, grid=(t // tm, f // tn),
        in_specs=[pl.BlockSpec((tm, d), lambda i, j: (i, 0)), pl.BlockSpec((1, d), lambda i, j: (0, 0)),
                  pl.BlockSpec((tn, d), lambda i, j: (j, 0)), o, o],
        out_specs=[o, o],
        out_shape=[jax.ShapeDtypeStruct((t, f), CD), jax.ShapeDtypeStruct((t, f), CD)],
        compiler_params=_params(("parallel", "parallel")),
    )(dxo, s, wd, gate, up)


def _bdot(a, b, ca, cb, precision=None):
    if precision is None:
        a, b = a.astype(CD), b.astype(CD)
    return lax.dot_general(a, b, (((ca,), (cb,)), ((0,), (0,))), precision=precision, preferred_element_type=F32)


def _bnn(a, b, precision=None):
    return _bdot(a, b, 2, 1, precision)


def _bnt(a, b, precision=None):
    return _bdot(a, b, 2, 2, precision)


def _btn(a, b, precision=None):
    return _bdot(a, b, 1, 1, precision)


def _att_fn(q, kp, kc, vp, vc, qg, kg, mask_p, mask_c):
    b, w = q.shape
    nh = w // ATT_HEAD_DIM
    head_of_lane = lax.broadcasted_iota(jnp.int32, (nh, 1, w), 2) // ATT_HEAD_DIM
    hm = (head_of_lane == lax.broadcasted_iota(jnp.int32, (nh, 1, w), 0)).astype(F32)

    def rn(x, g):
        ss = jnp.sum((x * x)[None] * hm, axis=-1, keepdims=True)
        r = jnp.sum(lax.rsqrt(ss * (1.0 / ATT_HEAD_DIM) + EPS) * hm, axis=0)
        return x * r * g

    qn, kpn, kcn = rn(q, qg), rn(kp, kg), rn(kc, kg)
    q4 = (qn[None] * hm).reshape(nh * b, w)
    scale = ATT_HEAD_DIM ** -0.5
    sp = jnp.where(mask_p, _nt(q4, kpn) * scale, NEG)
    sc = jnp.where(mask_c, _nt(q4, kcn) * scale, NEG)
    m = jnp.maximum(jnp.max(sp, axis=-1, keepdims=True), jnp.max(sc, axis=-1, keepdims=True))
    pp, pc = jnp.exp(sp - m), jnp.exp(sc - m)
    den = jnp.sum(pp, axis=-1, keepdims=True) + jnp.sum(pc, axis=-1, keepdims=True)
    o4 = _nn(pp / den, vp) + _nn(pc / den, vc)
    o = jnp.sum(o4.reshape(nh, b, w) * hm, axis=0)
    lse = jnp.sum((m + jnp.log(den)).reshape(nh, b, 1) * hm, axis=0)
    return o, lse


def _att_masks(g, j, nb_total, nh):
    nb = jnp.int32(nb_total // DILATIONS[0])
    for gi in range(1, len(DILATIONS)):
        nb = jnp.where(g == gi, jnp.int32(nb_total // DILATIONS[gi]), nb)
    has_prev = (j % nb) != 0
    row = lax.broadcasted_iota(jnp.int32, (nh * ATT_BLOCK, ATT_BLOCK), 0) % ATT_BLOCK
    col = lax.broadcasted_iota(jnp.int32, (nh * ATT_BLOCK, ATT_BLOCK), 1)
    mask_p = jnp.logical_and(col >= row, has_prev)
    mask_c = col <= row
    return mask_p, mask_c


def _att_specs(w):
    blk = (1, 1, ATT_BLOCK, w)
    q = pl.BlockSpec(blk, lambda g, j: (0, g, j, 0))
    kp = pl.BlockSpec(blk, lambda g, j: (1, g, jnp.maximum(j - 1, 0), 0))
    kc = pl.BlockSpec(blk, lambda g, j: (1, g, j, 0))
    vp = pl.BlockSpec(blk, lambda g, j: (2, g, jnp.maximum(j - 1, 0), 0))
    vc = pl.BlockSpec(blk, lambda g, j: (2, g, j, 0))
    gain = pl.BlockSpec((1, w), lambda g, j: (0, 0))
    out = pl.BlockSpec((1, ATT_BLOCK, w), lambda g, j: (g, j, 0))
    return [q, kp, kc, vp, vc, gain, gain], out


def att_fwd(qkv, qg, kg, name):
    _, ng, t, w = qkv.shape
    nbt = t // ATT_BLOCK
    in_specs, out = _att_specs(w)

    def body(q_ref, kp_ref, kc_ref, vp_ref, vc_ref, qg_ref, kg_ref, o_ref, lse_ref):
        mask_p, mask_c = _att_masks(pl.program_id(0), pl.program_id(1), nbt, w // ATT_HEAD_DIM)
        o, lse = _att_fn(q_ref[0, 0], kp_ref[0, 0], kc_ref[0, 0], vp_ref[0, 0], vc_ref[0, 0],
                         qg_ref[...], kg_ref[...], mask_p, mask_c)
        o_ref[0] = o
        lse_ref[0] = lse

    sh = jax.ShapeDtypeStruct((ng, t, w), F32)
    return pl.pallas_call(
        body, name=name, grid=(ng, nbt), in_specs=in_specs, out_specs=[out, out], out_shape=[sh, sh],
        compiler_params=_params(("parallel", "parallel")),
    )(qkv, qkv, qkv, qkv, qkv, qg, kg)


def att_bwd(qkv, qg, kg, do, dlse, name):
    _, ng, t, w = qkv.shape
    nbt = t // ATT_BLOCK
    in_specs, out = _att_specs(w)
    whole = pl.BlockSpec((1, t, w), lambda g, j: (g, 0, 0))
    gain = in_specs[-1]

    def body(q_ref, kp_ref, kc_ref, vp_ref, vc_ref, qg_ref, kg_ref, do_ref, dlse_ref,
             dq_ref, dk_ref, dv_ref, dqg_ref, dkg_ref):
        g, j = pl.program_id(0), pl.program_id(1)
        mask_p, mask_c = _att_masks(g, j, nbt, w // ATT_HEAD_DIM)

        @pl.when(j == 0)
        def _():
            dk_ref[...] = jnp.zeros_like(dk_ref)
            dv_ref[...] = jnp.zeros_like(dv_ref)

        @pl.when(jnp.logical_and(g == 0, j == 0))
        def _():
            dqg_ref[...] = jnp.zeros_like(dqg_ref)
            dkg_ref[...] = jnp.zeros_like(dkg_ref)

        rows_c = pl.ds(pl.multiple_of(j * ATT_BLOCK, ATT_BLOCK), ATT_BLOCK)
        rows_p = pl.ds(pl.multiple_of(jnp.maximum(j - 1, 0) * ATT_BLOCK, ATT_BLOCK), ATT_BLOCK)
        fn = functools.partial(_att_fn, mask_p=mask_p, mask_c=mask_c)
        _, vjp = jax.vjp(fn, q_ref[0, 0], kp_ref[0, 0], kc_ref[0, 0], vp_ref[0, 0], vc_ref[0, 0],
                         qg_ref[...], kg_ref[...])
        dq, dkp, dkc, dvp, dvc, dqg, dkg = vjp((do_ref[0], dlse_ref[0]))
        dq_ref[0] = dq
        dk_ref[0, rows_p, :] += dkp
        dv_ref[0, rows_p, :] += dvp
        dk_ref[0, rows_c, :] += dkc
        dv_ref[0, rows_c, :] += dvc
        dqg_ref[...] += dqg
        dkg_ref[...] += dkg

    sh = jax.ShapeDtypeStruct((ng, t, w), F32)
    gshape = jax.ShapeDtypeStruct((1, w), F32)
    return pl.pallas_call(
        body, name=name, grid=(ng, nbt), in_specs=in_specs + [out, out],
        out_specs=[out, whole, whole, gain, gain], out_shape=[sh, sh, sh, gshape, gshape],
        compiler_params=_params(("arbitrary", "arbitrary")),
    )(qkv, qkv, qkv, qkv, qkv, qg, kg, do, dlse)


def _combine_fn(o, lse):
    m = jnp.max(lse, axis=0, keepdims=True)
    e = jnp.exp(lse - m)
    w = e / jnp.sum(e, axis=0, keepdims=True)
    return jnp.sum(w * o, axis=0)


def att_combine(o, lse, name):
    ng, t, w = o.shape
    tr = _tile(t, 256, 16)
    spec = pl.BlockSpec((ng, tr, w), lambda i: (0, i, 0))
    y_spec = pl.BlockSpec((tr, w), lambda i: (i, 0))

    def body(o_ref, l_ref, y_ref):
        y_ref[...] = _combine_fn(o_ref[...], l_ref[...]).astype(y_ref.dtype)

    return pl.pallas_call(
        body, name=name, grid=(t // tr,), in_specs=[spec, spec], out_specs=y_spec,
        out_shape=jax.ShapeDtypeStruct((t, w), CD), compiler_params=_params(("parallel",)),
    )(o, lse)


def att_combine_bwd(o, lse, dy, name):
    ng, t, w = o.shape
    tr = _tile(t, 256, 8)
    spec = pl.BlockSpec((ng, tr, w), lambda i: (0, i, 0))
    y_spec = pl.BlockSpec((tr, w), lambda i: (i, 0))

    def body(o_ref, l_ref, dy_ref, do_ref, dl_ref):
        _, vjp = jax.vjp(_combine_fn, o_ref[...], l_ref[...])
        do, dl = vjp(dy_ref[...])
        do_ref[...] = do
        dl_ref[...] = dl

    sh = jax.ShapeDtypeStruct(o.shape, F32)
    return pl.pallas_call(
        body, name=name, grid=(t // tr,), in_specs=[spec, spec, y_spec], out_specs=[spec, spec],
        out_shape=[sh, sh], compiler_params=_params(("parallel",)),
    )(o, lse, dy)


def _shift_down(x, s):
    if s == 0:
        return x
    row = lax.broadcasted_iota(jnp.int32, x.shape, 0)
    return jnp.where(row >= s, pltpu.roll(x, s, 0), 0.0)


def _shift_up(x, s):
    if s == 0:
        return x
    t = x.shape[0]
    row = lax.broadcasted_iota(jnp.int32, x.shape, 0)
    return jnp.where(row < t - s, pltpu.roll(x, t - s, 0), 0.0)


def conv_fwd(z, col0, w, name):
    t = z.shape[0]
    c = w.shape[1]
    nblk0 = col0 // LANE

    def body(z_ref, w_ref, c_ref):
        zv = z_ref[...]
        acc = None
        for i in range(CONV_WIDTH):
            term = _shift_down(zv, CONV_WIDTH - 1 - i) * w_ref[i:i + 1, :]
            acc = term if acc is None else acc + term
        c_ref[...] = acc

    return pl.pallas_call(
        body, name=name, grid=(c // LANE,),
        in_specs=[pl.BlockSpec((t, LANE), lambda j: (0, nblk0 + j)), pl.BlockSpec((CONV_WIDTH, LANE), lambda j: (0, j))],
        out_specs=pl.BlockSpec((t, LANE), lambda j: (0, j)),
        out_shape=jax.ShapeDtypeStruct((t, c), F32), compiler_params=_params(("parallel",)),
    )(z, w)


def conv_bwd(dc, z, col0, w, name):
    t = z.shape[0]
    c = w.shape[1]
    nblk0 = col0 // LANE

    def body(dc_ref, z_ref, w_ref, dz_ref, dw_ref):
        dcv, zv = dc_ref[...], z_ref[...]
        acc = None
        for i in range(CONV_WIDTH):
            s = CONV_WIDTH - 1 - i
            term = _shift_up(dcv, s) * w_ref[i:i + 1, :]
            acc = term if acc is None else acc + term
            dw_ref[i:i + 1, :] = jnp.sum(dcv * _shift_down(zv, s), axis=0, keepdims=True)
        dz_ref[...] = acc.astype(dz_ref.dtype)

    blk = pl.BlockSpec((t, LANE), lambda j: (0, j))
    wblk = pl.BlockSpec((CONV_WIDTH, LANE), lambda j: (0, j))
    return pl.pallas_call(
        body, name=name, grid=(c // LANE,),
        in_specs=[blk, pl.BlockSpec((t, LANE), lambda j: (0, nblk0 + j)), wblk], out_specs=[blk, wblk],
        out_shape=[jax.ShapeDtypeStruct((t, c), CD), jax.ShapeDtypeStruct((CONV_WIDTH, c), F32)],
        compiler_params=_params(("parallel",)),
    )(dc, z, w)


def _dn_consts():
    c = DN_CHUNK
    row = lax.broadcasted_iota(jnp.int32, (c, c), 0)
    col = lax.broadcasted_iota(jnp.int32, (c, c), 1)
    return dict(tril=row >= col, strict=row > col, eye=(row == col).astype(F32),
                tril_f=(row >= col).astype(F32), triu_f=(row <= col).astype(F32))


def _softplus(x):
    return jnp.maximum(x, 0.0) + jnp.log(1.0 + jnp.exp(-jnp.abs(x)))


def _split(x):
    hi = x.astype(CD)
    return hi, (x - hi.astype(F32)).astype(CD)


def _bdot3(a, b, ca, cb):
    ah, al = _split(a)
    bh, bl = _split(b)
    return _bdot(ah, bh, ca, cb) + (_bdot(ah, bl, ca, cb) + _bdot(al, bh, ca, cb))


def _tri_inv_impl(a_mat):
    c = a_mat.shape[-1]
    eye = (lax.broadcasted_iota(jnp.int32, (c, c), 0) == lax.broadcasted_iota(jnp.int32, (c, c), 1)).astype(F32)
    nk_ = -a_mat
    t_inv = eye + nk_
    for _ in range(c.bit_length() - 2):
        nk_ = _bdot3(nk_, nk_, 2, 1)
        t_inv = t_inv + _bdot3(t_inv, nk_, 2, 1)
    return t_inv


@jax.custom_vjp
def _tri_inv(a_mat):
    return _tri_inv_impl(a_mat)


def _tri_inv_fwd(a_mat):
    t_inv = _tri_inv_impl(a_mat)
    return t_inv, t_inv


def _tri_inv_bwd(t_inv, dt_inv):
    return (-_bdot3(_bdot3(t_inv, dt_inv, 1, 1), t_inv, 2, 2),)


_tri_inv.defvjp(_tri_inv_fwd, _tri_inv_bwd)


def _dn_chunk(cq, ck, cv, og, a_col, b_col, al, dt, gn, s_prev, *, k, inv):
    q = _silu(cq)
    q = q * lax.rsqrt(jnp.sum(q * q, axis=-1, keepdims=True) + EPS) * (DN_HEAD_DIM ** -0.5)
    kk = _silu(ck)
    kk = kk * lax.rsqrt(jnp.sum(kk * kk, axis=-1, keepdims=True) + EPS)
    v = _silu(cv)
    g = -jnp.exp(al) * _softplus(a_col + dt)
    beta = jax.nn.sigmoid(b_col)
    g_row = jnp.sum(k["eye"] * g, axis=1, keepdims=True)
    gc_col = jnp.sum(k["tril_f"] * g_row, axis=2, keepdims=True)
    gc_row = jnp.sum(k["triu_f"] * g, axis=1, keepdims=True)
    ldec = jnp.where(k["tril"], jnp.exp(jnp.where(k["tril"], gc_col - gc_row, 0.0)), 0.0)
    kb, vb = kk * beta, v * beta
    a_mat = jnp.where(k["strict"], _bnt(kb, kk) * ldec, 0.0)
    t_inv = inv(a_mat)
    egc = jnp.exp(gc_col)
    u = _bnn(t_inv, vb)
    w = _bnn(t_inv, kb * egc)
    attn = jnp.where(k["tril"], _bnt(q, kk) * ldec, 0.0)
    gc_last = jnp.sum(g, axis=1, keepdims=True)
    k_dec = kk * jnp.exp(gc_last - gc_col)
    v_new = u - _bnn(w, s_prev)
    o = _bnn(q * egc, s_prev) + _bnn(attn, v_new)
    s_new = s_prev * jnp.exp(gc_last) + _btn(k_dec, v_new)
    y = o * lax.rsqrt(jnp.mean(o * o, axis=-1, keepdims=True) + EPS) * gn * _silu(og)
    return y, s_new


def _dn_heads(ref, nh):
    hd = DN_HEAD_DIM
    return jnp.stack([ref[:, h * hd:(h + 1) * hd] for h in range(nh)])


def _dn_specs(nh, col_gate, col_ab, order):
    hd, c = DN_HEAD_DIM, DN_CHUNK
    w = nh * hd
    qs = pl.BlockSpec((c, w), lambda n: (order(n), 0))
    ks = pl.BlockSpec((c, w), lambda n: (order(n), 1))
    vs = pl.BlockSpec((c, w), lambda n: (order(n), 2))
    gs = pl.BlockSpec((c, w), lambda n: (order(n), col_gate // w))
    ab = pl.BlockSpec((c, LANE), lambda n: (order(n), col_ab // LANE))
    scal = pl.BlockSpec((nh, 1, 1), lambda n: (0, 0, 0))
    gn = pl.BlockSpec((1, hd), lambda n: (0, 0))
    st = pl.BlockSpec((1, nh, hd, hd), lambda n: (order(n), 0, 0, 0))
    return qs, ks, vs, gs, ab, scal, gn, st


def _lane_pick(x, idx):
    lane = lax.broadcasted_iota(jnp.int32, x.shape, 1)
    return jnp.sum(jnp.where(lane == idx, x, 0.0), axis=1, keepdims=True)


def dn_fwd(cv, z, col_gate, col_ab, a_log, dt_bias, gn, name):
    t = cv.shape[0]
    nh = a_log.shape[0]
    hd, c = DN_HEAD_DIM, DN_CHUNK
    n_chunks = t // c
    qs, ks, vs, gs, ab, scal, gnspec, st = _dn_specs(nh, col_gate, col_ab, lambda n: n)

    def body(q_ref, k_ref, v_ref, g_ref, ab_ref, al_ref, dt_ref, gn_ref, y_ref, st_ref, s_scr):
        @pl.when(pl.program_id(0) == 0)
        def _():
            s_scr[...] = jnp.zeros_like(s_scr)

        abv = ab_ref[...]
        a_col = jnp.stack([_lane_pick(abv, h) for h in range(nh)])
        b_col = jnp.stack([_lane_pick(abv, nh + h) for h in range(nh)])
        s_prev = s_scr[...]
        st_ref[0] = s_prev
        y, s_new = _dn_chunk(_dn_heads(q_ref, nh), _dn_heads(k_ref, nh), _dn_heads(v_ref, nh), _dn_heads(g_ref, nh),
                             a_col, b_col, al_ref[...], dt_ref[...], gn_ref[...], s_prev,
                             k=_dn_consts(), inv=_tri_inv_impl)
        for h in range(nh):
            y_ref[:, h * hd:(h + 1) * hd] = y[h].astype(y_ref.dtype)
        s_scr[...] = s_new

    return pl.pallas_call(
        body, name=name, grid=(n_chunks,),
        in_specs=[qs, ks, vs, gs, ab, scal, scal, gnspec],
        out_specs=[pl.BlockSpec((c, nh * hd), lambda n: (n, 0)), st],
        out_shape=[jax.ShapeDtypeStruct((t, nh * hd), CD), jax.ShapeDtypeStruct((n_chunks, nh, hd, hd), F32)],
        scratch_shapes=[pltpu.VMEM((nh, hd, hd), F32)],
        compiler_params=_params(("arbitrary",)),
    )(cv, cv, cv, z, z, a_log, dt_bias, gn)


def dn_bwd(cv, z, col_gate, col_ab, a_log, dt_bias, gn, states, dy, name):
    t = cv.shape[0]
    nh = a_log.shape[0]
    hd, c = DN_HEAD_DIM, DN_CHUNK
    w = nh * hd
    n_chunks = t // c
    rev = lambda n: n_chunks - 1 - n
    qs, ks, vs, gs, ab, scal, gnspec, st = _dn_specs(nh, col_gate, col_ab, rev)
    yspec = pl.BlockSpec((c, w), lambda n: (rev(n), 0))

    def body(q_ref, k_ref, v_ref, g_ref, ab_ref, al_ref, dt_ref, gn_ref, st_ref, dy_ref,
             dc_ref, dg_ref, dab_ref, dal_ref, ddt_ref, dgn_ref, ds_scr):
        @pl.when(pl.program_id(0) == 0)
        def _():
            ds_scr[...] = jnp.zeros_like(ds_scr)
            dal_ref[...] = jnp.zeros_like(dal_ref)
            ddt_ref[...] = jnp.zeros_like(ddt_ref)
            dgn_ref[...] = jnp.zeros_like(dgn_ref)

        abv = ab_ref[...]
        a_col = jnp.stack([_lane_pick(abv, h) for h in range(nh)])
        b_col = jnp.stack([_lane_pick(abv, nh + h) for h in range(nh)])
        fn = functools.partial(_dn_chunk, k=_dn_consts(), inv=_tri_inv)
        _, vjp = jax.vjp(fn, _dn_heads(q_ref, nh), _dn_heads(k_ref, nh), _dn_heads(v_ref, nh), _dn_heads(g_ref, nh),
                         a_col, b_col, al_ref[...], dt_ref[...], gn_ref[...], st_ref[0])
        dq, dk, dv, dg, da, db, dal, ddt, dgn, ds = vjp((_dn_heads(dy_ref, nh), ds_scr[...]))
        lane = lax.broadcasted_iota(jnp.int32, (c, LANE), 1)
        dab = jnp.zeros((c, LANE), F32)
        for h in range(nh):
            cols = slice(h * hd, (h + 1) * hd)
            dc_ref[:, cols] = dq[h]
            dc_ref[:, w + h * hd:w + (h + 1) * hd] = dk[h]
            dc_ref[:, 2 * w + h * hd:2 * w + (h + 1) * hd] = dv[h]
            dg_ref[:, cols] = dg[h].astype(dg_ref.dtype)
            dab = dab + jnp.where(lane == h, da[h], 0.0) + jnp.where(lane == nh + h, db[h], 0.0)
        dab_ref[...] = dab.astype(dab_ref.dtype)
        dal_ref[...] += dal
        ddt_ref[...] += ddt
        dgn_ref[...] += dgn
        ds_scr[...] = ds

    sshape = jax.ShapeDtypeStruct((nh, 1, 1), F32)
    res = pl.pallas_call(
        body, name=name, grid=(n_chunks,),
        in_specs=[qs, ks, vs, gs, ab, scal, scal, gnspec, st, yspec],
        out_specs=[pl.BlockSpec((c, 3 * w), lambda n: (rev(n), 0)), yspec,
                   pl.BlockSpec((c, LANE), lambda n: (rev(n), 0)), scal, scal, gnspec],
        out_shape=[jax.ShapeDtypeStruct((t, 3 * w), F32), jax.ShapeDtypeStruct((t, w), CD),
                   jax.ShapeDtypeStruct((t, LANE), CD), sshape, sshape, jax.ShapeDtypeStruct((1, hd), F32)],
        scratch_shapes=[pltpu.VMEM((nh, hd, hd), F32)],
        compiler_params=_params(("arbitrary",)),
    )(cv, cv, cv, z, z, a_log, dt_bias, gn, states, dy)
    return res


def merge_fwd(ya, wpa, yd, wpd, z, col_m, name):
    t, d = yd.shape[0], wpd.shape[1]
    tm, tn = _tile(t, 512, LANE), _tile(d, 256, LANE)
    nb1, nb2 = col_m // tn, (col_m + d) // tn

    def body(ya_ref, wpa_ref, yd_ref, wpd_ref, z1_ref, z2_ref, m_ref, pa_ref, pd_ref):
        pa = _nn(ya_ref[...], wpa_ref[...])
        pd = _nn(yd_ref[...], wpd_ref[...])
        pa_ref[...] = pa
        pd_ref[...] = pd
        m_ref[...] = (jax.nn.sigmoid(z1_ref[...]) * pa + jax.nn.sigmoid(z2_ref[...]) * pd).astype(m_ref.dtype)

    o = pl.BlockSpec((tm, tn), lambda i, j: (i, j))
    return pl.pallas_call(
        body, name=name, grid=(t // tm, d // tn),
        in_specs=[pl.BlockSpec((tm, ya.shape[1]), lambda i, j: (i, 0)),
                  pl.BlockSpec((wpa.shape[0], tn), lambda i, j: (0, j)),
                  pl.BlockSpec((tm, yd.shape[1]), lambda i, j: (i, 0)),
                  pl.BlockSpec((wpd.shape[0], tn), lambda i, j: (0, j)),
                  pl.BlockSpec((tm, tn), lambda i, j: (i, nb1 + j)),
                  pl.BlockSpec((tm, tn), lambda i, j: (i, nb2 + j))],
        out_specs=[o, o, o],
        out_shape=[jax.ShapeDtypeStruct((t, d), CD), jax.ShapeDtypeStruct((t, d), F32),
                   jax.ShapeDtypeStruct((t, d), F32)],
        compiler_params=_params(("parallel", "parallel")),
    )(ya, wpa, yd, wpd, z, z)


def merge_bwd(dm, pa, pd, z, col_m, name):
    t, d = dm.shape
    tm, tn = _tile(t, 512, 8), _tile(d, 256, LANE)
    nb1, nb2 = col_m // tn, (col_m + d) // tn

    def body(dm_ref, pa_ref, pd_ref, z1_ref, z2_ref, dpa_ref, dpd_ref, dz1_ref, dz2_ref):
        dmv = dm_ref[...]
        s1, s2 = jax.nn.sigmoid(z1_ref[...]), jax.nn.sigmoid(z2_ref[...])
        dpa_ref[...] = (dmv * s1).astype(dpa_ref.dtype)
        dpd_ref[...] = (dmv * s2).astype(dpd_ref.dtype)
        dz1_ref[...] = (dmv * pa_ref[...] * s1 * (1.0 - s1)).astype(dz1_ref.dtype)
        dz2_ref[...] = (dmv * pd_ref[...] * s2 * (1.0 - s2)).astype(dz2_ref.dtype)

    o = pl.BlockSpec((tm, tn), lambda i, j: (i, j))
    sh = jax.ShapeDtypeStruct((t, d), CD)
    return pl.pallas_call(
        body, name=name, grid=(t // tm, d // tn),
        in_specs=[o, o, o, pl.BlockSpec((tm, tn), lambda i, j: (i, nb1 + j)),
                  pl.BlockSpec((tm, tn), lambda i, j: (i, nb2 + j))],
        out_specs=[o, o, o, o], out_shape=[sh, sh, sh, sh],
        compiler_params=_params(("parallel", "parallel")),
    )(dm, pa, pd, z, z)


def ada_fwd(c_all, w, b, name):
    nl, d, n = w.shape
    tn = _tile(n, 384, LANE)

    def body(c_ref, w_ref, b_ref, o_ref):
        o_ref[0] = _nn(_silu(c_ref[...]), w_ref[0]) + b_ref[0]

    return pl.pallas_call(
        body, name=name, grid=(nl, n // tn),
        in_specs=[pl.BlockSpec(c_all.shape, lambda l, j: (0, 0)), pl.BlockSpec((1, d, tn), lambda l, j: (l, 0, j)),
                  pl.BlockSpec((1, 1, tn), lambda l, j: (l, 0, j))],
        out_specs=pl.BlockSpec((1, c_all.shape[0], tn), lambda l, j: (l, 0, j)),
        out_shape=jax.ShapeDtypeStruct((nl, c_all.shape[0], n), F32),
        compiler_params=_params(("parallel", "parallel")),
    )(c_all, w, b)


def ada_bwd(c_pad, dmod_pad, name):
    nl, kp, n = dmod_pad.shape
    d = c_pad.shape[1]
    tn = _tile(n, 384, LANE)

    def body(c_ref, g_ref, o_ref):
        o_ref[0] = _tn(_silu(c_ref[...]), g_ref[0])

    return pl.pallas_call(
        body, name=name, grid=(nl, n // tn),
        in_specs=[pl.BlockSpec((kp, d), lambda l, j: (0, 0)), pl.BlockSpec((1, kp, tn), lambda l, j: (l, 0, j))],
        out_specs=pl.BlockSpec((1, d, tn), lambda l, j: (l, 0, j)),
        out_shape=jax.ShapeDtypeStruct((nl, d, n), F32),
        compiler_params=_params(("parallel", "parallel")),
    )(c_pad, dmod_pad)


def loss_head(y, target, name):
    t, d = y.shape
    tr = _tile(t, 256, 8)

    def body(y_ref, t_ref, dy_ref, l_ref):
        err = y_ref[...] - t_ref[...]
        dy_ref[...] = err * (1.0 / d)
        part = jnp.sum(jnp.sum(err * err, axis=1, keepdims=True), axis=0, keepdims=True) * (0.5 / d)

        @pl.when(pl.program_id(0) == 0)
        def _():
            l_ref[...] = jnp.zeros_like(l_ref)

        l_ref[...] += part

    row = pl.BlockSpec((tr, d), lambda i: (i, 0))
    return pl.pallas_call(
        body, name=name, grid=(t // tr,), in_specs=[row, row],
        out_specs=[row, pl.BlockSpec((8, LANE), lambda i: (0, 0))],
        out_shape=[jax.ShapeDtypeStruct((t, d), F32), jax.ShapeDtypeStruct((8, LANE), F32)],
        compiler_params=_params(("arbitrary",)),
    )(y, target)


def _adamw_update(g, w_ref, m_ref, v_ref, go_ref, d_ref, mo_ref, vo_ref):
    m_new = ADAM_B1 * m_ref[...] + (1.0 - ADAM_B1) * g
    v_new = ADAM_B2 * v_ref[...] + (1.0 - ADAM_B2) * jnp.square(g)
    m_hat = m_new / (1.0 - ADAM_B1 ** ADAM_STEP)
    v_hat = v_new / (1.0 - ADAM_B2 ** ADAM_STEP)
    go_ref[...] = g
    d_ref[...] = -ADAM_LR * (m_hat / (jnp.sqrt(v_hat) + ADAM_EPS) + ADAM_WD * w_ref[...])
    mo_ref[...] = m_new
    vo_ref[...] = v_new


def adamw(w, m, v, g_slots, name):
    shape = w.shape
    nslot = g_slots.shape[0]
    if w.ndim == 2:
        w3, m3, v3, g4 = w[None], m[None], v[None], g_slots[:, None]
    else:
        w3, m3, v3, g4 = w, m, v, g_slots
    nl, r, c = w3.shape
    tr = _tile(r, 256, 16)

    def body(w_ref, m_ref, v_ref, g_ref, *outs):
        g = g_ref[0].astype(F32)
        for s in range(1, nslot):
            g = g + g_ref[s].astype(F32)
        _adamw_update(g, w_ref, m_ref, v_ref, *outs)

    blk = pl.BlockSpec((1, tr, c), lambda l, i: (l, i, 0))
    gblk = pl.BlockSpec((nslot, 1, tr, c), lambda l, i: (0, l, i, 0))
    sh = jax.ShapeDtypeStruct(w3.shape, F32)
    outs = pl.pallas_call(
        body, name=name, grid=(nl, r // tr), in_specs=[blk, blk, blk, gblk], out_specs=[blk] * 4,
        out_shape=[sh] * 4, compiler_params=_params(("parallel", "parallel")),
    )(w3, m3, v3, g4)
    return tuple(o.reshape(shape) for o in outs)


def adamw_layers(w, m, v, g_layers, first, name, prev=None):
    _, r, c = w.shape
    n = len(g_layers)
    nslot, _, cg = g_layers[0].shape
    tr = _tile(r, 256, 16)

    def body(w_ref, m_ref, v_ref, *rest):
        g_refs, outs = rest[:n], rest[-4:]
        for li in range(n):
            @pl.when(pl.program_id(0) == li)
            def _(g_ref=g_refs[li]):
                g = g_ref[0, :, :c].astype(F32)
                for s in range(1, nslot):
                    g = g + g_ref[s, :, :c].astype(F32)
                _adamw_update(g[None], w_ref, m_ref, v_ref, *outs)

    blk = pl.BlockSpec((1, tr, c), lambda l, i: (first + l, i, 0))
    gblks = [pl.BlockSpec((nslot, tr, cg), lambda l, i, li=li: (0, jnp.where(l == li, i, 0), 0)) for li in range(n)]
    sh = jax.ShapeDtypeStruct(w.shape, F32)
    extra = [] if prev is None else list(prev)
    return tuple(pl.pallas_call(
        body, name=name, grid=(n, r // tr),
        in_specs=[blk, blk, blk] + gblks + [pl.BlockSpec(memory_space=pl.ANY)] * len(extra),
        out_specs=[blk] * 4, out_shape=[sh] * 4,
        input_output_aliases={3 + n + k: k for k in range(len(extra))},
        compiler_params=_params(("arbitrary", "arbitrary")),
    )(w, m, v, *g_layers, *extra))


def own_slots(arrs, gather, name):
    n = len(arrs)

    def body(*refs):
        srcs, outs, sems = refs[:n], refs[n:2 * n], refs[2 * n]
        me = 4 * lax.axis_index("x") + 2 * lax.axis_index("y") + lax.axis_index("c")
        cps = [pltpu.make_async_copy(srcs[a] if gather else srcs[a].at[me], outs[a].at[me], sems.at[a])
               for a in range(n)]
        for cp in cps:
            cp.start()
        for cp in cps:
            cp.wait()

    hbm = pl.BlockSpec(memory_space=pl.ANY)
    return list(pl.pallas_call(
        body, name=name,
        out_shape=[jax.ShapeDtypeStruct(((N_DEV,) + a.shape) if gather else a.shape, a.dtype) for a in arrs],
        in_specs=[hbm] * n, out_specs=[hbm] * n, scratch_shapes=[pltpu.SemaphoreType.DMA((n,))],
    )(*arrs))


def _rows(flat, unit=16):
    n = flat.shape[0]
    per = 1024 * unit
    pad = (-n) % per
    if pad:
        flat = jnp.concatenate([flat, jnp.zeros((pad,), flat.dtype)])
    return flat.reshape(-1, 1024)


def _to_classes(a):
    t, w = a.shape[1], a.shape[2]
    return jnp.stack([a[gi].reshape(t // dil, dil, w).transpose(1, 0, 2).reshape(t, w)
                      for gi, dil in enumerate(DILATIONS)])


def _from_classes(a):
    t, w = a.shape[1], a.shape[2]
    return jnp.stack([a[gi].reshape(dil, t // dil, w).transpose(1, 0, 2).reshape(t, w)
                      for gi, dil in enumerate(DILATIONS)])


def _unshard_cols(g):
    return g.transpose(1, 0, 2).reshape(g.shape[1], -1)


def _shard_cols(full):
    r = full.shape[0]
    return full.reshape(r, N_DEV, -1).transpose(1, 0, 2)


def kernel(x, c, ada_w, ada_b, norm_ff1, ffn1_w_up, ffn1_w_down, norm_mix, w_in, q_norm, k_norm, conv_w, a_log, dt_bias, dn_norm, w_proj_att, w_proj_dn, w_out, norm_ff2, ffn2_w_up, ffn2_w_down, loss_target, m_ada_w, m_ada_b, m_norm_ff1, m_ffn1_w_up, m_ffn1_w_down, m_norm_mix, m_w_in, m_q_norm, m_k_norm, m_conv_w, m_a_log, m_dt_bias, m_dn_norm, m_w_proj_att, m_w_proj_dn, m_w_out, m_norm_ff2, m_ffn2_w_up, m_ffn2_w_down, v_ada_w, v_ada_b, v_norm_ff1, v_ffn1_w_up, v_ffn1_w_down, v_norm_mix, v_w_in, v_q_norm, v_k_norm, v_conv_w, v_a_log, v_dt_bias, v_dn_norm, v_w_proj_att, v_w_proj_dn, v_w_out, v_norm_ff2, v_ffn2_w_up, v_ffn2_w_down):
    nl = ada_w.shape[0]
    t, d = x.shape[1], x.shape[2]
    dff = ffn1_w_down.shape[1] * N_DEV
    ha = d // 256
    ng = len(DILATIONS)
    wa = ha * ATT_HEAD_DIM
    att_w = ng * wa
    nh = d // DN_HEAD_DIM
    dn_w = nh * DN_HEAD_DIM
    n_in = w_in.shape[2] * N_DEV
    off_dn, off_gate = 3 * att_w, 3 * att_w + 3 * dn_w
    off_a = off_gate + dn_w
    off_merge = off_a + 2 * nh
    assert off_merge + 2 * d == n_in
    col_dn, col_gate, col_m = 0, 3 * dn_w, 4 * dn_w
    col_att = col_m + 2 * d
    col_ab = col_att + 3 * att_w
    zw = col_ab + 2 * LANE
    me = 4 * lax.axis_index("x") + 2 * lax.axis_index("y") + lax.axis_index("c")
    xs = x[0]
    target = loss_target[0]

    conv_rows = _rows(conv_w.reshape(-1), 8)
    pack0 = jnp.concatenate([jnp.concatenate([c, jnp.zeros((7, d), F32)]).reshape(-1), conv_rows.reshape(-1)])
    pack0 = _rows(pack0, 8)
    g0 = all_gather([pack0], "ag_c_conv")[0].reshape(N_DEV, -1)
    c_all = g0[:, :d]
    cw = g0[:, 8 * d:8 * d + conv_w.size].reshape(N_DEV, nl, CONV_WIDTH, -1)
    conv_full = cw.transpose(1, 2, 0, 3).reshape(nl, CONV_WIDTH, 3 * dn_w)

    n_ada = ada_w.shape[2]
    b_mine = lax.dynamic_slice_in_dim(ada_b, me * n_ada, n_ada, axis=1)[:, None, :]
    mod_s = ada_fwd(c_all, ada_w, b_mine, "ada_fwd")
    gm = all_gather([mod_s], "ag_mod")[0]
    mod = lax.dynamic_index_in_dim(gm, me, axis=2, keepdims=False)
    mod = mod.transpose(1, 0, 2).reshape(nl, N_ADA, 1, d)

    kinds = [ffn1_w_up, ffn1_w_down, w_in, w_proj_att, w_proj_dn, w_out, ffn2_w_up, ffn2_w_down]
    c_up = ffn1_w_up.shape[2]
    cp = -(-c_up // LANE) * LANE
    r_dn = ffn1_w_down.shape[1]
    assert 2 * r_dn == c_up

    up1_t, up2_t = jnp.swapaxes(ffn1_w_up, 1, 2), jnp.swapaxes(ffn2_w_up, 1, 2)
    w_in_t = jnp.transpose(w_in, (2, 0, 1))

    def layer_shards(l):
        def pad_up(wt):
            return jnp.pad(wt.astype(CD), ((0, cp - c_up), (0, 0)))
        return [pad_up(up1_t[l]), ffn1_w_down[l].astype(CD), w_in_t[:, l].astype(CD), w_proj_att[l].astype(CD),
                w_proj_dn[l].astype(CD), w_out[l].astype(CD), pad_up(up2_t[l]), ffn2_w_down[l].astype(CD)]

    def gather_landing(shards):
        return own_slots(shards, True, "own_slots_ag")

    def slot_landing(arrs):
        return own_slots(arrs, False, "own_slots_a2a")

    gathered = all_gather(layer_shards(0), "ag_weights0")
    prefetch = None

    def full_weight(ki, l):
        blk = gathered[ki]
        if ki == 3:
            return _unshard_cols(blk)
        if ki == 2:
            return w_in_rows(blk.reshape(n_in, d))
        if ki in (1, 7):
            pairs = blk.reshape(HALF, c_up, d)
            return jnp.pad(pairs, ((0, 0), (0, cp - c_up), (0, 0))).reshape(HALF * cp, d)
        return blk.reshape(-1, blk.shape[2])

    def down_grad_slots(g):
        return g.reshape(HALF, cp, d)[:, :c_up].reshape(N_DEV, r_dn, d)

    def w_in_rows(wt):
        pad = jnp.zeros((zw - n_in, wt.shape[1]), wt.dtype)
        return jnp.concatenate([wt[off_dn:off_a], wt[off_merge:], wt[:off_dn], wt[off_a:off_merge], pad], axis=0)

    def w_in_cols_inv(g):
        return jnp.concatenate([g[:, col_att:col_ab], g[:, :col_m], g[:, col_ab:col_ab + 2 * nh], g[:, col_m:col_att]], axis=1)

    saved = []
    xc = xs
    mods = []
    for l in range(nl):
        sv = {}
        if prefetch is not None:
            gathered = exchange_wait(*prefetch[:4], xc, True, f"ag_wait{l}")
        mod_l = mod[l]
        if l + 1 < nl:
            shards = layer_shards(l + 1)
            behind = gathered[0][0, :1, :1].astype(F32) + mod[0, 0, :, :1]
            prefetch = exchange_start(shards, gather_landing(shards), behind, True, f"ag_start{l + 1}")
            mod_l = mod_l + prefetch[4][0, 0]
        mods.append(mod_l)
        sh1, sc1, gt1, sh2, sc2, gt2, sh3, sc3, gt3 = [mod_l[i] for i in range(N_ADA)]
        w_dn1, w_dn2 = full_weight(1, l), full_weight(7, l)
        win = full_weight(2, l)
        wpa, wpd, wo = full_weight(3, l), full_weight(4, l), full_weight(5, l)
        sv["w"] = (gathered[0], gathered[6], w_dn1, w_dn2, win, wpa, wpd, wo)

        def ffn(xin, g, sh, sc, gt, u_all, w_dn):
            h = norm_mod(xin, g, sc, sh, "norm_mod")
            gate, up, a = ffn_up(h, u_all, "ffn_up")
            xo, f = matmul([(a, w_dn)], "nn", F32, "ffn_down", out_scale=0.5 * gt, resid=xin, save_acc=True,
                           tm=1024, tk=4096)
            return xo, (xin, h, gate, up, a, f)

        xc, sv["ffn1"] = ffn(xc, norm_ff1[l:l + 1], sh1, sc1, gt1, gathered[0], w_dn1)

        x_mix = xc
        h2 = norm_mod(x_mix, norm_mix[l:l + 1], sc2, sh2, "norm_mod")
        z = matmul([(h2, win)], "nt", F32, "w_in", tm=1024, tn=512)
        z_att = z[:, col_att:col_ab].reshape(t, 3, ng, wa).transpose(1, 2, 0, 3)
        qkv = jnp.stack([_to_classes(z_att[i]) for i in range(3)])
        qg, kg = jnp.tile(q_norm[l:l + 1], (1, ha)), jnp.tile(k_norm[l:l + 1], (1, ha))
        o_cls, lse_cls = att_fwd(qkv, qg, kg, "att_fwd")
        o_tok, lse_tok = _from_classes(o_cls), _from_classes(lse_cls)
        ya = att_combine(o_tok, lse_tok, "att_combine")
        cvo = conv_fwd(z, col_dn, conv_full[l], "conv_fwd")
        al3, dt3 = a_log[l].reshape(nh, 1, 1), dt_bias[l].reshape(nh, 1, 1)
        gn = dn_norm[l:l + 1]
        yd, states = dn_fwd(cvo, z, col_gate, col_ab, al3, dt3, gn, "dn_fwd")
        mrg, pa, pd = merge_fwd(ya, wpa, yd, wpd, z, col_m, "merge_fwd")
        xc, f2 = matmul([(mrg, wo)], "nn", F32, "w_out", out_scale=gt2, resid=x_mix, save_acc=True)
        sv["mix"] = (x_mix, h2, z, qkv, o_tok, lse_tok, ya, cvo, yd, states, mrg, pa, pd, f2)

        xc, sv["ffn2"] = ffn(xc, norm_ff2[l:l + 1], sh3, sc3, gt3, gathered[6], w_dn2)
        saved.append(sv)

    dxc, loss_blk = loss_head(xc, target, "loss_head")

    def slots(ki, g):
        if ki in (0, 1, 6, 7):
            return g
        return _shard_cols(g) if ki in (2, 3) else g.reshape(N_DEV, -1, g.shape[1])

    gbig = [None] * len(kinds)
    dmods, small, pending = [], [], []
    token = None
    for l in reversed(range(nl)):
        sv = saved[l]
        mod_l = mods[l] if token is None else mods[l] + token[0, 0]
        sh1, sc1, gt1, sh2, sc2, gt2, sh3, sc3, gt3 = [mod_l[i] for i in range(N_ADA)]
        u_all1, u_all2, w_dn1, w_dn2, win, wpa, wpd, wo = sv["w"]

        def ffn_bwd(dxo, g, sh, sc, gt, u_all, w_dn, sv_f):
            xin, h, gate, up, a, f = sv_f
            s = 0.5 * gt
            g_dn = down_grad_slots(matmul([(a, dxo)], "tn", CD, "ffn_down_wg", out_scale=s, tm=1024, tn=1024))
            dgate, dup = ffn_dact(dxo, s, w_dn, gate, up, "ffn_dact")
            g_up = ffn_up_wg(h, dgate, dup, cp, "ffn_up_wg")
            dh = ffn_up_dg(dgate, dup, u_all, "ffn_up_dg")
            dx, dg, dsc, dsh, dgt = norm_mod_bwd(xin, g, sc, sh, dh, dxo, f, 0.5, "norm_mod_bwd")
            return dx, g_up, g_dn, (dg, dsc, dsh, dgt)

        def start_exchange(ids, tag, behind):
            send = [slots(ki, gbig[ki]) for ki in ids]
            started = exchange_start(send, slot_landing(send), behind, False, f"a2a_start{l}{tag}")
            pending.append((l, ids, tag, started))
            return started[4]

        dxc, gbig[6], gbig[7], (dg3, dsc3, dsh3, dgt3) = ffn_bwd(
            dxc, norm_ff2[l:l + 1], sh3, sc3, gt3, u_all2, w_dn2, sv["ffn2"])
        token = start_exchange([6, 7], "f2", dxc)
        sh2, sc2, gt2 = (vec + token[0, 0] for vec in (sh2, sc2, gt2))

        x_mix, h2, z, qkv, o_tok, lse_tok, ya, cvo, yd, states, mrg, pa, pd, f2 = sv["mix"]
        gbig[5] = matmul([(mrg, dxc)], "tn", CD, "w_out_wg", out_scale=gt2)
        dm = matmul([(dxc, wo)], "nt", F32, "w_out_dg", a_scale=gt2)
        dpa, dpd, dz1, dz2 = merge_bwd(dm, pa, pd, z, col_m, "merge_bwd")
        gbig[3] = matmul([(ya, dpa)], "tn", CD, "patt_wg")
        gbig[4] = matmul([(yd, dpd)], "tn", CD, "pdn_wg")
        dya = matmul([(dpa, wpa)], "nt", F32, "patt_dg")
        dyd = matmul([(dpd, wpd)], "nt", F32, "pdn_dg")
        do_tok, dlse_tok = att_combine_bwd(o_tok, lse_tok, dya, "att_combine_bwd")
        qg, kg = jnp.tile(q_norm[l:l + 1], (1, ha)), jnp.tile(k_norm[l:l + 1], (1, ha))
        dq, dk, dv, dqg, dkg = att_bwd(qkv, qg, kg, _to_classes(do_tok), _to_classes(dlse_tok), "att_bwd")
        dqg, dkg = (jnp.sum(v_.reshape(ha, ATT_HEAD_DIM), axis=0, keepdims=True) for v_ in (dqg, dkg))
        dz_att = jnp.stack([_from_classes(a_) for a_ in (dq, dk, dv)])
        dz_att = dz_att.transpose(2, 0, 1, 3).reshape(t, 3 * att_w).astype(CD)
        al3, dt3 = a_log[l].reshape(nh, 1, 1), dt_bias[l].reshape(nh, 1, 1)
        gn = dn_norm[l:l + 1]
        dcvo, dz_gate, dz_ab, dal, ddt, dgn = dn_bwd(
            cvo, z, col_gate, col_ab, al3, dt3, gn, states, dyd, "dn_bwd")
        dz_dn, dconv = conv_bwd(dcvo, z, col_dn, conv_full[l], "conv_bwd")
        dz = jnp.concatenate([dz_dn, dz_gate, dz1, dz2, dz_att, dz_ab, jnp.zeros((t, LANE), CD)], axis=1)
        gbig[2] = w_in_cols_inv(matmul([(h2, dz)], "tn", CD, "w_in_wg", tm=1024, tn=1024))
        dh2 = matmul([(dz, win)], "nn", F32, "w_in_dg", tm=1024, tn=1024,
                     tk=zw // 4 if (zw // 4) % LANE == 0 else zw)
        dxc, dg2, dsc2, dsh2, dgt2 = norm_mod_bwd(x_mix, norm_mix[l:l + 1], sc2, sh2, dh2, dxc, f2, 1.0, "norm_mod_bwd_mix")

        token = start_exchange([2, 3, 4, 5], "mx", dxc)
        sh1, sc1, gt1 = (vec + token[0, 0] for vec in (sh1, sc1, gt1))
        dxc, gbig[0], gbig[1], (dg1, dsc1, dsh1, dgt1) = ffn_bwd(
            dxc, norm_ff1[l:l + 1], sh1, sc1, gt1, u_all1, w_dn1, sv["ffn1"])
        if l > 0:
            token = start_exchange([0, 1], "f1", dxc)
        dmods.append(jnp.concatenate([dsh1, dsc1, dgt1, dsh2, dsc2, dgt2, dsh3, dsc3, dgt3], axis=1))
        small.append((dg1, dg2, dg3, dconv, dqg, dkg, dal.reshape(1, nh), ddt.reshape(1, nh), dgn))
    dmods.reverse()
    small.reverse()

    big_names = ["ffn1_w_up", "ffn1_w_down", "w_in", "w_proj_att", "w_proj_dn", "w_out", "ffn2_w_up", "ffn2_w_down"]
    big_m = [m_ffn1_w_up, m_ffn1_w_down, m_w_in, m_w_proj_att, m_w_proj_dn, m_w_out, m_ffn2_w_up, m_ffn2_w_down]
    big_v = [v_ffn1_w_up, v_ffn1_w_down, v_w_in, v_w_proj_att, v_w_proj_dn, v_w_out, v_ffn2_w_up, v_ffn2_w_down]
    recv_layers = [[None] * len(kinds) for _ in range(nl)]

    def wait_exchanges(layers, behind):
        for pl_, ids, tag, st in pending:
            if pl_ in layers:
                got = exchange_wait(*st[:4], behind, False, f"a2a_wait{pl_}{tag}")
                for ki, arr in zip(ids, got):
                    recv_layers[pl_][ki] = arr

    def big_adamw(ki, layers, prev, tag):
        g_layers = [recv_layers[li][ki] for li in layers]
        wmv = (kinds[ki], big_m[ki], big_v[ki])
        if ki in (0, 6):
            wmv = tuple(jnp.swapaxes(a, 1, 2) for a in wmv)
        return adamw_layers(*wmv, g_layers, layers[0], f"adamw_{big_names[ki]}{tag}", prev)

    later = list(range(1, nl))
    partial = [None] * len(kinds)
    if later:
        wait_exchanges(later, dxc)
        partial = [big_adamw(ki, later, None, "_l1") for ki in range(len(kinds))]

    fields = [jnp.stack(dmods).reshape(-1)]
    fields += [jnp.stack([s[i] for s in small]).reshape(-1) for i in range(9)]
    fields.append(loss_blk[0, :1])
    if later:
        fields.append(partial[-1][1][1, :1, 0] * 0.0)
    fsizes = [f.size for f in fields]
    foffs = [sum(fsizes[:i]) for i in range(len(fields))]
    g1 = all_gather([_rows(jnp.concatenate(fields), 8)], "ag_small")[0].reshape(N_DEV, -1)
    l = 0
    start_exchange([0, 1], "f1", g1)

    def field(i, shape):
        return g1[:, foffs[i]:foffs[i] + fsizes[i]].reshape(N_DEV, *shape)

    loss = field(10, (1,))[0, 0]
    for j in range(1, N_DEV):
        loss = loss + field(10, (1,))[j, 0]

    results = {}
    dmod_all = field(0, (nl, N_ADA * d))
    c_pad = jnp.concatenate([c_all, jnp.zeros((8, d), F32)])
    dmod_mine = lax.dynamic_slice_in_dim(dmod_all, me * n_ada, n_ada, axis=2).transpose(1, 0, 2)
    dmod_pad = jnp.concatenate([dmod_mine, jnp.zeros((nl, 8, n_ada), F32)], axis=1)
    g_ada_w = ada_bwd(c_pad, dmod_pad, "ada_bwd")
    results["ada_w"] = adamw(ada_w, m_ada_w, v_ada_w, g_ada_w[None], "adamw_ada_w")
    results["ada_b"] = adamw(ada_b, m_ada_b, v_ada_b, dmod_all, "adamw_ada_b")
    results["norm_ff1"] = adamw(norm_ff1, m_norm_ff1, v_norm_ff1, field(1, (nl, d)), "adamw_norm_ff1")
    results["norm_mix"] = adamw(norm_mix, m_norm_mix, v_norm_mix, field(2, (nl, d)), "adamw_norm_mix")
    results["norm_ff2"] = adamw(norm_ff2, m_norm_ff2, v_norm_ff2, field(3, (nl, d)), "adamw_norm_ff2")
    conv_slots = lax.dynamic_slice_in_dim(field(4, (nl, CONV_WIDTH, 3 * dn_w)), me * conv_w.shape[2],
                                          conv_w.shape[2], axis=3)
    results["conv_w"] = adamw(conv_w, m_conv_w, v_conv_w, conv_slots, "adamw_conv_w")
    results["q_norm"] = adamw(q_norm, m_q_norm, v_q_norm, field(5, (nl, ATT_HEAD_DIM)), "adamw_q_norm")
    results["k_norm"] = adamw(k_norm, m_k_norm, v_k_norm, field(6, (nl, ATT_HEAD_DIM)), "adamw_k_norm")
    results["a_log"] = adamw(a_log, m_a_log, v_a_log, field(7, (nl, nh)), "adamw_a_log")
    results["dt_bias"] = adamw(dt_bias, m_dt_bias, v_dt_bias, field(8, (nl, nh)), "adamw_dt_bias")
    results["dn_norm"] = adamw(dn_norm, m_dn_norm, v_dn_norm, field(9, (nl, DN_HEAD_DIM)), "adamw_dn_norm")
    wait_exchanges([0], results["ada_w"][1])
    for ki, nm in enumerate(big_names):
        res = big_adamw(ki, [0], partial[ki], "_l0")
        results[nm] = tuple(jnp.swapaxes(r, 1, 2) for r in res) if ki in (0, 6) else res

    order = ["ada_w", "ada_b", "norm_ff1", "ffn1_w_up", "ffn1_w_down", "norm_mix", "w_in", "q_norm", "k_norm",
             "conv_w", "a_log", "dt_bias", "dn_norm", "w_proj_att", "w_proj_dn", "w_out", "norm_ff2",
             "ffn2_w_up", "ffn2_w_down"]
    outs = [loss, dxc[None]]
    for part in range(4):
        outs += [results[n][part] for n in order]
    return tuple(outs)
```

```python
import functools

import jax
import jax.numpy as jnp
from jax import lax
from jax.experimental import pallas as pl
from jax.experimental.pallas import tpu as pltpu

F32 = jnp.float32
CD = jnp.bfloat16
EPS = 1e-6
N_DEV = 8
LANE = 128
ATT_HEAD_DIM = 64
ATT_BLOCK = 128
DILATIONS = (1, 4, 16)
DN_HEAD_DIM = 128
DN_CHUNK = 64
CONV_WIDTH = 4
N_ADA = 9
ADAM_LR, ADAM_B1, ADAM_B2, ADAM_EPS, ADAM_WD, ADAM_STEP = 0.001, 0.9, 0.999, 1e-08, 0.01, 10
VMEM_LIMIT = 56 * 1024 * 1024
NEG = -1e30
MESH = pl.DeviceIdType.MESH
HI = lax.Precision.HIGHEST


def _params(sem=None):
    return pltpu.CompilerParams(dimension_semantics=sem, vmem_limit_bytes=VMEM_LIMIT)


def _tile(n, pref, unit):
    best = None
    t = unit
    while t <= min(n, pref):
        if n % t == 0:
            best = t
        t += unit
    return best if best is not None else n


def _silu(x):
    return x * jax.nn.sigmoid(x)


def _dot(a, b, dims, precision=None):
    if precision is None:
        a, b = a.astype(CD), b.astype(CD)
    return lax.dot_general(a, b, (dims, ((), ())), precision=precision, preferred_element_type=F32)


def _nn(a, b, precision=None):
    return _dot(a, b, ((1,), (0,)), precision)


def _nt(a, b, precision=None):
    return _dot(a, b, ((1,), (1,)), precision)


def _tn(a, b, precision=None):
    return _dot(a, b, ((0,), (0,)), precision)


def all_gather(arrs, name):
    n = len(arrs)

    def body(*refs):
        x_refs, out_refs = refs[:n], refs[n:2 * n]
        send_sems, recv_sems, local_sems = refs[2 * n:]
        x, y, c = lax.axis_index("x"), lax.axis_index("y"), lax.axis_index("c")
        me, sibling = (x, y, c), (x, y, 1 - c)
        chips = [(1 - x, y), (x, 1 - y), (1 - x, 1 - y)]

        def copy(a, k, block, to, src=None):
            slot = out_refs[a].at[4 * block[0] + 2 * block[1] + block[2]]
            return pltpu.make_async_remote_copy(
                src_ref=slot if src is None else src, dst_ref=slot,
                send_sem=send_sems.at[7 * a + k], recv_sem=recv_sems.at[7 * a + k],
                device_id=to, device_id_type=MESH)

        mine, first, passed = [], [], []
        for a in range(n):
            cp = pltpu.make_async_copy(x_refs[a], out_refs[a].at[4 * x + 2 * y + c], local_sems.at[a])
            cp.start()
            mine.append(cp)
            first.append(copy(a, 0, me, sibling, src=x_refs[a]))
            first += [copy(a, 1 + j, me, (*chip, c), src=x_refs[a]) for j, chip in enumerate(chips)]
        for cp in first:
            cp.start()
        for j, chip in enumerate(chips):
            for a in range(n):
                copy(a, 1 + j, (*chip, c), me).wait_recv()
                cp = copy(a, 4 + j, (*chip, c), sibling)
                cp.start()
                passed.append(cp)
        for a in range(n):
            copy(a, 0, sibling, me).wait_recv()
            for j, chip in enumerate(chips):
                copy(a, 4 + j, (*chip, 1 - c), me).wait_recv()
        for cp in first + passed:
            cp.wait_send()
        for cp in mine:
            cp.wait()

    hbm = pl.BlockSpec(memory_space=pl.ANY)
    outs = pl.pallas_call(
        body, name=name,
        out_shape=[jax.ShapeDtypeStruct((N_DEV,) + a.shape, a.dtype) for a in arrs],
        in_specs=[hbm] * n, out_specs=[hbm] * n,
        scratch_shapes=[pltpu.SemaphoreType.DMA((7 * n,)), pltpu.SemaphoreType.DMA((7 * n,)),
                        pltpu.SemaphoreType.DMA((n,))],
    )(*arrs)
    return list(outs)


_HBM = pl.BlockSpec(memory_space=pltpu.HBM)
_SEM = pl.BlockSpec(memory_space=pltpu.SEMAPHORE)
_EFFECT = pltpu.SideEffectType.DATAFLOW_SIDE_EFFECTING


def _exchange_copies(src_refs, land_refs, send_sem, recv_sem, gather):
    x, y, c = lax.axis_index("x"), lax.axis_index("y"), lax.axis_index("c")
    me = 4 * x + 2 * y + c
    pairs = []
    for k in range(1, N_DEV):
        px, py, pc = x ^ (k >> 2), y ^ ((k >> 1) & 1), c ^ (k & 1)
        pidx = 4 * px + 2 * py + pc
        for src, land in zip(src_refs, land_refs):
            mine = src if gather else src.at[pidx]
            out = pltpu.make_async_remote_copy(src_ref=mine, dst_ref=land.at[me], send_sem=send_sem, recv_sem=recv_sem,
                                               device_id=(px, py, pc), device_id_type=MESH)
            inc = pltpu.make_async_remote_copy(src_ref=mine, dst_ref=land.at[pidx], send_sem=send_sem, recv_sem=recv_sem,
                                               device_id=(px, py, pc), device_id_type=MESH)
            pairs.append((out, inc))
    return pairs


def exchange_start(srcs, lands, after, gather, name):
    n = len(srcs)

    def body(*refs):
        src_refs, land_refs = refs[:n], refs[n:2 * n]
        send_sem, recv_sem = refs[2 * n + 1], refs[2 * n + 2]
        token = refs[-1]
        for out, _ in _exchange_copies(src_refs, land_refs, send_sem, recv_sem, gather):
            out.start()
        token[...] = jnp.zeros_like(token)

    res = pl.pallas_call(
        body, name=name,
        out_shape=(pltpu.SemaphoreType.DMA(()), pltpu.SemaphoreType.DMA(()),
                   *[pltpu.HBM(a.shape, a.dtype) for a in srcs], *[pltpu.HBM(a.shape, a.dtype) for a in lands],
                   jax.ShapeDtypeStruct((8, LANE), F32)),
        in_specs=[_HBM] * (2 * n) + [pl.BlockSpec(memory_space=pl.ANY)],
        out_specs=(_SEM, _SEM, *[_HBM] * (2 * n), pl.BlockSpec(memory_space=pltpu.VMEM)),
        input_output_aliases={i: 2 + i for i in range(2 * n)},
        compiler_params=pltpu.CompilerParams(has_side_effects=_EFFECT),
    )(*[pltpu.with_memory_space_constraint(a, pltpu.HBM) for a in list(srcs) + list(lands)], after)
    return res[0], res[1], list(res[2:2 + n]), list(res[2 + n:2 + 2 * n]), res[-1]


def exchange_wait(send_sem, recv_sem, srcs, lands, after, gather, name):
    n = len(srcs)

    def body(*refs):
        src_refs, land_refs = refs[:n], refs[n:2 * n]
        s_sem, r_sem = refs[2 * n], refs[2 * n + 1]
        for out, inc in _exchange_copies(src_refs, land_refs, s_sem, r_sem, gather):
            out.wait_send()
            inc.wait_recv()

    res = pl.pallas_call(
        body, name=name,
        out_shape=[pltpu.HBM(a.shape, a.dtype) for a in list(srcs) + list(lands)],
        in_specs=[_HBM] * (2 * n) + [_SEM, _SEM, pl.BlockSpec(memory_space=pl.ANY)],
        out_specs=[_HBM] * (2 * n),
        input_output_aliases={i: i for i in range(2 * n)},
        compiler_params=pltpu.CompilerParams(has_side_effects=_EFFECT),
    )(*srcs, *lands, send_sem, recv_sem, after)
    return list(res[n:])


def matmul(pairs, mode, out_dtype, name, *, a_scale=None, out_scale=None, resid=None,
           save_acc=False, tm=512, tn=512, tk=2048):
    a0, b0 = pairs[0]
    if mode == "nn":
        (m, kdim), n = a0.shape, b0.shape[1]
    elif mode == "nt":
        (m, kdim), n = a0.shape, b0.shape[0]
    else:
        (kdim, m), n = a0.shape, b0.shape[1]
    tm, tn = _tile(m, tm, 8 if m % LANE else LANE), _tile(n, tn, LANE)
    tk = _tile(kdim, tk, LANE)
    nk = kdim // tk
    sub = _tile(tn, 256, LANE)
    npairs = len(pairs)
    dims = {"nn": ((1,), (0,)), "nt": ((1,), (1,)), "tn": ((0,), (0,))}[mode]

    if mode == "nn":
        a_spec = pl.BlockSpec((tm, tk), lambda i, j, k: (i, k))
        b_spec = pl.BlockSpec((tk, tn), lambda i, j, k: (k, j))
    elif mode == "nt":
        a_spec = pl.BlockSpec((tm, tk), lambda i, j, k: (i, k))
        b_spec = pl.BlockSpec((tn, tk), lambda i, j, k: (j, k))
    else:
        a_spec = pl.BlockSpec((tk, tm), lambda i, j, k: (k, i))
        b_spec = pl.BlockSpec((tk, tn), lambda i, j, k: (k, j))

    def body(*refs):
        it = iter(refs)
        pair_refs = [(next(it), next(it)) for _ in range(npairs)]
        as_ref = next(it) if a_scale is not None else None
        os_ref = next(it) if out_scale is not None else None
        rs_ref = next(it) if resid is not None else None
        o_ref = next(it)
        acc_out = next(it) if save_acc else None
        acc_ref = next(it) if nk > 1 else None
        lhs = []
        for a_ref, _ in pair_refs:
            a = a_ref[...]
            if as_ref is not None:
                a = a.astype(F32) * as_ref[...]
            lhs.append(a.astype(CD))

        def product(cols):
            part = None
            for a, (_, b_ref) in zip(lhs, pair_refs):
                b = b_ref[...] if cols is None else (b_ref[cols, :] if mode == "nt" else b_ref[:, cols])
                d = _dot(a, b, dims)
                part = d if part is None else part + d
            return part

        def finish(acc, cols=slice(None)):
            if acc_out is not None:
                acc_out[:, cols] = acc
            if os_ref is not None:
                acc = acc * os_ref[:, cols]
            if rs_ref is not None:
                acc = rs_ref[:, cols] + acc
            o_ref[:, cols] = acc.astype(o_ref.dtype)

        if nk == 1:
            for c0 in range(0, tn, sub):
                finish(product(slice(c0, c0 + sub)), slice(c0, c0 + sub))
        else:
            part = product(None)
            k = pl.program_id(2)

            @pl.when(k == 0)
            def _():
                acc_ref[...] = part

            @pl.when(k > 0)
            def _():
                acc_ref[...] += part

            @pl.when(k == nk - 1)
            def _():
                finish(acc_ref[...])

    in_specs, args = [], []
    for a, b in pairs:
        in_specs += [a_spec, b_spec]
        args += [a, b]
    if a_scale is not None:
        assert mode != "tn"
        in_specs.append(pl.BlockSpec((1, tk), lambda i, j, k: (0, k)))
        args.append(a_scale)
    if out_scale is not None:
        in_specs.append(pl.BlockSpec((1, tn), lambda i, j, k: (0, j)))
        args.append(out_scale)
    if resid is not None:
        in_specs.append(pl.BlockSpec((tm, tn), lambda i, j, k: (i, j)))
        args.append(resid)
    o_spec = pl.BlockSpec((tm, tn), lambda i, j, k: (i, j))
    out_shape = [jax.ShapeDtypeStruct((m, n), out_dtype)]
    out_specs = [o_spec]
    if save_acc:
        out_shape.append(jax.ShapeDtypeStruct((m, n), F32))
        out_specs.append(o_spec)
    scratch = [pltpu.VMEM((tm, tn), F32)] if nk > 1 else []
    res = pl.pallas_call(
        body, name=name, grid=(m // tm, n // tn, nk),
        in_specs=in_specs, out_specs=out_specs, out_shape=out_shape, scratch_shapes=scratch,
        compiler_params=_params(("parallel", "parallel", "arbitrary")),
    )(*args)
    return res if save_acc else res[0]


def _nm_fn(x, g, sc, sh):
    r = lax.rsqrt(jnp.mean(x * x, axis=-1, keepdims=True) + EPS)
    return (x * r * g) * (1.0 + sc) + sh


def norm_mod(x, g, sc, sh, name):
    t, d = x.shape
    tr = _tile(t, 256, 8)

    def body(x_ref, g_ref, sc_ref, sh_ref, h_ref):
        h_ref[...] = _nm_fn(x_ref[...], g_ref[...], sc_ref[...], sh_ref[...]).astype(h_ref.dtype)

    row = pl.BlockSpec((tr, d), lambda i: (i, 0))
    vec = pl.BlockSpec((1, d), lambda i: (0, 0))
    return pl.pallas_call(
        body, name=name, grid=(t // tr,), in_specs=[row, vec, vec, vec], out_specs=row,
        out_shape=jax.ShapeDtypeStruct((t, d), CD), compiler_params=_params(("parallel",)),
    )(x, g, sc, sh)


def norm_mod_bwd(x, g, sc, sh, dh, dxo, f, gate_scale, name):
    t, d = x.shape
    tr = _tile(t, 256, 8)

    def body(x_ref, g_ref, sc_ref, sh_ref, dh_ref, dxo_ref, f_ref, dx_ref, dg_ref, dsc_ref, dsh_ref, dgt_ref):
        _, vjp = jax.vjp(_nm_fn, x_ref[...], g_ref[...], sc_ref[...], sh_ref[...])
        dx, dg, dsc, dsh = vjp(dh_ref[...])
        dxo_v = dxo_ref[...]
        dx_ref[...] = dxo_v + dx
        dgt = gate_scale * jnp.sum(f_ref[...] * dxo_v, axis=0, keepdims=True)

        @pl.when(pl.program_id(0) == 0)
        def _():
            dg_ref[...] = dg
            dsc_ref[...] = dsc
            dsh_ref[...] = dsh
            dgt_ref[...] = dgt

        @pl.when(pl.program_id(0) > 0)
        def _():
            dg_ref[...] += dg
            dsc_ref[...] += dsc
            dsh_ref[...] += dsh
            dgt_ref[...] += dgt

    row = pl.BlockSpec((tr, d), lambda i: (i, 0))
    vec = pl.BlockSpec((1, d), lambda i: (0, 0))
    vshape = jax.ShapeDtypeStruct((1, d), F32)
    return pl.pallas_call(
        body, name=name, grid=(t // tr,), in_specs=[row, vec, vec, vec, row, row, row],
        out_specs=[row, vec, vec, vec, vec],
        out_shape=[jax.ShapeDtypeStruct((t, d), F32), vshape, vshape, vshape, vshape],
        compiler_params=_params(("arbitrary",)),
    )(x, g, sc, sh, dh, dxo, f)


HALF = N_DEV // 2


def ffn_up(h, u_all, name):
    t, d = h.shape
    cp = u_all.shape[1]
    f = HALF * cp
    tm, tn = _tile(t, 1024, LANE), cp
    per = cp // tn
    sub = _tile(tn, 256, LANE)

    def body(h_ref, wg_ref, wu_ref, g_ref, u_ref, a_ref):
        hv = h_ref[...]
        for c0 in range(0, tn, sub):
            cols = slice(c0, c0 + sub)
            gate = _nt(hv, wg_ref[0, cols, :])
            up = _nt(hv, wu_ref[0, cols, :])
            g_ref[:, cols] = gate.astype(g_ref.dtype)
            u_ref[:, cols] = up.astype(u_ref.dtype)
            a_ref[:, cols] = (_silu(gate) * up).astype(a_ref.dtype)

    o = pl.BlockSpec((tm, tn), lambda i, j: (i, j))
    wg = pl.BlockSpec((1, tn, d), lambda i, j: (j // per, j % per, 0))
    wu = pl.BlockSpec((1, tn, d), lambda i, j: (HALF + j // per, j % per, 0))
    return pl.pallas_call(
        body, name=name, grid=(t // tm, f // tn),
        in_specs=[pl.BlockSpec((tm, d), lambda i, j: (i, 0)), wg, wu], out_specs=[o, o, o],
        out_shape=[jax.ShapeDtypeStruct((t, f), CD)] * 3,
        compiler_params=_params(("parallel", "parallel")),
    )(h, u_all, u_all)


def ffn_up_wg(h, dgate, dup, cp, name):
    t, d = h.shape
    tn = _tile(d, 512, LANE)

    def body(h_ref, dg_ref, du_ref, o_ref):
        s = pl.program_id(0)

        @pl.when(s < HALF)
        def _():
            o_ref[0] = _tn(dg_ref[...], h_ref[...]).astype(o_ref.dtype)

        @pl.when(s >= HALF)
        def _():
            o_ref[0] = _tn(du_ref[...], h_ref[...]).astype(o_ref.dtype)

    return pl.pallas_call(
        body, name=name, grid=(N_DEV, d // tn),
        in_specs=[pl.BlockSpec((t, tn), lambda s, i: (0, i)),
                  pl.BlockSpec((t, cp), lambda s, i: (0, jnp.minimum(s, HALF - 1))),
                  pl.BlockSpec((t, cp), lambda s, i: (0, jnp.maximum(s - HALF, 0)))],
        out_specs=pl.BlockSpec((1, cp, tn), lambda s, i: (s, 0, i)),
        out_shape=jax.ShapeDtypeStruct((N_DEV, cp, d), CD),
        compiler_params=_params(("parallel", "parallel")),
    )(h, dgate, dup)


def ffn_up_dg(dgate, dup, u_all, name):
    t, f = dgate.shape
    cp, d = u_all.shape[1], u_all.shape[2]
    tm, tn = _tile(t, 1024, LANE), _tile(d, 512, LANE)

    def body(dg_ref, du_ref, u_ref, o_ref):
        wg = u_ref[0:HALF].reshape(f, tn)
        wu = u_ref[HALF:N_DEV].reshape(f, tn)
        o_ref[...] = _nn(dg_ref[...], wg) + _nn(du_ref[...], wu)

    return pl.pallas_call(
        body, name=name, grid=(t // tm, d // tn),
        in_specs=[pl.BlockSpec((tm, f), lambda i, j: (i, 0)), pl.BlockSpec((tm, f), lambda i, j: (i, 0)),
                  pl.BlockSpec((N_DEV, cp, tn), lambda i, j: (0, 0, j))],
        out_specs=pl.BlockSpec((tm, tn), lambda i, j: (i, j)),
        out_shape=jax.ShapeDtypeStruct((t, d), F32),
        compiler_params=_params(("parallel", "parallel")),
    )(dgate, dup, u_all)


def ffn_dact(dxo, s, wd, gate, up, name):
    t, d = dxo.shape
    f = wd.shape[0]
    tm, tn = _tile(t, 1024, LANE), _tile(f, 768, LANE)
    sub = _tile(tn, 256, LANE)

    def body(dxo_ref, s_ref, wd_ref, g_ref, u_ref, dg_ref, du_ref):
        dxs = (dxo_ref[...] * s_ref[...]).astype(CD)
        for c0 in range(0, tn, sub):
            cols = slice(c0, c0 + sub)
            da = _nt(dxs, wd_ref[cols, :])
            gate, up = g_ref[:, cols].astype(F32), u_ref[:, cols].astype(F32)
            sg = jax.nn.sigmoid(gate)
            dg_ref[:, cols] = (da * up * sg * (1.0 + gate * (1.0 - sg))).astype(dg_ref.dtype)
            du_ref[:, cols] = (da * gate * sg).astype(du_ref.dtype)

    o = pl.BlockSpec((tm, tn), lambda i, j: (i, j))
    return pl.pallas_call(
        body, name=name, grid=(t // tm, f // tn),
        in_specs=[pl.BlockSpec((tm, d), lambda i, j: (i, 0)), pl.BlockSpec((1, d), lambda i, j: (0, 0)),
                  pl.BlockSpec((tn, d), lambda i, j: (j, 0)), o, o],
        out_specs=[o, o],
        out_shape=[jax.ShapeDtypeStruct((t, f), CD), jax.ShapeDtypeStruct((t, f), CD)],
        compiler_params=_params(("parallel", "parallel")),
    )(dxo, s, wd, gate, up)


def _bdot(a, b, ca, cb, precision=None):
    if precision is None:
        a, b = a.astype(CD), b.astype(CD)
    return lax.dot_general(a, b, (((ca,), (cb,)), ((0,), (0,))), precision=precision, preferred_element_type=F32)


def _bnn(a, b, precision=None):
    return _bdot(a, b, 2, 1, precision)


def _bnt(a, b, precision=None):
    return _bdot(a, b, 2, 2, precision)


def _btn(a, b, precision=None):
    return _bdot(a, b, 1, 1, precision)


def _att_fn(q, kp, kc, vp, vc, qg, kg, mask_p, mask_c):
    b, w = q.shape
    nh = w // ATT_HEAD_DIM
    head_of_lane = lax.broadcasted_iota(jnp.int32, (nh, 1, w), 2) // ATT_HEAD_DIM
    hm = (head_of_lane == lax.broadcasted_iota(jnp.int32, (nh, 1, w), 0)).astype(F32)

    def rn(x, g):
        ss = jnp.sum((x * x)[None] * hm, axis=-1, keepdims=True)
        r = jnp.sum(lax.rsqrt(ss * (1.0 / ATT_HEAD_DIM) + EPS) * hm, axis=0)
        return x * r * g

    qn, kpn, kcn = rn(q, qg), rn(kp, kg), rn(kc, kg)
    q4 = (qn[None] * hm).reshape(nh * b, w)
    scale = ATT_HEAD_DIM ** -0.5
    sp = jnp.where(mask_p, _nt(q4, kpn) * scale, NEG)
    sc = jnp.where(mask_c, _nt(q4, kcn) * scale, NEG)
    m = jnp.maximum(jnp.max(sp, axis=-1, keepdims=True), jnp.max(sc, axis=-1, keepdims=True))
    pp, pc = jnp.exp(sp - m), jnp.exp(sc - m)
    den = jnp.sum(pp, axis=-1, keepdims=True) + jnp.sum(pc, axis=-1, keepdims=True)
    o4 = _nn(pp / den, vp) + _nn(pc / den, vc)
    o = jnp.sum(o4.reshape(nh, b, w) * hm, axis=0)
    lse = jnp.sum((m + jnp.log(den)).reshape(nh, b, 1) * hm, axis=0)
    return o, lse


def _att_masks(g, j, nb_total, nh):
    nb = jnp.int32(nb_total // DILATIONS[0])
    for gi in range(1, len(DILATIONS)):
        nb = jnp.where(g == gi, jnp.int32(nb_total // DILATIONS[gi]), nb)
    has_prev = (j % nb) != 0
    row = lax.broadcasted_iota(jnp.int32, (nh * ATT_BLOCK, ATT_BLOCK), 0) % ATT_BLOCK
    col = lax.broadcasted_iota(jnp.int32, (nh * ATT_BLOCK, ATT_BLOCK), 1)
    mask_p = jnp.logical_and(col >= row, has_prev)
    mask_c = col <= row
    return mask_p, mask_c


def _att_specs(w):
    blk = (1, 1, ATT_BLOCK, w)
    q = pl.BlockSpec(blk, lambda g, j: (0, g, j, 0))
    kp = pl.BlockSpec(blk, lambda g, j: (1, g, jnp.maximum(j - 1, 0), 0))
    kc = pl.BlockSpec(blk, lambda g, j: (1, g, j, 0))
    vp = pl.BlockSpec(blk, lambda g, j: (2, g, jnp.maximum(j - 1, 0), 0))
    vc = pl.BlockSpec(blk, lambda g, j: (2, g, j, 0))
    gain = pl.BlockSpec((1, w), lambda g, j: (0, 0))
    out = pl.BlockSpec((1, ATT_BLOCK, w), lambda g, j: (g, j, 0))
    return [q, kp, kc, vp, vc, gain, gain], out


def att_fwd(qkv, qg, kg, name):
    _, ng, t, w = qkv.shape
    nbt = t // ATT_BLOCK
    in_specs, out = _att_specs(w)

    def body(q_ref, kp_ref, kc_ref, vp_ref, vc_ref, qg_ref, kg_ref, o_ref, lse_ref):
        mask_p, mask_c = _att_masks(pl.program_id(0), pl.program_id(1), nbt, w // ATT_HEAD_DIM)
        o, lse = _att_fn(q_ref[0, 0], kp_ref[0, 0], kc_ref[0, 0], vp_ref[0, 0], vc_ref[0, 0],
                         qg_ref[...], kg_ref[...], mask_p, mask_c)
        o_ref[0] = o
        lse_ref[0] = lse

    sh = jax.ShapeDtypeStruct((ng, t, w), F32)
    return pl.pallas_call(
        body, name=name, grid=(ng, nbt), in_specs=in_specs, out_specs=[out, out], out_shape=[sh, sh],
        compiler_params=_params(("parallel", "parallel")),
    )(qkv, qkv, qkv, qkv, qkv, qg, kg)


def att_bwd(qkv, qg, kg, do, dlse, name):
    _, ng, t, w = qkv.shape
    nbt = t // ATT_BLOCK
    in_specs, out = _att_specs(w)
    whole = pl.BlockSpec((1, t, w), lambda g, j: (g, 0, 0))
    gain = in_specs[-1]

    def body(q_ref, kp_ref, kc_ref, vp_ref, vc_ref, qg_ref, kg_ref, do_ref, dlse_ref,
             dq_ref, dk_ref, dv_ref, dqg_ref, dkg_ref):
        g, j = pl.program_id(0), pl.program_id(1)
        mask_p, mask_c = _att_masks(g, j, nbt, w // ATT_HEAD_DIM)

        @pl.when(j == 0)
        def _():
            dk_ref[...] = jnp.zeros_like(dk_ref)
            dv_ref[...] = jnp.zeros_like(dv_ref)

        @pl.when(jnp.logical_and(g == 0, j == 0))
        def _():
            dqg_ref[...] = jnp.zeros_like(dqg_ref)
            dkg_ref[...] = jnp.zeros_like(dkg_ref)

        rows_c = pl.ds(pl.multiple_of(j * ATT_BLOCK, ATT_BLOCK), ATT_BLOCK)
        rows_p = pl.ds(pl.multiple_of(jnp.maximum(j - 1, 0) * ATT_BLOCK, ATT_BLOCK), ATT_BLOCK)
        fn = functools.partial(_att_fn, mask_p=mask_p, mask_c=mask_c)
        _, vjp = jax.vjp(fn, q_ref[0, 0], kp_ref[0, 0], kc_ref[0, 0], vp_ref[0, 0], vc_ref[0, 0],
                         qg_ref[...], kg_ref[...])
        dq, dkp, dkc, dvp, dvc, dqg, dkg = vjp((do_ref[0], dlse_ref[0]))
        dq_ref[0] = dq
        dk_ref[0, rows_p, :] += dkp
        dv_ref[0, rows_p, :] += dvp
        dk_ref[0, rows_c, :] += dkc
        dv_ref[0, rows_c, :] += dvc
        dqg_ref[...] += dqg
        dkg_ref[...] += dkg

    sh = jax.ShapeDtypeStruct((ng, t, w), F32)
    gshape = jax.ShapeDtypeStruct((1, w), F32)
    return pl.pallas_call(
        body, name=name, grid=(ng, nbt), in_specs=in_specs + [out, out],
        out_specs=[out, whole, whole, gain, gain], out_shape=[sh, sh, sh, gshape, gshape],
        compiler_params=_params(("arbitrary", "arbitrary")),
    )(qkv, qkv, qkv, qkv, qkv, qg, kg, do, dlse)


def _combine_fn(o, lse):
    m = jnp.max(lse, axis=0, keepdims=True)
    e = jnp.exp(lse - m)
    w = e / jnp.sum(e, axis=0, keepdims=True)
    return jnp.sum(w * o, axis=0)


def att_combine(o, lse, name):
    ng, t, w = o.shape
    tr = _tile(t, 256, 16)
    spec = pl.BlockSpec((ng, tr, w), lambda i: (0, i, 0))
    y_spec = pl.BlockSpec((tr, w), lambda i: (i, 0))

    def body(o_ref, l_ref, y_ref):
        y_ref[...] = _combine_fn(o_ref[...], l_ref[...]).astype(y_ref.dtype)

    return pl.pallas_call(
        body, name=name, grid=(t // tr,), in_specs=[spec, spec], out_specs=y_spec,
        out_shape=jax.ShapeDtypeStruct((t, w), CD), compiler_params=_params(("parallel",)),
    )(o, lse)


def att_combine_bwd(o, lse, dy, name):
    ng, t, w = o.shape
    tr = _tile(t, 256, 8)
    spec = pl.BlockSpec((ng, tr, w), lambda i: (0, i, 0))
    y_spec = pl.BlockSpec((tr, w), lambda i: (i, 0))

    def body(o_ref, l_ref, dy_ref, do_ref, dl_ref):
        _, vjp = jax.vjp(_combine_fn, o_ref[...], l_ref[...])
        do, dl = vjp(dy_ref[...])
        do_ref[...] = do
        dl_ref[...] = dl

    sh = jax.ShapeDtypeStruct(o.shape, F32)
    return pl.pallas_call(
        body, name=name, grid=(t // tr,), in_specs=[spec, spec, y_spec], out_specs=[spec, spec],
        out_shape=[sh, sh], compiler_params=_params(("parallel",)),
    )(o, lse, dy)


def _shift_down(x, s):
    if s == 0:
        return x
    row = lax.broadcasted_iota(jnp.int32, x.shape, 0)
    return jnp.where(row >= s, pltpu.roll(x, s, 0), 0.0)


def _shift_up(x, s):
    if s == 0:
        return x
    t = x.shape[0]
    row = lax.broadcasted_iota(jnp.int32, x.shape, 0)
    return jnp.where(row < t - s, pltpu.roll(x, t - s, 0), 0.0)


def conv_fwd(z, col0, w, name):
    t = z.shape[0]
    c = w.shape[1]
    nblk0 = col0 // LANE

    def body(z_ref, w_ref, c_ref):
        zv = z_ref[...]
        acc = None
        for i in range(CONV_WIDTH):
            term = _shift_down(zv, CONV_WIDTH - 1 - i) * w_ref[i:i + 1, :]
            acc = term if acc is None else acc + term
        c_ref[...] = acc

    return pl.pallas_call(
        body, name=name, grid=(c // LANE,),
        in_specs=[pl.BlockSpec((t, LANE), lambda j: (0, nblk0 + j)), pl.BlockSpec((CONV_WIDTH, LANE), lambda j: (0, j))],
        out_specs=pl.BlockSpec((t, LANE), lambda j: (0, j)),
        out_shape=jax.ShapeDtypeStruct((t, c), F32), compiler_params=_params(("parallel",)),
    )(z, w)


def conv_bwd(dc, z, col0, w, name):
    t = z.shape[0]
    c = w.shape[1]
    nblk0 = col0 // LANE

    def body(dc_ref, z_ref, w_ref, dz_ref, dw_ref):
        dcv, zv = dc_ref[...], z_ref[...]
        acc = None
        for i in range(CONV_WIDTH):
            s = CONV_WIDTH - 1 - i
            term = _shift_up(dcv, s) * w_ref[i:i + 1, :]
            acc = term if acc is None else acc + term
            dw_ref[i:i + 1, :] = jnp.sum(dcv * _shift_down(zv, s), axis=0, keepdims=True)
        dz_ref[...] = acc.astype(dz_ref.dtype)

    blk = pl.BlockSpec((t, LANE), lambda j: (0, j))
    wblk = pl.BlockSpec((CONV_WIDTH, LANE), lambda j: (0, j))
    return pl.pallas_call(
        body, name=name, grid=(c // LANE,),
        in_specs=[blk, pl.BlockSpec((t, LANE), lambda j: (0, nblk0 + j)), wblk], out_specs=[blk, wblk],
        out_shape=[jax.ShapeDtypeStruct((t, c), CD), jax.ShapeDtypeStruct((CONV_WIDTH, c), F32)],
        compiler_params=_params(("parallel",)),
    )(dc, z, w)


def _dn_consts():
    c = DN_CHUNK
    row = lax.broadcasted_iota(jnp.int32, (c, c), 0)
    col = lax.broadcasted_iota(jnp.int32, (c, c), 1)
    return dict(tril=row >= col, strict=row > col, eye=(row == col).astype(F32),
                tril_f=(row >= col).astype(F32), triu_f=(row <= col).astype(F32))


def _softplus(x):
    return jnp.maximum(x, 0.0) + jnp.log(1.0 + jnp.exp(-jnp.abs(x)))


def _split(x):
    hi = x.astype(CD)
    return hi, (x - hi.astype(F32)).astype(CD)


def _bdot3(a, b, ca, cb):
    ah, al = _split(a)
    bh, bl = _split(b)
    return _bdot(ah, bh, ca, cb) + (_bdot(ah, bl, ca, cb) + _bdot(al, bh, ca, cb))


def _tri_inv_impl(a_mat):
    c = a_mat.shape[-1]
    eye = (lax.broadcasted_iota(jnp.int32, (c, c), 0) == lax.broadcasted_iota(jnp.int32, (c, c), 1)).astype(F32)
    nk_ = -a_mat
    t_inv = eye + nk_
    for _ in range(c.bit_length() - 2):
        nk_ = _bdot3(nk_, nk_, 2, 1)
        t_inv = t_inv + _bdot3(t_inv, nk_, 2, 1)
    return t_inv


@jax.custom_vjp
def _tri_inv(a_mat):
    return _tri_inv_impl(a_mat)


def _tri_inv_fwd(a_mat):
    t_inv = _tri_inv_impl(a_mat)
    return t_inv, t_inv


def _tri_inv_bwd(t_inv, dt_inv):
    return (-_bdot3(_bdot3(t_inv, dt_inv, 1, 1), t_inv, 2, 2),)


_tri_inv.defvjp(_tri_inv_fwd, _tri_inv_bwd)


def _dn_chunk(cq, ck, cv, og, a_col, b_col, al, dt, gn, s_prev, *, k, inv):
    q = _silu(cq)
    q = q * lax.rsqrt(jnp.sum(q * q, axis=-1, keepdims=True) + EPS) * (DN_HEAD_DIM ** -0.5)
    kk = _silu(ck)
    kk = kk * lax.rsqrt(jnp.sum(kk * kk, axis=-1, keepdims=True) + EPS)
    v = _silu(cv)
    g = -jnp.exp(al) * _softplus(a_col + dt)
    beta = jax.nn.sigmoid(b_col)
    g_row = jnp.sum(k["eye"] * g, axis=1, keepdims=True)
    gc_col = jnp.sum(k["tril_f"] * g_row, axis=2, keepdims=True)
    gc_row = jnp.sum(k["triu_f"] * g, axis=1, keepdims=True)
    ldec = jnp.where(k["tril"], jnp.exp(jnp.where(k["tril"], gc_col - gc_row, 0.0)), 0.0)
    kb, vb = kk * beta, v * beta
    a_mat = jnp.where(k["strict"], _bnt(kb, kk) * ldec, 0.0)
    t_inv = inv(a_mat)
    egc = jnp.exp(gc_col)
    u = _bnn(t_inv, vb)
    w = _bnn(t_inv, kb * egc)
    attn = jnp.where(k["tril"], _bnt(q, kk) * ldec, 0.0)
    gc_last = jnp.sum(g, axis=1, keepdims=True)
    k_dec = kk * jnp.exp(gc_last - gc_col)
    v_new = u - _bnn(w, s_prev)
    o = _bnn(q * egc, s_prev) + _bnn(attn, v_new)
    s_new = s_prev * jnp.exp(gc_last) + _btn(k_dec, v_new)
    y = o * lax.rsqrt(jnp.mean(o * o, axis=-1, keepdims=True) + EPS) * gn * _silu(og)
    return y, s_new


def _dn_heads(ref, nh):
    hd = DN_HEAD_DIM
    return jnp.stack([ref[:, h * hd:(h + 1) * hd] for h in range(nh)])


def _dn_specs(nh, col_gate, col_ab, order):
    hd, c = DN_HEAD_DIM, DN_CHUNK
    w = nh * hd
    qs = pl.BlockSpec((c, w), lambda n: (order(n), 0))
    ks = pl.BlockSpec((c, w), lambda n: (order(n), 1))
    vs = pl.BlockSpec((c, w), lambda n: (order(n), 2))
    gs = pl.BlockSpec((c, w), lambda n: (order(n), col_gate // w))
    ab = pl.BlockSpec((c, LANE), lambda n: (order(n), col_ab // LANE))
    scal = pl.BlockSpec((nh, 1, 1), lambda n: (0, 0, 0))
    gn = pl.BlockSpec((1, hd), lambda n: (0, 0))
    st = pl.BlockSpec((1, nh, hd, hd), lambda n: (order(n), 0, 0, 0))
    return qs, ks, vs, gs, ab, scal, gn, st


def _lane_pick(x, idx):
    lane = lax.broadcasted_iota(jnp.int32, x.shape, 1)
    return jnp.sum(jnp.where(lane == idx, x, 0.0), axis=1, keepdims=True)


def dn_fwd(cv, z, col_gate, col_ab, a_log, dt_bias, gn, name):
    t = cv.shape[0]
    nh = a_log.shape[0]
    hd, c = DN_HEAD_DIM, DN_CHUNK
    n_chunks = t // c
    qs, ks, vs, gs, ab, scal, gnspec, st = _dn_specs(nh, col_gate, col_ab, lambda n: n)

    def body(q_ref, k_ref, v_ref, g_ref, ab_ref, al_ref, dt_ref, gn_ref, y_ref, st_ref, s_scr):
        @pl.when(pl.program_id(0) == 0)
        def _():
            s_scr[...] = jnp.zeros_like(s_scr)

        abv = ab_ref[...]
        a_col = jnp.stack([_lane_pick(abv, h) for h in range(nh)])
        b_col = jnp.stack([_lane_pick(abv, nh + h) for h in range(nh)])
        s_prev = s_scr[...]
        st_ref[0] = s_prev
        y, s_new = _dn_chunk(_dn_heads(q_ref, nh), _dn_heads(k_ref, nh), _dn_heads(v_ref, nh), _dn_heads(g_ref, nh),
                             a_col, b_col, al_ref[...], dt_ref[...], gn_ref[...], s_prev,
                             k=_dn_consts(), inv=_tri_inv_impl)
        for h in range(nh):
            y_ref[:, h * hd:(h + 1) * hd] = y[h].astype(y_ref.dtype)
        s_scr[...] = s_new

    return pl.pallas_call(
        body, name=name, grid=(n_chunks,),
        in_specs=[qs, ks, vs, gs, ab, scal, scal, gnspec],
        out_specs=[pl.BlockSpec((c, nh * hd), lambda n: (n, 0)), st],
        out_shape=[jax.ShapeDtypeStruct((t, nh * hd), CD), jax.ShapeDtypeStruct((n_chunks, nh, hd, hd), F32)],
        scratch_shapes=[pltpu.VMEM((nh, hd, hd), F32)],
        compiler_params=_params(("arbitrary",)),
    )(cv, cv, cv, z, z, a_log, dt_bias, gn)


def dn_bwd(cv, z, col_gate, col_ab, a_log, dt_bias, gn, states, dy, name):
    t = cv.shape[0]
    nh = a_log.shape[0]
    hd, c = DN_HEAD_DIM, DN_CHUNK
    w = nh * hd
    n_chunks = t // c
    rev = lambda n: n_chunks - 1 - n
    qs, ks, vs, gs, ab, scal, gnspec, st = _dn_specs(nh, col_gate, col_ab, rev)
    yspec = pl.BlockSpec((c, w), lambda n: (rev(n), 0))

    def body(q_ref, k_ref, v_ref, g_ref, ab_ref, al_ref, dt_ref, gn_ref, st_ref, dy_ref,
             dc_ref, dg_ref, dab_ref, dal_ref, ddt_ref, dgn_ref, ds_scr):
        @pl.when(pl.program_id(0) == 0)
        def _():
            ds_scr[...] = jnp.zeros_like(ds_scr)
            dal_ref[...] = jnp.zeros_like(dal_ref)
            ddt_ref[...] = jnp.zeros_like(ddt_ref)
            dgn_ref[...] = jnp.zeros_like(dgn_ref)

        abv = ab_ref[...]
        a_col = jnp.stack([_lane_pick(abv, h) for h in range(nh)])
        b_col = jnp.stack([_lane_pick(abv, nh + h) for h in range(nh)])
        fn = functools.partial(_dn_chunk, k=_dn_consts(), inv=_tri_inv)
        _, vjp = jax.vjp(fn, _dn_heads(q_ref, nh), _dn_heads(k_ref, nh), _dn_heads(v_ref, nh), _dn_heads(g_ref, nh),
                         a_col, b_col, al_ref[...], dt_ref[...], gn_ref[...], st_ref[0])
        dq, dk, dv, dg, da, db, dal, ddt, dgn, ds = vjp((_dn_heads(dy_ref, nh), ds_scr[...]))
        lane = lax.broadcasted_iota(jnp.int32, (c, LANE), 1)
        dab = jnp.zeros((c, LANE), F32)
        for h in range(nh):
            cols = slice(h * hd, (h + 1) * hd)
            dc_ref[:, cols] = dq[h]
            dc_ref[:, w + h * hd:w + (h + 1) * hd] = dk[h]
            dc_ref[:, 2 * w + h * hd:2 * w + (h + 1) * hd] = dv[h]
            dg_ref[:, cols] = dg[h].astype(dg_ref.dtype)
            dab = dab + jnp.where(lane == h, da[h], 0.0) + jnp.where(lane == nh + h, db[h], 0.0)
        dab_ref[...] = dab.astype(dab_ref.dtype)
        dal_ref[...] += dal
        ddt_ref[...] += ddt
        dgn_ref[...] += dgn
        ds_scr[...] = ds

    sshape = jax.ShapeDtypeStruct((nh, 1, 1), F32)
    res = pl.pallas_call(
        body, name=name, grid=(n_chunks,),
        in_specs=[qs, ks, vs, gs, ab, scal, scal, gnspec, st, yspec],
        out_specs=[pl.BlockSpec((c, 3 * w), lambda n: (rev(n), 0)), yspec,
                   pl.BlockSpec((c, LANE), lambda n: (rev(n), 0)), scal, scal, gnspec],
        out_shape=[jax.ShapeDtypeStruct((t, 3 * w), F32), jax.ShapeDtypeStruct((t, w), CD),
                   jax.ShapeDtypeStruct((t, LANE), CD), sshape, sshape, jax.ShapeDtypeStruct((1, hd), F32)],
        scratch_shapes=[pltpu.VMEM((nh, hd, hd), F32)],
        compiler_params=_params(("arbitrary",)),
    )(cv, cv, cv, z, z, a_log, dt_bias, gn, states, dy)
    return res


def merge_fwd(ya, wpa, yd, wpd, z, col_m, name):
    t, d = yd.shape[0], wpd.shape[1]
    tm, tn = _tile(t, 512, LANE), _tile(d, 256, LANE)
    nb1, nb2 = col_m // tn, (col_m + d) // tn

    def body(ya_ref, wpa_ref, yd_ref, wpd_ref, z1_ref, z2_ref, m_ref, pa_ref, pd_ref):
        pa = _nn(ya_ref[...], wpa_ref[...])
        pd = _nn(yd_ref[...], wpd_ref[...])
        pa_ref[...] = pa
        pd_ref[...] = pd
        m_ref[...] = (jax.nn.sigmoid(z1_ref[...]) * pa + jax.nn.sigmoid(z2_ref[...]) * pd).astype(m_ref.dtype)

    o = pl.BlockSpec((tm, tn), lambda i, j: (i, j))
    return pl.pallas_call(
        body, name=name, grid=(t // tm, d // tn),
        in_specs=[pl.BlockSpec((tm, ya.shape[1]), lambda i, j: (i, 0)),
                  pl.BlockSpec((wpa.shape[0], tn), lambda i, j: (0, j)),
                  pl.BlockSpec((tm, yd.shape[1]), lambda i, j: (i, 0)),
                  pl.BlockSpec((wpd.shape[0], tn), lambda i, j: (0, j)),
                  pl.BlockSpec((tm, tn), lambda i, j: (i, nb1 + j)),
                  pl.BlockSpec((tm, tn), lambda i, j: (i, nb2 + j))],
        out_specs=[o, o, o],
        out_shape=[jax.ShapeDtypeStruct((t, d), CD), jax.ShapeDtypeStruct((t, d), F32),
                   jax.ShapeDtypeStruct((t, d), F32)],
        compiler_params=_params(("parallel", "parallel")),
    )(ya, wpa, yd, wpd, z, z)


def merge_bwd(dm, pa, pd, z, col_m, name):
    t, d = dm.shape
    tm, tn = _tile(t, 512, 8), _tile(d, 256, LANE)
    nb1, nb2 = col_m // tn, (col_m + d) // tn

    def body(dm_ref, pa_ref, pd_ref, z1_ref, z2_ref, dpa_ref, dpd_ref, dz1_ref, dz2_ref):
        dmv = dm_ref[...]
        s1, s2 = jax.nn.sigmoid(z1_ref[...]), jax.nn.sigmoid(z2_ref[...])
        dpa_ref[...] = (dmv * s1).astype(dpa_ref.dtype)
        dpd_ref[...] = (dmv * s2).astype(dpd_ref.dtype)
        dz1_ref[...] = (dmv * pa_ref[...] * s1 * (1.0 - s1)).astype(dz1_ref.dtype)
        dz2_ref[...] = (dmv * pd_ref[...] * s2 * (1.0 - s2)).astype(dz2_ref.dtype)

    o = pl.BlockSpec((tm, tn), lambda i, j: (i, j))
    sh = jax.ShapeDtypeStruct((t, d), CD)
    return pl.pallas_call(
        body, name=name, grid=(t // tm, d // tn),
        in_specs=[o, o, o, pl.BlockSpec((tm, tn), lambda i, j: (i, nb1 + j)),
                  pl.BlockSpec((tm, tn), lambda i, j: (i, nb2 + j))],
        out_specs=[o, o, o, o], out_shape=[sh, sh, sh, sh],
        compiler_params=_params(("parallel", "parallel")),
    )(dm, pa, pd, z, z)


def ada_fwd(c_all, w, b, name):
    nl, d, n = w.shape
    tn = _tile(n, 384, LANE)

    def body(c_ref, w_ref, b_ref, o_ref):
        o_ref[0] = _nn(_silu(c_ref[...]), w_ref[0]) + b_ref[0]

    return pl.pallas_call(
        body, name=name, grid=(nl, n // tn),
        in_specs=[pl.BlockSpec(c_all.shape, lambda l, j: (0, 0)), pl.BlockSpec((1, d, tn), lambda l, j: (l, 0, j)),
                  pl.BlockSpec((1, 1, tn), lambda l, j: (l, 0, j))],
        out_specs=pl.BlockSpec((1, c_all.shape[0], tn), lambda l, j: (l, 0, j)),
        out_shape=jax.ShapeDtypeStruct((nl, c_all.shape[0], n), F32),
        compiler_params=_params(("parallel", "parallel")),
    )(c_all, w, b)


def ada_bwd(c_pad, dmod_pad, name):
    nl, kp, n = dmod_pad.shape
    d = c_pad.shape[1]
    tn = _tile(n, 384, LANE)

    def body(c_ref, g_ref, o_ref):
        o_ref[0] = _tn(_silu(c_ref[...]), g_ref[0])

    return pl.pallas_call(
        body, name=name, grid=(nl, n // tn),
        in_specs=[pl.BlockSpec((kp, d), lambda l, j: (0, 0)), pl.BlockSpec((1, kp, tn), lambda l, j: (l, 0, j))],
        out_specs=pl.BlockSpec((1, d, tn), lambda l, j: (l, 0, j)),
        out_shape=jax.ShapeDtypeStruct((nl, d, n), F32),
        compiler_params=_params(("parallel", "parallel")),
    )(c_pad, dmod_pad)


def loss_head(y, target, name):
    t, d = y.shape
    tr = _tile(t, 256, 8)

    def body(y_ref, t_ref, dy_ref, l_ref):
        err = y_ref[...] - t_ref[...]
        dy_ref[...] = err * (1.0 / d)
        part = jnp.sum(jnp.sum(err * err, axis=1, keepdims=True), axis=0, keepdims=True) * (0.5 / d)

        @pl.when(pl.program_id(0) == 0)
        def _():
            l_ref[...] = jnp.zeros_like(l_ref)

        l_ref[...] += part

    row = pl.BlockSpec((tr, d), lambda i: (i, 0))
    return pl.pallas_call(
        body, name=name, grid=(t // tr,), in_specs=[row, row],
        out_specs=[row, pl.BlockSpec((8, LANE), lambda i: (0, 0))],
        out_shape=[jax.ShapeDtypeStruct((t, d), F32), jax.ShapeDtypeStruct((8, LANE), F32)],
        compiler_params=_params(("arbitrary",)),
    )(y, target)


def _adamw_update(g, w_ref, m_ref, v_ref, go_ref, d_ref, mo_ref, vo_ref):
    m_new = ADAM_B1 * m_ref[...] + (1.0 - ADAM_B1) * g
    v_new = ADAM_B2 * v_ref[...] + (1.0 - ADAM_B2) * jnp.square(g)
    m_hat = m_new / (1.0 - ADAM_B1 ** ADAM_STEP)
    v_hat = v_new / (1.0 - ADAM_B2 ** ADAM_STEP)
    go_ref[...] = g
    d_ref[...] = -ADAM_LR * (m_hat / (jnp.sqrt(v_hat) + ADAM_EPS) + ADAM_WD * w_ref[...])
    mo_ref[...] = m_new
    vo_ref[...] = v_new


def adamw(w, m, v, g_slots, name):
    shape = w.shape
    nslot = g_slots.shape[0]
    if w.ndim == 2:
        w3, m3, v3, g4 = w[None], m[None], v[None], g_slots[:, None]
    else:
        w3, m3, v3, g4 = w, m, v, g_slots
    nl, r, c = w3.shape
    tr = _tile(r, 256, 16)

    def body(w_ref, m_ref, v_ref, g_ref, *outs):
        g = g_ref[0].astype(F32)
        for s in range(1, nslot):
            g = g + g_ref[s].astype(F32)
        _adamw_update(g, w_ref, m_ref, v_ref, *outs)

    blk = pl.BlockSpec((1, tr, c), lambda l, i: (l, i, 0))
    gblk = pl.BlockSpec((nslot, 1, tr, c), lambda l, i: (0, l, i, 0))
    sh = jax.ShapeDtypeStruct(w3.shape, F32)
    outs = pl.pallas_call(
        body, name=name, grid=(nl, r // tr), in_specs=[blk, blk, blk, gblk], out_specs=[blk] * 4,
        out_shape=[sh] * 4, compiler_params=_params(("parallel", "parallel")),
    )(w3, m3, v3, g4)
    return tuple(o.reshape(shape) for o in outs)


def adamw_layers(w, m, v, g_layers, first, name, prev=None):
    _, r, c = w.shape
    n = len(g_layers)
    nslot, _, cg = g_layers[0].shape
    tr = _tile(r, 256, 16)

    def body(w_ref, m_ref, v_ref, *rest):
        g_refs, outs = rest[:n], rest[-4:]
        for li in range(n):
            @pl.when(pl.program_id(0) == li)
            def _(g_ref=g_refs[li]):
                g = g_ref[0, :, :c].astype(F32)
                for s in range(1, nslot):
                    g = g + g_ref[s, :, :c].astype(F32)
                _adamw_update(g[None], w_ref, m_ref, v_ref, *outs)

    blk = pl.BlockSpec((1, tr, c), lambda l, i: (first + l, i, 0))
    gblks = [pl.BlockSpec((nslot, tr, cg), lambda l, i, li=li: (0, jnp.where(l == li, i, 0), 0)) for li in range(n)]
    sh = jax.ShapeDtypeStruct(w.shape, F32)
    extra = [] if prev is None else list(prev)
    return tuple(pl.pallas_call(
        body, name=name, grid=(n, r // tr),
        in_specs=[blk, blk, blk] + gblks + [pl.BlockSpec(memory_space=pl.ANY)] * len(extra),
        out_specs=[blk] * 4, out_shape=[sh] * 4,
        input_output_aliases={3 + n + k: k for k in range(len(extra))},
        compiler_params=_params(("arbitrary", "arbitrary")),
    )(w, m, v, *g_layers, *extra))


def _rows(flat, unit=16):
    n = flat.shape[0]
    per = 1024 * unit
    pad = (-n) % per
    if pad:
        flat = jnp.concatenate([flat, jnp.zeros((pad,), flat.dtype)])
    return flat.reshape(-1, 1024)


def _to_classes(a):
    t, w = a.shape[1], a.shape[2]
    return jnp.stack([a[gi].reshape(t // dil, dil, w).transpose(1, 0, 2).reshape(t, w)
                      for gi, dil in enumerate(DILATIONS)])


def _from_classes(a):
    t, w = a.shape[1], a.shape[2]
    return jnp.stack([a[gi].reshape(dil, t // dil, w).transpose(1, 0, 2).reshape(t, w)
                      for gi, dil in enumerate(DILATIONS)])


def _unshard_cols(g):
    return g.transpose(1, 0, 2).reshape(g.shape[1], -1)


def _shard_cols(full):
    r = full.shape[0]
    return full.reshape(r, N_DEV, -1).transpose(1, 0, 2)


def kernel(x, c, ada_w, ada_b, norm_ff1, ffn1_w_up, ffn1_w_down, norm_mix, w_in, q_norm, k_norm, conv_w, a_log, dt_bias, dn_norm, w_proj_att, w_proj_dn, w_out, norm_ff2, ffn2_w_up, ffn2_w_down, loss_target, m_ada_w, m_ada_b, m_norm_ff1, m_ffn1_w_up, m_ffn1_w_down, m_norm_mix, m_w_in, m_q_norm, m_k_norm, m_conv_w, m_a_log, m_dt_bias, m_dn_norm, m_w_proj_att, m_w_proj_dn, m_w_out, m_norm_ff2, m_ffn2_w_up, m_ffn2_w_down, v_ada_w, v_ada_b, v_norm_ff1, v_ffn1_w_up, v_ffn1_w_down, v_norm_mix, v_w_in, v_q_norm, v_k_norm, v_conv_w, v_a_log, v_dt_bias, v_dn_norm, v_w_proj_att, v_w_proj_dn, v_w_out, v_norm_ff2, v_ffn2_w_up, v_ffn2_w_down):
    nl = ada_w.shape[0]
    t, d = x.shape[1], x.shape[2]
    dff = ffn1_w_down.shape[1] * N_DEV
    ha = d // 256
    ng = len(DILATIONS)
    wa = ha * ATT_HEAD_DIM
    att_w = ng * wa
    nh = d // DN_HEAD_DIM
    dn_w = nh * DN_HEAD_DIM
    n_in = w_in.shape[2] * N_DEV
    off_dn, off_gate = 3 * att_w, 3 * att_w + 3 * dn_w
    off_a = off_gate + dn_w
    off_merge = off_a + 2 * nh
    assert off_merge + 2 * d == n_in
    col_dn, col_gate, col_m = 0, 3 * dn_w, 4 * dn_w
    col_att = col_m + 2 * d
    col_ab = col_att + 3 * att_w
    zw = col_ab + 2 * LANE
    me = 4 * lax.axis_index("x") + 2 * lax.axis_index("y") + lax.axis_index("c")
    xs = x[0]
    target = loss_target[0]

    conv_rows = _rows(conv_w.reshape(-1), 8)
    pack0 = jnp.concatenate([jnp.concatenate([c, jnp.zeros((7, d), F32)]).reshape(-1), conv_rows.reshape(-1)])
    pack0 = _rows(pack0, 8)
    g0 = all_gather([pack0], "ag_c_conv")[0].reshape(N_DEV, -1)
    c_all = g0[:, :d]
    cw = g0[:, 8 * d:8 * d + conv_w.size].reshape(N_DEV, nl, CONV_WIDTH, -1)
    conv_full = cw.transpose(1, 2, 0, 3).reshape(nl, CONV_WIDTH, 3 * dn_w)

    n_ada = ada_w.shape[2]
    b_mine = lax.dynamic_slice_in_dim(ada_b, me * n_ada, n_ada, axis=1)[:, None, :]
    mod_s = ada_fwd(c_all, ada_w, b_mine, "ada_fwd")
    gm = all_gather([mod_s], "ag_mod")[0]
    mod = lax.dynamic_index_in_dim(gm, me, axis=2, keepdims=False)
    mod = mod.transpose(1, 0, 2).reshape(nl, N_ADA, 1, d)

    kinds = [ffn1_w_up, ffn1_w_down, w_in, w_proj_att, w_proj_dn, w_out, ffn2_w_up, ffn2_w_down]
    c_up = ffn1_w_up.shape[2]
    cp = -(-c_up // LANE) * LANE
    r_dn = ffn1_w_down.shape[1]
    assert 2 * r_dn == c_up

    up1_t, up2_t = jnp.swapaxes(ffn1_w_up, 1, 2), jnp.swapaxes(ffn2_w_up, 1, 2)
    w_in_t = jnp.transpose(w_in, (2, 0, 1))

    def layer_shards(l):
        def pad_up(wt):
            return jnp.pad(wt.astype(CD), ((0, cp - c_up), (0, 0)))
        return [pad_up(up1_t[l]), ffn1_w_down[l].astype(CD), w_in_t[:, l].astype(CD), w_proj_att[l].astype(CD),
                w_proj_dn[l].astype(CD), w_out[l].astype(CD), pad_up(up2_t[l]), ffn2_w_down[l].astype(CD)]

    def gather_landing(shards):
        return [lax.dynamic_update_index_in_dim(lax.empty((N_DEV,) + s.shape, s.dtype), s, me, 0) for s in shards]

    def slot_landing(arrs):
        return [lax.dynamic_update_index_in_dim(lax.empty(a.shape, a.dtype),
                                                lax.dynamic_index_in_dim(a, me, 0, keepdims=False), me, 0) for a in arrs]

    gathered = all_gather(layer_shards(0), "ag_weights0")
    prefetch = None

    def full_weight(ki, l):
        blk = gathered[ki]
        if ki == 3:
            return _unshard_cols(blk)
        if ki == 2:
            return w_in_rows(blk.reshape(n_in, d))
        if ki in (1, 7):
            pairs = blk.reshape(HALF, c_up, d)
            return jnp.pad(pairs, ((0, 0), (0, cp - c_up), (0, 0))).reshape(HALF * cp, d)
        return blk.reshape(-1, blk.shape[2])

    def down_grad_slots(g):
        return g.reshape(HALF, cp, d)[:, :c_up].reshape(N_DEV, r_dn, d)

    def w_in_rows(wt):
        pad = jnp.zeros((zw - n_in, wt.shape[1]), wt.dtype)
        return jnp.concatenate([wt[off_dn:off_a], wt[off_merge:], wt[:off_dn], wt[off_a:off_merge], pad], axis=0)

    def w_in_cols_inv(g):
        return jnp.concatenate([g[:, col_att:col_ab], g[:, :col_m], g[:, col_ab:col_ab + 2 * nh], g[:, col_m:col_att]], axis=1)

    saved = []
    xc = xs
    mods = []
    for l in range(nl):
        sv = {}
        if prefetch is not None:
            gathered = exchange_wait(*prefetch[:4], xc, True, f"ag_wait{l}")
        mod_l = mod[l]
        if l + 1 < nl:
            shards = layer_shards(l + 1)
            behind = gathered[0][0, :1, :1].astype(F32) + mod[0, 0, :, :1]
            prefetch = exchange_start(shards, gather_landing(shards), behind, True, f"ag_start{l + 1}")
            mod_l = mod_l + prefetch[4][0, 0]
        mods.append(mod_l)
        sh1, sc1, gt1, sh2, sc2, gt2, sh3, sc3, gt3 = [mod_l[i] for i in range(N_ADA)]
        w_dn1, w_dn2 = full_weight(1, l), full_weight(7, l)
        win = full_weight(2, l)
        wpa, wpd, wo = full_weight(3, l), full_weight(4, l), full_weight(5, l)
        sv["w"] = (gathered[0], gathered[6], w_dn1, w_dn2, win, wpa, wpd, wo)

        def ffn(xin, g, sh, sc, gt, u_all, w_dn):
            h = norm_mod(xin, g, sc, sh, "norm_mod")
            gate, up, a = ffn_up(h, u_all, "ffn_up")
            xo, f = matmul([(a, w_dn)], "nn", F32, "ffn_down", out_scale=0.5 * gt, resid=xin, save_acc=True,
                           tm=1024, tk=4096)
            return xo, (xin, h, gate, up, a, f)

        xc, sv["ffn1"] = ffn(xc, norm_ff1[l:l + 1], sh1, sc1, gt1, gathered[0], w_dn1)

        x_mix = xc
        h2 = norm_mod(x_mix, norm_mix[l:l + 1], sc2, sh2, "norm_mod")
        z = matmul([(h2, win)], "nt", F32, "w_in", tm=1024, tn=512)
        z_att = z[:, col_att:col_ab].reshape(t, 3, ng, wa).transpose(1, 2, 0, 3)
        qkv = jnp.stack([_to_classes(z_att[i]) for i in range(3)])
        qg, kg = jnp.tile(q_norm[l:l + 1], (1, ha)), jnp.tile(k_norm[l:l + 1], (1, ha))
        o_cls, lse_cls = att_fwd(qkv, qg, kg, "att_fwd")
        o_tok, lse_tok = _from_classes(o_cls), _from_classes(lse_cls)
        ya = att_combine(o_tok, lse_tok, "att_combine")
        cvo = conv_fwd(z, col_dn, conv_full[l], "conv_fwd")
        al3, dt3 = a_log[l].reshape(nh, 1, 1), dt_bias[l].reshape(nh, 1, 1)
        gn = dn_norm[l:l + 1]
        yd, states = dn_fwd(cvo, z, col_gate, col_ab, al3, dt3, gn, "dn_fwd")
        mrg, pa, pd = merge_fwd(ya, wpa, yd, wpd, z, col_m, "merge_fwd")
        xc, f2 = matmul([(mrg, wo)], "nn", F32, "w_out", out_scale=gt2, resid=x_mix, save_acc=True)
        sv["mix"] = (x_mix, h2, z, qkv, o_tok, lse_tok, ya, cvo, yd, states, mrg, pa, pd, f2)

        xc, sv["ffn2"] = ffn(xc, norm_ff2[l:l + 1], sh3, sc3, gt3, gathered[6], w_dn2)
        saved.append(sv)

    dxc, loss_blk = loss_head(xc, target, "loss_head")

    def slots(ki, g):
        if ki in (0, 1, 6, 7):
            return g
        return _shard_cols(g) if ki in (2, 3) else g.reshape(N_DEV, -1, g.shape[1])

    gbig = [None] * len(kinds)
    dmods, small, pending = [], [], []
    token = None
    for l in reversed(range(nl)):
        sv = saved[l]
        mod_l = mods[l] if token is None else mods[l] + token[0, 0]
        sh1, sc1, gt1, sh2, sc2, gt2, sh3, sc3, gt3 = [mod_l[i] for i in range(N_ADA)]
        u_all1, u_all2, w_dn1, w_dn2, win, wpa, wpd, wo = sv["w"]

        def ffn_bwd(dxo, g, sh, sc, gt, u_all, w_dn, sv_f):
            xin, h, gate, up, a, f = sv_f
            s = 0.5 * gt
            g_dn = down_grad_slots(matmul([(a, dxo)], "tn", CD, "ffn_down_wg", out_scale=s, tm=1024, tn=1024))
            dgate, dup = ffn_dact(dxo, s, w_dn, gate, up, "ffn_dact")
            g_up = ffn_up_wg(h, dgate, dup, cp, "ffn_up_wg")
            dh = ffn_up_dg(dgate, dup, u_all, "ffn_up_dg")
            dx, dg, dsc, dsh, dgt = norm_mod_bwd(xin, g, sc, sh, dh, dxo, f, 0.5, "norm_mod_bwd")
            return dx, g_up, g_dn, (dg, dsc, dsh, dgt)

        def start_exchange(ids, tag, behind):
            send = [slots(ki, gbig[ki]) for ki in ids]
            started = exchange_start(send, slot_landing(send), behind, False, f"a2a_start{l}{tag}")
            pending.append((l, ids, tag, started))
            return started[4]

        dxc, gbig[6], gbig[7], (dg3, dsc3, dsh3, dgt3) = ffn_bwd(
            dxc, norm_ff2[l:l + 1], sh3, sc3, gt3, u_all2, w_dn2, sv["ffn2"])
        token = start_exchange([6, 7], "f2", dxc)
        sh2, sc2, gt2 = (vec + token[0, 0] for vec in (sh2, sc2, gt2))

        x_mix, h2, z, qkv, o_tok, lse_tok, ya, cvo, yd, states, mrg, pa, pd, f2 = sv["mix"]
        gbig[5] = matmul([(mrg, dxc)], "tn", CD, "w_out_wg", out_scale=gt2)
        dm = matmul([(dxc, wo)], "nt", F32, "w_out_dg", a_scale=gt2)
        dpa, dpd, dz1, dz2 = merge_bwd(dm, pa, pd, z, col_m, "merge_bwd")
        gbig[3] = matmul([(ya, dpa)], "tn", CD, "patt_wg")
        gbig[4] = matmul([(yd, dpd)], "tn", CD, "pdn_wg")
        dya = matmul([(dpa, wpa)], "nt", F32, "patt_dg")
        dyd = matmul([(dpd, wpd)], "nt", F32, "pdn_dg")
        do_tok, dlse_tok = att_combine_bwd(o_tok, lse_tok, dya, "att_combine_bwd")
        qg, kg = jnp.tile(q_norm[l:l + 1], (1, ha)), jnp.tile(k_norm[l:l + 1], (1, ha))
        dq, dk, dv, dqg, dkg = att_bwd(qkv, qg, kg, _to_classes(do_tok), _to_classes(dlse_tok), "att_bwd")
        dqg, dkg = (jnp.sum(v_.reshape(ha, ATT_HEAD_DIM), axis=0, keepdims=True) for v_ in (dqg, dkg))
        dz_att = jnp.stack([_from_classes(a_) for a_ in (dq, dk, dv)])
        dz_att = dz_att.transpose(2, 0, 1, 3).reshape(t, 3 * att_w).astype(CD)
        al3, dt3 = a_log[l].reshape(nh, 1, 1), dt_bias[l].reshape(nh, 1, 1)
        gn = dn_norm[l:l + 1]
        dcvo, dz_gate, dz_ab, dal, ddt, dgn = dn_bwd(
            cvo, z, col_gate, col_ab, al3, dt3, gn, states, dyd, "dn_bwd")
        dz_dn, dconv = conv_bwd(dcvo, z, col_dn, conv_full[l], "conv_bwd")
        dz = jnp.concatenate([dz_dn, dz_gate, dz1, dz2, dz_att, dz_ab, jnp.zeros((t, LANE), CD)], axis=1)
        gbig[2] = w_in_cols_inv(matmul([(h2, dz)], "tn", CD, "w_in_wg", tm=1024, tn=1024))
        dh2 = matmul([(dz, win)], "nn", F32, "w_in_dg", tm=1024, tn=1024,
                     tk=zw // 4 if (zw // 4) % LANE == 0 else zw)
        dxc, dg2, dsc2, dsh2, dgt2 = norm_mod_bwd(x_mix, norm_mix[l:l + 1], sc2, sh2, dh2, dxc, f2, 1.0, "norm_mod_bwd_mix")

        token = start_exchange([2, 3, 4, 5], "mx", dxc)
        sh1, sc1, gt1 = (vec + token[0, 0] for vec in (sh1, sc1, gt1))
        dxc, gbig[0], gbig[1], (dg1, dsc1, dsh1, dgt1) = ffn_bwd(
            dxc, norm_ff1[l:l + 1], sh1, sc1, gt1, u_all1, w_dn1, sv["ffn1"])
        if l > 0:
            token = start_exchange([0, 1], "f1", dxc)
        dmods.append(jnp.concatenate([dsh1, dsc1, dgt1, dsh2, dsc2, dgt2, dsh3, dsc3, dgt3], axis=1))
        small.append((dg1, dg2, dg3, dconv, dqg, dkg, dal.reshape(1, nh), ddt.reshape(1, nh), dgn))
    dmods.reverse()
    small.reverse()

    big_names = ["ffn1_w_up", "ffn1_w_down", "w_in", "w_proj_att", "w_proj_dn", "w_out", "ffn2_w_up", "ffn2_w_down"]
    big_m = [m_ffn1_w_up, m_ffn1_w_down, m_w_in, m_w_proj_att, m_w_proj_dn, m_w_out, m_ffn2_w_up, m_ffn2_w_down]
    big_v = [v_ffn1_w_up, v_ffn1_w_down, v_w_in, v_w_proj_att, v_w_proj_dn, v_w_out, v_ffn2_w_up, v_ffn2_w_down]
    recv_layers = [[None] * len(kinds) for _ in range(nl)]

    def wait_exchanges(layers, behind):
        for pl_, ids, tag, st in pending:
            if pl_ in layers:
                got = exchange_wait(*st[:4], behind, False, f"a2a_wait{pl_}{tag}")
                for ki, arr in zip(ids, got):
                    recv_layers[pl_][ki] = arr

    def big_adamw(ki, layers, prev, tag):
        g_layers = [recv_layers[li][ki] for li in layers]
        wmv = (kinds[ki], big_m[ki], big_v[ki])
        if ki in (0, 6):
            wmv = tuple(jnp.swapaxes(a, 1, 2) for a in wmv)
        return adamw_layers(*wmv, g_layers, layers[0], f"adamw_{big_names[ki]}{tag}", prev)

    later = list(range(1, nl))
    partial = [None] * len(kinds)
    if later:
        wait_exchanges(later, dxc)
        partial = [big_adamw(ki, later, None, "_l1") for ki in range(len(kinds))]

    fields = [jnp.stack(dmods).reshape(-1)]
    fields += [jnp.stack([s[i] for s in small]).reshape(-1) for i in range(9)]
    fields.append(loss_blk[0, :1])
    if later:
        fields.append(partial[-1][1][1, :1, 0] * 0.0)
    fsizes = [f.size for f in fields]
    foffs = [sum(fsizes[:i]) for i in range(len(fields))]
    g1 = all_gather([_rows(jnp.concatenate(fields), 8)], "ag_small")[0].reshape(N_DEV, -1)
    l = 0
    start_exchange([0, 1], "f1", g1)

    def field(i, shape):
        return g1[:, foffs[i]:foffs[i] + fsizes[i]].reshape(N_DEV, *shape)

    loss = field(10, (1,))[0, 0]
    for j in range(1, N_DEV):
        loss = loss + field(10, (1,))[j, 0]

    results = {}
    dmod_all = field(0, (nl, N_ADA * d))
    c_pad = jnp.concatenate([c_all, jnp.zeros((8, d), F32)])
    dmod_mine = lax.dynamic_slice_in_dim(dmod_all, me * n_ada, n_ada, axis=2).transpose(1, 0, 2)
    dmod_pad = jnp.concatenate([dmod_mine, jnp.zeros((nl, 8, n_ada), F32)], axis=1)
    g_ada_w = ada_bwd(c_pad, dmod_pad, "ada_bwd")
    results["ada_w"] = adamw(ada_w, m_ada_w, v_ada_w, g_ada_w[None], "adamw_ada_w")
    results["ada_b"] = adamw(ada_b, m_ada_b, v_ada_b, dmod_all, "adamw_ada_b")
    results["norm_ff1"] = adamw(norm_ff1, m_norm_ff1, v_norm_ff1, field(1, (nl, d)), "adamw_norm_ff1")
    results["norm_mix"] = adamw(norm_mix, m_norm_mix, v_norm_mix, field(2, (nl, d)), "adamw_norm_mix")
    results["norm_ff2"] = adamw(norm_ff2, m_norm_ff2, v_norm_ff2, field(3, (nl, d)), "adamw_norm_ff2")
    conv_slots = lax.dynamic_slice_in_dim(field(4, (nl, CONV_WIDTH, 3 * dn_w)), me * conv_w.shape[2],
                                          conv_w.shape[2], axis=3)
    results["conv_w"] = adamw(conv_w, m_conv_w, v_conv_w, conv_slots, "adamw_conv_w")
    results["q_norm"] = adamw(q_norm, m_q_norm, v_q_norm, field(5, (nl, ATT_HEAD_DIM)), "adamw_q_norm")
    results["k_norm"] = adamw(k_norm, m_k_norm, v_k_norm, field(6, (nl, ATT_HEAD_DIM)), "adamw_k_norm")
    results["a_log"] = adamw(a_log, m_a_log, v_a_log, field(7, (nl, nh)), "adamw_a_log")
    results["dt_bias"] = adamw(dt_bias, m_dt_bias, v_dt_bias, field(8, (nl, nh)), "adamw_dt_bias")
    results["dn_norm"] = adamw(dn_norm, m_dn_norm, v_dn_norm, field(9, (nl, DN_HEAD_DIM)), "adamw_dn_norm")
    wait_exchanges([0], results["ada_w"][1])
    for ki, nm in enumerate(big_names):
        res = big_adamw(ki, [0], partial[ki], "_l0")
        results[nm] = tuple(jnp.swapaxes(r, 1, 2) for r in res) if ki in (0, 6) else res

    order = ["ada_w", "ada_b", "norm_ff1", "ffn1_w_up", "ffn1_w_down", "norm_mix", "w_in", "q_norm", "k_norm",
             "conv_w", "a_log", "dt_bias", "dn_norm", "w_proj_att", "w_proj_dn", "w_out", "norm_ff2",
             "ffn2_w_up", "ffn2_w_down"]
    outs = [loss, dxc[None]]
    for part in range(4):
        outs += [results[n][part] for n in order]
    return tuple(outs)
```

```python
import functools

import jax
import jax.numpy as jnp
from jax import lax
from jax.experimental import pallas as pl
from jax.experimental.pallas import tpu as pltpu

F32 = jnp.float32
CD = jnp.bfloat16
EPS = 1e-6
N_DEV = 8
LANE = 128
ATT_HEAD_DIM = 64
ATT_BLOCK = 128
DILATIONS = (1, 4, 16)
DN_HEAD_DIM = 128
DN_CHUNK = 64
CONV_WIDTH = 4
N_ADA = 9
ADAM_LR, ADAM_B1, ADAM_B2, ADAM_EPS, ADAM_WD, ADAM_STEP = 0.001, 0.9, 0.999, 1e-08, 0.01, 10
VMEM_LIMIT = 56 * 1024 * 1024
NEG = -1e30
MESH = pl.DeviceIdType.MESH
HI = lax.Precision.HIGHEST


def _params(sem=None):
    return pltpu.CompilerParams(dimension_semantics=sem, vmem_limit_bytes=VMEM_LIMIT)


def _tile(n, pref, unit):
    best = None
    t = unit
    while t <= min(n, pref):
        if n % t == 0:
            best = t
        t += unit
    return best if best is not None else n


def _silu(x):
    return x * jax.nn.sigmoid(x)


def _dot(a, b, dims, precision=None):
    if precision is None:
        a, b = a.astype(CD), b.astype(CD)
    return lax.dot_general(a, b, (dims, ((), ())), precision=precision, preferred_element_type=F32)


def _nn(a, b, precision=None):
    return _dot(a, b, ((1,), (0,)), precision)


def _nt(a, b, precision=None):
    return _dot(a, b, ((1,), (1,)), precision)


def _tn(a, b, precision=None):
    return _dot(a, b, ((0,), (0,)), precision)


def all_gather(arrs, name):
    n = len(arrs)

    def body(*refs):
        x_refs, out_refs = refs[:n], refs[n:2 * n]
        send_sems, recv_sems, local_sems = refs[2 * n:]
        x, y, c = lax.axis_index("x"), lax.axis_index("y"), lax.axis_index("c")
        me, sibling = (x, y, c), (x, y, 1 - c)
        chips = [(1 - x, y), (x, 1 - y), (1 - x, 1 - y)]

        def copy(a, k, block, to, src=None):
            slot = out_refs[a].at[4 * block[0] + 2 * block[1] + block[2]]
            return pltpu.make_async_remote_copy(
                src_ref=slot if src is None else src, dst_ref=slot,
                send_sem=send_sems.at[7 * a + k], recv_sem=recv_sems.at[7 * a + k],
                device_id=to, device_id_type=MESH)

        mine, first, passed = [], [], []
        for a in range(n):
            cp = pltpu.make_async_copy(x_refs[a], out_refs[a].at[4 * x + 2 * y + c], local_sems.at[a])
            cp.start()
            mine.append(cp)
            first.append(copy(a, 0, me, sibling, src=x_refs[a]))
            first += [copy(a, 1 + j, me, (*chip, c), src=x_refs[a]) for j, chip in enumerate(chips)]
        for cp in first:
            cp.start()
        for j, chip in enumerate(chips):
            for a in range(n):
                copy(a, 1 + j, (*chip, c), me).wait_recv()
                cp = copy(a, 4 + j, (*chip, c), sibling)
                cp.start()
                passed.append(cp)
        for a in range(n):
            copy(a, 0, sibling, me).wait_recv()
            for j, chip in enumerate(chips):
                copy(a, 4 + j, (*chip, 1 - c), me).wait_recv()
        for cp in first + passed:
            cp.wait_send()
        for cp in mine:
            cp.wait()

    hbm = pl.BlockSpec(memory_space=pl.ANY)
    outs = pl.pallas_call(
        body, name=name,
        out_shape=[jax.ShapeDtypeStruct((N_DEV,) + a.shape, a.dtype) for a in arrs],
        in_specs=[hbm] * n, out_specs=[hbm] * n,
        scratch_shapes=[pltpu.SemaphoreType.DMA((7 * n,)), pltpu.SemaphoreType.DMA((7 * n,)),
                        pltpu.SemaphoreType.DMA((n,))],
    )(*arrs)
    return list(outs)


_HBM = pl.BlockSpec(memory_space=pltpu.HBM)
_SEM = pl.BlockSpec(memory_space=pltpu.SEMAPHORE)
_EFFECT = pltpu.SideEffectType.DATAFLOW_SIDE_EFFECTING


def _exchange_copies(src_refs, land_refs, send_sem, recv_sem, gather):
    x, y, c = lax.axis_index("x"), lax.axis_index("y"), lax.axis_index("c")
    me = 4 * x + 2 * y + c
    pairs = []
    for k in range(1, N_DEV):
        px, py, pc = x ^ (k >> 2), y ^ ((k >> 1) & 1), c ^ (k & 1)
        pidx = 4 * px + 2 * py + pc
        for src, land in zip(src_refs, land_refs):
            mine = src if gather else src.at[pidx]
            out = pltpu.make_async_remote_copy(src_ref=mine, dst_ref=land.at[me], send_sem=send_sem, recv_sem=recv_sem,
                                               device_id=(px, py, pc), device_id_type=MESH)
            inc = pltpu.make_async_remote_copy(src_ref=mine, dst_ref=land.at[pidx], send_sem=send_sem, recv_sem=recv_sem,
                                               device_id=(px, py, pc), device_id_type=MESH)
            pairs.append((out, inc))
    return pairs


def exchange_start(srcs, lands, after, gather, name):
    n = len(srcs)

    def body(*refs):
        src_refs, land_refs = refs[:n], refs[n:2 * n]
        send_sem, recv_sem = refs[2 * n + 1], refs[2 * n + 2]
        token = refs[-1]
        for out, _ in _exchange_copies(src_refs, land_refs, send_sem, recv_sem, gather):
            out.start()
        token[...] = jnp.zeros_like(token)

    res = pl.pallas_call(
        body, name=name,
        out_shape=(pltpu.SemaphoreType.DMA(()), pltpu.SemaphoreType.DMA(()),
                   *[pltpu.HBM(a.shape, a.dtype) for a in srcs], *[pltpu.HBM(a.shape, a.dtype) for a in lands],
                   jax.ShapeDtypeStruct((8, LANE), F32)),
        in_specs=[_HBM] * (2 * n) + [pl.BlockSpec(memory_space=pl.ANY)],
        out_specs=(_SEM, _SEM, *[_HBM] * (2 * n), pl.BlockSpec(memory_space=pltpu.VMEM)),
        input_output_aliases={i: 2 + i for i in range(2 * n)},
        compiler_params=pltpu.CompilerParams(has_side_effects=_EFFECT),
    )(*[pltpu.with_memory_space_constraint(a, pltpu.HBM) for a in list(srcs) + list(lands)], after)
    return res[0], res[1], list(res[2:2 + n]), list(res[2 + n:2 + 2 * n]), res[-1]


def exchange_wait(send_sem, recv_sem, srcs, lands, after, gather, name):
    n = len(srcs)

    def body(*refs):
        src_refs, land_refs = refs[:n], refs[n:2 * n]
        s_sem, r_sem = refs[2 * n], refs[2 * n + 1]
        for out, inc in _exchange_copies(src_refs, land_refs, s_sem, r_sem, gather):
            out.wait_send()
            inc.wait_recv()

    res = pl.pallas_call(
        body, name=name,
        out_shape=[pltpu.HBM(a.shape, a.dtype) for a in list(srcs) + list(lands)],
        in_specs=[_HBM] * (2 * n) + [_SEM, _SEM, pl.BlockSpec(memory_space=pl.ANY)],
        out_specs=[_HBM] * (2 * n),
        input_output_aliases={i: i for i in range(2 * n)},
        compiler_params=pltpu.CompilerParams(has_side_effects=_EFFECT),
    )(*srcs, *lands, send_sem, recv_sem, after)
    return list(res[n:])


def matmul(pairs, mode, out_dtype, name, *, a_scale=None, out_scale=None, resid=None,
           save_acc=False, tm=512, tn=512, tk=2048):
    a0, b0 = pairs[0]
    if mode == "nn":
        (m, kdim), n = a0.shape, b0.shape[1]
    elif mode == "nt":
        (m, kdim), n = a0.shape, b0.shape[0]
    else:
        (kdim, m), n = a0.shape, b0.shape[1]
    tm, tn = _tile(m, tm, 8 if m % LANE else LANE), _tile(n, tn, LANE)
    tk = _tile(kdim, tk, LANE)
    nk = kdim // tk
    sub = tn if mode == "tn" else _tile(tn, 256, LANE)
    npairs = len(pairs)
    dims = {"nn": ((1,), (0,)), "nt": ((1,), (1,)), "tn": ((0,), (0,))}[mode]

    if mode == "nn":
        a_spec = pl.BlockSpec((tm, tk), lambda i, j, k: (i, k))
        b_spec = pl.BlockSpec((tk, tn), lambda i, j, k: (k, j))
    elif mode == "nt":
        a_spec = pl.BlockSpec((tm, tk), lambda i, j, k: (i, k))
        b_spec = pl.BlockSpec((tn, tk), lambda i, j, k: (j, k))
    else:
        a_spec = pl.BlockSpec((tk, tm), lambda i, j, k: (k, i))
        b_spec = pl.BlockSpec((tk, tn), lambda i, j, k: (k, j))

    def body(*refs):
        it = iter(refs)
        pair_refs = [(next(it), next(it)) for _ in range(npairs)]
        as_ref = next(it) if a_scale is not None else None
        os_ref = next(it) if out_scale is not None else None
        rs_ref = next(it) if resid is not None else None
        o_ref = next(it)
        acc_out = next(it) if save_acc else None
        acc_ref = next(it) if nk > 1 else None
        lhs = []
        for a_ref, _ in pair_refs:
            a = a_ref[...]
            if as_ref is not None:
                a = a.astype(F32) * as_ref[...]
            lhs.append(a.astype(CD))

        def product(cols):
            part = None
            for a, (_, b_ref) in zip(lhs, pair_refs):
                b = b_ref[...] if cols is None else (b_ref[cols, :] if mode == "nt" else b_ref[:, cols])
                d = _dot(a, b, dims)
                part = d if part is None else part + d
            return part

        def finish(acc, cols=slice(None)):
            if acc_out is not None:
                acc_out[:, cols] = acc
            if os_ref is not None:
                acc = acc * os_ref[:, cols]
            if rs_ref is not None:
                acc = rs_ref[:, cols] + acc
            o_ref[:, cols] = acc.astype(o_ref.dtype)

        if nk == 1:
            for c0 in range(0, tn, sub):
                finish(product(slice(c0, c0 + sub)), slice(c0, c0 + sub))
        else:
            part = product(None)
            k = pl.program_id(2)

            @pl.when(k == 0)
            def _():
                acc_ref[...] = part

            @pl.when(k > 0)
            def _():
                acc_ref[...] += part

            @pl.when(k == nk - 1)
            def _():
                finish(acc_ref[...])

    in_specs, args = [], []
    for a, b in pairs:
        in_specs += [a_spec, b_spec]
        args += [a, b]
    if a_scale is not None:
        assert mode != "tn"
        in_specs.append(pl.BlockSpec((1, tk), lambda i, j, k: (0, k)))
        args.append(a_scale)
    if out_scale is not None:
        in_specs.append(pl.BlockSpec((1, tn), lambda i, j, k: (0, j)))
        args.append(out_scale)
    if resid is not None:
        in_specs.append(pl.BlockSpec((tm, tn), lambda i, j, k: (i, j)))
        args.append(resid)
    o_spec = pl.BlockSpec((tm, tn), lambda i, j, k: (i, j))
    out_shape = [jax.ShapeDtypeStruct((m, n), out_dtype)]
    out_specs = [o_spec]
    if save_acc:
        out_shape.append(jax.ShapeDtypeStruct((m, n), F32))
        out_specs.append(o_spec)
    scratch = [pltpu.VMEM((tm, tn), F32)] if nk > 1 else []
    res = pl.pallas_call(
        body, name=name, grid=(m // tm, n // tn, nk),
        in_specs=in_specs, out_specs=out_specs, out_shape=out_shape, scratch_shapes=scratch,
        compiler_params=_params(("parallel", "parallel", "arbitrary")),
    )(*args)
    return res if save_acc else res[0]


def _nm_fn(x, g, sc, sh):
    r = lax.rsqrt(jnp.mean(x * x, axis=-1, keepdims=True) + EPS)
    return (x * r * g) * (1.0 + sc) + sh


def norm_mod(x, g, sc, sh, name):
    t, d = x.shape
    tr = _tile(t, 256, 8)

    def body(x_ref, g_ref, sc_ref, sh_ref, h_ref):
        h_ref[...] = _nm_fn(x_ref[...], g_ref[...], sc_ref[...], sh_ref[...]).astype(h_ref.dtype)

    row = pl.BlockSpec((tr, d), lambda i: (i, 0))
    vec = pl.BlockSpec((1, d), lambda i: (0, 0))
    return pl.pallas_call(
        body, name=name, grid=(t // tr,), in_specs=[row, vec, vec, vec], out_specs=row,
        out_shape=jax.ShapeDtypeStruct((t, d), CD), compiler_params=_params(("parallel",)),
    )(x, g, sc, sh)


def norm_mod_bwd(x, g, sc, sh, dh, dxo, f, gate_scale, name):
    t, d = x.shape
    tr = _tile(t, 256, 8)

    def body(x_ref, g_ref, sc_ref, sh_ref, dh_ref, dxo_ref, f_ref, dx_ref, dg_ref, dsc_ref, dsh_ref, dgt_ref):
        _, vjp = jax.vjp(_nm_fn, x_ref[...], g_ref[...], sc_ref[...], sh_ref[...])
        dx, dg, dsc, dsh = vjp(dh_ref[...])
        dxo_v = dxo_ref[...]
        dx_ref[...] = dxo_v + dx
        dgt = gate_scale * jnp.sum(f_ref[...] * dxo_v, axis=0, keepdims=True)

        @pl.when(pl.program_id(0) == 0)
        def _():
            dg_ref[...] = dg
            dsc_ref[...] = dsc
            dsh_ref[...] = dsh
            dgt_ref[...] = dgt

        @pl.when(pl.program_id(0) > 0)
        def _():
            dg_ref[...] += dg
            dsc_ref[...] += dsc
            dsh_ref[...] += dsh
            dgt_ref[...] += dgt

    row = pl.BlockSpec((tr, d), lambda i: (i, 0))
    vec = pl.BlockSpec((1, d), lambda i: (0, 0))
    vshape = jax.ShapeDtypeStruct((1, d), F32)
    return pl.pallas_call(
        body, name=name, grid=(t // tr,), in_specs=[row, vec, vec, vec, row, row, row],
        out_specs=[row, vec, vec, vec, vec],
        out_shape=[jax.ShapeDtypeStruct((t, d), F32), vshape, vshape, vshape, vshape],
        compiler_params=_params(("arbitrary",)),
    )(x, g, sc, sh, dh, dxo, f)


HALF = N_DEV // 2


def ffn_up(h, u_all, name):
    t, d = h.shape
    cp = u_all.shape[1]
    f = HALF * cp
    tm, tn = _tile(t, 1024, LANE), cp
    per = cp // tn
    sub = _tile(tn, 256, LANE)

    def body(h_ref, wg_ref, wu_ref, g_ref, u_ref, a_ref):
        hv = h_ref[...]
        for c0 in range(0, tn, sub):
            cols = slice(c0, c0 + sub)
            gate = _nt(hv, wg_ref[0, cols, :])
            up = _nt(hv, wu_ref[0, cols, :])
            g_ref[:, cols] = gate.astype(g_ref.dtype)
            u_ref[:, cols] = up.astype(u_ref.dtype)
            a_ref[:, cols] = (_silu(gate) * up).astype(a_ref.dtype)

    o = pl.BlockSpec((tm, tn), lambda i, j: (i, j))
    wg = pl.BlockSpec((1, tn, d), lambda i, j: (j // per, j % per, 0))
    wu = pl.BlockSpec((1, tn, d), lambda i, j: (HALF + j // per, j % per, 0))
    return pl.pallas_call(
        body, name=name, grid=(t // tm, f // tn),
        in_specs=[pl.BlockSpec((tm, d), lambda i, j: (i, 0)), wg, wu], out_specs=[o, o, o],
        out_shape=[jax.ShapeDtypeStruct((t, f), CD)] * 3,
        compiler_params=_params(("parallel", "parallel")),
    )(h, u_all, u_all)


def ffn_up_wg(h, dgate, dup, cp, name):
    t, d = h.shape
    tn = _tile(d, 1024, LANE)

    def body(h_ref, dg_ref, du_ref, o_ref):
        s = pl.program_id(0)

        @pl.when(s < HALF)
        def _():
            o_ref[0] = _tn(dg_ref[...], h_ref[...]).astype(o_ref.dtype)

        @pl.when(s >= HALF)
        def _():
            o_ref[0] = _tn(du_ref[...], h_ref[...]).astype(o_ref.dtype)

    return pl.pallas_call(
        body, name=name, grid=(N_DEV, d // tn),
        in_specs=[pl.BlockSpec((t, tn), lambda s, i: (0, i)),
                  pl.BlockSpec((t, cp), lambda s, i: (0, jnp.minimum(s, HALF - 1))),
                  pl.BlockSpec((t, cp), lambda s, i: (0, jnp.maximum(s - HALF, 0)))],
        out_specs=pl.BlockSpec((1, cp, tn), lambda s, i: (s, 0, i)),
        out_shape=jax.ShapeDtypeStruct((N_DEV, cp, d), CD),
        compiler_params=_params(("parallel", "parallel")),
    )(h, dgate, dup)


def ffn_up_dg(dgate, dup, u_all, name):
    t, f = dgate.shape
    cp, d = u_all.shape[1], u_all.shape[2]
    tm, tn = _tile(t, 1024, LANE), _tile(d, 512, LANE)

    def body(dg_ref, du_ref, u_ref, o_ref):
        wg = u_ref[0:HALF].reshape(f, tn)
        wu = u_ref[HALF:N_DEV].reshape(f, tn)
        o_ref[...] = _nn(dg_ref[...], wg) + _nn(du_ref[...], wu)

    return pl.pallas_call(
        body, name=name, grid=(t // tm, d // tn),
        in_specs=[pl.BlockSpec((tm, f), lambda i, j: (i, 0)), pl.BlockSpec((tm, f), lambda i, j: (i, 0)),
                  pl.BlockSpec((N_DEV, cp, tn), lambda i, j: (0, 0, j))],
        out_specs=pl.BlockSpec((tm, tn), lambda i, j: (i, j)),
        out_shape=jax.ShapeDtypeStruct((t, d), F32),
        compiler_params=_params(("parallel", "parallel")),
    )(dgate, dup, u_all)


def ffn_dact(dxo, s, wd, gate, up, name):
    t, d = dxo.shape
    f = wd.shape[0]
    tm, tn = _tile(t, 1024, LANE), _tile(f, 768, LANE)
    sub = _tile(tn, 256, LANE)

    def body(dxo_ref, s_ref, wd_ref, g_ref, u_ref, dg_ref, du_ref):
        dxs = (dxo_ref[...] * s_ref[...]).astype(CD)
        for c0 in range(0, tn, sub):
            cols = slice(c0, c0 + sub)
            da = _nt(dxs, wd_ref[cols, :])
            gate, up = g_ref[:, cols].astype(F32), u_ref[:, cols].astype(F32)
            sg = jax.nn.sigmoid(gate)
            dg_ref[:, cols] = (da * up * sg * (1.0 + gate * (1.0 - sg))).astype(dg_ref.dtype)
            du_ref[:, cols] = (da * gate * sg).astype(du_ref.dtype)

    o = pl.BlockSpec((tm, tn), lambda i, j: (i, j))
    return pl.pallas_call(
        body, name=name, grid=(t // tm, f // tn),
        in_specs=[pl.BlockSpec((tm, d), lambda i, j: (i, 0)), pl.BlockSpec((1, d), lambda i, j: (0, 0)),
                  pl.BlockSpec((tn, d), lambda i, j: (j, 0)), o, o],
        out_specs=[o, o],
        out_shape=[jax.ShapeDtypeStruct((t, f), CD), jax.ShapeDtypeStruct((t, f), CD)],
        compiler_params=_params(("parallel", "parallel")),
    )(dxo, s, wd, gate, up)


def _bdot(a, b, ca, cb, precision=None):
    if precision is None:
        a, b = a.astype(CD), b.astype(CD)
    return lax.dot_general(a, b, (((ca,), (cb,)), ((0,), (0,))), precision=precision, preferred_element_type=F32)


def _bnn(a, b, precision=None):
    return _bdot(a, b, 2, 1, precision)


def _bnt(a, b, precision=None):
    return _bdot(a, b, 2, 2, precision)


def _btn(a, b, precision=None):
    return _bdot(a, b, 1, 1, precision)


def _att_fn(q, kp, kc, vp, vc, qg, kg, mask_p, mask_c):
    b, w = q.shape
    nh = w // ATT_HEAD_DIM
    head_of_lane = lax.broadcasted_iota(jnp.int32, (nh, 1, w), 2) // ATT_HEAD_DIM
    hm = (head_of_lane == lax.broadcasted_iota(jnp.int32, (nh, 1, w), 0)).astype(F32)

    def rn(x, g):
        ss = jnp.sum((x * x)[None] * hm, axis=-1, keepdims=True)
        r = jnp.sum(lax.rsqrt(ss * (1.0 / ATT_HEAD_DIM) + EPS) * hm, axis=0)
        return x * r * g

    qn, kcn = rn(q, qg), rn(kc, kg)
    q4 = (qn[None] * hm).reshape(nh * b, w)
    scale = ATT_HEAD_DIM ** -0.5
    sc = jnp.where(mask_c, _nt(q4, kcn) * scale, NEG)
    if kp is None:
        m = jnp.max(sc, axis=-1, keepdims=True)
        pc = jnp.exp(sc - m)
        den = jnp.sum(pc, axis=-1, keepdims=True)
        o4 = _nn(pc / den, vc)
    else:
        sp = jnp.where(mask_p, _nt(q4, rn(kp, kg)) * scale, NEG)
        m = jnp.maximum(jnp.max(sp, axis=-1, keepdims=True), jnp.max(sc, axis=-1, keepdims=True))
        pp, pc = jnp.exp(sp - m), jnp.exp(sc - m)
        den = jnp.sum(pp, axis=-1, keepdims=True) + jnp.sum(pc, axis=-1, keepdims=True)
        o4 = _nn(pp / den, vp) + _nn(pc / den, vc)
    o = jnp.sum(o4.reshape(nh, b, w) * hm, axis=0)
    lse = jnp.sum((m + jnp.log(den)).reshape(nh, b, 1) * hm, axis=0)
    return o, lse


def _att_masks(g, j, nb_total, nh):
    nb = jnp.int32(nb_total // DILATIONS[0])
    for gi in range(1, len(DILATIONS)):
        nb = jnp.where(g == gi, jnp.int32(nb_total // DILATIONS[gi]), nb)
    has_prev = (j % nb) != 0
    row = lax.broadcasted_iota(jnp.int32, (nh * ATT_BLOCK, ATT_BLOCK), 0) % ATT_BLOCK
    col = lax.broadcasted_iota(jnp.int32, (nh * ATT_BLOCK, ATT_BLOCK), 1)
    return has_prev, col >= row, col <= row


def _att_specs(w):
    blk = (1, 1, ATT_BLOCK, w)
    q = pl.BlockSpec(blk, lambda g, j: (0, g, j, 0))
    kp = pl.BlockSpec(blk, lambda g, j: (1, g, jnp.maximum(j - 1, 0), 0))
    kc = pl.BlockSpec(blk, lambda g, j: (1, g, j, 0))
    vp = pl.BlockSpec(blk, lambda g, j: (2, g, jnp.maximum(j - 1, 0), 0))
    vc = pl.BlockSpec(blk, lambda g, j: (2, g, j, 0))
    gain = pl.BlockSpec((1, w), lambda g, j: (0, 0))
    out = pl.BlockSpec((1, ATT_BLOCK, w), lambda g, j: (g, j, 0))
    return [q, kp, kc, vp, vc, gain, gain], out


def att_fwd(qkv, qg, kg, name):
    _, ng, t, w = qkv.shape
    nbt = t // ATT_BLOCK
    in_specs, out = _att_specs(w)

    def body(q_ref, kp_ref, kc_ref, vp_ref, vc_ref, qg_ref, kg_ref, o_ref, lse_ref):
        has_prev, mask_p, mask_c = _att_masks(pl.program_id(0), pl.program_id(1), nbt, w // ATT_HEAD_DIM)

        @pl.when(has_prev)
        def _():
            o, lse = _att_fn(q_ref[0, 0], kp_ref[0, 0], kc_ref[0, 0], vp_ref[0, 0], vc_ref[0, 0],
                             qg_ref[...], kg_ref[...], mask_p, mask_c)
            o_ref[0] = o
            lse_ref[0] = lse

        @pl.when(jnp.logical_not(has_prev))
        def _():
            o, lse = _att_fn(q_ref[0, 0], None, kc_ref[0, 0], None, vc_ref[0, 0],
                             qg_ref[...], kg_ref[...], None, mask_c)
            o_ref[0] = o
            lse_ref[0] = lse

    sh = jax.ShapeDtypeStruct((ng, t, w), F32)
    return pl.pallas_call(
        body, name=name, grid=(ng, nbt), in_specs=in_specs, out_specs=[out, out], out_shape=[sh, sh],
        compiler_params=_params(("parallel", "parallel")),
    )(qkv, qkv, qkv, qkv, qkv, qg, kg)


def att_bwd(qkv, qg, kg, do, dlse, name):
    _, ng, t, w = qkv.shape
    nbt = t // ATT_BLOCK
    in_specs, out = _att_specs(w)
    whole = pl.BlockSpec((1, t, w), lambda g, j: (g, 0, 0))
    gain = in_specs[-1]

    def body(q_ref, kp_ref, kc_ref, vp_ref, vc_ref, qg_ref, kg_ref, do_ref, dlse_ref,
             dq_ref, dk_ref, dv_ref, dqg_ref, dkg_ref):
        g, j = pl.program_id(0), pl.program_id(1)
        has_prev, mask_p, mask_c = _att_masks(g, j, nbt, w // ATT_HEAD_DIM)

        @pl.when(j == 0)
        def _():
            dk_ref[...] = jnp.zeros_like(dk_ref)
            dv_ref[...] = jnp.zeros_like(dv_ref)

        @pl.when(jnp.logical_and(g == 0, j == 0))
        def _():
            dqg_ref[...] = jnp.zeros_like(dqg_ref)
            dkg_ref[...] = jnp.zeros_like(dkg_ref)

        rows_c = pl.ds(pl.multiple_of(j * ATT_BLOCK, ATT_BLOCK), ATT_BLOCK)
        rows_p = pl.ds(pl.multiple_of(jnp.maximum(j - 1, 0) * ATT_BLOCK, ATT_BLOCK), ATT_BLOCK)
        @pl.when(has_prev)
        def _():
            fn = functools.partial(_att_fn, mask_p=mask_p, mask_c=mask_c)
            _, vjp = jax.vjp(fn, q_ref[0, 0], kp_ref[0, 0], kc_ref[0, 0], vp_ref[0, 0], vc_ref[0, 0],
                             qg_ref[...], kg_ref[...])
            dq, dkp, dkc, dvp, dvc, dqg, dkg = vjp((do_ref[0], dlse_ref[0]))
            dq_ref[0] = dq
            dk_ref[0, rows_p, :] += dkp
            dv_ref[0, rows_p, :] += dvp
            dk_ref[0, rows_c, :] += dkc
            dv_ref[0, rows_c, :] += dvc
            dqg_ref[...] += dqg
            dkg_ref[...] += dkg

        @pl.when(jnp.logical_not(has_prev))
        def _():
            def fn(q, kc, vc, qg, kg):
                return _att_fn(q, None, kc, None, vc, qg, kg, None, mask_c)

            _, vjp = jax.vjp(fn, q_ref[0, 0], kc_ref[0, 0], vc_ref[0, 0], qg_ref[...], kg_ref[...])
            dq, dkc, dvc, dqg, dkg = vjp((do_ref[0], dlse_ref[0]))
            dq_ref[0] = dq
            dk_ref[0, rows_c, :] += dkc
            dv_ref[0, rows_c, :] += dvc
            dqg_ref[...] += dqg
            dkg_ref[...] += dkg

    sh = jax.ShapeDtypeStruct((ng, t, w), F32)
    gshape = jax.ShapeDtypeStruct((1, w), F32)
    return pl.pallas_call(
        body, name=name, grid=(ng, nbt), in_specs=in_specs + [out, out],
        out_specs=[out, whole, whole, gain, gain], out_shape=[sh, sh, sh, gshape, gshape],
        compiler_params=_params(("arbitrary", "arbitrary")),
    )(qkv, qkv, qkv, qkv, qkv, qg, kg, do, dlse)


def _combine_fn(o, lse):
    m = jnp.max(lse, axis=0, keepdims=True)
    e = jnp.exp(lse - m)
    w = e / jnp.sum(e, axis=0, keepdims=True)
    return jnp.sum(w * o, axis=0)


def att_combine(o, lse, name):
    ng, t, w = o.shape
    tr = _tile(t, 256, 16)
    spec = pl.BlockSpec((ng, tr, w), lambda i: (0, i, 0))
    y_spec = pl.BlockSpec((tr, w), lambda i: (i, 0))

    def body(o_ref, l_ref, y_ref):
        y_ref[...] = _combine_fn(o_ref[...], l_ref[...]).astype(y_ref.dtype)

    return pl.pallas_call(
        body, name=name, grid=(t // tr,), in_specs=[spec, spec], out_specs=y_spec,
        out_shape=jax.ShapeDtypeStruct((t, w), CD), compiler_params=_params(("parallel",)),
    )(o, lse)


def att_combine_bwd(o, lse, dy, name):
    ng, t, w = o.shape
    tr = _tile(t, 256, 8)
    spec = pl.BlockSpec((ng, tr, w), lambda i: (0, i, 0))
    y_spec = pl.BlockSpec((tr, w), lambda i: (i, 0))

    def body(o_ref, l_ref, dy_ref, do_ref, dl_ref):
        _, vjp = jax.vjp(_combine_fn, o_ref[...], l_ref[...])
        do, dl = vjp(dy_ref[...])
        do_ref[...] = do
        dl_ref[...] = dl

    sh = jax.ShapeDtypeStruct(o.shape, F32)
    return pl.pallas_call(
        body, name=name, grid=(t // tr,), in_specs=[spec, spec, y_spec], out_specs=[spec, spec],
        out_shape=[sh, sh], compiler_params=_params(("parallel",)),
    )(o, lse, dy)


def _shift_down(x, s):
    if s == 0:
        return x
    row = lax.broadcasted_iota(jnp.int32, x.shape, 0)
    return jnp.where(row >= s, pltpu.roll(x, s, 0), 0.0)


def _shift_up(x, s):
    if s == 0:
        return x
    t = x.shape[0]
    row = lax.broadcasted_iota(jnp.int32, x.shape, 0)
    return jnp.where(row < t - s, pltpu.roll(x, t - s, 0), 0.0)


def conv_fwd(z, col0, w, name):
    t = z.shape[0]
    c = w.shape[1]
    nblk0 = col0 // LANE

    def body(z_ref, w_ref, c_ref):
        zv = z_ref[...]
        acc = None
        for i in range(CONV_WIDTH):
            term = _shift_down(zv, CONV_WIDTH - 1 - i) * w_ref[i:i + 1, :]
            acc = term if acc is None else acc + term
        c_ref[...] = acc

    return pl.pallas_call(
        body, name=name, grid=(c // LANE,),
        in_specs=[pl.BlockSpec((t, LANE), lambda j: (0, nblk0 + j)), pl.BlockSpec((CONV_WIDTH, LANE), lambda j: (0, j))],
        out_specs=pl.BlockSpec((t, LANE), lambda j: (0, j)),
        out_shape=jax.ShapeDtypeStruct((t, c), F32), compiler_params=_params(("parallel",)),
    )(z, w)


def conv_bwd(dc, z, col0, w, name):
    t = z.shape[0]
    c = w.shape[1]
    nblk0 = col0 // LANE

    def body(dc_ref, z_ref, w_ref, dz_ref, dw_ref):
        dcv, zv = dc_ref[...], z_ref[...]
        acc = None
        for i in range(CONV_WIDTH):
            s = CONV_WIDTH - 1 - i
            term = _shift_up(dcv, s) * w_ref[i:i + 1, :]
            acc = term if acc is None else acc + term
            dw_ref[i:i + 1, :] = jnp.sum(dcv * _shift_down(zv, s), axis=0, keepdims=True)
        dz_ref[...] = acc.astype(dz_ref.dtype)

    blk = pl.BlockSpec((t, LANE), lambda j: (0, j))
    wblk = pl.BlockSpec((CONV_WIDTH, LANE), lambda j: (0, j))
    return pl.pallas_call(
        body, name=name, grid=(c // LANE,),
        in_specs=[blk, pl.BlockSpec((t, LANE), lambda j: (0, nblk0 + j)), wblk], out_specs=[blk, wblk],
        out_shape=[jax.ShapeDtypeStruct((t, c), CD), jax.ShapeDtypeStruct((CONV_WIDTH, c), F32)],
        compiler_params=_params(("parallel",)),
    )(dc, z, w)


def _dn_consts():
    c = DN_CHUNK
    row = lax.broadcasted_iota(jnp.int32, (c, c), 0)
    col = lax.broadcasted_iota(jnp.int32, (c, c), 1)
    return dict(tril=row >= col, strict=row > col, eye=(row == col).astype(F32),
                tril_f=(row >= col).astype(F32), triu_f=(row <= col).astype(F32))


def _softplus(x):
    return jnp.maximum(x, 0.0) + jnp.log(1.0 + jnp.exp(-jnp.abs(x)))


def _split(x):
    hi = x.astype(CD)
    return hi, (x - hi.astype(F32)).astype(CD)


def _bdot3(a, b, ca, cb):
    ah, al = _split(a)
    bh, bl = _split(b)
    return _bdot(ah, bh, ca, cb) + (_bdot(ah, bl, ca, cb) + _bdot(al, bh, ca, cb))


def _tri_inv_impl(a_mat):
    c = a_mat.shape[-1]
    eye = (lax.broadcasted_iota(jnp.int32, (c, c), 0) == lax.broadcasted_iota(jnp.int32, (c, c), 1)).astype(F32)
    nk_ = -a_mat
    t_inv = eye + nk_
    for _ in range(c.bit_length() - 2):
        nk_ = _bdot3(nk_, nk_, 2, 1)
        t_inv = t_inv + _bdot3(t_inv, nk_, 2, 1)
    return t_inv


@jax.custom_vjp
def _tri_inv(a_mat):
    return _tri_inv_impl(a_mat)


def _tri_inv_fwd(a_mat):
    t_inv = _tri_inv_impl(a_mat)
    return t_inv, t_inv


def _tri_inv_bwd(t_inv, dt_inv):
    return (-_bdot3(_bdot3(t_inv, dt_inv, 1, 1), t_inv, 2, 2),)


_tri_inv.defvjp(_tri_inv_fwd, _tri_inv_bwd)


def _dn_chunk(cq, ck, cv, og, a_col, b_col, al, dt, gn, s_prev, *, k, inv):
    q = _silu(cq)
    q = q * lax.rsqrt(jnp.sum(q * q, axis=-1, keepdims=True) + EPS) * (DN_HEAD_DIM ** -0.5)
    kk = _silu(ck)
    kk = kk * lax.rsqrt(jnp.sum(kk * kk, axis=-1, keepdims=True) + EPS)
    v = _silu(cv)
    g = -jnp.exp(al) * _softplus(a_col + dt)
    beta = jax.nn.sigmoid(b_col)
    g_row = jnp.sum(k["eye"] * g, axis=1, keepdims=True)
    gc_col = jnp.sum(k["tril_f"] * g_row, axis=2, keepdims=True)
    gc_row = jnp.sum(k["triu_f"] * g, axis=1, keepdims=True)
    ldec = jnp.where(k["tril"], jnp.exp(jnp.where(k["tril"], gc_col - gc_row, 0.0)), 0.0)
    kb, vb = kk * beta, v * beta
    a_mat = jnp.where(k["strict"], _bnt(kb, kk) * ldec, 0.0)
    t_inv = inv(a_mat)
    egc = jnp.exp(gc_col)
    u = _bnn(t_inv, vb)
    w = _bnn(t_inv, kb * egc)
    attn = jnp.where(k["tril"], _bnt(q, kk) * ldec, 0.0)
    gc_last = jnp.sum(g, axis=1, keepdims=True)
    k_dec = kk * jnp.exp(gc_last - gc_col)
    v_new = u - _bnn(w, s_prev)
    o = _bnn(q * egc, s_prev) + _bnn(attn, v_new)
    s_new = s_prev * jnp.exp(gc_last) + _btn(k_dec, v_new)
    y = o * lax.rsqrt(jnp.mean(o * o, axis=-1, keepdims=True) + EPS) * gn * _silu(og)
    return y, s_new


def _dn_heads(ref, nh):
    hd = DN_HEAD_DIM
    return jnp.stack([ref[:, h * hd:(h + 1) * hd] for h in range(nh)])


def _dn_specs(nh, col_gate, col_ab, order):
    hd, c = DN_HEAD_DIM, DN_CHUNK
    w = nh * hd
    qs = pl.BlockSpec((c, w), lambda n: (order(n), 0))
    ks = pl.BlockSpec((c, w), lambda n: (order(n), 1))
    vs = pl.BlockSpec((c, w), lambda n: (order(n), 2))
    gs = pl.BlockSpec((c, w), lambda n: (order(n), col_gate // w))
    ab = pl.BlockSpec((c, LANE), lambda n: (order(n), col_ab // LANE))
    scal = pl.BlockSpec((nh, 1, 1), lambda n: (0, 0, 0))
    gn = pl.BlockSpec((1, hd), lambda n: (0, 0))
    st = pl.BlockSpec((1, nh, hd, hd), lambda n: (order(n), 0, 0, 0))
    return qs, ks, vs, gs, ab, scal, gn, st


def _lane_pick(x, idx):
    lane = lax.broadcasted_iota(jnp.int32, x.shape, 1)
    return jnp.sum(jnp.where(lane == idx, x, 0.0), axis=1, keepdims=True)


def dn_fwd(cv, z, col_gate, col_ab, a_log, dt_bias, gn, name):
    t = cv.shape[0]
    nh = a_log.shape[0]
    hd, c = DN_HEAD_DIM, DN_CHUNK
    n_chunks = t // c
    qs, ks, vs, gs, ab, scal, gnspec, st = _dn_specs(nh, col_gate, col_ab, lambda n: n)

    def body(q_ref, k_ref, v_ref, g_ref, ab_ref, al_ref, dt_ref, gn_ref, y_ref, st_ref, s_scr):
        @pl.when(pl.program_id(0) == 0)
        def _():
            s_scr[...] = jnp.zeros_like(s_scr)

        abv = ab_ref[...]
        a_col = jnp.stack([_lane_pick(abv, h) for h in range(nh)])
        b_col = jnp.stack([_lane_pick(abv, nh + h) for h in range(nh)])
        s_prev = s_scr[...]
        st_ref[0] = s_prev
        y, s_new = _dn_chunk(_dn_heads(q_ref, nh), _dn_heads(k_ref, nh), _dn_heads(v_ref, nh), _dn_heads(g_ref, nh),
                             a_col, b_col, al_ref[...], dt_ref[...], gn_ref[...], s_prev,
                             k=_dn_consts(), inv=_tri_inv_impl)
        for h in range(nh):
            y_ref[:, h * hd:(h + 1) * hd] = y[h].astype(y_ref.dtype)
        s_scr[...] = s_new

    return pl.pallas_call(
        body, name=name, grid=(n_chunks,),
        in_specs=[qs, ks, vs, gs, ab, scal, scal, gnspec],
        out_specs=[pl.BlockSpec((c, nh * hd), lambda n: (n, 0)), st],
        out_shape=[jax.ShapeDtypeStruct((t, nh * hd), CD), jax.ShapeDtypeStruct((n_chunks, nh, hd, hd), F32)],
        scratch_shapes=[pltpu.VMEM((nh, hd, hd), F32)],
        compiler_params=_params(("arbitrary",)),
    )(cv, cv, cv, z, z, a_log, dt_bias, gn)


def dn_bwd(cv, z, col_gate, col_ab, a_log, dt_bias, gn, states, dy, name):
    t = cv.shape[0]
    nh = a_log.shape[0]
    hd, c = DN_HEAD_DIM, DN_CHUNK
    w = nh * hd
    n_chunks = t // c
    rev = lambda n: n_chunks - 1 - n
    qs, ks, vs, gs, ab, scal, gnspec, st = _dn_specs(nh, col_gate, col_ab, rev)
    yspec = pl.BlockSpec((c, w), lambda n: (rev(n), 0))

    def body(q_ref, k_ref, v_ref, g_ref, ab_ref, al_ref, dt_ref, gn_ref, st_ref, dy_ref,
             dc_ref, dg_ref, dab_ref, dal_ref, ddt_ref, dgn_ref, ds_scr):
        @pl.when(pl.program_id(0) == 0)
        def _():
            ds_scr[...] = jnp.zeros_like(ds_scr)
            dal_ref[...] = jnp.zeros_like(dal_ref)
            ddt_ref[...] = jnp.zeros_like(ddt_ref)
            dgn_ref[...] = jnp.zeros_like(dgn_ref)

        abv = ab_ref[...]
        a_col = jnp.stack([_lane_pick(abv, h) for h in range(nh)])
        b_col = jnp.stack([_lane_pick(abv, nh + h) for h in range(nh)])
        fn = functools.partial(_dn_chunk, k=_dn_consts(), inv=_tri_inv)
        _, vjp = jax.vjp(fn, _dn_heads(q_ref, nh), _dn_heads(k_ref, nh), _dn_heads(v_ref, nh), _dn_heads(g_ref, nh),
                         a_col, b_col, al_ref[...], dt_ref[...], gn_ref[...], st_ref[0])
        dq, dk, dv, dg, da, db, dal, ddt, dgn, ds = vjp((_dn_heads(dy_ref, nh), ds_scr[...]))
        lane = lax.broadcasted_iota(jnp.int32, (c, LANE), 1)
        dab = jnp.zeros((c, LANE), F32)
        for h in range(nh):
            cols = slice(h * hd, (h + 1) * hd)
            dc_ref[:, cols] = dq[h]
            dc_ref[:, w + h * hd:w + (h + 1) * hd] = dk[h]
            dc_ref[:, 2 * w + h * hd:2 * w + (h + 1) * hd] = dv[h]
            dg_ref[:, cols] = dg[h].astype(dg_ref.dtype)
            dab = dab + jnp.where(lane == h, da[h], 0.0) + jnp.where(lane == nh + h, db[h], 0.0)
        dab_ref[...] = dab.astype(dab_ref.dtype)
        dal_ref[...] += dal
        ddt_ref[...] += ddt
        dgn_ref[...] += dgn
        ds_scr[...] = ds

    sshape = jax.ShapeDtypeStruct((nh, 1, 1), F32)
    res = pl.pallas_call(
        body, name=name, grid=(n_chunks,),
        in_specs=[qs, ks, vs, gs, ab, scal, scal, gnspec, st, yspec],
        out_specs=[pl.BlockSpec((c, 3 * w), lambda n: (rev(n), 0)), yspec,
                   pl.BlockSpec((c, LANE), lambda n: (rev(n), 0)), scal, scal, gnspec],
        out_shape=[jax.ShapeDtypeStruct((t, 3 * w), F32), jax.ShapeDtypeStruct((t, w), CD),
                   jax.ShapeDtypeStruct((t, LANE), CD), sshape, sshape, jax.ShapeDtypeStruct((1, hd), F32)],
        scratch_shapes=[pltpu.VMEM((nh, hd, hd), F32)],
        compiler_params=_params(("arbitrary",)),
    )(cv, cv, cv, z, z, a_log, dt_bias, gn, states, dy)
    return res


def merge_fwd(ya, wpa, yd, wpd, z, col_m, name):
    t, d = yd.shape[0], wpd.shape[1]
    tm, tn = _tile(t, 512, LANE), _tile(d, 256, LANE)
    nb1, nb2 = col_m // tn, (col_m + d) // tn

    def body(ya_ref, wpa_ref, yd_ref, wpd_ref, z1_ref, z2_ref, m_ref, pa_ref, pd_ref):
        pa = _nn(ya_ref[...], wpa_ref[...])
        pd = _nn(yd_ref[...], wpd_ref[...])
        pa_ref[...] = pa
        pd_ref[...] = pd
        m_ref[...] = (jax.nn.sigmoid(z1_ref[...]) * pa + jax.nn.sigmoid(z2_ref[...]) * pd).astype(m_ref.dtype)

    o = pl.BlockSpec((tm, tn), lambda i, j: (i, j))
    return pl.pallas_call(
        body, name=name, grid=(t // tm, d // tn),
        in_specs=[pl.BlockSpec((tm, ya.shape[1]), lambda i, j: (i, 0)),
                  pl.BlockSpec((wpa.shape[0], tn), lambda i, j: (0, j)),
                  pl.BlockSpec((tm, yd.shape[1]), lambda i, j: (i, 0)),
                  pl.BlockSpec((wpd.shape[0], tn), lambda i, j: (0, j)),
                  pl.BlockSpec((tm, tn), lambda i, j: (i, nb1 + j)),
                  pl.BlockSpec((tm, tn), lambda i, j: (i, nb2 + j))],
        out_specs=[o, o, o],
        out_shape=[jax.ShapeDtypeStruct((t, d), CD), jax.ShapeDtypeStruct((t, d), F32),
                   jax.ShapeDtypeStruct((t, d), F32)],
        compiler_params=_params(("parallel", "parallel")),
    )(ya, wpa, yd, wpd, z, z)


def merge_bwd(dm, pa, pd, z, col_m, name):
    t, d = dm.shape
    tm, tn = _tile(t, 512, 8), _tile(d, 256, LANE)
    nb1, nb2 = col_m // tn, (col_m + d) // tn

    def body(dm_ref, pa_ref, pd_ref, z1_ref, z2_ref, dpa_ref, dpd_ref, dz1_ref, dz2_ref):
        dmv = dm_ref[...]
        s1, s2 = jax.nn.sigmoid(z1_ref[...]), jax.nn.sigmoid(z2_ref[...])
        dpa_ref[...] = (dmv * s1).astype(dpa_ref.dtype)
        dpd_ref[...] = (dmv * s2).astype(dpd_ref.dtype)
        dz1_ref[...] = (dmv * pa_ref[...] * s1 * (1.0 - s1)).astype(dz1_ref.dtype)
        dz2_ref[...] = (dmv * pd_ref[...] * s2 * (1.0 - s2)).astype(dz2_ref.dtype)

    o = pl.BlockSpec((tm, tn), lambda i, j: (i, j))
    sh = jax.ShapeDtypeStruct((t, d), CD)
    return pl.pallas_call(
        body, name=name, grid=(t // tm, d // tn),
        in_specs=[o, o, o, pl.BlockSpec((tm, tn), lambda i, j: (i, nb1 + j)),
                  pl.BlockSpec((tm, tn), lambda i, j: (i, nb2 + j))],
        out_specs=[o, o, o, o], out_shape=[sh, sh, sh, sh],
        compiler_params=_params(("parallel", "parallel")),
    )(dm, pa, pd, z, z)


def ada_fwd(c_all, w, b, name):
    nl, d, n = w.shape
    tn = _tile(n, 384, LANE)

    def body(c_ref, w_ref, b_ref, o_ref):
        o_ref[0] = _nn(_silu(c_ref[...]), w_ref[0]) + b_ref[0]

    return pl.pallas_call(
        body, name=name, grid=(nl, n // tn),
        in_specs=[pl.BlockSpec(c_all.shape, lambda l, j: (0, 0)), pl.BlockSpec((1, d, tn), lambda l, j: (l, 0, j)),
                  pl.BlockSpec((1, 1, tn), lambda l, j: (l, 0, j))],
        out_specs=pl.BlockSpec((1, c_all.shape[0], tn), lambda l, j: (l, 0, j)),
        out_shape=jax.ShapeDtypeStruct((nl, c_all.shape[0], n), F32),
        compiler_params=_params(("parallel", "parallel")),
    )(c_all, w, b)


def ada_bwd(c_pad, dmod_pad, name):
    nl, kp, n = dmod_pad.shape
    d = c_pad.shape[1]
    tn = _tile(n, 384, LANE)

    def body(c_ref, g_ref, o_ref):
        o_ref[0] = _tn(_silu(c_ref[...]), g_ref[0])

    return pl.pallas_call(
        body, name=name, grid=(nl, n // tn),
        in_specs=[pl.BlockSpec((kp, d), lambda l, j: (0, 0)), pl.BlockSpec((1, kp, tn), lambda l, j: (l, 0, j))],
        out_specs=pl.BlockSpec((1, d, tn), lambda l, j: (l, 0, j)),
        out_shape=jax.ShapeDtypeStruct((nl, d, n), F32),
        compiler_params=_params(("parallel", "parallel")),
    )(c_pad, dmod_pad)


def loss_head(y, target, name):
    t, d = y.shape
    tr = _tile(t, 256, 8)

    def body(y_ref, t_ref, dy_ref, l_ref):
        err = y_ref[...] - t_ref[...]
        dy_ref[...] = err * (1.0 / d)
        part = jnp.sum(jnp.sum(err * err, axis=1, keepdims=True), axis=0, keepdims=True) * (0.5 / d)

        @pl.when(pl.program_id(0) == 0)
        def _():
            l_ref[...] = jnp.zeros_like(l_ref)

        l_ref[...] += part

    row = pl.BlockSpec((tr, d), lambda i: (i, 0))
    return pl.pallas_call(
        body, name=name, grid=(t // tr,), in_specs=[row, row],
        out_specs=[row, pl.BlockSpec((8, LANE), lambda i: (0, 0))],
        out_shape=[jax.ShapeDtypeStruct((t, d), F32), jax.ShapeDtypeStruct((8, LANE), F32)],
        compiler_params=_params(("arbitrary",)),
    )(y, target)


def _adamw_update(g, w_ref, m_ref, v_ref, go_ref, d_ref, mo_ref, vo_ref):
    m_new = ADAM_B1 * m_ref[...] + (1.0 - ADAM_B1) * g
    v_new = ADAM_B2 * v_ref[...] + (1.0 - ADAM_B2) * jnp.square(g)
    m_hat = m_new / (1.0 - ADAM_B1 ** ADAM_STEP)
    v_hat = v_new / (1.0 - ADAM_B2 ** ADAM_STEP)
    go_ref[...] = g
    d_ref[...] = -ADAM_LR * (m_hat / (jnp.sqrt(v_hat) + ADAM_EPS) + ADAM_WD * w_ref[...])
    mo_ref[...] = m_new
    vo_ref[...] = v_new


def adamw(w, m, v, g_slots, name):
    shape = w.shape
    nslot = g_slots.shape[0]
    if w.ndim == 2:
        w3, m3, v3, g4 = w[None], m[None], v[None], g_slots[:, None]
    else:
        w3, m3, v3, g4 = w, m, v, g_slots
    nl, r, c = w3.shape
    tr = _tile(r, 256, 16)

    def body(w_ref, m_ref, v_ref, g_ref, *outs):
        g = g_ref[0].astype(F32)
        for s in range(1, nslot):
            g = g + g_ref[s].astype(F32)
        _adamw_update(g, w_ref, m_ref, v_ref, *outs)

    blk = pl.BlockSpec((1, tr, c), lambda l, i: (l, i, 0))
    gblk = pl.BlockSpec((nslot, 1, tr, c), lambda l, i: (0, l, i, 0))
    sh = jax.ShapeDtypeStruct(w3.shape, F32)
    outs = pl.pallas_call(
        body, name=name, grid=(nl, r // tr), in_specs=[blk, blk, blk, gblk], out_specs=[blk] * 4,
        out_shape=[sh] * 4, compiler_params=_params(("parallel", "parallel")),
    )(w3, m3, v3, g4)
    return tuple(o.reshape(shape) for o in outs)


def adamw_layers(w, m, v, g_layers, first, name, prev=None):
    _, r, c = w.shape
    n = len(g_layers)
    nslot, _, cg = g_layers[0].shape
    tr = _tile(r, 256, 16)

    def body(w_ref, m_ref, v_ref, *rest):
        g_refs, outs = rest[:n], rest[-4:]
        for li in range(n):
            @pl.when(pl.program_id(0) == li)
            def _(g_ref=g_refs[li]):
                g = g_ref[0, :, :c].astype(F32)
                for s in range(1, nslot):
                    g = g + g_ref[s, :, :c].astype(F32)
                _adamw_update(g[None], w_ref, m_ref, v_ref, *outs)

    blk = pl.BlockSpec((1, tr, c), lambda l, i: (first + l, i, 0))
    gblks = [pl.BlockSpec((nslot, tr, cg), lambda l, i, li=li: (0, jnp.where(l == li, i, 0), 0)) for li in range(n)]
    sh = jax.ShapeDtypeStruct(w.shape, F32)
    extra = [] if prev is None else list(prev)
    return tuple(pl.pallas_call(
        body, name=name, grid=(n, r // tr),
        in_specs=[blk, blk, blk] + gblks + [pl.BlockSpec(memory_space=pl.ANY)] * len(extra),
        out_specs=[blk] * 4, out_shape=[sh] * 4,
        input_output_aliases={3 + n + k: k for k in range(len(extra))},
        compiler_params=_params(("arbitrary", "arbitrary")),
    )(w, m, v, *g_layers, *extra))


def _rows(flat, unit=16):
    n = flat.shape[0]
    per = 1024 * unit
    pad = (-n) % per
    if pad:
        flat = jnp.concatenate([flat, jnp.zeros((pad,), flat.dtype)])
    return flat.reshape(-1, 1024)


def _to_classes(a):
    t, w = a.shape[1], a.shape[2]
    return jnp.stack([a[gi].reshape(t // dil, dil, w).transpose(1, 0, 2).reshape(t, w)
                      for gi, dil in enumerate(DILATIONS)])


def _from_classes(a):
    t, w = a.shape[1], a.shape[2]
    return jnp.stack([a[gi].reshape(dil, t // dil, w).transpose(1, 0, 2).reshape(t, w)
                      for gi, dil in enumerate(DILATIONS)])


def _unshard_cols(g):
    return g.transpose(1, 0, 2).reshape(g.shape[1], -1)


def _shard_cols(full):
    r = full.shape[0]
    return full.reshape(r, N_DEV, -1).transpose(1, 0, 2)


def kernel(x, c, ada_w, ada_b, norm_ff1, ffn1_w_up, ffn1_w_down, norm_mix, w_in, q_norm, k_norm, conv_w, a_log, dt_bias, dn_norm, w_proj_att, w_proj_dn, w_out, norm_ff2, ffn2_w_up, ffn2_w_down, loss_target, m_ada_w, m_ada_b, m_norm_ff1, m_ffn1_w_up, m_ffn1_w_down, m_norm_mix, m_w_in, m_q_norm, m_k_norm, m_conv_w, m_a_log, m_dt_bias, m_dn_norm, m_w_proj_att, m_w_proj_dn, m_w_out, m_norm_ff2, m_ffn2_w_up, m_ffn2_w_down, v_ada_w, v_ada_b, v_norm_ff1, v_ffn1_w_up, v_ffn1_w_down, v_norm_mix, v_w_in, v_q_norm, v_k_norm, v_conv_w, v_a_log, v_dt_bias, v_dn_norm, v_w_proj_att, v_w_proj_dn, v_w_out, v_norm_ff2, v_ffn2_w_up, v_ffn2_w_down):
    nl = ada_w.shape[0]
    t, d = x.shape[1], x.shape[2]
    dff = ffn1_w_down.shape[1] * N_DEV
    ha = d // 256
    ng = len(DILATIONS)
    wa = ha * ATT_HEAD_DIM
    att_w = ng * wa
    nh = d // DN_HEAD_DIM
    dn_w = nh * DN_HEAD_DIM
    n_in = w_in.shape[2] * N_DEV
    off_dn, off_gate = 3 * att_w, 3 * att_w + 3 * dn_w
    off_a = off_gate + dn_w
    off_merge = off_a + 2 * nh
    assert off_merge + 2 * d == n_in
    col_dn, col_gate, col_m = 0, 3 * dn_w, 4 * dn_w
    col_att = col_m + 2 * d
    col_ab = col_att + 3 * att_w
    zw = col_ab + 2 * LANE
    me = 4 * lax.axis_index("x") + 2 * lax.axis_index("y") + lax.axis_index("c")
    xs = x[0]
    target = loss_target[0]

    conv_rows = _rows(conv_w.reshape(-1), 8)
    pack0 = jnp.concatenate([jnp.concatenate([c, jnp.zeros((7, d), F32)]).reshape(-1), conv_rows.reshape(-1)])
    pack0 = _rows(pack0, 8)
    g0 = all_gather([pack0], "ag_c_conv")[0].reshape(N_DEV, -1)
    c_all = g0[:, :d]
    cw = g0[:, 8 * d:8 * d + conv_w.size].reshape(N_DEV, nl, CONV_WIDTH, -1)
    conv_full = cw.transpose(1, 2, 0, 3).reshape(nl, CONV_WIDTH, 3 * dn_w)

    n_ada = ada_w.shape[2]
    b_mine = lax.dynamic_slice_in_dim(ada_b, me * n_ada, n_ada, axis=1)[:, None, :]
    mod_s = ada_fwd(c_all, ada_w, b_mine, "ada_fwd")
    gm = all_gather([mod_s], "ag_mod")[0]
    mod = lax.dynamic_index_in_dim(gm, me, axis=2, keepdims=False)
    mod = mod.transpose(1, 0, 2).reshape(nl, N_ADA, 1, d)

    kinds = [ffn1_w_up, ffn1_w_down, w_in, w_proj_att, w_proj_dn, w_out, ffn2_w_up, ffn2_w_down]
    c_up = ffn1_w_up.shape[2]
    cp = -(-c_up // LANE) * LANE
    r_dn = ffn1_w_down.shape[1]
    assert 2 * r_dn == c_up

    up1_t, up2_t = jnp.swapaxes(ffn1_w_up, 1, 2), jnp.swapaxes(ffn2_w_up, 1, 2)
    w_in_t = jnp.transpose(w_in, (2, 0, 1))

    def layer_shards(l):
        def pad_up(wt):
            return jnp.pad(wt.astype(CD), ((0, cp - c_up), (0, 0)))
        return [pad_up(up1_t[l]), ffn1_w_down[l].astype(CD), w_in_t[:, l].astype(CD), w_proj_att[l].astype(CD),
                w_proj_dn[l].astype(CD), w_out[l].astype(CD), pad_up(up2_t[l]), ffn2_w_down[l].astype(CD)]

    def gather_landing(shards):
        return [lax.dynamic_update_index_in_dim(lax.empty((N_DEV,) + s.shape, s.dtype), s, me, 0) for s in shards]

    def slot_landing(arrs):
        return [lax.dynamic_update_index_in_dim(lax.empty(a.shape, a.dtype),
                                                lax.dynamic_index_in_dim(a, me, 0, keepdims=False), me, 0) for a in arrs]

    gathered = all_gather(layer_shards(0), "ag_weights0")
    prefetch = None

    def full_weight(ki, l):
        blk = gathered[ki]
        if ki == 3:
            return _unshard_cols(blk)
        if ki == 2:
            return w_in_rows(blk.reshape(n_in, d))
        if ki in (1, 7):
            pairs = blk.reshape(HALF, c_up, d)
            return jnp.pad(pairs, ((0, 0), (0, cp - c_up), (0, 0))).reshape(HALF * cp, d)
        return blk.reshape(-1, blk.shape[2])

    def down_grad_slots(g):
        return g.reshape(HALF, cp, d)[:, :c_up].reshape(N_DEV, r_dn, d)

    def w_in_rows(wt):
        pad = jnp.zeros((zw - n_in, wt.shape[1]), wt.dtype)
        return jnp.concatenate([wt[off_dn:off_a], wt[off_merge:], wt[:off_dn], wt[off_a:off_merge], pad], axis=0)

    def w_in_cols_inv(g):
        return jnp.concatenate([g[:, col_att:col_ab], g[:, :col_m], g[:, col_ab:col_ab + 2 * nh], g[:, col_m:col_att]], axis=1)

    saved = []
    xc = xs
    mods = []
    for l in range(nl):
        sv = {}
        if prefetch is not None:
            gathered = exchange_wait(*prefetch[:4], xc, True, f"ag_wait{l}")
        mod_l = mod[l]
        if l + 1 < nl:
            shards = layer_shards(l + 1)
            behind = gathered[0][0, :1, :1].astype(F32) + mod[0, 0, :, :1]
            prefetch = exchange_start(shards, gather_landing(shards), behind, True, f"ag_start{l + 1}")
            mod_l = mod_l + prefetch[4][0, 0]
        mods.append(mod_l)
        sh1, sc1, gt1, sh2, sc2, gt2, sh3, sc3, gt3 = [mod_l[i] for i in range(N_ADA)]
        w_dn1, w_dn2 = full_weight(1, l), full_weight(7, l)
        win = full_weight(2, l)
        wpa, wpd, wo = full_weight(3, l), full_weight(4, l), full_weight(5, l)
        sv["w"] = (gathered[0], gathered[6], w_dn1, w_dn2, win, wpa, wpd, wo)

        def ffn(xin, g, sh, sc, gt, u_all, w_dn):
            h = norm_mod(xin, g, sc, sh, "norm_mod")
            gate, up, a = ffn_up(h, u_all, "ffn_up")
            xo, f = matmul([(a, w_dn)], "nn", F32, "ffn_down", out_scale=0.5 * gt, resid=xin, save_acc=True,
                           tm=1024, tk=4096)
            return xo, (xin, h, gate, up, a, f)

        xc, sv["ffn1"] = ffn(xc, norm_ff1[l:l + 1], sh1, sc1, gt1, gathered[0], w_dn1)

        x_mix = xc
        h2 = norm_mod(x_mix, norm_mix[l:l + 1], sc2, sh2, "norm_mod")
        z = matmul([(h2, win)], "nt", F32, "w_in", tm=1024, tn=512)
        z_att = z[:, col_att:col_ab].reshape(t, 3, ng, wa).transpose(1, 2, 0, 3)
        qkv = jnp.stack([_to_classes(z_att[i]) for i in range(3)])
        qg, kg = jnp.tile(q_norm[l:l + 1], (1, ha)), jnp.tile(k_norm[l:l + 1], (1, ha))
        o_cls, lse_cls = att_fwd(qkv, qg, kg, "att_fwd")
        o_tok, lse_tok = _from_classes(o_cls), _from_classes(lse_cls)
        ya = att_combine(o_tok, lse_tok, "att_combine")
        cvo = conv_fwd(z, col_dn, conv_full[l], "conv_fwd")
        al3, dt3 = a_log[l].reshape(nh, 1, 1), dt_bias[l].reshape(nh, 1, 1)
        gn = dn_norm[l:l + 1]
        yd, states = dn_fwd(cvo, z, col_gate, col_ab, al3, dt3, gn, "dn_fwd")
        mrg, pa, pd = merge_fwd(ya, wpa, yd, wpd, z, col_m, "merge_fwd")
        xc, f2 = matmul([(mrg, wo)], "nn", F32, "w_out", out_scale=gt2, resid=x_mix, save_acc=True)
        sv["mix"] = (x_mix, h2, z, qkv, o_tok, lse_tok, ya, cvo, yd, states, mrg, pa, pd, f2)

        xc, sv["ffn2"] = ffn(xc, norm_ff2[l:l + 1], sh3, sc3, gt3, gathered[6], w_dn2)
        saved.append(sv)

    dxc, loss_blk = loss_head(xc, target, "loss_head")

    def slots(ki, g):
        if ki in (0, 1, 6, 7):
            return g
        return _shard_cols(g) if ki in (2, 3) else g.reshape(N_DEV, -1, g.shape[1])

    gbig = [None] * len(kinds)
    dmods, small, pending = [], [], []
    token = None
    for l in reversed(range(nl)):
        sv = saved[l]
        mod_l = mods[l] if token is None else mods[l] + token[0, 0]
        sh1, sc1, gt1, sh2, sc2, gt2, sh3, sc3, gt3 = [mod_l[i] for i in range(N_ADA)]
        u_all1, u_all2, w_dn1, w_dn2, win, wpa, wpd, wo = sv["w"]

        def ffn_bwd(dxo, g, sh, sc, gt, u_all, w_dn, sv_f):
            xin, h, gate, up, a, f = sv_f
            s = 0.5 * gt
            g_dn = down_grad_slots(matmul([(a, dxo)], "tn", CD, "ffn_down_wg", out_scale=s, tm=1024, tn=1024))
            dgate, dup = ffn_dact(dxo, s, w_dn, gate, up, "ffn_dact")
            g_up = ffn_up_wg(h, dgate, dup, cp, "ffn_up_wg")
            dh = ffn_up_dg(dgate, dup, u_all, "ffn_up_dg")
            dx, dg, dsc, dsh, dgt = norm_mod_bwd(xin, g, sc, sh, dh, dxo, f, 0.5, "norm_mod_bwd")
            return dx, g_up, g_dn, (dg, dsc, dsh, dgt)

        def start_exchange(ids, tag, behind):
            send = [slots(ki, gbig[ki]) for ki in ids]
            started = exchange_start(send, slot_landing(send), behind, False, f"a2a_start{l}{tag}")
            pending.append((l, ids, tag, started))
            return started[4]

        dxc, gbig[6], gbig[7], (dg3, dsc3, dsh3, dgt3) = ffn_bwd(
            dxc, norm_ff2[l:l + 1], sh3, sc3, gt3, u_all2, w_dn2, sv["ffn2"])
        token = start_exchange([6, 7], "f2", dxc)
        sh2, sc2, gt2 = (vec + token[0, 0] for vec in (sh2, sc2, gt2))

        x_mix, h2, z, qkv, o_tok, lse_tok, ya, cvo, yd, states, mrg, pa, pd, f2 = sv["mix"]
        gbig[5] = matmul([(mrg, dxc)], "tn", CD, "w_out_wg", out_scale=gt2)
        dm = matmul([(dxc, wo)], "nt", F32, "w_out_dg", a_scale=gt2)
        dpa, dpd, dz1, dz2 = merge_bwd(dm, pa, pd, z, col_m, "merge_bwd")
        gbig[3] = matmul([(ya, dpa)], "tn", CD, "patt_wg")
        gbig[4] = matmul([(yd, dpd)], "tn", CD, "pdn_wg")
        dya = matmul([(dpa, wpa)], "nt", F32, "patt_dg")
        dyd = matmul([(dpd, wpd)], "nt", F32, "pdn_dg")
        do_tok, dlse_tok = att_combine_bwd(o_tok, lse_tok, dya, "att_combine_bwd")
        qg, kg = jnp.tile(q_norm[l:l + 1], (1, ha)), jnp.tile(k_norm[l:l + 1], (1, ha))
        dq, dk, dv, dqg, dkg = att_bwd(qkv, qg, kg, _to_classes(do_tok), _to_classes(dlse_tok), "att_bwd")
        dqg, dkg = (jnp.sum(v_.reshape(ha, ATT_HEAD_DIM), axis=0, keepdims=True) for v_ in (dqg, dkg))
        dz_att = jnp.stack([_from_classes(a_) for a_ in (dq, dk, dv)])
        dz_att = dz_att.transpose(2, 0, 1, 3).reshape(t, 3 * att_w).astype(CD)
        al3, dt3 = a_log[l].reshape(nh, 1, 1), dt_bias[l].reshape(nh, 1, 1)
        gn = dn_norm[l:l + 1]
        dcvo, dz_gate, dz_ab, dal, ddt, dgn = dn_bwd(
            cvo, z, col_gate, col_ab, al3, dt3, gn, states, dyd, "dn_bwd")
        dz_dn, dconv = conv_bwd(dcvo, z, col_dn, conv_full[l], "conv_bwd")
        dz = jnp.concatenate([dz_dn, dz_gate, dz1, dz2, dz_att, dz_ab, jnp.zeros((t, LANE), CD)], axis=1)
        gbig[2] = w_in_cols_inv(matmul([(h2, dz)], "tn", CD, "w_in_wg", tm=1024, tn=zw // 4))
        dh2 = matmul([(dz, win)], "nn", F32, "w_in_dg", tm=1024, tn=1024,
                     tk=zw // 4 if (zw // 4) % LANE == 0 else zw)
        dxc, dg2, dsc2, dsh2, dgt2 = norm_mod_bwd(x_mix, norm_mix[l:l + 1], sc2, sh2, dh2, dxc, f2, 1.0, "norm_mod_bwd_mix")

        token = start_exchange([2, 3, 4, 5], "mx", dxc)
        sh1, sc1, gt1 = (vec + token[0, 0] for vec in (sh1, sc1, gt1))
        dxc, gbig[0], gbig[1], (dg1, dsc1, dsh1, dgt1) = ffn_bwd(
            dxc, norm_ff1[l:l + 1], sh1, sc1, gt1, u_all1, w_dn1, sv["ffn1"])
        if l > 0:
            token = start_exchange([0, 1], "f1", dxc)
        dmods.append(jnp.concatenate([dsh1, dsc1, dgt1, dsh2, dsc2, dgt2, dsh3, dsc3, dgt3], axis=1))
        small.append((dg1, dg2, dg3, dconv, dqg, dkg, dal.reshape(1, nh), ddt.reshape(1, nh), dgn))
    dmods.reverse()
    small.reverse()

    big_names = ["ffn1_w_up", "ffn1_w_down", "w_in", "w_proj_att", "w_proj_dn", "w_out", "ffn2_w_up", "ffn2_w_down"]
    big_m = [m_ffn1_w_up, m_ffn1_w_down, m_w_in, m_w_proj_att, m_w_proj_dn, m_w_out, m_ffn2_w_up, m_ffn2_w_down]
    big_v = [v_ffn1_w_up, v_ffn1_w_down, v_w_in, v_w_proj_att, v_w_proj_dn, v_w_out, v_ffn2_w_up, v_ffn2_w_down]
    recv_layers = [[None] * len(kinds) for _ in range(nl)]

    def wait_exchanges(layers, behind):
        for pl_, ids, tag, st in pending:
            if pl_ in layers:
                got = exchange_wait(*st[:4], behind, False, f"a2a_wait{pl_}{tag}")
                for ki, arr in zip(ids, got):
                    recv_layers[pl_][ki] = arr

    def big_adamw(ki, layers, prev, tag):
        g_layers = [recv_layers[li][ki] for li in layers]
        wmv = (kinds[ki], big_m[ki], big_v[ki])
        if ki in (0, 6):
            wmv = tuple(jnp.swapaxes(a, 1, 2) for a in wmv)
        return adamw_layers(*wmv, g_layers, layers[0], f"adamw_{big_names[ki]}{tag}", prev)

    later = list(range(1, nl))
    partial = [None] * len(kinds)
    if later:
        wait_exchanges(later, dxc)
        partial = [big_adamw(ki, later, None, "_l1") for ki in range(len(kinds))]

    fields = [jnp.stack(dmods).reshape(-1)]
    fields += [jnp.stack([s[i] for s in small]).reshape(-1) for i in range(9)]
    fields.append(loss_blk[0, :1])
    if later:
        fields.append(partial[-1][1][1, :1, 0] * 0.0)
    fsizes = [f.size for f in fields]
    foffs = [sum(fsizes[:i]) for i in range(len(fields))]
    g1 = all_gather([_rows(jnp.concatenate(fields), 8)], "ag_small")[0].reshape(N_DEV, -1)
    l = 0
    start_exchange([0, 1], "f1", g1)

    def field(i, shape):
        return g1[:, foffs[i]:foffs[i] + fsizes[i]].reshape(N_DEV, *shape)

    loss = field(10, (1,))[0, 0]
    for j in range(1, N_DEV):
        loss = loss + field(10, (1,))[j, 0]

    results = {}
    dmod_all = field(0, (nl, N_ADA * d))
    c_pad = jnp.concatenate([c_all, jnp.zeros((8, d), F32)])
    dmod_mine = lax.dynamic_slice_in_dim(dmod_all, me * n_ada, n_ada, axis=2).transpose(1, 0, 2)
    dmod_pad = jnp.concatenate([dmod_mine, jnp.zeros((nl, 8, n_ada), F32)], axis=1)
    g_ada_w = ada_bwd(c_pad, dmod_pad, "ada_bwd")
    results["ada_w"] = adamw(ada_w, m_ada_w, v_ada_w, g_ada_w[None], "adamw_ada_w")
    results["ada_b"] = adamw(ada_b, m_ada_b, v_ada_b, dmod_all, "adamw_ada_b")
    results["norm_ff1"] = adamw(norm_ff1, m_norm_ff1, v_norm_ff1, field(1, (nl, d)), "adamw_norm_ff1")
    results["norm_mix"] = adamw(norm_mix, m_norm_mix, v_norm_mix, field(2, (nl, d)), "adamw_norm_mix")
    results["norm_ff2"] = adamw(norm_ff2, m_norm_ff2, v_norm_ff2, field(3, (nl, d)), "adamw_norm_ff2")
    conv_slots = lax.dynamic_slice_in_dim(field(4, (nl, CONV_WIDTH, 3 * dn_w)), me * conv_w.shape[2],
                                          conv_w.shape[2], axis=3)
    results["conv_w"] = adamw(conv_w, m_conv_w, v_conv_w, conv_slots, "adamw_conv_w")
    results["q_norm"] = adamw(q_norm, m_q_norm, v_q_norm, field(5, (nl, ATT_HEAD_DIM)), "adamw_q_norm")
    results["k_norm"] = adamw(k_norm, m_k_norm, v_k_norm, field(6, (nl, ATT_HEAD_DIM)), "adamw_k_norm")
    results["a_log"] = adamw(a_log, m_a_log, v_a_log, field(7, (nl, nh)), "adamw_a_log")
    results["dt_bias"] = adamw(dt_bias, m_dt_bias, v_dt_bias, field(8, (nl, nh)), "adamw_dt_bias")
    results["dn_norm"] = adamw(dn_norm, m_dn_norm, v_dn_norm, field(9, (nl, DN_HEAD_DIM)), "adamw_dn_norm")
    wait_exchanges([0], results["ada_w"][1])
    for ki, nm in enumerate(big_names):
        res = big_adamw(ki, [0], partial[ki], "_l0")
        results[nm] = tuple(jnp.swapaxes(r, 1, 2) for r in res) if ki in (0, 6) else res

    order = ["ada_w", "ada_b", "norm_ff1", "ffn1_w_up", "ffn1_w_down", "norm_mix", "w_in", "q_norm", "k_norm",
             "conv_w", "a_log", "dt_bias", "dn_norm", "w_proj_att", "w_proj_dn", "w_out", "norm_ff2",
             "ffn2_w_up", "ffn2_w_down"]
    outs = [loss, dxc[None]]
    for part in range(4):
        outs += [results[n][part] for n in order]
    return tuple(outs)
```

```python
import functools

import jax
import jax.numpy as jnp
from jax import lax
from jax.experimental import pallas as pl
from jax.experimental.pallas import tpu as pltpu

F32 = jnp.float32
CD = jnp.bfloat16
EPS = 1e-6
N_DEV = 8
LANE = 128
ATT_HEAD_DIM = 64
ATT_BLOCK = 128
DILATIONS = (1, 4, 16)
DN_HEAD_DIM = 128
DN_CHUNK = 64
CONV_WIDTH = 4
N_ADA = 9
ADAM_LR, ADAM_B1, ADAM_B2, ADAM_EPS, ADAM_WD, ADAM_STEP = 0.001, 0.9, 0.999, 1e-08, 0.01, 10
VMEM_LIMIT = 56 * 1024 * 1024
NEG = -1e30
MESH = pl.DeviceIdType.MESH
HI = lax.Precision.HIGHEST


def _params(sem=None):
    return pltpu.CompilerParams(dimension_semantics=sem, vmem_limit_bytes=VMEM_LIMIT)


def _tile(n, pref, unit):
    best = None
    t = unit
    while t <= min(n, pref):
        if n % t == 0:
            best = t
        t += unit
    return best if best is not None else n


def _silu(x):
    return x * jax.nn.sigmoid(x)


def _dot(a, b, dims, precision=None):
    if precision is None:
        a, b = a.astype(CD), b.astype(CD)
    return lax.dot_general(a, b, (dims, ((), ())), precision=precision, preferred_element_type=F32)


def _nn(a, b, precision=None):
    return _dot(a, b, ((1,), (0,)), precision)


def _nt(a, b, precision=None):
    return _dot(a, b, ((1,), (1,)), precision)


def _tn(a, b, precision=None):
    return _dot(a, b, ((0,), (0,)), precision)


def all_gather(arrs, name):
    n = len(arrs)

    def body(*refs):
        x_refs, out_refs = refs[:n], refs[n:2 * n]
        send_sems, recv_sems, local_sems = refs[2 * n:]
        x, y, c = lax.axis_index("x"), lax.axis_index("y"), lax.axis_index("c")
        me, sibling = (x, y, c), (x, y, 1 - c)
        chips = [(1 - x, y), (x, 1 - y), (1 - x, 1 - y)]

        def copy(a, k, block, to, src=None):
            slot = out_refs[a].at[4 * block[0] + 2 * block[1] + block[2]]
            return pltpu.make_async_remote_copy(
                src_ref=slot if src is None else src, dst_ref=slot,
                send_sem=send_sems.at[7 * a + k], recv_sem=recv_sems.at[7 * a + k],
                device_id=to, device_id_type=MESH)

        mine, first, passed = [], [], []
        for a in range(n):
            cp = pltpu.make_async_copy(x_refs[a], out_refs[a].at[4 * x + 2 * y + c], local_sems.at[a])
            cp.start()
            mine.append(cp)
            first.append(copy(a, 0, me, sibling, src=x_refs[a]))
            first += [copy(a, 1 + j, me, (*chip, c), src=x_refs[a]) for j, chip in enumerate(chips)]
        for cp in first:
            cp.start()
        for j, chip in enumerate(chips):
            for a in range(n):
                copy(a, 1 + j, (*chip, c), me).wait_recv()
                cp = copy(a, 4 + j, (*chip, c), sibling)
                cp.start()
                passed.append(cp)
        for a in range(n):
            copy(a, 0, sibling, me).wait_recv()
            for j, chip in enumerate(chips):
                copy(a, 4 + j, (*chip, 1 - c), me).wait_recv()
        for cp in first + passed:
            cp.wait_send()
        for cp in mine:
            cp.wait()

    hbm = pl.BlockSpec(memory_space=pl.ANY)
    outs = pl.pallas_call(
        body, name=name,
        out_shape=[jax.ShapeDtypeStruct((N_DEV,) + a.shape, a.dtype) for a in arrs],
        in_specs=[hbm] * n, out_specs=[hbm] * n,
        scratch_shapes=[pltpu.SemaphoreType.DMA((7 * n,)), pltpu.SemaphoreType.DMA((7 * n,)),
                        pltpu.SemaphoreType.DMA((n,))],
    )(*arrs)
    return list(outs)


_HBM = pl.BlockSpec(memory_space=pltpu.HBM)
_SEM = pl.BlockSpec(memory_space=pltpu.SEMAPHORE)
_EFFECT = pltpu.SideEffectType.DATAFLOW_SIDE_EFFECTING


def _exchange_copies(src_refs, land_refs, send_sem, recv_sem, gather):
    x, y, c = lax.axis_index("x"), lax.axis_index("y"), lax.axis_index("c")
    me = 4 * x + 2 * y + c
    pairs = []
    for k in range(1, N_DEV):
        px, py, pc = x ^ (k >> 2), y ^ ((k >> 1) & 1), c ^ (k & 1)
        pidx = 4 * px + 2 * py + pc
        for src, land in zip(src_refs, land_refs):
            mine = src if gather else src.at[pidx]
            out = pltpu.make_async_remote_copy(src_ref=mine, dst_ref=land.at[me], send_sem=send_sem, recv_sem=recv_sem,
                                               device_id=(px, py, pc), device_id_type=MESH)
            inc = pltpu.make_async_remote_copy(src_ref=mine, dst_ref=land.at[pidx], send_sem=send_sem, recv_sem=recv_sem,
                                               device_id=(px, py, pc), device_id_type=MESH)
            pairs.append((out, inc))
    return pairs


def exchange_start(srcs, lands, after, gather, name):
    n = len(srcs)

    def body(*refs):
        src_refs, land_refs = refs[:n], refs[n:2 * n]
        send_sem, recv_sem = refs[2 * n + 1], refs[2 * n + 2]
        token = refs[-1]
        for out, _ in _exchange_copies(src_refs, land_refs, send_sem, recv_sem, gather):
            out.start()
        token[...] = jnp.zeros_like(token)

    res = pl.pallas_call(
        body, name=name,
        out_shape=(pltpu.SemaphoreType.DMA(()), pltpu.SemaphoreType.DMA(()),
                   *[pltpu.HBM(a.shape, a.dtype) for a in srcs], *[pltpu.HBM(a.shape, a.dtype) for a in lands],
                   jax.ShapeDtypeStruct((8, LANE), F32)),
        in_specs=[_HBM] * (2 * n) + [pl.BlockSpec(memory_space=pl.ANY)],
        out_specs=(_SEM, _SEM, *[_HBM] * (2 * n), pl.BlockSpec(memory_space=pltpu.VMEM)),
        input_output_aliases={i: 2 + i for i in range(2 * n)},
        compiler_params=pltpu.CompilerParams(has_side_effects=_EFFECT),
    )(*[pltpu.with_memory_space_constraint(a, pltpu.HBM) for a in list(srcs) + list(lands)], after)
    return res[0], res[1], list(res[2:2 + n]), list(res[2 + n:2 + 2 * n]), res[-1]


def exchange_wait(send_sem, recv_sem, srcs, lands, after, gather, name):
    n = len(srcs)

    def body(*refs):
        src_refs, land_refs = refs[:n], refs[n:2 * n]
        s_sem, r_sem = refs[2 * n], refs[2 * n + 1]
        for out, inc in _exchange_copies(src_refs, land_refs, s_sem, r_sem, gather):
            out.wait_send()
            inc.wait_recv()

    res = pl.pallas_call(
        body, name=name,
        out_shape=[pltpu.HBM(a.shape, a.dtype) for a in list(srcs) + list(lands)],
        in_specs=[_HBM] * (2 * n) + [_SEM, _SEM, pl.BlockSpec(memory_space=pl.ANY)],
        out_specs=[_HBM] * (2 * n),
        input_output_aliases={i: i for i in range(2 * n)},
        compiler_params=pltpu.CompilerParams(has_side_effects=_EFFECT),
    )(*srcs, *lands, send_sem, recv_sem, after)
    return list(res[n:])


def matmul(pairs, mode, out_dtype, name, *, a_scale=None, out_scale=None, resid=None,
           save_acc=False, tm=512, tn=512, tk=2048):
    a0, b0 = pairs[0]
    if mode == "nn":
        (m, kdim), n = a0.shape, b0.shape[1]
    elif mode == "nt":
        (m, kdim), n = a0.shape, b0.shape[0]
    else:
        (kdim, m), n = a0.shape, b0.shape[1]
    tm, tn = _tile(m, tm, 8 if m % LANE else LANE), _tile(n, tn, LANE)
    tk = _tile(kdim, tk, LANE)
    nk = kdim // tk
    sub = tn if mode == "tn" else _tile(tn, 256, LANE)
    npairs = len(pairs)
    dims = {"nn": ((1,), (0,)), "nt": ((1,), (1,)), "tn": ((0,), (0,))}[mode]

    if mode == "nn":
        a_spec = pl.BlockSpec((tm, tk), lambda i, j, k: (i, k))
        b_spec = pl.BlockSpec((tk, tn), lambda i, j, k: (k, j))
    elif mode == "nt":
        a_spec = pl.BlockSpec((tm, tk), lambda i, j, k: (i, k))
        b_spec = pl.BlockSpec((tn, tk), lambda i, j, k: (j, k))
    else:
        a_spec = pl.BlockSpec((tk, tm), lambda i, j, k: (k, i))
        b_spec = pl.BlockSpec((tk, tn), lambda i, j, k: (k, j))

    def body(*refs):
        it = iter(refs)
        pair_refs = [(next(it), next(it)) for _ in range(npairs)]
        as_ref = next(it) if a_scale is not None else None
        os_ref = next(it) if out_scale is not None else None
        rs_ref = next(it) if resid is not None else None
        o_ref = next(it)
        acc_out = next(it) if save_acc else None
        acc_ref = next(it) if nk > 1 else None
        lhs = []
        for a_ref, _ in pair_refs:
            a = a_ref[...]
            if as_ref is not None:
                a = a.astype(F32) * as_ref[...]
            lhs.append(a.astype(CD))

        def product(cols):
            part = None
            for a, (_, b_ref) in zip(lhs, pair_refs):
                b = b_ref[...] if cols is None else (b_ref[cols, :] if mode == "nt" else b_ref[:, cols])
                d = _dot(a, b, dims)
                part = d if part is None else part + d
            return part

        def finish(acc, cols=slice(None)):
            if acc_out is not None:
                acc_out[:, cols] = acc
            if os_ref is not None:
                acc = acc * os_ref[:, cols]
            if rs_ref is not None:
                acc = rs_ref[:, cols] + acc
            o_ref[:, cols] = acc.astype(o_ref.dtype)

        if nk == 1:
            for c0 in range(0, tn, sub):
                finish(product(slice(c0, c0 + sub)), slice(c0, c0 + sub))
        else:
            part = product(None)
            k = pl.program_id(2)

            @pl.when(k == 0)
            def _():
                acc_ref[...] = part

            @pl.when(k > 0)
            def _():
                acc_ref[...] += part

            @pl.when(k == nk - 1)
            def _():
                finish(acc_ref[...])

    in_specs, args = [], []
    for a, b in pairs:
        in_specs += [a_spec, b_spec]
        args += [a, b]
    if a_scale is not None:
        assert mode != "tn"
        in_specs.append(pl.BlockSpec((1, tk), lambda i, j, k: (0, k)))
        args.append(a_scale)
    if out_scale is not None:
        in_specs.append(pl.BlockSpec((1, tn), lambda i, j, k: (0, j)))
        args.append(out_scale)
    if resid is not None:
        in_specs.append(pl.BlockSpec((tm, tn), lambda i, j, k: (i, j)))
        args.append(resid)
    o_spec = pl.BlockSpec((tm, tn), lambda i, j, k: (i, j))
    out_shape = [jax.ShapeDtypeStruct((m, n), out_dtype)]
    out_specs = [o_spec]
    if save_acc:
        out_shape.append(jax.ShapeDtypeStruct((m, n), F32))
        out_specs.append(o_spec)
    scratch = [pltpu.VMEM((tm, tn), F32)] if nk > 1 else []
    res = pl.pallas_call(
        body, name=name, grid=(m // tm, n // tn, nk),
        in_specs=in_specs, out_specs=out_specs, out_shape=out_shape, scratch_shapes=scratch,
        compiler_params=_params(("parallel", "parallel", "arbitrary")),
    )(*args)
    return res if save_acc else res[0]


def _nm_fn(x, g, sc, sh):
    r = lax.rsqrt(jnp.mean(x * x, axis=-1, keepdims=True) + EPS)
    return (x * r * g) * (1.0 + sc) + sh


def norm_mod(x, g, sc, sh, name):
    t, d = x.shape
    tr = _tile(t, 256, 8)

    def body(x_ref, g_ref, sc_ref, sh_ref, h_ref):
        h_ref[...] = _nm_fn(x_ref[...], g_ref[...], sc_ref[...], sh_ref[...]).astype(h_ref.dtype)

    row = pl.BlockSpec((tr, d), lambda i: (i, 0))
    vec = pl.BlockSpec((1, d), lambda i: (0, 0))
    return pl.pallas_call(
        body, name=name, grid=(t // tr,), in_specs=[row, vec, vec, vec], out_specs=row,
        out_shape=jax.ShapeDtypeStruct((t, d), CD), compiler_params=_params(("parallel",)),
    )(x, g, sc, sh)


def norm_mod_bwd(x, g, sc, sh, dh, dxo, f, gate_scale, name):
    t, d = x.shape
    tr = _tile(t, 256, 8)

    def body(x_ref, g_ref, sc_ref, sh_ref, dh_ref, dxo_ref, f_ref, dx_ref, dg_ref, dsc_ref, dsh_ref, dgt_ref):
        _, vjp = jax.vjp(_nm_fn, x_ref[...], g_ref[...], sc_ref[...], sh_ref[...])
        dx, dg, dsc, dsh = vjp(dh_ref[...])
        dxo_v = dxo_ref[...]
        dx_ref[...] = dxo_v + dx
        dgt = gate_scale * jnp.sum(f_ref[...] * dxo_v, axis=0, keepdims=True)

        @pl.when(pl.program_id(0) == 0)
        def _():
            dg_ref[...] = dg
            dsc_ref[...] = dsc
            dsh_ref[...] = dsh
            dgt_ref[...] = dgt

        @pl.when(pl.program_id(0) > 0)
        def _():
            dg_ref[...] += dg
            dsc_ref[...] += dsc
            dsh_ref[...] += dsh
            dgt_ref[...] += dgt

    row = pl.BlockSpec((tr, d), lambda i: (i, 0))
    vec = pl.BlockSpec((1, d), lambda i: (0, 0))
    vshape = jax.ShapeDtypeStruct((1, d), F32)
    return pl.pallas_call(
        body, name=name, grid=(t // tr,), in_specs=[row, vec, vec, vec, row, row, row],
        out_specs=[row, vec, vec, vec, vec],
        out_shape=[jax.ShapeDtypeStruct((t, d), F32), vshape, vshape, vshape, vshape],
        compiler_params=_params(("arbitrary",)),
    )(x, g, sc, sh, dh, dxo, f)


HALF = N_DEV // 2


def ffn_up(h, u_all, name):
    t, d = h.shape
    cp = u_all.shape[1]
    f = HALF * cp
    tm, tn = _tile(t, 1024, LANE), cp
    per = cp // tn
    sub = _tile(tn, 256, LANE)

    def body(h_ref, wg_ref, wu_ref, g_ref, u_ref, a_ref):
        hv = h_ref[...]
        for c0 in range(0, tn, sub):
            cols = slice(c0, c0 + sub)
            gate = _nt(hv, wg_ref[0, cols, :])
            up = _nt(hv, wu_ref[0, cols, :])
            g_ref[:, cols] = gate.astype(g_ref.dtype)
            u_ref[:, cols] = up.astype(u_ref.dtype)
            a_ref[:, cols] = (_silu(gate) * up).astype(a_ref.dtype)

    o = pl.BlockSpec((tm, tn), lambda i, j: (i, j))
    wg = pl.BlockSpec((1, tn, d), lambda i, j: (j // per, j % per, 0))
    wu = pl.BlockSpec((1, tn, d), lambda i, j: (HALF + j // per, j % per, 0))
    return pl.pallas_call(
        body, name=name, grid=(t // tm, f // tn),
        in_specs=[pl.BlockSpec((tm, d), lambda i, j: (i, 0)), wg, wu], out_specs=[o, o, o],
        out_shape=[jax.ShapeDtypeStruct((t, f), CD)] * 3,
        compiler_params=_params(("parallel", "parallel")),
    )(h, u_all, u_all)


def ffn_up_wg(h, dgate, dup, cp, name):
    t, d = h.shape
    tn = _tile(d, 1024, LANE)

    def body(h_ref, dg_ref, du_ref, o_ref):
        s = pl.program_id(0)

        @pl.when(s < HALF)
        def _():
            o_ref[0] = _tn(dg_ref[...], h_ref[...]).astype(o_ref.dtype)

        @pl.when(s >= HALF)
        def _():
            o_ref[0] = _tn(du_ref[...], h_ref[...]).astype(o_ref.dtype)

    return pl.pallas_call(
        body, name=name, grid=(N_DEV, d // tn),
        in_specs=[pl.BlockSpec((t, tn), lambda s, i: (0, i)),
                  pl.BlockSpec((t, cp), lambda s, i: (0, jnp.minimum(s, HALF - 1))),
                  pl.BlockSpec((t, cp), lambda s, i: (0, jnp.maximum(s - HALF, 0)))],
        out_specs=pl.BlockSpec((1, cp, tn), lambda s, i: (s, 0, i)),
        out_shape=jax.ShapeDtypeStruct((N_DEV, cp, d), CD),
        compiler_params=_params(("parallel", "parallel")),
    )(h, dgate, dup)


def ffn_up_dg(dgate, dup, u_all, name):
    t, f = dgate.shape
    cp, d = u_all.shape[1], u_all.shape[2]
    tm, tn = _tile(t, 1024, LANE), _tile(d, 512, LANE)

    def body(dg_ref, du_ref, u_ref, o_ref):
        wg = u_ref[0:HALF].reshape(f, tn)
        wu = u_ref[HALF:N_DEV].reshape(f, tn)
        o_ref[...] = _nn(dg_ref[...], wg) + _nn(du_ref[...], wu)

    return pl.pallas_call(
        body, name=name, grid=(t // tm, d // tn),
        in_specs=[pl.BlockSpec((tm, f), lambda i, j: (i, 0)), pl.BlockSpec((tm, f), lambda i, j: (i, 0)),
                  pl.BlockSpec((N_DEV, cp, tn), lambda i, j: (0, 0, j))],
        out_specs=pl.BlockSpec((tm, tn), lambda i, j: (i, j)),
        out_shape=jax.ShapeDtypeStruct((t, d), F32),
        compiler_params=_params(("parallel", "parallel")),
    )(dgate, dup, u_all)


def ffn_dact(dxo, s, wd, gate, up, name):
    t, d = dxo.shape
    f = wd.shape[0]
    tm, tn = _tile(t, 1024, LANE), _tile(f, 768, LANE)
    sub = _tile(tn, 256, LANE)

    def body(dxo_ref, s_ref, wd_ref, g_ref, u_ref, dg_ref, du_ref):
        dxs = (dxo_ref[...] * s_ref[...]).astype(CD)
        for c0 in range(0, tn, sub):
            cols = slice(c0, c0 + sub)
            da = _nt(dxs, wd_ref[cols, :])
            gate, up = g_ref[:, cols].astype(F32), u_ref[:, cols].astype(F32)
            sg = jax.nn.sigmoid(gate)
            dg_ref[:, cols] = (da * up * sg * (1.0 + gate * (1.0 - sg))).astype(dg_ref.dtype)
            du_ref[:, cols] = (da * gate * sg).astype(du_ref.dtype)

    o = pl.BlockSpec((tm, tn), lambda i, j: (i, j))
    return pl.pallas_call(
        body, name=name, grid=(t // tm, f // tn),
        in_specs=[pl.BlockSpec((tm, d), lambda i, j: (i, 0)), pl.BlockSpec((1, d), lambda i, j: (0, 0)),
                  pl.BlockSpec((tn, d), lambda i, j: (j, 0)), o, o],
        out_specs=[o, o],
        out_shape=[jax.ShapeDtypeStruct((t, f), CD), jax.ShapeDtypeStruct((t, f), CD)],
        compiler_params=_params(("parallel", "parallel")),
    )(dxo, s, wd, gate, up)


def _bdot(a, b, ca, cb, precision=None):
    if precision is None:
        a, b = a.astype(CD), b.astype(CD)
    return lax.dot_general(a, b, (((ca,), (cb,)), ((0,), (0,))), precision=precision, preferred_element_type=F32)


def _bnn(a, b, precision=None):
    return _bdot(a, b, 2, 1, precision)


def _bnt(a, b, precision=None):
    return _bdot(a, b, 2, 2, precision)


def _btn(a, b, precision=None):
    return _bdot(a, b, 1, 1, precision)


def _att_fn(q, kp, kc, vp, vc, qg, kg, mask_p, mask_c):
    b, w = q.shape
    nh = w // ATT_HEAD_DIM
    head_of_lane = lax.broadcasted_iota(jnp.int32, (nh, 1, w), 2) // ATT_HEAD_DIM
    hm = (head_of_lane == lax.broadcasted_iota(jnp.int32, (nh, 1, w), 0)).astype(F32)

    def rn(x, g):
        ss = jnp.sum((x * x)[None] * hm, axis=-1, keepdims=True)
        r = jnp.sum(lax.rsqrt(ss * (1.0 / ATT_HEAD_DIM) + EPS) * hm, axis=0)
        return x * r * g

    qn, kcn = rn(q, qg), rn(kc, kg)
    q4 = (qn[None] * hm).reshape(nh * b, w)
    scale = ATT_HEAD_DIM ** -0.5
    sc = jnp.where(mask_c, _nt(q4, kcn) * scale, NEG)
    if kp is None:
        m = jnp.max(sc, axis=-1, keepdims=True)
        pc = jnp.exp(sc - m)
        den = jnp.sum(pc, axis=-1, keepdims=True)
        o4 = _nn(pc / den, vc)
    else:
        sp = jnp.where(mask_p, _nt(q4, rn(kp, kg)) * scale, NEG)
        m = jnp.maximum(jnp.max(sp, axis=-1, keepdims=True), jnp.max(sc, axis=-1, keepdims=True))
        pp, pc = jnp.exp(sp - m), jnp.exp(sc - m)
        den = jnp.sum(pp, axis=-1, keepdims=True) + jnp.sum(pc, axis=-1, keepdims=True)
        o4 = _nn(pp / den, vp) + _nn(pc / den, vc)
    o = jnp.sum(o4.reshape(nh, b, w) * hm, axis=0)
    lse = jnp.sum((m + jnp.log(den)).reshape(nh, b, 1) * hm, axis=0)
    return o, lse


def _att_masks(g, j, nb_total, nh):
    nb = jnp.int32(nb_total // DILATIONS[0])
    for gi in range(1, len(DILATIONS)):
        nb = jnp.where(g == gi, jnp.int32(nb_total // DILATIONS[gi]), nb)
    has_prev = (j % nb) != 0
    row = lax.broadcasted_iota(jnp.int32, (nh * ATT_BLOCK, ATT_BLOCK), 0) % ATT_BLOCK
    col = lax.broadcasted_iota(jnp.int32, (nh * ATT_BLOCK, ATT_BLOCK), 1)
    return has_prev, col >= row, col <= row


def _att_specs(w):
    blk = (1, 1, ATT_BLOCK, w)
    q = pl.BlockSpec(blk, lambda g, j: (0, g, j, 0))
    kp = pl.BlockSpec(blk, lambda g, j: (1, g, jnp.maximum(j - 1, 0), 0))
    kc = pl.BlockSpec(blk, lambda g, j: (1, g, j, 0))
    vp = pl.BlockSpec(blk, lambda g, j: (2, g, jnp.maximum(j - 1, 0), 0))
    vc = pl.BlockSpec(blk, lambda g, j: (2, g, j, 0))
    gain = pl.BlockSpec((1, w), lambda g, j: (0, 0))
    out = pl.BlockSpec((1, ATT_BLOCK, w), lambda g, j: (g, j, 0))
    return [q, kp, kc, vp, vc, gain, gain], out


def att_fwd(qkv, qg, kg, name):
    _, ng, t, w = qkv.shape
    nbt = t // ATT_BLOCK
    in_specs, out = _att_specs(w)

    def body(q_ref, kp_ref, kc_ref, vp_ref, vc_ref, qg_ref, kg_ref, o_ref, lse_ref):
        has_prev, mask_p, mask_c = _att_masks(pl.program_id(0), pl.program_id(1), nbt, w // ATT_HEAD_DIM)

        @pl.when(has_prev)
        def _():
            o, lse = _att_fn(q_ref[0, 0], kp_ref[0, 0], kc_ref[0, 0], vp_ref[0, 0], vc_ref[0, 0],
                             qg_ref[...], kg_ref[...], mask_p, mask_c)
            o_ref[0] = o
            lse_ref[0] = lse

        @pl.when(jnp.logical_not(has_prev))
        def _():
            o, lse = _att_fn(q_ref[0, 0], None, kc_ref[0, 0], None, vc_ref[0, 0],
                             qg_ref[...], kg_ref[...], None, mask_c)
            o_ref[0] = o
            lse_ref[0] = lse

    sh = jax.ShapeDtypeStruct((ng, t, w), F32)
    return pl.pallas_call(
        body, name=name, grid=(ng, nbt), in_specs=in_specs, out_specs=[out, out], out_shape=[sh, sh],
        compiler_params=_params(("parallel", "parallel")),
    )(qkv, qkv, qkv, qkv, qkv, qg, kg)


def att_bwd(qkv, qg, kg, do, dlse, name):
    _, ng, t, w = qkv.shape
    nbt = t // ATT_BLOCK
    in_specs, out = _att_specs(w)
    whole = pl.BlockSpec((1, t, w), lambda g, j: (g, 0, 0))
    gain = in_specs[-1]

    def body(q_ref, kp_ref, kc_ref, vp_ref, vc_ref, qg_ref, kg_ref, do_ref, dlse_ref,
             dq_ref, dk_ref, dv_ref, dqg_ref, dkg_ref):
        g, j = pl.program_id(0), pl.program_id(1)
        has_prev, mask_p, mask_c = _att_masks(g, j, nbt, w // ATT_HEAD_DIM)

        @pl.when(j == 0)
        def _():
            dk_ref[...] = jnp.zeros_like(dk_ref)
            dv_ref[...] = jnp.zeros_like(dv_ref)

        @pl.when(jnp.logical_and(g == 0, j == 0))
        def _():
            dqg_ref[...] = jnp.zeros_like(dqg_ref)
            dkg_ref[...] = jnp.zeros_like(dkg_ref)

        rows_c = pl.ds(pl.multiple_of(j * ATT_BLOCK, ATT_BLOCK), ATT_BLOCK)
        rows_p = pl.ds(pl.multiple_of(jnp.maximum(j - 1, 0) * ATT_BLOCK, ATT_BLOCK), ATT_BLOCK)
        @pl.when(has_prev)
        def _():
            fn = functools.partial(_att_fn, mask_p=mask_p, mask_c=mask_c)
            _, vjp = jax.vjp(fn, q_ref[0, 0], kp_ref[0, 0], kc_ref[0, 0], vp_ref[0, 0], vc_ref[0, 0],
                             qg_ref[...], kg_ref[...])
            dq, dkp, dkc, dvp, dvc, dqg, dkg = vjp((do_ref[0], dlse_ref[0]))
            dq_ref[0] = dq
            dk_ref[0, rows_p, :] += dkp
            dv_ref[0, rows_p, :] += dvp
            dk_ref[0, rows_c, :] += dkc
            dv_ref[0, rows_c, :] += dvc
            dqg_ref[...] += dqg
            dkg_ref[...] += dkg

        @pl.when(jnp.logical_not(has_prev))
        def _():
            def fn(q, kc, vc, qg, kg):
                return _att_fn(q, None, kc, None, vc, qg, kg, None, mask_c)

            _, vjp = jax.vjp(fn, q_ref[0, 0], kc_ref[0, 0], vc_ref[0, 0], qg_ref[...], kg_ref[...])
            dq, dkc, dvc, dqg, dkg = vjp((do_ref[0], dlse_ref[0]))
            dq_ref[0] = dq
            dk_ref[0, rows_c, :] += dkc
            dv_ref[0, rows_c, :] += dvc
            dqg_ref[...] += dqg
            dkg_ref[...] += dkg

    sh = jax.ShapeDtypeStruct((ng, t, w), F32)
    gshape = jax.ShapeDtypeStruct((1, w), F32)
    return pl.pallas_call(
        body, name=name, grid=(ng, nbt), in_specs=in_specs + [out, out],
        out_specs=[out, whole, whole, gain, gain], out_shape=[sh, sh, sh, gshape, gshape],
        compiler_params=_params(("arbitrary", "arbitrary")),
    )(qkv, qkv, qkv, qkv, qkv, qg, kg, do, dlse)


def _combine_fn(o, lse):
    m = jnp.max(lse, axis=0, keepdims=True)
    e = jnp.exp(lse - m)
    w = e / jnp.sum(e, axis=0, keepdims=True)
    return jnp.sum(w * o, axis=0)


def att_combine(o, lse, name):
    ng, t, w = o.shape
    tr = _tile(t, 256, 16)
    spec = pl.BlockSpec((ng, tr, w), lambda i: (0, i, 0))
    y_spec = pl.BlockSpec((tr, w), lambda i: (i, 0))

    def body(o_ref, l_ref, y_ref):
        y_ref[...] = _combine_fn(o_ref[...], l_ref[...]).astype(y_ref.dtype)

    return pl.pallas_call(
        body, name=name, grid=(t // tr,), in_specs=[spec, spec], out_specs=y_spec,
        out_shape=jax.ShapeDtypeStruct((t, w), CD), compiler_params=_params(("parallel",)),
    )(o, lse)


def att_combine_bwd(o, lse, dy, name):
    ng, t, w = o.shape
    tr = _tile(t, 256, 8)
    spec = pl.BlockSpec((ng, tr, w), lambda i: (0, i, 0))
    y_spec = pl.BlockSpec((tr, w), lambda i: (i, 0))

    def body(o_ref, l_ref, dy_ref, do_ref, dl_ref):
        _, vjp = jax.vjp(_combine_fn, o_ref[...], l_ref[...])
        do, dl = vjp(dy_ref[...])
        do_ref[...] = do
        dl_ref[...] = dl

    sh = jax.ShapeDtypeStruct(o.shape, F32)
    return pl.pallas_call(
        body, name=name, grid=(t // tr,), in_specs=[spec, spec, y_spec], out_specs=[spec, spec],
        out_shape=[sh, sh], compiler_params=_params(("parallel",)),
    )(o, lse, dy)


def _shift_down(x, s):
    if s == 0:
        return x
    row = lax.broadcasted_iota(jnp.int32, x.shape, 0)
    return jnp.where(row >= s, pltpu.roll(x, s, 0), 0.0)


def _shift_up(x, s):
    if s == 0:
        return x
    t = x.shape[0]
    row = lax.broadcasted_iota(jnp.int32, x.shape, 0)
    return jnp.where(row < t - s, pltpu.roll(x, t - s, 0), 0.0)


def conv_fwd(z, col0, w, name):
    t = z.shape[0]
    c = w.shape[1]
    cb = _tile(c, 512, LANE)
    assert col0 % cb == 0
    nblk0 = col0 // cb

    def body(z_ref, w_ref, c_ref):
        for c0 in range(0, cb, LANE):
            cols = slice(c0, c0 + LANE)
            zv = z_ref[:, cols]
            acc = None
            for i in range(CONV_WIDTH):
                term = _shift_down(zv, CONV_WIDTH - 1 - i) * w_ref[i:i + 1, cols]
                acc = term if acc is None else acc + term
            c_ref[:, cols] = acc

    return pl.pallas_call(
        body, name=name, grid=(c // cb,),
        in_specs=[pl.BlockSpec((t, cb), lambda j: (0, nblk0 + j)), pl.BlockSpec((CONV_WIDTH, cb), lambda j: (0, j))],
        out_specs=pl.BlockSpec((t, cb), lambda j: (0, j)),
        out_shape=jax.ShapeDtypeStruct((t, c), F32), compiler_params=_params(("parallel",)),
    )(z, w)


def conv_bwd(dc, z, col0, w, name):
    t = z.shape[0]
    c = w.shape[1]
    cb = _tile(c, 512, LANE)
    assert col0 % cb == 0
    nblk0 = col0 // cb

    def body(dc_ref, z_ref, w_ref, dz_ref, dw_ref):
        for c0 in range(0, cb, LANE):
            cols = slice(c0, c0 + LANE)
            dcv, zv = dc_ref[:, cols], z_ref[:, cols]
            acc = None
            for i in range(CONV_WIDTH):
                s = CONV_WIDTH - 1 - i
                term = _shift_up(dcv, s) * w_ref[i:i + 1, cols]
                acc = term if acc is None else acc + term
                dw_ref[i:i + 1, cols] = jnp.sum(dcv * _shift_down(zv, s), axis=0, keepdims=True)
            dz_ref[:, cols] = acc.astype(dz_ref.dtype)

    blk = pl.BlockSpec((t, cb), lambda j: (0, j))
    wblk = pl.BlockSpec((CONV_WIDTH, cb), lambda j: (0, j))
    return pl.pallas_call(
        body, name=name, grid=(c // cb,),
        in_specs=[blk, pl.BlockSpec((t, cb), lambda j: (0, nblk0 + j)), wblk], out_specs=[blk, wblk],
        out_shape=[jax.ShapeDtypeStruct((t, c), CD), jax.ShapeDtypeStruct((CONV_WIDTH, c), F32)],
        compiler_params=_params(("parallel",)),
    )(dc, z, w)


def _dn_consts():
    c = DN_CHUNK
    row = lax.broadcasted_iota(jnp.int32, (c, c), 0)
    col = lax.broadcasted_iota(jnp.int32, (c, c), 1)
    return dict(tril=row >= col, strict=row > col, eye=(row == col).astype(F32),
                tril_f=(row >= col).astype(F32), triu_f=(row <= col).astype(F32))


def _softplus(x):
    return jnp.maximum(x, 0.0) + jnp.log(1.0 + jnp.exp(-jnp.abs(x)))


def _split(x):
    hi = x.astype(CD)
    return hi, (x - hi.astype(F32)).astype(CD)


def _bdot3(a, b, ca, cb):
    ah, al = _split(a)
    bh, bl = _split(b)
    return _bdot(ah, bh, ca, cb) + (_bdot(ah, bl, ca, cb) + _bdot(al, bh, ca, cb))


def _tri_inv_impl(a_mat):
    c = a_mat.shape[-1]
    eye = (lax.broadcasted_iota(jnp.int32, (c, c), 0) == lax.broadcasted_iota(jnp.int32, (c, c), 1)).astype(F32)
    nk_ = -a_mat
    t_inv = eye + nk_
    for _ in range(c.bit_length() - 2):
        nk_ = _bdot3(nk_, nk_, 2, 1)
        t_inv = t_inv + _bdot3(t_inv, nk_, 2, 1)
    return t_inv


@jax.custom_vjp
def _tri_inv(a_mat):
    return _tri_inv_impl(a_mat)


def _tri_inv_fwd(a_mat):
    t_inv = _tri_inv_impl(a_mat)
    return t_inv, t_inv


def _tri_inv_bwd(t_inv, dt_inv):
    return (-_bdot3(_bdot3(t_inv, dt_inv, 1, 1), t_inv, 2, 2),)


_tri_inv.defvjp(_tri_inv_fwd, _tri_inv_bwd)


@jax.custom_vjp
def _tri_inv_saved(a_mat, t_inv):
    return t_inv


def _tri_inv_saved_fwd(a_mat, t_inv):
    return t_inv, t_inv


def _tri_inv_saved_bwd(t_inv, dt_inv):
    return _tri_inv_bwd(t_inv, dt_inv)[0], jnp.zeros_like(t_inv)


_tri_inv_saved.defvjp(_tri_inv_saved_fwd, _tri_inv_saved_bwd)


def _dn_chunk(cq, ck, cv, og, a_col, b_col, al, dt, gn, s_prev, *, k, inv, with_inv=False):
    q = _silu(cq)
    q = q * lax.rsqrt(jnp.sum(q * q, axis=-1, keepdims=True) + EPS) * (DN_HEAD_DIM ** -0.5)
    kk = _silu(ck)
    kk = kk * lax.rsqrt(jnp.sum(kk * kk, axis=-1, keepdims=True) + EPS)
    v = _silu(cv)
    g = -jnp.exp(al) * _softplus(a_col + dt)
    beta = jax.nn.sigmoid(b_col)
    g_row = jnp.sum(k["eye"] * g, axis=1, keepdims=True)
    gc_col = jnp.sum(k["tril_f"] * g_row, axis=2, keepdims=True)
    gc_row = jnp.sum(k["triu_f"] * g, axis=1, keepdims=True)
    ldec = jnp.where(k["tril"], jnp.exp(jnp.where(k["tril"], gc_col - gc_row, 0.0)), 0.0)
    kb, vb = kk * beta, v * beta
    a_mat = jnp.where(k["strict"], _bnt(kb, kk) * ldec, 0.0)
    t_inv = inv(a_mat)
    egc = jnp.exp(gc_col)
    u = _bnn(t_inv, vb)
    w = _bnn(t_inv, kb * egc)
    attn = jnp.where(k["tril"], _bnt(q, kk) * ldec, 0.0)
    gc_last = jnp.sum(g, axis=1, keepdims=True)
    k_dec = kk * jnp.exp(gc_last - gc_col)
    v_new = u - _bnn(w, s_prev)
    o = _bnn(q * egc, s_prev) + _bnn(attn, v_new)
    s_new = s_prev * jnp.exp(gc_last) + _btn(k_dec, v_new)
    y = o * lax.rsqrt(jnp.mean(o * o, axis=-1, keepdims=True) + EPS) * gn * _silu(og)
    return (y, s_new, t_inv) if with_inv else (y, s_new)


def _dn_heads(ref, nh):
    hd = DN_HEAD_DIM
    return jnp.stack([ref[:, h * hd:(h + 1) * hd] for h in range(nh)])


def _dn_specs(nh, col_gate, col_ab, order):
    hd, c = DN_HEAD_DIM, DN_CHUNK
    w = nh * hd
    qs = pl.BlockSpec((c, w), lambda n: (order(n), 0))
    ks = pl.BlockSpec((c, w), lambda n: (order(n), 1))
    vs = pl.BlockSpec((c, w), lambda n: (order(n), 2))
    gs = pl.BlockSpec((c, w), lambda n: (order(n), col_gate // w))
    ab = pl.BlockSpec((c, LANE), lambda n: (order(n), col_ab // LANE))
    scal = pl.BlockSpec((nh, 1, 1), lambda n: (0, 0, 0))
    gn = pl.BlockSpec((1, hd), lambda n: (0, 0))
    st = pl.BlockSpec((1, nh, hd, hd), lambda n: (order(n), 0, 0, 0))
    return qs, ks, vs, gs, ab, scal, gn, st


def _lane_pick(x, idx):
    lane = lax.broadcasted_iota(jnp.int32, x.shape, 1)
    return jnp.sum(jnp.where(lane == idx, x, 0.0), axis=1, keepdims=True)


def dn_fwd(cv, z, col_gate, col_ab, a_log, dt_bias, gn, name):
    t = cv.shape[0]
    nh = a_log.shape[0]
    hd, c = DN_HEAD_DIM, DN_CHUNK
    n_chunks = t // c
    qs, ks, vs, gs, ab, scal, gnspec, st = _dn_specs(nh, col_gate, col_ab, lambda n: n)

    def body(q_ref, k_ref, v_ref, g_ref, ab_ref, al_ref, dt_ref, gn_ref, y_ref, st_ref, ti_ref, s_scr):
        @pl.when(pl.program_id(0) == 0)
        def _():
            s_scr[...] = jnp.zeros_like(s_scr)

        abv = ab_ref[...]
        a_col = jnp.stack([_lane_pick(abv, h) for h in range(nh)])
        b_col = jnp.stack([_lane_pick(abv, nh + h) for h in range(nh)])
        s_prev = s_scr[...]
        st_ref[0] = s_prev
        y, s_new, t_inv = _dn_chunk(
            _dn_heads(q_ref, nh), _dn_heads(k_ref, nh), _dn_heads(v_ref, nh), _dn_heads(g_ref, nh),
            a_col, b_col, al_ref[...], dt_ref[...], gn_ref[...], s_prev,
            k=_dn_consts(), inv=_tri_inv_impl, with_inv=True)
        for h in range(nh):
            y_ref[:, h * hd:(h + 1) * hd] = y[h].astype(y_ref.dtype)
        ti_ref[0] = t_inv
        s_scr[...] = s_new

    return pl.pallas_call(
        body, name=name, grid=(n_chunks,),
        in_specs=[qs, ks, vs, gs, ab, scal, scal, gnspec],
        out_specs=[pl.BlockSpec((c, nh * hd), lambda n: (n, 0)), st,
                   pl.BlockSpec((1, nh, c, c), lambda n: (n, 0, 0, 0))],
        out_shape=[jax.ShapeDtypeStruct((t, nh * hd), CD), jax.ShapeDtypeStruct((n_chunks, nh, hd, hd), F32),
                   jax.ShapeDtypeStruct((n_chunks, nh, c, c), F32)],
        scratch_shapes=[pltpu.VMEM((nh, hd, hd), F32)],
        compiler_params=_params(("arbitrary",)),
    )(cv, cv, cv, z, z, a_log, dt_bias, gn)


def dn_bwd(cv, z, col_gate, col_ab, a_log, dt_bias, gn, states, t_invs, dy, name):
    t = cv.shape[0]
    nh = a_log.shape[0]
    hd, c = DN_HEAD_DIM, DN_CHUNK
    w = nh * hd
    n_chunks = t // c
    rev = lambda n: n_chunks - 1 - n
    qs, ks, vs, gs, ab, scal, gnspec, st = _dn_specs(nh, col_gate, col_ab, rev)
    yspec = pl.BlockSpec((c, w), lambda n: (rev(n), 0))

    def body(q_ref, k_ref, v_ref, g_ref, ab_ref, al_ref, dt_ref, gn_ref, st_ref, ti_ref, dy_ref,
             dc_ref, dg_ref, dab_ref, dal_ref, ddt_ref, dgn_ref, ds_scr):
        @pl.when(pl.program_id(0) == 0)
        def _():
            ds_scr[...] = jnp.zeros_like(ds_scr)
            dal_ref[...] = jnp.zeros_like(dal_ref)
            ddt_ref[...] = jnp.zeros_like(ddt_ref)
            dgn_ref[...] = jnp.zeros_like(dgn_ref)

        abv = ab_ref[...]
        a_col = jnp.stack([_lane_pick(abv, h) for h in range(nh)])
        b_col = jnp.stack([_lane_pick(abv, nh + h) for h in range(nh)])
        t_inv = ti_ref[0]
        fn = functools.partial(_dn_chunk, k=_dn_consts(), inv=lambda a_mat: _tri_inv_saved(a_mat, t_inv))
        _, vjp = jax.vjp(fn, _dn_heads(q_ref, nh), _dn_heads(k_ref, nh), _dn_heads(v_ref, nh), _dn_heads(g_ref, nh),
                         a_col, b_col, al_ref[...], dt_ref[...], gn_ref[...], st_ref[0])
        dq, dk, dv, dg, da, db, dal, ddt, dgn, ds = vjp((_dn_heads(dy_ref, nh), ds_scr[...]))
        lane = lax.broadcasted_iota(jnp.int32, (c, LANE), 1)
        dab = jnp.zeros((c, LANE), F32)
        for h in range(nh):
            cols = slice(h * hd, (h + 1) * hd)
            dc_ref[:, cols] = dq[h]
            dc_ref[:, w + h * hd:w + (h + 1) * hd] = dk[h]
            dc_ref[:, 2 * w + h * hd:2 * w + (h + 1) * hd] = dv[h]
            dg_ref[:, cols] = dg[h].astype(dg_ref.dtype)
            dab = dab + jnp.where(lane == h, da[h], 0.0) + jnp.where(lane == nh + h, db[h], 0.0)
        dab_ref[...] = dab.astype(dab_ref.dtype)
        dal_ref[...] += dal
        ddt_ref[...] += ddt
        dgn_ref[...] += dgn
        ds_scr[...] = ds

    sshape = jax.ShapeDtypeStruct((nh, 1, 1), F32)
    res = pl.pallas_call(
        body, name=name, grid=(n_chunks,),
        in_specs=[qs, ks, vs, gs, ab, scal, scal, gnspec, st,
                  pl.BlockSpec((1, nh, c, c), lambda n: (rev(n), 0, 0, 0)), yspec],
        out_specs=[pl.BlockSpec((c, 3 * w), lambda n: (rev(n), 0)), yspec,
                   pl.BlockSpec((c, LANE), lambda n: (rev(n), 0)), scal, scal, gnspec],
        out_shape=[jax.ShapeDtypeStruct((t, 3 * w), F32), jax.ShapeDtypeStruct((t, w), CD),
                   jax.ShapeDtypeStruct((t, LANE), CD), sshape, sshape, jax.ShapeDtypeStruct((1, hd), F32)],
        scratch_shapes=[pltpu.VMEM((nh, hd, hd), F32)],
        compiler_params=_params(("arbitrary",)),
    )(cv, cv, cv, z, z, a_log, dt_bias, gn, states, t_invs, dy)
    return res


def merge_fwd(ya, wpa, yd, wpd, z, col_m, name):
    t, d = yd.shape[0], wpd.shape[1]
    tm, tn = _tile(t, 512, LANE), _tile(d, 256, LANE)
    nb1, nb2 = col_m // tn, (col_m + d) // tn

    def body(ya_ref, wpa_ref, yd_ref, wpd_ref, z1_ref, z2_ref, m_ref, pa_ref, pd_ref):
        pa = _nn(ya_ref[...], wpa_ref[...])
        pd = _nn(yd_ref[...], wpd_ref[...])
        pa_ref[...] = pa
        pd_ref[...] = pd
        m_ref[...] = (jax.nn.sigmoid(z1_ref[...]) * pa + jax.nn.sigmoid(z2_ref[...]) * pd).astype(m_ref.dtype)

    o = pl.BlockSpec((tm, tn), lambda i, j: (i, j))
    return pl.pallas_call(
        body, name=name, grid=(t // tm, d // tn),
        in_specs=[pl.BlockSpec((tm, ya.shape[1]), lambda i, j: (i, 0)),
                  pl.BlockSpec((wpa.shape[0], tn), lambda i, j: (0, j)),
                  pl.BlockSpec((tm, yd.shape[1]), lambda i, j: (i, 0)),
                  pl.BlockSpec((wpd.shape[0], tn), lambda i, j: (0, j)),
                  pl.BlockSpec((tm, tn), lambda i, j: (i, nb1 + j)),
                  pl.BlockSpec((tm, tn), lambda i, j: (i, nb2 + j))],
        out_specs=[o, o, o],
        out_shape=[jax.ShapeDtypeStruct((t, d), CD), jax.ShapeDtypeStruct((t, d), F32),
                   jax.ShapeDtypeStruct((t, d), F32)],
        compiler_params=_params(("parallel", "parallel")),
    )(ya, wpa, yd, wpd, z, z)


def merge_bwd(dm, pa, pd, z, col_m, name):
    t, d = dm.shape
    tm, tn = _tile(t, 512, 8), _tile(d, 256, LANE)
    nb1, nb2 = col_m // tn, (col_m + d) // tn

    def body(dm_ref, pa_ref, pd_ref, z1_ref, z2_ref, dpa_ref, dpd_ref, dz1_ref, dz2_ref):
        dmv = dm_ref[...]
        s1, s2 = jax.nn.sigmoid(z1_ref[...]), jax.nn.sigmoid(z2_ref[...])
        dpa_ref[...] = (dmv * s1).astype(dpa_ref.dtype)
        dpd_ref[...] = (dmv * s2).astype(dpd_ref.dtype)
        dz1_ref[...] = (dmv * pa_ref[...] * s1 * (1.0 - s1)).astype(dz1_ref.dtype)
        dz2_ref[...] = (dmv * pd_ref[...] * s2 * (1.0 - s2)).astype(dz2_ref.dtype)

    o = pl.BlockSpec((tm, tn), lambda i, j: (i, j))
    sh = jax.ShapeDtypeStruct((t, d), CD)
    return pl.pallas_call(
        body, name=name, grid=(t // tm, d // tn),
        in_specs=[o, o, o, pl.BlockSpec((tm, tn), lambda i, j: (i, nb1 + j)),
                  pl.BlockSpec((tm, tn), lambda i, j: (i, nb2 + j))],
        out_specs=[o, o, o, o], out_shape=[sh, sh, sh, sh],
        compiler_params=_params(("parallel", "parallel")),
    )(dm, pa, pd, z, z)


def ada_fwd(c_all, w, b, name):
    nl, d, n = w.shape
    tn = _tile(n, 384, LANE)

    def body(c_ref, w_ref, b_ref, o_ref):
        o_ref[0] = _nn(_silu(c_ref[...]), w_ref[0]) + b_ref[0]

    return pl.pallas_call(
        body, name=name, grid=(nl, n // tn),
        in_specs=[pl.BlockSpec(c_all.shape, lambda l, j: (0, 0)), pl.BlockSpec((1, d, tn), lambda l, j: (l, 0, j)),
                  pl.BlockSpec((1, 1, tn), lambda l, j: (l, 0, j))],
        out_specs=pl.BlockSpec((1, c_all.shape[0], tn), lambda l, j: (l, 0, j)),
        out_shape=jax.ShapeDtypeStruct((nl, c_all.shape[0], n), F32),
        compiler_params=_params(("parallel", "parallel")),
    )(c_all, w, b)


def ada_bwd(c_pad, dmod_pad, name):
    nl, kp, n = dmod_pad.shape
    d = c_pad.shape[1]
    tn = _tile(n, 384, LANE)

    def body(c_ref, g_ref, o_ref):
        o_ref[0] = _tn(_silu(c_ref[...]), g_ref[0])

    return pl.pallas_call(
        body, name=name, grid=(nl, n // tn),
        in_specs=[pl.BlockSpec((kp, d), lambda l, j: (0, 0)), pl.BlockSpec((1, kp, tn), lambda l, j: (l, 0, j))],
        out_specs=pl.BlockSpec((1, d, tn), lambda l, j: (l, 0, j)),
        out_shape=jax.ShapeDtypeStruct((nl, d, n), F32),
        compiler_params=_params(("parallel", "parallel")),
    )(c_pad, dmod_pad)


def loss_head(y, target, name):
    t, d = y.shape
    tr = _tile(t, 256, 8)

    def body(y_ref, t_ref, dy_ref, l_ref):
        err = y_ref[...] - t_ref[...]
        dy_ref[...] = err * (1.0 / d)
        part = jnp.sum(jnp.sum(err * err, axis=1, keepdims=True), axis=0, keepdims=True) * (0.5 / d)

        @pl.when(pl.program_id(0) == 0)
        def _():
            l_ref[...] = jnp.zeros_like(l_ref)

        l_ref[...] += part

    row = pl.BlockSpec((tr, d), lambda i: (i, 0))
    return pl.pallas_call(
        body, name=name, grid=(t // tr,), in_specs=[row, row],
        out_specs=[row, pl.BlockSpec((8, LANE), lambda i: (0, 0))],
        out_shape=[jax.ShapeDtypeStruct((t, d), F32), jax.ShapeDtypeStruct((8, LANE), F32)],
        compiler_params=_params(("arbitrary",)),
    )(y, target)


def _adamw_update(g, w_ref, m_ref, v_ref, go_ref, d_ref, mo_ref, vo_ref):
    m_new = ADAM_B1 * m_ref[...] + (1.0 - ADAM_B1) * g
    v_new = ADAM_B2 * v_ref[...] + (1.0 - ADAM_B2) * jnp.square(g)
    m_hat = m_new / (1.0 - ADAM_B1 ** ADAM_STEP)
    v_hat = v_new / (1.0 - ADAM_B2 ** ADAM_STEP)
    go_ref[...] = g
    d_ref[...] = -ADAM_LR * (m_hat / (jnp.sqrt(v_hat) + ADAM_EPS) + ADAM_WD * w_ref[...])
    mo_ref[...] = m_new
    vo_ref[...] = v_new


def adamw(w, m, v, g_slots, name):
    shape = w.shape
    nslot = g_slots.shape[0]
    if w.ndim == 2:
        w3, m3, v3, g4 = w[None], m[None], v[None], g_slots[:, None]
    else:
        w3, m3, v3, g4 = w, m, v, g_slots
    nl, r, c = w3.shape
    tr = _tile(r, 256, 16)

    def body(w_ref, m_ref, v_ref, g_ref, *outs):
        g = g_ref[0].astype(F32)
        for s in range(1, nslot):
            g = g + g_ref[s].astype(F32)
        _adamw_update(g, w_ref, m_ref, v_ref, *outs)

    blk = pl.BlockSpec((1, tr, c), lambda l, i: (l, i, 0))
    gblk = pl.BlockSpec((nslot, 1, tr, c), lambda l, i: (0, l, i, 0))
    sh = jax.ShapeDtypeStruct(w3.shape, F32)
    outs = pl.pallas_call(
        body, name=name, grid=(nl, r // tr), in_specs=[blk, blk, blk, gblk], out_specs=[blk] * 4,
        out_shape=[sh] * 4, compiler_params=_params(("parallel", "parallel")),
    )(w3, m3, v3, g4)
    return tuple(o.reshape(shape) for o in outs)


def adamw_layers(w, m, v, g_layers, first, name, prev=None):
    _, r, c = w.shape
    n = len(g_layers)
    nslot, _, cg = g_layers[0].shape
    tr = _tile(r, 256, 16)

    def body(w_ref, m_ref, v_ref, *rest):
        g_refs, outs = rest[:n], rest[-4:]
        for li in range(n):
            @pl.when(pl.program_id(0) == li)
            def _(g_ref=g_refs[li]):
                g = g_ref[0, :, :c].astype(F32)
                for s in range(1, nslot):
                    g = g + g_ref[s, :, :c].astype(F32)
                _adamw_update(g[None], w_ref, m_ref, v_ref, *outs)

    blk = pl.BlockSpec((1, tr, c), lambda l, i: (first + l, i, 0))
    gblks = [pl.BlockSpec((nslot, tr, cg), lambda l, i, li=li: (0, jnp.where(l == li, i, 0), 0)) for li in range(n)]
    sh = jax.ShapeDtypeStruct(w.shape, F32)
    extra = [] if prev is None else list(prev)
    return tuple(pl.pallas_call(
        body, name=name, grid=(n, r // tr),
        in_specs=[blk, blk, blk] + gblks + [pl.BlockSpec(memory_space=pl.ANY)] * len(extra),
        out_specs=[blk] * 4, out_shape=[sh] * 4,
        input_output_aliases={3 + n + k: k for k in range(len(extra))},
        compiler_params=_params(("arbitrary", "arbitrary")),
    )(w, m, v, *g_layers, *extra))


def _rows(flat, unit=16):
    n = flat.shape[0]
    per = 1024 * unit
    pad = (-n) % per
    if pad:
        flat = jnp.concatenate([flat, jnp.zeros((pad,), flat.dtype)])
    return flat.reshape(-1, 1024)


def _to_classes(a):
    t, w = a.shape[1], a.shape[2]
    return jnp.stack([a[gi].reshape(t // dil, dil, w).transpose(1, 0, 2).reshape(t, w)
                      for gi, dil in enumerate(DILATIONS)])


def _from_classes(a):
    t, w = a.shape[1], a.shape[2]
    return jnp.stack([a[gi].reshape(dil, t // dil, w).transpose(1, 0, 2).reshape(t, w)
                      for gi, dil in enumerate(DILATIONS)])


def _unshard_cols(g):
    return g.transpose(1, 0, 2).reshape(g.shape[1], -1)


def _shard_cols(full):
    r = full.shape[0]
    return full.reshape(r, N_DEV, -1).transpose(1, 0, 2)


def kernel(x, c, ada_w, ada_b, norm_ff1, ffn1_w_up, ffn1_w_down, norm_mix, w_in, q_norm, k_norm, conv_w, a_log, dt_bias, dn_norm, w_proj_att, w_proj_dn, w_out, norm_ff2, ffn2_w_up, ffn2_w_down, loss_target, m_ada_w, m_ada_b, m_norm_ff1, m_ffn1_w_up, m_ffn1_w_down, m_norm_mix, m_w_in, m_q_norm, m_k_norm, m_conv_w, m_a_log, m_dt_bias, m_dn_norm, m_w_proj_att, m_w_proj_dn, m_w_out, m_norm_ff2, m_ffn2_w_up, m_ffn2_w_down, v_ada_w, v_ada_b, v_norm_ff1, v_ffn1_w_up, v_ffn1_w_down, v_norm_mix, v_w_in, v_q_norm, v_k_norm, v_conv_w, v_a_log, v_dt_bias, v_dn_norm, v_w_proj_att, v_w_proj_dn, v_w_out, v_norm_ff2, v_ffn2_w_up, v_ffn2_w_down):
    nl = ada_w.shape[0]
    t, d = x.shape[1], x.shape[2]
    dff = ffn1_w_down.shape[1] * N_DEV
    ha = d // 256
    ng = len(DILATIONS)
    wa = ha * ATT_HEAD_DIM
    att_w = ng * wa
    nh = d // DN_HEAD_DIM
    dn_w = nh * DN_HEAD_DIM
    n_in = w_in.shape[2] * N_DEV
    off_dn, off_gate = 3 * att_w, 3 * att_w + 3 * dn_w
    off_a = off_gate + dn_w
    off_merge = off_a + 2 * nh
    assert off_merge + 2 * d == n_in
    col_dn, col_gate, col_m = 0, 3 * dn_w, 4 * dn_w
    col_att = col_m + 2 * d
    col_ab = col_att + 3 * att_w
    zw = col_ab + 2 * LANE
    me = 4 * lax.axis_index("x") + 2 * lax.axis_index("y") + lax.axis_index("c")
    xs = x[0]
    target = loss_target[0]

    conv_rows = _rows(conv_w.reshape(-1), 8)
    pack0 = jnp.concatenate([jnp.concatenate([c, jnp.zeros((7, d), F32)]).reshape(-1), conv_rows.reshape(-1)])
    pack0 = _rows(pack0, 8)
    g0 = all_gather([pack0], "ag_c_conv")[0].reshape(N_DEV, -1)
    c_all = g0[:, :d]
    cw = g0[:, 8 * d:8 * d + conv_w.size].reshape(N_DEV, nl, CONV_WIDTH, -1)
    conv_full = cw.transpose(1, 2, 0, 3).reshape(nl, CONV_WIDTH, 3 * dn_w)

    n_ada = ada_w.shape[2]
    b_mine = lax.dynamic_slice_in_dim(ada_b, me * n_ada, n_ada, axis=1)[:, None, :]
    mod_s = ada_fwd(c_all, ada_w, b_mine, "ada_fwd")
    gm = all_gather([mod_s], "ag_mod")[0]
    mod = lax.dynamic_index_in_dim(gm, me, axis=2, keepdims=False)
    mod = mod.transpose(1, 0, 2).reshape(nl, N_ADA, 1, d)

    kinds = [ffn1_w_up, ffn1_w_down, w_in, w_proj_att, w_proj_dn, w_out, ffn2_w_up, ffn2_w_down]
    c_up = ffn1_w_up.shape[2]
    cp = -(-c_up // LANE) * LANE
    r_dn = ffn1_w_down.shape[1]
    assert 2 * r_dn == c_up

    up1_t, up2_t = jnp.swapaxes(ffn1_w_up, 1, 2), jnp.swapaxes(ffn2_w_up, 1, 2)
    w_in_t = jnp.transpose(w_in, (2, 0, 1))

    def layer_shards(l):
        def pad_up(wt):
            return jnp.pad(wt.astype(CD), ((0, cp - c_up), (0, 0)))
        return [pad_up(up1_t[l]), ffn1_w_down[l].astype(CD), w_in_t[:, l].astype(CD), w_proj_att[l].astype(CD),
                w_proj_dn[l].astype(CD), w_out[l].astype(CD), pad_up(up2_t[l]), ffn2_w_down[l].astype(CD)]

    def gather_landing(shards):
        return [lax.dynamic_update_index_in_dim(lax.empty((N_DEV,) + s.shape, s.dtype), s, me, 0) for s in shards]

    def slot_landing(arrs):
        return [lax.dynamic_update_index_in_dim(lax.empty(a.shape, a.dtype),
                                                lax.dynamic_index_in_dim(a, me, 0, keepdims=False), me, 0) for a in arrs]

    gathered = all_gather(layer_shards(0), "ag_weights0")
    prefetch = None

    def full_weight(ki, l):
        blk = gathered[ki]
        if ki == 3:
            return _unshard_cols(blk)
        if ki == 2:
            return w_in_rows(blk.reshape(n_in, d))
        if ki in (1, 7):
            pairs = blk.reshape(HALF, c_up, d)
            return jnp.pad(pairs, ((0, 0), (0, cp - c_up), (0, 0))).reshape(HALF * cp, d)
        return blk.reshape(-1, blk.shape[2])

    def down_grad_slots(g):
        return g.reshape(HALF, cp, d)[:, :c_up].reshape(N_DEV, r_dn, d)

    def w_in_rows(wt):
        pad = jnp.zeros((zw - n_in, wt.shape[1]), wt.dtype)
        return jnp.concatenate([wt[off_dn:off_a], wt[off_merge:], wt[:off_dn], wt[off_a:off_merge], pad], axis=0)

    def w_in_cols_inv(g):
        return jnp.concatenate([g[:, col_att:col_ab], g[:, :col_m], g[:, col_ab:col_ab + 2 * nh], g[:, col_m:col_att]], axis=1)

    saved = []
    xc = xs
    mods = []
    for l in range(nl):
        sv = {}
        if prefetch is not None:
            gathered = exchange_wait(*prefetch[:4], xc, True, f"ag_wait{l}")
        mod_l = mod[l]
        if l + 1 < nl:
            shards = layer_shards(l + 1)
            behind = gathered[0][0, :1, :1].astype(F32) + mod[0, 0, :, :1]
            prefetch = exchange_start(shards, gather_landing(shards), behind, True, f"ag_start{l + 1}")
            mod_l = mod_l + prefetch[4][0, 0]
        mods.append(mod_l)
        sh1, sc1, gt1, sh2, sc2, gt2, sh3, sc3, gt3 = [mod_l[i] for i in range(N_ADA)]
        w_dn1, w_dn2 = full_weight(1, l), full_weight(7, l)
        win = full_weight(2, l)
        wpa, wpd, wo = full_weight(3, l), full_weight(4, l), full_weight(5, l)
        sv["w"] = (gathered[0], gathered[6], w_dn1, w_dn2, win, wpa, wpd, wo)

        def ffn(xin, g, sh, sc, gt, u_all, w_dn):
            h = norm_mod(xin, g, sc, sh, "norm_mod")
            gate, up, a = ffn_up(h, u_all, "ffn_up")
            xo, f = matmul([(a, w_dn)], "nn", F32, "ffn_down", out_scale=0.5 * gt, resid=xin, save_acc=True,
                           tm=1024, tk=4096)
            return xo, (xin, h, gate, up, a, f)

        xc, sv["ffn1"] = ffn(xc, norm_ff1[l:l + 1], sh1, sc1, gt1, gathered[0], w_dn1)

        x_mix = xc
        h2 = norm_mod(x_mix, norm_mix[l:l + 1], sc2, sh2, "norm_mod")
        z = matmul([(h2, win)], "nt", F32, "w_in", tm=1024, tn=512)
        z_att = z[:, col_att:col_ab].reshape(t, 3, ng, wa).transpose(1, 2, 0, 3)
        qkv = jnp.stack([_to_classes(z_att[i]) for i in range(3)])
        qg, kg = jnp.tile(q_norm[l:l + 1], (1, ha)), jnp.tile(k_norm[l:l + 1], (1, ha))
        o_cls, lse_cls = att_fwd(qkv, qg, kg, "att_fwd")
        o_tok, lse_tok = _from_classes(o_cls), _from_classes(lse_cls)
        ya = att_combine(o_tok, lse_tok, "att_combine")
        cvo = conv_fwd(z, col_dn, conv_full[l], "conv_fwd")
        al3, dt3 = a_log[l].reshape(nh, 1, 1), dt_bias[l].reshape(nh, 1, 1)
        gn = dn_norm[l:l + 1]
        yd, *states = dn_fwd(cvo, z, col_gate, col_ab, al3, dt3, gn, "dn_fwd")
        mrg, pa, pd = merge_fwd(ya, wpa, yd, wpd, z, col_m, "merge_fwd")
        xc, f2 = matmul([(mrg, wo)], "nn", F32, "w_out", out_scale=gt2, resid=x_mix, save_acc=True)
        sv["mix"] = (x_mix, h2, z, qkv, o_tok, lse_tok, ya, cvo, yd, states, mrg, pa, pd, f2)

        xc, sv["ffn2"] = ffn(xc, norm_ff2[l:l + 1], sh3, sc3, gt3, gathered[6], w_dn2)
        saved.append(sv)

    dxc, loss_blk = loss_head(xc, target, "loss_head")

    def slots(ki, g):
        if ki in (0, 1, 6, 7):
            return g
        return _shard_cols(g) if ki in (2, 3) else g.reshape(N_DEV, -1, g.shape[1])

    gbig = [None] * len(kinds)
    dmods, small, pending = [], [], []
    token = None
    for l in reversed(range(nl)):
        sv = saved[l]
        mod_l = mods[l] if token is None else mods[l] + token[0, 0]
        sh1, sc1, gt1, sh2, sc2, gt2, sh3, sc3, gt3 = [mod_l[i] for i in range(N_ADA)]
        u_all1, u_all2, w_dn1, w_dn2, win, wpa, wpd, wo = sv["w"]

        def ffn_bwd(dxo, g, sh, sc, gt, u_all, w_dn, sv_f):
            xin, h, gate, up, a, f = sv_f
            s = 0.5 * gt
            g_dn = down_grad_slots(matmul([(a, dxo)], "tn", CD, "ffn_down_wg", out_scale=s, tm=1024, tn=1024))
            dgate, dup = ffn_dact(dxo, s, w_dn, gate, up, "ffn_dact")
            g_up = ffn_up_wg(h, dgate, dup, cp, "ffn_up_wg")
            dh = ffn_up_dg(dgate, dup, u_all, "ffn_up_dg")
            dx, dg, dsc, dsh, dgt = norm_mod_bwd(xin, g, sc, sh, dh, dxo, f, 0.5, "norm_mod_bwd")
            return dx, g_up, g_dn, (dg, dsc, dsh, dgt)

        def start_exchange(ids, tag, behind):
            send = [slots(ki, gbig[ki]) for ki in ids]
            started = exchange_start(send, slot_landing(send), behind, False, f"a2a_start{l}{tag}")
            pending.append((l, ids, tag, started))
            return started[4]

        dxc, gbig[6], gbig[7], (dg3, dsc3, dsh3, dgt3) = ffn_bwd(
            dxc, norm_ff2[l:l + 1], sh3, sc3, gt3, u_all2, w_dn2, sv["ffn2"])
        token = start_exchange([6, 7], "f2", dxc)
        sh2, sc2, gt2 = (vec + token[0, 0] for vec in (sh2, sc2, gt2))

        x_mix, h2, z, qkv, o_tok, lse_tok, ya, cvo, yd, states, mrg, pa, pd, f2 = sv["mix"]
        gbig[5] = matmul([(mrg, dxc)], "tn", CD, "w_out_wg", out_scale=gt2)
        dm = matmul([(dxc, wo)], "nt", F32, "w_out_dg", a_scale=gt2)
        dpa, dpd, dz1, dz2 = merge_bwd(dm, pa, pd, z, col_m, "merge_bwd")
        gbig[3] = matmul([(ya, dpa)], "tn", CD, "patt_wg")
        gbig[4] = matmul([(yd, dpd)], "tn", CD, "pdn_wg")
        dya = matmul([(dpa, wpa)], "nt", F32, "patt_dg")
        dyd = matmul([(dpd, wpd)], "nt", F32, "pdn_dg")
        do_tok, dlse_tok = att_combine_bwd(o_tok, lse_tok, dya, "att_combine_bwd")
        qg, kg = jnp.tile(q_norm[l:l + 1], (1, ha)), jnp.tile(k_norm[l:l + 1], (1, ha))
        dq, dk, dv, dqg, dkg = att_bwd(qkv, qg, kg, _to_classes(do_tok), _to_classes(dlse_tok), "att_bwd")
        dqg, dkg = (jnp.sum(v_.reshape(ha, ATT_HEAD_DIM), axis=0, keepdims=True) for v_ in (dqg, dkg))
        dz_att = jnp.stack([_from_classes(a_) for a_ in (dq, dk, dv)])
        dz_att = dz_att.transpose(2, 0, 1, 3).reshape(t, 3 * att_w).astype(CD)
        al3, dt3 = a_log[l].reshape(nh, 1, 1), dt_bias[l].reshape(nh, 1, 1)
        gn = dn_norm[l:l + 1]
        dcvo, dz_gate, dz_ab, dal, ddt, dgn = dn_bwd(
            cvo, z, col_gate, col_ab, al3, dt3, gn, *states, dyd, "dn_bwd")
        dz_dn, dconv = conv_bwd(dcvo, z, col_dn, conv_full[l], "conv_bwd")
        dz = jnp.concatenate([dz_dn, dz_gate, dz1, dz2, dz_att, dz_ab, jnp.zeros((t, LANE), CD)], axis=1)
        gbig[2] = w_in_cols_inv(matmul([(h2, dz)], "tn", CD, "w_in_wg", tm=1024, tn=zw // 4))
        dh2 = matmul([(dz, win)], "nn", F32, "w_in_dg", tm=1024, tn=1024,
                     tk=zw // 4 if (zw // 4) % LANE == 0 else zw)
        dxc, dg2, dsc2, dsh2, dgt2 = norm_mod_bwd(x_mix, norm_mix[l:l + 1], sc2, sh2, dh2, dxc, f2, 1.0, "norm_mod_bwd_mix")

        token = start_exchange([2, 3, 4, 5], "mx", dxc)
        sh1, sc1, gt1 = (vec + token[0, 0] for vec in (sh1, sc1, gt1))
        dxc, gbig[0], gbig[1], (dg1, dsc1, dsh1, dgt1) = ffn_bwd(
            dxc, norm_ff1[l:l + 1], sh1, sc1, gt1, u_all1, w_dn1, sv["ffn1"])
        if l > 0:
            token = start_exchange([0, 1], "f1", dxc)
        dmods.append(jnp.concatenate([dsh1, dsc1, dgt1, dsh2, dsc2, dgt2, dsh3, dsc3, dgt3], axis=1))
        small.append((dg1, dg2, dg3, dconv, dqg, dkg, dal.reshape(1, nh), ddt.reshape(1, nh), dgn))
    dmods.reverse()
    small.reverse()

    big_names = ["ffn1_w_up", "ffn1_w_down", "w_in", "w_proj_att", "w_proj_dn", "w_out", "ffn2_w_up", "ffn2_w_down"]
    big_m = [m_ffn1_w_up, m_ffn1_w_down, m_w_in, m_w_proj_att, m_w_proj_dn, m_w_out, m_ffn2_w_up, m_ffn2_w_down]
    big_v = [v_ffn1_w_up, v_ffn1_w_down, v_w_in, v_w_proj_att, v_w_proj_dn, v_w_out, v_ffn2_w_up, v_ffn2_w_down]
    recv_layers = [[None] * len(kinds) for _ in range(nl)]

    def wait_exchanges(layers, behind):
        for pl_, ids, tag, st in pending:
            if pl_ in layers:
                got = exchange_wait(*st[:4], behind, False, f"a2a_wait{pl_}{tag}")
                for ki, arr in zip(ids, got):
                    recv_layers[pl_][ki] = arr

    def big_adamw(ki, layers, prev, tag):
        g_layers = [recv_layers[li][ki] for li in layers]
        wmv = (kinds[ki], big_m[ki], big_v[ki])
        if ki in (0, 6):
            wmv = tuple(jnp.swapaxes(a, 1, 2) for a in wmv)
        return adamw_layers(*wmv, g_layers, layers[0], f"adamw_{big_names[ki]}{tag}", prev)

    later = list(range(1, nl))
    partial = [None] * len(kinds)
    if later:
        wait_exchanges(later, dxc)
        partial = [big_adamw(ki, later, None, "_l1") for ki in range(len(kinds))]

    fields = [jnp.stack(dmods).reshape(-1)]
    fields += [jnp.stack([s[i] for s in small]).reshape(-1) for i in range(9)]
    fields.append(loss_blk[0, :1])
    if later:
        fields.append(partial[-1][1][1, :1, 0] * 0.0)
    fsizes = [f.size for f in fields]
    foffs = [sum(fsizes[:i]) for i in range(len(fields))]
    g1 = all_gather([_rows(jnp.concatenate(fields), 8)], "ag_small")[0].reshape(N_DEV, -1)
    l = 0
    start_exchange([0, 1], "f1", g1)

    def field(i, shape):
        return g1[:, foffs[i]:foffs[i] + fsizes[i]].reshape(N_DEV, *shape)

    loss = field(10, (1,))[0, 0]
    for j in range(1, N_DEV):
        loss = loss + field(10, (1,))[j, 0]

    results = {}
    dmod_all = field(0, (nl, N_ADA * d))
    c_pad = jnp.concatenate([c_all, jnp.zeros((8, d), F32)])
    dmod_mine = lax.dynamic_slice_in_dim(dmod_all, me * n_ada, n_ada, axis=2).transpose(1, 0, 2)
    dmod_pad = jnp.concatenate([dmod_mine, jnp.zeros((nl, 8, n_ada), F32)], axis=1)
    g_ada_w = ada_bwd(c_pad, dmod_pad, "ada_bwd")
    results["ada_w"] = adamw(ada_w, m_ada_w, v_ada_w, g_ada_w[None], "adamw_ada_w")
    results["ada_b"] = adamw(ada_b, m_ada_b, v_ada_b, dmod_all, "adamw_ada_b")
    results["norm_ff1"] = adamw(norm_ff1, m_norm_ff1, v_norm_ff1, field(1, (nl, d)), "adamw_norm_ff1")
    results["norm_mix"] = adamw(norm_mix, m_norm_mix, v_norm_mix, field(2, (nl, d)), "adamw_norm_mix")
    results["norm_ff2"] = adamw(norm_ff2, m_norm_ff2, v_norm_ff2, field(3, (nl, d)), "adamw_norm_ff2")
    conv_slots = lax.dynamic_slice_in_dim(field(4, (nl, CONV_WIDTH, 3 * dn_w)), me * conv_w.shape[2],
                                          conv_w.shape[2], axis=3)
    results["conv_w"] = adamw(conv_w, m_conv_w, v_conv_w, conv_slots, "adamw_conv_w")
    results["q_norm"] = adamw(q_norm, m_q_norm, v_q_norm, field(5, (nl, ATT_HEAD_DIM)), "adamw_q_norm")
    results["k_norm"] = adamw(k_norm, m_k_norm, v_k_norm, field(6, (nl, ATT_HEAD_DIM)), "adamw_k_norm")
    results["a_log"] = adamw(a_log, m_a_log, v_a_log, field(7, (nl, nh)), "adamw_a_log")
    results["dt_bias"] = adamw(dt_bias, m_dt_bias, v_dt_bias, field(8, (nl, nh)), "adamw_dt_bias")
    results["dn_norm"] = adamw(dn_norm, m_dn_norm, v_dn_norm, field(9, (nl, DN_HEAD_DIM)), "adamw_dn_norm")
    wait_exchanges([0], results["ada_w"][1])
    for ki, nm in enumerate(big_names):
        res = big_adamw(ki, [0], partial[ki], "_l0")
        results[nm] = tuple(jnp.swapaxes(r, 1, 2) for r in res) if ki in (0, 6) else res

    order = ["ada_w", "ada_b", "norm_ff1", "ffn1_w_up", "ffn1_w_down", "norm_mix", "w_in", "q_norm", "k_norm",
             "conv_w", "a_log", "dt_bias", "dn_norm", "w_proj_att", "w_proj_dn", "w_out", "norm_ff2",
             "ffn2_w_up", "ffn2_w_down"]
    outs = [loss, dxc[None]]
    for part in range(4):
        outs += [results[n][part] for n in order]
    return tuple(outs)
```

```python
import functools

import jax
import jax.numpy as jnp
from jax import lax
from jax.experimental import pallas as pl
from jax.experimental.pallas import tpu as pltpu

F32 = jnp.float32
CD = jnp.bfloat16
EPS = 1e-6
N_DEV = 8
LANE = 128
ATT_HEAD_DIM = 64
ATT_BLOCK = 128
DILATIONS = (1, 4, 16)
DN_HEAD_DIM = 128
DN_CHUNK = 64
CONV_WIDTH = 4
N_ADA = 9
ADAM_LR, ADAM_B1, ADAM_B2, ADAM_EPS, ADAM_WD, ADAM_STEP = 0.001, 0.9, 0.999, 1e-08, 0.01, 10
VMEM_LIMIT = 56 * 1024 * 1024
NEG = -1e30
MESH = pl.DeviceIdType.MESH
HI = lax.Precision.HIGHEST


def _params(sem=None):
    return pltpu.CompilerParams(dimension_semantics=sem, vmem_limit_bytes=VMEM_LIMIT)


def _tile(n, pref, unit):
    best = None
    t = unit
    while t <= min(n, pref):
        if n % t == 0:
            best = t
        t += unit
    return best if best is not None else n


def _silu(x):
    return x * jax.nn.sigmoid(x)


def _dot(a, b, dims, precision=None):
    if precision is None:
        a, b = a.astype(CD), b.astype(CD)
    return lax.dot_general(a, b, (dims, ((), ())), precision=precision, preferred_element_type=F32)


def _nn(a, b, precision=None):
    return _dot(a, b, ((1,), (0,)), precision)


def _nt(a, b, precision=None):
    return _dot(a, b, ((1,), (1,)), precision)


def _tn(a, b, precision=None):
    return _dot(a, b, ((0,), (0,)), precision)


def all_gather(arrs, name):
    n = len(arrs)

    def body(*refs):
        x_refs, out_refs = refs[:n], refs[n:2 * n]
        send_sems, recv_sems, local_sems = refs[2 * n:]
        x, y, c = lax.axis_index("x"), lax.axis_index("y"), lax.axis_index("c")
        me, sibling = (x, y, c), (x, y, 1 - c)
        chips = [(1 - x, y), (x, 1 - y), (1 - x, 1 - y)]

        def copy(a, k, block, to, src=None):
            slot = out_refs[a].at[4 * block[0] + 2 * block[1] + block[2]]
            return pltpu.make_async_remote_copy(
                src_ref=slot if src is None else src, dst_ref=slot,
                send_sem=send_sems.at[7 * a + k], recv_sem=recv_sems.at[7 * a + k],
                device_id=to, device_id_type=MESH)

        mine, first, passed = [], [], []
        for a in range(n):
            cp = pltpu.make_async_copy(x_refs[a], out_refs[a].at[4 * x + 2 * y + c], local_sems.at[a])
            cp.start()
            mine.append(cp)
            first.append(copy(a, 0, me, sibling, src=x_refs[a]))
            first += [copy(a, 1 + j, me, (*chip, c), src=x_refs[a]) for j, chip in enumerate(chips)]
        for cp in first:
            cp.start()
        for j, chip in enumerate(chips):
            for a in range(n):
                copy(a, 1 + j, (*chip, c), me).wait_recv()
                cp = copy(a, 4 + j, (*chip, c), sibling)
                cp.start()
                passed.append(cp)
        for a in range(n):
            copy(a, 0, sibling, me).wait_recv()
            for j, chip in enumerate(chips):
                copy(a, 4 + j, (*chip, 1 - c), me).wait_recv()
        for cp in first + passed:
            cp.wait_send()
        for cp in mine:
            cp.wait()

    hbm = pl.BlockSpec(memory_space=pl.ANY)
    outs = pl.pallas_call(
        body, name=name,
        out_shape=[jax.ShapeDtypeStruct((N_DEV,) + a.shape, a.dtype) for a in arrs],
        in_specs=[hbm] * n, out_specs=[hbm] * n,
        scratch_shapes=[pltpu.SemaphoreType.DMA((7 * n,)), pltpu.SemaphoreType.DMA((7 * n,)),
                        pltpu.SemaphoreType.DMA((n,))],
    )(*arrs)
    return list(outs)


_HBM = pl.BlockSpec(memory_space=pltpu.HBM)
_SEM = pl.BlockSpec(memory_space=pltpu.SEMAPHORE)
_EFFECT = pltpu.SideEffectType.DATAFLOW_SIDE_EFFECTING


def _exchange_copies(src_refs, land_refs, send_sem, recv_sem, gather):
    x, y, c = lax.axis_index("x"), lax.axis_index("y"), lax.axis_index("c")
    me = 4 * x + 2 * y + c
    pairs = []
    for k in range(1, N_DEV):
        px, py, pc = x ^ (k >> 2), y ^ ((k >> 1) & 1), c ^ (k & 1)
        pidx = 4 * px + 2 * py + pc
        for src, land in zip(src_refs, land_refs):
            mine = src if gather else src.at[pidx]
            out = pltpu.make_async_remote_copy(src_ref=mine, dst_ref=land.at[me], send_sem=send_sem, recv_sem=recv_sem,
                                               device_id=(px, py, pc), device_id_type=MESH)
            inc = pltpu.make_async_remote_copy(src_ref=mine, dst_ref=land.at[pidx], send_sem=send_sem, recv_sem=recv_sem,
                                               device_id=(px, py, pc), device_id_type=MESH)
            pairs.append((out, inc))
    return pairs


def exchange_start(srcs, lands, after, gather, name):
    n = len(srcs)

    def body(*refs):
        src_refs, land_refs = refs[:n], refs[n:2 * n]
        send_sem, recv_sem = refs[2 * n + 1], refs[2 * n + 2]
        token = refs[-1]
        for out, _ in _exchange_copies(src_refs, land_refs, send_sem, recv_sem, gather):
            out.start()
        token[...] = jnp.zeros_like(token)

    res = pl.pallas_call(
        body, name=name,
        out_shape=(pltpu.SemaphoreType.DMA(()), pltpu.SemaphoreType.DMA(()),
                   *[pltpu.HBM(a.shape, a.dtype) for a in srcs], *[pltpu.HBM(a.shape, a.dtype) for a in lands],
                   jax.ShapeDtypeStruct((8, LANE), F32)),
        in_specs=[_HBM] * (2 * n) + [pl.BlockSpec(memory_space=pl.ANY)],
        out_specs=(_SEM, _SEM, *[_HBM] * (2 * n), pl.BlockSpec(memory_space=pltpu.VMEM)),
        input_output_aliases={i: 2 + i for i in range(2 * n)},
        compiler_params=pltpu.CompilerParams(has_side_effects=_EFFECT),
    )(*[pltpu.with_memory_space_constraint(a, pltpu.HBM) for a in list(srcs) + list(lands)], after)
    return res[0], res[1], list(res[2:2 + n]), list(res[2 + n:2 + 2 * n]), res[-1]


def exchange_wait(send_sem, recv_sem, srcs, lands, after, gather, name):
    n = len(srcs)

    def body(*refs):
        src_refs, land_refs = refs[:n], refs[n:2 * n]
        s_sem, r_sem = refs[2 * n], refs[2 * n + 1]
        for out, inc in _exchange_copies(src_refs, land_refs, s_sem, r_sem, gather):
            out.wait_send()
            inc.wait_recv()

    res = pl.pallas_call(
        body, name=name,
        out_shape=[pltpu.HBM(a.shape, a.dtype) for a in list(srcs) + list(lands)],
        in_specs=[_HBM] * (2 * n) + [_SEM, _SEM, pl.BlockSpec(memory_space=pl.ANY)],
        out_specs=[_HBM] * (2 * n),
        input_output_aliases={i: i for i in range(2 * n)},
        compiler_params=pltpu.CompilerParams(has_side_effects=_EFFECT),
    )(*srcs, *lands, send_sem, recv_sem, after)
    return list(res[n:])


def matmul(pairs, mode, out_dtype, name, *, a_scale=None, out_scale=None, resid=None,
           save_acc=False, tm=512, tn=512, tk=2048):
    a0, b0 = pairs[0]
    if mode == "nn":
        (m, kdim), n = a0.shape, b0.shape[1]
    elif mode == "nt":
        (m, kdim), n = a0.shape, b0.shape[0]
    else:
        (kdim, m), n = a0.shape, b0.shape[1]
    tm, tn = _tile(m, tm, 8 if m % LANE else LANE), _tile(n, tn, LANE)
    tk = _tile(kdim, tk, LANE)
    nk = kdim // tk
    sub = tn if mode == "tn" else _tile(tn, 256, LANE)
    npairs = len(pairs)
    dims = {"nn": ((1,), (0,)), "nt": ((1,), (1,)), "tn": ((0,), (0,))}[mode]

    if mode == "nn":
        a_spec = pl.BlockSpec((tm, tk), lambda i, j, k: (i, k))
        b_spec = pl.BlockSpec((tk, tn), lambda i, j, k: (k, j))
    elif mode == "nt":
        a_spec = pl.BlockSpec((tm, tk), lambda i, j, k: (i, k))
        b_spec = pl.BlockSpec((tn, tk), lambda i, j, k: (j, k))
    else:
        a_spec = pl.BlockSpec((tk, tm), lambda i, j, k: (k, i))
        b_spec = pl.BlockSpec((tk, tn), lambda i, j, k: (k, j))

    def body(*refs):
        it = iter(refs)
        pair_refs = [(next(it), next(it)) for _ in range(npairs)]
        as_ref = next(it) if a_scale is not None else None
        os_ref = next(it) if out_scale is not None else None
        rs_ref = next(it) if resid is not None else None
        o_ref = next(it)
        acc_out = next(it) if save_acc else None
        acc_ref = next(it) if nk > 1 else None
        lhs = []
        for a_ref, _ in pair_refs:
            a = a_ref[...]
            if as_ref is not None:
                a = a.astype(F32) * as_ref[...]
            lhs.append(a.astype(CD))

        def product(cols):
            part = None
            for a, (_, b_ref) in zip(lhs, pair_refs):
                b = b_ref[...] if cols is None else (b_ref[cols, :] if mode == "nt" else b_ref[:, cols])
                d = _dot(a, b, dims)
                part = d if part is None else part + d
            return part

        def finish(acc, cols=slice(None)):
            if acc_out is not None:
                acc_out[:, cols] = acc
            if os_ref is not None:
                acc = acc * os_ref[:, cols]
            if rs_ref is not None:
                acc = rs_ref[:, cols] + acc
            o_ref[:, cols] = acc.astype(o_ref.dtype)

        if nk == 1:
            for c0 in range(0, tn, sub):
                finish(product(slice(c0, c0 + sub)), slice(c0, c0 + sub))
        else:
            part = product(None)
            k = pl.program_id(2)

            @pl.when(k == 0)
            def _():
                acc_ref[...] = part

            @pl.when(k > 0)
            def _():
                acc_ref[...] += part

            @pl.when(k == nk - 1)
            def _():
                finish(acc_ref[...])

    in_specs, args = [], []
    for a, b in pairs:
        in_specs += [a_spec, b_spec]
        args += [a, b]
    if a_scale is not None:
        assert mode != "tn"
        in_specs.append(pl.BlockSpec((1, tk), lambda i, j, k: (0, k)))
        args.append(a_scale)
    if out_scale is not None:
        in_specs.append(pl.BlockSpec((1, tn), lambda i, j, k: (0, j)))
        args.append(out_scale)
    if resid is not None:
        in_specs.append(pl.BlockSpec((tm, tn), lambda i, j, k: (i, j)))
        args.append(resid)
    o_spec = pl.BlockSpec((tm, tn), lambda i, j, k: (i, j))
    out_shape = [jax.ShapeDtypeStruct((m, n), out_dtype)]
    out_specs = [o_spec]
    if save_acc:
        out_shape.append(jax.ShapeDtypeStruct((m, n), F32))
        out_specs.append(o_spec)
    scratch = [pltpu.VMEM((tm, tn), F32)] if nk > 1 else []
    res = pl.pallas_call(
        body, name=name, grid=(m // tm, n // tn, nk),
        in_specs=in_specs, out_specs=out_specs, out_shape=out_shape, scratch_shapes=scratch,
        compiler_params=_params(("parallel", "parallel", "arbitrary")),
    )(*args)
    return res if save_acc else res[0]


def _nm_fn(x, g, sc, sh):
    r = lax.rsqrt(jnp.mean(x * x, axis=-1, keepdims=True) + EPS)
    return (x * r * g) * (1.0 + sc) + sh


def norm_mod(x, g, sc, sh, name):
    t, d = x.shape
    tr = _tile(t, 256, 8)

    def body(x_ref, g_ref, sc_ref, sh_ref, h_ref):
        h_ref[...] = _nm_fn(x_ref[...], g_ref[...], sc_ref[...], sh_ref[...]).astype(h_ref.dtype)

    row = pl.BlockSpec((tr, d), lambda i: (i, 0))
    vec = pl.BlockSpec((1, d), lambda i: (0, 0))
    return pl.pallas_call(
        body, name=name, grid=(t // tr,), in_specs=[row, vec, vec, vec], out_specs=row,
        out_shape=jax.ShapeDtypeStruct((t, d), CD), compiler_params=_params(("parallel",)),
    )(x, g, sc, sh)


def norm_mod_bwd(x, g, sc, sh, dh, dxo, f, gate_scale, name):
    t, d = x.shape
    tr = _tile(t, 256, 8)

    def body(x_ref, g_ref, sc_ref, sh_ref, dh_ref, dxo_ref, f_ref, dx_ref, dg_ref, dsc_ref, dsh_ref, dgt_ref):
        _, vjp = jax.vjp(_nm_fn, x_ref[...], g_ref[...], sc_ref[...], sh_ref[...])
        dx, dg, dsc, dsh = vjp(dh_ref[...])
        dxo_v = dxo_ref[...]
        dx_ref[...] = dxo_v + dx
        dgt = gate_scale * jnp.sum(f_ref[...] * dxo_v, axis=0, keepdims=True)

        @pl.when(pl.program_id(0) == 0)
        def _():
            dg_ref[...] = dg
            dsc_ref[...] = dsc
            dsh_ref[...] = dsh
            dgt_ref[...] = dgt

        @pl.when(pl.program_id(0) > 0)
        def _():
            dg_ref[...] += dg
            dsc_ref[...] += dsc
            dsh_ref[...] += dsh
            dgt_ref[...] += dgt

    row = pl.BlockSpec((tr, d), lambda i: (i, 0))
    vec = pl.BlockSpec((1, d), lambda i: (0, 0))
    vshape = jax.ShapeDtypeStruct((1, d), F32)
    return pl.pallas_call(
        body, name=name, grid=(t // tr,), in_specs=[row, vec, vec, vec, row, row, row],
        out_specs=[row, vec, vec, vec, vec],
        out_shape=[jax.ShapeDtypeStruct((t, d), F32), vshape, vshape, vshape, vshape],
        compiler_params=_params(("arbitrary",)),
    )(x, g, sc, sh, dh, dxo, f)


HALF = N_DEV // 2


def ffn_up(h, u_all, name):
    t, d = h.shape
    cp = u_all.shape[1]
    f = HALF * cp
    tm, tn = _tile(t, 1024, LANE), cp
    per = cp // tn
    sub = _tile(tn, 256, LANE)

    def body(h_ref, wg_ref, wu_ref, g_ref, u_ref, a_ref):
        hv = h_ref[...]
        for c0 in range(0, tn, sub):
            cols = slice(c0, c0 + sub)
            gate = _nt(hv, wg_ref[0, cols, :])
            up = _nt(hv, wu_ref[0, cols, :])
            g_ref[:, cols] = gate.astype(g_ref.dtype)
            u_ref[:, cols] = up.astype(u_ref.dtype)
            a_ref[:, cols] = (_silu(gate) * up).astype(a_ref.dtype)

    o = pl.BlockSpec((tm, tn), lambda i, j: (i, j))
    wg = pl.BlockSpec((1, tn, d), lambda i, j: (j // per, j % per, 0))
    wu = pl.BlockSpec((1, tn, d), lambda i, j: (HALF + j // per, j % per, 0))
    return pl.pallas_call(
        body, name=name, grid=(t // tm, f // tn),
        in_specs=[pl.BlockSpec((tm, d), lambda i, j: (i, 0)), wg, wu], out_specs=[o, o, o],
        out_shape=[jax.ShapeDtypeStruct((t, f), CD)] * 3,
        compiler_params=_params(("parallel", "parallel")),
    )(h, u_all, u_all)


def ffn_up_wg(h, dgate, dup, cp, name):
    t, d = h.shape
    tn = _tile(d, 1024, LANE)

    def body(h_ref, dg_ref, du_ref, o_ref):
        s = pl.program_id(0)

        @pl.when(s < HALF)
        def _():
            o_ref[0] = _tn(dg_ref[...], h_ref[...]).astype(o_ref.dtype)

        @pl.when(s >= HALF)
        def _():
            o_ref[0] = _tn(du_ref[...], h_ref[...]).astype(o_ref.dtype)

    return pl.pallas_call(
        body, name=name, grid=(N_DEV, d // tn),
        in_specs=[pl.BlockSpec((t, tn), lambda s, i: (0, i)),
                  pl.BlockSpec((t, cp), lambda s, i: (0, jnp.minimum(s, HALF - 1))),
                  pl.BlockSpec((t, cp), lambda s, i: (0, jnp.maximum(s - HALF, 0)))],
        out_specs=pl.BlockSpec((1, cp, tn), lambda s, i: (s, 0, i)),
        out_shape=jax.ShapeDtypeStruct((N_DEV, cp, d), CD),
        compiler_params=_params(("parallel", "parallel")),
    )(h, dgate, dup)


def ffn_up_dg(dgate, dup, u_all, name):
    t, f = dgate.shape
    cp, d = u_all.shape[1], u_all.shape[2]
    tm, tn = _tile(t, 1024, LANE), _tile(d, 512, LANE)

    def body(dg_ref, du_ref, u_ref, o_ref):
        wg = u_ref[0:HALF].reshape(f, tn)
        wu = u_ref[HALF:N_DEV].reshape(f, tn)
        o_ref[...] = _nn(dg_ref[...], wg) + _nn(du_ref[...], wu)

    return pl.pallas_call(
        body, name=name, grid=(t // tm, d // tn),
        in_specs=[pl.BlockSpec((tm, f), lambda i, j: (i, 0)), pl.BlockSpec((tm, f), lambda i, j: (i, 0)),
                  pl.BlockSpec((N_DEV, cp, tn), lambda i, j: (0, 0, j))],
        out_specs=pl.BlockSpec((tm, tn), lambda i, j: (i, j)),
        out_shape=jax.ShapeDtypeStruct((t, d), F32),
        compiler_params=_params(("parallel", "parallel")),
    )(dgate, dup, u_all)


def ffn_dact(dxo, s, wd, gate, up, name):
    t, d = dxo.shape
    f = wd.shape[0]
    tm, tn = _tile(t, 1024, LANE), _tile(f, 768, LANE)
    sub = _tile(tn, 256, LANE)

    def body(dxo_ref, s_ref, wd_ref, g_ref, u_ref, dg_ref, du_ref):
        dxs = (dxo_ref[...] * s_ref[...]).astype(CD)
        for c0 in range(0, tn, sub):
            cols = slice(c0, c0 + sub)
            da = _nt(dxs, wd_ref[cols, :])
            gate, up = g_ref[:, cols].astype(F32), u_ref[:, cols].astype(F32)
            sg = jax.nn.sigmoid(gate)
            dg_ref[:, cols] = (da * up * sg * (1.0 + gate * (1.0 - sg))).astype(dg_ref.dtype)
            du_ref[:, cols] = (da * gate * sg).astype(du_ref.dtype)

    o = pl.BlockSpec((tm, tn), lambda i, j: (i, j))
    return pl.pallas_call(
        body, name=name, grid=(t // tm, f // tn),
        in_specs=[pl.BlockSpec((tm, d), lambda i, j: (i, 0)), pl.BlockSpec((1, d), lambda i, j: (0, 0)),
                  pl.BlockSpec((tn, d), lambda i, j: (j, 0)), o, o],
        out_specs=[o, o],
        out_shape=[jax.ShapeDtypeStruct((t, f), CD), jax.ShapeDtypeStruct((t, f), CD)],
        compiler_params=_params(("parallel", "parallel")),
    )(dxo, s, wd, gate, up)


def _bdot(a, b, ca, cb, precision=None):
    if precision is None:
        a, b = a.astype(CD), b.astype(CD)
    return lax.dot_general(a, b, (((ca,), (cb,)), ((0,), (0,))), precision=precision, preferred_element_type=F32)


def _bnn(a, b, precision=None):
    return _bdot(a, b, 2, 1, precision)


def _bnt(a, b, precision=None):
    return _bdot(a, b, 2, 2, precision)


def _btn(a, b, precision=None):
    return _bdot(a, b, 1, 1, precision)


def _att_fn(q, kp, kc, vp, vc, qg, kg, mask_p, mask_c):
    b, w = q.shape
    nh = w // ATT_HEAD_DIM
    head_of_lane = lax.broadcasted_iota(jnp.int32, (nh, 1, w), 2) // ATT_HEAD_DIM
    hm = (head_of_lane == lax.broadcasted_iota(jnp.int32, (nh, 1, w), 0)).astype(F32)

    def rn(x, g):
        ss = jnp.sum((x * x)[None] * hm, axis=-1, keepdims=True)
        r = jnp.sum(lax.rsqrt(ss * (1.0 / ATT_HEAD_DIM) + EPS) * hm, axis=0)
        return x * r * g

    qn, kcn = rn(q, qg), rn(kc, kg)
    q4 = (qn[None] * hm).reshape(nh * b, w)
    scale = ATT_HEAD_DIM ** -0.5
    sc = jnp.where(mask_c, _nt(q4, kcn) * scale, NEG)
    if kp is None:
        m = jnp.max(sc, axis=-1, keepdims=True)
        pc = jnp.exp(sc - m)
        den = jnp.sum(pc, axis=-1, keepdims=True)
        o4 = _nn(pc / den, vc)
    else:
        sp = jnp.where(mask_p, _nt(q4, rn(kp, kg)) * scale, NEG)
        m = jnp.maximum(jnp.max(sp, axis=-1, keepdims=True), jnp.max(sc, axis=-1, keepdims=True))
        pp, pc = jnp.exp(sp - m), jnp.exp(sc - m)
        den = jnp.sum(pp, axis=-1, keepdims=True) + jnp.sum(pc, axis=-1, keepdims=True)
        o4 = _nn(pp / den, vp) + _nn(pc / den, vc)
    o = jnp.sum(o4.reshape(nh, b, w) * hm, axis=0)
    lse = jnp.sum((m + jnp.log(den)).reshape(nh, b, 1) * hm, axis=0)
    return o, lse


def _att_masks(g, j, nb_total, nh):
    nb = jnp.int32(nb_total // DILATIONS[0])
    for gi in range(1, len(DILATIONS)):
        nb = jnp.where(g == gi, jnp.int32(nb_total // DILATIONS[gi]), nb)
    has_prev = (j % nb) != 0
    row = lax.broadcasted_iota(jnp.int32, (nh * ATT_BLOCK, ATT_BLOCK), 0) % ATT_BLOCK
    col = lax.broadcasted_iota(jnp.int32, (nh * ATT_BLOCK, ATT_BLOCK), 1)
    return has_prev, col >= row, col <= row


def _att_specs(w):
    blk = (1, 1, ATT_BLOCK, w)
    q = pl.BlockSpec(blk, lambda g, j: (0, g, j, 0))
    kp = pl.BlockSpec(blk, lambda g, j: (1, g, jnp.maximum(j - 1, 0), 0))
    kc = pl.BlockSpec(blk, lambda g, j: (1, g, j, 0))
    vp = pl.BlockSpec(blk, lambda g, j: (2, g, jnp.maximum(j - 1, 0), 0))
    vc = pl.BlockSpec(blk, lambda g, j: (2, g, j, 0))
    gain = pl.BlockSpec((1, w), lambda g, j: (0, 0))
    out = pl.BlockSpec((1, ATT_BLOCK, w), lambda g, j: (g, j, 0))
    return [q, kp, kc, vp, vc, gain, gain], out


def att_fwd(qkv, qg, kg, name):
    _, ng, t, w = qkv.shape
    nbt = t // ATT_BLOCK
    in_specs, out = _att_specs(w)

    def body(q_ref, kp_ref, kc_ref, vp_ref, vc_ref, qg_ref, kg_ref, o_ref, lse_ref):
        has_prev, mask_p, mask_c = _att_masks(pl.program_id(0), pl.program_id(1), nbt, w // ATT_HEAD_DIM)

        @pl.when(has_prev)
        def _():
            o, lse = _att_fn(q_ref[0, 0], kp_ref[0, 0], kc_ref[0, 0], vp_ref[0, 0], vc_ref[0, 0],
                             qg_ref[...], kg_ref[...], mask_p, mask_c)
            o_ref[0] = o
            lse_ref[0] = lse

        @pl.when(jnp.logical_not(has_prev))
        def _():
            o, lse = _att_fn(q_ref[0, 0], None, kc_ref[0, 0], None, vc_ref[0, 0],
                             qg_ref[...], kg_ref[...], None, mask_c)
            o_ref[0] = o
            lse_ref[0] = lse

    sh = jax.ShapeDtypeStruct((ng, t, w), F32)
    return pl.pallas_call(
        body, name=name, grid=(ng, nbt), in_specs=in_specs, out_specs=[out, out], out_shape=[sh, sh],
        compiler_params=_params(("parallel", "parallel")),
    )(qkv, qkv, qkv, qkv, qkv, qg, kg)


def att_bwd(qkv, qg, kg, do, dlse, name):
    _, ng, t, w = qkv.shape
    nbt = t // ATT_BLOCK
    in_specs, out = _att_specs(w)
    whole = pl.BlockSpec((1, t, w), lambda g, j: (g, 0, 0))
    gain = in_specs[-1]

    def body(q_ref, kp_ref, kc_ref, vp_ref, vc_ref, qg_ref, kg_ref, do_ref, dlse_ref,
             dq_ref, dk_ref, dv_ref, dqg_ref, dkg_ref):
        g, j = pl.program_id(0), pl.program_id(1)
        has_prev, mask_p, mask_c = _att_masks(g, j, nbt, w // ATT_HEAD_DIM)

        @pl.when(j == 0)
        def _():
            dk_ref[...] = jnp.zeros_like(dk_ref)
            dv_ref[...] = jnp.zeros_like(dv_ref)

        @pl.when(jnp.logical_and(g == 0, j == 0))
        def _():
            dqg_ref[...] = jnp.zeros_like(dqg_ref)
            dkg_ref[...] = jnp.zeros_like(dkg_ref)

        rows_c = pl.ds(pl.multiple_of(j * ATT_BLOCK, ATT_BLOCK), ATT_BLOCK)
        rows_p = pl.ds(pl.multiple_of(jnp.maximum(j - 1, 0) * ATT_BLOCK, ATT_BLOCK), ATT_BLOCK)
        @pl.when(has_prev)
        def _():
            fn = functools.partial(_att_fn, mask_p=mask_p, mask_c=mask_c)
            _, vjp = jax.vjp(fn, q_ref[0, 0], kp_ref[0, 0], kc_ref[0, 0], vp_ref[0, 0], vc_ref[0, 0],
                             qg_ref[...], kg_ref[...])
            dq, dkp, dkc, dvp, dvc, dqg, dkg = vjp((do_ref[0], dlse_ref[0]))
            dq_ref[0] = dq
            dk_ref[0, rows_p, :] += dkp
            dv_ref[0, rows_p, :] += dvp
            dk_ref[0, rows_c, :] += dkc
            dv_ref[0, rows_c, :] += dvc
            dqg_ref[...] += dqg
            dkg_ref[...] += dkg

        @pl.when(jnp.logical_not(has_prev))
        def _():
            def fn(q, kc, vc, qg, kg):
                return _att_fn(q, None, kc, None, vc, qg, kg, None, mask_c)

            _, vjp = jax.vjp(fn, q_ref[0, 0], kc_ref[0, 0], vc_ref[0, 0], qg_ref[...], kg_ref[...])
            dq, dkc, dvc, dqg, dkg = vjp((do_ref[0], dlse_ref[0]))
            dq_ref[0] = dq
            dk_ref[0, rows_c, :] += dkc
            dv_ref[0, rows_c, :] += dvc
            dqg_ref[...] += dqg
            dkg_ref[...] += dkg

    sh = jax.ShapeDtypeStruct((ng, t, w), F32)
    gshape = jax.ShapeDtypeStruct((1, w), F32)
    return pl.pallas_call(
        body, name=name, grid=(ng, nbt), in_specs=in_specs + [out, out],
        out_specs=[out, whole, whole, gain, gain], out_shape=[sh, sh, sh, gshape, gshape],
        compiler_params=_params(("arbitrary", "arbitrary")),
    )(qkv, qkv, qkv, qkv, qkv, qg, kg, do, dlse)


def _combine_fn(o, lse):
    m = jnp.max(lse, axis=0, keepdims=True)
    e = jnp.exp(lse - m)
    w = e / jnp.sum(e, axis=0, keepdims=True)
    return jnp.sum(w * o, axis=0)


def att_combine(o, lse, name):
    ng, t, w = o.shape
    tr = _tile(t, 256, 16)
    spec = pl.BlockSpec((ng, tr, w), lambda i: (0, i, 0))
    y_spec = pl.BlockSpec((tr, w), lambda i: (i, 0))

    def body(o_ref, l_ref, y_ref):
        y_ref[...] = _combine_fn(o_ref[...], l_ref[...]).astype(y_ref.dtype)

    return pl.pallas_call(
        body, name=name, grid=(t // tr,), in_specs=[spec, spec], out_specs=y_spec,
        out_shape=jax.ShapeDtypeStruct((t, w), CD), compiler_params=_params(("parallel",)),
    )(o, lse)


def att_combine_bwd(o, lse, dy, name):
    ng, t, w = o.shape
    tr = _tile(t, 256, 8)
    spec = pl.BlockSpec((ng, tr, w), lambda i: (0, i, 0))
    y_spec = pl.BlockSpec((tr, w), lambda i: (i, 0))

    def body(o_ref, l_ref, dy_ref, do_ref, dl_ref):
        _, vjp = jax.vjp(_combine_fn, o_ref[...], l_ref[...])
        do, dl = vjp(dy_ref[...])
        do_ref[...] = do
        dl_ref[...] = dl

    sh = jax.ShapeDtypeStruct(o.shape, F32)
    return pl.pallas_call(
        body, name=name, grid=(t // tr,), in_specs=[spec, spec, y_spec], out_specs=[spec, spec],
        out_shape=[sh, sh], compiler_params=_params(("parallel",)),
    )(o, lse, dy)


def _shift_down(x, s):
    if s == 0:
        return x
    row = lax.broadcasted_iota(jnp.int32, x.shape, 0)
    return jnp.where(row >= s, pltpu.roll(x, s, 0), 0.0)


def _shift_up(x, s):
    if s == 0:
        return x
    t = x.shape[0]
    row = lax.broadcasted_iota(jnp.int32, x.shape, 0)
    return jnp.where(row < t - s, pltpu.roll(x, t - s, 0), 0.0)


def conv_fwd(z, col0, w, name):
    t = z.shape[0]
    c = w.shape[1]
    cb = _tile(c, 512, LANE)
    assert col0 % cb == 0
    nblk0 = col0 // cb

    def body(z_ref, w_ref, c_ref):
        for c0 in range(0, cb, LANE):
            cols = slice(c0, c0 + LANE)
            zv = z_ref[:, cols]
            acc = None
            for i in range(CONV_WIDTH):
                term = _shift_down(zv, CONV_WIDTH - 1 - i) * w_ref[i:i + 1, cols]
                acc = term if acc is None else acc + term
            c_ref[:, cols] = acc

    return pl.pallas_call(
        body, name=name, grid=(c // cb,),
        in_specs=[pl.BlockSpec((t, cb), lambda j: (0, nblk0 + j)), pl.BlockSpec((CONV_WIDTH, cb), lambda j: (0, j))],
        out_specs=pl.BlockSpec((t, cb), lambda j: (0, j)),
        out_shape=jax.ShapeDtypeStruct((t, c), F32), compiler_params=_params(("parallel",)),
    )(z, w)


def conv_bwd(dc, z, col0, w, name):
    t = z.shape[0]
    c = w.shape[1]
    cb = _tile(c, 512, LANE)
    assert col0 % cb == 0
    nblk0 = col0 // cb

    def body(dc_ref, z_ref, w_ref, dz_ref, dw_ref):
        for c0 in range(0, cb, LANE):
            cols = slice(c0, c0 + LANE)
            dcv, zv = dc_ref[:, cols], z_ref[:, cols]
            acc = None
            for i in range(CONV_WIDTH):
                s = CONV_WIDTH - 1 - i
                term = _shift_up(dcv, s) * w_ref[i:i + 1, cols]
                acc = term if acc is None else acc + term
                dw_ref[i:i + 1, cols] = jnp.sum(dcv * _shift_down(zv, s), axis=0, keepdims=True)
            dz_ref[:, cols] = acc.astype(dz_ref.dtype)

    blk = pl.BlockSpec((t, cb), lambda j: (0, j))
    wblk = pl.BlockSpec((CONV_WIDTH, cb), lambda j: (0, j))
    return pl.pallas_call(
        body, name=name, grid=(c // cb,),
        in_specs=[blk, pl.BlockSpec((t, cb), lambda j: (0, nblk0 + j)), wblk], out_specs=[blk, wblk],
        out_shape=[jax.ShapeDtypeStruct((t, c), CD), jax.ShapeDtypeStruct((CONV_WIDTH, c), F32)],
        compiler_params=_params(("parallel",)),
    )(dc, z, w)


def _dn_consts():
    c = DN_CHUNK
    row = lax.broadcasted_iota(jnp.int32, (c, c), 0)
    col = lax.broadcasted_iota(jnp.int32, (c, c), 1)
    return dict(tril=row >= col, strict=row > col, eye=(row == col).astype(F32),
                tril_f=(row >= col).astype(F32), triu_f=(row <= col).astype(F32))


def _softplus(x):
    return jnp.maximum(x, 0.0) + jnp.log(1.0 + jnp.exp(-jnp.abs(x)))


def _split(x):
    hi = x.astype(CD)
    return hi, (x - hi.astype(F32)).astype(CD)


def _bdot3(a, b, ca, cb):
    ah, al = _split(a)
    bh, bl = _split(b)
    return _bdot(ah, bh, ca, cb) + (_bdot(ah, bl, ca, cb) + _bdot(al, bh, ca, cb))


def _tri_inv_impl(a_mat):
    c = a_mat.shape[-1]
    eye = (lax.broadcasted_iota(jnp.int32, (c, c), 0) == lax.broadcasted_iota(jnp.int32, (c, c), 1)).astype(F32)
    nk_ = -a_mat
    t_inv = eye + nk_
    for _ in range(c.bit_length() - 2):
        nk_ = _bdot3(nk_, nk_, 2, 1)
        t_inv = t_inv + _bdot3(t_inv, nk_, 2, 1)
    return t_inv


@jax.custom_vjp
def _tri_inv(a_mat):
    return _tri_inv_impl(a_mat)


def _tri_inv_fwd(a_mat):
    t_inv = _tri_inv_impl(a_mat)
    return t_inv, t_inv


def _tri_inv_bwd(t_inv, dt_inv):
    return (-_bdot3(_bdot3(t_inv, dt_inv, 1, 1), t_inv, 2, 2),)


_tri_inv.defvjp(_tri_inv_fwd, _tri_inv_bwd)


@jax.custom_vjp
def _tri_inv_saved(a_mat, t_inv):
    return t_inv


def _tri_inv_saved_fwd(a_mat, t_inv):
    return t_inv, t_inv


def _tri_inv_saved_bwd(t_inv, dt_inv):
    return _tri_inv_bwd(t_inv, dt_inv)[0], jnp.zeros_like(t_inv)


_tri_inv_saved.defvjp(_tri_inv_saved_fwd, _tri_inv_saved_bwd)


def _dn_chunk(cq, ck, cv, og, a_col, b_col, al, dt, gn, s_prev, *, k, inv, with_inv=False):
    q = _silu(cq)
    q = q * lax.rsqrt(jnp.sum(q * q, axis=-1, keepdims=True) + EPS) * (DN_HEAD_DIM ** -0.5)
    kk = _silu(ck)
    kk = kk * lax.rsqrt(jnp.sum(kk * kk, axis=-1, keepdims=True) + EPS)
    v = _silu(cv)
    g = -jnp.exp(al) * _softplus(a_col + dt)
    beta = jax.nn.sigmoid(b_col)
    g_row = jnp.sum(k["eye"] * g, axis=1, keepdims=True)
    gc_col = jnp.sum(k["tril_f"] * g_row, axis=2, keepdims=True)
    gc_row = jnp.sum(k["triu_f"] * g, axis=1, keepdims=True)
    ldec = jnp.where(k["tril"], jnp.exp(jnp.where(k["tril"], gc_col - gc_row, 0.0)), 0.0)
    kb, vb = kk * beta, v * beta
    a_mat = jnp.where(k["strict"], _bnt(kb, kk) * ldec, 0.0)
    t_inv = inv(a_mat)
    egc = jnp.exp(gc_col)
    u = _bnn(t_inv, vb)
    w = _bnn(t_inv, kb * egc)
    attn = jnp.where(k["tril"], _bnt(q, kk) * ldec, 0.0)
    gc_last = jnp.sum(g, axis=1, keepdims=True)
    k_dec = kk * jnp.exp(gc_last - gc_col)
    v_new = u - _bnn(w, s_prev)
    o = _bnn(q * egc, s_prev) + _bnn(attn, v_new)
    s_new = s_prev * jnp.exp(gc_last) + _btn(k_dec, v_new)
    y = o * lax.rsqrt(jnp.mean(o * o, axis=-1, keepdims=True) + EPS) * gn * _silu(og)
    return (y, s_new, t_inv) if with_inv else (y, s_new)


def _dn_heads(ref, nh):
    hd = DN_HEAD_DIM
    return jnp.stack([ref[:, h * hd:(h + 1) * hd] for h in range(nh)])


def _dn_specs(nh, col_gate, col_ab, order):
    hd, c = DN_HEAD_DIM, DN_CHUNK
    w = nh * hd
    qs = pl.BlockSpec((c, w), lambda n: (order(n), 0))
    ks = pl.BlockSpec((c, w), lambda n: (order(n), 1))
    vs = pl.BlockSpec((c, w), lambda n: (order(n), 2))
    gs = pl.BlockSpec((c, w), lambda n: (order(n), col_gate // w))
    ab = pl.BlockSpec((c, LANE), lambda n: (order(n), col_ab // LANE))
    scal = pl.BlockSpec((nh, 1, 1), lambda n: (0, 0, 0))
    gn = pl.BlockSpec((1, hd), lambda n: (0, 0))
    st = pl.BlockSpec((1, nh, hd, hd), lambda n: (order(n), 0, 0, 0))
    return qs, ks, vs, gs, ab, scal, gn, st


def _lane_pick(x, idx):
    lane = lax.broadcasted_iota(jnp.int32, x.shape, 1)
    return jnp.sum(jnp.where(lane == idx, x, 0.0), axis=1, keepdims=True)


def dn_fwd(cv, z, col_gate, col_ab, a_log, dt_bias, gn, name):
    t = cv.shape[0]
    nh = a_log.shape[0]
    hd, c = DN_HEAD_DIM, DN_CHUNK
    n_chunks = t // c
    qs, ks, vs, gs, ab, scal, gnspec, st = _dn_specs(nh, col_gate, col_ab, lambda n: n)

    def body(q_ref, k_ref, v_ref, g_ref, ab_ref, al_ref, dt_ref, gn_ref, y_ref, st_ref, ti_ref, s_scr):
        @pl.when(pl.program_id(0) == 0)
        def _():
            s_scr[...] = jnp.zeros_like(s_scr)

        abv = ab_ref[...]
        a_col = jnp.stack([_lane_pick(abv, h) for h in range(nh)])
        b_col = jnp.stack([_lane_pick(abv, nh + h) for h in range(nh)])
        s_prev = s_scr[...]
        st_ref[0] = s_prev
        y, s_new, t_inv = _dn_chunk(
            _dn_heads(q_ref, nh), _dn_heads(k_ref, nh), _dn_heads(v_ref, nh), _dn_heads(g_ref, nh),
            a_col, b_col, al_ref[...], dt_ref[...], gn_ref[...], s_prev,
            k=_dn_consts(), inv=_tri_inv_impl, with_inv=True)
        for h in range(nh):
            y_ref[:, h * hd:(h + 1) * hd] = y[h].astype(y_ref.dtype)
        ti_ref[0] = t_inv
        s_scr[...] = s_new

    return pl.pallas_call(
        body, name=name, grid=(n_chunks,),
        in_specs=[qs, ks, vs, gs, ab, scal, scal, gnspec],
        out_specs=[pl.BlockSpec((c, nh * hd), lambda n: (n, 0)), st,
                   pl.BlockSpec((1, nh, c, c), lambda n: (n, 0, 0, 0))],
        out_shape=[jax.ShapeDtypeStruct((t, nh * hd), CD), jax.ShapeDtypeStruct((n_chunks, nh, hd, hd), F32),
                   jax.ShapeDtypeStruct((n_chunks, nh, c, c), F32)],
        scratch_shapes=[pltpu.VMEM((nh, hd, hd), F32)],
        compiler_params=_params(("arbitrary",)),
    )(cv, cv, cv, z, z, a_log, dt_bias, gn)


def dn_bwd(cv, z, col_gate, col_ab, a_log, dt_bias, gn, states, t_invs, dy, name):
    t = cv.shape[0]
    nh = a_log.shape[0]
    hd, c = DN_HEAD_DIM, DN_CHUNK
    w = nh * hd
    n_chunks = t // c
    rev = lambda n: n_chunks - 1 - n
    qs, ks, vs, gs, ab, scal, gnspec, st = _dn_specs(nh, col_gate, col_ab, rev)
    yspec = pl.BlockSpec((c, w), lambda n: (rev(n), 0))

    def body(q_ref, k_ref, v_ref, g_ref, ab_ref, al_ref, dt_ref, gn_ref, st_ref, ti_ref, dy_ref,
             dc_ref, dg_ref, dab_ref, dal_ref, ddt_ref, dgn_ref, ds_scr):
        @pl.when(pl.program_id(0) == 0)
        def _():
            ds_scr[...] = jnp.zeros_like(ds_scr)
            dal_ref[...] = jnp.zeros_like(dal_ref)
            ddt_ref[...] = jnp.zeros_like(ddt_ref)
            dgn_ref[...] = jnp.zeros_like(dgn_ref)

        abv = ab_ref[...]
        a_col = jnp.stack([_lane_pick(abv, h) for h in range(nh)])
        b_col = jnp.stack([_lane_pick(abv, nh + h) for h in range(nh)])
        t_inv = ti_ref[0]
        fn = functools.partial(_dn_chunk, k=_dn_consts(), inv=lambda a_mat: _tri_inv_saved(a_mat, t_inv))
        _, vjp = jax.vjp(fn, _dn_heads(q_ref, nh), _dn_heads(k_ref, nh), _dn_heads(v_ref, nh), _dn_heads(g_ref, nh),
                         a_col, b_col, al_ref[...], dt_ref[...], gn_ref[...], st_ref[0])
        dq, dk, dv, dg, da, db, dal, ddt, dgn, ds = vjp((_dn_heads(dy_ref, nh), ds_scr[...]))
        lane = lax.broadcasted_iota(jnp.int32, (c, LANE), 1)
        dab = jnp.zeros((c, LANE), F32)
        for h in range(nh):
            cols = slice(h * hd, (h + 1) * hd)
            dc_ref[:, cols] = dq[h]
            dc_ref[:, w + h * hd:w + (h + 1) * hd] = dk[h]
            dc_ref[:, 2 * w + h * hd:2 * w + (h + 1) * hd] = dv[h]
            dg_ref[:, cols] = dg[h].astype(dg_ref.dtype)
            dab = dab + jnp.where(lane == h, da[h], 0.0) + jnp.where(lane == nh + h, db[h], 0.0)
        dab_ref[...] = dab.astype(dab_ref.dtype)
        dal_ref[...] += dal
        ddt_ref[...] += ddt
        dgn_ref[...] += dgn
        ds_scr[...] = ds

    sshape = jax.ShapeDtypeStruct((nh, 1, 1), F32)
    res = pl.pallas_call(
        body, name=name, grid=(n_chunks,),
        in_specs=[qs, ks, vs, gs, ab, scal, scal, gnspec, st,
                  pl.BlockSpec((1, nh, c, c), lambda n: (rev(n), 0, 0, 0)), yspec],
        out_specs=[pl.BlockSpec((c, 3 * w), lambda n: (rev(n), 0)), yspec,
                   pl.BlockSpec((c, LANE), lambda n: (rev(n), 0)), scal, scal, gnspec],
        out_shape=[jax.ShapeDtypeStruct((t, 3 * w), F32), jax.ShapeDtypeStruct((t, w), CD),
                   jax.ShapeDtypeStruct((t, LANE), CD), sshape, sshape, jax.ShapeDtypeStruct((1, hd), F32)],
        scratch_shapes=[pltpu.VMEM((nh, hd, hd), F32)],
        compiler_params=_params(("arbitrary",)),
    )(cv, cv, cv, z, z, a_log, dt_bias, gn, states, t_invs, dy)
    return res


def merge_fwd(ya, wpa, yd, wpd, z, col_m, name):
    t, d = yd.shape[0], wpd.shape[1]
    tm, tn = _tile(t, 512, LANE), _tile(d, 256, LANE)
    nb1, nb2 = col_m // tn, (col_m + d) // tn

    def body(ya_ref, wpa_ref, yd_ref, wpd_ref, z1_ref, z2_ref, m_ref, pa_ref, pd_ref):
        pa = _nn(ya_ref[...], wpa_ref[...])
        pd = _nn(yd_ref[...], wpd_ref[...])
        pa_ref[...] = pa
        pd_ref[...] = pd
        m_ref[...] = (jax.nn.sigmoid(z1_ref[...]) * pa + jax.nn.sigmoid(z2_ref[...]) * pd).astype(m_ref.dtype)

    o = pl.BlockSpec((tm, tn), lambda i, j: (i, j))
    return pl.pallas_call(
        body, name=name, grid=(t // tm, d // tn),
        in_specs=[pl.BlockSpec((tm, ya.shape[1]), lambda i, j: (i, 0)),
                  pl.BlockSpec((wpa.shape[0], tn), lambda i, j: (0, j)),
                  pl.BlockSpec((tm, yd.shape[1]), lambda i, j: (i, 0)),
                  pl.BlockSpec((wpd.shape[0], tn), lambda i, j: (0, j)),
                  pl.BlockSpec((tm, tn), lambda i, j: (i, nb1 + j)),
                  pl.BlockSpec((tm, tn), lambda i, j: (i, nb2 + j))],
        out_specs=[o, o, o],
        out_shape=[jax.ShapeDtypeStruct((t, d), CD), jax.ShapeDtypeStruct((t, d), F32),
                   jax.ShapeDtypeStruct((t, d), F32)],
        compiler_params=_params(("parallel", "parallel")),
    )(ya, wpa, yd, wpd, z, z)


def merge_bwd(dm, pa, pd, z, col_m, name):
    t, d = dm.shape
    tm, tn = _tile(t, 512, 8), _tile(d, 256, LANE)
    nb1, nb2 = col_m // tn, (col_m + d) // tn

    def body(dm_ref, pa_ref, pd_ref, z1_ref, z2_ref, dpa_ref, dpd_ref, dz1_ref, dz2_ref):
        dmv = dm_ref[...]
        s1, s2 = jax.nn.sigmoid(z1_ref[...]), jax.nn.sigmoid(z2_ref[...])
        dpa_ref[...] = (dmv * s1).astype(dpa_ref.dtype)
        dpd_ref[...] = (dmv * s2).astype(dpd_ref.dtype)
        dz1_ref[...] = (dmv * pa_ref[...] * s1 * (1.0 - s1)).astype(dz1_ref.dtype)
        dz2_ref[...] = (dmv * pd_ref[...] * s2 * (1.0 - s2)).astype(dz2_ref.dtype)

    o = pl.BlockSpec((tm, tn), lambda i, j: (i, j))
    sh = jax.ShapeDtypeStruct((t, d), CD)
    return pl.pallas_call(
        body, name=name, grid=(t // tm, d // tn),
        in_specs=[o, o, o, pl.BlockSpec((tm, tn), lambda i, j: (i, nb1 + j)),
                  pl.BlockSpec((tm, tn), lambda i, j: (i, nb2 + j))],
        out_specs=[o, o, o, o], out_shape=[sh, sh, sh, sh],
        compiler_params=_params(("parallel", "parallel")),
    )(dm, pa, pd, z, z)


def ada_fwd(c_all, w, b, name):
    nl, d, n = w.shape
    tn = _tile(n, 384, LANE)

    def body(c_ref, w_ref, b_ref, o_ref):
        o_ref[0] = _nn(_silu(c_ref[...]), w_ref[0]) + b_ref[0]

    return pl.pallas_call(
        body, name=name, grid=(nl, n // tn),
        in_specs=[pl.BlockSpec(c_all.shape, lambda l, j: (0, 0)), pl.BlockSpec((1, d, tn), lambda l, j: (l, 0, j)),
                  pl.BlockSpec((1, 1, tn), lambda l, j: (l, 0, j))],
        out_specs=pl.BlockSpec((1, c_all.shape[0], tn), lambda l, j: (l, 0, j)),
        out_shape=jax.ShapeDtypeStruct((nl, c_all.shape[0], n), F32),
        compiler_params=_params(("parallel", "parallel")),
    )(c_all, w, b)


def ada_bwd(c_pad, dmod_pad, name):
    nl, kp, n = dmod_pad.shape
    d = c_pad.shape[1]
    tn = _tile(n, 384, LANE)

    def body(c_ref, g_ref, o_ref):
        o_ref[0] = _tn(_silu(c_ref[...]), g_ref[0])

    return pl.pallas_call(
        body, name=name, grid=(nl, n // tn),
        in_specs=[pl.BlockSpec((kp, d), lambda l, j: (0, 0)), pl.BlockSpec((1, kp, tn), lambda l, j: (l, 0, j))],
        out_specs=pl.BlockSpec((1, d, tn), lambda l, j: (l, 0, j)),
        out_shape=jax.ShapeDtypeStruct((nl, d, n), F32),
        compiler_params=_params(("parallel", "parallel")),
    )(c_pad, dmod_pad)


def loss_head(y, target, name):
    t, d = y.shape
    tr = _tile(t, 256, 8)

    def body(y_ref, t_ref, dy_ref, l_ref):
        err = y_ref[...] - t_ref[...]
        dy_ref[...] = err * (1.0 / d)
        part = jnp.sum(jnp.sum(err * err, axis=1, keepdims=True), axis=0, keepdims=True) * (0.5 / d)

        @pl.when(pl.program_id(0) == 0)
        def _():
            l_ref[...] = jnp.zeros_like(l_ref)

        l_ref[...] += part

    row = pl.BlockSpec((tr, d), lambda i: (i, 0))
    return pl.pallas_call(
        body, name=name, grid=(t // tr,), in_specs=[row, row],
        out_specs=[row, pl.BlockSpec((8, LANE), lambda i: (0, 0))],
        out_shape=[jax.ShapeDtypeStruct((t, d), F32), jax.ShapeDtypeStruct((8, LANE), F32)],
        compiler_params=_params(("arbitrary",)),
    )(y, target)


def _adamw_update(g, w_ref, m_ref, v_ref, go_ref, d_ref, mo_ref, vo_ref):
    m_new = ADAM_B1 * m_ref[...] + (1.0 - ADAM_B1) * g
    v_new = ADAM_B2 * v_ref[...] + (1.0 - ADAM_B2) * jnp.square(g)
    m_hat = m_new / (1.0 - ADAM_B1 ** ADAM_STEP)
    v_hat = v_new / (1.0 - ADAM_B2 ** ADAM_STEP)
    go_ref[...] = g
    d_ref[...] = -ADAM_LR * (m_hat / (jnp.sqrt(v_hat) + ADAM_EPS) + ADAM_WD * w_ref[...])
    mo_ref[...] = m_new
    vo_ref[...] = v_new


def adamw(w, m, v, g_slots, name):
    shape = w.shape
    nslot = g_slots.shape[0]
    if w.ndim == 2:
        w3, m3, v3, g4 = w[None], m[None], v[None], g_slots[:, None]
    else:
        w3, m3, v3, g4 = w, m, v, g_slots
    nl, r, c = w3.shape
    tr = _tile(r, 256, 16)

    def body(w_ref, m_ref, v_ref, g_ref, *outs):
        g = g_ref[0].astype(F32)
        for s in range(1, nslot):
            g = g + g_ref[s].astype(F32)
        _adamw_update(g, w_ref, m_ref, v_ref, *outs)

    blk = pl.BlockSpec((1, tr, c), lambda l, i: (l, i, 0))
    gblk = pl.BlockSpec((nslot, 1, tr, c), lambda l, i: (0, l, i, 0))
    sh = jax.ShapeDtypeStruct(w3.shape, F32)
    outs = pl.pallas_call(
        body, name=name, grid=(nl, r // tr), in_specs=[blk, blk, blk, gblk], out_specs=[blk] * 4,
        out_shape=[sh] * 4, compiler_params=_params(("parallel", "parallel")),
    )(w3, m3, v3, g4)
    return tuple(o.reshape(shape) for o in outs)


def adamw_layers(w, m, v, g_layers, first, name, prev=None):
    _, r, c = w.shape
    n = len(g_layers)
    nslot, _, cg = g_layers[0].shape
    tr = _tile(r, 256, 16)

    def body(w_ref, m_ref, v_ref, *rest):
        g_refs, outs = rest[:n], rest[-4:]
        for li in range(n):
            @pl.when(pl.program_id(0) == li)
            def _(g_ref=g_refs[li]):
                g = g_ref[0, :, :c].astype(F32)
                for s in range(1, nslot):
                    g = g + g_ref[s, :, :c].astype(F32)
                _adamw_update(g[None], w_ref, m_ref, v_ref, *outs)

    blk = pl.BlockSpec((1, tr, c), lambda l, i: (first + l, i, 0))
    gblks = [pl.BlockSpec((nslot, tr, cg), lambda l, i, li=li: (0, jnp.where(l == li, i, 0), 0)) for li in range(n)]
    sh = jax.ShapeDtypeStruct(w.shape, F32)
    extra = [] if prev is None else list(prev)
    return tuple(pl.pallas_call(
        body, name=name, grid=(n, r // tr),
        in_specs=[blk, blk, blk] + gblks + [pl.BlockSpec(memory_space=pl.ANY)] * len(extra),
        out_specs=[blk] * 4, out_shape=[sh] * 4,
        input_output_aliases={3 + n + k: k for k in range(len(extra))},
        compiler_params=_params(("arbitrary", "arbitrary")),
    )(w, m, v, *g_layers, *extra))


def _rows(flat, unit=16):
    n = flat.shape[0]
    per = 1024 * unit
    pad = (-n) % per
    if pad:
        flat = jnp.concatenate([flat, jnp.zeros((pad,), flat.dtype)])
    return flat.reshape(-1, 1024)


def _class_rows(x, dil):
    if dil == 1:
        return x
    t, w = x.shape
    return x.reshape(t // dil, dil, w).transpose(1, 0, 2).reshape(t, w)


def _token_rows(x, dil):
    if dil == 1:
        return x
    t, w = x.shape
    return x.reshape(dil, t // dil, w).transpose(1, 0, 2).reshape(t, w)


def _to_classes(a):
    return jnp.stack([_class_rows(a[gi], dil) for gi, dil in enumerate(DILATIONS)])


def _from_classes(a):
    return jnp.stack([_token_rows(a[gi], dil) for gi, dil in enumerate(DILATIONS)])


def _unshard_cols(g):
    return g.transpose(1, 0, 2).reshape(g.shape[1], -1)


def _shard_cols(full):
    r = full.shape[0]
    return full.reshape(r, N_DEV, -1).transpose(1, 0, 2)


def kernel(x, c, ada_w, ada_b, norm_ff1, ffn1_w_up, ffn1_w_down, norm_mix, w_in, q_norm, k_norm, conv_w, a_log, dt_bias, dn_norm, w_proj_att, w_proj_dn, w_out, norm_ff2, ffn2_w_up, ffn2_w_down, loss_target, m_ada_w, m_ada_b, m_norm_ff1, m_ffn1_w_up, m_ffn1_w_down, m_norm_mix, m_w_in, m_q_norm, m_k_norm, m_conv_w, m_a_log, m_dt_bias, m_dn_norm, m_w_proj_att, m_w_proj_dn, m_w_out, m_norm_ff2, m_ffn2_w_up, m_ffn2_w_down, v_ada_w, v_ada_b, v_norm_ff1, v_ffn1_w_up, v_ffn1_w_down, v_norm_mix, v_w_in, v_q_norm, v_k_norm, v_conv_w, v_a_log, v_dt_bias, v_dn_norm, v_w_proj_att, v_w_proj_dn, v_w_out, v_norm_ff2, v_ffn2_w_up, v_ffn2_w_down):
    nl = ada_w.shape[0]
    t, d = x.shape[1], x.shape[2]
    dff = ffn1_w_down.shape[1] * N_DEV
    ha = d // 256
    ng = len(DILATIONS)
    wa = ha * ATT_HEAD_DIM
    att_w = ng * wa
    nh = d // DN_HEAD_DIM
    dn_w = nh * DN_HEAD_DIM
    n_in = w_in.shape[2] * N_DEV
    off_dn, off_gate = 3 * att_w, 3 * att_w + 3 * dn_w
    off_a = off_gate + dn_w
    off_merge = off_a + 2 * nh
    assert off_merge + 2 * d == n_in
    col_dn, col_gate, col_m = 0, 3 * dn_w, 4 * dn_w
    col_att = col_m + 2 * d
    col_ab = col_att + 3 * att_w
    zw = col_ab + 2 * LANE
    me = 4 * lax.axis_index("x") + 2 * lax.axis_index("y") + lax.axis_index("c")
    xs = x[0]
    target = loss_target[0]

    conv_rows = _rows(conv_w.reshape(-1), 8)
    pack0 = jnp.concatenate([jnp.concatenate([c, jnp.zeros((7, d), F32)]).reshape(-1), conv_rows.reshape(-1)])
    pack0 = _rows(pack0, 8)
    g0 = all_gather([pack0], "ag_c_conv")[0].reshape(N_DEV, -1)
    c_all = g0[:, :d]
    cw = g0[:, 8 * d:8 * d + conv_w.size].reshape(N_DEV, nl, CONV_WIDTH, -1)
    conv_full = cw.transpose(1, 2, 0, 3).reshape(nl, CONV_WIDTH, 3 * dn_w)

    n_ada = ada_w.shape[2]
    b_mine = lax.dynamic_slice_in_dim(ada_b, me * n_ada, n_ada, axis=1)[:, None, :]
    mod_s = ada_fwd(c_all, ada_w, b_mine, "ada_fwd")
    gm = all_gather([mod_s], "ag_mod")[0]
    mod = lax.dynamic_index_in_dim(gm, me, axis=2, keepdims=False)
    mod = mod.transpose(1, 0, 2).reshape(nl, N_ADA, 1, d)

    kinds = [ffn1_w_up, ffn1_w_down, w_in, w_proj_att, w_proj_dn, w_out, ffn2_w_up, ffn2_w_down]
    c_up = ffn1_w_up.shape[2]
    cp = -(-c_up // LANE) * LANE
    r_dn = ffn1_w_down.shape[1]
    assert 2 * r_dn == c_up

    up1_t, up2_t = jnp.swapaxes(ffn1_w_up, 1, 2), jnp.swapaxes(ffn2_w_up, 1, 2)
    w_in_t = jnp.transpose(w_in, (2, 0, 1))

    def layer_shards(l):
        def pad_up(wt):
            return jnp.pad(wt.astype(CD), ((0, cp - c_up), (0, 0)))
        return [pad_up(up1_t[l]), ffn1_w_down[l].astype(CD), w_in_t[:, l].astype(CD), w_proj_att[l].astype(CD),
                w_proj_dn[l].astype(CD), w_out[l].astype(CD), pad_up(up2_t[l]), ffn2_w_down[l].astype(CD)]

    def gather_landing(shards):
        return [lax.dynamic_update_index_in_dim(lax.empty((N_DEV,) + s.shape, s.dtype), s, me, 0) for s in shards]

    def slot_landing(arrs):
        return [lax.dynamic_update_index_in_dim(lax.empty(a.shape, a.dtype),
                                                lax.dynamic_index_in_dim(a, me, 0, keepdims=False), me, 0) for a in arrs]

    gathered = all_gather(layer_shards(0), "ag_weights0")
    prefetch = None

    def full_weight(ki, l):
        blk = gathered[ki]
        if ki == 3:
            return _unshard_cols(blk)
        if ki == 2:
            return w_in_rows(blk.reshape(n_in, d))
        if ki in (1, 7):
            pairs = blk.reshape(HALF, c_up, d)
            return jnp.pad(pairs, ((0, 0), (0, cp - c_up), (0, 0))).reshape(HALF * cp, d)
        return blk.reshape(-1, blk.shape[2])

    def down_grad_slots(g):
        return g.reshape(HALF, cp, d)[:, :c_up].reshape(N_DEV, r_dn, d)

    def w_in_rows(wt):
        pad = jnp.zeros((zw - n_in, wt.shape[1]), wt.dtype)
        return jnp.concatenate([wt[off_dn:off_a], wt[off_merge:], wt[:off_dn], wt[off_a:off_merge], pad], axis=0)

    def w_in_cols_inv(g):
        return jnp.concatenate([g[:, col_att:col_ab], g[:, :col_m], g[:, col_ab:col_ab + 2 * nh], g[:, col_m:col_att]], axis=1)

    saved = []
    xc = xs
    mods = []
    for l in range(nl):
        sv = {}
        if prefetch is not None:
            gathered = exchange_wait(*prefetch[:4], xc, True, f"ag_wait{l}")
        mod_l = mod[l]
        if l + 1 < nl:
            shards = layer_shards(l + 1)
            behind = gathered[0][0, :1, :1].astype(F32) + mod[0, 0, :, :1]
            prefetch = exchange_start(shards, gather_landing(shards), behind, True, f"ag_start{l + 1}")
            mod_l = mod_l + prefetch[4][0, 0]
        mods.append(mod_l)
        sh1, sc1, gt1, sh2, sc2, gt2, sh3, sc3, gt3 = [mod_l[i] for i in range(N_ADA)]
        w_dn1, w_dn2 = full_weight(1, l), full_weight(7, l)
        win = full_weight(2, l)
        wpa, wpd, wo = full_weight(3, l), full_weight(4, l), full_weight(5, l)
        sv["w"] = (gathered[0], gathered[6], w_dn1, w_dn2, win, wpa, wpd, wo)

        def ffn(xin, g, sh, sc, gt, u_all, w_dn):
            h = norm_mod(xin, g, sc, sh, "norm_mod")
            gate, up, a = ffn_up(h, u_all, "ffn_up")
            xo, f = matmul([(a, w_dn)], "nn", F32, "ffn_down", out_scale=0.5 * gt, resid=xin, save_acc=True,
                           tm=1024, tk=4096)
            return xo, (xin, h, gate, up, a, f)

        xc, sv["ffn1"] = ffn(xc, norm_ff1[l:l + 1], sh1, sc1, gt1, gathered[0], w_dn1)

        x_mix = xc
        h2 = norm_mod(x_mix, norm_mix[l:l + 1], sc2, sh2, "norm_mod")
        z = matmul([(h2, win)], "nt", F32, "w_in", tm=1024, tn=512)
        z_att = z[:, col_att:col_ab].reshape(t, 3, ng, wa)
        qkv = jnp.stack([jnp.stack([_class_rows(z_att[:, i, gi], dil) for gi, dil in enumerate(DILATIONS)])
                         for i in range(3)])
        qg, kg = jnp.tile(q_norm[l:l + 1], (1, ha)), jnp.tile(k_norm[l:l + 1], (1, ha))
        o_cls, lse_cls = att_fwd(qkv, qg, kg, "att_fwd")
        o_tok, lse_tok = _from_classes(o_cls), _from_classes(lse_cls)
        ya = att_combine(o_tok, lse_tok, "att_combine")
        cvo = conv_fwd(z, col_dn, conv_full[l], "conv_fwd")
        al3, dt3 = a_log[l].reshape(nh, 1, 1), dt_bias[l].reshape(nh, 1, 1)
        gn = dn_norm[l:l + 1]
        yd, *states = dn_fwd(cvo, z, col_gate, col_ab, al3, dt3, gn, "dn_fwd")
        mrg, pa, pd = merge_fwd(ya, wpa, yd, wpd, z, col_m, "merge_fwd")
        xc, f2 = matmul([(mrg, wo)], "nn", F32, "w_out", out_scale=gt2, resid=x_mix, save_acc=True)
        sv["mix"] = (x_mix, h2, z, qkv, o_tok, lse_tok, ya, cvo, yd, states, mrg, pa, pd, f2)

        xc, sv["ffn2"] = ffn(xc, norm_ff2[l:l + 1], sh3, sc3, gt3, gathered[6], w_dn2)
        saved.append(sv)

    dxc, loss_blk = loss_head(xc, target, "loss_head")

    def slots(ki, g):
        if ki in (0, 1, 6, 7):
            return g
        return _shard_cols(g) if ki in (2, 3) else g.reshape(N_DEV, -1, g.shape[1])

    gbig = [None] * len(kinds)
    dmods, small, pending = [], [], []
    token = None
    for l in reversed(range(nl)):
        sv = saved[l]
        mod_l = mods[l] if token is None else mods[l] + token[0, 0]
        sh1, sc1, gt1, sh2, sc2, gt2, sh3, sc3, gt3 = [mod_l[i] for i in range(N_ADA)]
        u_all1, u_all2, w_dn1, w_dn2, win, wpa, wpd, wo = sv["w"]

        def ffn_bwd(dxo, g, sh, sc, gt, u_all, w_dn, sv_f):
            xin, h, gate, up, a, f = sv_f
            s = 0.5 * gt
            g_dn = down_grad_slots(matmul([(a, dxo)], "tn", CD, "ffn_down_wg", out_scale=s, tm=1024, tn=1024))
            dgate, dup = ffn_dact(dxo, s, w_dn, gate, up, "ffn_dact")
            g_up = ffn_up_wg(h, dgate, dup, cp, "ffn_up_wg")
            dh = ffn_up_dg(dgate, dup, u_all, "ffn_up_dg")
            dx, dg, dsc, dsh, dgt = norm_mod_bwd(xin, g, sc, sh, dh, dxo, f, 0.5, "norm_mod_bwd")
            return dx, g_up, g_dn, (dg, dsc, dsh, dgt)

        def start_exchange(ids, tag, behind):
            send = [slots(ki, gbig[ki]) for ki in ids]
            started = exchange_start(send, slot_landing(send), behind, False, f"a2a_start{l}{tag}")
            pending.append((l, ids, tag, started))
            return started[4]

        dxc, gbig[6], gbig[7], (dg3, dsc3, dsh3, dgt3) = ffn_bwd(
            dxc, norm_ff2[l:l + 1], sh3, sc3, gt3, u_all2, w_dn2, sv["ffn2"])
        token = start_exchange([6, 7], "f2", dxc)
        sh2, sc2, gt2 = (vec + token[0, 0] for vec in (sh2, sc2, gt2))

        x_mix, h2, z, qkv, o_tok, lse_tok, ya, cvo, yd, states, mrg, pa, pd, f2 = sv["mix"]
        gbig[5] = matmul([(mrg, dxc)], "tn", CD, "w_out_wg", out_scale=gt2)
        dm = matmul([(dxc, wo)], "nt", F32, "w_out_dg", a_scale=gt2)
        dpa, dpd, dz1, dz2 = merge_bwd(dm, pa, pd, z, col_m, "merge_bwd")
        gbig[3] = matmul([(ya, dpa)], "tn", CD, "patt_wg")
        gbig[4] = matmul([(yd, dpd)], "tn", CD, "pdn_wg")
        dya = matmul([(dpa, wpa)], "nt", F32, "patt_dg")
        dyd = matmul([(dpd, wpd)], "nt", F32, "pdn_dg")
        do_tok, dlse_tok = att_combine_bwd(o_tok, lse_tok, dya, "att_combine_bwd")
        qg, kg = jnp.tile(q_norm[l:l + 1], (1, ha)), jnp.tile(k_norm[l:l + 1], (1, ha))
        dq, dk, dv, dqg, dkg = att_bwd(qkv, qg, kg, _to_classes(do_tok), _to_classes(dlse_tok), "att_bwd")
        dqg, dkg = (jnp.sum(v_.reshape(ha, ATT_HEAD_DIM), axis=0, keepdims=True) for v_ in (dqg, dkg))
        dz_att = jnp.concatenate([_token_rows(a_[gi], dil).astype(CD) for a_ in (dq, dk, dv)
                                  for gi, dil in enumerate(DILATIONS)], axis=1)
        al3, dt3 = a_log[l].reshape(nh, 1, 1), dt_bias[l].reshape(nh, 1, 1)
        gn = dn_norm[l:l + 1]
        dcvo, dz_gate, dz_ab, dal, ddt, dgn = dn_bwd(
            cvo, z, col_gate, col_ab, al3, dt3, gn, *states, dyd, "dn_bwd")
        dz_dn, dconv = conv_bwd(dcvo, z, col_dn, conv_full[l], "conv_bwd")
        dz = jnp.concatenate([dz_dn, dz_gate, dz1, dz2, dz_att, dz_ab, jnp.zeros((t, LANE), CD)], axis=1)
        gbig[2] = w_in_cols_inv(matmul([(h2, dz)], "tn", CD, "w_in_wg", tm=1024, tn=zw // 4))
        dh2 = matmul([(dz, win)], "nn", F32, "w_in_dg", tm=1024, tn=1024,
                     tk=zw // 4 if (zw // 4) % LANE == 0 else zw)
        dxc, dg2, dsc2, dsh2, dgt2 = norm_mod_bwd(x_mix, norm_mix[l:l + 1], sc2, sh2, dh2, dxc, f2, 1.0, "norm_mod_bwd_mix")

        token = start_exchange([2, 3, 4, 5], "mx", dxc)
        sh1, sc1, gt1 = (vec + token[0, 0] for vec in (sh1, sc1, gt1))
        dxc, gbig[0], gbig[1], (dg1, dsc1, dsh1, dgt1) = ffn_bwd(
            dxc, norm_ff1[l:l + 1], sh1, sc1, gt1, u_all1, w_dn1, sv["ffn1"])
        if l > 0:
            token = start_exchange([0, 1], "f1", dxc)
        dmods.append(jnp.concatenate([dsh1, dsc1, dgt1, dsh2, dsc2, dgt2, dsh3, dsc3, dgt3], axis=1))
        small.append((dg1, dg2, dg3, dconv, dqg, dkg, dal.reshape(1, nh), ddt.reshape(1, nh), dgn))
    dmods.reverse()
    small.reverse()

    big_names = ["ffn1_w_up", "ffn1_w_down", "w_in", "w_proj_att", "w_proj_dn", "w_out", "ffn2_w_up", "ffn2_w_down"]
    big_m = [m_ffn1_w_up, m_ffn1_w_down, m_w_in, m_w_proj_att, m_w_proj_dn, m_w_out, m_ffn2_w_up, m_ffn2_w_down]
    big_v = [v_ffn1_w_up, v_ffn1_w_down, v_w_in, v_w_proj_att, v_w_proj_dn, v_w_out, v_ffn2_w_up, v_ffn2_w_down]
    recv_layers = [[None] * len(kinds) for _ in range(nl)]

    def wait_exchanges(layers, behind):
        for pl_, ids, tag, st in pending:
            if pl_ in layers:
                got = exchange_wait(*st[:4], behind, False, f"a2a_wait{pl_}{tag}")
                for ki, arr in zip(ids, got):
                    recv_layers[pl_][ki] = arr

    def big_adamw(ki, layers, prev, tag):
        g_layers = [recv_layers[li][ki] for li in layers]
        wmv = (kinds[ki], big_m[ki], big_v[ki])
        if ki in (0, 6):
            wmv = tuple(jnp.swapaxes(a, 1, 2) for a in wmv)
        return adamw_layers(*wmv, g_layers, layers[0], f"adamw_{big_names[ki]}{tag}", prev)

    later = list(range(1, nl))
    partial = [None] * len(kinds)
    if later:
        wait_exchanges(later, dxc)
        partial = [big_adamw(ki, later, None, "_l1") for ki in range(len(kinds))]

    fields = [jnp.stack(dmods).reshape(-1)]
    fields += [jnp.stack([s[i] for s in small]).reshape(-1) for i in range(9)]
    fields.append(loss_blk[0, :1])
    if later:
        fields.append(partial[-1][1][1, :1, 0] * 0.0)
    fsizes = [f.size for f in fields]
    foffs = [sum(fsizes[:i]) for i in range(len(fields))]
    g1 = all_gather([_rows(jnp.concatenate(fields), 8)], "ag_small")[0].reshape(N_DEV, -1)
    l = 0
    start_exchange([0, 1], "f1", g1)

    def field(i, shape):
        return g1[:, foffs[i]:foffs[i] + fsizes[i]].reshape(N_DEV, *shape)

    loss = field(10, (1,))[0, 0]
    for j in range(1, N_DEV):
        loss = loss + field(10, (1,))[j, 0]

    results = {}
    dmod_all = field(0, (nl, N_ADA * d))
    c_pad = jnp.concatenate([c_all, jnp.zeros((8, d), F32)])
    dmod_mine = lax.dynamic_slice_in_dim(dmod_all, me * n_ada, n_ada, axis=2).transpose(1, 0, 2)
    dmod_pad = jnp.concatenate([dmod_mine, jnp.zeros((nl, 8, n_ada), F32)], axis=1)
    g_ada_w = ada_bwd(c_pad, dmod_pad, "ada_bwd")
    results["ada_w"] = adamw(ada_w, m_ada_w, v_ada_w, g_ada_w[None], "adamw_ada_w")
    results["ada_b"] = adamw(ada_b, m_ada_b, v_ada_b, dmod_all, "adamw_ada_b")
    results["norm_ff1"] = adamw(norm_ff1, m_norm_ff1, v_norm_ff1, field(1, (nl, d)), "adamw_norm_ff1")
    results["norm_mix"] = adamw(norm_mix, m_norm_mix, v_norm_mix, field(2, (nl, d)), "adamw_norm_mix")
    results["norm_ff2"] = adamw(norm_ff2, m_norm_ff2, v_norm_ff2, field(3, (nl, d)), "adamw_norm_ff2")
    conv_slots = lax.dynamic_slice_in_dim(field(4, (nl, CONV_WIDTH, 3 * dn_w)), me * conv_w.shape[2],
                                          conv_w.shape[2], axis=3)
    results["conv_w"] = adamw(conv_w, m_conv_w, v_conv_w, conv_slots, "adamw_conv_w")
    results["q_norm"] = adamw(q_norm, m_q_norm, v_q_norm, field(5, (nl, ATT_HEAD_DIM)), "adamw_q_norm")
    results["k_norm"] = adamw(k_norm, m_k_norm, v_k_norm, field(6, (nl, ATT_HEAD_DIM)), "adamw_k_norm")
    results["a_log"] = adamw(a_log, m_a_log, v_a_log, field(7, (nl, nh)), "adamw_a_log")
    results["dt_bias"] = adamw(dt_bias, m_dt_bias, v_dt_bias, field(8, (nl, nh)), "adamw_dt_bias")
    results["dn_norm"] = adamw(dn_norm, m_dn_norm, v_dn_norm, field(9, (nl, DN_HEAD_DIM)), "adamw_dn_norm")
    wait_exchanges([0], results["ada_w"][1])
    for ki, nm in enumerate(big_names):
        res = big_adamw(ki, [0], partial[ki], "_l0")
        results[nm] = tuple(jnp.swapaxes(r, 1, 2) for r in res) if ki in (0, 6) else res

    order = ["ada_w", "ada_b", "norm_ff1", "ffn1_w_up", "ffn1_w_down", "norm_mix", "w_in", "q_norm", "k_norm",
             "conv_w", "a_log", "dt_bias", "dn_norm", "w_proj_att", "w_proj_dn", "w_out", "norm_ff2",
             "ffn2_w_up", "ffn2_w_down"]
    outs = [loss, dxc[None]]
    for part in range(4):
        outs += [results[n][part] for n in order]
    return tuple(outs)
```

```python
import functools

import jax
import jax.numpy as jnp
from jax import lax
from jax.experimental import pallas as pl
from jax.experimental.pallas import tpu as pltpu

F32 = jnp.float32
CD = jnp.bfloat16
EPS = 1e-6
N_DEV = 8
LANE = 128
ATT_HEAD_DIM = 64
ATT_BLOCK = 128
DILATIONS = (1, 4, 16)
DN_HEAD_DIM = 128
DN_CHUNK = 64
CONV_WIDTH = 4
N_ADA = 9
ADAM_LR, ADAM_B1, ADAM_B2, ADAM_EPS, ADAM_WD, ADAM_STEP = 0.001, 0.9, 0.999, 1e-08, 0.01, 10
VMEM_LIMIT = 56 * 1024 * 1024
NEG = -1e30
MESH = pl.DeviceIdType.MESH


def _params(sem=None):
    return pltpu.CompilerParams(dimension_semantics=sem, vmem_limit_bytes=VMEM_LIMIT)


def _tile(n, pref, unit):
    best = None
    t = unit
    while t <= min(n, pref):
        if n % t == 0:
            best = t
        t += unit
    return best if best is not None else n


def _silu(x):
    return x * jax.nn.sigmoid(x)


def _dot(a, b, dims, precision=None):
    if precision is None:
        a, b = a.astype(CD), b.astype(CD)
    return lax.dot_general(a, b, (dims, ((), ())), precision=precision, preferred_element_type=F32)


def _nn(a, b, precision=None):
    return _dot(a, b, ((1,), (0,)), precision)


def _nt(a, b, precision=None):
    return _dot(a, b, ((1,), (1,)), precision)


def _tn(a, b, precision=None):
    return _dot(a, b, ((0,), (0,)), precision)


def all_gather(arrs, name):
    n = len(arrs)

    def body(*refs):
        x_refs, out_refs = refs[:n], refs[n:2 * n]
        send_sems, recv_sems, local_sems = refs[2 * n:]
        x, y, c = lax.axis_index("x"), lax.axis_index("y"), lax.axis_index("c")
        me, sibling = (x, y, c), (x, y, 1 - c)
        chips = [(1 - x, y), (x, 1 - y), (1 - x, 1 - y)]

        def copy(a, k, block, to, src=None):
            slot = out_refs[a].at[4 * block[0] + 2 * block[1] + block[2]]
            return pltpu.make_async_remote_copy(
                src_ref=slot if src is None else src, dst_ref=slot,
                send_sem=send_sems.at[7 * a + k], recv_sem=recv_sems.at[7 * a + k],
                device_id=to, device_id_type=MESH)

        mine, first, passed = [], [], []
        for a in range(n):
            cp = pltpu.make_async_copy(x_refs[a], out_refs[a].at[4 * x + 2 * y + c], local_sems.at[a])
            cp.start()
            mine.append(cp)
            first.append(copy(a, 0, me, sibling, src=x_refs[a]))
            first += [copy(a, 1 + j, me, (*chip, c), src=x_refs[a]) for j, chip in enumerate(chips)]
        for cp in first:
            cp.start()
        for j, chip in enumerate(chips):
            for a in range(n):
                copy(a, 1 + j, (*chip, c), me).wait_recv()
                cp = copy(a, 4 + j, (*chip, c), sibling)
                cp.start()
                passed.append(cp)
        for a in range(n):
            copy(a, 0, sibling, me).wait_recv()
            for j, chip in enumerate(chips):
                copy(a, 4 + j, (*chip, 1 - c), me).wait_recv()
        for cp in first + passed:
            cp.wait_send()
        for cp in mine:
            cp.wait()

    hbm = pl.BlockSpec(memory_space=pl.ANY)
    outs = pl.pallas_call(
        body, name=name,
        out_shape=[jax.ShapeDtypeStruct((N_DEV,) + a.shape, a.dtype) for a in arrs],
        in_specs=[hbm] * n, out_specs=[hbm] * n,
        scratch_shapes=[pltpu.SemaphoreType.DMA((7 * n,)), pltpu.SemaphoreType.DMA((7 * n,)),
                        pltpu.SemaphoreType.DMA((n,))],
    )(*arrs)
    return list(outs)


_HBM = pl.BlockSpec(memory_space=pltpu.HBM)
_SEM = pl.BlockSpec(memory_space=pltpu.SEMAPHORE)
_EFFECT = pltpu.SideEffectType.DATAFLOW_SIDE_EFFECTING


def _exchange_copies(src_refs, land_refs, send_sem, recv_sem, gather):
    x, y, c = lax.axis_index("x"), lax.axis_index("y"), lax.axis_index("c")
    me = 4 * x + 2 * y + c
    pairs = []
    for k in range(1, N_DEV):
        px, py, pc = x ^ (k >> 2), y ^ ((k >> 1) & 1), c ^ (k & 1)
        pidx = 4 * px + 2 * py + pc
        for src, land in zip(src_refs, land_refs):
            mine = src if gather else src.at[pidx]
            out = pltpu.make_async_remote_copy(src_ref=mine, dst_ref=land.at[me], send_sem=send_sem, recv_sem=recv_sem,
                                               device_id=(px, py, pc), device_id_type=MESH)
            inc = pltpu.make_async_remote_copy(src_ref=mine, dst_ref=land.at[pidx], send_sem=send_sem, recv_sem=recv_sem,
                                               device_id=(px, py, pc), device_id_type=MESH)
            pairs.append((out, inc))
    return pairs


def exchange_start(srcs, lands, after, gather, name):
    n = len(srcs)

    def body(*refs):
        src_refs, land_refs = refs[:n], refs[n:2 * n]
        send_sem, recv_sem = refs[2 * n + 1], refs[2 * n + 2]
        token = refs[-1]
        for out, _ in _exchange_copies(src_refs, land_refs, send_sem, recv_sem, gather):
            out.start()
        token[...] = jnp.zeros_like(token)

    res = pl.pallas_call(
        body, name=name,
        out_shape=(pltpu.SemaphoreType.DMA(()), pltpu.SemaphoreType.DMA(()),
                   *[pltpu.HBM(a.shape, a.dtype) for a in srcs], *[pltpu.HBM(a.shape, a.dtype) for a in lands],
                   jax.ShapeDtypeStruct((8, LANE), F32)),
        in_specs=[_HBM] * (2 * n) + [pl.BlockSpec(memory_space=pl.ANY)],
        out_specs=(_SEM, _SEM, *[_HBM] * (2 * n), pl.BlockSpec(memory_space=pltpu.VMEM)),
        input_output_aliases={i: 2 + i for i in range(2 * n)},
        compiler_params=pltpu.CompilerParams(has_side_effects=_EFFECT),
    )(*[pltpu.with_memory_space_constraint(a, pltpu.HBM) for a in list(srcs) + list(lands)], after)
    return res[0], res[1], list(res[2:2 + n]), list(res[2 + n:2 + 2 * n]), res[-1]


def unwritten(shapes, like, name):
    def body(*refs):
        pass

    hbm = pl.BlockSpec(memory_space=pl.ANY)
    return list(pl.pallas_call(
        body, name=name, out_shape=[jax.ShapeDtypeStruct(s, a.dtype) for s, a in zip(shapes, like)],
        in_specs=[hbm] * len(like), out_specs=[hbm] * len(like),
    )(*like))


def exchange_wait(send_sem, recv_sem, srcs, lands, after, gather, name):
    n = len(srcs)

    def body(*refs):
        src_refs, land_refs = refs[:n], refs[n:2 * n]
        s_sem, r_sem = refs[2 * n], refs[2 * n + 1]
        for out, inc in _exchange_copies(src_refs, land_refs, s_sem, r_sem, gather):
            out.wait_send()
            inc.wait_recv()

    res = pl.pallas_call(
        body, name=name,
        out_shape=[pltpu.HBM(a.shape, a.dtype) for a in list(srcs) + list(lands)],
        in_specs=[_HBM] * (2 * n) + [_SEM, _SEM, pl.BlockSpec(memory_space=pl.ANY)],
        out_specs=[_HBM] * (2 * n),
        input_output_aliases={i: i for i in range(2 * n)},
        compiler_params=pltpu.CompilerParams(has_side_effects=_EFFECT),
    )(*srcs, *lands, send_sem, recv_sem, after)
    return list(res[n:])


def matmul(pairs, mode, out_dtype, name, *, a_scale=None, out_scale=None, resid=None,
           save_acc=False, tm=512, tn=512, tk=2048):
    a0, b0 = pairs[0]
    if mode == "nn":
        (m, kdim), n = a0.shape, b0.shape[1]
    elif mode == "nt":
        (m, kdim), n = a0.shape, b0.shape[0]
    else:
        (kdim, m), n = a0.shape, b0.shape[1]
    tm, tn = _tile(m, tm, 8 if m % LANE else LANE), _tile(n, tn, LANE)
    tk = _tile(kdim, tk, LANE)
    nk = kdim // tk
    sub = tn if mode == "tn" else _tile(tn, 256, LANE)
    npairs = len(pairs)
    dims = {"nn": ((1,), (0,)), "nt": ((1,), (1,)), "tn": ((0,), (0,))}[mode]

    if mode == "nn":
        a_spec = pl.BlockSpec((tm, tk), lambda i, j, k: (i, k))
        b_spec = pl.BlockSpec((tk, tn), lambda i, j, k: (k, j))
    elif mode == "nt":
        a_spec = pl.BlockSpec((tm, tk), lambda i, j, k: (i, k))
        b_spec = pl.BlockSpec((tn, tk), lambda i, j, k: (j, k))
    else:
        a_spec = pl.BlockSpec((tk, tm), lambda i, j, k: (k, i))
        b_spec = pl.BlockSpec((tk, tn), lambda i, j, k: (k, j))

    def body(*refs):
        it = iter(refs)
        pair_refs = [(next(it), next(it)) for _ in range(npairs)]
        as_ref = next(it) if a_scale is not None else None
        os_ref = next(it) if out_scale is not None else None
        rs_ref = next(it) if resid is not None else None
        o_ref = next(it)
        acc_out = next(it) if save_acc else None
        acc_ref = next(it) if nk > 1 else None
        lhs = []
        for a_ref, _ in pair_refs:
            a = a_ref[...]
            if as_ref is not None:
                a = a.astype(F32) * as_ref[...]
            lhs.append(a.astype(CD))

        def product(cols):
            part = None
            for a, (_, b_ref) in zip(lhs, pair_refs):
                b = b_ref[...] if cols is None else (b_ref[cols, :] if mode == "nt" else b_ref[:, cols])
                d = _dot(a, b, dims)
                part = d if part is None else part + d
            return part

        def finish(acc, cols=slice(None)):
            if acc_out is not None:
                acc_out[:, cols] = acc
            if os_ref is not None:
                acc = acc * os_ref[:, cols]
            if rs_ref is not None:
                acc = rs_ref[:, cols] + acc
            o_ref[:, cols] = acc.astype(o_ref.dtype)

        if nk == 1:
            for c0 in range(0, tn, sub):
                finish(product(slice(c0, c0 + sub)), slice(c0, c0 + sub))
        else:
            part = product(None)
            k = pl.program_id(2)

            @pl.when(k == 0)
            def _():
                acc_ref[...] = part

            @pl.when(k > 0)
            def _():
                acc_ref[...] += part

            @pl.when(k == nk - 1)
            def _():
                finish(acc_ref[...])

    in_specs, args = [], []
    for a, b in pairs:
        in_specs += [a_spec, b_spec]
        args += [a, b]
    if a_scale is not None:
        assert mode != "tn"
        in_specs.append(pl.BlockSpec((1, tk), lambda i, j, k: (0, k)))
        args.append(a_scale)
    if out_scale is not None:
        in_specs.append(pl.BlockSpec((1, tn), lambda i, j, k: (0, j)))
        args.append(out_scale)
    if resid is not None:
        in_specs.append(pl.BlockSpec((tm, tn), lambda i, j, k: (i, j)))
        args.append(resid)
    o_spec = pl.BlockSpec((tm, tn), lambda i, j, k: (i, j))
    out_shape = [jax.ShapeDtypeStruct((m, n), out_dtype)]
    out_specs = [o_spec]
    if save_acc:
        out_shape.append(jax.ShapeDtypeStruct((m, n), F32))
        out_specs.append(o_spec)
    scratch = [pltpu.VMEM((tm, tn), F32)] if nk > 1 else []
    res = pl.pallas_call(
        body, name=name, grid=(m // tm, n // tn, nk),
        in_specs=in_specs, out_specs=out_specs, out_shape=out_shape, scratch_shapes=scratch,
        compiler_params=_params(("parallel", "parallel", "arbitrary")),
    )(*args)
    return res if save_acc else res[0]


def _nm_fn(x, g, sc, sh):
    r = lax.rsqrt(jnp.mean(x * x, axis=-1, keepdims=True) + EPS)
    return (x * r * g) * (1.0 + sc) + sh


def norm_mod(x, g, sc, sh, name):
    t, d = x.shape
    tr = _tile(t, 256, 8)

    def body(x_ref, g_ref, sc_ref, sh_ref, h_ref):
        h_ref[...] = _nm_fn(x_ref[...], g_ref[...], sc_ref[...], sh_ref[...]).astype(h_ref.dtype)

    row = pl.BlockSpec((tr, d), lambda i: (i, 0))
    vec = pl.BlockSpec((1, d), lambda i: (0, 0))
    return pl.pallas_call(
        body, name=name, grid=(t // tr,), in_specs=[row, vec, vec, vec], out_specs=row,
        out_shape=jax.ShapeDtypeStruct((t, d), CD), compiler_params=_params(("parallel",)),
    )(x, g, sc, sh)


def norm_mod_bwd(x, g, sc, sh, dh, dxo, f, gate_scale, name):
    t, d = x.shape
    tr = _tile(t, 256, 8)

    def body(x_ref, g_ref, sc_ref, sh_ref, dh_ref, dxo_ref, f_ref, dx_ref, dg_ref, dsc_ref, dsh_ref, dgt_ref):
        _, vjp = jax.vjp(_nm_fn, x_ref[...], g_ref[...], sc_ref[...], sh_ref[...])
        dx, dg, dsc, dsh = vjp(dh_ref[...])
        dxo_v = dxo_ref[...]
        dx_ref[...] = dxo_v + dx
        dgt = gate_scale * jnp.sum(f_ref[...] * dxo_v, axis=0, keepdims=True)

        @pl.when(pl.program_id(0) == 0)
        def _():
            dg_ref[...] = dg
            dsc_ref[...] = dsc
            dsh_ref[...] = dsh
            dgt_ref[...] = dgt

        @pl.when(pl.program_id(0) > 0)
        def _():
            dg_ref[...] += dg
            dsc_ref[...] += dsc
            dsh_ref[...] += dsh
            dgt_ref[...] += dgt

    row = pl.BlockSpec((tr, d), lambda i: (i, 0))
    vec = pl.BlockSpec((1, d), lambda i: (0, 0))
    vshape = jax.ShapeDtypeStruct((1, d), F32)
    return pl.pallas_call(
        body, name=name, grid=(t // tr,), in_specs=[row, vec, vec, vec, row, row, row],
        out_specs=[row, vec, vec, vec, vec],
        out_shape=[jax.ShapeDtypeStruct((t, d), F32), vshape, vshape, vshape, vshape],
        compiler_params=_params(("arbitrary",)),
    )(x, g, sc, sh, dh, dxo, f)


HALF = N_DEV // 2


def ffn_up(h, u_all, name):
    t, d = h.shape
    cp = u_all.shape[1]
    f = HALF * cp
    tm, tn = _tile(t, 1024, LANE), cp
    per = cp // tn
    sub = _tile(tn, 256, LANE)

    def body(h_ref, wg_ref, wu_ref, g_ref, u_ref, a_ref):
        hv = h_ref[...]
        for c0 in range(0, tn, sub):
            cols = slice(c0, c0 + sub)
            gate = _nt(hv, wg_ref[0, cols, :])
            up = _nt(hv, wu_ref[0, cols, :])
            g_ref[:, cols] = gate.astype(g_ref.dtype)
            u_ref[:, cols] = up.astype(u_ref.dtype)
            a_ref[:, cols] = (_silu(gate) * up).astype(a_ref.dtype)

    o = pl.BlockSpec((tm, tn), lambda i, j: (i, j))
    wg = pl.BlockSpec((1, tn, d), lambda i, j: (j // per, j % per, 0))
    wu = pl.BlockSpec((1, tn, d), lambda i, j: (HALF + j // per, j % per, 0))
    return pl.pallas_call(
        body, name=name, grid=(t // tm, f // tn),
        in_specs=[pl.BlockSpec((tm, d), lambda i, j: (i, 0)), wg, wu], out_specs=[o, o, o],
        out_shape=[jax.ShapeDtypeStruct((t, f), CD)] * 3,
        compiler_params=_params(("parallel", "parallel")),
    )(h, u_all, u_all)


def ffn_up_wg(h, dgate, dup, cp, name):
    t, d = h.shape
    tn = _tile(d, 1024, LANE)

    def body(h_ref, dg_ref, du_ref, o_ref):
        s = pl.program_id(0)

        @pl.when(s < HALF)
        def _():
            o_ref[0] = _tn(dg_ref[...], h_ref[...]).astype(o_ref.dtype)

        @pl.when(s >= HALF)
        def _():
            o_ref[0] = _tn(du_ref[...], h_ref[...]).astype(o_ref.dtype)

    return pl.pallas_call(
        body, name=name, grid=(N_DEV, d // tn),
        in_specs=[pl.BlockSpec((t, tn), lambda s, i: (0, i)),
                  pl.BlockSpec((t, cp), lambda s, i: (0, jnp.minimum(s, HALF - 1))),
                  pl.BlockSpec((t, cp), lambda s, i: (0, jnp.maximum(s - HALF, 0)))],
        out_specs=pl.BlockSpec((1, cp, tn), lambda s, i: (s, 0, i)),
        out_shape=jax.ShapeDtypeStruct((N_DEV, cp, d), CD),
        compiler_params=_params(("parallel", "parallel")),
    )(h, dgate, dup)


def ffn_up_dg(dgate, dup, u_all, name):
    t, f = dgate.shape
    cp, d = u_all.shape[1], u_all.shape[2]
    tm, tn = _tile(t, 1024, LANE), _tile(d, 512, LANE)

    def body(dg_ref, du_ref, u_ref, o_ref):
        wg = u_ref[0:HALF].reshape(f, tn)
        wu = u_ref[HALF:N_DEV].reshape(f, tn)
        o_ref[...] = _nn(dg_ref[...], wg) + _nn(du_ref[...], wu)

    return pl.pallas_call(
        body, name=name, grid=(t // tm, d // tn),
        in_specs=[pl.BlockSpec((tm, f), lambda i, j: (i, 0)), pl.BlockSpec((tm, f), lambda i, j: (i, 0)),
                  pl.BlockSpec((N_DEV, cp, tn), lambda i, j: (0, 0, j))],
        out_specs=pl.BlockSpec((tm, tn), lambda i, j: (i, j)),
        out_shape=jax.ShapeDtypeStruct((t, d), F32),
        compiler_params=_params(("parallel", "parallel")),
    )(dgate, dup, u_all)


def ffn_dact(dxo, s, wd, gate, up, name):
    t, d = dxo.shape
    f = wd.shape[0]
    tm, tn = _tile(t, 1024, LANE), _tile(f, 768, LANE)
    sub = _tile(tn, 256, LANE)

    def body(dxo_ref, s_ref, wd_ref, g_ref, u_ref, dg_ref, du_ref):
        dxs = (dxo_ref[...] * s_ref[...]).astype(CD)
        for c0 in range(0, tn, sub):
            cols = slice(c0, c0 + sub)
            da = _nt(dxs, wd_ref[cols, :])
            gate, up = g_ref[:, cols].astype(F32), u_ref[:, cols].astype(F32)
            sg = jax.nn.sigmoid(gate)
            dg_ref[:, cols] = (da * up * sg * (1.0 + gate * (1.0 - sg))).astype(dg_ref.dtype)
            du_ref[:, cols] = (da * gate * sg).astype(du_ref.dtype)

    o = pl.BlockSpec((tm, tn), lambda i, j: (i, j))
    return pl.pallas_call(
        body, name=name, grid=(t // tm, f // tn),
        in_specs=[pl.BlockSpec((tm, d), lambda i, j: (i, 0)), pl.BlockSpec((1, d), lambda i, j: (0, 0)),
                  pl.BlockSpec((tn, d), lambda i, j: (j, 0)), o, o],
        out_specs=[o, o],
        out_shape=[jax.ShapeDtypeStruct((t, f), CD), jax.ShapeDtypeStruct((t, f), CD)],
        compiler_params=_params(("parallel", "parallel")),
    )(dxo, s, wd, gate, up)


def _bdot(a, b, ca, cb, precision=None):
    if precision is None:
        a, b = a.astype(CD), b.astype(CD)
    return lax.dot_general(a, b, (((ca,), (cb,)), ((0,), (0,))), precision=precision, preferred_element_type=F32)


def _bnn(a, b, precision=None):
    return _bdot(a, b, 2, 1, precision)


def _bnt(a, b, precision=None):
    return _bdot(a, b, 2, 2, precision)


def _btn(a, b, precision=None):
    return _bdot(a, b, 1, 1, precision)


def _att_fn(q, kp, kc, vp, vc, qg, kg, mask_p, mask_c):
    b, w = q.shape
    nh = w // ATT_HEAD_DIM
    head_of_lane = lax.broadcasted_iota(jnp.int32, (nh, 1, w), 2) // ATT_HEAD_DIM
    hm = (head_of_lane == lax.broadcasted_iota(jnp.int32, (nh, 1, w), 0)).astype(F32)

    def rn(x, g):
        ss = jnp.sum((x * x)[None] * hm, axis=-1, keepdims=True)
        r = jnp.sum(lax.rsqrt(ss * (1.0 / ATT_HEAD_DIM) + EPS) * hm, axis=0)
        return x * r * g

    qn, kcn = rn(q, qg), rn(kc, kg)
    q4 = (qn[None] * hm).reshape(nh * b, w)
    scale = ATT_HEAD_DIM ** -0.5
    sc = jnp.where(mask_c, _nt(q4, kcn) * scale, NEG)
    if kp is None:
        m = jnp.max(sc, axis=-1, keepdims=True)
        pc = jnp.exp(sc - m)
        den = jnp.sum(pc, axis=-1, keepdims=True)
        o4 = _nn(pc / den, vc)
    else:
        sp = jnp.where(mask_p, _nt(q4, rn(kp, kg)) * scale, NEG)
        m = jnp.maximum(jnp.max(sp, axis=-1, keepdims=True), jnp.max(sc, axis=-1, keepdims=True))
        pp, pc = jnp.exp(sp - m), jnp.exp(sc - m)
        den = jnp.sum(pp, axis=-1, keepdims=True) + jnp.sum(pc, axis=-1, keepdims=True)
        o4 = _nn(pp / den, vp) + _nn(pc / den, vc)
    o = jnp.sum(o4.reshape(nh, b, w) * hm, axis=0)
    lse = jnp.sum((m + jnp.log(den)).reshape(nh, b, 1) * hm, axis=0)
    return o, lse


def _att_masks(g, j, nb_total, nh):
    nb = jnp.int32(nb_total // DILATIONS[0])
    for gi in range(1, len(DILATIONS)):
        nb = jnp.where(g == gi, jnp.int32(nb_total // DILATIONS[gi]), nb)
    has_prev = (j % nb) != 0
    row = lax.broadcasted_iota(jnp.int32, (nh * ATT_BLOCK, ATT_BLOCK), 0) % ATT_BLOCK
    col = lax.broadcasted_iota(jnp.int32, (nh * ATT_BLOCK, ATT_BLOCK), 1)
    return has_prev, col >= row, col <= row


def _att_specs(w):
    blk = (1, 1, ATT_BLOCK, w)
    q = pl.BlockSpec(blk, lambda g, j: (0, g, j, 0))
    kp = pl.BlockSpec(blk, lambda g, j: (1, g, jnp.maximum(j - 1, 0), 0))
    kc = pl.BlockSpec(blk, lambda g, j: (1, g, j, 0))
    vp = pl.BlockSpec(blk, lambda g, j: (2, g, jnp.maximum(j - 1, 0), 0))
    vc = pl.BlockSpec(blk, lambda g, j: (2, g, j, 0))
    gain = pl.BlockSpec((1, w), lambda g, j: (0, 0))
    out = pl.BlockSpec((1, ATT_BLOCK, w), lambda g, j: (g, j, 0))
    return [q, kp, kc, vp, vc, gain, gain], out


def att_fwd(qkv, qg, kg, name):
    _, ng, t, w = qkv.shape
    nbt = t // ATT_BLOCK
    in_specs, out = _att_specs(w)

    def body(q_ref, kp_ref, kc_ref, vp_ref, vc_ref, qg_ref, kg_ref, o_ref, lse_ref):
        has_prev, mask_p, mask_c = _att_masks(pl.program_id(0), pl.program_id(1), nbt, w // ATT_HEAD_DIM)

        @pl.when(has_prev)
        def _():
            o, lse = _att_fn(q_ref[0, 0], kp_ref[0, 0], kc_ref[0, 0], vp_ref[0, 0], vc_ref[0, 0],
                             qg_ref[...], kg_ref[...], mask_p, mask_c)
            o_ref[0] = o
            lse_ref[0] = lse

        @pl.when(jnp.logical_not(has_prev))
        def _():
            o, lse = _att_fn(q_ref[0, 0], None, kc_ref[0, 0], None, vc_ref[0, 0],
                             qg_ref[...], kg_ref[...], None, mask_c)
            o_ref[0] = o
            lse_ref[0] = lse

    sh = jax.ShapeDtypeStruct((ng, t, w), F32)
    return pl.pallas_call(
        body, name=name, grid=(ng, nbt), in_specs=in_specs, out_specs=[out, out], out_shape=[sh, sh],
        compiler_params=_params(("parallel", "parallel")),
    )(qkv, qkv, qkv, qkv, qkv, qg, kg)


def att_bwd(qkv, qg, kg, do, dlse, name):
    _, ng, t, w = qkv.shape
    nbt = t // ATT_BLOCK
    in_specs, out = _att_specs(w)
    whole = pl.BlockSpec((1, t, w), lambda g, j: (g, 0, 0))
    gain = in_specs[-1]

    def body(q_ref, kp_ref, kc_ref, vp_ref, vc_ref, qg_ref, kg_ref, do_ref, dlse_ref,
             dq_ref, dk_ref, dv_ref, dqg_ref, dkg_ref):
        g, j = pl.program_id(0), pl.program_id(1)
        has_prev, mask_p, mask_c = _att_masks(g, j, nbt, w // ATT_HEAD_DIM)

        @pl.when(j == 0)
        def _():
            dk_ref[...] = jnp.zeros_like(dk_ref)
            dv_ref[...] = jnp.zeros_like(dv_ref)

        @pl.when(jnp.logical_and(g == 0, j == 0))
        def _():
            dqg_ref[...] = jnp.zeros_like(dqg_ref)
            dkg_ref[...] = jnp.zeros_like(dkg_ref)

        rows_c = pl.ds(pl.multiple_of(j * ATT_BLOCK, ATT_BLOCK), ATT_BLOCK)
        rows_p = pl.ds(pl.multiple_of(jnp.maximum(j - 1, 0) * ATT_BLOCK, ATT_BLOCK), ATT_BLOCK)
        @pl.when(has_prev)
        def _():
            fn = functools.partial(_att_fn, mask_p=mask_p, mask_c=mask_c)
            _, vjp = jax.vjp(fn, q_ref[0, 0], kp_ref[0, 0], kc_ref[0, 0], vp_ref[0, 0], vc_ref[0, 0],
                             qg_ref[...], kg_ref[...])
            dq, dkp, dkc, dvp, dvc, dqg, dkg = vjp((do_ref[0], dlse_ref[0]))
            dq_ref[0] = dq
            dk_ref[0, rows_p, :] += dkp
            dv_ref[0, rows_p, :] += dvp
            dk_ref[0, rows_c, :] += dkc
            dv_ref[0, rows_c, :] += dvc
            dqg_ref[...] += dqg
            dkg_ref[...] += dkg

        @pl.when(jnp.logical_not(has_prev))
        def _():
            def fn(q, kc, vc, qg, kg):
                return _att_fn(q, None, kc, None, vc, qg, kg, None, mask_c)

            _, vjp = jax.vjp(fn, q_ref[0, 0], kc_ref[0, 0], vc_ref[0, 0], qg_ref[...], kg_ref[...])
            dq, dkc, dvc, dqg, dkg = vjp((do_ref[0], dlse_ref[0]))
            dq_ref[0] = dq
            dk_ref[0, rows_c, :] += dkc
            dv_ref[0, rows_c, :] += dvc
            dqg_ref[...] += dqg
            dkg_ref[...] += dkg

    sh = jax.ShapeDtypeStruct((ng, t, w), F32)
    gshape = jax.ShapeDtypeStruct((1, w), F32)
    return pl.pallas_call(
        body, name=name, grid=(ng, nbt), in_specs=in_specs + [out, out],
        out_specs=[out, whole, whole, gain, gain], out_shape=[sh, sh, sh, gshape, gshape],
        compiler_params=_params(("arbitrary", "arbitrary")),
    )(qkv, qkv, qkv, qkv, qkv, qg, kg, do, dlse)


def _combine_fn(o, lse):
    m = jnp.max(lse, axis=0, keepdims=True)
    e = jnp.exp(lse - m)
    w = e / jnp.sum(e, axis=0, keepdims=True)
    return jnp.sum(w * o, axis=0)


def att_combine(o, lse, name):
    ng, t, w = o.shape
    tr = _tile(t, 256, 16)
    spec = pl.BlockSpec((ng, tr, w), lambda i: (0, i, 0))
    y_spec = pl.BlockSpec((tr, w), lambda i: (i, 0))

    def body(o_ref, l_ref, y_ref):
        y_ref[...] = _combine_fn(o_ref[...], l_ref[...]).astype(y_ref.dtype)

    return pl.pallas_call(
        body, name=name, grid=(t // tr,), in_specs=[spec, spec], out_specs=y_spec,
        out_shape=jax.ShapeDtypeStruct((t, w), CD), compiler_params=_params(("parallel",)),
    )(o, lse)


def att_combine_bwd(o, lse, dy, name):
    ng, t, w = o.shape
    tr = _tile(t, 256, 8)
    spec = pl.BlockSpec((ng, tr, w), lambda i: (0, i, 0))
    y_spec = pl.BlockSpec((tr, w), lambda i: (i, 0))

    def body(o_ref, l_ref, dy_ref, do_ref, dl_ref):
        _, vjp = jax.vjp(_combine_fn, o_ref[...], l_ref[...])
        do, dl = vjp(dy_ref[...])
        do_ref[...] = do
        dl_ref[...] = dl

    sh = jax.ShapeDtypeStruct(o.shape, F32)
    return pl.pallas_call(
        body, name=name, grid=(t // tr,), in_specs=[spec, spec, y_spec], out_specs=[spec, spec],
        out_shape=[sh, sh], compiler_params=_params(("parallel",)),
    )(o, lse, dy)


def _shift_down(x, s):
    if s == 0:
        return x
    row = lax.broadcasted_iota(jnp.int32, x.shape, 0)
    return jnp.where(row >= s, pltpu.roll(x, s, 0), 0.0)


def _shift_up(x, s):
    if s == 0:
        return x
    t = x.shape[0]
    row = lax.broadcasted_iota(jnp.int32, x.shape, 0)
    return jnp.where(row < t - s, pltpu.roll(x, t - s, 0), 0.0)


def conv_fwd(z, col0, w, name):
    t = z.shape[0]
    c = w.shape[1]
    cb = _tile(c, 512, LANE)
    assert col0 % cb == 0
    nblk0 = col0 // cb

    def body(z_ref, w_ref, c_ref):
        for c0 in range(0, cb, LANE):
            cols = slice(c0, c0 + LANE)
            zv = z_ref[:, cols]
            acc = None
            for i in range(CONV_WIDTH):
                term = _shift_down(zv, CONV_WIDTH - 1 - i) * w_ref[i:i + 1, cols]
                acc = term if acc is None else acc + term
            c_ref[:, cols] = acc

    return pl.pallas_call(
        body, name=name, grid=(c // cb,),
        in_specs=[pl.BlockSpec((t, cb), lambda j: (0, nblk0 + j)), pl.BlockSpec((CONV_WIDTH, cb), lambda j: (0, j))],
        out_specs=pl.BlockSpec((t, cb), lambda j: (0, j)),
        out_shape=jax.ShapeDtypeStruct((t, c), F32), compiler_params=_params(("parallel",)),
    )(z, w)


def conv_bwd(dc, z, col0, w, name):
    t = z.shape[0]
    c = w.shape[1]
    cb = _tile(c, 512, LANE)
    assert col0 % cb == 0
    nblk0 = col0 // cb

    def body(dc_ref, z_ref, w_ref, dz_ref, dw_ref):
        for c0 in range(0, cb, LANE):
            cols = slice(c0, c0 + LANE)
            dcv, zv = dc_ref[:, cols], z_ref[:, cols]
            acc = None
            for i in range(CONV_WIDTH):
                s = CONV_WIDTH - 1 - i
                term = _shift_up(dcv, s) * w_ref[i:i + 1, cols]
                acc = term if acc is None else acc + term
                dw_ref[i:i + 1, cols] = jnp.sum(dcv * _shift_down(zv, s), axis=0, keepdims=True)
            dz_ref[:, cols] = acc.astype(dz_ref.dtype)

    blk = pl.BlockSpec((t, cb), lambda j: (0, j))
    wblk = pl.BlockSpec((CONV_WIDTH, cb), lambda j: (0, j))
    return pl.pallas_call(
        body, name=name, grid=(c // cb,),
        in_specs=[blk, pl.BlockSpec((t, cb), lambda j: (0, nblk0 + j)), wblk], out_specs=[blk, wblk],
        out_shape=[jax.ShapeDtypeStruct((t, c), CD), jax.ShapeDtypeStruct((CONV_WIDTH, c), F32)],
        compiler_params=_params(("parallel",)),
    )(dc, z, w)


def _dn_consts():
    c = DN_CHUNK
    row = lax.broadcasted_iota(jnp.int32, (c, c), 0)
    col = lax.broadcasted_iota(jnp.int32, (c, c), 1)
    return dict(tril=row >= col, strict=row > col, eye=(row == col).astype(F32),
                tril_f=(row >= col).astype(F32), triu_f=(row <= col).astype(F32))


def _softplus(x):
    return jnp.maximum(x, 0.0) + jnp.log(1.0 + jnp.exp(-jnp.abs(x)))


def _split(x):
    hi = x.astype(CD)
    return hi, (x - hi.astype(F32)).astype(CD)


def _bdot3(a, b, ca, cb):
    ah, al = _split(a)
    bh, bl = _split(b)
    return _bdot(ah, bh, ca, cb) + (_bdot(ah, bl, ca, cb) + _bdot(al, bh, ca, cb))


def _tri_inv_impl(a_mat):
    c = a_mat.shape[-1]
    eye = (lax.broadcasted_iota(jnp.int32, (c, c), 0) == lax.broadcasted_iota(jnp.int32, (c, c), 1)).astype(F32)
    nk_ = -a_mat
    t_inv = eye + nk_
    for _ in range(c.bit_length() - 2):
        nk_ = _bdot3(nk_, nk_, 2, 1)
        t_inv = t_inv + _bdot3(t_inv, nk_, 2, 1)
    return t_inv


@jax.custom_vjp
def _tri_inv_saved(a_mat, t_inv):
    return t_inv


def _tri_inv_saved_fwd(a_mat, t_inv):
    return t_inv, t_inv


def _tri_inv_saved_bwd(t_inv, dt_inv):
    return -_bdot3(_bdot3(t_inv, dt_inv, 1, 1), t_inv, 2, 2), jnp.zeros_like(t_inv)


_tri_inv_saved.defvjp(_tri_inv_saved_fwd, _tri_inv_saved_bwd)


def _dn_chunk(cq, ck, cv, og, a_col, b_col, al, dt, gn, s_prev, *, k, inv, with_inv=False):
    q = _silu(cq)
    q = q * lax.rsqrt(jnp.sum(q * q, axis=-1, keepdims=True) + EPS) * (DN_HEAD_DIM ** -0.5)
    kk = _silu(ck)
    kk = kk * lax.rsqrt(jnp.sum(kk * kk, axis=-1, keepdims=True) + EPS)
    v = _silu(cv)
    g = -jnp.exp(al) * _softplus(a_col + dt)
    beta = jax.nn.sigmoid(b_col)
    g_row = jnp.sum(k["eye"] * g, axis=1, keepdims=True)
    gc_col = jnp.sum(k["tril_f"] * g_row, axis=2, keepdims=True)
    gc_row = jnp.sum(k["triu_f"] * g, axis=1, keepdims=True)
    ldec = jnp.where(k["tril"], jnp.exp(jnp.where(k["tril"], gc_col - gc_row, 0.0)), 0.0)
    kb, vb = kk * beta, v * beta
    a_mat = jnp.where(k["strict"], _bnt(kb, kk) * ldec, 0.0)
    t_inv = inv(a_mat)
    egc = jnp.exp(gc_col)
    u = _bnn(t_inv, vb)
    w = _bnn(t_inv, kb * egc)
    attn = jnp.where(k["tril"], _bnt(q, kk) * ldec, 0.0)
    gc_last = jnp.sum(g, axis=1, keepdims=True)
    k_dec = kk * jnp.exp(gc_last - gc_col)
    v_new = u - _bnn(w, s_prev)
    o = _bnn(q * egc, s_prev) + _bnn(attn, v_new)
    s_new = s_prev * jnp.exp(gc_last) + _btn(k_dec, v_new)
    y = o * lax.rsqrt(jnp.mean(o * o, axis=-1, keepdims=True) + EPS) * gn * _silu(og)
    return (y, s_new, t_inv) if with_inv else (y, s_new)


def _dn_heads(ref, nh):
    hd = DN_HEAD_DIM
    return jnp.stack([ref[:, h * hd:(h + 1) * hd] for h in range(nh)])


def _dn_specs(nh, col_gate, col_ab, order):
    hd, c = DN_HEAD_DIM, DN_CHUNK
    w = nh * hd
    qs = pl.BlockSpec((c, w), lambda n: (order(n), 0))
    ks = pl.BlockSpec((c, w), lambda n: (order(n), 1))
    vs = pl.BlockSpec((c, w), lambda n: (order(n), 2))
    gs = pl.BlockSpec((c, w), lambda n: (order(n), col_gate // w))
    ab = pl.BlockSpec((c, LANE), lambda n: (order(n), col_ab // LANE))
    scal = pl.BlockSpec((nh, 1, 1), lambda n: (0, 0, 0))
    gn = pl.BlockSpec((1, hd), lambda n: (0, 0))
    st = pl.BlockSpec((1, nh, hd, hd), lambda n: (order(n), 0, 0, 0))
    return qs, ks, vs, gs, ab, scal, gn, st


def _lane_pick(x, idx):
    lane = lax.broadcasted_iota(jnp.int32, x.shape, 1)
    return jnp.sum(jnp.where(lane == idx, x, 0.0), axis=1, keepdims=True)


def dn_fwd(cv, z, col_gate, col_ab, a_log, dt_bias, gn, name):
    t = cv.shape[0]
    nh = a_log.shape[0]
    hd, c = DN_HEAD_DIM, DN_CHUNK
    n_chunks = t // c
    qs, ks, vs, gs, ab, scal, gnspec, st = _dn_specs(nh, col_gate, col_ab, lambda n: n)

    def body(q_ref, k_ref, v_ref, g_ref, ab_ref, al_ref, dt_ref, gn_ref, y_ref, st_ref, ti_ref, s_scr):
        @pl.when(pl.program_id(0) == 0)
        def _():
            s_scr[...] = jnp.zeros_like(s_scr)

        abv = ab_ref[...]
        a_col = jnp.stack([_lane_pick(abv, h) for h in range(nh)])
        b_col = jnp.stack([_lane_pick(abv, nh + h) for h in range(nh)])
        s_prev = s_scr[...]
        st_ref[0] = s_prev
        y, s_new, t_inv = _dn_chunk(
            _dn_heads(q_ref, nh), _dn_heads(k_ref, nh), _dn_heads(v_ref, nh), _dn_heads(g_ref, nh),
            a_col, b_col, al_ref[...], dt_ref[...], gn_ref[...], s_prev,
            k=_dn_consts(), inv=_tri_inv_impl, with_inv=True)
        for h in range(nh):
            y_ref[:, h * hd:(h + 1) * hd] = y[h].astype(y_ref.dtype)
        ti_ref[0] = t_inv
        s_scr[...] = s_new

    return pl.pallas_call(
        body, name=name, grid=(n_chunks,),
        in_specs=[qs, ks, vs, gs, ab, scal, scal, gnspec],
        out_specs=[pl.BlockSpec((c, nh * hd), lambda n: (n, 0)), st,
                   pl.BlockSpec((1, nh, c, c), lambda n: (n, 0, 0, 0))],
        out_shape=[jax.ShapeDtypeStruct((t, nh * hd), CD), jax.ShapeDtypeStruct((n_chunks, nh, hd, hd), F32),
                   jax.ShapeDtypeStruct((n_chunks, nh, c, c), F32)],
        scratch_shapes=[pltpu.VMEM((nh, hd, hd), F32)],
        compiler_params=_params(("arbitrary",)),
    )(cv, cv, cv, z, z, a_log, dt_bias, gn)


def dn_bwd(cv, z, col_gate, col_ab, a_log, dt_bias, gn, states, t_invs, dy, name):
    t = cv.shape[0]
    nh = a_log.shape[0]
    hd, c = DN_HEAD_DIM, DN_CHUNK
    w = nh * hd
    n_chunks = t // c
    rev = lambda n: n_chunks - 1 - n
    qs, ks, vs, gs, ab, scal, gnspec, st = _dn_specs(nh, col_gate, col_ab, rev)
    yspec = pl.BlockSpec((c, w), lambda n: (rev(n), 0))

    def body(q_ref, k_ref, v_ref, g_ref, ab_ref, al_ref, dt_ref, gn_ref, st_ref, ti_ref, dy_ref,
             dc_ref, dg_ref, dab_ref, dal_ref, ddt_ref, dgn_ref, ds_scr):
        @pl.when(pl.program_id(0) == 0)
        def _():
            ds_scr[...] = jnp.zeros_like(ds_scr)
            dal_ref[...] = jnp.zeros_like(dal_ref)
            ddt_ref[...] = jnp.zeros_like(ddt_ref)
            dgn_ref[...] = jnp.zeros_like(dgn_ref)

        abv = ab_ref[...]
        a_col = jnp.stack([_lane_pick(abv, h) for h in range(nh)])
        b_col = jnp.stack([_lane_pick(abv, nh + h) for h in range(nh)])
        t_inv = ti_ref[0]
        fn = functools.partial(_dn_chunk, k=_dn_consts(), inv=lambda a_mat: _tri_inv_saved(a_mat, t_inv))
        _, vjp = jax.vjp(fn, _dn_heads(q_ref, nh), _dn_heads(k_ref, nh), _dn_heads(v_ref, nh), _dn_heads(g_ref, nh),
                         a_col, b_col, al_ref[...], dt_ref[...], gn_ref[...], st_ref[0])
        dq, dk, dv, dg, da, db, dal, ddt, dgn, ds = vjp((_dn_heads(dy_ref, nh), ds_scr[...]))
        lane = lax.broadcasted_iota(jnp.int32, (c, LANE), 1)
        dab = jnp.zeros((c, LANE), F32)
        for h in range(nh):
            cols = slice(h * hd, (h + 1) * hd)
            dc_ref[:, cols] = dq[h]
            dc_ref[:, w + h * hd:w + (h + 1) * hd] = dk[h]
            dc_ref[:, 2 * w + h * hd:2 * w + (h + 1) * hd] = dv[h]
            dg_ref[:, cols] = dg[h].astype(dg_ref.dtype)
            dab = dab + jnp.where(lane == h, da[h], 0.0) + jnp.where(lane == nh + h, db[h], 0.0)
        dab_ref[...] = dab.astype(dab_ref.dtype)
        dal_ref[...] += dal
        ddt_ref[...] += ddt
        dgn_ref[...] += dgn
        ds_scr[...] = ds

    sshape = jax.ShapeDtypeStruct((nh, 1, 1), F32)
    res = pl.pallas_call(
        body, name=name, grid=(n_chunks,),
        in_specs=[qs, ks, vs, gs, ab, scal, scal, gnspec, st,
                  pl.BlockSpec((1, nh, c, c), lambda n: (rev(n), 0, 0, 0)), yspec],
        out_specs=[pl.BlockSpec((c, 3 * w), lambda n: (rev(n), 0)), yspec,
                   pl.BlockSpec((c, LANE), lambda n: (rev(n), 0)), scal, scal, gnspec],
        out_shape=[jax.ShapeDtypeStruct((t, 3 * w), F32), jax.ShapeDtypeStruct((t, w), CD),
                   jax.ShapeDtypeStruct((t, LANE), CD), sshape, sshape, jax.ShapeDtypeStruct((1, hd), F32)],
        scratch_shapes=[pltpu.VMEM((nh, hd, hd), F32)],
        compiler_params=_params(("arbitrary",)),
    )(cv, cv, cv, z, z, a_log, dt_bias, gn, states, t_invs, dy)
    return res


def merge_fwd(ya, wpa, yd, wpd, z, col_m, name):
    t, d = yd.shape[0], wpd.shape[1]
    tm, tn = _tile(t, 512, LANE), _tile(d, 256, LANE)
    nb1, nb2 = col_m // tn, (col_m + d) // tn

    def body(ya_ref, wpa_ref, yd_ref, wpd_ref, z1_ref, z2_ref, m_ref, pa_ref, pd_ref):
        pa = _nn(ya_ref[...], wpa_ref[...])
        pd = _nn(yd_ref[...], wpd_ref[...])
        pa_ref[...] = pa
        pd_ref[...] = pd
        m_ref[...] = (jax.nn.sigmoid(z1_ref[...]) * pa + jax.nn.sigmoid(z2_ref[...]) * pd).astype(m_ref.dtype)

    o = pl.BlockSpec((tm, tn), lambda i, j: (i, j))
    return pl.pallas_call(
        body, name=name, grid=(t // tm, d // tn),
        in_specs=[pl.BlockSpec((tm, ya.shape[1]), lambda i, j: (i, 0)),
                  pl.BlockSpec((wpa.shape[0], tn), lambda i, j: (0, j)),
                  pl.BlockSpec((tm, yd.shape[1]), lambda i, j: (i, 0)),
                  pl.BlockSpec((wpd.shape[0], tn), lambda i, j: (0, j)),
                  pl.BlockSpec((tm, tn), lambda i, j: (i, nb1 + j)),
                  pl.BlockSpec((tm, tn), lambda i, j: (i, nb2 + j))],
        out_specs=[o, o, o],
        out_shape=[jax.ShapeDtypeStruct((t, d), CD), jax.ShapeDtypeStruct((t, d), F32),
                   jax.ShapeDtypeStruct((t, d), F32)],
        compiler_params=_params(("parallel", "parallel")),
    )(ya, wpa, yd, wpd, z, z)


def merge_bwd(dm, pa, pd, z, col_m, name):
    t, d = dm.shape
    tm, tn = _tile(t, 512, 8), _tile(d, 256, LANE)
    nb1, nb2 = col_m // tn, (col_m + d) // tn

    def body(dm_ref, pa_ref, pd_ref, z1_ref, z2_ref, dpa_ref, dpd_ref, dz1_ref, dz2_ref):
        dmv = dm_ref[...]
        s1, s2 = jax.nn.sigmoid(z1_ref[...]), jax.nn.sigmoid(z2_ref[...])
        dpa_ref[...] = (dmv * s1).astype(dpa_ref.dtype)
        dpd_ref[...] = (dmv * s2).astype(dpd_ref.dtype)
        dz1_ref[...] = (dmv * pa_ref[...] * s1 * (1.0 - s1)).astype(dz1_ref.dtype)
        dz2_ref[...] = (dmv * pd_ref[...] * s2 * (1.0 - s2)).astype(dz2_ref.dtype)

    o = pl.BlockSpec((tm, tn), lambda i, j: (i, j))
    sh = jax.ShapeDtypeStruct((t, d), CD)
    return pl.pallas_call(
        body, name=name, grid=(t // tm, d // tn),
        in_specs=[o, o, o, pl.BlockSpec((tm, tn), lambda i, j: (i, nb1 + j)),
                  pl.BlockSpec((tm, tn), lambda i, j: (i, nb2 + j))],
        out_specs=[o, o, o, o], out_shape=[sh, sh, sh, sh],
        compiler_params=_params(("parallel", "parallel")),
    )(dm, pa, pd, z, z)


def ada_fwd(c_all, w, b, name):
    nl, d, n = w.shape
    tn = _tile(n, 384, LANE)

    def body(c_ref, w_ref, b_ref, o_ref):
        o_ref[0] = _nn(_silu(c_ref[...]), w_ref[0]) + b_ref[0]

    return pl.pallas_call(
        body, name=name, grid=(nl, n // tn),
        in_specs=[pl.BlockSpec(c_all.shape, lambda l, j: (0, 0)), pl.BlockSpec((1, d, tn), lambda l, j: (l, 0, j)),
                  pl.BlockSpec((1, 1, tn), lambda l, j: (l, 0, j))],
        out_specs=pl.BlockSpec((1, c_all.shape[0], tn), lambda l, j: (l, 0, j)),
        out_shape=jax.ShapeDtypeStruct((nl, c_all.shape[0], n), F32),
        compiler_params=_params(("parallel", "parallel")),
    )(c_all, w, b)


def ada_bwd(c_pad, dmod_pad, name):
    nl, kp, n = dmod_pad.shape
    d = c_pad.shape[1]
    tn = _tile(n, 384, LANE)

    def body(c_ref, g_ref, o_ref):
        o_ref[0] = _tn(_silu(c_ref[...]), g_ref[0])

    return pl.pallas_call(
        body, name=name, grid=(nl, n // tn),
        in_specs=[pl.BlockSpec((kp, d), lambda l, j: (0, 0)), pl.BlockSpec((1, kp, tn), lambda l, j: (l, 0, j))],
        out_specs=pl.BlockSpec((1, d, tn), lambda l, j: (l, 0, j)),
        out_shape=jax.ShapeDtypeStruct((nl, d, n), F32),
        compiler_params=_params(("parallel", "parallel")),
    )(c_pad, dmod_pad)


def loss_head(y, target, name):
    t, d = y.shape
    tr = _tile(t, 256, 8)

    def body(y_ref, t_ref, dy_ref, l_ref):
        err = y_ref[...] - t_ref[...]
        dy_ref[...] = err * (1.0 / d)
        part = jnp.sum(jnp.sum(err * err, axis=1, keepdims=True), axis=0, keepdims=True) * (0.5 / d)

        @pl.when(pl.program_id(0) == 0)
        def _():
            l_ref[...] = jnp.zeros_like(l_ref)

        l_ref[...] += part

    row = pl.BlockSpec((tr, d), lambda i: (i, 0))
    return pl.pallas_call(
        body, name=name, grid=(t // tr,), in_specs=[row, row],
        out_specs=[row, pl.BlockSpec((8, LANE), lambda i: (0, 0))],
        out_shape=[jax.ShapeDtypeStruct((t, d), F32), jax.ShapeDtypeStruct((8, LANE), F32)],
        compiler_params=_params(("arbitrary",)),
    )(y, target)


def _adamw_update(g, w_ref, m_ref, v_ref, go_ref, d_ref, mo_ref, vo_ref):
    m_new = ADAM_B1 * m_ref[...] + (1.0 - ADAM_B1) * g
    v_new = ADAM_B2 * v_ref[...] + (1.0 - ADAM_B2) * jnp.square(g)
    m_hat = m_new / (1.0 - ADAM_B1 ** ADAM_STEP)
    v_hat = v_new / (1.0 - ADAM_B2 ** ADAM_STEP)
    go_ref[...] = g
    d_ref[...] = -ADAM_LR * (m_hat / (jnp.sqrt(v_hat) + ADAM_EPS) + ADAM_WD * w_ref[...])
    mo_ref[...] = m_new
    vo_ref[...] = v_new


def adamw(w, m, v, g_slots, name):
    shape = w.shape
    nslot = g_slots.shape[0]
    if w.ndim == 2:
        w3, m3, v3, g4 = w[None], m[None], v[None], g_slots[:, None]
    else:
        w3, m3, v3, g4 = w, m, v, g_slots
    nl, r, c = w3.shape
    tr = _tile(r, 256, 16)

    def body(w_ref, m_ref, v_ref, g_ref, *outs):
        g = g_ref[0].astype(F32)
        for s in range(1, nslot):
            g = g + g_ref[s].astype(F32)
        _adamw_update(g, w_ref, m_ref, v_ref, *outs)

    blk = pl.BlockSpec((1, tr, c), lambda l, i: (l, i, 0))
    gblk = pl.BlockSpec((nslot, 1, tr, c), lambda l, i: (0, l, i, 0))
    sh = jax.ShapeDtypeStruct(w3.shape, F32)
    outs = pl.pallas_call(
        body, name=name, grid=(nl, r // tr), in_specs=[blk, blk, blk, gblk], out_specs=[blk] * 4,
        out_shape=[sh] * 4, compiler_params=_params(("parallel", "parallel")),
    )(w3, m3, v3, g4)
    return tuple(o.reshape(shape) for o in outs)


def adamw_layers(w, m, v, g_layers, first, name, prev=None):
    _, r, c = w.shape
    n = len(g_layers)
    nslot, _, cg = g_layers[0].shape
    tr = _tile(r, 256, 16)

    def body(w_ref, m_ref, v_ref, *rest):
        g_refs, outs = rest[:n], rest[-4:]
        for li in range(n):
            @pl.when(pl.program_id(0) == li)
            def _(g_ref=g_refs[li]):
                g = g_ref[0, :, :c].astype(F32)
                for s in range(1, nslot):
                    g = g + g_ref[s, :, :c].astype(F32)
                _adamw_update(g[None], w_ref, m_ref, v_ref, *outs)

    blk = pl.BlockSpec((1, tr, c), lambda l, i: (first + l, i, 0))
    gblks = [pl.BlockSpec((nslot, tr, cg), lambda l, i, li=li: (0, jnp.where(l == li, i, 0), 0)) for li in range(n)]
    sh = jax.ShapeDtypeStruct(w.shape, F32)
    extra = [] if prev is None else list(prev)
    return tuple(pl.pallas_call(
        body, name=name, grid=(n, r // tr),
        in_specs=[blk, blk, blk] + gblks + [pl.BlockSpec(memory_space=pl.ANY)] * len(extra),
        out_specs=[blk] * 4, out_shape=[sh] * 4,
        input_output_aliases={3 + n + k: k for k in range(len(extra))},
        compiler_params=_params(("arbitrary", "arbitrary")),
    )(w, m, v, *g_layers, *extra))


def _rows(flat, unit=16):
    n = flat.shape[0]
    per = 1024 * unit
    pad = (-n) % per
    if pad:
        flat = jnp.concatenate([flat, jnp.zeros((pad,), flat.dtype)])
    return flat.reshape(-1, 1024)


def _class_rows(x, dil):
    if dil == 1:
        return x
    t, w = x.shape
    return x.reshape(t // dil, dil, w).transpose(1, 0, 2).reshape(t, w)


def _token_rows(x, dil):
    if dil == 1:
        return x
    t, w = x.shape
    return x.reshape(dil, t // dil, w).transpose(1, 0, 2).reshape(t, w)


def _to_classes(a):
    return jnp.stack([_class_rows(a[gi], dil) for gi, dil in enumerate(DILATIONS)])


def _from_classes(a):
    return jnp.stack([_token_rows(a[gi], dil) for gi, dil in enumerate(DILATIONS)])


def _unshard_cols(g):
    return g.transpose(1, 0, 2).reshape(g.shape[1], -1)


def _shard_cols(full):
    r = full.shape[0]
    return full.reshape(r, N_DEV, -1).transpose(1, 0, 2)


def kernel(x, c, ada_w, ada_b, norm_ff1, ffn1_w_up, ffn1_w_down, norm_mix, w_in, q_norm, k_norm, conv_w, a_log, dt_bias, dn_norm, w_proj_att, w_proj_dn, w_out, norm_ff2, ffn2_w_up, ffn2_w_down, loss_target, m_ada_w, m_ada_b, m_norm_ff1, m_ffn1_w_up, m_ffn1_w_down, m_norm_mix, m_w_in, m_q_norm, m_k_norm, m_conv_w, m_a_log, m_dt_bias, m_dn_norm, m_w_proj_att, m_w_proj_dn, m_w_out, m_norm_ff2, m_ffn2_w_up, m_ffn2_w_down, v_ada_w, v_ada_b, v_norm_ff1, v_ffn1_w_up, v_ffn1_w_down, v_norm_mix, v_w_in, v_q_norm, v_k_norm, v_conv_w, v_a_log, v_dt_bias, v_dn_norm, v_w_proj_att, v_w_proj_dn, v_w_out, v_norm_ff2, v_ffn2_w_up, v_ffn2_w_down):
    nl = ada_w.shape[0]
    t, d = x.shape[1], x.shape[2]
    dff = ffn1_w_down.shape[1] * N_DEV
    ha = d // 256
    ng = len(DILATIONS)
    wa = ha * ATT_HEAD_DIM
    att_w = ng * wa
    nh = d // DN_HEAD_DIM
    dn_w = nh * DN_HEAD_DIM
    n_in = w_in.shape[2] * N_DEV
    off_dn, off_gate = 3 * att_w, 3 * att_w + 3 * dn_w
    off_a = off_gate + dn_w
    off_merge = off_a + 2 * nh
    assert off_merge + 2 * d == n_in
    col_dn, col_gate, col_m = 0, 3 * dn_w, 4 * dn_w
    col_att = col_m + 2 * d
    col_ab = col_att + 3 * att_w
    zw = col_ab + 2 * LANE
    me = 4 * lax.axis_index("x") + 2 * lax.axis_index("y") + lax.axis_index("c")
    xs = x[0]
    target = loss_target[0]

    conv_rows = _rows(conv_w.reshape(-1), 8)
    pack0 = jnp.concatenate([jnp.concatenate([c, jnp.zeros((7, d), F32)]).reshape(-1), conv_rows.reshape(-1)])
    pack0 = _rows(pack0, 8)
    g0 = all_gather([pack0], "ag_c_conv")[0].reshape(N_DEV, -1)
    c_all = g0[:, :d]
    cw = g0[:, 8 * d:8 * d + conv_w.size].reshape(N_DEV, nl, CONV_WIDTH, -1)
    conv_full = cw.transpose(1, 2, 0, 3).reshape(nl, CONV_WIDTH, 3 * dn_w)

    n_ada = ada_w.shape[2]
    b_mine = lax.dynamic_slice_in_dim(ada_b, me * n_ada, n_ada, axis=1)[:, None, :]
    mod_s = ada_fwd(c_all, ada_w, b_mine, "ada_fwd")
    gm = all_gather([mod_s], "ag_mod")[0]
    mod = lax.dynamic_index_in_dim(gm, me, axis=2, keepdims=False)
    mod = mod.transpose(1, 0, 2).reshape(nl, N_ADA, 1, d)

    kinds = [ffn1_w_up, ffn1_w_down, w_in, w_proj_att, w_proj_dn, w_out, ffn2_w_up, ffn2_w_down]
    c_up = ffn1_w_up.shape[2]
    cp = -(-c_up // LANE) * LANE
    r_dn = ffn1_w_down.shape[1]
    assert 2 * r_dn == c_up

    up1_t, up2_t = jnp.swapaxes(ffn1_w_up, 1, 2), jnp.swapaxes(ffn2_w_up, 1, 2)
    w_in_t = jnp.transpose(w_in, (2, 0, 1))

    def layer_shards(l):
        def pad_up(wt):
            return jnp.pad(wt.astype(CD), ((0, cp - c_up), (0, 0)))
        return [pad_up(up1_t[l]), ffn1_w_down[l].astype(CD), w_in_t[:, l].astype(CD), w_proj_att[l].astype(CD),
                w_proj_dn[l].astype(CD), w_out[l].astype(CD), pad_up(up2_t[l]), ffn2_w_down[l].astype(CD)]

    def gather_landing(shards):
        blank = unwritten([(N_DEV,) + s.shape for s in shards], shards, "landing_ag")
        return [lax.dynamic_update_index_in_dim(b, s, me, 0) for b, s in zip(blank, shards)]

    def slot_landing(arrs):
        blank = unwritten([a.shape for a in arrs], arrs, "landing_a2a")
        return [lax.dynamic_update_index_in_dim(b, lax.dynamic_index_in_dim(a, me, 0, keepdims=False), me, 0)
                for b, a in zip(blank, arrs)]

    gathered = all_gather(layer_shards(0), "ag_weights0")
    prefetch = None

    def full_weight(ki, l):
        blk = gathered[ki]
        if ki == 3:
            return _unshard_cols(blk)
        if ki == 2:
            return w_in_rows(blk.reshape(n_in, d))
        if ki in (1, 7):
            pairs = blk.reshape(HALF, c_up, d)
            return jnp.pad(pairs, ((0, 0), (0, cp - c_up), (0, 0))).reshape(HALF * cp, d)
        return blk.reshape(-1, blk.shape[2])

    def down_grad_slots(g):
        return g.reshape(HALF, cp, d)[:, :c_up].reshape(N_DEV, r_dn, d)

    def w_in_rows(wt):
        pad = jnp.zeros((zw - n_in, wt.shape[1]), wt.dtype)
        return jnp.concatenate([wt[off_dn:off_a], wt[off_merge:], wt[:off_dn], wt[off_a:off_merge], pad], axis=0)

    def w_in_cols_inv(g):
        return jnp.concatenate([g[:, col_att:col_ab], g[:, :col_m], g[:, col_ab:col_ab + 2 * nh], g[:, col_m:col_att]], axis=1)

    saved = []
    xc = xs
    mods = []
    for l in range(nl):
        sv = {}
        if prefetch is not None:
            gathered = exchange_wait(*prefetch[:4], xc, True, f"ag_wait{l}")
        mod_l = mod[l]
        if l + 1 < nl:
            shards = layer_shards(l + 1)
            behind = gathered[0][0, :1, :1].astype(F32) + mod[0, 0, :, :1]
            prefetch = exchange_start(shards, gather_landing(shards), behind, True, f"ag_start{l + 1}")
            mod_l = mod_l + prefetch[4][0, 0]
        mods.append(mod_l)
        sh1, sc1, gt1, sh2, sc2, gt2, sh3, sc3, gt3 = [mod_l[i] for i in range(N_ADA)]
        w_dn1, w_dn2 = full_weight(1, l), full_weight(7, l)
        win = full_weight(2, l)
        wpa, wpd, wo = full_weight(3, l), full_weight(4, l), full_weight(5, l)
        sv["w"] = (gathered[0], gathered[6], w_dn1, w_dn2, win, wpa, wpd, wo)

        def ffn(xin, g, sh, sc, gt, u_all, w_dn):
            h = norm_mod(xin, g, sc, sh, "norm_mod")
            gate, up, a = ffn_up(h, u_all, "ffn_up")
            xo, f = matmul([(a, w_dn)], "nn", F32, "ffn_down", out_scale=0.5 * gt, resid=xin, save_acc=True,
                           tm=1024, tk=4096)
            return xo, (xin, h, gate, up, a, f)

        xc, sv["ffn1"] = ffn(xc, norm_ff1[l:l + 1], sh1, sc1, gt1, gathered[0], w_dn1)

        x_mix = xc
        h2 = norm_mod(x_mix, norm_mix[l:l + 1], sc2, sh2, "norm_mod")
        z = matmul([(h2, win)], "nt", F32, "w_in", tm=1024, tn=512)
        z_att = z[:, col_att:col_ab].reshape(t, 3, ng, wa)
        qkv = jnp.stack([jnp.stack([_class_rows(z_att[:, i, gi], dil) for gi, dil in enumerate(DILATIONS)])
                         for i in range(3)])
        qg, kg = jnp.tile(q_norm[l:l + 1], (1, ha)), jnp.tile(k_norm[l:l + 1], (1, ha))
        o_cls, lse_cls = att_fwd(qkv, qg, kg, "att_fwd")
        o_tok, lse_tok = _from_classes(o_cls), _from_classes(lse_cls)
        ya = att_combine(o_tok, lse_tok, "att_combine")
        cvo = conv_fwd(z, col_dn, conv_full[l], "conv_fwd")
        al3, dt3 = a_log[l].reshape(nh, 1, 1), dt_bias[l].reshape(nh, 1, 1)
        gn = dn_norm[l:l + 1]
        yd, *states = dn_fwd(cvo, z, col_gate, col_ab, al3, dt3, gn, "dn_fwd")
        mrg, pa, pd = merge_fwd(ya, wpa, yd, wpd, z, col_m, "merge_fwd")
        xc, f2 = matmul([(mrg, wo)], "nn", F32, "w_out", out_scale=gt2, resid=x_mix, save_acc=True)
        sv["mix"] = (x_mix, h2, z, qkv, o_tok, lse_tok, ya, cvo, yd, states, mrg, pa, pd, f2)

        xc, sv["ffn2"] = ffn(xc, norm_ff2[l:l + 1], sh3, sc3, gt3, gathered[6], w_dn2)
        saved.append(sv)

    dxc, loss_blk = loss_head(xc, target, "loss_head")

    def slots(ki, g):
        if ki in (0, 1, 6, 7):
            return g
        return _shard_cols(g) if ki in (2, 3) else g.reshape(N_DEV, -1, g.shape[1])

    gbig = [None] * len(kinds)
    dmods, small, pending = [], [], []
    token = None
    for l in reversed(range(nl)):
        sv = saved[l]
        mod_l = mods[l] if token is None else mods[l] + token[0, 0]
        sh1, sc1, gt1, sh2, sc2, gt2, sh3, sc3, gt3 = [mod_l[i] for i in range(N_ADA)]
        u_all1, u_all2, w_dn1, w_dn2, win, wpa, wpd, wo = sv["w"]

        def ffn_bwd(dxo, g, sh, sc, gt, u_all, w_dn, sv_f):
            xin, h, gate, up, a, f = sv_f
            s = 0.5 * gt
            g_dn = down_grad_slots(matmul([(a, dxo)], "tn", CD, "ffn_down_wg", out_scale=s, tm=1024, tn=1024))
            dgate, dup = ffn_dact(dxo, s, w_dn, gate, up, "ffn_dact")
            g_up = ffn_up_wg(h, dgate, dup, cp, "ffn_up_wg")
            dh = ffn_up_dg(dgate, dup, u_all, "ffn_up_dg")
            dx, dg, dsc, dsh, dgt = norm_mod_bwd(xin, g, sc, sh, dh, dxo, f, 0.5, "norm_mod_bwd")
            return dx, g_up, g_dn, (dg, dsc, dsh, dgt)

        def start_exchange(ids, tag, behind):
            send = [slots(ki, gbig[ki]) for ki in ids]
            started = exchange_start(send, slot_landing(send), behind, False, f"a2a_start{l}{tag}")
            pending.append((l, ids, tag, started))
            return started[4]

        dxc, gbig[6], gbig[7], (dg3, dsc3, dsh3, dgt3) = ffn_bwd(
            dxc, norm_ff2[l:l + 1], sh3, sc3, gt3, u_all2, w_dn2, sv["ffn2"])
        token = start_exchange([6, 7], "f2", dxc)
        sh2, sc2, gt2 = (vec + token[0, 0] for vec in (sh2, sc2, gt2))

        x_mix, h2, z, qkv, o_tok, lse_tok, ya, cvo, yd, states, mrg, pa, pd, f2 = sv["mix"]
        gbig[5] = matmul([(mrg, dxc)], "tn", CD, "w_out_wg", out_scale=gt2)
        dm = matmul([(dxc, wo)], "nt", F32, "w_out_dg", a_scale=gt2)
        dpa, dpd, dz1, dz2 = merge_bwd(dm, pa, pd, z, col_m, "merge_bwd")
        gbig[3] = matmul([(ya, dpa)], "tn", CD, "patt_wg")
        gbig[4] = matmul([(yd, dpd)], "tn", CD, "pdn_wg")
        dya = matmul([(dpa, wpa)], "nt", F32, "patt_dg")
        dyd = matmul([(dpd, wpd)], "nt", F32, "pdn_dg")
        do_tok, dlse_tok = att_combine_bwd(o_tok, lse_tok, dya, "att_combine_bwd")
        qg, kg = jnp.tile(q_norm[l:l + 1], (1, ha)), jnp.tile(k_norm[l:l + 1], (1, ha))
        dq, dk, dv, dqg, dkg = att_bwd(qkv, qg, kg, _to_classes(do_tok), _to_classes(dlse_tok), "att_bwd")
        dqg, dkg = (jnp.sum(v_.reshape(ha, ATT_HEAD_DIM), axis=0, keepdims=True) for v_ in (dqg, dkg))
        dz_att = jnp.concatenate([_token_rows(a_[gi], dil).astype(CD) for a_ in (dq, dk, dv)
                                  for gi, dil in enumerate(DILATIONS)], axis=1)
        al3, dt3 = a_log[l].reshape(nh, 1, 1), dt_bias[l].reshape(nh, 1, 1)
        gn = dn_norm[l:l + 1]
        dcvo, dz_gate, dz_ab, dal, ddt, dgn = dn_bwd(
            cvo, z, col_gate, col_ab, al3, dt3, gn, *states, dyd, "dn_bwd")
        dz_dn, dconv = conv_bwd(dcvo, z, col_dn, conv_full[l], "conv_bwd")
        dz = jnp.concatenate([dz_dn, dz_gate, dz1, dz2, dz_att, dz_ab, jnp.zeros((t, LANE), CD)], axis=1)
        gbig[2] = w_in_cols_inv(matmul([(h2, dz)], "tn", CD, "w_in_wg", tm=1024, tn=zw // 4))
        dh2 = matmul([(dz, win)], "nn", F32, "w_in_dg", tm=1024, tn=1024,
                     tk=zw // 4 if (zw // 4) % LANE == 0 else zw)
        dxc, dg2, dsc2, dsh2, dgt2 = norm_mod_bwd(x_mix, norm_mix[l:l + 1], sc2, sh2, dh2, dxc, f2, 1.0, "norm_mod_bwd_mix")

        token = start_exchange([2, 3, 4, 5], "mx", dxc)
        sh1, sc1, gt1 = (vec + token[0, 0] for vec in (sh1, sc1, gt1))
        dxc, gbig[0], gbig[1], (dg1, dsc1, dsh1, dgt1) = ffn_bwd(
            dxc, norm_ff1[l:l + 1], sh1, sc1, gt1, u_all1, w_dn1, sv["ffn1"])
        if l > 0:
            token = start_exchange([0, 1], "f1", dxc)
        dmods.append(jnp.concatenate([dsh1, dsc1, dgt1, dsh2, dsc2, dgt2, dsh3, dsc3, dgt3], axis=1))
        small.append((dg1, dg2, dg3, dconv, dqg, dkg, dal.reshape(1, nh), ddt.reshape(1, nh), dgn))
    dmods.reverse()
    small.reverse()

    big_names = ["ffn1_w_up", "ffn1_w_down", "w_in", "w_proj_att", "w_proj_dn", "w_out", "ffn2_w_up", "ffn2_w_down"]
    big_m = [m_ffn1_w_up, m_ffn1_w_down, m_w_in, m_w_proj_att, m_w_proj_dn, m_w_out, m_ffn2_w_up, m_ffn2_w_down]
    big_v = [v_ffn1_w_up, v_ffn1_w_down, v_w_in, v_w_proj_att, v_w_proj_dn, v_w_out, v_ffn2_w_up, v_ffn2_w_down]
    recv_layers = [[None] * len(kinds) for _ in range(nl)]

    def wait_exchanges(layers, behind):
        for pl_, ids, tag, st in pending:
            if pl_ in layers:
                got = exchange_wait(*st[:4], behind, False, f"a2a_wait{pl_}{tag}")
                for ki, arr in zip(ids, got):
                    recv_layers[pl_][ki] = arr

    def big_adamw(ki, layers, prev, tag):
        g_layers = [recv_layers[li][ki] for li in layers]
        wmv = (kinds[ki], big_m[ki], big_v[ki])
        if ki in (0, 6):
            wmv = tuple(jnp.swapaxes(a, 1, 2) for a in wmv)
        return adamw_layers(*wmv, g_layers, layers[0], f"adamw_{big_names[ki]}{tag}", prev)

    later = list(range(1, nl))
    partial = [None] * len(kinds)
    if later:
        wait_exchanges(later, dxc)
        partial = [big_adamw(ki, later, None, "_l1") for ki in range(len(kinds))]

    fields = [jnp.stack(dmods).reshape(-1)]
    fields += [jnp.stack([s[i] for s in small]).reshape(-1) for i in range(9)]
    fields.append(loss_blk[0, :1])
    if later:
        fields.append(partial[-1][1][1, :1, 0] * 0.0)
    fsizes = [f.size for f in fields]
    foffs = [sum(fsizes[:i]) for i in range(len(fields))]
    g1 = all_gather([_rows(jnp.concatenate(fields), 8)], "ag_small")[0].reshape(N_DEV, -1)
    l = 0
    start_exchange([0, 1], "f1", g1)

    def field(i, shape):
        return g1[:, foffs[i]:foffs[i] + fsizes[i]].reshape(N_DEV, *shape)

    loss = field(10, (1,))[0, 0]
    for j in range(1, N_DEV):
        loss = loss + field(10, (1,))[j, 0]

    results = {}
    dmod_all = field(0, (nl, N_ADA * d))
    c_pad = jnp.concatenate([c_all, jnp.zeros((8, d), F32)])
    dmod_mine = lax.dynamic_slice_in_dim(dmod_all, me * n_ada, n_ada, axis=2).transpose(1, 0, 2)
    dmod_pad = jnp.concatenate([dmod_mine, jnp.zeros((nl, 8, n_ada), F32)], axis=1)
    g_ada_w = ada_bwd(c_pad, dmod_pad, "ada_bwd")
    results["ada_w"] = adamw(ada_w, m_ada_w, v_ada_w, g_ada_w[None], "adamw_ada_w")
    results["ada_b"] = adamw(ada_b, m_ada_b, v_ada_b, dmod_all, "adamw_ada_b")
    results["norm_ff1"] = adamw(norm_ff1, m_norm_ff1, v_norm_ff1, field(1, (nl, d)), "adamw_norm_ff1")
    results["norm_mix"] = adamw(norm_mix, m_norm_mix, v_norm_mix, field(2, (nl, d)), "adamw_norm_mix")
    results["norm_ff2"] = adamw(norm_ff2, m_norm_ff2, v_norm_ff2, field(3, (nl, d)), "adamw_norm_ff2")
    conv_slots = lax.dynamic_slice_in_dim(field(4, (nl, CONV_WIDTH, 3 * dn_w)), me * conv_w.shape[2],
                                          conv_w.shape[2], axis=3)
    results["conv_w"] = adamw(conv_w, m_conv_w, v_conv_w, conv_slots, "adamw_conv_w")
    results["q_norm"] = adamw(q_norm, m_q_norm, v_q_norm, field(5, (nl, ATT_HEAD_DIM)), "adamw_q_norm")
    results["k_norm"] = adamw(k_norm, m_k_norm, v_k_norm, field(6, (nl, ATT_HEAD_DIM)), "adamw_k_norm")
    results["a_log"] = adamw(a_log, m_a_log, v_a_log, field(7, (nl, nh)), "adamw_a_log")
    results["dt_bias"] = adamw(dt_bias, m_dt_bias, v_dt_bias, field(8, (nl, nh)), "adamw_dt_bias")
    results["dn_norm"] = adamw(dn_norm, m_dn_norm, v_dn_norm, field(9, (nl, DN_HEAD_DIM)), "adamw_dn_norm")
    wait_exchanges([0], results["ada_w"][1])
    for ki, nm in enumerate(big_names):
        res = big_adamw(ki, [0], partial[ki], "_l0")
        results[nm] = tuple(jnp.swapaxes(r, 1, 2) for r in res) if ki in (0, 6) else res

    order = ["ada_w", "ada_b", "norm_ff1", "ffn1_w_up", "ffn1_w_down", "norm_mix", "w_in", "q_norm", "k_norm",
             "conv_w", "a_log", "dt_bias", "dn_norm", "w_proj_att", "w_proj_dn", "w_out", "norm_ff2",
             "ffn2_w_up", "ffn2_w_down"]
    outs = [loss, dxc[None]]
    for part in range(4):
        outs += [results[n][part] for n in order]
    return tuple(outs)
```

```python
import functools

import jax
import jax.numpy as jnp
from jax import lax
from jax.experimental import pallas as pl
from jax.experimental.pallas import tpu as pltpu

F32 = jnp.float32
CD = jnp.bfloat16
EPS = 1e-6
N_DEV = 8
LANE = 128
ATT_HEAD_DIM = 64
ATT_BLOCK = 128
DILATIONS = (1, 4, 16)
DN_HEAD_DIM = 128
DN_CHUNK = 64
CONV_WIDTH = 4
N_ADA = 9
ADAM_LR, ADAM_B1, ADAM_B2, ADAM_EPS, ADAM_WD, ADAM_STEP = 0.001, 0.9, 0.999, 1e-08, 0.01, 10
VMEM_LIMIT = 56 * 1024 * 1024
NEG = -1e30
MESH = pl.DeviceIdType.MESH


def _params(sem=None):
    return pltpu.CompilerParams(dimension_semantics=sem, vmem_limit_bytes=VMEM_LIMIT)


def _tile(n, pref, unit):
    best = None
    t = unit
    while t <= min(n, pref):
        if n % t == 0:
            best = t
        t += unit
    return best if best is not None else n


def _silu(x):
    return x * jax.nn.sigmoid(x)


def _dot(a, b, dims, precision=None):
    if precision is None:
        a, b = a.astype(CD), b.astype(CD)
    return lax.dot_general(a, b, (dims, ((), ())), precision=precision, preferred_element_type=F32)


def _nn(a, b, precision=None):
    return _dot(a, b, ((1,), (0,)), precision)


def _nt(a, b, precision=None):
    return _dot(a, b, ((1,), (1,)), precision)


def _tn(a, b, precision=None):
    return _dot(a, b, ((0,), (0,)), precision)


def all_gather(arrs, name):
    n = len(arrs)

    def body(*refs):
        x_refs, out_refs = refs[:n], refs[n:2 * n]
        send_sems, recv_sems, local_sems = refs[2 * n:]
        x, y, c = lax.axis_index("x"), lax.axis_index("y"), lax.axis_index("c")
        me, sibling = (x, y, c), (x, y, 1 - c)
        chips = [(1 - x, y), (x, 1 - y), (1 - x, 1 - y)]

        def copy(a, k, block, to, src=None):
            slot = out_refs[a].at[4 * block[0] + 2 * block[1] + block[2]]
            return pltpu.make_async_remote_copy(
                src_ref=slot if src is None else src, dst_ref=slot,
                send_sem=send_sems.at[7 * a + k], recv_sem=recv_sems.at[7 * a + k],
                device_id=to, device_id_type=MESH)

        mine, first, passed = [], [], []
        for a in range(n):
            cp = pltpu.make_async_copy(x_refs[a], out_refs[a].at[4 * x + 2 * y + c], local_sems.at[a])
            cp.start()
            mine.append(cp)
            first.append(copy(a, 0, me, sibling, src=x_refs[a]))
            first += [copy(a, 1 + j, me, (*chip, c), src=x_refs[a]) for j, chip in enumerate(chips)]
        for cp in first:
            cp.start()
        for j, chip in enumerate(chips):
            for a in range(n):
                copy(a, 1 + j, (*chip, c), me).wait_recv()
                cp = copy(a, 4 + j, (*chip, c), sibling)
                cp.start()
                passed.append(cp)
        for a in range(n):
            copy(a, 0, sibling, me).wait_recv()
            for j, chip in enumerate(chips):
                copy(a, 4 + j, (*chip, 1 - c), me).wait_recv()
        for cp in first + passed:
            cp.wait_send()
        for cp in mine:
            cp.wait()

    hbm = pl.BlockSpec(memory_space=pl.ANY)
    outs = pl.pallas_call(
        body, name=name,
        out_shape=[jax.ShapeDtypeStruct((N_DEV,) + a.shape, a.dtype) for a in arrs],
        in_specs=[hbm] * n, out_specs=[hbm] * n,
        scratch_shapes=[pltpu.SemaphoreType.DMA((7 * n,)), pltpu.SemaphoreType.DMA((7 * n,)),
                        pltpu.SemaphoreType.DMA((n,))],
    )(*arrs)
    return list(outs)


_HBM = pl.BlockSpec(memory_space=pltpu.HBM)
_SEM = pl.BlockSpec(memory_space=pltpu.SEMAPHORE)
_EFFECT = pltpu.SideEffectType.DATAFLOW_SIDE_EFFECTING


def _exchange_copies(src_refs, land_refs, send_sem, recv_sem, gather):
    x, y, c = lax.axis_index("x"), lax.axis_index("y"), lax.axis_index("c")
    me = 4 * x + 2 * y + c
    pairs = []
    for k in range(1, N_DEV):
        px, py, pc = x ^ (k >> 2), y ^ ((k >> 1) & 1), c ^ (k & 1)
        pidx = 4 * px + 2 * py + pc
        for src, land in zip(src_refs, land_refs):
            mine = src if gather else src.at[pidx]
            out = pltpu.make_async_remote_copy(src_ref=mine, dst_ref=land.at[me], send_sem=send_sem, recv_sem=recv_sem,
                                               device_id=(px, py, pc), device_id_type=MESH)
            inc = pltpu.make_async_remote_copy(src_ref=mine, dst_ref=land.at[pidx], send_sem=send_sem, recv_sem=recv_sem,
                                               device_id=(px, py, pc), device_id_type=MESH)
            pairs.append((out, inc))
    return pairs


def exchange_start(srcs, lands, after, gather, name):
    n = len(srcs)

    def body(*refs):
        src_refs, land_refs = refs[:n], refs[n:2 * n]
        send_sem, recv_sem = refs[2 * n + 1], refs[2 * n + 2]
        token = refs[-1]
        for out, _ in _exchange_copies(src_refs, land_refs, send_sem, recv_sem, gather):
            out.start()
        token[...] = jnp.zeros_like(token)

    res = pl.pallas_call(
        body, name=name,
        out_shape=(pltpu.SemaphoreType.DMA(()), pltpu.SemaphoreType.DMA(()),
                   *[pltpu.HBM(a.shape, a.dtype) for a in srcs], *[pltpu.HBM(a.shape, a.dtype) for a in lands],
                   jax.ShapeDtypeStruct((8, LANE), F32)),
        in_specs=[_HBM] * (2 * n) + [pl.BlockSpec(memory_space=pl.ANY)],
        out_specs=(_SEM, _SEM, *[_HBM] * (2 * n), pl.BlockSpec(memory_space=pltpu.VMEM)),
        input_output_aliases={i: 2 + i for i in range(2 * n)},
        compiler_params=pltpu.CompilerParams(has_side_effects=_EFFECT),
    )(*[pltpu.with_memory_space_constraint(a, pltpu.HBM) for a in list(srcs) + list(lands)], after)
    return res[0], res[1], list(res[2:2 + n]), list(res[2 + n:2 + 2 * n]), res[-1]


def exchange_wait(send_sem, recv_sem, srcs, lands, after, gather, name):
    n = len(srcs)

    def body(*refs):
        src_refs, land_refs = refs[:n], refs[n:2 * n]
        s_sem, r_sem = refs[2 * n], refs[2 * n + 1]
        for out, inc in _exchange_copies(src_refs, land_refs, s_sem, r_sem, gather):
            out.wait_send()
            inc.wait_recv()

    res = pl.pallas_call(
        body, name=name,
        out_shape=[pltpu.HBM(a.shape, a.dtype) for a in list(srcs) + list(lands)],
        in_specs=[_HBM] * (2 * n) + [_SEM, _SEM, pl.BlockSpec(memory_space=pl.ANY)],
        out_specs=[_HBM] * (2 * n),
        input_output_aliases={i: i for i in range(2 * n)},
        compiler_params=pltpu.CompilerParams(has_side_effects=_EFFECT),
    )(*srcs, *lands, send_sem, recv_sem, after)
    return list(res[n:])


def matmul(pairs, mode, out_dtype, name, *, a_scale=None, out_scale=None, resid=None,
           save_acc=False, tm=512, tn=512, tk=2048):
    a0, b0 = pairs[0]
    if mode == "nn":
        (m, kdim), n = a0.shape, b0.shape[1]
    elif mode == "nt":
        (m, kdim), n = a0.shape, b0.shape[0]
    else:
        (kdim, m), n = a0.shape, b0.shape[1]
    tm, tn = _tile(m, tm, 8 if m % LANE else LANE), _tile(n, tn, LANE)
    tk = _tile(kdim, tk, LANE)
    nk = kdim // tk
    sub = tn if mode == "tn" else _tile(tn, 256, LANE)
    npairs = len(pairs)
    dims = {"nn": ((1,), (0,)), "nt": ((1,), (1,)), "tn": ((0,), (0,))}[mode]

    if mode == "nn":
        a_spec = pl.BlockSpec((tm, tk), lambda i, j, k: (i, k))
        b_spec = pl.BlockSpec((tk, tn), lambda i, j, k: (k, j))
    elif mode == "nt":
        a_spec = pl.BlockSpec((tm, tk), lambda i, j, k: (i, k))
        b_spec = pl.BlockSpec((tn, tk), lambda i, j, k: (j, k))
    else:
        a_spec = pl.BlockSpec((tk, tm), lambda i, j, k: (k, i))
        b_spec = pl.BlockSpec((tk, tn), lambda i, j, k: (k, j))

    def body(*refs):
        it = iter(refs)
        pair_refs = [(next(it), next(it)) for _ in range(npairs)]
        as_ref = next(it) if a_scale is not None else None
        os_ref = next(it) if out_scale is not None else None
        rs_ref = next(it) if resid is not None else None
        o_ref = next(it)
        acc_out = next(it) if save_acc else None
        acc_ref = next(it) if nk > 1 else None
        lhs = []
        for a_ref, _ in pair_refs:
            a = a_ref[...]
            if as_ref is not None:
                a = a.astype(F32) * as_ref[...]
            lhs.append(a.astype(CD))

        def product(cols):
            part = None
            for a, (_, b_ref) in zip(lhs, pair_refs):
                b = b_ref[...] if cols is None else (b_ref[cols, :] if mode == "nt" else b_ref[:, cols])
                d = _dot(a, b, dims)
                part = d if part is None else part + d
            return part

        def finish(acc, cols=slice(None)):
            if acc_out is not None:
                acc_out[:, cols] = acc
            if os_ref is not None:
                acc = acc * os_ref[:, cols]
            if rs_ref is not None:
                acc = rs_ref[:, cols] + acc
            o_ref[:, cols] = acc.astype(o_ref.dtype)

        if nk == 1:
            for c0 in range(0, tn, sub):
                finish(product(slice(c0, c0 + sub)), slice(c0, c0 + sub))
        else:
            part = product(None)
            k = pl.program_id(2)

            @pl.when(k == 0)
            def _():
                acc_ref[...] = part

            @pl.when(k > 0)
            def _():
                acc_ref[...] += part

            @pl.when(k == nk - 1)
            def _():
                finish(acc_ref[...])

    in_specs, args = [], []
    for a, b in pairs:
        in_specs += [a_spec, b_spec]
        args += [a, b]
    if a_scale is not None:
        assert mode != "tn"
        in_specs.append(pl.BlockSpec((1, tk), lambda i, j, k: (0, k)))
        args.append(a_scale)
    if out_scale is not None:
        in_specs.append(pl.BlockSpec((1, tn), lambda i, j, k: (0, j)))
        args.append(out_scale)
    if resid is not None:
        in_specs.append(pl.BlockSpec((tm, tn), lambda i, j, k: (i, j)))
        args.append(resid)
    o_spec = pl.BlockSpec((tm, tn), lambda i, j, k: (i, j))
    out_shape = [jax.ShapeDtypeStruct((m, n), out_dtype)]
    out_specs = [o_spec]
    if save_acc:
        out_shape.append(jax.ShapeDtypeStruct((m, n), F32))
        out_specs.append(o_spec)
    scratch = [pltpu.VMEM((tm, tn), F32)] if nk > 1 else []
    res = pl.pallas_call(
        body, name=name, grid=(m // tm, n // tn, nk),
        in_specs=in_specs, out_specs=out_specs, out_shape=out_shape, scratch_shapes=scratch,
        compiler_params=_params(("parallel", "parallel", "arbitrary")),
    )(*args)
    return res if save_acc else res[0]


def _nm_fn(x, g, sc, sh):
    r = lax.rsqrt(jnp.mean(x * x, axis=-1, keepdims=True) + EPS)
    return (x * r * g) * (1.0 + sc) + sh


def norm_mod(x, g, sc, sh, name):
    t, d = x.shape
    tr = _tile(t, 256, 8)

    def body(x_ref, g_ref, sc_ref, sh_ref, h_ref):
        h_ref[...] = _nm_fn(x_ref[...], g_ref[...], sc_ref[...], sh_ref[...]).astype(h_ref.dtype)

    row = pl.BlockSpec((tr, d), lambda i: (i, 0))
    vec = pl.BlockSpec((1, d), lambda i: (0, 0))
    return pl.pallas_call(
        body, name=name, grid=(t // tr,), in_specs=[row, vec, vec, vec], out_specs=row,
        out_shape=jax.ShapeDtypeStruct((t, d), CD), compiler_params=_params(("parallel",)),
    )(x, g, sc, sh)


def norm_mod_bwd(x, g, sc, sh, dh, dxo, f, gate_scale, name):
    t, d = x.shape
    tr = _tile(t, 256, 8)

    def body(x_ref, g_ref, sc_ref, sh_ref, dh_ref, dxo_ref, f_ref, dx_ref, dg_ref, dsc_ref, dsh_ref, dgt_ref):
        _, vjp = jax.vjp(_nm_fn, x_ref[...], g_ref[...], sc_ref[...], sh_ref[...])
        dx, dg, dsc, dsh = vjp(dh_ref[...])
        dxo_v = dxo_ref[...]
        dx_ref[...] = dxo_v + dx
        dgt = gate_scale * jnp.sum(f_ref[...] * dxo_v, axis=0, keepdims=True)

        @pl.when(pl.program_id(0) == 0)
        def _():
            dg_ref[...] = dg
            dsc_ref[...] = dsc
            dsh_ref[...] = dsh
            dgt_ref[...] = dgt

        @pl.when(pl.program_id(0) > 0)
        def _():
            dg_ref[...] += dg
            dsc_ref[...] += dsc
            dsh_ref[...] += dsh
            dgt_ref[...] += dgt

    row = pl.BlockSpec((tr, d), lambda i: (i, 0))
    vec = pl.BlockSpec((1, d), lambda i: (0, 0))
    vshape = jax.ShapeDtypeStruct((1, d), F32)
    return pl.pallas_call(
        body, name=name, grid=(t // tr,), in_specs=[row, vec, vec, vec, row, row, row],
        out_specs=[row, vec, vec, vec, vec],
        out_shape=[jax.ShapeDtypeStruct((t, d), F32), vshape, vshape, vshape, vshape],
        compiler_params=_params(("arbitrary",)),
    )(x, g, sc, sh, dh, dxo, f)


HALF = N_DEV // 2


def ffn_up(h, u_all, name):
    t, d = h.shape
    cp = u_all.shape[1]
    f = HALF * cp
    tm, tn = _tile(t, 1024, LANE), cp
    per = cp // tn
    sub = _tile(tn, 256, LANE)

    def body(h_ref, wg_ref, wu_ref, g_ref, u_ref, a_ref):
        hv = h_ref[...]
        for c0 in range(0, tn, sub):
            cols = slice(c0, c0 + sub)
            gate = _nt(hv, wg_ref[0, cols, :])
            up = _nt(hv, wu_ref[0, cols, :])
            g_ref[:, cols] = gate.astype(g_ref.dtype)
            u_ref[:, cols] = up.astype(u_ref.dtype)
            a_ref[:, cols] = (_silu(gate) * up).astype(a_ref.dtype)

    o = pl.BlockSpec((tm, tn), lambda i, j: (i, j))
    wg = pl.BlockSpec((1, tn, d), lambda i, j: (j // per, j % per, 0))
    wu = pl.BlockSpec((1, tn, d), lambda i, j: (HALF + j // per, j % per, 0))
    return pl.pallas_call(
        body, name=name, grid=(t // tm, f // tn),
        in_specs=[pl.BlockSpec((tm, d), lambda i, j: (i, 0)), wg, wu], out_specs=[o, o, o],
        out_shape=[jax.ShapeDtypeStruct((t, f), CD)] * 3,
        compiler_params=_params(("parallel", "parallel")),
    )(h, u_all, u_all)


def ffn_up_wg(h, dgate, dup, cp, name):
    t, d = h.shape
    tn = _tile(d, 1024, LANE)

    def body(h_ref, dg_ref, du_ref, o_ref):
        s = pl.program_id(0)

        @pl.when(s < HALF)
        def _():
            o_ref[0] = _tn(dg_ref[...], h_ref[...]).astype(o_ref.dtype)

        @pl.when(s >= HALF)
        def _():
            o_ref[0] = _tn(du_ref[...], h_ref[...]).astype(o_ref.dtype)

    return pl.pallas_call(
        body, name=name, grid=(N_DEV, d // tn),
        in_specs=[pl.BlockSpec((t, tn), lambda s, i: (0, i)),
                  pl.BlockSpec((t, cp), lambda s, i: (0, jnp.minimum(s, HALF - 1))),
                  pl.BlockSpec((t, cp), lambda s, i: (0, jnp.maximum(s - HALF, 0)))],
        out_specs=pl.BlockSpec((1, cp, tn), lambda s, i: (s, 0, i)),
        out_shape=jax.ShapeDtypeStruct((N_DEV, cp, d), CD),
        compiler_params=_params(("parallel", "parallel")),
    )(h, dgate, dup)


def ffn_up_dg(dgate, dup, u_all, name):
    t, f = dgate.shape
    cp, d = u_all.shape[1], u_all.shape[2]
    tm, tn = _tile(t, 1024, LANE), _tile(d, 512, LANE)

    def body(dg_ref, du_ref, u_ref, o_ref):
        wg = u_ref[0:HALF].reshape(f, tn)
        wu = u_ref[HALF:N_DEV].reshape(f, tn)
        o_ref[...] = _nn(dg_ref[...], wg) + _nn(du_ref[...], wu)

    return pl.pallas_call(
        body, name=name, grid=(t // tm, d // tn),
        in_specs=[pl.BlockSpec((tm, f), lambda i, j: (i, 0)), pl.BlockSpec((tm, f), lambda i, j: (i, 0)),
                  pl.BlockSpec((N_DEV, cp, tn), lambda i, j: (0, 0, j))],
        out_specs=pl.BlockSpec((tm, tn), lambda i, j: (i, j)),
        out_shape=jax.ShapeDtypeStruct((t, d), F32),
        compiler_params=_params(("parallel", "parallel")),
    )(dgate, dup, u_all)


def ffn_dact(dxo, s, wd, gate, up, name):
    t, d = dxo.shape
    f = wd.shape[0]
    tm, tn = _tile(t, 1024, LANE), _tile(f, 768, LANE)
    sub = _tile(tn, 256, LANE)

    def body(dxo_ref, s_ref, wd_ref, g_ref, u_ref, dg_ref, du_ref):
        dxs = (dxo_ref[...] * s_ref[...]).astype(CD)
        for c0 in range(0, tn, sub):
            cols = slice(c0, c0 + sub)
            da = _nt(dxs, wd_ref[cols, :])
            gate, up = g_ref[:, cols].astype(F32), u_ref[:, cols].astype(F32)
            sg = jax.nn.sigmoid(gate)
            dg_ref[:, cols] = (da * up * sg * (1.0 + gate * (1.0 - sg))).astype(dg_ref.dtype)
            du_ref[:, cols] = (da * gate * sg).astype(du_ref.dtype)

    o = pl.BlockSpec((tm, tn), lambda i, j: (i, j))
    return pl.pallas_call(
        body, name=name, grid=(t // tm, f // tn),
        in_specs=[pl.BlockSpec((tm, d), lambda i, j: (i, 0)), pl.BlockSpec((1, d), lambda i, j: (0, 0)),
                  pl.BlockSpec((tn, d), lambda i, j: (j, 0)), o, o],
        out_specs=[o, o],
        out_shape=[jax.ShapeDtypeStruct((t, f), CD), jax.ShapeDtypeStruct((t, f), CD)],
        compiler_params=_params(("parallel", "parallel")),
    )(dxo, s, wd, gate, up)


def _bdot(a, b, ca, cb, precision=None):
    if precision is None:
        a, b = a.astype(CD), b.astype(CD)
    return lax.dot_general(a, b, (((ca,), (cb,)), ((0,), (0,))), precision=precision, preferred_element_type=F32)


def _bnn(a, b, precision=None):
    return _bdot(a, b, 2, 1, precision)


def _bnt(a, b, precision=None):
    return _bdot(a, b, 2, 2, precision)


def _btn(a, b, precision=None):
    return _bdot(a, b, 1, 1, precision)


def _att_fn(q, kp, kc, vp, vc, qg, kg, mask_p, mask_c):
    b, w = q.shape
    nh = w // ATT_HEAD_DIM
    head_of_lane = lax.broadcasted_iota(jnp.int32, (nh, 1, w), 2) // ATT_HEAD_DIM
    hm = (head_of_lane == lax.broadcasted_iota(jnp.int32, (nh, 1, w), 0)).astype(F32)

    def rn(x, g):
        ss = jnp.sum((x * x)[None] * hm, axis=-1, keepdims=True)
        r = jnp.sum(lax.rsqrt(ss * (1.0 / ATT_HEAD_DIM) + EPS) * hm, axis=0)
        return x * r * g

    qn, kcn = rn(q, qg), rn(kc, kg)
    q4 = (qn[None] * hm).reshape(nh * b, w)
    scale = ATT_HEAD_DIM ** -0.5
    sc = jnp.where(mask_c, _nt(q4, kcn) * scale, NEG)
    if kp is None:
        m = jnp.max(sc, axis=-1, keepdims=True)
        pc = jnp.exp(sc - m)
        den = jnp.sum(pc, axis=-1, keepdims=True)
        o4 = _nn(pc / den, vc)
    else:
        sp = jnp.where(mask_p, _nt(q4, rn(kp, kg)) * scale, NEG)
        m = jnp.maximum(jnp.max(sp, axis=-1, keepdims=True), jnp.max(sc, axis=-1, keepdims=True))
        pp, pc = jnp.exp(sp - m), jnp.exp(sc - m)
        den = jnp.sum(pp, axis=-1, keepdims=True) + jnp.sum(pc, axis=-1, keepdims=True)
        o4 = _nn(pp / den, vp) + _nn(pc / den, vc)
    o = jnp.sum(o4.reshape(nh, b, w) * hm, axis=0)
    lse = jnp.sum((m + jnp.log(den)).reshape(nh, b, 1) * hm, axis=0)
    return o, lse


def _att_masks(g, j, nb_total, nh):
    nb = jnp.int32(nb_total // DILATIONS[0])
    for gi in range(1, len(DILATIONS)):
        nb = jnp.where(g == gi, jnp.int32(nb_total // DILATIONS[gi]), nb)
    has_prev = (j % nb) != 0
    row = lax.broadcasted_iota(jnp.int32, (nh * ATT_BLOCK, ATT_BLOCK), 0) % ATT_BLOCK
    col = lax.broadcasted_iota(jnp.int32, (nh * ATT_BLOCK, ATT_BLOCK), 1)
    return has_prev, col >= row, col <= row


def _att_specs(w):
    blk = (1, 1, ATT_BLOCK, w)
    q = pl.BlockSpec(blk, lambda g, j: (0, g, j, 0))
    kp = pl.BlockSpec(blk, lambda g, j: (1, g, jnp.maximum(j - 1, 0), 0))
    kc = pl.BlockSpec(blk, lambda g, j: (1, g, j, 0))
    vp = pl.BlockSpec(blk, lambda g, j: (2, g, jnp.maximum(j - 1, 0), 0))
    vc = pl.BlockSpec(blk, lambda g, j: (2, g, j, 0))
    gain = pl.BlockSpec((1, w), lambda g, j: (0, 0))
    out = pl.BlockSpec((1, ATT_BLOCK, w), lambda g, j: (g, j, 0))
    return [q, kp, kc, vp, vc, gain, gain], out


def att_fwd(qkv, qg, kg, name):
    _, ng, t, w = qkv.shape
    nbt = t // ATT_BLOCK
    in_specs, out = _att_specs(w)

    def body(q_ref, kp_ref, kc_ref, vp_ref, vc_ref, qg_ref, kg_ref, o_ref, lse_ref):
        has_prev, mask_p, mask_c = _att_masks(pl.program_id(0), pl.program_id(1), nbt, w // ATT_HEAD_DIM)

        @pl.when(has_prev)
        def _():
            o, lse = _att_fn(q_ref[0, 0], kp_ref[0, 0], kc_ref[0, 0], vp_ref[0, 0], vc_ref[0, 0],
                             qg_ref[...], kg_ref[...], mask_p, mask_c)
            o_ref[0] = o
            lse_ref[0] = lse

        @pl.when(jnp.logical_not(has_prev))
        def _():
            o, lse = _att_fn(q_ref[0, 0], None, kc_ref[0, 0], None, vc_ref[0, 0],
                             qg_ref[...], kg_ref[...], None, mask_c)
            o_ref[0] = o
            lse_ref[0] = lse

    sh = jax.ShapeDtypeStruct((ng, t, w), F32)
    return pl.pallas_call(
        body, name=name, grid=(ng, nbt), in_specs=in_specs, out_specs=[out, out], out_shape=[sh, sh],
        compiler_params=_params(("parallel", "parallel")),
    )(qkv, qkv, qkv, qkv, qkv, qg, kg)


def att_bwd(qkv, qg, kg, do, dlse, name):
    _, ng, t, w = qkv.shape
    nbt = t // ATT_BLOCK
    in_specs, out = _att_specs(w)
    whole = pl.BlockSpec((1, t, w), lambda g, j: (g, 0, 0))
    gain = in_specs[-1]

    def body(q_ref, kp_ref, kc_ref, vp_ref, vc_ref, qg_ref, kg_ref, do_ref, dlse_ref,
             dq_ref, dk_ref, dv_ref, dqg_ref, dkg_ref):
        g, j = pl.program_id(0), pl.program_id(1)
        has_prev, mask_p, mask_c = _att_masks(g, j, nbt, w // ATT_HEAD_DIM)

        @pl.when(j == 0)
        def _():
            dk_ref[...] = jnp.zeros_like(dk_ref)
            dv_ref[...] = jnp.zeros_like(dv_ref)

        @pl.when(jnp.logical_and(g == 0, j == 0))
        def _():
            dqg_ref[...] = jnp.zeros_like(dqg_ref)
            dkg_ref[...] = jnp.zeros_like(dkg_ref)

        rows_c = pl.ds(pl.multiple_of(j * ATT_BLOCK, ATT_BLOCK), ATT_BLOCK)
        rows_p = pl.ds(pl.multiple_of(jnp.maximum(j - 1, 0) * ATT_BLOCK, ATT_BLOCK), ATT_BLOCK)
        @pl.when(has_prev)
        def _():
            fn = functools.partial(_att_fn, mask_p=mask_p, mask_c=mask_c)
            _, vjp = jax.vjp(fn, q_ref[0, 0], kp_ref[0, 0], kc_ref[0, 0], vp_ref[0, 0], vc_ref[0, 0],
                             qg_ref[...], kg_ref[...])
            dq, dkp, dkc, dvp, dvc, dqg, dkg = vjp((do_ref[0], dlse_ref[0]))
            dq_ref[0] = dq
            dk_ref[0, rows_p, :] += dkp
            dv_ref[0, rows_p, :] += dvp
            dk_ref[0, rows_c, :] += dkc
            dv_ref[0, rows_c, :] += dvc
            dqg_ref[...] += dqg
            dkg_ref[...] += dkg

        @pl.when(jnp.logical_not(has_prev))
        def _():
            def fn(q, kc, vc, qg, kg):
                return _att_fn(q, None, kc, None, vc, qg, kg, None, mask_c)

            _, vjp = jax.vjp(fn, q_ref[0, 0], kc_ref[0, 0], vc_ref[0, 0], qg_ref[...], kg_ref[...])
            dq, dkc, dvc, dqg, dkg = vjp((do_ref[0], dlse_ref[0]))
            dq_ref[0] = dq
            dk_ref[0, rows_c, :] += dkc
            dv_ref[0, rows_c, :] += dvc
            dqg_ref[...] += dqg
            dkg_ref[...] += dkg

    sh = jax.ShapeDtypeStruct((ng, t, w), F32)
    gshape = jax.ShapeDtypeStruct((1, w), F32)
    return pl.pallas_call(
        body, name=name, grid=(ng, nbt), in_specs=in_specs + [out, out],
        out_specs=[out, whole, whole, gain, gain], out_shape=[sh, sh, sh, gshape, gshape],
        compiler_params=_params(("arbitrary", "arbitrary")),
    )(qkv, qkv, qkv, qkv, qkv, qg, kg, do, dlse)


def _combine_fn(o, lse):
    m = jnp.max(lse, axis=0, keepdims=True)
    e = jnp.exp(lse - m)
    w = e / jnp.sum(e, axis=0, keepdims=True)
    return jnp.sum(w * o, axis=0)


def att_combine(o, lse, name):
    ng, t, w = o.shape
    tr = _tile(t, 256, 16)
    spec = pl.BlockSpec((ng, tr, w), lambda i: (0, i, 0))
    y_spec = pl.BlockSpec((tr, w), lambda i: (i, 0))

    def body(o_ref, l_ref, y_ref):
        y_ref[...] = _combine_fn(o_ref[...], l_ref[...]).astype(y_ref.dtype)

    return pl.pallas_call(
        body, name=name, grid=(t // tr,), in_specs=[spec, spec], out_specs=y_spec,
        out_shape=jax.ShapeDtypeStruct((t, w), CD), compiler_params=_params(("parallel",)),
    )(o, lse)


def att_combine_bwd(o, lse, dy, name):
    ng, t, w = o.shape
    tr = _tile(t, 256, 8)
    spec = pl.BlockSpec((ng, tr, w), lambda i: (0, i, 0))
    y_spec = pl.BlockSpec((tr, w), lambda i: (i, 0))

    def body(o_ref, l_ref, dy_ref, do_ref, dl_ref):
        _, vjp = jax.vjp(_combine_fn, o_ref[...], l_ref[...])
        do, dl = vjp(dy_ref[...])
        do_ref[...] = do
        dl_ref[...] = dl

    sh = jax.ShapeDtypeStruct(o.shape, F32)
    return pl.pallas_call(
        body, name=name, grid=(t // tr,), in_specs=[spec, spec, y_spec], out_specs=[spec, spec],
        out_shape=[sh, sh], compiler_params=_params(("parallel",)),
    )(o, lse, dy)


def _shift_down(x, s):
    if s == 0:
        return x
    row = lax.broadcasted_iota(jnp.int32, x.shape, 0)
    return jnp.where(row >= s, pltpu.roll(x, s, 0), 0.0)


def _shift_up(x, s):
    if s == 0:
        return x
    t = x.shape[0]
    row = lax.broadcasted_iota(jnp.int32, x.shape, 0)
    return jnp.where(row < t - s, pltpu.roll(x, t - s, 0), 0.0)


def conv_fwd(z, col0, w, name):
    t = z.shape[0]
    c = w.shape[1]
    cb = _tile(c, 512, LANE)
    assert col0 % cb == 0
    nblk0 = col0 // cb

    def body(z_ref, w_ref, c_ref):
        for c0 in range(0, cb, LANE):
            cols = slice(c0, c0 + LANE)
            zv = z_ref[:, cols]
            acc = None
            for i in range(CONV_WIDTH):
                term = _shift_down(zv, CONV_WIDTH - 1 - i) * w_ref[i:i + 1, cols]
                acc = term if acc is None else acc + term
            c_ref[:, cols] = acc

    return pl.pallas_call(
        body, name=name, grid=(c // cb,),
        in_specs=[pl.BlockSpec((t, cb), lambda j: (0, nblk0 + j)), pl.BlockSpec((CONV_WIDTH, cb), lambda j: (0, j))],
        out_specs=pl.BlockSpec((t, cb), lambda j: (0, j)),
        out_shape=jax.ShapeDtypeStruct((t, c), F32), compiler_params=_params(("parallel",)),
    )(z, w)


def conv_bwd(dc, z, col0, w, name):
    t = z.shape[0]
    c = w.shape[1]
    cb = _tile(c, 512, LANE)
    assert col0 % cb == 0
    nblk0 = col0 // cb

    def body(dc_ref, z_ref, w_ref, dz_ref, dw_ref):
        for c0 in range(0, cb, LANE):
            cols = slice(c0, c0 + LANE)
            dcv, zv = dc_ref[:, cols], z_ref[:, cols]
            acc = None
            for i in range(CONV_WIDTH):
                s = CONV_WIDTH - 1 - i
                term = _shift_up(dcv, s) * w_ref[i:i + 1, cols]
                acc = term if acc is None else acc + term
                dw_ref[i:i + 1, cols] = jnp.sum(dcv * _shift_down(zv, s), axis=0, keepdims=True)
            dz_ref[:, cols] = acc.astype(dz_ref.dtype)

    blk = pl.BlockSpec((t, cb), lambda j: (0, j))
    wblk = pl.BlockSpec((CONV_WIDTH, cb), lambda j: (0, j))
    return pl.pallas_call(
        body, name=name, grid=(c // cb,),
        in_specs=[blk, pl.BlockSpec((t, cb), lambda j: (0, nblk0 + j)), wblk], out_specs=[blk, wblk],
        out_shape=[jax.ShapeDtypeStruct((t, c), CD), jax.ShapeDtypeStruct((CONV_WIDTH, c), F32)],
        compiler_params=_params(("parallel",)),
    )(dc, z, w)


def _dn_consts():
    c = DN_CHUNK
    row = lax.broadcasted_iota(jnp.int32, (c, c), 0)
    col = lax.broadcasted_iota(jnp.int32, (c, c), 1)
    return dict(tril=row >= col, strict=row > col, eye=(row == col).astype(F32),
                tril_f=(row >= col).astype(F32), triu_f=(row <= col).astype(F32))


def _softplus(x):
    return jnp.maximum(x, 0.0) + jnp.log(1.0 + jnp.exp(-jnp.abs(x)))


def _split(x):
    hi = x.astype(CD)
    return hi, (x - hi.astype(F32)).astype(CD)


def _bdot3(a, b, ca, cb):
    ah, al = _split(a)
    bh, bl = _split(b)
    return _bdot(ah, bh, ca, cb) + (_bdot(ah, bl, ca, cb) + _bdot(al, bh, ca, cb))


def _tri_inv_impl(a_mat):
    c = a_mat.shape[-1]
    eye = (lax.broadcasted_iota(jnp.int32, (c, c), 0) == lax.broadcasted_iota(jnp.int32, (c, c), 1)).astype(F32)
    nk_ = -a_mat
    t_inv = eye + nk_
    for _ in range(c.bit_length() - 2):
        nk_ = _bdot3(nk_, nk_, 2, 1)
        t_inv = t_inv + _bdot3(t_inv, nk_, 2, 1)
    return t_inv


@jax.custom_vjp
def _tri_inv_saved(a_mat, t_inv):
    return t_inv


def _tri_inv_saved_fwd(a_mat, t_inv):
    return t_inv, t_inv


def _tri_inv_saved_bwd(t_inv, dt_inv):
    return -_bdot3(_bdot3(t_inv, dt_inv, 1, 1), t_inv, 2, 2), jnp.zeros_like(t_inv)


_tri_inv_saved.defvjp(_tri_inv_saved_fwd, _tri_inv_saved_bwd)


def _dn_chunk(cq, ck, cv, og, a_col, b_col, al, dt, gn, s_prev, *, k, inv, with_inv=False):
    q = _silu(cq)
    q = q * lax.rsqrt(jnp.sum(q * q, axis=-1, keepdims=True) + EPS) * (DN_HEAD_DIM ** -0.5)
    kk = _silu(ck)
    kk = kk * lax.rsqrt(jnp.sum(kk * kk, axis=-1, keepdims=True) + EPS)
    v = _silu(cv)
    g = -jnp.exp(al) * _softplus(a_col + dt)
    beta = jax.nn.sigmoid(b_col)
    g_row = jnp.sum(k["eye"] * g, axis=1, keepdims=True)
    gc_col = jnp.sum(k["tril_f"] * g_row, axis=2, keepdims=True)
    gc_row = jnp.sum(k["triu_f"] * g, axis=1, keepdims=True)
    ldec = jnp.where(k["tril"], jnp.exp(jnp.where(k["tril"], gc_col - gc_row, 0.0)), 0.0)
    kb, vb = kk * beta, v * beta
    a_mat = jnp.where(k["strict"], _bnt(kb, kk) * ldec, 0.0)
    t_inv = inv(a_mat)
    egc = jnp.exp(gc_col)
    u = _bnn(t_inv, vb)
    w = _bnn(t_inv, kb * egc)
    attn = jnp.where(k["tril"], _bnt(q, kk) * ldec, 0.0)
    gc_last = jnp.sum(g, axis=1, keepdims=True)
    k_dec = kk * jnp.exp(gc_last - gc_col)
    v_new = u - _bnn(w, s_prev)
    o = _bnn(q * egc, s_prev) + _bnn(attn, v_new)
    s_new = s_prev * jnp.exp(gc_last) + _btn(k_dec, v_new)
    y = o * lax.rsqrt(jnp.mean(o * o, axis=-1, keepdims=True) + EPS) * gn * _silu(og)
    return (y, s_new, t_inv) if with_inv else (y, s_new)


def _dn_heads(ref, nh):
    hd = DN_HEAD_DIM
    return jnp.stack([ref[:, h * hd:(h + 1) * hd] for h in range(nh)])


def _dn_specs(nh, col_gate, col_ab, order):
    hd, c = DN_HEAD_DIM, DN_CHUNK
    w = nh * hd
    qs = pl.BlockSpec((c, w), lambda n: (order(n), 0))
    ks = pl.BlockSpec((c, w), lambda n: (order(n), 1))
    vs = pl.BlockSpec((c, w), lambda n: (order(n), 2))
    gs = pl.BlockSpec((c, w), lambda n: (order(n), col_gate // w))
    ab = pl.BlockSpec((c, LANE), lambda n: (order(n), col_ab // LANE))
    scal = pl.BlockSpec((nh, 1, 1), lambda n: (0, 0, 0))
    gn = pl.BlockSpec((1, hd), lambda n: (0, 0))
    st = pl.BlockSpec((1, nh, hd, hd), lambda n: (order(n), 0, 0, 0))
    return qs, ks, vs, gs, ab, scal, gn, st


def _lane_pick(x, idx):
    lane = lax.broadcasted_iota(jnp.int32, x.shape, 1)
    return jnp.sum(jnp.where(lane == idx, x, 0.0), axis=1, keepdims=True)


def dn_fwd(cv, z, col_gate, col_ab, a_log, dt_bias, gn, name):
    t = cv.shape[0]
    nh = a_log.shape[0]
    hd, c = DN_HEAD_DIM, DN_CHUNK
    n_chunks = t // c
    qs, ks, vs, gs, ab, scal, gnspec, st = _dn_specs(nh, col_gate, col_ab, lambda n: n)

    def body(q_ref, k_ref, v_ref, g_ref, ab_ref, al_ref, dt_ref, gn_ref, y_ref, st_ref, ti_ref, s_scr):
        @pl.when(pl.program_id(0) == 0)
        def _():
            s_scr[...] = jnp.zeros_like(s_scr)

        abv = ab_ref[...]
        a_col = jnp.stack([_lane_pick(abv, h) for h in range(nh)])
        b_col = jnp.stack([_lane_pick(abv, nh + h) for h in range(nh)])
        s_prev = s_scr[...]
        st_ref[0] = s_prev
        y, s_new, t_inv = _dn_chunk(
            _dn_heads(q_ref, nh), _dn_heads(k_ref, nh), _dn_heads(v_ref, nh), _dn_heads(g_ref, nh),
            a_col, b_col, al_ref[...], dt_ref[...], gn_ref[...], s_prev,
            k=_dn_consts(), inv=_tri_inv_impl, with_inv=True)
        for h in range(nh):
            y_ref[:, h * hd:(h + 1) * hd] = y[h].astype(y_ref.dtype)
        ti_ref[0] = t_inv
        s_scr[...] = s_new

    return pl.pallas_call(
        body, name=name, grid=(n_chunks,),
        in_specs=[qs, ks, vs, gs, ab, scal, scal, gnspec],
        out_specs=[pl.BlockSpec((c, nh * hd), lambda n: (n, 0)), st,
                   pl.BlockSpec((1, nh, c, c), lambda n: (n, 0, 0, 0))],
        out_shape=[jax.ShapeDtypeStruct((t, nh * hd), CD), jax.ShapeDtypeStruct((n_chunks, nh, hd, hd), F32),
                   jax.ShapeDtypeStruct((n_chunks, nh, c, c), F32)],
        scratch_shapes=[pltpu.VMEM((nh, hd, hd), F32)],
        compiler_params=_params(("arbitrary",)),
    )(cv, cv, cv, z, z, a_log, dt_bias, gn)


def dn_bwd(cv, z, col_gate, col_ab, a_log, dt_bias, gn, states, t_invs, dy, name):
    t = cv.shape[0]
    nh = a_log.shape[0]
    hd, c = DN_HEAD_DIM, DN_CHUNK
    w = nh * hd
    n_chunks = t // c
    rev = lambda n: n_chunks - 1 - n
    qs, ks, vs, gs, ab, scal, gnspec, st = _dn_specs(nh, col_gate, col_ab, rev)
    yspec = pl.BlockSpec((c, w), lambda n: (rev(n), 0))

    def body(q_ref, k_ref, v_ref, g_ref, ab_ref, al_ref, dt_ref, gn_ref, st_ref, ti_ref, dy_ref,
             dc_ref, dg_ref, dab_ref, dal_ref, ddt_ref, dgn_ref, ds_scr):
        @pl.when(pl.program_id(0) == 0)
        def _():
            ds_scr[...] = jnp.zeros_like(ds_scr)
            dal_ref[...] = jnp.zeros_like(dal_ref)
            ddt_ref[...] = jnp.zeros_like(ddt_ref)
            dgn_ref[...] = jnp.zeros_like(dgn_ref)

        abv = ab_ref[...]
        lane = lax.broadcasted_iota(jnp.int32, (c, LANE), 1)
        dab = jnp.zeros((c, LANE), F32)
        consts = _dn_consts()
        half = max(nh // 2, 1)
        for h0 in range(0, nh, half):
            hs = range(h0, h0 + half)
            grp = slice(h0, h0 + half)

            def heads(ref):
                return jnp.stack([ref[:, h * hd:(h + 1) * hd] for h in hs])

            a_col = jnp.stack([_lane_pick(abv, h) for h in hs])
            b_col = jnp.stack([_lane_pick(abv, nh + h) for h in hs])
            t_inv = ti_ref[0, grp]
            fn = functools.partial(_dn_chunk, k=consts, inv=lambda a_mat, t_inv=t_inv: _tri_inv_saved(a_mat, t_inv))
            _, vjp = jax.vjp(fn, heads(q_ref), heads(k_ref), heads(v_ref), heads(g_ref),
                             a_col, b_col, al_ref[grp], dt_ref[grp], gn_ref[...], st_ref[0, grp])
            dq, dk, dv, dg, da, db, dal, ddt, dgn, ds = vjp((heads(dy_ref), ds_scr[grp]))
            for i, h in enumerate(hs):
                cols = slice(h * hd, (h + 1) * hd)
                dc_ref[:, cols] = dq[i]
                dc_ref[:, w + h * hd:w + (h + 1) * hd] = dk[i]
                dc_ref[:, 2 * w + h * hd:2 * w + (h + 1) * hd] = dv[i]
                dg_ref[:, cols] = dg[i].astype(dg_ref.dtype)
                dab = dab + jnp.where(lane == h, da[i], 0.0) + jnp.where(lane == nh + h, db[i], 0.0)
            dal_ref[grp] += dal
            ddt_ref[grp] += ddt
            dgn_ref[...] += dgn
            ds_scr[grp] = ds
        dab_ref[...] = dab.astype(dab_ref.dtype)

    sshape = jax.ShapeDtypeStruct((nh, 1, 1), F32)
    res = pl.pallas_call(
        body, name=name, grid=(n_chunks,),
        in_specs=[qs, ks, vs, gs, ab, scal, scal, gnspec, st,
                  pl.BlockSpec((1, nh, c, c), lambda n: (rev(n), 0, 0, 0)), yspec],
        out_specs=[pl.BlockSpec((c, 3 * w), lambda n: (rev(n), 0)), yspec,
                   pl.BlockSpec((c, LANE), lambda n: (rev(n), 0)), scal, scal, gnspec],
        out_shape=[jax.ShapeDtypeStruct((t, 3 * w), F32), jax.ShapeDtypeStruct((t, w), CD),
                   jax.ShapeDtypeStruct((t, LANE), CD), sshape, sshape, jax.ShapeDtypeStruct((1, hd), F32)],
        scratch_shapes=[pltpu.VMEM((nh, hd, hd), F32)],
        compiler_params=_params(("arbitrary",)),
    )(cv, cv, cv, z, z, a_log, dt_bias, gn, states, t_invs, dy)
    return res


def merge_fwd(ya, wpa, yd, wpd, z, col_m, name):
    t, d = yd.shape[0], wpd.shape[1]
    tm, tn = _tile(t, 512, LANE), _tile(d, 256, LANE)
    nb1, nb2 = col_m // tn, (col_m + d) // tn

    def body(ya_ref, wpa_ref, yd_ref, wpd_ref, z1_ref, z2_ref, m_ref, pa_ref, pd_ref):
        pa = _nn(ya_ref[...], wpa_ref[...])
        pd = _nn(yd_ref[...], wpd_ref[...])
        pa_ref[...] = pa
        pd_ref[...] = pd
        m_ref[...] = (jax.nn.sigmoid(z1_ref[...]) * pa + jax.nn.sigmoid(z2_ref[...]) * pd).astype(m_ref.dtype)

    o = pl.BlockSpec((tm, tn), lambda i, j: (i, j))
    return pl.pallas_call(
        body, name=name, grid=(t // tm, d // tn),
        in_specs=[pl.BlockSpec((tm, ya.shape[1]), lambda i, j: (i, 0)),
                  pl.BlockSpec((wpa.shape[0], tn), lambda i, j: (0, j)),
                  pl.BlockSpec((tm, yd.shape[1]), lambda i, j: (i, 0)),
                  pl.BlockSpec((wpd.shape[0], tn), lambda i, j: (0, j)),
                  pl.BlockSpec((tm, tn), lambda i, j: (i, nb1 + j)),
                  pl.BlockSpec((tm, tn), lambda i, j: (i, nb2 + j))],
        out_specs=[o, o, o],
        out_shape=[jax.ShapeDtypeStruct((t, d), CD), jax.ShapeDtypeStruct((t, d), F32),
                   jax.ShapeDtypeStruct((t, d), F32)],
        compiler_params=_params(("parallel", "parallel")),
    )(ya, wpa, yd, wpd, z, z)


def merge_bwd(dm, pa, pd, z, col_m, name):
    t, d = dm.shape
    tm, tn = _tile(t, 512, 8), _tile(d, 256, LANE)
    nb1, nb2 = col_m // tn, (col_m + d) // tn

    def body(dm_ref, pa_ref, pd_ref, z1_ref, z2_ref, dpa_ref, dpd_ref, dz1_ref, dz2_ref):
        dmv = dm_ref[...]
        s1, s2 = jax.nn.sigmoid(z1_ref[...]), jax.nn.sigmoid(z2_ref[...])
        dpa_ref[...] = (dmv * s1).astype(dpa_ref.dtype)
        dpd_ref[...] = (dmv * s2).astype(dpd_ref.dtype)
        dz1_ref[...] = (dmv * pa_ref[...] * s1 * (1.0 - s1)).astype(dz1_ref.dtype)
        dz2_ref[...] = (dmv * pd_ref[...] * s2 * (1.0 - s2)).astype(dz2_ref.dtype)

    o = pl.BlockSpec((tm, tn), lambda i, j: (i, j))
    sh = jax.ShapeDtypeStruct((t, d), CD)
    return pl.pallas_call(
        body, name=name, grid=(t // tm, d // tn),
        in_specs=[o, o, o, pl.BlockSpec((tm, tn), lambda i, j: (i, nb1 + j)),
                  pl.BlockSpec((tm, tn), lambda i, j: (i, nb2 + j))],
        out_specs=[o, o, o, o], out_shape=[sh, sh, sh, sh],
        compiler_params=_params(("parallel", "parallel")),
    )(dm, pa, pd, z, z)


def ada_fwd(c_all, w, b, name):
    nl, d, n = w.shape
    tn = _tile(n, 384, LANE)

    def body(c_ref, w_ref, b_ref, o_ref):
        o_ref[0] = _nn(_silu(c_ref[...]), w_ref[0]) + b_ref[0]

    return pl.pallas_call(
        body, name=name, grid=(nl, n // tn),
        in_specs=[pl.BlockSpec(c_all.shape, lambda l, j: (0, 0)), pl.BlockSpec((1, d, tn), lambda l, j: (l, 0, j)),
                  pl.BlockSpec((1, 1, tn), lambda l, j: (l, 0, j))],
        out_specs=pl.BlockSpec((1, c_all.shape[0], tn), lambda l, j: (l, 0, j)),
        out_shape=jax.ShapeDtypeStruct((nl, c_all.shape[0], n), F32),
        compiler_params=_params(("parallel", "parallel")),
    )(c_all, w, b)


def ada_bwd(c_pad, dmod_pad, name):
    nl, kp, n = dmod_pad.shape
    d = c_pad.shape[1]
    tn = _tile(n, 384, LANE)

    def body(c_ref, g_ref, o_ref):
        o_ref[0] = _tn(_silu(c_ref[...]), g_ref[0])

    return pl.pallas_call(
        body, name=name, grid=(nl, n // tn),
        in_specs=[pl.BlockSpec((kp, d), lambda l, j: (0, 0)), pl.BlockSpec((1, kp, tn), lambda l, j: (l, 0, j))],
        out_specs=pl.BlockSpec((1, d, tn), lambda l, j: (l, 0, j)),
        out_shape=jax.ShapeDtypeStruct((nl, d, n), F32),
        compiler_params=_params(("parallel", "parallel")),
    )(c_pad, dmod_pad)


def loss_head(y, target, name):
    t, d = y.shape
    tr = _tile(t, 256, 8)

    def body(y_ref, t_ref, dy_ref, l_ref):
        err = y_ref[...] - t_ref[...]
        dy_ref[...] = err * (1.0 / d)
        part = jnp.sum(jnp.sum(err * err, axis=1, keepdims=True), axis=0, keepdims=True) * (0.5 / d)

        @pl.when(pl.program_id(0) == 0)
        def _():
            l_ref[...] = jnp.zeros_like(l_ref)

        l_ref[...] += part

    row = pl.BlockSpec((tr, d), lambda i: (i, 0))
    return pl.pallas_call(
        body, name=name, grid=(t // tr,), in_specs=[row, row],
        out_specs=[row, pl.BlockSpec((8, LANE), lambda i: (0, 0))],
        out_shape=[jax.ShapeDtypeStruct((t, d), F32), jax.ShapeDtypeStruct((8, LANE), F32)],
        compiler_params=_params(("arbitrary",)),
    )(y, target)


def _adamw_update(g, w_ref, m_ref, v_ref, go_ref, d_ref, mo_ref, vo_ref):
    m_new = ADAM_B1 * m_ref[...] + (1.0 - ADAM_B1) * g
    v_new = ADAM_B2 * v_ref[...] + (1.0 - ADAM_B2) * jnp.square(g)
    m_hat = m_new / (1.0 - ADAM_B1 ** ADAM_STEP)
    v_hat = v_new / (1.0 - ADAM_B2 ** ADAM_STEP)
    go_ref[...] = g
    d_ref[...] = -ADAM_LR * (m_hat / (jnp.sqrt(v_hat) + ADAM_EPS) + ADAM_WD * w_ref[...])
    mo_ref[...] = m_new
    vo_ref[...] = v_new


def adamw(w, m, v, g_slots, name):
    shape = w.shape
    nslot = g_slots.shape[0]
    if w.ndim == 2:
        w3, m3, v3, g4 = w[None], m[None], v[None], g_slots[:, None]
    else:
        w3, m3, v3, g4 = w, m, v, g_slots
    nl, r, c = w3.shape
    tr = _tile(r, 256, 16)

    def body(w_ref, m_ref, v_ref, g_ref, *outs):
        g = g_ref[0].astype(F32)
        for s in range(1, nslot):
            g = g + g_ref[s].astype(F32)
        _adamw_update(g, w_ref, m_ref, v_ref, *outs)

    blk = pl.BlockSpec((1, tr, c), lambda l, i: (l, i, 0))
    gblk = pl.BlockSpec((nslot, 1, tr, c), lambda l, i: (0, l, i, 0))
    sh = jax.ShapeDtypeStruct(w3.shape, F32)
    outs = pl.pallas_call(
        body, name=name, grid=(nl, r // tr), in_specs=[blk, blk, blk, gblk], out_specs=[blk] * 4,
        out_shape=[sh] * 4, compiler_params=_params(("parallel", "parallel")),
    )(w3, m3, v3, g4)
    return tuple(o.reshape(shape) for o in outs)


def adamw_layers(w, m, v, g_layers, first, name, prev=None):
    _, r, c = w.shape
    n = len(g_layers)
    nslot, _, cg = g_layers[0].shape
    tr = _tile(r, 256, 16)

    def body(w_ref, m_ref, v_ref, *rest):
        g_refs, outs = rest[:n], rest[-4:]
        for li in range(n):
            @pl.when(pl.program_id(0) == li)
            def _(g_ref=g_refs[li]):
                g = g_ref[0, :, :c].astype(F32)
                for s in range(1, nslot):
                    g = g + g_ref[s, :, :c].astype(F32)
                _adamw_update(g[None], w_ref, m_ref, v_ref, *outs)

    blk = pl.BlockSpec((1, tr, c), lambda l, i: (first + l, i, 0))
    gblks = [pl.BlockSpec((nslot, tr, cg), lambda l, i, li=li: (0, jnp.where(l == li, i, 0), 0)) for li in range(n)]
    sh = jax.ShapeDtypeStruct(w.shape, F32)
    extra = [] if prev is None else list(prev)
    return tuple(pl.pallas_call(
        body, name=name, grid=(n, r // tr),
        in_specs=[blk, blk, blk] + gblks + [pl.BlockSpec(memory_space=pl.ANY)] * len(extra),
        out_specs=[blk] * 4, out_shape=[sh] * 4,
        input_output_aliases={3 + n + k: k for k in range(len(extra))},
        compiler_params=_params(("arbitrary", "arbitrary")),
    )(w, m, v, *g_layers, *extra))


def _rows(flat, unit=16):
    n = flat.shape[0]
    per = 1024 * unit
    pad = (-n) % per
    if pad:
        flat = jnp.concatenate([flat, jnp.zeros((pad,), flat.dtype)])
    return flat.reshape(-1, 1024)


def _class_rows(x, dil):
    if dil == 1:
        return x
    t, w = x.shape
    return x.reshape(t // dil, dil, w).transpose(1, 0, 2).reshape(t, w)


def _token_rows(x, dil):
    if dil == 1:
        return x
    t, w = x.shape
    return x.reshape(dil, t // dil, w).transpose(1, 0, 2).reshape(t, w)


def _to_classes(a):
    return jnp.stack([_class_rows(a[gi], dil) for gi, dil in enumerate(DILATIONS)])


def _from_classes(a):
    return jnp.stack([_token_rows(a[gi], dil) for gi, dil in enumerate(DILATIONS)])


def _unshard_cols(g):
    return g.transpose(1, 0, 2).reshape(g.shape[1], -1)


def _shard_cols(full):
    r = full.shape[0]
    return full.reshape(r, N_DEV, -1).transpose(1, 0, 2)


def kernel(x, c, ada_w, ada_b, norm_ff1, ffn1_w_up, ffn1_w_down, norm_mix, w_in, q_norm, k_norm, conv_w, a_log, dt_bias, dn_norm, w_proj_att, w_proj_dn, w_out, norm_ff2, ffn2_w_up, ffn2_w_down, loss_target, m_ada_w, m_ada_b, m_norm_ff1, m_ffn1_w_up, m_ffn1_w_down, m_norm_mix, m_w_in, m_q_norm, m_k_norm, m_conv_w, m_a_log, m_dt_bias, m_dn_norm, m_w_proj_att, m_w_proj_dn, m_w_out, m_norm_ff2, m_ffn2_w_up, m_ffn2_w_down, v_ada_w, v_ada_b, v_norm_ff1, v_ffn1_w_up, v_ffn1_w_down, v_norm_mix, v_w_in, v_q_norm, v_k_norm, v_conv_w, v_a_log, v_dt_bias, v_dn_norm, v_w_proj_att, v_w_proj_dn, v_w_out, v_norm_ff2, v_ffn2_w_up, v_ffn2_w_down):
    nl = ada_w.shape[0]
    t, d = x.shape[1], x.shape[2]
    dff = ffn1_w_down.shape[1] * N_DEV
    ha = d // 256
    ng = len(DILATIONS)
    wa = ha * ATT_HEAD_DIM
    att_w = ng * wa
    nh = d // DN_HEAD_DIM
    dn_w = nh * DN_HEAD_DIM
    n_in = w_in.shape[2] * N_DEV
    off_dn, off_gate = 3 * att_w, 3 * att_w + 3 * dn_w
    off_a = off_gate + dn_w
    off_merge = off_a + 2 * nh
    assert off_merge + 2 * d == n_in
    col_dn, col_gate, col_m = 0, 3 * dn_w, 4 * dn_w
    col_att = col_m + 2 * d
    col_ab = col_att + 3 * att_w
    zw = col_ab + 2 * LANE
    me = 4 * lax.axis_index("x") + 2 * lax.axis_index("y") + lax.axis_index("c")
    xs = x[0]
    target = loss_target[0]

    conv_rows = _rows(conv_w.reshape(-1), 8)
    pack0 = jnp.concatenate([jnp.concatenate([c, jnp.zeros((7, d), F32)]).reshape(-1), conv_rows.reshape(-1)])
    pack0 = _rows(pack0, 8)
    g0 = all_gather([pack0], "ag_c_conv")[0].reshape(N_DEV, -1)
    c_all = g0[:, :d]
    cw = g0[:, 8 * d:8 * d + conv_w.size].reshape(N_DEV, nl, CONV_WIDTH, -1)
    conv_full = cw.transpose(1, 2, 0, 3).reshape(nl, CONV_WIDTH, 3 * dn_w)

    n_ada = ada_w.shape[2]
    b_mine = lax.dynamic_slice_in_dim(ada_b, me * n_ada, n_ada, axis=1)[:, None, :]
    mod_s = ada_fwd(c_all, ada_w, b_mine, "ada_fwd")
    gm = all_gather([mod_s], "ag_mod")[0]
    mod = lax.dynamic_index_in_dim(gm, me, axis=2, keepdims=False)
    mod = mod.transpose(1, 0, 2).reshape(nl, N_ADA, 1, d)

    kinds = [ffn1_w_up, ffn1_w_down, w_in, w_proj_att, w_proj_dn, w_out, ffn2_w_up, ffn2_w_down]
    c_up = ffn1_w_up.shape[2]
    cp = -(-c_up // LANE) * LANE
    r_dn = ffn1_w_down.shape[1]
    assert 2 * r_dn == c_up

    up1_t, up2_t = jnp.swapaxes(ffn1_w_up, 1, 2), jnp.swapaxes(ffn2_w_up, 1, 2)
    w_in_t = jnp.transpose(w_in, (2, 0, 1))

    def layer_shards(l):
        def pad_up(wt):
            return jnp.pad(wt.astype(CD), ((0, cp - c_up), (0, 0)))
        return [pad_up(up1_t[l]), ffn1_w_down[l].astype(CD), w_in_t[:, l].astype(CD), w_proj_att[l].astype(CD),
                w_proj_dn[l].astype(CD), w_out[l].astype(CD), pad_up(up2_t[l]), ffn2_w_down[l].astype(CD)]

    def gather_landing(shards):
        return [lax.dynamic_update_index_in_dim(lax.empty((N_DEV,) + s.shape, s.dtype), s, me, 0) for s in shards]

    def slot_landing(arrs):
        return [lax.dynamic_update_index_in_dim(lax.empty(a.shape, a.dtype),
                                                lax.dynamic_index_in_dim(a, me, 0, keepdims=False), me, 0) for a in arrs]

    gathered = all_gather(layer_shards(0), "ag_weights0")
    prefetch = None

    def full_weight(ki, l):
        blk = gathered[ki]
        if ki == 3:
            return _unshard_cols(blk)
        if ki == 2:
            return w_in_rows(blk.reshape(n_in, d))
        if ki in (1, 7):
            pairs = blk.reshape(HALF, c_up, d)
            return jnp.pad(pairs, ((0, 0), (0, cp - c_up), (0, 0))).reshape(HALF * cp, d)
        return blk.reshape(-1, blk.shape[2])

    def down_grad_slots(g):
        return g.reshape(HALF, cp, d)[:, :c_up].reshape(N_DEV, r_dn, d)

    def w_in_rows(wt):
        pad = jnp.zeros((zw - n_in, wt.shape[1]), wt.dtype)
        return jnp.concatenate([wt[off_dn:off_a], wt[off_merge:], wt[:off_dn], wt[off_a:off_merge], pad], axis=0)

    def w_in_cols_inv(g):
        return jnp.concatenate([g[:, col_att:col_ab], g[:, :col_m], g[:, col_ab:col_ab + 2 * nh], g[:, col_m:col_att]], axis=1)

    saved = []
    xc = xs
    mods = []
    for l in range(nl):
        sv = {}
        if prefetch is not None:
            gathered = exchange_wait(*prefetch[:4], xc, True, f"ag_wait{l}")
        mod_l = mod[l]
        if l + 1 < nl:
            shards = layer_shards(l + 1)
            behind = gathered[0][0, :1, :1].astype(F32) + mod[0, 0, :, :1]
            prefetch = exchange_start(shards, gather_landing(shards), behind, True, f"ag_start{l + 1}")
            mod_l = mod_l + prefetch[4][0, 0]
        mods.append(mod_l)
        sh1, sc1, gt1, sh2, sc2, gt2, sh3, sc3, gt3 = [mod_l[i] for i in range(N_ADA)]
        w_dn1, w_dn2 = full_weight(1, l), full_weight(7, l)
        win = full_weight(2, l)
        wpa, wpd, wo = full_weight(3, l), full_weight(4, l), full_weight(5, l)
        sv["w"] = (gathered[0], gathered[6], w_dn1, w_dn2, win, wpa, wpd, wo)

        def ffn(xin, g, sh, sc, gt, u_all, w_dn):
            h = norm_mod(xin, g, sc, sh, "norm_mod")
            gate, up, a = ffn_up(h, u_all, "ffn_up")
            xo, f = matmul([(a, w_dn)], "nn", F32, "ffn_down", out_scale=0.5 * gt, resid=xin, save_acc=True,
                           tm=1024, tk=4096)
            return xo, (xin, h, gate, up, a, f)

        xc, sv["ffn1"] = ffn(xc, norm_ff1[l:l + 1], sh1, sc1, gt1, gathered[0], w_dn1)

        x_mix = xc
        h2 = norm_mod(x_mix, norm_mix[l:l + 1], sc2, sh2, "norm_mod")
        z = matmul([(h2, win)], "nt", F32, "w_in", tm=1024, tn=512)
        z_att = z[:, col_att:col_ab].reshape(t, 3, ng, wa)
        qkv = jnp.stack([jnp.stack([_class_rows(z_att[:, i, gi], dil) for gi, dil in enumerate(DILATIONS)])
                         for i in range(3)])
        qg, kg = jnp.tile(q_norm[l:l + 1], (1, ha)), jnp.tile(k_norm[l:l + 1], (1, ha))
        o_cls, lse_cls = att_fwd(qkv, qg, kg, "att_fwd")
        o_tok, lse_tok = _from_classes(o_cls), _from_classes(lse_cls)
        ya = att_combine(o_tok, lse_tok, "att_combine")
        cvo = conv_fwd(z, col_dn, conv_full[l], "conv_fwd")
        al3, dt3 = a_log[l].reshape(nh, 1, 1), dt_bias[l].reshape(nh, 1, 1)
        gn = dn_norm[l:l + 1]
        yd, *states = dn_fwd(cvo, z, col_gate, col_ab, al3, dt3, gn, "dn_fwd")
        mrg, pa, pd = merge_fwd(ya, wpa, yd, wpd, z, col_m, "merge_fwd")
        xc, f2 = matmul([(mrg, wo)], "nn", F32, "w_out", out_scale=gt2, resid=x_mix, save_acc=True)
        sv["mix"] = (x_mix, h2, z, qkv, o_tok, lse_tok, ya, cvo, yd, states, mrg, pa, pd, f2)

        xc, sv["ffn2"] = ffn(xc, norm_ff2[l:l + 1], sh3, sc3, gt3, gathered[6], w_dn2)
        saved.append(sv)

    dxc, loss_blk = loss_head(xc, target, "loss_head")

    def slots(ki, g):
        if ki in (0, 1, 6, 7):
            return g
        return _shard_cols(g) if ki in (2, 3) else g.reshape(N_DEV, -1, g.shape[1])

    gbig = [None] * len(kinds)
    dmods, small, pending = [], [], []
    token = None
    for l in reversed(range(nl)):
        sv = saved[l]
        mod_l = mods[l] if token is None else mods[l] + token[0, 0]
        sh1, sc1, gt1, sh2, sc2, gt2, sh3, sc3, gt3 = [mod_l[i] for i in range(N_ADA)]
        u_all1, u_all2, w_dn1, w_dn2, win, wpa, wpd, wo = sv["w"]

        def ffn_bwd(dxo, g, sh, sc, gt, u_all, w_dn, sv_f):
            xin, h, gate, up, a, f = sv_f
            s = 0.5 * gt
            g_dn = down_grad_slots(matmul([(a, dxo)], "tn", CD, "ffn_down_wg", out_scale=s, tm=1024, tn=1024))
            dgate, dup = ffn_dact(dxo, s, w_dn, gate, up, "ffn_dact")
            g_up = ffn_up_wg(h, dgate, dup, cp, "ffn_up_wg")
            dh = ffn_up_dg(dgate, dup, u_all, "ffn_up_dg")
            dx, dg, dsc, dsh, dgt = norm_mod_bwd(xin, g, sc, sh, dh, dxo, f, 0.5, "norm_mod_bwd")
            return dx, g_up, g_dn, (dg, dsc, dsh, dgt)

        def start_exchange(ids, tag, behind):
            send = [slots(ki, gbig[ki]) for ki in ids]
            started = exchange_start(send, slot_landing(send), behind, False, f"a2a_start{l}{tag}")
            pending.append((l, ids, tag, started))
            return started[4]

        dxc, gbig[6], gbig[7], (dg3, dsc3, dsh3, dgt3) = ffn_bwd(
            dxc, norm_ff2[l:l + 1], sh3, sc3, gt3, u_all2, w_dn2, sv["ffn2"])
        token = start_exchange([6, 7], "f2", dxc)
        sh2, sc2, gt2 = (vec + token[0, 0] for vec in (sh2, sc2, gt2))

        x_mix, h2, z, qkv, o_tok, lse_tok, ya, cvo, yd, states, mrg, pa, pd, f2 = sv["mix"]
        gbig[5] = matmul([(mrg, dxc)], "tn", CD, "w_out_wg", out_scale=gt2)
        dm = matmul([(dxc, wo)], "nt", F32, "w_out_dg", a_scale=gt2)
        dpa, dpd, dz1, dz2 = merge_bwd(dm, pa, pd, z, col_m, "merge_bwd")
        gbig[3] = matmul([(ya, dpa)], "tn", CD, "patt_wg")
        gbig[4] = matmul([(yd, dpd)], "tn", CD, "pdn_wg")
        dya = matmul([(dpa, wpa)], "nt", F32, "patt_dg")
        dyd = matmul([(dpd, wpd)], "nt", F32, "pdn_dg")
        do_tok, dlse_tok = att_combine_bwd(o_tok, lse_tok, dya, "att_combine_bwd")
        qg, kg = jnp.tile(q_norm[l:l + 1], (1, ha)), jnp.tile(k_norm[l:l + 1], (1, ha))
        dq, dk, dv, dqg, dkg = att_bwd(qkv, qg, kg, _to_classes(do_tok), _to_classes(dlse_tok), "att_bwd")
        dqg, dkg = (jnp.sum(v_.reshape(ha, ATT_HEAD_DIM), axis=0, keepdims=True) for v_ in (dqg, dkg))
        dz_att = jnp.concatenate([_token_rows(a_[gi], dil).astype(CD) for a_ in (dq, dk, dv)
                                  for gi, dil in enumerate(DILATIONS)], axis=1)
        al3, dt3 = a_log[l].reshape(nh, 1, 1), dt_bias[l].reshape(nh, 1, 1)
        gn = dn_norm[l:l + 1]
        dcvo, dz_gate, dz_ab, dal, ddt, dgn = dn_bwd(
            cvo, z, col_gate, col_ab, al3, dt3, gn, *states, dyd, "dn_bwd")
        dz_dn, dconv = conv_bwd(dcvo, z, col_dn, conv_full[l], "conv_bwd")
        dz = jnp.concatenate([dz_dn, dz_gate, dz1, dz2, dz_att, dz_ab, jnp.zeros((t, LANE), CD)], axis=1)
        gbig[2] = w_in_cols_inv(matmul([(h2, dz)], "tn", CD, "w_in_wg", tm=1024, tn=zw // 4))
        dh2 = matmul([(dz, win)], "nn", F32, "w_in_dg", tm=1024, tn=1024,
                     tk=zw // 4 if (zw // 4) % LANE == 0 else zw)
        dxc, dg2, dsc2, dsh2, dgt2 = norm_mod_bwd(x_mix, norm_mix[l:l + 1], sc2, sh2, dh2, dxc, f2, 1.0, "norm_mod_bwd_mix")

        token = start_exchange([2, 3, 4, 5], "mx", dxc)
        sh1, sc1, gt1 = (vec + token[0, 0] for vec in (sh1, sc1, gt1))
        dxc, gbig[0], gbig[1], (dg1, dsc1, dsh1, dgt1) = ffn_bwd(
            dxc, norm_ff1[l:l + 1], sh1, sc1, gt1, u_all1, w_dn1, sv["ffn1"])
        if l > 0:
            token = start_exchange([0, 1], "f1", dxc)
        dmods.append(jnp.concatenate([dsh1, dsc1, dgt1, dsh2, dsc2, dgt2, dsh3, dsc3, dgt3], axis=1))
        small.append((dg1, dg2, dg3, dconv, dqg, dkg, dal.reshape(1, nh), ddt.reshape(1, nh), dgn))
    dmods.reverse()
    small.reverse()

    big_names = ["ffn1_w_up", "ffn1_w_down", "w_in", "w_proj_att", "w_proj_dn", "w_out", "ffn2_w_up", "ffn2_w_down"]
    big_m = [m_ffn1_w_up, m_ffn1_w_down, m_w_in, m_w_proj_att, m_w_proj_dn, m_w_out, m_ffn2_w_up, m_ffn2_w_down]
    big_v = [v_ffn1_w_up, v_ffn1_w_down, v_w_in, v_w_proj_att, v_w_proj_dn, v_w_out, v_ffn2_w_up, v_ffn2_w_down]
    recv_layers = [[None] * len(kinds) for _ in range(nl)]

    def wait_exchanges(layers, behind):
        for pl_, ids, tag, st in pending:
            if pl_ in layers:
                got = exchange_wait(*st[:4], behind, False, f"a2a_wait{pl_}{tag}")
                for ki, arr in zip(ids, got):
                    recv_layers[pl_][ki] = arr

    def big_adamw(ki, layers, prev, tag):
        g_layers = [recv_layers[li][ki] for li in layers]
        wmv = (kinds[ki], big_m[ki], big_v[ki])
        if ki in (0, 6):
            wmv = tuple(jnp.swapaxes(a, 1, 2) for a in wmv)
        return adamw_layers(*wmv, g_layers, layers[0], f"adamw_{big_names[ki]}{tag}", prev)

    later = list(range(1, nl))
    partial = [None] * len(kinds)
    if later:
        wait_exchanges(later, dxc)
        partial = [big_adamw(ki, later, None, "_l1") for ki in range(len(kinds))]

    fields = [jnp.stack(dmods).reshape(-1)]
    fields += [jnp.stack([s[i] for s in small]).reshape(-1) for i in range(9)]
    fields.append(loss_blk[0, :1])
    if later:
        fields.append(partial[-1][1][1, :1, 0] * 0.0)
    fsizes = [f.size for f in fields]
    foffs = [sum(fsizes[:i]) for i in range(len(fields))]
    g1 = all_gather([_rows(jnp.concatenate(fields), 8)], "ag_small")[0].reshape(N_DEV, -1)
    l = 0
    start_exchange([0, 1], "f1", g1)

    def field(i, shape):
        return g1[:, foffs[i]:foffs[i] + fsizes[i]].reshape(N_DEV, *shape)

    loss = field(10, (1,))[0, 0]
    for j in range(1, N_DEV):
        loss = loss + field(10, (1,))[j, 0]

    results = {}
    dmod_all = field(0, (nl, N_ADA * d))
    c_pad = jnp.concatenate([c_all, jnp.zeros((8, d), F32)])
    dmod_mine = lax.dynamic_slice_in_dim(dmod_all, me * n_ada, n_ada, axis=2).transpose(1, 0, 2)
    dmod_pad = jnp.concatenate([dmod_mine, jnp.zeros((nl, 8, n_ada), F32)], axis=1)
    g_ada_w = ada_bwd(c_pad, dmod_pad, "ada_bwd")
    results["ada_w"] = adamw(ada_w, m_ada_w, v_ada_w, g_ada_w[None], "adamw_ada_w")
    results["ada_b"] = adamw(ada_b, m_ada_b, v_ada_b, dmod_all, "adamw_ada_b")
    results["norm_ff1"] = adamw(norm_ff1, m_norm_ff1, v_norm_ff1, field(1, (nl, d)), "adamw_norm_ff1")
    results["norm_mix"] = adamw(norm_mix, m_norm_mix, v_norm_mix, field(2, (nl, d)), "adamw_norm_mix")
    results["norm_ff2"] = adamw(norm_ff2, m_norm_ff2, v_norm_ff2, field(3, (nl, d)), "adamw_norm_ff2")
    conv_slots = lax.dynamic_slice_in_dim(field(4, (nl, CONV_WIDTH, 3 * dn_w)), me * conv_w.shape[2],
                                          conv_w.shape[2], axis=3)
    results["conv_w"] = adamw(conv_w, m_conv_w, v_conv_w, conv_slots, "adamw_conv_w")
    results["q_norm"] = adamw(q_norm, m_q_norm, v_q_norm, field(5, (nl, ATT_HEAD_DIM)), "adamw_q_norm")
    results["k_norm"] = adamw(k_norm, m_k_norm, v_k_norm, field(6, (nl, ATT_HEAD_DIM)), "adamw_k_norm")
    results["a_log"] = adamw(a_log, m_a_log, v_a_log, field(7, (nl, nh)), "adamw_a_log")
    results["dt_bias"] = adamw(dt_bias, m_dt_bias, v_dt_bias, field(8, (nl, nh)), "adamw_dt_bias")
    results["dn_norm"] = adamw(dn_norm, m_dn_norm, v_dn_norm, field(9, (nl, DN_HEAD_DIM)), "adamw_dn_norm")
    wait_exchanges([0], results["ada_w"][1])
    for ki, nm in enumerate(big_names):
        res = big_adamw(ki, [0], partial[ki], "_l0")
        results[nm] = tuple(jnp.swapaxes(r, 1, 2) for r in res) if ki in (0, 6) else res

    order = ["ada_w", "ada_b", "norm_ff1", "ffn1_w_up", "ffn1_w_down", "norm_mix", "w_in", "q_norm", "k_norm",
             "conv_w", "a_log", "dt_bias", "dn_norm", "w_proj_att", "w_proj_dn", "w_out", "norm_ff2",
             "ffn2_w_up", "ffn2_w_down"]
    outs = [loss, dxc[None]]
    for part in range(4):
        outs += [results[n][part] for n in order]
    return tuple(outs)
```
